```python
import math
import jax, jax.numpy as jnp
from jax import lax
import numpy as np

D_MODEL = 4096
BATCH = 4
SEQ = 2048
DEPTH = 4
DEC_BATCH = 8
DEC_SEQ = 8
PAST_LEN = 8192
PAGE_SIZE = 128

N_EVEN = (DEPTH + 1) // 2
N_ODD = DEPTH // 2
D_A = D_MODEL // 2
S5_GRP = 16
G_A = D_A // S5_GRP
S5_P = 64
D_B = D_MODEL // 2
POOL_WINDOWS = (2, 4, 8, 16)
G_B = len(POOL_WINDOWS)
C_B = D_B // G_B
POOL_BUF = max(POOL_WINDOWS) - 1
HEAD_DIM = 128
N_HEADS_C = (D_MODEL // 2) // HEAD_DIM
D_C = N_HEADS_C * HEAD_DIM
BRANCHES = ((128, 1), (512, 4), (2048, 16))
MAX_WINDOW = 2048
ATT_BLOCK = 128
ATT_SCALE = HEAD_DIM ** -0.5
D_D = D_MODEL // 2
G_D = 8
C_D = D_D // G_D
CHUNK = 128
D_FF = ((8 * D_MODEL + 3 * 256 - 1) // (3 * 256)) * 256
RMS_EPS = 1e-6
LN_EPS = 1e-5
NEG_INF = -1e30

kernel_name = 'hybrid_s5_pool_dilattn_sgu_decode_step'


def _rmsnorm(x, g):
    xf = x.astype(jnp.float32)
    y = xf * lax.rsqrt(jnp.mean(xf * xf, axis=-1, keepdims=True) + RMS_EPS)
    return (y * g.astype(jnp.float32)).astype(x.dtype)


def _layernorm(x, g, b):
    xf = x.astype(jnp.float32)
    mu = jnp.mean(xf, axis=-1, keepdims=True)
    xc = xf - mu
    y = xc * lax.rsqrt(jnp.mean(xc * xc, axis=-1, keepdims=True) + LN_EPS)
    return y * g.astype(jnp.float32) + b.astype(jnp.float32)


def _linrec_combine(e1, e2):
    a1, b1 = e1
    a2, b2 = e2
    return a1 * a2, a2 * b1 + b2


def _s5(u, h0_re, h0_im, lam_re, lam_im, log_dt, b_re, b_im, c_re, c_im, d_skip, w_glu, b_glu):
    f32 = jnp.float32
    bsz, L, _ = u.shape
    uf = u.astype(f32)
    lam = lax.complex(lam_re.astype(f32), lam_im.astype(f32))
    dt = jnp.exp(log_dt.astype(f32))[:, None]
    lam_bar = jnp.exp(lam * dt)
    b_bar = ((lam_bar - 1.0) / lam)[:, :, None] * lax.complex(b_re.astype(f32), b_im.astype(f32))
    bu = jnp.einsum('blgh,gph->blgp', uf.reshape(bsz, L, G_A, S5_GRP).astype(jnp.complex64), b_bar)
    h0 = lax.complex(h0_re.astype(f32), h0_im.astype(f32))
    bu = bu.at[:, 0].add(lam_bar * h0)
    a = jnp.broadcast_to(lam_bar, (1, L, G_A, S5_P))
    _, h = lax.associative_scan(_linrec_combine, (a, bu), axis=1)
    c = lax.complex(c_re.astype(f32), c_im.astype(f32))
    y = jnp.einsum('blgp,ghp->blgh', h, c).real.reshape(bsz, L, D_A) + d_skip.astype(f32) * uf
    y = jax.nn.gelu(y, approximate=False)
    y = y * jax.nn.sigmoid(y @ w_glu.astype(f32) + b_glu.astype(f32))
    h_last = h[:, -1]
    return y.astype(u.dtype), h_last.real.astype(u.dtype), h_last.imag.astype(u.dtype)


def _pool(u, buf, start_pos, w_lin, scale):
    f32 = jnp.float32
    bsz, L, _ = u.shape
    ext = jnp.concatenate([buf.astype(f32), u.astype(f32)], axis=1)
    cs = jnp.concatenate([jnp.zeros((bsz, 1, D_B), f32), jnp.cumsum(ext, axis=1)], axis=1)
    pos = start_pos + jnp.arange(L, dtype=jnp.int32)
    hi = cs[:, POOL_BUF + 1:]
    groups = []
    for g, w in enumerate(POOL_WINDOWS):
        sl = slice(g * C_B, (g + 1) * C_B)
        lo = cs[:, POOL_BUF + 1 - w:POOL_BUF + 1 - w + L, sl]
        cnt = jnp.minimum(pos + 1, w).astype(f32)[None, :, None]
        groups.append((hi[:, :, sl] - lo) / cnt - ext[:, POOL_BUF:, sl])
    z = jnp.stack(groups, axis=2)
    y = jnp.einsum('blgc,gcd->blgd', z, w_lin.astype(f32)).reshape(bsz, L, D_B) * scale.astype(f32)
    return y.astype(u.dtype), ext[:, -POOL_BUF:].astype(u.dtype)


def _branch_prompt(q, k, v, window, dilation):
    bsz, L, H, E = q.shape
    n_prev = window // dilation
    nback = -(-n_prev // ATT_BLOCK)
    span = dilation * ATT_BLOCK
    Lp = -(-L // span) * span
    nb = Lp // span

    def to_sub(t):
        t = jnp.pad(t, ((0, 0), (0, Lp - L), (0, 0), (0, 0))).reshape(bsz, Lp // dilation, dilation, H, E)
        return t.transpose(0, 2, 1, 3, 4).reshape(bsz, dilation, nb, ATT_BLOCK, H, E)

    def band(t):
        tp = jnp.pad(t, ((0, 0), (0, 0), (nback, 0), (0, 0), (0, 0), (0, 0)))
        return jnp.concatenate([tp[:, :, j:j + nb] for j in range(nback + 1)], axis=3)

    qs = to_sub(q)
    kb = band(to_sub(k))
    vb = band(to_sub(v))
    s = jnp.einsum('brnqhe,brnkhe->brnhqk', qs, kb) * ATT_SCALE
    kj = jnp.arange((nback + 1) * ATT_BLOCK)
    dist = nback * ATT_BLOCK + jnp.arange(ATT_BLOCK)[:, None] - kj[None, :]
    kidx = (jnp.arange(nb)[:, None] - nback) * ATT_BLOCK + kj[None, :]
    valid = ((dist >= 0) & (dist <= n_prev))[None] & (kidx >= 0)[:, None, :]
    s = jnp.where(valid[None, None, :, None], s, NEG_INF)
    m = jnp.max(s, axis=-1, keepdims=True)
    p = jnp.exp(s - m)
    l = jnp.sum(p, axis=-1, keepdims=True)
    o = jnp.einsum('brnhqk,brnkhe->brnqhe', p / l, vb)
    o = o.reshape(bsz, dilation, Lp // dilation, H, E).transpose(0, 2, 1, 3, 4).reshape(bsz, Lp, H, E)[:, :L]
    lse = jnp.moveaxis((m + jnp.log(l))[..., 0], 3, 4)
    lse = lse.reshape(bsz, dilation, Lp // dilation, H).transpose(0, 2, 1, 3).reshape(bsz, Lp, H)[:, :L]
    return o, lse


def _branch_sample(q, k_ext, v_ext, n_buf, window, dilation):
    S = q.shape[1]
    n_prev = window // dilation
    idx = n_buf + jnp.arange(S)[:, None] - dilation * jnp.arange(n_prev + 1)[None, :]
    valid = idx >= 0
    idx = jnp.maximum(idx, 0)
    kg = k_ext[:, idx]
    vg = v_ext[:, idx]
    s = jnp.einsum('bshe,bskhe->bhsk', q, kg) * ATT_SCALE
    s = jnp.where(valid[None, None], s, NEG_INF)
    m = jnp.max(s, axis=-1, keepdims=True)
    p = jnp.exp(s - m)
    l = jnp.sum(p, axis=-1, keepdims=True)
    o = jnp.einsum('bhsk,bskhe->bshe', p / l, vg)
    lse = jnp.moveaxis((m + jnp.log(l))[..., 0], 1, 2)
    return o, lse


def _combine_branches(outs, lses):
    wts = jax.nn.softmax(jnp.stack(lses, axis=0), axis=0)
    return jnp.sum(wts[..., None] * jnp.stack(outs, axis=0), axis=0)


def _sgu(gu, gv, ln_g, ln_b, w_s, b_s):
    f32 = jnp.float32
    bsz, L, _ = gu.shape
    T = min(L, CHUNK)
    vn = _layernorm(gv, ln_g, ln_b)
    mask = jnp.tril(jnp.ones((T, T), dtype=bool))
    w = jnp.where(mask[None], w_s[:, :T, :T].astype(f32), 0.0)
    vr = vn.reshape(bsz, L // T, T, G_D, C_D)
    mixed = jnp.einsum('gts,bnsgc->bntgc', w, vr) + b_s[:, :T].astype(f32).T[None, None, :, :, None]
    out = gu.astype(f32) * mixed.reshape(bsz, L, D_D)
    return out.astype(gu.dtype), vn.astype(gu.dtype)


def _even_layer(x, h0_re, h0_im, pool_buf, start_pos, norm_g, w_in, w_out, s5_params, pool_w, pool_scale):
    z = _rmsnorm(x, norm_g) @ w_in
    ya, hr, hi = _s5(z[..., :D_A], h0_re, h0_im, *s5_params)
    yb, buf = _pool(z[..., D_A:], pool_buf, start_pos, pool_w, pool_scale)
    return x + jnp.concatenate([ya, yb], axis=-1) @ w_out, hr, hi, buf


def _odd_layer(x, k_buf, v_buf, norm_g, w_in, w_out, qn, kn, ln_g, ln_b, w_s, b_s):
    f32 = jnp.float32
    bsz, L, _ = x.shape
    z = _rmsnorm(x, norm_g) @ w_in
    q, k, v, gu, gv = jnp.split(z, [D_C, 2 * D_C, 3 * D_C, 3 * D_C + D_D], axis=-1)
    q = _rmsnorm(q.reshape(bsz, L, N_HEADS_C, HEAD_DIM), qn).astype(f32)
    k = _rmsnorm(k.reshape(bsz, L, N_HEADS_C, HEAD_DIM), kn).astype(f32)
    v = v.reshape(bsz, L, N_HEADS_C, HEAD_DIM).astype(f32)
    outs, lses = [], []
    if k_buf is None:
        for w, d in BRANCHES:
            o, lse = _branch_prompt(q, k, v, w, d)
            outs.append(o)
            lses.append(lse)
        n_keep = min(MAX_WINDOW, L)
        new_k, new_v = k[:, L - n_keep:], v[:, L - n_keep:]
    else:
        n_buf = k_buf.shape[1]
        k_ext = jnp.concatenate([k_buf.astype(f32), k], axis=1)
        v_ext = jnp.concatenate([v_buf.astype(f32), v], axis=1)
        for w, d in BRANCHES:
            o, lse = _branch_sample(q, k_ext, v_ext, n_buf, w, d)
            outs.append(o)
            lses.append(lse)
        new_k, new_v = k, v
    att = _combine_branches(outs, lses).reshape(bsz, L, D_C).astype(x.dtype)
    sg, v_rows = _sgu(jax.nn.gelu(gu, approximate=False), jax.nn.gelu(gv, approximate=False), ln_g, ln_b, w_s, b_s)
    y = jnp.concatenate([att, sg], axis=-1) @ w_out
    return x + y, new_k.astype(x.dtype), new_v.astype(x.dtype), v_rows


def _swiglu(x, g, w1, w3, w2):
    h = _rmsnorm(x, g)
    return x + (jax.nn.silu(h @ w1) * (h @ w3)) @ w2


def setup_inputs(seed: int = 0) -> dict:
    key = jax.random.key(seed)
    keys = iter(jax.random.split(key, 48))
    f32 = jnp.float32

    def nrm(shape, scale):
        return jax.random.normal(next(keys), shape, f32) * scale

    w_buf = min(MAX_WINDOW, PAST_LEN)
    lam_im0 = jnp.pi * jnp.arange(S5_P, dtype=f32)
    return {
        'x_prompt': nrm((BATCH, SEQ, D_MODEL), 1.0),
        'x_sample': nrm((DEC_BATCH, DEC_SEQ, D_MODEL), 1.0),
        'state_s5_re': nrm((N_EVEN, DEC_BATCH, G_A, S5_P), 0.1),
        'state_s5_im': nrm((N_EVEN, DEC_BATCH, G_A, S5_P), 0.1),
        'state_pool': nrm((N_EVEN, DEC_BATCH, POOL_BUF, D_B), 1.0),
        'cache_k': nrm((N_ODD, DEC_BATCH, w_buf, N_HEADS_C, HEAD_DIM), 1.0),
        'cache_v': nrm((N_ODD, DEC_BATCH, w_buf, N_HEADS_C, HEAD_DIM), 1.0),
        'norm_mix': 1.0 + nrm((DEPTH, D_MODEL), 0.02),
        'norm_ffn': 1.0 + nrm((DEPTH, D_MODEL), 0.02),
        'ev_w_in': nrm((N_EVEN, D_MODEL, D_A + D_B), D_MODEL ** -0.5),
        'ev_w_out': nrm((N_EVEN, D_A + D_B, D_MODEL), (D_A + D_B) ** -0.5),
        's5_lambda_re': -0.5 + nrm((N_EVEN, G_A, S5_P), 0.01),
        's5_lambda_im': jnp.broadcast_to(lam_im0, (N_EVEN, G_A, S5_P)) + nrm((N_EVEN, G_A, S5_P), 0.01),
        's5_log_dt': jax.random.uniform(next(keys), (N_EVEN, G_A), f32, minval=math.log(1e-3), maxval=math.log(1e-1)),
        's5_b_re': nrm((N_EVEN, G_A, S5_P, S5_GRP), (2 * S5_GRP) ** -0.5),
        's5_b_im': nrm((N_EVEN, G_A, S5_P, S5_GRP), (2 * S5_GRP) ** -0.5),
        's5_c_re': nrm((N_EVEN, G_A, S5_GRP, S5_P), 1.0),
        's5_c_im': nrm((N_EVEN, G_A, S5_GRP, S5_P), 1.0),
        's5_d': nrm((N_EVEN, D_A), 1.0),
        's5_w_glu': nrm((N_EVEN, D_A, D_A), D_A ** -0.5),
        's5_b_glu': nrm((N_EVEN, D_A), 0.02),
        'pool_w': nrm((N_EVEN, G_B, C_B, C_B), C_B ** -0.5),
        'pool_scale': 1.0 + nrm((N_EVEN, D_B), 0.02),
        'od_w_in': nrm((N_ODD, D_MODEL, 3 * D_C + 2 * D_D), D_MODEL ** -0.5),
        'od_w_out': nrm((N_ODD, D_C + D_D, D_MODEL), (D_C + D_D) ** -0.5),
        'q_norm': 1.0 + nrm((N_ODD, HEAD_DIM), 0.02),
        'k_norm': 1.0 + nrm((N_ODD, HEAD_DIM), 0.02),
        'sgu_ln_g': 1.0 + nrm((N_ODD, D_D), 0.02),
        'sgu_ln_b': nrm((N_ODD, D_D), 0.02),
        'sgu_w': nrm((N_ODD, G_D, CHUNK, CHUNK), CHUNK ** -0.5),
        'sgu_b': 1.0 + nrm((N_ODD, G_D, CHUNK), 0.02),
        'ffn_w1': nrm((DEPTH, D_MODEL, D_FF), D_MODEL ** -0.5),
        'ffn_w3': nrm((DEPTH, D_MODEL, D_FF), D_MODEL ** -0.5),
        'ffn_w2': nrm((DEPTH, D_FF, D_MODEL), D_FF ** -0.5),
    }


def reference(x_prompt, x_sample, state_s5_re, state_s5_im, state_pool, cache_k, cache_v,
              norm_mix, norm_ffn, ev_w_in, ev_w_out, s5_lambda_re, s5_lambda_im, s5_log_dt,
              s5_b_re, s5_b_im, s5_c_re, s5_c_im, s5_d, s5_w_glu, s5_b_glu, pool_w, pool_scale,
              od_w_in, od_w_out, q_norm, k_norm, sgu_ln_g, sgu_ln_b, sgu_w, sgu_b,
              ffn_w1, ffn_w3, ffn_w2):
    xp, xs = x_prompt, x_sample
    bp = xp.shape[0]
    s5r_p, s5i_p, pool_p, k_p, v_p = [], [], [], [], []
    s5r_s, s5i_s, pool_s, k_s, v_s, sgu_s = [], [], [], [], [], []
    for l in range(DEPTH):
        i = l // 2
        if l % 2 == 0:
            s5_params = (s5_lambda_re[i], s5_lambda_im[i], s5_log_dt[i], s5_b_re[i], s5_b_im[i],
                         s5_c_re[i], s5_c_im[i], s5_d[i], s5_w_glu[i], s5_b_glu[i])
            zero_h = jnp.zeros((bp, G_A, S5_P), jnp.float32)
            zero_buf = jnp.zeros((bp, POOL_BUF, D_B), xp.dtype)
            xp, hr, hi, buf = _even_layer(xp, zero_h, zero_h, zero_buf, 0, norm_mix[l], ev_w_in[i], ev_w_out[i],
                                          s5_params, pool_w[i], pool_scale[i])
            s5r_p.append(hr)
            s5i_p.append(hi)
            pool_p.append(buf)
            xs, hr, hi, buf = _even_layer(xs, state_s5_re[i], state_s5_im[i], state_pool[i], PAST_LEN,
                                          norm_mix[l], ev_w_in[i], ev_w_out[i], s5_params, pool_w[i], pool_scale[i])
            s5r_s.append(hr)
            s5i_s.append(hi)
            pool_s.append(buf)
        else:
            xp, nk, nv, _ = _odd_layer(xp, None, None, norm_mix[l], od_w_in[i], od_w_out[i], q_norm[i], k_norm[i],
                                       sgu_ln_g[i], sgu_ln_b[i], sgu_w[i], sgu_b[i])
            k_p.append(nk)
            v_p.append(nv)
            xs, nk, nv, vrows = _odd_layer(xs, cache_k[i], cache_v[i], norm_mix[l], od_w_in[i], od_w_out[i],
                                           q_norm[i], k_norm[i], sgu_ln_g[i], sgu_ln_b[i], sgu_w[i], sgu_b[i])
            k_s.append(nk)
            v_s.append(nv)
            sgu_s.append(vrows)
        xp = _swiglu(xp, norm_ffn[l], ffn_w1[l], ffn_w3[l], ffn_w2[l])
        xs = _swiglu(xs, norm_ffn[l], ffn_w1[l], ffn_w3[l], ffn_w2[l])
    return (xp, xs,
            jnp.stack(s5r_p), jnp.stack(s5i_p), jnp.stack(pool_p), jnp.stack(k_p), jnp.stack(v_p),
            jnp.stack(s5r_s), jnp.stack(s5i_s), jnp.stack(pool_s), jnp.stack(k_s), jnp.stack(v_s),
            jnp.stack(sgu_s))
```

```python
import functools
import math

import jax
import jax.numpy as jnp
from jax import lax
from jax.experimental import pallas as pl
from jax.experimental.pallas import tpu as pltpu

F32 = jnp.float32
BF16 = jnp.bfloat16

RMS_EPS = 1e-6
LN_EPS = 1e-5
NEG_INF = -1e30

LANES = 128
SUBLANES = 8
ATT_BLOCK = 128
CHUNK = 128
POOL_WINDOWS = (2, 4, 8, 16)
POOL_HIST = 16
BRANCHES = ((128, 1), (512, 4), (2048, 16))
S5_GRP = 16
S5_P = 64
SLAB_GROUPS = LANES // S5_GRP
SLAB_STATE = SLAB_GROUPS * S5_P
PAST_LEN = 8192


def _params(sem, vmem_mib):
    return pltpu.CompilerParams(dimension_semantics=sem, vmem_limit_bytes=vmem_mib << 20)


def _gelu(x):
    return 0.5 * x * (1.0 + lax.erf(x * (1.0 / math.sqrt(2.0))))


def _sigmoid(x):
    return 1.0 / (1.0 + jnp.exp(-x))


def _split_bf16(a):
    hi = a.astype(BF16)
    lo = (a - hi.astype(F32)).astype(BF16)
    return hi, lo


def _dot(a, b):
    return jnp.dot(a, b, preferred_element_type=F32)


def _dot_nt(a, b):
    return lax.dot_general(a, b, (((1,), (1,)), ((), ())), preferred_element_type=F32)


def _rms_rows_to(x_ref, g_ref, h_ref, rows):
    step = 16 if rows % 16 == 0 else rows

    def body(r, c):
        sl = pl.ds(pl.multiple_of(r * step, step), step)
        x = x_ref[sl, :]
        ms = jnp.mean(x * x, axis=-1, keepdims=True)
        h_ref[sl, :] = ((x * lax.rsqrt(ms + RMS_EPS)) * g_ref[...]).astype(h_ref.dtype)
        return c

    lax.fori_loop(0, rows // step, body, 0)


def _norm_matmul_kernel(x_ref, g_ref, w_ref, o_ref, h_ref):
    @pl.when(pl.program_id(1) == 0)
    def _():
        _rms_rows_to(x_ref, g_ref, h_ref, x_ref.shape[0])

    o_ref[...] = _dot(h_ref[...], w_ref[...]).astype(o_ref.dtype)


def norm_matmul(x, g, w, *, tm, tn, out_dtype=F32):
    t, d = x.shape
    n = w.shape[1]
    tm = min(tm, t)
    return pl.pallas_call(
        _norm_matmul_kernel,
        grid=(t // tm, n // tn),
        in_specs=[
            pl.BlockSpec((tm, d), lambda i, j: (i, 0)),
            pl.BlockSpec((1, d), lambda i, j: (0, 0)),
            pl.BlockSpec((d, tn), lambda i, j: (0, j)),
        ],
        out_specs=pl.BlockSpec((tm, tn), lambda i, j: (i, j)),
        out_shape=jax.ShapeDtypeStruct((t, n), out_dtype),
        scratch_shapes=[pltpu.VMEM((tm, d), BF16)],
        compiler_params=_params(("arbitrary", "arbitrary"), 48),
        name="norm_matmul",
    )(x, g.reshape(1, d), w)


def _ffn_kernel(x_ref, g_ref, w1_ref, w3_ref, w2_ref, o_ref, h_ref, *, n_chunk):
    @pl.when(pl.program_id(1) == 0)
    def _():
        _rms_rows_to(x_ref, g_ref, h_ref, x_ref.shape[0])
        o_ref[...] = x_ref[...]

    h = h_ref[...]
    a = _dot(h, w1_ref[...])
    b = _dot(h, w3_ref[...])
    u = ((a * _sigmoid(a)) * b).astype(BF16)
    d = o_ref.shape[1]
    for c in range(0, d, n_chunk):
        o_ref[:, c:c + n_chunk] += _dot(u, w2_ref[:, c:c + n_chunk])


def ffn(x, g, w1, w3, w2, *, tm, tf):
    t, d = x.shape
    f = w1.shape[1]
    tm = min(tm, t)
    return pl.pallas_call(
        functools.partial(_ffn_kernel, n_chunk=512),
        grid=(t // tm, f // tf),
        in_specs=[
            pl.BlockSpec((tm, d), lambda i, j: (i, 0)),
            pl.BlockSpec((1, d), lambda i, j: (0, 0)),
            pl.BlockSpec((d, tf), lambda i, j: (0, j)),
            pl.BlockSpec((d, tf), lambda i, j: (0, j)),
            pl.BlockSpec((tf, d), lambda i, j: (j, 0)),
        ],
        out_specs=pl.BlockSpec((tm, d), lambda i, j: (i, 0)),
        out_shape=jax.ShapeDtypeStruct((t, d), F32),
        scratch_shapes=[pltpu.VMEM((tm, d), BF16)],
        compiler_params=_params(("arbitrary", "arbitrary"), 56),
        name="ffn",
    )(x, g.reshape(1, d), w1, w3, w2)


def _out_proj_kernel(x_ref, a_ref, b_ref, wa_ref, wb_ref, o_ref):
    o_ref[...] = x_ref[...] + _dot(a_ref[...], wa_ref[...]) + _dot(b_ref[...], wb_ref[...])


def out_proj(x, a, b, wa, wb, *, tm, tn):
    t, d = x.shape
    ka, kb = a.shape[1], b.shape[1]
    tm = min(tm, t)
    return pl.pallas_call(
        _out_proj_kernel,
        grid=(t // tm, d // tn),
        in_specs=[
            pl.BlockSpec((tm, tn), lambda i, j: (i, j)),
            pl.BlockSpec((tm, ka), lambda i, j: (i, 0)),
            pl.BlockSpec((tm, kb), lambda i, j: (i, 0)),
            pl.BlockSpec((ka, tn), lambda i, j: (0, j)),
            pl.BlockSpec((kb, tn), lambda i, j: (0, j)),
        ],
        out_specs=pl.BlockSpec((tm, tn), lambda i, j: (i, j)),
        out_shape=jax.ShapeDtypeStruct((t, d), F32),
        compiler_params=_params(("arbitrary", "arbitrary"), 48),
        name="out_proj",
    )(x, a, b, wa, wb)


def _s5_prep_kernel(lr_ref, li_ref, ldt_ref, lrx_ref, lix_ref, ldtx_ref, br_ref, bi_ref,
                    pwr_ref, pwi_ref, bbr_ref, bbi_ref):
    dt = jnp.exp(ldt_ref[...])
    mag = jnp.exp(lr_ref[...] * dt)
    ang = li_ref[...] * dt
    p_r, p_i = mag * jnp.cos(ang), mag * jnp.sin(ang)
    c_r, c_i = p_r, p_i
    pwr_ref[0], pwi_ref[0] = c_r, c_i
    for j in range(1, SUBLANES):
        c_r, c_i = c_r * p_r - c_i * p_i, c_r * p_i + c_i * p_r
        pwr_ref[j], pwi_ref[j] = c_r, c_i
    lr, li = lrx_ref[...], lix_ref[...]
    dtx = jnp.exp(ldtx_ref[...])
    magx = jnp.exp(lr * dtx)
    angx = li * dtx
    nr, ni = magx * jnp.cos(angx) - 1.0, magx * jnp.sin(angx)
    den = lr * lr + li * li
    qr = (nr * lr + ni * li) / den
    qi = (ni * lr - nr * li) / den
    br, bi = br_ref[...], bi_ref[...]
    bbr_ref[...] = qr * br - qi * bi
    bbi_ref[...] = qr * bi + qi * br


def s5_prep(lam_re, lam_im, log_dt, b_re, b_im):
    g, p = lam_re.shape
    h = b_re.shape[2]
    n_slab = g // SLAB_GROUPS
    slab = lambda a: a.reshape(n_slab, SLAB_GROUPS * p)
    rep = lambda a: jnp.repeat(a, h, axis=1)
    ldt_gp = jnp.broadcast_to(log_dt[:, None], (g, p))
    outs = pl.pallas_call(
        _s5_prep_kernel,
        out_shape=[
            jax.ShapeDtypeStruct((SUBLANES, n_slab, SLAB_GROUPS * p), F32),
            jax.ShapeDtypeStruct((SUBLANES, n_slab, SLAB_GROUPS * p), F32),
            jax.ShapeDtypeStruct((g, p * h), F32),
            jax.ShapeDtypeStruct((g, p * h), F32),
        ],
        name="s5_prep",
    )(slab(lam_re), slab(lam_im), slab(ldt_gp), rep(lam_re), rep(lam_im), rep(ldt_gp),
      b_re.reshape(g, p * h), b_im.reshape(g, p * h))
    pw_re, pw_im, bb_re, bb_im = outs
    pw_re = jnp.transpose(pw_re, (1, 0, 2))
    pw_im = jnp.transpose(pw_im, (1, 0, 2))
    return pw_re, pw_im, bb_re.reshape(g, p, h), bb_im.reshape(g, p, h)


def s5_block_matrices(bb_re, bb_im, c_re, c_im):
    g, p, h = bb_re.shape
    n_slab = g // SLAB_GROUPS
    eye = jnp.eye(SLAB_GROUPS, dtype=F32)

    def in_map(bb):
        t = bb.reshape(n_slab, SLAB_GROUPS, p, h)
        return jnp.einsum("kgph,gj->kghjp", t, eye).reshape(n_slab, SLAB_GROUPS * h, SLAB_GROUPS * p)

    def out_map(c):
        t = c.reshape(n_slab, SLAB_GROUPS, h, p)
        return jnp.einsum("kghp,gj->kgpjh", t, eye).reshape(n_slab, SLAB_GROUPS * p, SLAB_GROUPS * h)

    b_blk = jnp.concatenate([in_map(bb_re), in_map(bb_im)], axis=2)
    c_blk = jnp.concatenate([out_map(c_re), -out_map(c_im)], axis=1)
    return b_blk, c_blk


def _s5_scan_kernel(u_ref, bblk_ref, cblk_ref, pwr_ref, pwi_ref, d_ref, h0_ref,
                    y_ref, hl_ref, h_scr, bh_scr, bl_scr, ch_scr, cl_scr, *, seq, row_chunk):
    ns = SLAB_STATE
    bh, bl = _split_bf16(bblk_ref[...])
    bh_scr[...], bl_scr[...] = bh, bl
    ch, cl = _split_bf16(cblk_ref[...])
    ch_scr[...], cl_scr[...] = ch, cl
    n_chunks = seq // row_chunk

    def proj_in(r, c):
        sl = pl.ds(pl.multiple_of(r * row_chunk, row_chunk), row_chunk)
        uh, ul = _split_bf16(u_ref[sl, :])
        h_scr[sl, :] = _dot(uh, bh_scr[...]) + _dot(ul, bh_scr[...]) + _dot(uh, bl_scr[...])
        return c

    lax.fori_loop(0, n_chunks, proj_in, 0)

    rowid = lax.broadcasted_iota(jnp.int32, (SUBLANES, LANES), 0)
    for c in range(ns // LANES):
        re_l = slice(c * LANES, (c + 1) * LANES)
        im_l = slice(ns + c * LANES, ns + (c + 1) * LANES)
        p_r, p_i = pwr_ref[:, re_l], pwi_ref[:, re_l]
        steps = []
        for dist in (1, 2, 4):
            a_r = jnp.where(rowid >= dist, jnp.broadcast_to(p_r[dist - 1:dist], (SUBLANES, LANES)), 0.0)
            a_i = jnp.where(rowid >= dist, jnp.broadcast_to(p_i[dist - 1:dist], (SUBLANES, LANES)), 0.0)
            steps.append((dist, a_r, a_i))
        c_r = jnp.broadcast_to(h0_ref[:, re_l], (SUBLANES, LANES))
        c_i = jnp.broadcast_to(h0_ref[:, im_l], (SUBLANES, LANES))

        def scan_rows(gidx, carry, re_l=re_l, im_l=im_l, p_r=p_r, p_i=p_i, steps=steps):
            c_r, c_i = carry
            sl = pl.ds(pl.multiple_of(gidx * SUBLANES, SUBLANES), SUBLANES)
            r, i = h_scr[sl, re_l], h_scr[sl, im_l]
            for dist, a_r, a_i in steps:
                s_r, s_i = pltpu.roll(r, dist, 0), pltpu.roll(i, dist, 0)
                r, i = r + (s_r * a_r - s_i * a_i), i + (s_r * a_i + s_i * a_r)
            r, i = r + (c_r * p_r - c_i * p_i), i + (c_r * p_i + c_i * p_r)
            h_scr[sl, re_l], h_scr[sl, im_l] = r, i
            last = slice(SUBLANES - 1, SUBLANES)
            return (jnp.broadcast_to(r[last], (SUBLANES, LANES)), jnp.broadcast_to(i[last], (SUBLANES, LANES)))

        c_r, c_i = lax.fori_loop(0, seq // SUBLANES, scan_rows, (c_r, c_i))
        hl_ref[:, re_l] = c_r[0:1]
        hl_ref[:, im_l] = c_i[0:1]

    def proj_out(r, c):
        sl = pl.ds(pl.multiple_of(r * row_chunk, row_chunk), row_chunk)
        hh, hlo = _split_bf16(h_scr[sl, :])
        u = u_ref[sl, :]
        y = _dot(hh, ch_scr[...]) + _dot(hlo, ch_scr[...]) + _dot(hh, cl_scr[...]) + d_ref[...] * u
        y_ref[sl, :] = _gelu(y)
        return c

    lax.fori_loop(0, n_chunks, proj_out, 0)


def s5_scan(z, col0, b_blk, c_blk, pw_re, pw_im, d_skip, h0):
    bsz, seq, _ = z.shape
    n_slab = b_blk.shape[0]
    ns2 = 2 * SLAB_STATE
    row_chunk = min(seq, 256)
    kern = functools.partial(_s5_scan_kernel, seq=seq, row_chunk=row_chunk)
    return pl.pallas_call(
        kern,
        grid=(bsz, n_slab),
        in_specs=[
            pl.BlockSpec((None, seq, LANES), lambda b, k: (b, 0, col0 + k)),
            pl.BlockSpec((None, LANES, ns2), lambda b, k: (k, 0, 0)),
            pl.BlockSpec((None, ns2, LANES), lambda b, k: (k, 0, 0)),
            pl.BlockSpec((None, SUBLANES, SLAB_STATE), lambda b, k: (k, 0, 0)),
            pl.BlockSpec((None, SUBLANES, SLAB_STATE), lambda b, k: (k, 0, 0)),
            pl.BlockSpec((1, LANES), lambda b, k: (0, k)),
            pl.BlockSpec((None, None, 1, ns2), lambda b, k: (b, k, 0, 0)),
        ],
        out_specs=[
            pl.BlockSpec((None, seq, LANES), lambda b, k: (b, 0, k)),
            pl.BlockSpec((None, None, 1, ns2), lambda b, k: (b, k, 0, 0)),
        ],
        out_shape=[
            jax.ShapeDtypeStruct((bsz, seq, n_slab * LANES), F32),
            jax.ShapeDtypeStruct((bsz, n_slab, 1, ns2), F32),
        ],
        scratch_shapes=[
            pltpu.VMEM((seq, ns2), F32),
            pltpu.VMEM((LANES, ns2), BF16), pltpu.VMEM((LANES, ns2), BF16),
            pltpu.VMEM((ns2, LANES), BF16), pltpu.VMEM((ns2, LANES), BF16),
        ],
        compiler_params=_params(("arbitrary", "arbitrary"), 40),
        name="s5_scan",
    )(z, b_blk, c_blk, pw_re, pw_im, d_skip.reshape(1, -1), h0)


def _glu_kernel(yk_ref, yj_ref, w_ref, b_ref, o_ref, yb_scr):
    @pl.when(pl.program_id(1) == 0)
    def _():
        yb_scr[...] = yk_ref[...].astype(BF16)

    gate = _dot(yb_scr[...], w_ref[...]) + b_ref[...]
    o_ref[...] = (yj_ref[...] * _sigmoid(gate)).astype(o_ref.dtype)


def glu(y, w, b, *, tm, tn):
    t, d = y.shape
    tm = min(tm, t)
    return pl.pallas_call(
        _glu_kernel,
        grid=(t // tm, d // tn),
        in_specs=[
            pl.BlockSpec((tm, d), lambda i, j: (i, 0)),
            pl.BlockSpec((tm, tn), lambda i, j: (i, j)),
            pl.BlockSpec((d, tn), lambda i, j: (0, j)),
            pl.BlockSpec((1, tn), lambda i, j: (0, j)),
        ],
        out_specs=pl.BlockSpec((tm, tn), lambda i, j: (i, j)),
        out_shape=jax.ShapeDtypeStruct((t, d), BF16),
        scratch_shapes=[pltpu.VMEM((tm, d), BF16)],
        compiler_params=_params(("arbitrary", "arbitrary"), 40),
        name="glu",
    )(y, y, w, b.reshape(1, d))


def _pool_kernel(u_ref, buf_ref, w_ref, s_ref, y_ref, tail_ref, ext_scr, *, tc, start_pos, cg):
    c = pl.program_id(1)

    @pl.when(c == 0)
    def _():
        ext_scr[0:POOL_HIST, :] = buf_ref[...]

    ext_scr[POOL_HIST:POOL_HIST + tc, :] = u_ref[...]
    pos = start_pos + c * tc + lax.broadcasted_iota(jnp.int32, (tc, 1), 0)
    for g, win in enumerate(POOL_WINDOWS):
        cols = slice(g * cg, (g + 1) * cg)
        x = ext_scr[:, cols]
        acc, dist = x, 1
        while dist < win:
            acc = acc + pltpu.roll(acc, dist, 0)
            dist *= 2
        wsum = acc[POOL_HIST:, :]
        cnt = jnp.minimum(pos + 1, win).astype(F32)
        zg = wsum * (1.0 / cnt) - x[POOL_HIST:, :]
        y = _dot(zg.astype(BF16), w_ref[g]) * s_ref[:, cols]
        y_ref[:, cols] = y.astype(y_ref.dtype)

    tail = ext_scr[tc:tc + POOL_HIST, :]
    ext_scr[0:POOL_HIST, :] = tail

    @pl.when(c == pl.num_programs(1) - 1)
    def _():
        tail_ref[...] = tail


def pool(z, colblk, buf16, w, scale, *, start_pos, tc):
    bsz, seq, _ = z.shape
    n_g, cg, _ = w.shape
    db = n_g * cg
    tc = min(tc, seq)
    kern = functools.partial(_pool_kernel, tc=tc, start_pos=start_pos, cg=cg)
    return pl.pallas_call(
        kern,
        grid=(bsz, seq // tc),
        in_specs=[
            pl.BlockSpec((None, tc, db), lambda b, c: (b, c, colblk)),
            pl.BlockSpec((None, POOL_HIST, db), lambda b, c: (b, 0, 0)),
            pl.BlockSpec((n_g, cg, cg), lambda b, c: (0, 0, 0)),
            pl.BlockSpec((1, db), lambda b, c: (0, 0)),
        ],
        out_specs=[
            pl.BlockSpec((None, tc, db), lambda b, c: (b, c, 0)),
            pl.BlockSpec((None, POOL_HIST, db), lambda b, c: (b, 0, 0)),
        ],
        out_shape=[
            jax.ShapeDtypeStruct((bsz, seq, db), BF16),
            jax.ShapeDtypeStruct((bsz, POOL_HIST, db), F32),
        ],
        scratch_shapes=[pltpu.VMEM((POOL_HIST + tc, db), F32)],
        compiler_params=_params(("arbitrary", "arbitrary"), 40),
        name="pool",
    )(z, buf16, w, scale.reshape(1, db))


def _head_rms(x, g):
    ms = jnp.mean(x * x, axis=-1, keepdims=True)
    return (x * lax.rsqrt(ms + RMS_EPS)) * g


def _combine(os_, lses):
    m = jnp.maximum(jnp.maximum(lses[0], lses[1]), lses[2])
    ws = [jnp.exp(l - m) for l in lses]
    tot = ws[0] + ws[1] + ws[2]
    return (ws[0] * os_[0] + ws[1] * os_[1] + ws[2] * os_[2]) / tot


def _attn_prompt_kernel(q_ref, k_ref, v_ref, qn_ref, kn_ref, att_ref, ko_ref, vo_ref,
                        qs_scr, o_scr, l_scr, *, seq, scale):
    blk = ATT_BLOCK
    rows = 256

    def prep(r, c):
        sl = pl.ds(pl.multiple_of(r * rows, rows), rows)
        qs_scr[sl, :] = _head_rms(q_ref[sl, :], qn_ref[...])
        ko_ref[sl, :] = _head_rms(k_ref[sl, :], kn_ref[...])
        vo_ref[sl, :] = v_ref[sl, :]
        return c

    lax.fori_loop(0, seq // rows, prep, 0)

    qi = lax.broadcasted_iota(jnp.int32, (blk, blk), 0)
    kj = lax.broadcasted_iota(jnp.int32, (blk, blk), 1)
    cur_ok = kj <= qi
    prev_ok = kj >= qi

    for g, (window, dil) in enumerate(BRANCHES):
        n_blk = seq // (dil * blk)

        def block(idx, c, g=g, dil=dil, n_blk=n_blk):
            res = idx // n_blk
            n = idx - res * n_blk
            start = res + n * (dil * blk)
            cur = pl.ds(start, blk, stride=dil) if dil > 1 else pl.ds(pl.multiple_of(start, blk), blk)
            q = qs_scr[cur, :].astype(BF16)
            kc = ko_ref[cur, :].astype(BF16)
            vc = vo_ref[cur, :].astype(BF16)
            s_c = jnp.where(cur_ok, _dot_nt(q, kc) * scale, NEG_INF)
            if n_blk > 1:
                pstart = jnp.maximum(start - dil * blk, res)
                prv = pl.ds(pstart, blk, stride=dil) if dil > 1 else pl.ds(pl.multiple_of(pstart, blk), blk)
                kp = ko_ref[prv, :].astype(BF16)
                vp = vo_ref[prv, :].astype(BF16)
                s_p = jnp.where(jnp.logical_and(prev_ok, n > 0), _dot_nt(q, kp) * scale, NEG_INF)
                m = jnp.maximum(jnp.max(s_c, axis=-1, keepdims=True), jnp.max(s_p, axis=-1, keepdims=True))
                p_c, p_p = jnp.exp(s_c - m), jnp.exp(s_p - m)
                l = jnp.sum(p_c, axis=-1, keepdims=True) + jnp.sum(p_p, axis=-1, keepdims=True)
                o = _dot(p_c.astype(BF16), vc) + _dot(p_p.astype(BF16), vp)
            else:
                m = jnp.max(s_c, axis=-1, keepdims=True)
                p_c = jnp.exp(s_c - m)
                l = jnp.sum(p_c, axis=-1, keepdims=True)
                o = _dot(p_c.astype(BF16), vc)
            o_scr[g, cur, :] = o / l
            l_scr[g, cur, :] = jnp.broadcast_to(m + jnp.log(l), (blk, LANES))
            return c

        lax.fori_loop(0, seq // blk, block, 0)

    def comb(r, c):
        sl = pl.ds(pl.multiple_of(r * rows, rows), rows)
        out = _combine([o_scr[g, sl, :] for g in range(3)], [l_scr[g, sl, :] for g in range(3)])
        att_ref[sl, :] = out.astype(att_ref.dtype)
        return c

    lax.fori_loop(0, seq // rows, comb, 0)


def attn_prompt(z, qn, kn, *, n_heads):
    bsz, seq, _ = z.shape
    assert seq % (BRANCHES[-1][1] * ATT_BLOCK) == 0
    hd = LANES
    kern = functools.partial(_attn_prompt_kernel, seq=seq, scale=hd ** -0.5)
    blk = lambda off: pl.BlockSpec((None, seq, hd), lambda b, h: (b, 0, off + h))
    return pl.pallas_call(
        kern,
        grid=(bsz, n_heads),
        in_specs=[blk(0), blk(n_heads), blk(2 * n_heads),
                  pl.BlockSpec((1, hd), lambda b, h: (0, 0)), pl.BlockSpec((1, hd), lambda b, h: (0, 0))],
        out_specs=[blk(0), blk(0), blk(0)],
        out_shape=[
            jax.ShapeDtypeStruct((bsz, seq, n_heads * hd), BF16),
            jax.ShapeDtypeStruct((bsz, seq, n_heads * hd), F32),
            jax.ShapeDtypeStruct((bsz, seq, n_heads * hd), F32),
        ],
        scratch_shapes=[
            pltpu.VMEM((seq, hd), F32),
            pltpu.VMEM((3, seq, hd), F32),
            pltpu.VMEM((3, seq, hd), F32),
        ],
        compiler_params=_params(("arbitrary", "arbitrary"), 40),
        name="attn_prompt",
    )(z, z, z, qn.reshape(1, hd), kn.reshape(1, hd))


def _attn_sample_kernel(q_ref, k_ref, v_ref, ck_ref, cv_ref, qn_ref, kn_ref, att_ref, ko_ref, vo_ref,
                        q_scr, kn_scr, vn_scr, *, s_new, n_buf, scale):
    pad = q_scr.shape[0]
    q_scr[...] = jnp.zeros_like(q_scr)
    kn_scr[...] = jnp.zeros_like(kn_scr)
    vn_scr[...] = jnp.zeros_like(vn_scr)
    k_new = _head_rms(k_ref[...], kn_ref[...])
    v_new = v_ref[...]
    ko_ref[...] = k_new
    vo_ref[...] = v_new
    q_scr[0:s_new, :] = _head_rms(q_ref[...], qn_ref[...])
    kn_scr[0:s_new, :] = k_new
    vn_scr[0:s_new, :] = v_new

    q = q_scr[...].astype(BF16)
    ck = ck_ref[...].astype(BF16)
    cv = cv_ref[...].astype(BF16)
    s_c = _dot_nt(q, ck) * scale
    s_n = _dot_nt(q, kn_scr[...].astype(BF16)) * scale
    qi_c = lax.broadcasted_iota(jnp.int32, (pad, n_buf), 0)
    kj_c = lax.broadcasted_iota(jnp.int32, (pad, n_buf), 1)
    dist_c = n_buf + qi_c - kj_c
    qi_n = lax.broadcasted_iota(jnp.int32, (pad, pad), 0)
    kj_n = lax.broadcasted_iota(jnp.int32, (pad, pad), 1)
    dist_n = qi_n - kj_n
    new_ok = jnp.logical_and(dist_n >= 0, kj_n < s_new)
    outs, lses = [], []
    for window, dil in BRANCHES:
        ok_c = jnp.logical_and((dist_c & (dil - 1)) == 0, dist_c <= window)
        ok_n = jnp.logical_and(new_ok, (dist_n & (dil - 1)) == 0)
        m_c = jnp.where(ok_c, s_c, NEG_INF)
        m_n = jnp.where(ok_n, s_n, NEG_INF)
        m = jnp.maximum(jnp.max(m_c, axis=-1, keepdims=True), jnp.max(m_n, axis=-1, keepdims=True))
        p_c, p_n = jnp.exp(m_c - m), jnp.exp(m_n - m)
        l = jnp.sum(p_c, axis=-1, keepdims=True) + jnp.sum(p_n, axis=-1, keepdims=True)
        o = _dot(p_c.astype(BF16), cv) + _dot(p_n.astype(BF16), vn_scr[...].astype(BF16))
        outs.append(o / l)
        lses.append(jnp.broadcast_to(m + jnp.log(l), (pad, LANES)))
    att_ref[...] = _combine(outs, lses)[0:s_new, :].astype(att_ref.dtype)


def attn_sample(z, cache_k, cache_v, qn, kn, *, n_heads):
    bsz, s_new, _ = z.shape
    n_buf = cache_k.shape[1]
    assert n_buf >= BRANCHES[-1][0]
    hd = LANES
    pad = 16
    kern = functools.partial(_attn_sample_kernel, s_new=s_new, n_buf=n_buf, scale=hd ** -0.5)
    blk = lambda off: pl.BlockSpec((None, s_new, hd), lambda b, h: (b, 0, off + h))
    cblk = pl.BlockSpec((None, n_buf, hd), lambda b, h: (b, 0, h))
    vec = pl.BlockSpec((1, hd), lambda b, h: (0, 0))
    return pl.pallas_call(
        kern,
        grid=(bsz, n_heads),
        in_specs=[blk(0), blk(n_heads), blk(2 * n_heads), cblk, cblk, vec, vec],
        out_specs=[blk(0), blk(0), blk(0)],
        out_shape=[
            jax.ShapeDtypeStruct((bsz, s_new, n_heads * hd), BF16),
            jax.ShapeDtypeStruct((bsz, s_new, n_heads * hd), F32),
            jax.ShapeDtypeStruct((bsz, s_new, n_heads * hd), F32),
        ],
        scratch_shapes=[pltpu.VMEM((pad, hd), F32), pltpu.VMEM((pad, hd), F32), pltpu.VMEM((pad, hd), F32)],
        compiler_params=_params(("arbitrary", "arbitrary"), 40),
        name="attn_sample",
    )(z, z, z, cache_k, cache_v, qn.reshape(1, hd), kn.reshape(1, hd))


def _sgu_kernel(gu_ref, gv_ref, lg_ref, lb_ref, w_ref, bt_ref, o_ref, vn_ref, vb_scr, *, rows, n_g, cd):
    t = w_ref.shape[1]
    gv = _gelu(gv_ref[...])
    mu = jnp.mean(gv, axis=-1, keepdims=True)
    xc = gv - mu
    var = jnp.mean(xc * xc, axis=-1, keepdims=True)
    vn = (xc * lax.rsqrt(var + LN_EPS)) * lg_ref[...] + lb_ref[...]
    vn_ref[...] = vn
    if rows < t:
        vb_scr[...] = jnp.zeros_like(vb_scr)
    vb_scr[0:rows, :] = vn.astype(BF16)
    ri = lax.broadcasted_iota(jnp.int32, (t, t), 0)
    ci = lax.broadcasted_iota(jnp.int32, (t, t), 1)
    for g in range(n_g):
        cols = slice(g * cd, (g + 1) * cd)
        wg = jnp.where(ri >= ci, w_ref[g], 0.0).astype(BF16)
        mixed = _dot(wg, vb_scr[:, cols])[0:rows, :] + bt_ref[:, g:g + 1]
        o_ref[:, cols] = (_gelu(gu_ref[:, cols]) * mixed).astype(o_ref.dtype)


def sgu(z, colblk_u, ln_g, ln_b, w_s, b_s):
    bsz, seq, _ = z.shape
    n_g = w_s.shape[0]
    dd = ln_g.shape[0]
    cd = dd // n_g
    t = min(seq, CHUNK)
    tp = max(t, LANES)
    w = jnp.pad(w_s[:, :t, :t], ((0, 0), (0, tp - t), (0, tp - t)))
    bt = jnp.transpose(b_s[:, :t])
    kern = functools.partial(_sgu_kernel, rows=t, n_g=n_g, cd=cd)
    return pl.pallas_call(
        kern,
        grid=(bsz, seq // t),
        in_specs=[
            pl.BlockSpec((None, t, dd), lambda b, c: (b, c, colblk_u)),
            pl.BlockSpec((None, t, dd), lambda b, c: (b, c, colblk_u + 1)),
            pl.BlockSpec((1, dd), lambda b, c: (0, 0)),
            pl.BlockSpec((1, dd), lambda b, c: (0, 0)),
            pl.BlockSpec((n_g, tp, tp), lambda b, c: (0, 0, 0)),
            pl.BlockSpec((t, n_g), lambda b, c: (0, 0)),
        ],
        out_specs=[
            pl.BlockSpec((None, t, dd), lambda b, c: (b, c, 0)),
            pl.BlockSpec((None, t, dd), lambda b, c: (b, c, 0)),
        ],
        out_shape=[
            jax.ShapeDtypeStruct((bsz, seq, dd), BF16),
            jax.ShapeDtypeStruct((bsz, seq, dd), F32),
        ],
        scratch_shapes=[pltpu.VMEM((tp, dd), BF16)],
        compiler_params=_params(("arbitrary", "arbitrary"), 40),
        name="sgu",
    )(z, z, ln_g.reshape(1, dd), ln_b.reshape(1, dd), w, bt)


def _even_layer(x, bsz, seq, h0_re, h0_im, pool_buf, start_pos, norm_g, w_in, w_out, s5p, pool_w, pool_scale,
                w_glu, b_glu, d_skip, tiles):
    t, d = x.shape
    pw_re, pw_im, b_blk, c_blk = s5p
    n_slab = b_blk.shape[0]
    d_a = n_slab * LANES
    z = norm_matmul(x, norm_g, w_in, tm=tiles["tm"], tn=tiles["tn"]).reshape(bsz, seq, -1)
    h0 = jnp.concatenate([h0_re.reshape(bsz, n_slab, 1, SLAB_STATE), h0_im.reshape(bsz, n_slab, 1, SLAB_STATE)], axis=-1)
    y_pre, h_last = s5_scan(z, 0, b_blk, c_blk, pw_re, pw_im, d_skip, h0)
    ya = glu(y_pre.reshape(t, d_a), w_glu, b_glu, tm=tiles["tm"], tn=tiles["tn"])
    buf16 = jnp.pad(pool_buf, ((0, 0), (POOL_HIST - pool_buf.shape[1], 0), (0, 0)))
    yb, tail = pool(z, 1, buf16, pool_w, pool_scale, start_pos=start_pos, tc=256)
    x = out_proj(x, ya, yb.reshape(t, -1), w_out[:d_a], w_out[d_a:], tm=tiles["tm"], tn=tiles["tn_out"])
    g_a = n_slab * SLAB_GROUPS
    h_re = h_last[..., :SLAB_STATE].reshape(bsz, g_a, S5_P)
    h_im = h_last[..., SLAB_STATE:].reshape(bsz, g_a, S5_P)
    return x, h_re, h_im, tail[:, POOL_HIST - pool_buf.shape[1]:]


def _odd_layer(x, bsz, seq, k_buf, v_buf, norm_g, w_in, w_out, qn, kn, ln_g, ln_b, w_s, b_s, n_heads, tiles):
    t, d = x.shape
    d_c = n_heads * LANES
    z = norm_matmul(x, norm_g, w_in, tm=tiles["tm"], tn=tiles["tn"]).reshape(bsz, seq, -1)
    if k_buf is None:
        att, k_new, v_new = attn_prompt(z, qn, kn, n_heads=n_heads)
    else:
        att, k_new, v_new = attn_sample(z, k_buf.reshape(bsz, -1, d_c), v_buf.reshape(bsz, -1, d_c), qn, kn,
                                        n_heads=n_heads)
    dd = ln_g.shape[0]
    sg, vn = sgu(z, (3 * d_c) // dd, ln_g, ln_b, w_s, b_s)
    x = out_proj(x, att.reshape(t, d_c), sg.reshape(t, dd), w_out[:d_c], w_out[d_c:], tm=tiles["tm"],
                 tn=tiles["tn_out"])
    hd = LANES
    return x, k_new.reshape(bsz, seq, n_heads, hd), v_new.reshape(bsz, seq, n_heads, hd), vn


def kernel(x_prompt, x_sample, state_s5_re, state_s5_im, state_pool, cache_k, cache_v, norm_mix, norm_ffn, ev_w_in, ev_w_out, s5_lambda_re, s5_lambda_im, s5_log_dt, s5_b_re, s5_b_im, s5_c_re, s5_c_im, s5_d, s5_w_glu, s5_b_glu, pool_w, pool_scale, od_w_in, od_w_out, q_norm, k_norm, sgu_ln_g, sgu_ln_b, sgu_w, sgu_b, ffn_w1, ffn_w3, ffn_w2):
    bp, lp, d = x_prompt.shape
    bs, ls, _ = x_sample.shape
    depth = norm_mix.shape[0]
    n_heads = cache_k.shape[3]
    xp = x_prompt.reshape(bp * lp, d)
    xs = x_sample.reshape(bs * ls, d)
    bf = lambda a: a.astype(BF16)
    tiles_p = dict(tm=512, tn=512, tn_out=1024)
    tiles_s = dict(tm=64, tn=512, tn_out=1024)
    g_a, p_a = s5_lambda_re.shape[1:]

    s5r_p, s5i_p, pool_p, k_p, v_p = [], [], [], [], []
    s5r_s, s5i_s, pool_s, k_s, v_s, sgu_s = [], [], [], [], [], []
    for l in range(depth):
        i = l // 2
        if l % 2 == 0:
            pw_re, pw_im, bb_re, bb_im = s5_prep(s5_lambda_re[i], s5_lambda_im[i], s5_log_dt[i], s5_b_re[i], s5_b_im[i])
            b_blk, c_blk = s5_block_matrices(bb_re, bb_im, s5_c_re[i], s5_c_im[i])
            s5p = (pw_re, pw_im, b_blk, c_blk)
            w_in, w_out, w_glu, w_pool = bf(ev_w_in[i]), bf(ev_w_out[i]), bf(s5_w_glu[i]), bf(pool_w[i])
            zero_h = jnp.zeros((bp, g_a, p_a), F32)
            zero_buf = jnp.zeros((bp, state_pool.shape[2], state_pool.shape[3]), F32)
            xp, hr, hi, buf = _even_layer(xp, bp, lp, zero_h, zero_h, zero_buf, 0, norm_mix[l], w_in, w_out, s5p,
                                          w_pool, pool_scale[i], w_glu, s5_b_glu[i], s5_d[i], tiles_p)
            s5r_p.append(hr); s5i_p.append(hi); pool_p.append(buf)
            xs, hr, hi, buf = _even_layer(xs, bs, ls, state_s5_re[i], state_s5_im[i], state_pool[i], PAST_LEN,
                                          norm_mix[l], w_in, w_out, s5p, w_pool, pool_scale[i], w_glu, s5_b_glu[i],
                                          s5_d[i], tiles_s)
            s5r_s.append(hr); s5i_s.append(hi); pool_s.append(buf)
        else:
            w_in, w_out = bf(od_w_in[i]), bf(od_w_out[i])
            args = (norm_mix[l], w_in, w_out, q_norm[i], k_norm[i], sgu_ln_g[i], sgu_ln_b[i], sgu_w[i], sgu_b[i], n_heads)
            xp, nk, nv, _ = _odd_layer(xp, bp, lp, None, None, *args, tiles_p)
            k_p.append(nk); v_p.append(nv)
            xs, nk, nv, vrows = _odd_layer(xs, bs, ls, cache_k[i], cache_v[i], *args, tiles_s)
            k_s.append(nk); v_s.append(nv); sgu_s.append(vrows)
        w1, w3, w2 = bf(ffn_w1[l]), bf(ffn_w3[l]), bf(ffn_w2[l])
        xp = ffn(xp, norm_ffn[l], w1, w3, w2, tm=512, tf=256)
        xs = ffn(xs, norm_ffn[l], w1, w3, w2, tm=64, tf=256)
    return (xp.reshape(bp, lp, d), xs.reshape(bs, ls, d),
            jnp.stack(s5r_p), jnp.stack(s5i_p), jnp.stack(pool_p), jnp.stack(k_p), jnp.stack(v_p),
            jnp.stack(s5r_s), jnp.stack(s5i_s), jnp.stack(pool_s), jnp.stack(k_s), jnp.stack(v_s),
            jnp.stack(sgu_s))
```

```python
import functools
import math

import jax
import jax.numpy as jnp
from jax import lax
from jax.experimental import pallas as pl
from jax.experimental.pallas import tpu as pltpu

F32 = jnp.float32
BF16 = jnp.bfloat16

RMS_EPS = 1e-6
LN_EPS = 1e-5
NEG_INF = -1e30

LANES = 128
SUBLANES = 8
ATT_BLOCK = 128
CHUNK = 128
POOL_WINDOWS = (2, 4, 8, 16)
POOL_HIST = 16
BRANCHES = ((128, 1), (512, 4), (2048, 16))
S5_GRP = 16
S5_P = 64
SLAB_GROUPS = LANES // S5_GRP
SLAB_STATE = SLAB_GROUPS * S5_P
PAST_LEN = 8192
FFN_TILE = 512


def _params(sem, vmem_mib):
    return pltpu.CompilerParams(dimension_semantics=sem, vmem_limit_bytes=vmem_mib << 20)


def _gelu(x):
    return 0.5 * x * (1.0 + lax.erf(x * (1.0 / math.sqrt(2.0))))


def _sigmoid(x):
    return 1.0 / (1.0 + jnp.exp(-x))


def _split_bf16(a):
    hi = a.astype(BF16)
    lo = (a - hi.astype(F32)).astype(BF16)
    return hi, lo


def _dot(a, b):
    return jnp.dot(a, b, preferred_element_type=F32)


def _dot_nt(a, b):
    return lax.dot_general(a, b, (((1,), (1,)), ((), ())), preferred_element_type=F32)


def _rms_rows_to(x_ref, g_ref, h_ref, rows):
    step = 16 if rows % 16 == 0 else rows

    def body(r, c):
        sl = pl.ds(pl.multiple_of(r * step, step), step)
        x = x_ref[sl, :]
        ms = jnp.mean(x * x, axis=-1, keepdims=True)
        h_ref[sl, :] = ((x * lax.rsqrt(ms + RMS_EPS)) * g_ref[...]).astype(h_ref.dtype)
        return c

    lax.fori_loop(0, rows // step, body, 0)


def _norm_matmul_kernel(x_ref, g_ref, w_ref, o_ref, h_ref):
    @pl.when(pl.program_id(1) == 0)
    def _():
        _rms_rows_to(x_ref, g_ref, h_ref, x_ref.shape[0])

    o_ref[...] = _dot(h_ref[...], w_ref[...]).astype(o_ref.dtype)


def norm_matmul(x, g, w, layer, *, tm, tn, out_dtype=F32):
    t, d = x.shape
    n = w.shape[2]
    tm = min(tm, t)
    return pl.pallas_call(
        _norm_matmul_kernel,
        grid=(t // tm, n // tn),
        in_specs=[
            pl.BlockSpec((tm, d), lambda i, j: (i, 0)),
            pl.BlockSpec((1, d), lambda i, j: (0, 0)),
            pl.BlockSpec((None, d, tn), lambda i, j: (layer, 0, j)),
        ],
        out_specs=pl.BlockSpec((tm, tn), lambda i, j: (i, j)),
        out_shape=jax.ShapeDtypeStruct((t, n), out_dtype),
        scratch_shapes=[pltpu.VMEM((tm, d), BF16)],
        compiler_params=_params(("arbitrary", "arbitrary"), 58),
        name="norm_matmul",
    )(x, g.reshape(1, d), w)


def _ffn_kernel(x_ref, g_ref, w1_ref, w3_ref, w2_ref, o_ref, h_ref, *, n_chunk):
    @pl.when(pl.program_id(1) == 0)
    def _():
        _rms_rows_to(x_ref, g_ref, h_ref, x_ref.shape[0])
        o_ref[...] = x_ref[...]

    h = h_ref[...]
    a = _dot(h, w1_ref[...])
    b = _dot(h, w3_ref[...])
    u = ((a * _sigmoid(a)) * b).astype(BF16)
    d = o_ref.shape[1]
    for c in range(0, d, n_chunk):
        o_ref[:, c:c + n_chunk] += _dot(u, w2_ref[:, c:c + n_chunk])


def ffn(x, g, w1, w3, w2, layer, *, tm, tf):
    t, d = x.shape
    f = w1.shape[2]
    tm = min(tm, t)
    return pl.pallas_call(
        functools.partial(_ffn_kernel, n_chunk=512),
        grid=(t // tm, f // tf),
        in_specs=[
            pl.BlockSpec((tm, d), lambda i, j: (i, 0), pipeline_mode=pl.Buffered(1)),
            pl.BlockSpec((1, d), lambda i, j: (0, 0)),
            pl.BlockSpec((None, d, tf), lambda i, j: (layer, 0, j)),
            pl.BlockSpec((None, d, tf), lambda i, j: (layer, 0, j)),
            pl.BlockSpec((None, tf, d), lambda i, j: (layer, j, 0)),
        ],
        out_specs=pl.BlockSpec((tm, d), lambda i, j: (i, 0)),
        out_shape=jax.ShapeDtypeStruct((t, d), F32),
        scratch_shapes=[pltpu.VMEM((tm, d), BF16)],
        compiler_params=_params(("arbitrary", "arbitrary"), 60),
        name="ffn",
    )(x, g.reshape(1, d), w1, w3, w2)


def _out_proj_kernel(x_ref, a_ref, b_ref, wa_ref, wb_ref, o_ref):
    o_ref[...] = x_ref[...] + _dot(a_ref[...], wa_ref[...]) + _dot(b_ref[...], wb_ref[...])


def out_proj(x, a, b, w, layer, *, tm, tn):
    t, d = x.shape
    ka, kb = a.shape[1], b.shape[1]
    assert ka == kb and w.shape[1] == ka + kb
    tm = min(tm, t)
    return pl.pallas_call(
        _out_proj_kernel,
        grid=(t // tm, d // tn),
        in_specs=[
            pl.BlockSpec((tm, tn), lambda i, j: (i, j)),
            pl.BlockSpec((tm, ka), lambda i, j: (i, 0)),
            pl.BlockSpec((tm, kb), lambda i, j: (i, 0)),
            pl.BlockSpec((None, ka, tn), lambda i, j: (layer, 0, j)),
            pl.BlockSpec((None, kb, tn), lambda i, j: (layer, 1, j)),
        ],
        out_specs=pl.BlockSpec((tm, tn), lambda i, j: (i, j)),
        out_shape=jax.ShapeDtypeStruct((t, d), F32),
        compiler_params=_params(("arbitrary", "arbitrary"), 48),
        name="out_proj",
    )(x, a, b, w, w)


def _s5_prep_kernel(lr_ref, li_ref, ldt_ref, lrx_ref, lix_ref, ldtx_ref, br_ref, bi_ref,
                    pwr_ref, pwi_ref, bbr_ref, bbi_ref):
    dt = jnp.exp(ldt_ref[...])
    mag = jnp.exp(lr_ref[...] * dt)
    ang = li_ref[...] * dt
    p_r, p_i = mag * jnp.cos(ang), mag * jnp.sin(ang)
    c_r, c_i = p_r, p_i
    pwr_ref[0], pwi_ref[0] = c_r, c_i
    for j in range(1, SUBLANES):
        c_r, c_i = c_r * p_r - c_i * p_i, c_r * p_i + c_i * p_r
        pwr_ref[j], pwi_ref[j] = c_r, c_i
    lr, li = lrx_ref[...], lix_ref[...]
    dtx = jnp.exp(ldtx_ref[...])
    magx = jnp.exp(lr * dtx)
    angx = li * dtx
    nr, ni = magx * jnp.cos(angx) - 1.0, magx * jnp.sin(angx)
    den = lr * lr + li * li
    qr = (nr * lr + ni * li) / den
    qi = (ni * lr - nr * li) / den
    br, bi = br_ref[...], bi_ref[...]
    bbr_ref[...] = qr * br - qi * bi
    bbi_ref[...] = qr * bi + qi * br


def s5_prep(lam_re, lam_im, log_dt, b_re, b_im):
    g, p = lam_re.shape
    h = b_re.shape[2]
    n_slab = g // SLAB_GROUPS
    slab = lambda a: a.reshape(n_slab, SLAB_GROUPS * p)
    rep = lambda a: jnp.repeat(a, h, axis=1)
    ldt_gp = jnp.broadcast_to(log_dt[:, None], (g, p))
    outs = pl.pallas_call(
        _s5_prep_kernel,
        out_shape=[
            jax.ShapeDtypeStruct((SUBLANES, n_slab, SLAB_GROUPS * p), F32),
            jax.ShapeDtypeStruct((SUBLANES, n_slab, SLAB_GROUPS * p), F32),
            jax.ShapeDtypeStruct((g, p * h), F32),
            jax.ShapeDtypeStruct((g, p * h), F32),
        ],
        name="s5_prep",
    )(slab(lam_re), slab(lam_im), slab(ldt_gp), rep(lam_re), rep(lam_im), rep(ldt_gp),
      b_re.reshape(g, p * h), b_im.reshape(g, p * h))
    pw_re, pw_im, bb_re, bb_im = outs
    pw_re = jnp.transpose(pw_re, (1, 0, 2))
    pw_im = jnp.transpose(pw_im, (1, 0, 2))
    return pw_re, pw_im, bb_re.reshape(g, p, h), bb_im.reshape(g, p, h)


def s5_block_matrices(bb_re, bb_im, c_re, c_im):
    g, p, h = bb_re.shape
    n_slab = g // SLAB_GROUPS
    eye = jnp.eye(SLAB_GROUPS, dtype=F32)

    def in_map(bb):
        t = bb.reshape(n_slab, SLAB_GROUPS, p, h)
        return jnp.einsum("kgph,gj->kghjp", t, eye).reshape(n_slab, SLAB_GROUPS * h, SLAB_GROUPS * p)

    def out_map(c):
        t = c.reshape(n_slab, SLAB_GROUPS, h, p)
        return jnp.einsum("kghp,gj->kgpjh", t, eye).reshape(n_slab, SLAB_GROUPS * p, SLAB_GROUPS * h)

    b_blk = jnp.concatenate([in_map(bb_re), in_map(bb_im)], axis=2)
    c_blk = jnp.concatenate([out_map(c_re), -out_map(c_im)], axis=1)
    return b_blk, c_blk


def _s5_scan_kernel(u_ref, bblk_ref, cblk_ref, pwr_ref, pwi_ref, d_ref, h0_ref,
                    y_ref, hl_ref, h_scr, bh_scr, bl_scr, ch_scr, *, seq, row_chunk, split_in):
    ns = SLAB_STATE
    bh, bl = _split_bf16(bblk_ref[...])
    bh_scr[...], bl_scr[...] = bh, bl
    ch_scr[...] = cblk_ref[...].astype(BF16)
    n_chunks = seq // row_chunk

    def proj_in(r, c):
        sl = pl.ds(pl.multiple_of(r * row_chunk, row_chunk), row_chunk)
        if split_in:
            uh, ul = _split_bf16(u_ref[sl, :])
            h_scr[sl, :] = _dot(uh, bh_scr[...]) + _dot(ul, bh_scr[...]) + _dot(uh, bl_scr[...])
        else:
            h_scr[sl, :] = _dot(u_ref[sl, :].astype(BF16), bh_scr[...])
        return c

    lax.fori_loop(0, n_chunks, proj_in, 0)

    rowid = lax.broadcasted_iota(jnp.int32, (SUBLANES, LANES), 0)
    for c in range(ns // LANES):
        re_l = slice(c * LANES, (c + 1) * LANES)
        im_l = slice(ns + c * LANES, ns + (c + 1) * LANES)
        p_r, p_i = pwr_ref[:, re_l], pwi_ref[:, re_l]
        steps = []
        for dist in (1, 2, 4):
            a_r = jnp.where(rowid >= dist, jnp.broadcast_to(p_r[dist - 1:dist], (SUBLANES, LANES)), 0.0)
            a_i = jnp.where(rowid >= dist, jnp.broadcast_to(p_i[dist - 1:dist], (SUBLANES, LANES)), 0.0)
            steps.append((dist, a_r, a_i))
        c_r = jnp.broadcast_to(h0_ref[:, re_l], (SUBLANES, LANES))
        c_i = jnp.broadcast_to(h0_ref[:, im_l], (SUBLANES, LANES))

        last = slice(SUBLANES - 1, SUBLANES)
        full = (SUBLANES, LANES)
        p8_r, p8_i = jnp.broadcast_to(p_r[last], full), jnp.broadcast_to(p_i[last], full)
        n_groups = seq // SUBLANES
        per_it = min(4, n_groups)

        def scan_rows(it, carry, re_l=re_l, im_l=im_l, p_r=p_r, p_i=p_i, p8_r=p8_r, p8_i=p8_i, steps=steps):
            base = pl.multiple_of(it * (per_it * SUBLANES), per_it * SUBLANES)
            sls = [pl.ds(base + j * SUBLANES, SUBLANES) for j in range(per_it)]
            loc = []
            for sl in sls:
                r, i = h_scr[sl, re_l], h_scr[sl, im_l]
                for dist, a_r, a_i in steps:
                    s_r, s_i = pltpu.roll(r, dist, 0), pltpu.roll(i, dist, 0)
                    r, i = r + (s_r * a_r - s_i * a_i), i + (s_r * a_i + s_i * a_r)
                loc.append((r, i))
            c_r, c_i = carry
            outs = []
            for r, i in loc:
                outs.append((r + (c_r * p_r - c_i * p_i), i + (c_r * p_i + c_i * p_r)))
                e_r, e_i = jnp.broadcast_to(r[last], full), jnp.broadcast_to(i[last], full)
                c_r, c_i = e_r + (c_r * p8_r - c_i * p8_i), e_i + (c_r * p8_i + c_i * p8_r)
            for sl, (r, i) in zip(sls, outs):
                h_scr[sl, re_l], h_scr[sl, im_l] = r, i
            return c_r, c_i

        c_r, c_i = lax.fori_loop(0, n_groups // per_it, scan_rows, (c_r, c_i))
        hl_ref[:, re_l] = c_r[0:1]
        hl_ref[:, im_l] = c_i[0:1]

    def proj_out(r, c):
        sl = pl.ds(pl.multiple_of(r * row_chunk, row_chunk), row_chunk)
        y = _dot(h_scr[sl, :].astype(BF16), ch_scr[...]) + d_ref[...] * u_ref[sl, :]
        y_ref[sl, :] = _gelu(y)
        return c

    lax.fori_loop(0, n_chunks, proj_out, 0)


def s5_scan(z, col0, b_blk, c_blk, pw_re, pw_im, d_skip, h0, *, split_in):
    bsz, seq, _ = z.shape
    n_slab = b_blk.shape[0]
    ns2 = 2 * SLAB_STATE
    row_chunk = min(seq, 256)
    kern = functools.partial(_s5_scan_kernel, seq=seq, row_chunk=row_chunk, split_in=split_in)
    return pl.pallas_call(
        kern,
        grid=(bsz, n_slab),
        in_specs=[
            pl.BlockSpec((None, seq, LANES), lambda b, k: (b, 0, col0 + k)),
            pl.BlockSpec((None, LANES, ns2), lambda b, k: (k, 0, 0)),
            pl.BlockSpec((None, ns2, LANES), lambda b, k: (k, 0, 0)),
            pl.BlockSpec((None, SUBLANES, SLAB_STATE), lambda b, k: (k, 0, 0)),
            pl.BlockSpec((None, SUBLANES, SLAB_STATE), lambda b, k: (k, 0, 0)),
            pl.BlockSpec((1, LANES), lambda b, k: (0, k)),
            pl.BlockSpec((None, None, 1, ns2), lambda b, k: (b, k, 0, 0)),
        ],
        out_specs=[
            pl.BlockSpec((None, seq, LANES), lambda b, k: (b, 0, k)),
            pl.BlockSpec((None, None, 1, ns2), lambda b, k: (b, k, 0, 0)),
        ],
        out_shape=[
            jax.ShapeDtypeStruct((bsz, seq, n_slab * LANES), F32),
            jax.ShapeDtypeStruct((bsz, n_slab, 1, ns2), F32),
        ],
        scratch_shapes=[
            pltpu.VMEM((seq, ns2), F32),
            pltpu.VMEM((LANES, ns2), BF16), pltpu.VMEM((LANES, ns2), BF16),
            pltpu.VMEM((ns2, LANES), BF16),
        ],
        compiler_params=_params(("arbitrary", "arbitrary"), 40),
        name="s5_scan",
    )(z, b_blk, c_blk, pw_re, pw_im, d_skip.reshape(1, -1), h0)


def _glu_kernel(yk_ref, yj_ref, w_ref, b_ref, o_ref, yb_scr):
    @pl.when(pl.program_id(1) == 0)
    def _():
        yb_scr[...] = yk_ref[...].astype(BF16)

    gate = _dot(yb_scr[...], w_ref[...]) + b_ref[...]
    o_ref[...] = (yj_ref[...] * _sigmoid(gate)).astype(o_ref.dtype)


def glu(y, w, b, layer, *, tm, tn):
    t, d = y.shape
    tm = min(tm, t)
    return pl.pallas_call(
        _glu_kernel,
        grid=(t // tm, d // tn),
        in_specs=[
            pl.BlockSpec((tm, d), lambda i, j: (i, 0)),
            pl.BlockSpec((tm, tn), lambda i, j: (i, j)),
            pl.BlockSpec((None, d, tn), lambda i, j: (layer, 0, j)),
            pl.BlockSpec((1, tn), lambda i, j: (0, j)),
        ],
        out_specs=pl.BlockSpec((tm, tn), lambda i, j: (i, j)),
        out_shape=jax.ShapeDtypeStruct((t, d), BF16),
        scratch_shapes=[pltpu.VMEM((tm, d), BF16)],
        compiler_params=_params(("arbitrary", "arbitrary"), 40),
        name="glu",
    )(y, y, w, b.reshape(1, d))


def _pool_kernel(u_ref, buf_ref, w_ref, s_ref, y_ref, tail_ref, ext_scr, *, tc, start_pos, cg):
    c = pl.program_id(1)

    @pl.when(c == 0)
    def _():
        ext_scr[0:POOL_HIST, :] = buf_ref[...]

    ext_scr[POOL_HIST:POOL_HIST + tc, :] = u_ref[...]
    pos = start_pos + c * tc + lax.broadcasted_iota(jnp.int32, (tc, 1), 0)
    for g, win in enumerate(POOL_WINDOWS):
        cols = slice(g * cg, (g + 1) * cg)
        x = ext_scr[:, cols]
        acc, dist = x, 1
        while dist < win:
            acc = acc + pltpu.roll(acc, dist, 0)
            dist *= 2
        wsum = acc[POOL_HIST:, :]
        cnt = jnp.minimum(pos + 1, win).astype(F32)
        zg = wsum * (1.0 / cnt) - x[POOL_HIST:, :]
        y = _dot(zg.astype(BF16), w_ref[g]) * s_ref[:, cols]
        y_ref[:, cols] = y.astype(y_ref.dtype)

    tail = ext_scr[tc:tc + POOL_HIST, :]
    ext_scr[0:POOL_HIST, :] = tail

    @pl.when(c == pl.num_programs(1) - 1)
    def _():
        tail_ref[...] = tail


def pool(z, colblk, buf16, w, layer, scale, *, start_pos, tc):
    bsz, seq, _ = z.shape
    _, n_g, cg, _ = w.shape
    db = n_g * cg
    tc = min(tc, seq)
    kern = functools.partial(_pool_kernel, tc=tc, start_pos=start_pos, cg=cg)
    return pl.pallas_call(
        kern,
        grid=(bsz, seq // tc),
        in_specs=[
            pl.BlockSpec((None, tc, db), lambda b, c: (b, c, colblk)),
            pl.BlockSpec((None, POOL_HIST, db), lambda b, c: (b, 0, 0)),
            pl.BlockSpec((None, n_g, cg, cg), lambda b, c: (layer, 0, 0, 0)),
            pl.BlockSpec((1, db), lambda b, c: (0, 0)),
        ],
        out_specs=[
            pl.BlockSpec((None, tc, db), lambda b, c: (b, c, 0)),
            pl.BlockSpec((None, POOL_HIST, db), lambda b, c: (b, 0, 0)),
        ],
        out_shape=[
            jax.ShapeDtypeStruct((bsz, seq, db), BF16),
            jax.ShapeDtypeStruct((bsz, POOL_HIST, db), F32),
        ],
        scratch_shapes=[pltpu.VMEM((POOL_HIST + tc, db), F32)],
        compiler_params=_params(("arbitrary", "arbitrary"), 40),
        name="pool",
    )(z, buf16, w, scale.reshape(1, db))


def _head_rms(x, g):
    ms = jnp.mean(x * x, axis=-1, keepdims=True)
    return (x * lax.rsqrt(ms + RMS_EPS)) * g


def _combine(os_, lses):
    m = jnp.maximum(jnp.maximum(lses[0], lses[1]), lses[2])
    ws = [jnp.exp(l - m) for l in lses]
    tot = ws[0] + ws[1] + ws[2]
    return (ws[0] * os_[0] + ws[1] * os_[1] + ws[2] * os_[2]) / tot


def _attn_prompt_kernel(q_ref, k_ref, v_ref, qn_ref, kn_ref, att_ref, ko_ref, vo_ref,
                        qs_scr, qd_scr, kd_scr, vd_scr, s_scr, p_scr, m_scr, o_scr, l_scr, *, seq, scale):
    blk = ATT_BLOCK
    rows = 256
    n_all = seq // blk

    def prep(r, c):
        sl = pl.ds(pl.multiple_of(r * rows, rows), rows)
        qs_scr[sl, :] = _head_rms(q_ref[sl, :], qn_ref[...]) * scale
        ko_ref[sl, :] = _head_rms(k_ref[sl, :], kn_ref[...])
        vo_ref[sl, :] = v_ref[sl, :]
        vd_scr[sl, LANES:] = jnp.ones((rows, LANES), BF16)
        return c

    lax.fori_loop(0, seq // rows, prep, 0)

    qi = lax.broadcasted_iota(jnp.int32, (blk, blk), 0)
    kj = lax.broadcasted_iota(jnp.int32, (blk, blk), 1)
    cur_ok = kj <= qi
    prev_ok = kj >= qi

    for g, (window, dil) in enumerate(BRANCHES):
        n_blk = seq // (dil * blk)
        col0 = 0 if n_blk > 1 else LANES

        def place(idx, dil=dil, n_blk=n_blk):
            res = idx // n_blk
            n = idx - res * n_blk
            start = res + n * (dil * blk)
            nat = pl.ds(start, blk, stride=dil) if dil > 1 else pl.ds(pl.multiple_of(start, blk), blk)
            cur = pl.ds(pl.multiple_of(idx * blk, blk), blk)
            prv = pl.ds(pl.multiple_of(jnp.maximum(idx - 1, 0) * blk, blk), blk)
            return nat, cur, prv, n

        def gather(idx, c, place=place):
            nat, cur, _, _ = place(idx)
            qd_scr[cur, :] = qs_scr[nat, :].astype(BF16)
            kd_scr[cur, :] = ko_ref[nat, :].astype(BF16)
            vd_scr[cur, 0:LANES] = vo_ref[nat, :].astype(BF16)
            return c

        lax.fori_loop(0, n_all, gather, 0, unroll=4)

        def scores(idx, c, place=place, n_blk=n_blk):
            _, cur, prv, n = place(idx)
            q = qd_scr[cur, :]
            s_scr[idx, :, LANES:] = jnp.where(cur_ok, _dot_nt(q, kd_scr[cur, :]), NEG_INF)
            if n_blk > 1:
                ok = jnp.logical_and(prev_ok, n > 0)
                s_scr[idx, :, 0:LANES] = jnp.where(ok, _dot_nt(q, kd_scr[prv, :]), NEG_INF)
            return c

        lax.fori_loop(0, n_all, scores, 0, unroll=8)

        def softmax(idx, c, col0=col0):
            s = s_scr[idx, :, col0:]
            m = jnp.max(s, axis=-1, keepdims=True)
            p_scr[idx, :, col0:] = jnp.exp(s - m).astype(BF16)
            m_scr[idx] = jnp.broadcast_to(m, (blk, LANES))
            return c

        lax.fori_loop(0, n_all, softmax, 0, unroll=4)

        def values(idx, c, g=g, place=place, n_blk=n_blk):
            nat, cur, prv, _ = place(idx)
            ov = _dot(p_scr[idx, :, LANES:], vd_scr[cur, :])
            if n_blk > 1:
                ov = ov + _dot(p_scr[idx, :, 0:LANES], vd_scr[prv, :])
            l = ov[:, LANES:]
            o_scr[g, nat, :] = ov[:, 0:LANES] / l
            l_scr[g, nat, :] = m_scr[idx] + jnp.log(l)
            return c

        lax.fori_loop(0, n_all, values, 0, unroll=8)

    def comb(r, c):
        sl = pl.ds(pl.multiple_of(r * rows, rows), rows)
        out = _combine([o_scr[g, sl, :] for g in range(3)], [l_scr[g, sl, :] for g in range(3)])
        att_ref[sl, :] = out.astype(att_ref.dtype)
        return c

    lax.fori_loop(0, seq // rows, comb, 0)


def attn_prompt(z, qn, kn, *, n_heads):
    bsz, seq, _ = z.shape
    assert seq % (BRANCHES[-1][1] * ATT_BLOCK) == 0
    hd = LANES
    kern = functools.partial(_attn_prompt_kernel, seq=seq, scale=hd ** -0.5)
    blk = lambda off: pl.BlockSpec((None, seq, hd), lambda b, h: (b, 0, off + h))
    return pl.pallas_call(
        kern,
        grid=(bsz, n_heads),
        in_specs=[blk(0), blk(n_heads), blk(2 * n_heads),
                  pl.BlockSpec((1, hd), lambda b, h: (0, 0)), pl.BlockSpec((1, hd), lambda b, h: (0, 0))],
        out_specs=[blk(0), blk(0), blk(0)],
        out_shape=[
            jax.ShapeDtypeStruct((bsz, seq, n_heads * hd), BF16),
            jax.ShapeDtypeStruct((bsz, seq, n_heads * hd), F32),
            jax.ShapeDtypeStruct((bsz, seq, n_heads * hd), F32),
        ],
        scratch_shapes=[
            pltpu.VMEM((seq, hd), F32),
            pltpu.VMEM((seq, hd), BF16), pltpu.VMEM((seq, hd), BF16), pltpu.VMEM((seq, 2 * hd), BF16),
            pltpu.VMEM((seq // ATT_BLOCK, ATT_BLOCK, 2 * ATT_BLOCK), F32),
            pltpu.VMEM((seq // ATT_BLOCK, ATT_BLOCK, 2 * ATT_BLOCK), BF16),
            pltpu.VMEM((seq // ATT_BLOCK, ATT_BLOCK, hd), F32),
            pltpu.VMEM((3, seq, hd), F32),
            pltpu.VMEM((3, seq, hd), F32),
        ],
        compiler_params=_params(("arbitrary", "arbitrary"), 40),
        name="attn_prompt",
    )(z, z, z, qn.reshape(1, hd), kn.reshape(1, hd))


def _attn_sample_kernel(q_ref, k_ref, v_ref, ck_ref, cv_ref, qn_ref, kn_ref, att_ref, ko_ref, vo_ref,
                        q_scr, kn_scr, vn_scr, *, s_new, n_buf, scale):
    pad = q_scr.shape[0]
    q_scr[...] = jnp.zeros_like(q_scr)
    kn_scr[...] = jnp.zeros_like(kn_scr)
    vn_scr[...] = jnp.zeros_like(vn_scr)
    k_new = _head_rms(k_ref[...], kn_ref[...])
    v_new = v_ref[...]
    ko_ref[...] = k_new
    vo_ref[...] = v_new
    q_scr[0:s_new, :] = _head_rms(q_ref[...], qn_ref[...])
    kn_scr[0:s_new, :] = k_new
    vn_scr[0:s_new, :] = v_new

    q = q_scr[...].astype(BF16)
    ck = ck_ref[...].astype(BF16)
    cv = cv_ref[...].astype(BF16)
    s_c = _dot_nt(q, ck) * scale
    s_n = _dot_nt(q, kn_scr[...].astype(BF16)) * scale
    qi_c = lax.broadcasted_iota(jnp.int32, (pad, n_buf), 0)
    kj_c = lax.broadcasted_iota(jnp.int32, (pad, n_buf), 1)
    dist_c = n_buf + qi_c - kj_c
    qi_n = lax.broadcasted_iota(jnp.int32, (pad, pad), 0)
    kj_n = lax.broadcasted_iota(jnp.int32, (pad, pad), 1)
    dist_n = qi_n - kj_n
    new_ok = jnp.logical_and(dist_n >= 0, kj_n < s_new)
    outs, lses = [], []
    for window, dil in BRANCHES:
        ok_c = jnp.logical_and((dist_c & (dil - 1)) == 0, dist_c <= window)
        ok_n = jnp.logical_and(new_ok, (dist_n & (dil - 1)) == 0)
        m_c = jnp.where(ok_c, s_c, NEG_INF)
        m_n = jnp.where(ok_n, s_n, NEG_INF)
        m = jnp.maximum(jnp.max(m_c, axis=-1, keepdims=True), jnp.max(m_n, axis=-1, keepdims=True))
        p_c, p_n = jnp.exp(m_c - m), jnp.exp(m_n - m)
        l = jnp.sum(p_c, axis=-1, keepdims=True) + jnp.sum(p_n, axis=-1, keepdims=True)
        o = _dot(p_c.astype(BF16), cv) + _dot(p_n.astype(BF16), vn_scr[...].astype(BF16))
        outs.append(o / l)
        lses.append(jnp.broadcast_to(m + jnp.log(l), (pad, LANES)))
    att_ref[...] = _combine(outs, lses)[0:s_new, :].astype(att_ref.dtype)


def attn_sample(z, cache_k, cache_v, row0, qn, kn, *, n_heads):
    bsz, s_new, _ = z.shape
    n_buf = cache_k.shape[1]
    assert n_buf >= BRANCHES[-1][0]
    hd = LANES
    pad = 16
    kern = functools.partial(_attn_sample_kernel, s_new=s_new, n_buf=n_buf, scale=hd ** -0.5)
    blk = lambda off: pl.BlockSpec((None, s_new, hd), lambda b, h: (b, 0, off + h))
    cblk = pl.BlockSpec((None, n_buf, hd), lambda b, h: (row0 + b, 0, h))
    vec = pl.BlockSpec((1, hd), lambda b, h: (0, 0))
    return pl.pallas_call(
        kern,
        grid=(bsz, n_heads),
        in_specs=[blk(0), blk(n_heads), blk(2 * n_heads), cblk, cblk, vec, vec],
        out_specs=[blk(0), blk(0), blk(0)],
        out_shape=[
            jax.ShapeDtypeStruct((bsz, s_new, n_heads * hd), BF16),
            jax.ShapeDtypeStruct((bsz, s_new, n_heads * hd), F32),
            jax.ShapeDtypeStruct((bsz, s_new, n_heads * hd), F32),
        ],
        scratch_shapes=[pltpu.VMEM((pad, hd), F32), pltpu.VMEM((pad, hd), F32), pltpu.VMEM((pad, hd), F32)],
        compiler_params=_params(("arbitrary", "arbitrary"), 40),
        name="attn_sample",
    )(z, z, z, cache_k, cache_v, qn.reshape(1, hd), kn.reshape(1, hd))


def _sgu_kernel(gu_ref, gv_ref, lg_ref, lb_ref, w_ref, bt_ref, o_ref, vn_ref, vb_scr, *, rows, n_g, cd):
    t = w_ref.shape[1]
    gv = _gelu(gv_ref[...])
    mu = jnp.mean(gv, axis=-1, keepdims=True)
    xc = gv - mu
    var = jnp.mean(xc * xc, axis=-1, keepdims=True)
    vn = (xc * lax.rsqrt(var + LN_EPS)) * lg_ref[...] + lb_ref[...]
    vn_ref[...] = vn
    if rows < t:
        vb_scr[...] = jnp.zeros_like(vb_scr)
    vb_scr[0:rows, :] = vn.astype(BF16)
    ri = lax.broadcasted_iota(jnp.int32, (t, t), 0)
    ci = lax.broadcasted_iota(jnp.int32, (t, t), 1)
    for g in range(n_g):
        cols = slice(g * cd, (g + 1) * cd)
        wg = jnp.where(ri >= ci, w_ref[g], 0.0).astype(BF16)
        mixed = _dot(wg, vb_scr[:, cols])[0:rows, :] + bt_ref[:, g:g + 1]
        o_ref[:, cols] = (_gelu(gu_ref[:, cols]) * mixed).astype(o_ref.dtype)


def sgu(z, colblk_u, ln_g, ln_b, w_s, b_s):
    bsz, seq, _ = z.shape
    n_g = w_s.shape[0]
    dd = ln_g.shape[0]
    cd = dd // n_g
    t = min(seq, CHUNK)
    tp = max(t, LANES)
    w = jnp.pad(w_s[:, :t, :t], ((0, 0), (0, tp - t), (0, tp - t)))
    bt = jnp.transpose(b_s[:, :t])
    kern = functools.partial(_sgu_kernel, rows=t, n_g=n_g, cd=cd)
    return pl.pallas_call(
        kern,
        grid=(bsz, seq // t),
        in_specs=[
            pl.BlockSpec((None, t, dd), lambda b, c: (b, c, colblk_u)),
            pl.BlockSpec((None, t, dd), lambda b, c: (b, c, colblk_u + 1)),
            pl.BlockSpec((1, dd), lambda b, c: (0, 0)),
            pl.BlockSpec((1, dd), lambda b, c: (0, 0)),
            pl.BlockSpec((n_g, tp, tp), lambda b, c: (0, 0, 0)),
            pl.BlockSpec((t, n_g), lambda b, c: (0, 0)),
        ],
        out_specs=[
            pl.BlockSpec((None, t, dd), lambda b, c: (b, c, 0)),
            pl.BlockSpec((None, t, dd), lambda b, c: (b, c, 0)),
        ],
        out_shape=[
            jax.ShapeDtypeStruct((bsz, seq, dd), BF16),
            jax.ShapeDtypeStruct((bsz, seq, dd), F32),
        ],
        scratch_shapes=[pltpu.VMEM((tp, dd), BF16)],
        compiler_params=_params(("arbitrary", "arbitrary"), 40),
        name="sgu",
    )(z, z, ln_g.reshape(1, dd), ln_b.reshape(1, dd), w, bt)


def _even_layer(x, bsz, seq, h0_re, h0_im, pool_buf, start_pos, norm_g, w_in, w_out, i, s5p, pool_w, pool_scale,
                w_glu, b_glu, d_skip, tiles):
    t, d = x.shape
    pw_re, pw_im, b_blk, c_blk = s5p
    n_slab = b_blk.shape[0]
    d_a = n_slab * LANES
    z = norm_matmul(x, norm_g, w_in, i, tm=tiles["tm"], tn=tiles["tn"]).reshape(bsz, seq, -1)
    h0 = jnp.concatenate([h0_re.reshape(bsz, n_slab, 1, SLAB_STATE), h0_im.reshape(bsz, n_slab, 1, SLAB_STATE)], axis=-1)
    y_pre, h_last = s5_scan(z, 0, b_blk, c_blk, pw_re, pw_im, d_skip, h0, split_in=tiles["s5_split"])
    ya = glu(y_pre.reshape(t, d_a), w_glu, b_glu, i, tm=tiles["tm_glu"], tn=tiles["tn"])
    buf16 = jnp.pad(pool_buf, ((0, 0), (POOL_HIST - pool_buf.shape[1], 0), (0, 0)))
    yb, tail = pool(z, 1, buf16, pool_w, i, pool_scale, start_pos=start_pos, tc=256)
    x = out_proj(x, ya, yb.reshape(t, -1), w_out, i, tm=tiles["tm_out"], tn=tiles["tn_out"])
    g_a = n_slab * SLAB_GROUPS
    h_re = h_last[..., :SLAB_STATE].reshape(bsz, g_a, S5_P)
    h_im = h_last[..., SLAB_STATE:].reshape(bsz, g_a, S5_P)
    return x, h_re, h_im, tail[:, POOL_HIST - pool_buf.shape[1]:]


def _odd_layer(x, bsz, seq, k_buf, v_buf, norm_g, w_in, w_out, i, qn, kn, ln_g, ln_b, w_s, b_s, n_heads, tiles):
    t, d = x.shape
    d_c = n_heads * LANES
    z = norm_matmul(x, norm_g, w_in, i, tm=tiles["tm"], tn=tiles["tn"]).reshape(bsz, seq, -1)
    if k_buf is None:
        att, k_new, v_new = attn_prompt(z, qn, kn, n_heads=n_heads)
    else:
        att, k_new, v_new = attn_sample(z, k_buf, v_buf, i * bsz, qn, kn, n_heads=n_heads)
    dd = ln_g.shape[0]
    sg, vn = sgu(z, (3 * d_c) // dd, ln_g, ln_b, w_s, b_s)
    x = out_proj(x, att.reshape(t, d_c), sg.reshape(t, dd), w_out, i, tm=tiles["tm_out"], tn=tiles["tn_out"])
    hd = LANES
    return x, k_new.reshape(bsz, seq, n_heads, hd), v_new.reshape(bsz, seq, n_heads, hd), vn


def kernel(x_prompt, x_sample, state_s5_re, state_s5_im, state_pool, cache_k, cache_v, norm_mix, norm_ffn, ev_w_in, ev_w_out, s5_lambda_re, s5_lambda_im, s5_log_dt, s5_b_re, s5_b_im, s5_c_re, s5_c_im, s5_d, s5_w_glu, s5_b_glu, pool_w, pool_scale, od_w_in, od_w_out, q_norm, k_norm, sgu_ln_g, sgu_ln_b, sgu_w, sgu_b, ffn_w1, ffn_w3, ffn_w2):
    bp, lp, d = x_prompt.shape
    bs, ls, _ = x_sample.shape
    depth = norm_mix.shape[0]
    n_heads = cache_k.shape[3]
    xp = x_prompt.reshape(bp * lp, d)
    xs = x_sample.reshape(bs * ls, d)
    bf = lambda a: a.astype(BF16)
    tiles_p = dict(tm=1024, tn=512, tm_glu=512, tm_out=512, tn_out=1024, s5_split=False)
    tiles_s = dict(tm=64, tn=512, tm_glu=64, tm_out=64, tn_out=1024, s5_split=True)
    g_a, p_a = s5_lambda_re.shape[1:]
    ev_w_in_b, ev_w_out_b, w_glu_b, pool_w_b = bf(ev_w_in), bf(ev_w_out), bf(s5_w_glu), bf(pool_w)
    od_w_in_b, od_w_out_b = bf(od_w_in), bf(od_w_out)
    d_ff = ffn_w1.shape[2]
    f_pad = -d_ff % FFN_TILE
    w1_b = jnp.pad(bf(ffn_w1), ((0, 0), (0, 0), (0, f_pad)))
    w3_b = jnp.pad(bf(ffn_w3), ((0, 0), (0, 0), (0, f_pad)))
    w2_b = jnp.pad(bf(ffn_w2), ((0, 0), (0, f_pad), (0, 0)))
    d_c = n_heads * LANES
    cache_k2 = cache_k.reshape(-1, cache_k.shape[2], d_c)
    cache_v2 = cache_v.reshape(-1, cache_v.shape[2], d_c)

    s5r_p, s5i_p, pool_p, k_p, v_p = [], [], [], [], []
    s5r_s, s5i_s, pool_s, k_s, v_s, sgu_s = [], [], [], [], [], []
    for l in range(depth):
        i = l // 2
        if l % 2 == 0:
            pw_re, pw_im, bb_re, bb_im = s5_prep(s5_lambda_re[i], s5_lambda_im[i], s5_log_dt[i], s5_b_re[i], s5_b_im[i])
            b_blk, c_blk = s5_block_matrices(bb_re, bb_im, s5_c_re[i], s5_c_im[i])
            s5p = (pw_re, pw_im, b_blk, c_blk)
            zero_h = jnp.zeros((bp, g_a, p_a), F32)
            zero_buf = jnp.zeros((bp, state_pool.shape[2], state_pool.shape[3]), F32)
            wargs = (ev_w_in_b, ev_w_out_b, i, s5p, pool_w_b, pool_scale[i], w_glu_b, s5_b_glu[i], s5_d[i])
            xp, hr, hi, buf = _even_layer(xp, bp, lp, zero_h, zero_h, zero_buf, 0, norm_mix[l], *wargs, tiles_p)
            s5r_p.append(hr); s5i_p.append(hi); pool_p.append(buf)
            xs, hr, hi, buf = _even_layer(xs, bs, ls, state_s5_re[i], state_s5_im[i], state_pool[i], PAST_LEN,
                                          norm_mix[l], *wargs, tiles_s)
            s5r_s.append(hr); s5i_s.append(hi); pool_s.append(buf)
        else:
            args = (norm_mix[l], od_w_in_b, od_w_out_b, i, q_norm[i], k_norm[i], sgu_ln_g[i], sgu_ln_b[i], sgu_w[i],
                    sgu_b[i], n_heads)
            xp, nk, nv, _ = _odd_layer(xp, bp, lp, None, None, *args, tiles_p)
            k_p.append(nk); v_p.append(nv)
            xs, nk, nv, vrows = _odd_layer(xs, bs, ls, cache_k2, cache_v2, *args, tiles_s)
            k_s.append(nk); v_s.append(nv); sgu_s.append(vrows)
        xp = ffn(xp, norm_ffn[l], w1_b, w3_b, w2_b, l, tm=512, tf=FFN_TILE)
        xs = ffn(xs, norm_ffn[l], w1_b, w3_b, w2_b, l, tm=64, tf=FFN_TILE)
    return (xp.reshape(bp, lp, d), xs.reshape(bs, ls, d),
            jnp.stack(s5r_p), jnp.stack(s5i_p), jnp.stack(pool_p), jnp.stack(k_p), jnp.stack(v_p),
            jnp.stack(s5r_s), jnp.stack(s5i_s), jnp.stack(pool_s), jnp.stack(k_s), jnp.stack(v_s),
            jnp.stack(sgu_s))
```

```python
import functools
import math

import jax
import jax.numpy as jnp
from jax import lax
from jax.experimental import pallas as pl
from jax.experimental.pallas import tpu as pltpu

F32 = jnp.float32
BF16 = jnp.bfloat16

RMS_EPS = 1e-6
LN_EPS = 1e-5
NEG_INF = -1e30

LANES = 128
SUBLANES = 8
ATT_BLOCK = 128
CHUNK = 128
POOL_WINDOWS = (2, 4, 8, 16)
POOL_HIST = 16
BRANCHES = ((128, 1), (512, 4), (2048, 16))
S5_GRP = 16
S5_P = 64
SLAB_GROUPS = LANES // S5_GRP
SLAB_STATE = SLAB_GROUPS * S5_P
PAST_LEN = 8192
FFN_TILE = 256


def _params(sem, vmem_mib):
    return pltpu.CompilerParams(dimension_semantics=sem, vmem_limit_bytes=vmem_mib << 20)


def _gelu(x):
    return 0.5 * x * (1.0 + lax.erf(x * (1.0 / math.sqrt(2.0))))


def _sigmoid(x):
    return 1.0 / (1.0 + jnp.exp(-x))


def _split_bf16(a):
    hi = a.astype(BF16)
    lo = (a - hi.astype(F32)).astype(BF16)
    return hi, lo


def _dot(a, b):
    return jnp.dot(a, b, preferred_element_type=F32)


def _dot_nt(a, b):
    return lax.dot_general(a, b, (((1,), (1,)), ((), ())), preferred_element_type=F32)


def _rms_rows_to(x_ref, g_ref, h_ref, rows):
    step = 16 if rows % 16 == 0 else rows

    def body(r, c):
        sl = pl.ds(pl.multiple_of(r * step, step), step)
        x = x_ref[sl, :]
        ms = jnp.mean(x * x, axis=-1, keepdims=True)
        h_ref[sl, :] = ((x * lax.rsqrt(ms + RMS_EPS)) * g_ref[...]).astype(h_ref.dtype)
        return c

    lax.fori_loop(0, rows // step, body, 0)


def _norm_matmul_kernel(x_ref, g_ref, w_ref, o_ref, h_ref):
    @pl.when(pl.program_id(1) == 0)
    def _():
        _rms_rows_to(x_ref, g_ref, h_ref, x_ref.shape[0])

    o_ref[...] = _dot(h_ref[...], w_ref[...]).astype(o_ref.dtype)


def norm_matmul(x, g, w, layer, *, tm, tn, out_dtype=F32):
    t, d = x.shape
    n = w.shape[2]
    tm = min(tm, t)
    return pl.pallas_call(
        _norm_matmul_kernel,
        grid=(t // tm, n // tn),
        in_specs=[
            pl.BlockSpec((tm, d), lambda i, j: (i, 0)),
            pl.BlockSpec((1, d), lambda i, j: (0, 0)),
            pl.BlockSpec((None, d, tn), lambda i, j: (layer, 0, j)),
        ],
        out_specs=pl.BlockSpec((tm, tn), lambda i, j: (i, j)),
        out_shape=jax.ShapeDtypeStruct((t, n), out_dtype),
        scratch_shapes=[pltpu.VMEM((tm, d), BF16)],
        compiler_params=_params(("arbitrary", "arbitrary"), 58),
        name="norm_matmul",
    )(x, g.reshape(1, d), w)


FFN_OUT_CHUNK = 512


def _ffn_step(first, x_ref, g_ref, w1, w3, w2_ref, o_ref, h_ref):
    @pl.when(first)
    def _():
        _rms_rows_to(x_ref, g_ref, h_ref, x_ref.shape[0])
        o_ref[...] = x_ref[...]

    h = h_ref[...]
    a = _dot(h, w1)
    b = _dot(h, w3)
    u = ((a * _sigmoid(a)) * b).astype(BF16)
    for c in range(0, o_ref.shape[1], FFN_OUT_CHUNK):
        o_ref[:, c:c + FFN_OUT_CHUNK] += _dot(u, w2_ref[:, c:c + FFN_OUT_CHUNK])


def _ffn_kernel(x_ref, g_ref, w1_ref, w3_ref, w2_ref, o_ref, h_ref):
    _ffn_step(pl.program_id(1) == 0, x_ref, g_ref, w1_ref[...], w3_ref[...], w2_ref, o_ref, h_ref)


def ffn(x, g, w1, w3, w2, *, tm, tf):
    t, d = x.shape
    f = w1.shape[1]
    return pl.pallas_call(
        _ffn_kernel,
        grid=(t // tm, f // tf),
        in_specs=[
            pl.BlockSpec((tm, d), lambda i, j: (i, 0)),
            pl.BlockSpec((1, d), lambda i, j: (0, 0)),
            pl.BlockSpec((d, tf), lambda i, j: (0, j)),
            pl.BlockSpec((d, tf), lambda i, j: (0, j)),
            pl.BlockSpec((tf, d), lambda i, j: (j, 0)),
        ],
        out_specs=pl.BlockSpec((tm, d), lambda i, j: (i, 0)),
        out_shape=jax.ShapeDtypeStruct((t, d), F32),
        scratch_shapes=[pltpu.VMEM((tm, d), BF16)],
        compiler_params=_params(("arbitrary", "arbitrary"), 56),
        name="ffn",
    )(x, g.reshape(1, d), w1, w3, w2)


def _ffn_cast_kernel(x_ref, g_ref, w1_ref, w3_ref, w2_ref, o_ref, w1b_ref, w3b_ref, w2b_ref, h_ref):
    w1b_ref[...] = w1_ref[...].astype(BF16)
    w3b_ref[...] = w3_ref[...].astype(BF16)
    w2b_ref[...] = w2_ref[...].astype(BF16)
    _ffn_step(pl.program_id(0) == 0, x_ref, g_ref, w1b_ref[...], w3b_ref[...], w2b_ref, o_ref, h_ref)


def ffn_cast(x, g, w1, w3, w2, layer, *, tf):
    t, d = x.shape
    f = w1.shape[2]
    return pl.pallas_call(
        _ffn_cast_kernel,
        grid=(f // tf,),
        in_specs=[
            pl.BlockSpec((t, d), lambda j: (0, 0)),
            pl.BlockSpec((1, d), lambda j: (0, 0)),
            pl.BlockSpec((None, d, tf), lambda j: (layer, 0, j)),
            pl.BlockSpec((None, d, tf), lambda j: (layer, 0, j)),
            pl.BlockSpec((None, tf, d), lambda j: (layer, j, 0)),
        ],
        out_specs=[
            pl.BlockSpec((t, d), lambda j: (0, 0)),
            pl.BlockSpec((d, tf), lambda j: (0, j)),
            pl.BlockSpec((d, tf), lambda j: (0, j)),
            pl.BlockSpec((tf, d), lambda j: (j, 0)),
        ],
        out_shape=[
            jax.ShapeDtypeStruct((t, d), F32),
            jax.ShapeDtypeStruct((d, f), BF16),
            jax.ShapeDtypeStruct((d, f), BF16),
            jax.ShapeDtypeStruct((f, d), BF16),
        ],
        scratch_shapes=[pltpu.VMEM((t, d), BF16)],
        compiler_params=_params(("arbitrary",), 48),
        name="ffn_cast",
    )(x, g.reshape(1, d), w1, w3, w2)


def _out_proj_kernel(x_ref, a_ref, b_ref, wa_ref, wb_ref, o_ref):
    o_ref[...] = x_ref[...] + _dot(a_ref[...], wa_ref[...]) + _dot(b_ref[...], wb_ref[...])


def out_proj(x, a, b, w, layer, *, tm, tn):
    t, d = x.shape
    ka, kb = a.shape[1], b.shape[1]
    assert ka == kb and w.shape[1] == ka + kb
    tm = min(tm, t)
    return pl.pallas_call(
        _out_proj_kernel,
        grid=(t // tm, d // tn),
        in_specs=[
            pl.BlockSpec((tm, tn), lambda i, j: (i, j)),
            pl.BlockSpec((tm, ka), lambda i, j: (i, 0)),
            pl.BlockSpec((tm, kb), lambda i, j: (i, 0)),
            pl.BlockSpec((None, ka, tn), lambda i, j: (layer, 0, j)),
            pl.BlockSpec((None, kb, tn), lambda i, j: (layer, 1, j)),
        ],
        out_specs=pl.BlockSpec((tm, tn), lambda i, j: (i, j)),
        out_shape=jax.ShapeDtypeStruct((t, d), F32),
        compiler_params=_params(("arbitrary", "arbitrary"), 48),
        name="out_proj",
    )(x, a, b, w, w)


def _s5_prep_kernel(lr_ref, li_ref, ldt_ref, lrx_ref, lix_ref, ldtx_ref, br_ref, bi_ref,
                    pwr_ref, pwi_ref, bbr_ref, bbi_ref):
    dt = jnp.exp(ldt_ref[...])
    mag = jnp.exp(lr_ref[...] * dt)
    ang = li_ref[...] * dt
    p_r, p_i = mag * jnp.cos(ang), mag * jnp.sin(ang)
    c_r, c_i = p_r, p_i
    pwr_ref[0], pwi_ref[0] = c_r, c_i
    for j in range(1, SUBLANES):
        c_r, c_i = c_r * p_r - c_i * p_i, c_r * p_i + c_i * p_r
        pwr_ref[j], pwi_ref[j] = c_r, c_i
    lr, li = lrx_ref[...], lix_ref[...]
    dtx = jnp.exp(ldtx_ref[...])
    magx = jnp.exp(lr * dtx)
    angx = li * dtx
    nr, ni = magx * jnp.cos(angx) - 1.0, magx * jnp.sin(angx)
    den = lr * lr + li * li
    qr = (nr * lr + ni * li) / den
    qi = (ni * lr - nr * li) / den
    br, bi = br_ref[...], bi_ref[...]
    bbr_ref[...] = qr * br - qi * bi
    bbi_ref[...] = qr * bi + qi * br


def s5_prep(lam_re, lam_im, log_dt, b_re, b_im):
    g, p = lam_re.shape
    h = b_re.shape[2]
    n_slab = g // SLAB_GROUPS
    slab = lambda a: a.reshape(n_slab, SLAB_GROUPS * p)
    rep = lambda a: jnp.repeat(a, h, axis=1)
    ldt_gp = jnp.broadcast_to(log_dt[:, None], (g, p))
    outs = pl.pallas_call(
        _s5_prep_kernel,
        out_shape=[
            jax.ShapeDtypeStruct((SUBLANES, n_slab, SLAB_GROUPS * p), F32),
            jax.ShapeDtypeStruct((SUBLANES, n_slab, SLAB_GROUPS * p), F32),
            jax.ShapeDtypeStruct((g, p * h), F32),
            jax.ShapeDtypeStruct((g, p * h), F32),
        ],
        name="s5_prep",
    )(slab(lam_re), slab(lam_im), slab(ldt_gp), rep(lam_re), rep(lam_im), rep(ldt_gp),
      b_re.reshape(g, p * h), b_im.reshape(g, p * h))
    pw_re, pw_im, bb_re, bb_im = outs
    pw_re = jnp.transpose(pw_re, (1, 0, 2))
    pw_im = jnp.transpose(pw_im, (1, 0, 2))
    return pw_re, pw_im, bb_re.reshape(g, p, h), bb_im.reshape(g, p, h)


def s5_block_matrices(bb_re, bb_im, c_re, c_im):
    g, p, h = bb_re.shape
    n_slab = g // SLAB_GROUPS
    eye = jnp.eye(SLAB_GROUPS, dtype=F32)

    def in_map(bb):
        t = bb.reshape(n_slab, SLAB_GROUPS, p, h)
        return jnp.einsum("kgph,gj->kghjp", t, eye).reshape(n_slab, SLAB_GROUPS * h, SLAB_GROUPS * p)

    def out_map(c):
        t = c.reshape(n_slab, SLAB_GROUPS, h, p)
        return jnp.einsum("kghp,gj->kgpjh", t, eye).reshape(n_slab, SLAB_GROUPS * p, SLAB_GROUPS * h)

    b_blk = jnp.concatenate([in_map(bb_re), in_map(bb_im)], axis=2)
    c_blk = jnp.concatenate([out_map(c_re), -out_map(c_im)], axis=1)
    return b_blk, c_blk


def _s5_scan_kernel(u_ref, bblk_ref, cblk_ref, pwr_ref, pwi_ref, d_ref, h0_ref,
                    y_ref, hl_ref, h_scr, bh_scr, bl_scr, ch_scr, *, seq, row_chunk, split_in):
    ns = SLAB_STATE

    @pl.when(pl.program_id(1) == 0)
    def _():
        bh, bl = _split_bf16(bblk_ref[...])
        bh_scr[...], bl_scr[...] = bh, bl
        ch_scr[...] = cblk_ref[...].astype(BF16)

    n_chunks = seq // row_chunk

    def proj_in(r, c):
        sl = pl.ds(pl.multiple_of(r * row_chunk, row_chunk), row_chunk)
        if split_in:
            uh, ul = _split_bf16(u_ref[sl, :].astype(F32))
            h_scr[sl, :] = _dot(uh, bh_scr[...]) + _dot(ul, bh_scr[...]) + _dot(uh, bl_scr[...])
        else:
            h_scr[sl, :] = _dot(u_ref[sl, :].astype(BF16), bh_scr[...])
        return c

    lax.fori_loop(0, n_chunks, proj_in, 0)

    rowid = lax.broadcasted_iota(jnp.int32, (SUBLANES, LANES), 0)
    for c in range(ns // LANES):
        re_l = slice(c * LANES, (c + 1) * LANES)
        im_l = slice(ns + c * LANES, ns + (c + 1) * LANES)
        p_r, p_i = pwr_ref[:, re_l], pwi_ref[:, re_l]
        steps = []
        for dist in (1, 2, 4):
            a_r = jnp.where(rowid >= dist, jnp.broadcast_to(p_r[dist - 1:dist], (SUBLANES, LANES)), 0.0)
            a_i = jnp.where(rowid >= dist, jnp.broadcast_to(p_i[dist - 1:dist], (SUBLANES, LANES)), 0.0)
            steps.append((dist, a_r, a_i))
        c_r = jnp.broadcast_to(h0_ref[:, re_l], (SUBLANES, LANES))
        c_i = jnp.broadcast_to(h0_ref[:, im_l], (SUBLANES, LANES))

        last = slice(SUBLANES - 1, SUBLANES)
        full = (SUBLANES, LANES)
        p8_r, p8_i = jnp.broadcast_to(p_r[last], full), jnp.broadcast_to(p_i[last], full)
        n_groups = seq // SUBLANES
        per_it = min(4, n_groups)

        def scan_rows(it, carry, re_l=re_l, im_l=im_l, p_r=p_r, p_i=p_i, p8_r=p8_r, p8_i=p8_i, steps=steps):
            base = pl.multiple_of(it * (per_it * SUBLANES), per_it * SUBLANES)
            sls = [pl.ds(base + j * SUBLANES, SUBLANES) for j in range(per_it)]
            loc = []
            for sl in sls:
                r, i = h_scr[sl, re_l], h_scr[sl, im_l]
                for dist, a_r, a_i in steps:
                    s_r, s_i = pltpu.roll(r, dist, 0), pltpu.roll(i, dist, 0)
                    r, i = r + (s_r * a_r - s_i * a_i), i + (s_r * a_i + s_i * a_r)
                loc.append((r, i))
            c_r, c_i = carry
            outs = []
            for r, i in loc:
                outs.append((r + (c_r * p_r - c_i * p_i), i + (c_r * p_i + c_i * p_r)))
                e_r, e_i = jnp.broadcast_to(r[last], full), jnp.broadcast_to(i[last], full)
                c_r, c_i = e_r + (c_r * p8_r - c_i * p8_i), e_i + (c_r * p8_i + c_i * p8_r)
            for sl, (r, i) in zip(sls, outs):
                h_scr[sl, re_l], h_scr[sl, im_l] = r, i
            return c_r, c_i

        c_r, c_i = lax.fori_loop(0, n_groups // per_it, scan_rows, (c_r, c_i))
        hl_ref[:, re_l] = c_r[0:1]
        hl_ref[:, im_l] = c_i[0:1]

    def proj_out(r, c):
        sl = pl.ds(pl.multiple_of(r * row_chunk, row_chunk), row_chunk)
        y = _dot(h_scr[sl, :].astype(BF16), ch_scr[...]) + d_ref[...] * u_ref[sl, :].astype(F32)
        y_ref[sl, :] = _gelu(y)
        return c

    lax.fori_loop(0, n_chunks, proj_out, 0)


def s5_scan(z, col0, b_blk, c_blk, pw_re, pw_im, d_skip, h0, *, split_in):
    bsz, seq, _ = z.shape
    n_slab = b_blk.shape[0]
    ns2 = 2 * SLAB_STATE
    row_chunk = min(seq, 256)
    kern = functools.partial(_s5_scan_kernel, seq=seq, row_chunk=row_chunk, split_in=split_in)
    return pl.pallas_call(
        kern,
        grid=(n_slab, bsz),
        in_specs=[
            pl.BlockSpec((None, seq, LANES), lambda k, b: (b, 0, col0 + k)),
            pl.BlockSpec((None, LANES, ns2), lambda k, b: (k, 0, 0)),
            pl.BlockSpec((None, ns2, LANES), lambda k, b: (k, 0, 0)),
            pl.BlockSpec((None, SUBLANES, SLAB_STATE), lambda k, b: (k, 0, 0)),
            pl.BlockSpec((None, SUBLANES, SLAB_STATE), lambda k, b: (k, 0, 0)),
            pl.BlockSpec((1, LANES), lambda k, b: (0, k)),
            pl.BlockSpec((None, None, 1, ns2), lambda k, b: (b, k, 0, 0)),
        ],
        out_specs=[
            pl.BlockSpec((None, seq, LANES), lambda k, b: (b, 0, k)),
            pl.BlockSpec((None, None, 1, ns2), lambda k, b: (b, k, 0, 0)),
        ],
        out_shape=[
            jax.ShapeDtypeStruct((bsz, seq, n_slab * LANES), F32),
            jax.ShapeDtypeStruct((bsz, n_slab, 1, ns2), F32),
        ],
        scratch_shapes=[
            pltpu.VMEM((seq, ns2), F32),
            pltpu.VMEM((LANES, ns2), BF16), pltpu.VMEM((LANES, ns2), BF16),
            pltpu.VMEM((ns2, LANES), BF16),
        ],
        compiler_params=_params(("arbitrary", "arbitrary"), 40),
        name="s5_scan",
    )(z, b_blk, c_blk, pw_re, pw_im, d_skip.reshape(1, -1), h0)


def _glu_kernel(yk_ref, yj_ref, w_ref, b_ref, o_ref, yb_scr):
    @pl.when(pl.program_id(1) == 0)
    def _():
        yb_scr[...] = yk_ref[...].astype(BF16)

    gate = _dot(yb_scr[...], w_ref[...]) + b_ref[...]
    o_ref[...] = (yj_ref[...] * _sigmoid(gate)).astype(o_ref.dtype)


def glu(y, w, b, layer, *, tm, tn):
    t, d = y.shape
    tm = min(tm, t)
    return pl.pallas_call(
        _glu_kernel,
        grid=(t // tm, d // tn),
        in_specs=[
            pl.BlockSpec((tm, d), lambda i, j: (i, 0)),
            pl.BlockSpec((tm, tn), lambda i, j: (i, j)),
            pl.BlockSpec((None, d, tn), lambda i, j: (layer, 0, j)),
            pl.BlockSpec((1, tn), lambda i, j: (0, j)),
        ],
        out_specs=pl.BlockSpec((tm, tn), lambda i, j: (i, j)),
        out_shape=jax.ShapeDtypeStruct((t, d), BF16),
        scratch_shapes=[pltpu.VMEM((tm, d), BF16)],
        compiler_params=_params(("arbitrary", "arbitrary"), 40),
        name="glu",
    )(y, y, w, b.reshape(1, d))


def _pool_kernel(u_ref, buf_ref, w_ref, s_ref, y_ref, tail_ref, ext_scr, *, tc, start_pos, cg):
    c = pl.program_id(1)

    @pl.when(c == 0)
    def _():
        ext_scr[0:POOL_HIST, :] = buf_ref[...]

    ext_scr[POOL_HIST:POOL_HIST + tc, :] = u_ref[...].astype(F32)
    pos = start_pos + c * tc + lax.broadcasted_iota(jnp.int32, (tc, 1), 0)
    for g, win in enumerate(POOL_WINDOWS):
        cols = slice(g * cg, (g + 1) * cg)
        x = ext_scr[:, cols]
        acc, dist = x, 1
        while dist < win:
            acc = acc + pltpu.roll(acc, dist, 0)
            dist *= 2
        wsum = acc[POOL_HIST:, :]
        cnt = jnp.minimum(pos + 1, win).astype(F32)
        zg = wsum * (1.0 / cnt) - x[POOL_HIST:, :]
        y = _dot(zg.astype(BF16), w_ref[g]) * s_ref[:, cols]
        y_ref[:, cols] = y.astype(y_ref.dtype)

    tail = ext_scr[tc:tc + POOL_HIST, :]
    ext_scr[0:POOL_HIST, :] = tail

    @pl.when(c == pl.num_programs(1) - 1)
    def _():
        tail_ref[...] = tail


def pool(z, colblk, buf16, w, layer, scale, *, start_pos, tc):
    bsz, seq, _ = z.shape
    _, n_g, cg, _ = w.shape
    db = n_g * cg
    tc = min(tc, seq)
    kern = functools.partial(_pool_kernel, tc=tc, start_pos=start_pos, cg=cg)
    return pl.pallas_call(
        kern,
        grid=(bsz, seq // tc),
        in_specs=[
            pl.BlockSpec((None, tc, db), lambda b, c: (b, c, colblk)),
            pl.BlockSpec((None, POOL_HIST, db), lambda b, c: (b, 0, 0)),
            pl.BlockSpec((None, n_g, cg, cg), lambda b, c: (layer, 0, 0, 0)),
            pl.BlockSpec((1, db), lambda b, c: (0, 0)),
        ],
        out_specs=[
            pl.BlockSpec((None, tc, db), lambda b, c: (b, c, 0)),
            pl.BlockSpec((None, POOL_HIST, db), lambda b, c: (b, 0, 0)),
        ],
        out_shape=[
            jax.ShapeDtypeStruct((bsz, seq, db), BF16),
            jax.ShapeDtypeStruct((bsz, POOL_HIST, db), F32),
        ],
        scratch_shapes=[pltpu.VMEM((POOL_HIST + tc, db), F32)],
        compiler_params=_params(("arbitrary", "arbitrary"), 40),
        name="pool",
    )(z, buf16, w, scale.reshape(1, db))


def _head_rms(x, g):
    ms = jnp.mean(x * x, axis=-1, keepdims=True)
    return (x * lax.rsqrt(ms + RMS_EPS)) * g


def _combine(os_, lses):
    m = jnp.maximum(jnp.maximum(lses[0], lses[1]), lses[2])
    ws = [jnp.exp(l - m) for l in lses]
    tot = ws[0] + ws[1] + ws[2]
    return (ws[0] * os_[0] + ws[1] * os_[1] + ws[2] * os_[2]) / tot


def _attn_prompt_kernel(q_ref, k_ref, v_ref, qn_ref, kn_ref, att_ref, ko_ref, vo_ref,
                        qs_scr, qf_scr, kf_scr, vf_scr, qd_scr, kd_scr, vd_scr, s_scr, p_scr, m_scr, o_scr, l_scr,
                        *, seq, scale):
    blk = ATT_BLOCK
    rows = 256
    n_all = seq // blk

    def prep(r, c):
        sl = pl.ds(pl.multiple_of(r * rows, rows), rows)
        qs_scr[sl, :] = _head_rms(q_ref[sl, :].astype(F32), qn_ref[...]) * scale
        ko_ref[sl, :] = _head_rms(k_ref[sl, :].astype(F32), kn_ref[...])
        vo_ref[sl, :] = v_ref[sl, :].astype(F32)
        vd_scr[sl, LANES:] = jnp.ones((rows, LANES), BF16)
        return c

    lax.fori_loop(0, seq // rows, prep, 0)

    qi = lax.broadcasted_iota(jnp.int32, (blk, blk), 0)
    kj = lax.broadcasted_iota(jnp.int32, (blk, blk), 1)
    cur_ok = kj <= qi
    prev_ok = kj >= qi

    for g, (window, dil) in enumerate(BRANCHES):
        n_blk = seq // (dil * blk)
        col0 = 0 if n_blk > 1 else LANES

        def place(idx, dil=dil, n_blk=n_blk):
            res = idx // n_blk
            n = idx - res * n_blk
            start = res + n * (dil * blk)
            nat = pl.ds(start, blk, stride=dil) if dil > 1 else pl.ds(pl.multiple_of(start, blk), blk)
            cur = pl.ds(pl.multiple_of(idx * blk, blk), blk)
            prv = pl.ds(pl.multiple_of(jnp.maximum(idx - 1, 0) * blk, blk), blk)
            return nat, cur, prv, n

        keep_f32 = dil == BRANCHES[1][1]
        two_level = g == 2 and dil == BRANCHES[1][1] ** 2

        def gather(idx, c, place=place, dil=dil, n_blk=n_blk, keep_f32=keep_f32, two_level=two_level):
            nat, cur, _, n = place(idx)
            if two_level:
                mid = BRANCHES[1][1]
                res = idx // n_blk
                start = (res % mid) * (seq // mid) + res // mid + n * (mid * blk)
                src = pl.ds(start, blk, stride=mid)
                q, k, v = qf_scr[src, :], kf_scr[src, :], vf_scr[src, :]
            else:
                q, k, v = qs_scr[nat, :], ko_ref[nat, :], vo_ref[nat, :]
            if keep_f32:
                qf_scr[cur, :], kf_scr[cur, :], vf_scr[cur, :] = q, k, v
            qd_scr[cur, :] = q.astype(BF16)
            kd_scr[cur, :] = k.astype(BF16)
            vd_scr[cur, 0:LANES] = v.astype(BF16)
            return c

        lax.fori_loop(0, n_all, gather, 0, unroll=4)

        def scores(idx, c, place=place, n_blk=n_blk):
            _, cur, prv, n = place(idx)
            q = qd_scr[cur, :]
            s_scr[idx, :, LANES:] = jnp.where(cur_ok, _dot_nt(q, kd_scr[cur, :]), NEG_INF)
            if n_blk > 1:
                ok = jnp.logical_and(prev_ok, n > 0)
                s_scr[idx, :, 0:LANES] = jnp.where(ok, _dot_nt(q, kd_scr[prv, :]), NEG_INF)
            return c

        lax.fori_loop(0, n_all, scores, 0, unroll=8)

        def softmax(idx, c, col0=col0):
            s = s_scr[idx, :, col0:]
            m = jnp.max(s, axis=-1, keepdims=True)
            p_scr[idx, :, col0:] = jnp.exp(s - m).astype(BF16)
            m_scr[idx] = jnp.broadcast_to(m, (blk, LANES))
            return c

        lax.fori_loop(0, n_all, softmax, 0, unroll=4)

        def values(idx, c, g=g, place=place, n_blk=n_blk):
            nat, cur, prv, _ = place(idx)
            ov = _dot(p_scr[idx, :, LANES:], vd_scr[cur, :])
            if n_blk > 1:
                ov = ov + _dot(p_scr[idx, :, 0:LANES], vd_scr[prv, :])
            l = ov[:, LANES:]
            o_scr[g, nat, :] = ov[:, 0:LANES] / l
            l_scr[g, nat, :] = m_scr[idx] + jnp.log(l)
            return c

        lax.fori_loop(0, n_all, values, 0, unroll=8)

    def comb(r, c):
        sl = pl.ds(pl.multiple_of(r * rows, rows), rows)
        out = _combine([o_scr[g, sl, :] for g in range(3)], [l_scr[g, sl, :] for g in range(3)])
        att_ref[sl, :] = out.astype(att_ref.dtype)
        return c

    lax.fori_loop(0, seq // rows, comb, 0)


def attn_prompt(z, qn, kn, *, n_heads):
    bsz, seq, _ = z.shape
    assert seq % (BRANCHES[-1][1] * ATT_BLOCK) == 0
    hd = LANES
    kern = functools.partial(_attn_prompt_kernel, seq=seq, scale=hd ** -0.5)
    blk = lambda off: pl.BlockSpec((None, seq, hd), lambda b, h: (b, 0, off + h))
    return pl.pallas_call(
        kern,
        grid=(bsz, n_heads),
        in_specs=[blk(0), blk(n_heads), blk(2 * n_heads),
                  pl.BlockSpec((1, hd), lambda b, h: (0, 0)), pl.BlockSpec((1, hd), lambda b, h: (0, 0))],
        out_specs=[blk(0), blk(0), blk(0)],
        out_shape=[
            jax.ShapeDtypeStruct((bsz, seq, n_heads * hd), BF16),
            jax.ShapeDtypeStruct((bsz, seq, n_heads * hd), F32),
            jax.ShapeDtypeStruct((bsz, seq, n_heads * hd), F32),
        ],
        scratch_shapes=[
            pltpu.VMEM((seq, hd), F32),
            pltpu.VMEM((seq, hd), F32), pltpu.VMEM((seq, hd), F32), pltpu.VMEM((seq, hd), F32),
            pltpu.VMEM((seq, hd), BF16), pltpu.VMEM((seq, hd), BF16), pltpu.VMEM((seq, 2 * hd), BF16),
            pltpu.VMEM((seq // ATT_BLOCK, ATT_BLOCK, 2 * ATT_BLOCK), F32),
            pltpu.VMEM((seq // ATT_BLOCK, ATT_BLOCK, 2 * ATT_BLOCK), BF16),
            pltpu.VMEM((seq // ATT_BLOCK, ATT_BLOCK, hd), F32),
            pltpu.VMEM((3, seq, hd), F32),
            pltpu.VMEM((3, seq, hd), F32),
        ],
        compiler_params=_params(("arbitrary", "arbitrary"), 40),
        name="attn_prompt",
    )(z, z, z, qn.reshape(1, hd), kn.reshape(1, hd))


def _attn_sample_kernel(q_ref, k_ref, v_ref, ck_ref, cv_ref, qn_ref, kn_ref, att_ref, ko_ref, vo_ref,
                        q_scr, kn_scr, vn_scr, *, s_new, n_buf, scale):
    pad = q_scr.shape[0]
    q_scr[...] = jnp.zeros_like(q_scr)
    kn_scr[...] = jnp.zeros_like(kn_scr)
    vn_scr[...] = jnp.zeros_like(vn_scr)
    k_new = _head_rms(k_ref[...], kn_ref[...])
    v_new = v_ref[...]
    ko_ref[...] = k_new
    vo_ref[...] = v_new
    q_scr[0:s_new, :] = _head_rms(q_ref[...], qn_ref[...])
    kn_scr[0:s_new, :] = k_new
    vn_scr[0:s_new, :] = v_new

    q = q_scr[...].astype(BF16)
    ck = ck_ref[...].astype(BF16)
    cv = cv_ref[...].astype(BF16)
    s_c = _dot_nt(q, ck) * scale
    s_n = _dot_nt(q, kn_scr[...].astype(BF16)) * scale
    qi_c = lax.broadcasted_iota(jnp.int32, (pad, n_buf), 0)
    kj_c = lax.broadcasted_iota(jnp.int32, (pad, n_buf), 1)
    dist_c = n_buf + qi_c - kj_c
    qi_n = lax.broadcasted_iota(jnp.int32, (pad, pad), 0)
    kj_n = lax.broadcasted_iota(jnp.int32, (pad, pad), 1)
    dist_n = qi_n - kj_n
    new_ok = jnp.logical_and(dist_n >= 0, kj_n < s_new)
    outs, lses = [], []
    for window, dil in BRANCHES:
        ok_c = jnp.logical_and((dist_c & (dil - 1)) == 0, dist_c <= window)
        ok_n = jnp.logical_and(new_ok, (dist_n & (dil - 1)) == 0)
        m_c = jnp.where(ok_c, s_c, NEG_INF)
        m_n = jnp.where(ok_n, s_n, NEG_INF)
        m = jnp.maximum(jnp.max(m_c, axis=-1, keepdims=True), jnp.max(m_n, axis=-1, keepdims=True))
        p_c, p_n = jnp.exp(m_c - m), jnp.exp(m_n - m)
        l = jnp.sum(p_c, axis=-1, keepdims=True) + jnp.sum(p_n, axis=-1, keepdims=True)
        o = _dot(p_c.astype(BF16), cv) + _dot(p_n.astype(BF16), vn_scr[...].astype(BF16))
        outs.append(o / l)
        lses.append(jnp.broadcast_to(m + jnp.log(l), (pad, LANES)))
    att_ref[...] = _combine(outs, lses)[0:s_new, :].astype(att_ref.dtype)


def attn_sample(z, cache_k, cache_v, row0, qn, kn, *, n_heads):
    bsz, s_new, _ = z.shape
    n_buf = cache_k.shape[1]
    assert n_buf >= BRANCHES[-1][0]
    hd = LANES
    pad = 16
    kern = functools.partial(_attn_sample_kernel, s_new=s_new, n_buf=n_buf, scale=hd ** -0.5)
    blk = lambda off: pl.BlockSpec((None, s_new, hd), lambda b, h: (b, 0, off + h))
    cblk = pl.BlockSpec((None, n_buf, hd), lambda b, h: (row0 + b, 0, h))
    vec = pl.BlockSpec((1, hd), lambda b, h: (0, 0))
    return pl.pallas_call(
        kern,
        grid=(bsz, n_heads),
        in_specs=[blk(0), blk(n_heads), blk(2 * n_heads), cblk, cblk, vec, vec],
        out_specs=[blk(0), blk(0), blk(0)],
        out_shape=[
            jax.ShapeDtypeStruct((bsz, s_new, n_heads * hd), BF16),
            jax.ShapeDtypeStruct((bsz, s_new, n_heads * hd), F32),
            jax.ShapeDtypeStruct((bsz, s_new, n_heads * hd), F32),
        ],
        scratch_shapes=[pltpu.VMEM((pad, hd), F32), pltpu.VMEM((pad, hd), F32), pltpu.VMEM((pad, hd), F32)],
        compiler_params=_params(("arbitrary", "arbitrary"), 40),
        name="attn_sample",
    )(z, z, z, cache_k, cache_v, qn.reshape(1, hd), kn.reshape(1, hd))


def _sgu_kernel(gu_ref, gv_ref, lg_ref, lb_ref, w_ref, bt_ref, o_ref, vn_ref, vb_scr, *, rows, n_g, cd):
    t = w_ref.shape[1]
    gv = _gelu(gv_ref[...].astype(F32))
    mu = jnp.mean(gv, axis=-1, keepdims=True)
    xc = gv - mu
    var = jnp.mean(xc * xc, axis=-1, keepdims=True)
    vn = (xc * lax.rsqrt(var + LN_EPS)) * lg_ref[...] + lb_ref[...]
    vn_ref[...] = vn
    if rows < t:
        vb_scr[...] = jnp.zeros_like(vb_scr)
    vb_scr[0:rows, :] = vn.astype(BF16)
    ri = lax.broadcasted_iota(jnp.int32, (t, t), 0)
    ci = lax.broadcasted_iota(jnp.int32, (t, t), 1)
    for g in range(n_g):
        cols = slice(g * cd, (g + 1) * cd)
        wg = jnp.where(ri >= ci, w_ref[g], 0.0).astype(BF16)
        mixed = _dot(wg, vb_scr[:, cols])[0:rows, :] + bt_ref[:, g:g + 1]
        o_ref[:, cols] = (_gelu(gu_ref[:, cols].astype(F32)) * mixed).astype(o_ref.dtype)


def sgu(z, colblk_u, ln_g, ln_b, w_s, b_s):
    bsz, seq, _ = z.shape
    n_g = w_s.shape[0]
    dd = ln_g.shape[0]
    cd = dd // n_g
    t = min(seq, CHUNK)
    tp = max(t, LANES)
    w = jnp.pad(w_s[:, :t, :t], ((0, 0), (0, tp - t), (0, tp - t)))
    bt = jnp.transpose(b_s[:, :t])
    kern = functools.partial(_sgu_kernel, rows=t, n_g=n_g, cd=cd)
    return pl.pallas_call(
        kern,
        grid=(bsz, seq // t),
        in_specs=[
            pl.BlockSpec((None, t, dd), lambda b, c: (b, c, colblk_u)),
            pl.BlockSpec((None, t, dd), lambda b, c: (b, c, colblk_u + 1)),
            pl.BlockSpec((1, dd), lambda b, c: (0, 0)),
            pl.BlockSpec((1, dd), lambda b, c: (0, 0)),
            pl.BlockSpec((n_g, tp, tp), lambda b, c: (0, 0, 0)),
            pl.BlockSpec((t, n_g), lambda b, c: (0, 0)),
        ],
        out_specs=[
            pl.BlockSpec((None, t, dd), lambda b, c: (b, c, 0)),
            pl.BlockSpec((None, t, dd), lambda b, c: (b, c, 0)),
        ],
        out_shape=[
            jax.ShapeDtypeStruct((bsz, seq, dd), BF16),
            jax.ShapeDtypeStruct((bsz, seq, dd), F32),
        ],
        scratch_shapes=[pltpu.VMEM((tp, dd), BF16)],
        compiler_params=_params(("arbitrary", "arbitrary"), 40),
        name="sgu",
    )(z, z, ln_g.reshape(1, dd), ln_b.reshape(1, dd), w, bt)


def _even_layer(x, bsz, seq, h0_re, h0_im, pool_buf, start_pos, norm_g, w_in, w_out, i, s5p, pool_w, pool_scale,
                w_glu, b_glu, d_skip, tiles):
    t, d = x.shape
    pw_re, pw_im, b_blk, c_blk = s5p
    n_slab = b_blk.shape[0]
    d_a = n_slab * LANES
    z = norm_matmul(x, norm_g, w_in, i, tm=tiles["tm"], tn=tiles["tn"], out_dtype=tiles["z_dtype"])
    z = z.reshape(bsz, seq, -1)
    h0 = jnp.concatenate([h0_re.reshape(bsz, n_slab, 1, SLAB_STATE), h0_im.reshape(bsz, n_slab, 1, SLAB_STATE)], axis=-1)
    y_pre, h_last = s5_scan(z, 0, b_blk, c_blk, pw_re, pw_im, d_skip, h0, split_in=tiles["s5_split"])
    ya = glu(y_pre.reshape(t, d_a), w_glu, b_glu, i, tm=tiles["tm_glu"], tn=tiles["tn"])
    buf16 = jnp.pad(pool_buf, ((0, 0), (POOL_HIST - pool_buf.shape[1], 0), (0, 0)))
    yb, tail = pool(z, 1, buf16, pool_w, i, pool_scale, start_pos=start_pos, tc=256)
    x = out_proj(x, ya, yb.reshape(t, -1), w_out, i, tm=tiles["tm_out"], tn=tiles["tn_out"])
    g_a = n_slab * SLAB_GROUPS
    h_re = h_last[..., :SLAB_STATE].reshape(bsz, g_a, S5_P)
    h_im = h_last[..., SLAB_STATE:].reshape(bsz, g_a, S5_P)
    return x, h_re, h_im, tail[:, POOL_HIST - pool_buf.shape[1]:]


def _odd_layer(x, bsz, seq, k_buf, v_buf, norm_g, w_in, w_out, i, qn, kn, ln_g, ln_b, w_s, b_s, n_heads, tiles):
    t, d = x.shape
    d_c = n_heads * LANES
    z = norm_matmul(x, norm_g, w_in, i, tm=tiles["tm"], tn=tiles["tn"], out_dtype=tiles["z_dtype"])
    z = z.reshape(bsz, seq, -1)
    if k_buf is None:
        att, k_new, v_new = attn_prompt(z, qn, kn, n_heads=n_heads)
    else:
        att, k_new, v_new = attn_sample(z, k_buf, v_buf, i * bsz, qn, kn, n_heads=n_heads)
    dd = ln_g.shape[0]
    sg, vn = sgu(z, (3 * d_c) // dd, ln_g, ln_b, w_s, b_s)
    x = out_proj(x, att.reshape(t, d_c), sg.reshape(t, dd), w_out, i, tm=tiles["tm_out"], tn=tiles["tn_out"])
    hd = LANES
    return x, k_new.reshape(bsz, seq, n_heads, hd), v_new.reshape(bsz, seq, n_heads, hd), vn


def kernel(x_prompt, x_sample, state_s5_re, state_s5_im, state_pool, cache_k, cache_v, norm_mix, norm_ffn, ev_w_in, ev_w_out, s5_lambda_re, s5_lambda_im, s5_log_dt, s5_b_re, s5_b_im, s5_c_re, s5_c_im, s5_d, s5_w_glu, s5_b_glu, pool_w, pool_scale, od_w_in, od_w_out, q_norm, k_norm, sgu_ln_g, sgu_ln_b, sgu_w, sgu_b, ffn_w1, ffn_w3, ffn_w2):
    bp, lp, d = x_prompt.shape
    bs, ls, _ = x_sample.shape
    depth = norm_mix.shape[0]
    n_heads = cache_k.shape[3]
    xp = x_prompt.reshape(bp * lp, d)
    xs = x_sample.reshape(bs * ls, d)
    bf = lambda a: a.astype(BF16)
    tiles_p = dict(tm=1024, tn=512, tm_glu=512, tm_out=512, tn_out=1024, s5_split=False, z_dtype=BF16)
    tiles_s = dict(tm=64, tn=512, tm_glu=64, tm_out=64, tn_out=1024, s5_split=True, z_dtype=F32)
    g_a, p_a = s5_lambda_re.shape[1:]
    ev_w_in_b, ev_w_out_b, w_glu_b, pool_w_b = bf(ev_w_in), bf(ev_w_out), bf(s5_w_glu), bf(pool_w)
    od_w_in_b, od_w_out_b = bf(od_w_in), bf(od_w_out)
    d_c = n_heads * LANES
    cache_k2 = cache_k.reshape(-1, cache_k.shape[2], d_c)
    cache_v2 = cache_v.reshape(-1, cache_v.shape[2], d_c)

    s5r_p, s5i_p, pool_p, k_p, v_p = [], [], [], [], []
    s5r_s, s5i_s, pool_s, k_s, v_s, sgu_s = [], [], [], [], [], []
    for l in range(depth):
        i = l // 2
        if l % 2 == 0:
            pw_re, pw_im, bb_re, bb_im = s5_prep(s5_lambda_re[i], s5_lambda_im[i], s5_log_dt[i], s5_b_re[i], s5_b_im[i])
            b_blk, c_blk = s5_block_matrices(bb_re, bb_im, s5_c_re[i], s5_c_im[i])
            s5p = (pw_re, pw_im, b_blk, c_blk)
            zero_h = jnp.zeros((bp, g_a, p_a), F32)
            zero_buf = jnp.zeros((bp, state_pool.shape[2], state_pool.shape[3]), F32)
            wargs = (ev_w_in_b, ev_w_out_b, i, s5p, pool_w_b, pool_scale[i], w_glu_b, s5_b_glu[i], s5_d[i])
            xp, hr, hi, buf = _even_layer(xp, bp, lp, zero_h, zero_h, zero_buf, 0, norm_mix[l], *wargs, tiles_p)
            s5r_p.append(hr); s5i_p.append(hi); pool_p.append(buf)
            xs, hr, hi, buf = _even_layer(xs, bs, ls, state_s5_re[i], state_s5_im[i], state_pool[i], PAST_LEN,
                                          norm_mix[l], *wargs, tiles_s)
            s5r_s.append(hr); s5i_s.append(hi); pool_s.append(buf)
        else:
            args = (norm_mix[l], od_w_in_b, od_w_out_b, i, q_norm[i], k_norm[i], sgu_ln_g[i], sgu_ln_b[i], sgu_w[i],
                    sgu_b[i], n_heads)
            xp, nk, nv, _ = _odd_layer(xp, bp, lp, None, None, *args, tiles_p)
            k_p.append(nk); v_p.append(nv)
            xs, nk, nv, vrows = _odd_layer(xs, bs, ls, cache_k2, cache_v2, *args, tiles_s)
            k_s.append(nk); v_s.append(nv); sgu_s.append(vrows)
        xs, w1_b, w3_b, w2_b = ffn_cast(xs, norm_ffn[l], ffn_w1, ffn_w3, ffn_w2, l, tf=FFN_TILE)
        xp = ffn(xp, norm_ffn[l], w1_b, w3_b, w2_b, tm=512, tf=FFN_TILE)
    return (xp.reshape(bp, lp, d), xs.reshape(bs, ls, d),
            jnp.stack(s5r_p), jnp.stack(s5i_p), jnp.stack(pool_p), jnp.stack(k_p), jnp.stack(v_p),
            jnp.stack(s5r_s), jnp.stack(s5i_s), jnp.stack(pool_s), jnp.stack(k_s), jnp.stack(v_s),
            jnp.stack(sgu_s))
```

```python
import functools
import math

import jax
import jax.numpy as jnp
from jax import lax
from jax.experimental import pallas as pl
from jax.experimental.pallas import tpu as pltpu

F32 = jnp.float32
BF16 = jnp.bfloat16

RMS_EPS = 1e-6
LN_EPS = 1e-5
NEG_INF = -1e30

LANES = 128
SUBLANES = 8
ATT_BLOCK = 128
CHUNK = 128
POOL_WINDOWS = (2, 4, 8, 16)
POOL_HIST = 16
BRANCHES = ((128, 1), (512, 4), (2048, 16))
S5_GRP = 16
S5_P = 64
SLAB_GROUPS = LANES // S5_GRP
SLAB_STATE = SLAB_GROUPS * S5_P
PAST_LEN = 8192
FFN_TILE = 256
MIX_TILE = 512


def _params(sem, vmem_mib):
    return pltpu.CompilerParams(dimension_semantics=sem, vmem_limit_bytes=vmem_mib << 20)


def _gelu(x):
    return 0.5 * x * (1.0 + lax.erf(x * (1.0 / math.sqrt(2.0))))


def _sigmoid(x):
    return 1.0 / (1.0 + jnp.exp(-x))


def _split_bf16(a):
    hi = a.astype(BF16)
    lo = (a - hi.astype(F32)).astype(BF16)
    return hi, lo


def _dot(a, b):
    return jnp.dot(a, b, preferred_element_type=F32)


def _dot_nt(a, b):
    return lax.dot_general(a, b, (((1,), (1,)), ((), ())), preferred_element_type=F32)


def _rms_rows_to(x_ref, g_ref, h_ref, rows):
    step = 16 if rows % 16 == 0 else rows

    def body(r, c):
        sl = pl.ds(pl.multiple_of(r * step, step), step)
        x = x_ref[sl, :]
        ms = jnp.mean(x * x, axis=-1, keepdims=True)
        h_ref[sl, :] = ((x * lax.rsqrt(ms + RMS_EPS)) * g_ref[...]).astype(h_ref.dtype)
        return c

    lax.fori_loop(0, rows // step, body, 0)


def _norm_matmul_kernel(x_ref, g_ref, w_ref, o_ref, h_ref):
    @pl.when(pl.program_id(1) == 0)
    def _():
        _rms_rows_to(x_ref, g_ref, h_ref, x_ref.shape[0])

    o_ref[...] = _dot(h_ref[...], w_ref[...]).astype(o_ref.dtype)


def norm_matmul(x, g, w, *, tm, out_dtype=F32):
    t, d = x.shape
    n_t, _, tn = w.shape
    return pl.pallas_call(
        _norm_matmul_kernel,
        grid=(t // tm, n_t),
        in_specs=[
            pl.BlockSpec((tm, d), lambda i, j: (i, 0)),
            pl.BlockSpec((1, d), lambda i, j: (0, 0)),
            pl.BlockSpec((None, d, tn), lambda i, j: (j, 0, 0)),
        ],
        out_specs=pl.BlockSpec((tm, tn), lambda i, j: (i, j)),
        out_shape=jax.ShapeDtypeStruct((t, n_t * tn), out_dtype),
        scratch_shapes=[pltpu.VMEM((tm, d), BF16)],
        compiler_params=_params(("arbitrary", "arbitrary"), 58),
        name="norm_matmul",
    )(x, g.reshape(1, d), w)


def _norm_matmul_cast_kernel(x_ref, g_ref, w_ref, o_ref, wb_ref, h_ref):
    @pl.when(pl.program_id(0) == 0)
    def _():
        _rms_rows_to(x_ref, g_ref, h_ref, x_ref.shape[0])

    wb_ref[...] = w_ref[...].astype(BF16)
    o_ref[...] = _dot(h_ref[...], wb_ref[...]).astype(o_ref.dtype)


def norm_matmul_cast(x, g, w, layer, *, tn):
    t, d = x.shape
    n = w.shape[2]
    return pl.pallas_call(
        _norm_matmul_cast_kernel,
        grid=(n // tn,),
        in_specs=[
            pl.BlockSpec((t, d), lambda j: (0, 0)),
            pl.BlockSpec((1, d), lambda j: (0, 0)),
            pl.BlockSpec((None, d, tn), lambda j: (layer, 0, j)),
        ],
        out_specs=[
            pl.BlockSpec((t, tn), lambda j: (0, j)),
            pl.BlockSpec((None, d, tn), lambda j: (j, 0, 0)),
        ],
        out_shape=[
            jax.ShapeDtypeStruct((t, n), F32),
            jax.ShapeDtypeStruct((n // tn, d, tn), BF16),
        ],
        scratch_shapes=[pltpu.VMEM((t, d), BF16)],
        compiler_params=_params(("arbitrary",), 48),
        name="norm_matmul_cast",
    )(x, g.reshape(1, d), w)


FFN_OUT_CHUNK = 512


def _ffn_step(first, x_ref, g_ref, w1_ref, w3_ref, w2_ref, o_ref, h_ref):
    @pl.when(first)
    def _():
        _rms_rows_to(x_ref, g_ref, h_ref, x_ref.shape[0])
        o_ref[...] = x_ref[...]

    h = h_ref[...]
    a = _dot(h, w1_ref[...])
    b = _dot(h, w3_ref[...])
    u = ((a * _sigmoid(a)) * b).astype(BF16)
    for c in range(0, o_ref.shape[1], FFN_OUT_CHUNK):
        o_ref[:, c:c + FFN_OUT_CHUNK] += _dot(u, w2_ref[:, c:c + FFN_OUT_CHUNK])


def _ffn_kernel(x_ref, g_ref, w1_ref, w3_ref, w2_ref, o_ref, h_ref):
    _ffn_step(pl.program_id(1) == 0, x_ref, g_ref, w1_ref, w3_ref, w2_ref, o_ref, h_ref)


def ffn(x, g, w1, w3, w2, *, tm):
    t, d = x.shape
    n_f, _, tf = w1.shape
    return pl.pallas_call(
        _ffn_kernel,
        grid=(t // tm, n_f),
        in_specs=[
            pl.BlockSpec((tm, d), lambda i, j: (i, 0)),
            pl.BlockSpec((1, d), lambda i, j: (0, 0)),
            pl.BlockSpec((None, d, tf), lambda i, j: (j, 0, 0)),
            pl.BlockSpec((None, d, tf), lambda i, j: (j, 0, 0)),
            pl.BlockSpec((tf, d), lambda i, j: (j, 0)),
        ],
        out_specs=pl.BlockSpec((tm, d), lambda i, j: (i, 0)),
        out_shape=jax.ShapeDtypeStruct((t, d), F32),
        scratch_shapes=[pltpu.VMEM((tm, d), BF16)],
        compiler_params=_params(("arbitrary", "arbitrary"), 56),
        name="ffn",
    )(x, g.reshape(1, d), w1, w3, w2)


def _ffn_cast_kernel(x_ref, g_ref, w1_ref, w3_ref, w2_ref, o_ref, w1b_ref, w3b_ref, w2b_ref, h_ref):
    w1b_ref[...] = w1_ref[...].astype(BF16)
    w3b_ref[...] = w3_ref[...].astype(BF16)
    w2b_ref[...] = w2_ref[...].astype(BF16)
    _ffn_step(pl.program_id(0) == 0, x_ref, g_ref, w1b_ref, w3b_ref, w2b_ref, o_ref, h_ref)


def ffn_cast(x, g, w1, w3, w2, layer, *, tf):
    t, d = x.shape
    f = w1.shape[2]
    return pl.pallas_call(
        _ffn_cast_kernel,
        grid=(f // tf,),
        in_specs=[
            pl.BlockSpec((t, d), lambda j: (0, 0)),
            pl.BlockSpec((1, d), lambda j: (0, 0)),
            pl.BlockSpec((None, d, tf), lambda j: (layer, 0, j)),
            pl.BlockSpec((None, d, tf), lambda j: (layer, 0, j)),
            pl.BlockSpec((None, tf, d), lambda j: (layer, j, 0)),
        ],
        out_specs=[
            pl.BlockSpec((t, d), lambda j: (0, 0)),
            pl.BlockSpec((None, d, tf), lambda j: (j, 0, 0)),
            pl.BlockSpec((None, d, tf), lambda j: (j, 0, 0)),
            pl.BlockSpec((tf, d), lambda j: (j, 0)),
        ],
        out_shape=[
            jax.ShapeDtypeStruct((t, d), F32),
            jax.ShapeDtypeStruct((f // tf, d, tf), BF16),
            jax.ShapeDtypeStruct((f // tf, d, tf), BF16),
            jax.ShapeDtypeStruct((f, d), BF16),
        ],
        scratch_shapes=[pltpu.VMEM((t, d), BF16)],
        compiler_params=_params(("arbitrary",), 48),
        name="ffn_cast",
    )(x, g.reshape(1, d), w1, w3, w2)


def _out_proj_kernel(x_ref, a_ref, b_ref, wa_ref, wb_ref, o_ref):
    o_ref[...] = x_ref[...] + _dot(a_ref[...], wa_ref[...]) + _dot(b_ref[...], wb_ref[...])


def out_proj(x, a, b, w_pair, *, tm):
    t, d = x.shape
    wa, wb = w_pair
    n_t, k, tn = wa.shape
    assert a.shape[1] == k and b.shape[1] == k and wb.shape == wa.shape
    return pl.pallas_call(
        _out_proj_kernel,
        grid=(t // tm, n_t),
        in_specs=[
            pl.BlockSpec((tm, tn), lambda i, j: (i, j)),
            pl.BlockSpec((tm, k), lambda i, j: (i, 0)),
            pl.BlockSpec((tm, k), lambda i, j: (i, 0)),
            pl.BlockSpec((None, k, tn), lambda i, j: (j, 0, 0)),
            pl.BlockSpec((None, k, tn), lambda i, j: (j, 0, 0)),
        ],
        out_specs=pl.BlockSpec((tm, tn), lambda i, j: (i, j)),
        out_shape=jax.ShapeDtypeStruct((t, d), F32),
        compiler_params=_params(("arbitrary", "arbitrary"), 48),
        name="out_proj",
    )(x, a, b, wa, wb)


def _out_proj_cast_kernel(x_ref, a_ref, b_ref, wa_ref, wb_ref, o_ref, wab_ref, wbb_ref):
    wab_ref[...] = wa_ref[...].astype(BF16)
    wbb_ref[...] = wb_ref[...].astype(BF16)
    o_ref[...] = x_ref[...] + _dot(a_ref[...], wab_ref[...]) + _dot(b_ref[...], wbb_ref[...])


def out_proj_cast(x, a, b, w, layer, *, tn):
    t, d = x.shape
    k = a.shape[1]
    assert b.shape[1] == k and w.shape[1] == 2 * k
    out, wa_b, wb_b = pl.pallas_call(
        _out_proj_cast_kernel,
        grid=(d // tn,),
        in_specs=[
            pl.BlockSpec((t, tn), lambda j: (0, j)),
            pl.BlockSpec((t, k), lambda j: (0, 0)),
            pl.BlockSpec((t, k), lambda j: (0, 0)),
            pl.BlockSpec((None, k, tn), lambda j: (layer, 0, j)),
            pl.BlockSpec((None, k, tn), lambda j: (layer, 1, j)),
        ],
        out_specs=[
            pl.BlockSpec((t, tn), lambda j: (0, j)),
            pl.BlockSpec((None, k, tn), lambda j: (j, 0, 0)),
            pl.BlockSpec((None, k, tn), lambda j: (j, 0, 0)),
        ],
        out_shape=[
            jax.ShapeDtypeStruct((t, d), F32),
            jax.ShapeDtypeStruct((d // tn, k, tn), BF16),
            jax.ShapeDtypeStruct((d // tn, k, tn), BF16),
        ],
        compiler_params=_params(("arbitrary",), 48),
        name="out_proj_cast",
    )(x, a, b, w, w)
    return out, (wa_b, wb_b)


def _s5_prep_kernel(lr_ref, li_ref, ldt_ref, lrx_ref, lix_ref, ldtx_ref, br_ref, bi_ref,
                    pwr_ref, pwi_ref, bbr_ref, bbi_ref):
    dt = jnp.exp(ldt_ref[...])
    mag = jnp.exp(lr_ref[...] * dt)
    ang = li_ref[...] * dt
    p_r, p_i = mag * jnp.cos(ang), mag * jnp.sin(ang)
    c_r, c_i = p_r, p_i
    pwr_ref[0], pwi_ref[0] = c_r, c_i
    for j in range(1, SUBLANES):
        c_r, c_i = c_r * p_r - c_i * p_i, c_r * p_i + c_i * p_r
        pwr_ref[j], pwi_ref[j] = c_r, c_i
    lr, li = lrx_ref[...], lix_ref[...]
    dtx = jnp.exp(ldtx_ref[...])
    magx = jnp.exp(lr * dtx)
    angx = li * dtx
    nr, ni = magx * jnp.cos(angx) - 1.0, magx * jnp.sin(angx)
    den = lr * lr + li * li
    qr = (nr * lr + ni * li) / den
    qi = (ni * lr - nr * li) / den
    br, bi = br_ref[...], bi_ref[...]
    bbr_ref[...] = qr * br - qi * bi
    bbi_ref[...] = qr * bi + qi * br


def s5_prep(lam_re, lam_im, log_dt, b_re, b_im):
    g, p = lam_re.shape
    h = b_re.shape[2]
    n_slab = g // SLAB_GROUPS
    slab = lambda a: a.reshape(n_slab, SLAB_GROUPS * p)
    rep = lambda a: jnp.repeat(a, h, axis=1)
    ldt_gp = jnp.broadcast_to(log_dt[:, None], (g, p))
    outs = pl.pallas_call(
        _s5_prep_kernel,
        out_shape=[
            jax.ShapeDtypeStruct((SUBLANES, n_slab, SLAB_GROUPS * p), F32),
            jax.ShapeDtypeStruct((SUBLANES, n_slab, SLAB_GROUPS * p), F32),
            jax.ShapeDtypeStruct((g, p * h), F32),
            jax.ShapeDtypeStruct((g, p * h), F32),
        ],
        name="s5_prep",
    )(slab(lam_re), slab(lam_im), slab(ldt_gp), rep(lam_re), rep(lam_im), rep(ldt_gp),
      b_re.reshape(g, p * h), b_im.reshape(g, p * h))
    pw_re, pw_im, bb_re, bb_im = outs
    pw_re = jnp.transpose(pw_re, (1, 0, 2))
    pw_im = jnp.transpose(pw_im, (1, 0, 2))
    return pw_re, pw_im, bb_re.reshape(g, p, h), bb_im.reshape(g, p, h)


def s5_block_matrices(bb_re, bb_im, c_re, c_im):
    g, p, h = bb_re.shape
    n_slab = g // SLAB_GROUPS
    eye = jnp.eye(SLAB_GROUPS, dtype=F32)

    def in_map(bb):
        t = bb.reshape(n_slab, SLAB_GROUPS, p, h)
        return jnp.einsum("kgph,gj->kghjp", t, eye).reshape(n_slab, SLAB_GROUPS * h, SLAB_GROUPS * p)

    def out_map(c):
        t = c.reshape(n_slab, SLAB_GROUPS, h, p)
        return jnp.einsum("kghp,gj->kgpjh", t, eye).reshape(n_slab, SLAB_GROUPS * p, SLAB_GROUPS * h)

    b_blk = jnp.concatenate([in_map(bb_re), in_map(bb_im)], axis=2)
    c_blk = jnp.concatenate([out_map(c_re), -out_map(c_im)], axis=1)
    return b_blk, c_blk


def _s5_scan_kernel(u_ref, bblk_ref, cblk_ref, pwr_ref, pwi_ref, d_ref, h0_ref,
                    y_ref, hl_ref, h_scr, bh_scr, bl_scr, ch_scr, *, seq, row_chunk, split_in):
    ns = SLAB_STATE

    @pl.when(pl.program_id(1) == 0)
    def _():
        bh, bl = _split_bf16(bblk_ref[...])
        bh_scr[...], bl_scr[...] = bh, bl
        ch_scr[...] = cblk_ref[...].astype(BF16)

    n_chunks = seq // row_chunk

    def proj_in(r, c):
        sl = pl.ds(pl.multiple_of(r * row_chunk, row_chunk), row_chunk)
        if split_in:
            uh, ul = _split_bf16(u_ref[sl, :].astype(F32))
            h_scr[sl, :] = _dot(uh, bh_scr[...]) + _dot(ul, bh_scr[...]) + _dot(uh, bl_scr[...])
        else:
            h_scr[sl, :] = _dot(u_ref[sl, :].astype(BF16), bh_scr[...])
        return c

    lax.fori_loop(0, n_chunks, proj_in, 0)

    rowid = lax.broadcasted_iota(jnp.int32, (SUBLANES, LANES), 0)
    for c in range(ns // LANES):
        re_l = slice(c * LANES, (c + 1) * LANES)
        im_l = slice(ns + c * LANES, ns + (c + 1) * LANES)
        p_r, p_i = pwr_ref[:, re_l], pwi_ref[:, re_l]
        steps = []
        for dist in (1, 2, 4):
            a_r = jnp.where(rowid >= dist, jnp.broadcast_to(p_r[dist - 1:dist], (SUBLANES, LANES)), 0.0)
            a_i = jnp.where(rowid >= dist, jnp.broadcast_to(p_i[dist - 1:dist], (SUBLANES, LANES)), 0.0)
            steps.append((dist, a_r, a_i))
        c_r = jnp.broadcast_to(h0_ref[:, re_l], (SUBLANES, LANES))
        c_i = jnp.broadcast_to(h0_ref[:, im_l], (SUBLANES, LANES))

        last = slice(SUBLANES - 1, SUBLANES)
        full = (SUBLANES, LANES)
        p8_r, p8_i = jnp.broadcast_to(p_r[last], full), jnp.broadcast_to(p_i[last], full)
        n_groups = seq // SUBLANES
        per_it = min(4, n_groups)

        def scan_rows(it, carry, re_l=re_l, im_l=im_l, p_r=p_r, p_i=p_i, p8_r=p8_r, p8_i=p8_i, steps=steps):
            base = pl.multiple_of(it * (per_it * SUBLANES), per_it * SUBLANES)
            sls = [pl.ds(base + j * SUBLANES, SUBLANES) for j in range(per_it)]
            loc = []
            for sl in sls:
                r, i = h_scr[sl, re_l], h_scr[sl, im_l]
                for dist, a_r, a_i in steps:
                    s_r, s_i = pltpu.roll(r, dist, 0), pltpu.roll(i, dist, 0)
                    r, i = r + (s_r * a_r - s_i * a_i), i + (s_r * a_i + s_i * a_r)
                loc.append((r, i))
            c_r, c_i = carry
            outs = []
            for r, i in loc:
                outs.append((r + (c_r * p_r - c_i * p_i), i + (c_r * p_i + c_i * p_r)))
                e_r, e_i = jnp.broadcast_to(r[last], full), jnp.broadcast_to(i[last], full)
                c_r, c_i = e_r + (c_r * p8_r - c_i * p8_i), e_i + (c_r * p8_i + c_i * p8_r)
            for sl, (r, i) in zip(sls, outs):
                h_scr[sl, re_l], h_scr[sl, im_l] = r, i
            return c_r, c_i

        c_r, c_i = lax.fori_loop(0, n_groups // per_it, scan_rows, (c_r, c_i))
        hl_ref[:, re_l] = c_r[0:1]
        hl_ref[:, im_l] = c_i[0:1]

    def proj_out(r, c):
        sl = pl.ds(pl.multiple_of(r * row_chunk, row_chunk), row_chunk)
        y = _dot(h_scr[sl, :].astype(BF16), ch_scr[...]) + d_ref[...] * u_ref[sl, :].astype(F32)
        y_ref[sl, :] = _gelu(y)
        return c

    lax.fori_loop(0, n_chunks, proj_out, 0)


def s5_scan(z, col0, b_blk, c_blk, pw_re, pw_im, d_skip, h0, *, split_in):
    bsz, seq, _ = z.shape
    n_slab = b_blk.shape[0]
    ns2 = 2 * SLAB_STATE
    row_chunk = min(seq, 256)
    kern = functools.partial(_s5_scan_kernel, seq=seq, row_chunk=row_chunk, split_in=split_in)
    return pl.pallas_call(
        kern,
        grid=(n_slab, bsz),
        in_specs=[
            pl.BlockSpec((None, seq, LANES), lambda k, b: (b, 0, col0 + k)),
            pl.BlockSpec((None, LANES, ns2), lambda k, b: (k, 0, 0)),
            pl.BlockSpec((None, ns2, LANES), lambda k, b: (k, 0, 0)),
            pl.BlockSpec((None, SUBLANES, SLAB_STATE), lambda k, b: (k, 0, 0)),
            pl.BlockSpec((None, SUBLANES, SLAB_STATE), lambda k, b: (k, 0, 0)),
            pl.BlockSpec((1, LANES), lambda k, b: (0, k)),
            pl.BlockSpec((None, None, 1, ns2), lambda k, b: (b, k, 0, 0)),
        ],
        out_specs=[
            pl.BlockSpec((None, seq, LANES), lambda k, b: (b, 0, k)),
            pl.BlockSpec((None, None, 1, ns2), lambda k, b: (b, k, 0, 0)),
        ],
        out_shape=[
            jax.ShapeDtypeStruct((bsz, seq, n_slab * LANES), F32),
            jax.ShapeDtypeStruct((bsz, n_slab, 1, ns2), F32),
        ],
        scratch_shapes=[
            pltpu.VMEM((seq, ns2), F32),
            pltpu.VMEM((LANES, ns2), BF16), pltpu.VMEM((LANES, ns2), BF16),
            pltpu.VMEM((ns2, LANES), BF16),
        ],
        compiler_params=_params(("arbitrary", "arbitrary"), 40),
        name="s5_scan",
    )(z, b_blk, c_blk, pw_re, pw_im, d_skip.reshape(1, -1), h0)


def _glu_kernel(yk_ref, yj_ref, w_ref, b_ref, o_ref, yb_scr):
    @pl.when(pl.program_id(1) == 0)
    def _():
        yb_scr[...] = yk_ref[...].astype(BF16)

    gate = _dot(yb_scr[...], w_ref[...]) + b_ref[...]
    o_ref[...] = (yj_ref[...] * _sigmoid(gate)).astype(o_ref.dtype)


def glu(y, w, b, layer, *, tm, tn):
    t, d = y.shape
    tm = min(tm, t)
    return pl.pallas_call(
        _glu_kernel,
        grid=(t // tm, d // tn),
        in_specs=[
            pl.BlockSpec((tm, d), lambda i, j: (i, 0)),
            pl.BlockSpec((tm, tn), lambda i, j: (i, j)),
            pl.BlockSpec((None, d, tn), lambda i, j: (layer, 0, j)),
            pl.BlockSpec((1, tn), lambda i, j: (0, j)),
        ],
        out_specs=pl.BlockSpec((tm, tn), lambda i, j: (i, j)),
        out_shape=jax.ShapeDtypeStruct((t, d), BF16),
        scratch_shapes=[pltpu.VMEM((tm, d), BF16)],
        compiler_params=_params(("arbitrary", "arbitrary"), 40),
        name="glu",
    )(y, y, w, b.reshape(1, d))


def _pool_kernel(u_ref, buf_ref, w_ref, s_ref, y_ref, tail_ref, ext_scr, *, tc, start_pos, cg):
    c = pl.program_id(1)

    @pl.when(c == 0)
    def _():
        ext_scr[0:POOL_HIST, :] = buf_ref[...]

    ext_scr[POOL_HIST:POOL_HIST + tc, :] = u_ref[...].astype(F32)
    pos = start_pos + c * tc + lax.broadcasted_iota(jnp.int32, (tc, 1), 0)
    for g, win in enumerate(POOL_WINDOWS):
        cols = slice(g * cg, (g + 1) * cg)
        x = ext_scr[:, cols]
        acc, dist = x, 1
        while dist < win:
            acc = acc + pltpu.roll(acc, dist, 0)
            dist *= 2
        wsum = acc[POOL_HIST:, :]
        cnt = jnp.minimum(pos + 1, win).astype(F32)
        zg = wsum * (1.0 / cnt) - x[POOL_HIST:, :]
        y = _dot(zg.astype(BF16), w_ref[g]) * s_ref[:, cols]
        y_ref[:, cols] = y.astype(y_ref.dtype)

    tail = ext_scr[tc:tc + POOL_HIST, :]
    ext_scr[0:POOL_HIST, :] = tail

    @pl.when(c == pl.num_programs(1) - 1)
    def _():
        tail_ref[...] = tail


def pool(z, colblk, buf16, w, layer, scale, *, start_pos, tc):
    bsz, seq, _ = z.shape
    _, n_g, cg, _ = w.shape
    db = n_g * cg
    tc = min(tc, seq)
    kern = functools.partial(_pool_kernel, tc=tc, start_pos=start_pos, cg=cg)
    return pl.pallas_call(
        kern,
        grid=(bsz, seq // tc),
        in_specs=[
            pl.BlockSpec((None, tc, db), lambda b, c: (b, c, colblk)),
            pl.BlockSpec((None, POOL_HIST, db), lambda b, c: (b, 0, 0)),
            pl.BlockSpec((None, n_g, cg, cg), lambda b, c: (layer, 0, 0, 0)),
            pl.BlockSpec((1, db), lambda b, c: (0, 0)),
        ],
        out_specs=[
            pl.BlockSpec((None, tc, db), lambda b, c: (b, c, 0)),
            pl.BlockSpec((None, POOL_HIST, db), lambda b, c: (b, 0, 0)),
        ],
        out_shape=[
            jax.ShapeDtypeStruct((bsz, seq, db), BF16),
            jax.ShapeDtypeStruct((bsz, POOL_HIST, db), F32),
        ],
        scratch_shapes=[pltpu.VMEM((POOL_HIST + tc, db), F32)],
        compiler_params=_params(("arbitrary", "arbitrary"), 40),
        name="pool",
    )(z, buf16, w, scale.reshape(1, db))


def _head_rms(x, g):
    ms = jnp.mean(x * x, axis=-1, keepdims=True)
    return (x * lax.rsqrt(ms + RMS_EPS)) * g


def _combine(os_, lses):
    m = jnp.maximum(jnp.maximum(lses[0], lses[1]), lses[2])
    ws = [jnp.exp(l - m) for l in lses]
    tot = ws[0] + ws[1] + ws[2]
    return (ws[0] * os_[0] + ws[1] * os_[1] + ws[2] * os_[2]) / tot


def _attn_prompt_kernel(q_ref, k_ref, v_ref, qn_ref, kn_ref, att_ref, ko_ref, vo_ref,
                        qs_scr, qf_scr, kf_scr, vf_scr, qd_scr, kd_scr, vd_scr, s_scr, p_scr, m_scr, o_scr, l_scr,
                        *, seq, scale):
    blk = ATT_BLOCK
    rows = 256
    n_all = seq // blk

    def prep(r, c):
        sl = pl.ds(pl.multiple_of(r * rows, rows), rows)
        qs_scr[sl, :] = _head_rms(q_ref[sl, :].astype(F32), qn_ref[...]) * scale
        ko_ref[sl, :] = _head_rms(k_ref[sl, :].astype(F32), kn_ref[...])
        vo_ref[sl, :] = v_ref[sl, :].astype(F32)
        vd_scr[sl, LANES:] = jnp.ones((rows, LANES), BF16)
        return c

    lax.fori_loop(0, seq // rows, prep, 0)

    qi = lax.broadcasted_iota(jnp.int32, (blk, blk), 0)
    kj = lax.broadcasted_iota(jnp.int32, (blk, blk), 1)
    cur_ok = kj <= qi
    prev_ok = kj >= qi

    for g, (window, dil) in enumerate(BRANCHES):
        n_blk = seq // (dil * blk)
        col0 = 0 if n_blk > 1 else LANES

        def place(idx, dil=dil, n_blk=n_blk):
            res = idx // n_blk
            n = idx - res * n_blk
            start = res + n * (dil * blk)
            nat = pl.ds(start, blk, stride=dil) if dil > 1 else pl.ds(pl.multiple_of(start, blk), blk)
            cur = pl.ds(pl.multiple_of(idx * blk, blk), blk)
            prv = pl.ds(pl.multiple_of(jnp.maximum(idx - 1, 0) * blk, blk), blk)
            return nat, cur, prv, n

        keep_f32 = dil == BRANCHES[1][1]
        two_level = g == 2 and dil == BRANCHES[1][1] ** 2

        def gather(idx, c, place=place, dil=dil, n_blk=n_blk, keep_f32=keep_f32, two_level=two_level):
            nat, cur, _, n = place(idx)
            if two_level:
                mid = BRANCHES[1][1]
                res = idx // n_blk
                start = (res % mid) * (seq // mid) + res // mid + n * (mid * blk)
                src = pl.ds(start, blk, stride=mid)
                q, k, v = qf_scr[src, :], kf_scr[src, :], vf_scr[src, :]
            else:
                q, k, v = qs_scr[nat, :], ko_ref[nat, :], vo_ref[nat, :]
            if keep_f32:
                qf_scr[cur, :], kf_scr[cur, :], vf_scr[cur, :] = q, k, v
            qd_scr[cur, :] = q.astype(BF16)
            kd_scr[cur, :] = k.astype(BF16)
            vd_scr[cur, 0:LANES] = v.astype(BF16)
            return c

        lax.fori_loop(0, n_all, gather, 0, unroll=4)

        def scores(idx, c, place=place, n_blk=n_blk):
            _, cur, prv, n = place(idx)
            q = qd_scr[cur, :]
            s_scr[idx, :, LANES:] = jnp.where(cur_ok, _dot_nt(q, kd_scr[cur, :]), NEG_INF)
            if n_blk > 1:
                ok = jnp.logical_and(prev_ok, n > 0)
                s_scr[idx, :, 0:LANES] = jnp.where(ok, _dot_nt(q, kd_scr[prv, :]), NEG_INF)
            return c

        lax.fori_loop(0, n_all, scores, 0, unroll=8)

        def softmax(idx, c, col0=col0):
            s = s_scr[idx, :, col0:]
            m = jnp.max(s, axis=-1, keepdims=True)
            p_scr[idx, :, col0:] = jnp.exp(s - m).astype(BF16)
            m_scr[idx] = jnp.broadcast_to(m, (blk, LANES))
            return c

        lax.fori_loop(0, n_all, softmax, 0, unroll=4)

        def values(idx, c, g=g, place=place, n_blk=n_blk):
            nat, cur, prv, _ = place(idx)
            ov = _dot(p_scr[idx, :, LANES:], vd_scr[cur, :])
            if n_blk > 1:
                ov = ov + _dot(p_scr[idx, :, 0:LANES], vd_scr[prv, :])
            l = ov[:, LANES:]
            o_scr[g, nat, :] = ov[:, 0:LANES] / l
            l_scr[g, nat, :] = m_scr[idx] + jnp.log(l)
            return c

        lax.fori_loop(0, n_all, values, 0, unroll=8)

    def comb(r, c):
        sl = pl.ds(pl.multiple_of(r * rows, rows), rows)
        out = _combine([o_scr[g, sl, :] for g in range(3)], [l_scr[g, sl, :] for g in range(3)])
        att_ref[sl, :] = out.astype(att_ref.dtype)
        return c

    lax.fori_loop(0, seq // rows, comb, 0)


def attn_prompt(z, qn, kn, *, n_heads):
    bsz, seq, _ = z.shape
    assert seq % (BRANCHES[-1][1] * ATT_BLOCK) == 0
    hd = LANES
    kern = functools.partial(_attn_prompt_kernel, seq=seq, scale=hd ** -0.5)
    blk = lambda off: pl.BlockSpec((None, seq, hd), lambda b, h: (b, 0, off + h))
    return pl.pallas_call(
        kern,
        grid=(bsz, n_heads),
        in_specs=[blk(0), blk(n_heads), blk(2 * n_heads),
                  pl.BlockSpec((1, hd), lambda b, h: (0, 0)), pl.BlockSpec((1, hd), lambda b, h: (0, 0))],
        out_specs=[blk(0), blk(0), blk(0)],
        out_shape=[
            jax.ShapeDtypeStruct((bsz, seq, n_heads * hd), BF16),
            jax.ShapeDtypeStruct((bsz, seq, n_heads * hd), F32),
            jax.ShapeDtypeStruct((bsz, seq, n_heads * hd), F32),
        ],
        scratch_shapes=[
            pltpu.VMEM((seq, hd), F32),
            pltpu.VMEM((seq, hd), F32), pltpu.VMEM((seq, hd), F32), pltpu.VMEM((seq, hd), F32),
            pltpu.VMEM((seq, hd), BF16), pltpu.VMEM((seq, hd), BF16), pltpu.VMEM((seq, 2 * hd), BF16),
            pltpu.VMEM((seq // ATT_BLOCK, ATT_BLOCK, 2 * ATT_BLOCK), F32),
            pltpu.VMEM((seq // ATT_BLOCK, ATT_BLOCK, 2 * ATT_BLOCK), BF16),
            pltpu.VMEM((seq // ATT_BLOCK, ATT_BLOCK, hd), F32),
            pltpu.VMEM((3, seq, hd), F32),
            pltpu.VMEM((3, seq, hd), F32),
        ],
        compiler_params=_params(("arbitrary", "arbitrary"), 40),
        name="attn_prompt",
    )(z, z, z, qn.reshape(1, hd), kn.reshape(1, hd))


def _attn_sample_kernel(q_ref, k_ref, v_ref, ck_ref, cv_ref, qn_ref, kn_ref, att_ref, ko_ref, vo_ref,
                        q_scr, kn_scr, vn_scr, *, s_new, n_buf, scale):
    pad = q_scr.shape[0]
    q_scr[...] = jnp.zeros_like(q_scr)
    kn_scr[...] = jnp.zeros_like(kn_scr)
    vn_scr[...] = jnp.zeros_like(vn_scr)
    k_new = _head_rms(k_ref[...], kn_ref[...])
    v_new = v_ref[...]
    ko_ref[...] = k_new
    vo_ref[...] = v_new
    q_scr[0:s_new, :] = _head_rms(q_ref[...], qn_ref[...])
    kn_scr[0:s_new, :] = k_new
    vn_scr[0:s_new, :] = v_new

    q = q_scr[...].astype(BF16)
    ck = ck_ref[...].astype(BF16)
    cv = cv_ref[...].astype(BF16)
    s_c = _dot_nt(q, ck) * scale
    s_n = _dot_nt(q, kn_scr[...].astype(BF16)) * scale
    qi_c = lax.broadcasted_iota(jnp.int32, (pad, n_buf), 0)
    kj_c = lax.broadcasted_iota(jnp.int32, (pad, n_buf), 1)
    dist_c = n_buf + qi_c - kj_c
    qi_n = lax.broadcasted_iota(jnp.int32, (pad, pad), 0)
    kj_n = lax.broadcasted_iota(jnp.int32, (pad, pad), 1)
    dist_n = qi_n - kj_n
    new_ok = jnp.logical_and(dist_n >= 0, kj_n < s_new)
    outs, lses = [], []
    for window, dil in BRANCHES:
        ok_c = jnp.logical_and((dist_c & (dil - 1)) == 0, dist_c <= window)
        ok_n = jnp.logical_and(new_ok, (dist_n & (dil - 1)) == 0)
        m_c = jnp.where(ok_c, s_c, NEG_INF)
        m_n = jnp.where(ok_n, s_n, NEG_INF)
        m = jnp.maximum(jnp.max(m_c, axis=-1, keepdims=True), jnp.max(m_n, axis=-1, keepdims=True))
        p_c, p_n = jnp.exp(m_c - m), jnp.exp(m_n - m)
        l = jnp.sum(p_c, axis=-1, keepdims=True) + jnp.sum(p_n, axis=-1, keepdims=True)
        o = _dot(p_c.astype(BF16), cv) + _dot(p_n.astype(BF16), vn_scr[...].astype(BF16))
        outs.append(o / l)
        lses.append(jnp.broadcast_to(m + jnp.log(l), (pad, LANES)))
    att_ref[...] = _combine(outs, lses)[0:s_new, :].astype(att_ref.dtype)


def attn_sample(z, cache_k, cache_v, row0, qn, kn, *, n_heads):
    bsz, s_new, _ = z.shape
    n_buf = cache_k.shape[1]
    assert n_buf >= BRANCHES[-1][0]
    hd = LANES
    pad = 16
    kern = functools.partial(_attn_sample_kernel, s_new=s_new, n_buf=n_buf, scale=hd ** -0.5)
    blk = lambda off: pl.BlockSpec((None, s_new, hd), lambda b, h: (b, 0, off + h))
    cblk = pl.BlockSpec((None, n_buf, hd), lambda b, h: (row0 + b, 0, h))
    vec = pl.BlockSpec((1, hd), lambda b, h: (0, 0))
    return pl.pallas_call(
        kern,
        grid=(bsz, n_heads),
        in_specs=[blk(0), blk(n_heads), blk(2 * n_heads), cblk, cblk, vec, vec],
        out_specs=[blk(0), blk(0), blk(0)],
        out_shape=[
            jax.ShapeDtypeStruct((bsz, s_new, n_heads * hd), BF16),
            jax.ShapeDtypeStruct((bsz, s_new, n_heads * hd), F32),
            jax.ShapeDtypeStruct((bsz, s_new, n_heads * hd), F32),
        ],
        scratch_shapes=[pltpu.VMEM((pad, hd), F32), pltpu.VMEM((pad, hd), F32), pltpu.VMEM((pad, hd), F32)],
        compiler_params=_params(("arbitrary", "arbitrary"), 40),
        name="attn_sample",
    )(z, z, z, cache_k, cache_v, qn.reshape(1, hd), kn.reshape(1, hd))


def _sgu_kernel(gu_ref, gv_ref, lg_ref, lb_ref, w_ref, bt_ref, o_ref, vn_ref, vb_scr, *, rows, n_g, cd):
    t = w_ref.shape[1]
    gv = _gelu(gv_ref[...].astype(F32))
    mu = jnp.mean(gv, axis=-1, keepdims=True)
    xc = gv - mu
    var = jnp.mean(xc * xc, axis=-1, keepdims=True)
    vn = (xc * lax.rsqrt(var + LN_EPS)) * lg_ref[...] + lb_ref[...]
    vn_ref[...] = vn
    if rows < t:
        vb_scr[...] = jnp.zeros_like(vb_scr)
    vb_scr[0:rows, :] = vn.astype(BF16)
    ri = lax.broadcasted_iota(jnp.int32, (t, t), 0)
    ci = lax.broadcasted_iota(jnp.int32, (t, t), 1)
    for g in range(n_g):
        cols = slice(g * cd, (g + 1) * cd)
        wg = jnp.where(ri >= ci, w_ref[g], 0.0).astype(BF16)
        mixed = _dot(wg, vb_scr[:, cols])[0:rows, :] + bt_ref[:, g:g + 1]
        o_ref[:, cols] = (_gelu(gu_ref[:, cols].astype(F32)) * mixed).astype(o_ref.dtype)


def sgu(z, colblk_u, ln_g, ln_b, w_s, b_s):
    bsz, seq, _ = z.shape
    n_g = w_s.shape[0]
    dd = ln_g.shape[0]
    cd = dd // n_g
    t = min(seq, CHUNK)
    tp = max(t, LANES)
    w = jnp.pad(w_s[:, :t, :t], ((0, 0), (0, tp - t), (0, tp - t)))
    bt = jnp.transpose(b_s[:, :t])
    kern = functools.partial(_sgu_kernel, rows=t, n_g=n_g, cd=cd)
    return pl.pallas_call(
        kern,
        grid=(bsz, seq // t),
        in_specs=[
            pl.BlockSpec((None, t, dd), lambda b, c: (b, c, colblk_u)),
            pl.BlockSpec((None, t, dd), lambda b, c: (b, c, colblk_u + 1)),
            pl.BlockSpec((1, dd), lambda b, c: (0, 0)),
            pl.BlockSpec((1, dd), lambda b, c: (0, 0)),
            pl.BlockSpec((n_g, tp, tp), lambda b, c: (0, 0, 0)),
            pl.BlockSpec((t, n_g), lambda b, c: (0, 0)),
        ],
        out_specs=[
            pl.BlockSpec((None, t, dd), lambda b, c: (b, c, 0)),
            pl.BlockSpec((None, t, dd), lambda b, c: (b, c, 0)),
        ],
        out_shape=[
            jax.ShapeDtypeStruct((bsz, seq, dd), BF16),
            jax.ShapeDtypeStruct((bsz, seq, dd), F32),
        ],
        scratch_shapes=[pltpu.VMEM((tp, dd), BF16)],
        compiler_params=_params(("arbitrary", "arbitrary"), 40),
        name="sgu",
    )(z, z, ln_g.reshape(1, dd), ln_b.reshape(1, dd), w, bt)


def _in_proj(x, norm_g, w_in, i, tiles):
    if tiles["cast"]:
        return norm_matmul_cast(x, norm_g, w_in, i, tn=MIX_TILE)
    return norm_matmul(x, norm_g, w_in, tm=tiles["tm"], out_dtype=tiles["z_dtype"]), None


def _res_proj(x, a, b, w_out, i, tiles):
    if tiles["cast"]:
        return out_proj_cast(x, a, b, w_out, i, tn=MIX_TILE)
    return out_proj(x, a, b, w_out, tm=tiles["tm_out"]), None


def _even_layer(x, bsz, seq, h0_re, h0_im, pool_buf, start_pos, norm_g, w_in, w_out, i, s5p, pool_w, pool_scale,
                w_glu, b_glu, d_skip, tiles):
    t, d = x.shape
    pw_re, pw_im, b_blk, c_blk = s5p
    n_slab = b_blk.shape[0]
    d_a = n_slab * LANES
    z, w_in_b = _in_proj(x, norm_g, w_in, i, tiles)
    z = z.reshape(bsz, seq, -1)
    h0 = jnp.concatenate([h0_re.reshape(bsz, n_slab, 1, SLAB_STATE), h0_im.reshape(bsz, n_slab, 1, SLAB_STATE)], axis=-1)
    y_pre, h_last = s5_scan(z, 0, b_blk, c_blk, pw_re, pw_im, d_skip, h0, split_in=tiles["s5_split"])
    ya = glu(y_pre.reshape(t, d_a), w_glu, b_glu, i, tm=tiles["tm_glu"], tn=MIX_TILE)
    buf16 = jnp.pad(pool_buf, ((0, 0), (POOL_HIST - pool_buf.shape[1], 0), (0, 0)))
    yb, tail = pool(z, 1, buf16, pool_w, i, pool_scale, start_pos=start_pos, tc=256)
    x, w_out_b = _res_proj(x, ya, yb.reshape(t, -1), w_out, i, tiles)
    g_a = n_slab * SLAB_GROUPS
    h_re = h_last[..., :SLAB_STATE].reshape(bsz, g_a, S5_P)
    h_im = h_last[..., SLAB_STATE:].reshape(bsz, g_a, S5_P)
    return x, h_re, h_im, tail[:, POOL_HIST - pool_buf.shape[1]:], (w_in_b, w_out_b)


def _odd_layer(x, bsz, seq, k_buf, v_buf, norm_g, w_in, w_out, i, qn, kn, ln_g, ln_b, w_s, b_s, n_heads, tiles):
    t, d = x.shape
    d_c = n_heads * LANES
    z, w_in_b = _in_proj(x, norm_g, w_in, i, tiles)
    z = z.reshape(bsz, seq, -1)
    if k_buf is None:
        att, k_new, v_new = attn_prompt(z, qn, kn, n_heads=n_heads)
    else:
        att, k_new, v_new = attn_sample(z, k_buf, v_buf, i * bsz, qn, kn, n_heads=n_heads)
    dd = ln_g.shape[0]
    sg, vn = sgu(z, (3 * d_c) // dd, ln_g, ln_b, w_s, b_s)
    x, w_out_b = _res_proj(x, att.reshape(t, d_c), sg.reshape(t, dd), w_out, i, tiles)
    hd = LANES
    return x, k_new.reshape(bsz, seq, n_heads, hd), v_new.reshape(bsz, seq, n_heads, hd), vn, (w_in_b, w_out_b)


def kernel(x_prompt, x_sample, state_s5_re, state_s5_im, state_pool, cache_k, cache_v, norm_mix, norm_ffn, ev_w_in, ev_w_out, s5_lambda_re, s5_lambda_im, s5_log_dt, s5_b_re, s5_b_im, s5_c_re, s5_c_im, s5_d, s5_w_glu, s5_b_glu, pool_w, pool_scale, od_w_in, od_w_out, q_norm, k_norm, sgu_ln_g, sgu_ln_b, sgu_w, sgu_b, ffn_w1, ffn_w3, ffn_w2):
    bp, lp, d = x_prompt.shape
    bs, ls, _ = x_sample.shape
    depth = norm_mix.shape[0]
    n_heads = cache_k.shape[3]
    xp = x_prompt.reshape(bp * lp, d)
    xs = x_sample.reshape(bs * ls, d)
    tiles_p = dict(cast=False, tm=1024, tm_glu=512, tm_out=1024, s5_split=False, z_dtype=BF16)
    tiles_s = dict(cast=True, tm_glu=bs * ls, s5_split=True)
    g_a, p_a = s5_lambda_re.shape[1:]
    w_glu_b, pool_w_b = s5_w_glu.astype(BF16), pool_w.astype(BF16)
    d_c = n_heads * LANES
    cache_k2 = cache_k.reshape(-1, cache_k.shape[2], d_c)
    cache_v2 = cache_v.reshape(-1, cache_v.shape[2], d_c)

    s5r_p, s5i_p, pool_p, k_p, v_p = [], [], [], [], []
    s5r_s, s5i_s, pool_s, k_s, v_s, sgu_s = [], [], [], [], [], []
    for l in range(depth):
        i = l // 2
        if l % 2 == 0:
            pw_re, pw_im, bb_re, bb_im = s5_prep(s5_lambda_re[i], s5_lambda_im[i], s5_log_dt[i], s5_b_re[i], s5_b_im[i])
            b_blk, c_blk = s5_block_matrices(bb_re, bb_im, s5_c_re[i], s5_c_im[i])
            s5p = (pw_re, pw_im, b_blk, c_blk)
            rest = (i, s5p, pool_w_b, pool_scale[i], w_glu_b, s5_b_glu[i], s5_d[i])
            xs, hr, hi, buf, (w_in_b, w_out_b) = _even_layer(
                xs, bs, ls, state_s5_re[i], state_s5_im[i], state_pool[i], PAST_LEN, norm_mix[l], ev_w_in, ev_w_out,
                *rest, tiles_s)
            s5r_s.append(hr); s5i_s.append(hi); pool_s.append(buf)
            zero_h = jnp.zeros((bp, g_a, p_a), F32)
            zero_buf = jnp.zeros((bp, state_pool.shape[2], state_pool.shape[3]), F32)
            xp, hr, hi, buf, _ = _even_layer(xp, bp, lp, zero_h, zero_h, zero_buf, 0, norm_mix[l], w_in_b, w_out_b,
                                             *rest, tiles_p)
            s5r_p.append(hr); s5i_p.append(hi); pool_p.append(buf)
        else:
            rest = (i, q_norm[i], k_norm[i], sgu_ln_g[i], sgu_ln_b[i], sgu_w[i], sgu_b[i], n_heads)
            xs, nk, nv, vrows, (w_in_b, w_out_b) = _odd_layer(xs, bs, ls, cache_k2, cache_v2, norm_mix[l], od_w_in,
                                                              od_w_out, *rest, tiles_s)
            k_s.append(nk); v_s.append(nv); sgu_s.append(vrows)
            xp, nk, nv, _, _ = _odd_layer(xp, bp, lp, None, None, norm_mix[l], w_in_b, w_out_b, *rest, tiles_p)
            k_p.append(nk); v_p.append(nv)
        xs, w1_b, w3_b, w2_b = ffn_cast(xs, norm_ffn[l], ffn_w1, ffn_w3, ffn_w2, l, tf=FFN_TILE)
        xp = ffn(xp, norm_ffn[l], w1_b, w3_b, w2_b, tm=512)
    return (xp.reshape(bp, lp, d), xs.reshape(bs, ls, d),
            jnp.stack(s5r_p), jnp.stack(s5i_p), jnp.stack(pool_p), jnp.stack(k_p), jnp.stack(v_p),
            jnp.stack(s5r_s), jnp.stack(s5i_s), jnp.stack(pool_s), jnp.stack(k_s), jnp.stack(v_s),
            jnp.stack(sgu_s))
```

```python
import functools
import math

import jax
import jax.numpy as jnp
from jax import lax
from jax.experimental import pallas as pl
from jax.experimental.pallas import tpu as pltpu

F32 = jnp.float32
BF16 = jnp.bfloat16

RMS_EPS = 1e-6
LN_EPS = 1e-5
NEG_INF = -1e30

LANES = 128
SUBLANES = 8
ATT_BLOCK = 128
CHUNK = 128
POOL_WINDOWS = (2, 4, 8, 16)
POOL_HIST = 16
BRANCHES = ((128, 1), (512, 4), (2048, 16))
S5_GRP = 16
S5_P = 64
SLAB_GROUPS = LANES // S5_GRP
SLAB_STATE = SLAB_GROUPS * S5_P
PAST_LEN = 8192
FFN_TILE = 256
MIX_TILE = 512


def _params(sem, vmem_mib):
    return pltpu.CompilerParams(dimension_semantics=sem, vmem_limit_bytes=vmem_mib << 20)


def _gelu(x):
    return 0.5 * x * (1.0 + lax.erf(x * (1.0 / math.sqrt(2.0))))


def _sigmoid(x):
    return 1.0 / (1.0 + jnp.exp(-x))


def _split_bf16(a):
    hi = a.astype(BF16)
    lo = (a - hi.astype(F32)).astype(BF16)
    return hi, lo


def _dot(a, b):
    return jnp.dot(a, b, preferred_element_type=F32)


def _dot_nt(a, b):
    return lax.dot_general(a, b, (((1,), (1,)), ((), ())), preferred_element_type=F32)


def _rms_rows_to(x_ref, g_ref, h_ref, rows):
    step = 16 if rows % 16 == 0 else rows

    def body(r, c):
        sl = pl.ds(pl.multiple_of(r * step, step), step)
        x = x_ref[sl, :]
        ms = jnp.mean(x * x, axis=-1, keepdims=True)
        h_ref[sl, :] = ((x * lax.rsqrt(ms + RMS_EPS)) * g_ref[...]).astype(h_ref.dtype)
        return c

    lax.fori_loop(0, rows // step, body, 0, unroll=min(4, rows // step))


def _norm_matmul_kernel(x_ref, g_ref, w_ref, o_ref, h_ref):
    @pl.when(pl.program_id(1) == 0)
    def _():
        _rms_rows_to(x_ref, g_ref, h_ref, x_ref.shape[0])

    o_ref[...] = _dot(h_ref[...], w_ref[...]).astype(o_ref.dtype)


def norm_matmul(x, g, w, *, tm, out_dtype=F32):
    t, d = x.shape
    n_t, _, tn = w.shape
    return pl.pallas_call(
        _norm_matmul_kernel,
        grid=(t // tm, n_t),
        in_specs=[
            pl.BlockSpec((tm, d), lambda i, j: (i, 0)),
            pl.BlockSpec((1, d), lambda i, j: (0, 0)),
            pl.BlockSpec((None, d, tn), lambda i, j: (j, 0, 0)),
        ],
        out_specs=pl.BlockSpec((tm, tn), lambda i, j: (i, j)),
        out_shape=jax.ShapeDtypeStruct((t, n_t * tn), out_dtype),
        scratch_shapes=[pltpu.VMEM((tm, d), BF16)],
        compiler_params=_params(("arbitrary", "arbitrary"), 58),
        name="norm_matmul",
    )(x, g.reshape(1, d), w)


def _norm_matmul_cast_kernel(x_ref, g_ref, w_ref, o_ref, wb_ref, h_ref):
    @pl.when(pl.program_id(0) == 0)
    def _():
        _rms_rows_to(x_ref, g_ref, h_ref, x_ref.shape[0])

    wb_ref[...] = w_ref[...].astype(BF16)
    o_ref[...] = _dot(h_ref[...], wb_ref[...]).astype(o_ref.dtype)


def norm_matmul_cast(x, g, w, layer, *, tn):
    t, d = x.shape
    n = w.shape[2]
    return pl.pallas_call(
        _norm_matmul_cast_kernel,
        grid=(n // tn,),
        in_specs=[
            pl.BlockSpec((t, d), lambda j: (0, 0)),
            pl.BlockSpec((1, d), lambda j: (0, 0)),
            pl.BlockSpec((None, d, tn), lambda j: (layer, 0, j)),
        ],
        out_specs=[
            pl.BlockSpec((t, tn), lambda j: (0, j)),
            pl.BlockSpec((None, d, tn), lambda j: (j, 0, 0)),
        ],
        out_shape=[
            jax.ShapeDtypeStruct((t, n), F32),
            jax.ShapeDtypeStruct((n // tn, d, tn), BF16),
        ],
        scratch_shapes=[pltpu.VMEM((t, d), BF16)],
        compiler_params=_params(("arbitrary",), 48),
        name="norm_matmul_cast",
    )(x, g.reshape(1, d), w)


FFN_OUT_CHUNK = 512


def _ffn_step(first, x_ref, g_ref, w1_ref, w3_ref, w2_ref, o_ref, h_ref):
    @pl.when(first)
    def _():
        _rms_rows_to(x_ref, g_ref, h_ref, x_ref.shape[0])
        o_ref[...] = x_ref[...]

    h = h_ref[...]
    a = _dot(h, w1_ref[...])
    b = _dot(h, w3_ref[...])
    u = ((a * _sigmoid(a)) * b).astype(BF16)
    for c in range(0, o_ref.shape[1], FFN_OUT_CHUNK):
        o_ref[:, c:c + FFN_OUT_CHUNK] += _dot(u, w2_ref[:, c:c + FFN_OUT_CHUNK])


def _ffn_kernel(x_ref, g_ref, w1_ref, w3_ref, w2_ref, o_ref, h_ref):
    _ffn_step(pl.program_id(1) == 0, x_ref, g_ref, w1_ref, w3_ref, w2_ref, o_ref, h_ref)


def ffn(x, g, w1, w3, w2, *, tm):
    t, d = x.shape
    n_f, _, tf = w1.shape
    return pl.pallas_call(
        _ffn_kernel,
        grid=(t // tm, n_f),
        in_specs=[
            pl.BlockSpec((tm, d), lambda i, j: (i, 0)),
            pl.BlockSpec((1, d), lambda i, j: (0, 0)),
            pl.BlockSpec((None, d, tf), lambda i, j: (j, 0, 0)),
            pl.BlockSpec((None, d, tf), lambda i, j: (j, 0, 0)),
            pl.BlockSpec((tf, d), lambda i, j: (j, 0)),
        ],
        out_specs=pl.BlockSpec((tm, d), lambda i, j: (i, 0)),
        out_shape=jax.ShapeDtypeStruct((t, d), F32),
        scratch_shapes=[pltpu.VMEM((tm, d), BF16)],
        compiler_params=_params(("arbitrary", "arbitrary"), 56),
        name="ffn",
    )(x, g.reshape(1, d), w1, w3, w2)


def _ffn_cast_kernel(x_ref, g_ref, w1_ref, w3_ref, w2_ref, o_ref, w1b_ref, w3b_ref, w2b_ref, h_ref):
    w1b_ref[...] = w1_ref[...].astype(BF16)
    w3b_ref[...] = w3_ref[...].astype(BF16)
    w2b_ref[...] = w2_ref[...].astype(BF16)
    _ffn_step(pl.program_id(0) == 0, x_ref, g_ref, w1b_ref, w3b_ref, w2b_ref, o_ref, h_ref)


def ffn_cast(x, g, w1, w3, w2, layer, *, tf):
    t, d = x.shape
    f = w1.shape[2]
    return pl.pallas_call(
        _ffn_cast_kernel,
        grid=(f // tf,),
        in_specs=[
            pl.BlockSpec((t, d), lambda j: (0, 0)),
            pl.BlockSpec((1, d), lambda j: (0, 0)),
            pl.BlockSpec((None, d, tf), lambda j: (layer, 0, j)),
            pl.BlockSpec((None, d, tf), lambda j: (layer, 0, j)),
            pl.BlockSpec((None, tf, d), lambda j: (layer, j, 0)),
        ],
        out_specs=[
            pl.BlockSpec((t, d), lambda j: (0, 0)),
            pl.BlockSpec((None, d, tf), lambda j: (j, 0, 0)),
            pl.BlockSpec((None, d, tf), lambda j: (j, 0, 0)),
            pl.BlockSpec((tf, d), lambda j: (j, 0)),
        ],
        out_shape=[
            jax.ShapeDtypeStruct((t, d), F32),
            jax.ShapeDtypeStruct((f // tf, d, tf), BF16),
            jax.ShapeDtypeStruct((f // tf, d, tf), BF16),
            jax.ShapeDtypeStruct((f, d), BF16),
        ],
        scratch_shapes=[pltpu.VMEM((t, d), BF16)],
        compiler_params=_params(("arbitrary",), 48),
        name="ffn_cast",
    )(x, g.reshape(1, d), w1, w3, w2)


def _out_proj_kernel(x_ref, a_ref, b_ref, wa_ref, wb_ref, o_ref):
    o_ref[...] = x_ref[...] + _dot(a_ref[...], wa_ref[...]) + _dot(b_ref[...], wb_ref[...])


def out_proj(x, a, b, w_pair, *, tm):
    t, d = x.shape
    wa, wb = w_pair
    n_t, k, tn = wa.shape
    assert a.shape[1] == k and b.shape[1] == k and wb.shape == wa.shape
    return pl.pallas_call(
        _out_proj_kernel,
        grid=(t // tm, n_t),
        in_specs=[
            pl.BlockSpec((tm, tn), lambda i, j: (i, j)),
            pl.BlockSpec((tm, k), lambda i, j: (i, 0)),
            pl.BlockSpec((tm, k), lambda i, j: (i, 0)),
            pl.BlockSpec((None, k, tn), lambda i, j: (j, 0, 0)),
            pl.BlockSpec((None, k, tn), lambda i, j: (j, 0, 0)),
        ],
        out_specs=pl.BlockSpec((tm, tn), lambda i, j: (i, j)),
        out_shape=jax.ShapeDtypeStruct((t, d), F32),
        compiler_params=_params(("arbitrary", "arbitrary"), 48),
        name="out_proj",
    )(x, a, b, wa, wb)


def _out_proj_cast_kernel(x_ref, a_ref, b_ref, wa_ref, wb_ref, o_ref, wab_ref, wbb_ref):
    wab_ref[...] = wa_ref[...].astype(BF16)
    wbb_ref[...] = wb_ref[...].astype(BF16)
    o_ref[...] = x_ref[...] + _dot(a_ref[...], wab_ref[...]) + _dot(b_ref[...], wbb_ref[...])


def out_proj_cast(x, a, b, w, layer, *, tn):
    t, d = x.shape
    k = a.shape[1]
    assert b.shape[1] == k and w.shape[1] == 2 * k
    out, wa_b, wb_b = pl.pallas_call(
        _out_proj_cast_kernel,
        grid=(d // tn,),
        in_specs=[
            pl.BlockSpec((t, tn), lambda j: (0, j)),
            pl.BlockSpec((t, k), lambda j: (0, 0)),
            pl.BlockSpec((t, k), lambda j: (0, 0)),
            pl.BlockSpec((None, k, tn), lambda j: (layer, 0, j)),
            pl.BlockSpec((None, k, tn), lambda j: (layer, 1, j)),
        ],
        out_specs=[
            pl.BlockSpec((t, tn), lambda j: (0, j)),
            pl.BlockSpec((None, k, tn), lambda j: (j, 0, 0)),
            pl.BlockSpec((None, k, tn), lambda j: (j, 0, 0)),
        ],
        out_shape=[
            jax.ShapeDtypeStruct((t, d), F32),
            jax.ShapeDtypeStruct((d // tn, k, tn), BF16),
            jax.ShapeDtypeStruct((d // tn, k, tn), BF16),
        ],
        compiler_params=_params(("arbitrary",), 48),
        name="out_proj_cast",
    )(x, a, b, w, w)
    return out, (wa_b, wb_b)


def _s5_prep_kernel(lr_ref, li_ref, ldt_ref, lrx_ref, lix_ref, ldtx_ref, br_ref, bi_ref,
                    pwr_ref, pwi_ref, bbr_ref, bbi_ref):
    dt = jnp.exp(ldt_ref[...])
    mag = jnp.exp(lr_ref[...] * dt)
    ang = li_ref[...] * dt
    p_r, p_i = mag * jnp.cos(ang), mag * jnp.sin(ang)
    c_r, c_i = p_r, p_i
    pwr_ref[0], pwi_ref[0] = c_r, c_i
    for j in range(1, SUBLANES):
        c_r, c_i = c_r * p_r - c_i * p_i, c_r * p_i + c_i * p_r
        pwr_ref[j], pwi_ref[j] = c_r, c_i
    lr, li = lrx_ref[...], lix_ref[...]
    dtx = jnp.exp(ldtx_ref[...])
    magx = jnp.exp(lr * dtx)
    angx = li * dtx
    nr, ni = magx * jnp.cos(angx) - 1.0, magx * jnp.sin(angx)
    den = lr * lr + li * li
    qr = (nr * lr + ni * li) / den
    qi = (ni * lr - nr * li) / den
    br, bi = br_ref[...], bi_ref[...]
    bbr_ref[...] = qr * br - qi * bi
    bbi_ref[...] = qr * bi + qi * br


def s5_prep(lam_re, lam_im, log_dt, b_re, b_im):
    g, p = lam_re.shape
    h = b_re.shape[2]
    n_slab = g // SLAB_GROUPS
    slab = lambda a: a.reshape(n_slab, SLAB_GROUPS * p)
    rep = lambda a: jnp.repeat(a, h, axis=1)
    ldt_gp = jnp.broadcast_to(log_dt[:, None], (g, p))
    outs = pl.pallas_call(
        _s5_prep_kernel,
        out_shape=[
            jax.ShapeDtypeStruct((SUBLANES, n_slab, SLAB_GROUPS * p), F32),
            jax.ShapeDtypeStruct((SUBLANES, n_slab, SLAB_GROUPS * p), F32),
            jax.ShapeDtypeStruct((g, p * h), F32),
            jax.ShapeDtypeStruct((g, p * h), F32),
        ],
        name="s5_prep",
    )(slab(lam_re), slab(lam_im), slab(ldt_gp), rep(lam_re), rep(lam_im), rep(ldt_gp),
      b_re.reshape(g, p * h), b_im.reshape(g, p * h))
    pw_re, pw_im, bb_re, bb_im = outs
    pw_re = jnp.transpose(pw_re, (1, 0, 2))
    pw_im = jnp.transpose(pw_im, (1, 0, 2))
    return pw_re, pw_im, bb_re.reshape(g, p, h), bb_im.reshape(g, p, h)


def s5_block_matrices(bb_re, bb_im, c_re, c_im):
    g, p, h = bb_re.shape
    n_slab = g // SLAB_GROUPS
    eye = jnp.eye(SLAB_GROUPS, dtype=F32)

    def in_map(bb):
        t = bb.reshape(n_slab, SLAB_GROUPS, p, h)
        return jnp.einsum("kgph,gj->kghjp", t, eye).reshape(n_slab, SLAB_GROUPS * h, SLAB_GROUPS * p)

    def out_map(c):
        t = c.reshape(n_slab, SLAB_GROUPS, h, p)
        return jnp.einsum("kghp,gj->kgpjh", t, eye).reshape(n_slab, SLAB_GROUPS * p, SLAB_GROUPS * h)

    b_blk = jnp.concatenate([in_map(bb_re), in_map(bb_im)], axis=2)
    c_blk = jnp.concatenate([out_map(c_re), -out_map(c_im)], axis=1)
    return b_blk, c_blk


def _s5_scan_kernel(u_ref, bblk_ref, cblk_ref, pwr_ref, pwi_ref, d_ref, h0_ref,
                    y_ref, hl_ref, h_scr, bh_scr, bl_scr, ch_scr, *, seq, row_chunk, split_in):
    ns = SLAB_STATE

    @pl.when(pl.program_id(1) == 0)
    def _():
        bh, bl = _split_bf16(bblk_ref[...])
        bh_scr[...], bl_scr[...] = bh, bl
        ch_scr[...] = cblk_ref[...].astype(BF16)

    n_chunks = seq // row_chunk

    def proj_in(r, c):
        sl = pl.ds(pl.multiple_of(r * row_chunk, row_chunk), row_chunk)
        if split_in:
            uh, ul = _split_bf16(u_ref[sl, :].astype(F32))
            h_scr[sl, :] = _dot(uh, bh_scr[...]) + _dot(ul, bh_scr[...]) + _dot(uh, bl_scr[...])
        else:
            h_scr[sl, :] = _dot(u_ref[sl, :].astype(BF16), bh_scr[...])
        return c

    lax.fori_loop(0, n_chunks, proj_in, 0)

    rowid = lax.broadcasted_iota(jnp.int32, (SUBLANES, LANES), 0)
    for c in range(ns // LANES):
        re_l = slice(c * LANES, (c + 1) * LANES)
        im_l = slice(ns + c * LANES, ns + (c + 1) * LANES)
        p_r, p_i = pwr_ref[:, re_l], pwi_ref[:, re_l]
        steps = []
        for dist in (1, 2, 4):
            a_r = jnp.where(rowid >= dist, jnp.broadcast_to(p_r[dist - 1:dist], (SUBLANES, LANES)), 0.0)
            a_i = jnp.where(rowid >= dist, jnp.broadcast_to(p_i[dist - 1:dist], (SUBLANES, LANES)), 0.0)
            steps.append((dist, a_r, a_i))
        c_r = jnp.broadcast_to(h0_ref[:, re_l], (SUBLANES, LANES))
        c_i = jnp.broadcast_to(h0_ref[:, im_l], (SUBLANES, LANES))

        last = slice(SUBLANES - 1, SUBLANES)
        full = (SUBLANES, LANES)
        p8_r, p8_i = jnp.broadcast_to(p_r[last], full), jnp.broadcast_to(p_i[last], full)
        n_groups = seq // SUBLANES
        per_it = min(4, n_groups)

        def scan_rows(it, carry, re_l=re_l, im_l=im_l, p_r=p_r, p_i=p_i, p8_r=p8_r, p8_i=p8_i, steps=steps):
            base = pl.multiple_of(it * (per_it * SUBLANES), per_it * SUBLANES)
            sls = [pl.ds(base + j * SUBLANES, SUBLANES) for j in range(per_it)]
            loc = []
            for sl in sls:
                r, i = h_scr[sl, re_l], h_scr[sl, im_l]
                for dist, a_r, a_i in steps:
                    s_r, s_i = pltpu.roll(r, dist, 0), pltpu.roll(i, dist, 0)
                    r, i = r + (s_r * a_r - s_i * a_i), i + (s_r * a_i + s_i * a_r)
                loc.append((r, i))
            c_r, c_i = carry
            outs = []
            for r, i in loc:
                outs.append((r + (c_r * p_r - c_i * p_i), i + (c_r * p_i + c_i * p_r)))
                e_r, e_i = jnp.broadcast_to(r[last], full), jnp.broadcast_to(i[last], full)
                c_r, c_i = e_r + (c_r * p8_r - c_i * p8_i), e_i + (c_r * p8_i + c_i * p8_r)
            for sl, (r, i) in zip(sls, outs):
                h_scr[sl, re_l], h_scr[sl, im_l] = r, i
            return c_r, c_i

        c_r, c_i = lax.fori_loop(0, n_groups // per_it, scan_rows, (c_r, c_i))
        hl_ref[:, re_l] = c_r[0:1]
        hl_ref[:, im_l] = c_i[0:1]

    def proj_out(r, c):
        sl = pl.ds(pl.multiple_of(r * row_chunk, row_chunk), row_chunk)
        y = _dot(h_scr[sl, :].astype(BF16), ch_scr[...]) + d_ref[...] * u_ref[sl, :].astype(F32)
        y_ref[sl, :] = _gelu(y)
        return c

    lax.fori_loop(0, n_chunks, proj_out, 0)


def s5_scan(z, col0, b_blk, c_blk, pw_re, pw_im, d_skip, h0, *, split_in):
    bsz, seq, _ = z.shape
    n_slab = b_blk.shape[0]
    ns2 = 2 * SLAB_STATE
    row_chunk = min(seq, 1024)
    kern = functools.partial(_s5_scan_kernel, seq=seq, row_chunk=row_chunk, split_in=split_in)
    return pl.pallas_call(
        kern,
        grid=(n_slab, bsz),
        in_specs=[
            pl.BlockSpec((None, seq, LANES), lambda k, b: (b, 0, col0 + k)),
            pl.BlockSpec((None, LANES, ns2), lambda k, b: (k, 0, 0)),
            pl.BlockSpec((None, ns2, LANES), lambda k, b: (k, 0, 0)),
            pl.BlockSpec((None, SUBLANES, SLAB_STATE), lambda k, b: (k, 0, 0)),
            pl.BlockSpec((None, SUBLANES, SLAB_STATE), lambda k, b: (k, 0, 0)),
            pl.BlockSpec((1, LANES), lambda k, b: (0, k)),
            pl.BlockSpec((None, None, 1, ns2), lambda k, b: (b, k, 0, 0)),
        ],
        out_specs=[
            pl.BlockSpec((None, seq, LANES), lambda k, b: (b, 0, k)),
            pl.BlockSpec((None, None, 1, ns2), lambda k, b: (b, k, 0, 0)),
        ],
        out_shape=[
            jax.ShapeDtypeStruct((bsz, seq, n_slab * LANES), F32),
            jax.ShapeDtypeStruct((bsz, n_slab, 1, ns2), F32),
        ],
        scratch_shapes=[
            pltpu.VMEM((seq, ns2), F32),
            pltpu.VMEM((LANES, ns2), BF16), pltpu.VMEM((LANES, ns2), BF16),
            pltpu.VMEM((ns2, LANES), BF16),
        ],
        compiler_params=_params(("arbitrary", "arbitrary"), 40),
        name="s5_scan",
    )(z, b_blk, c_blk, pw_re, pw_im, d_skip.reshape(1, -1), h0)


def _glu_kernel(yk_ref, yj_ref, w_ref, b_ref, o_ref, yb_scr):
    @pl.when(pl.program_id(1) == 0)
    def _():
        yb_scr[...] = yk_ref[...].astype(BF16)

    gate = _dot(yb_scr[...], w_ref[...]) + b_ref[...]
    o_ref[...] = (yj_ref[...] * _sigmoid(gate)).astype(o_ref.dtype)


def glu(y, w, b, layer, *, tm, tn):
    t, d = y.shape
    tm = min(tm, t)
    return pl.pallas_call(
        _glu_kernel,
        grid=(t // tm, d // tn),
        in_specs=[
            pl.BlockSpec((tm, d), lambda i, j: (i, 0)),
            pl.BlockSpec((tm, tn), lambda i, j: (i, j)),
            pl.BlockSpec((None, d, tn), lambda i, j: (layer, 0, j)),
            pl.BlockSpec((1, tn), lambda i, j: (0, j)),
        ],
        out_specs=pl.BlockSpec((tm, tn), lambda i, j: (i, j)),
        out_shape=jax.ShapeDtypeStruct((t, d), BF16),
        scratch_shapes=[pltpu.VMEM((tm, d), BF16)],
        compiler_params=_params(("arbitrary", "arbitrary"), 40),
        name="glu",
    )(y, y, w, b.reshape(1, d))


def _pool_kernel(u_ref, buf_ref, w_ref, s_ref, y_ref, tail_ref, ext_scr, *, tc, start_pos, cg):
    c = pl.program_id(1)

    @pl.when(c == 0)
    def _():
        ext_scr[0:POOL_HIST, :] = buf_ref[...]

    ext_scr[POOL_HIST:POOL_HIST + tc, :] = u_ref[...].astype(F32)
    pos = start_pos + c * tc + lax.broadcasted_iota(jnp.int32, (tc, 1), 0)
    for g, win in enumerate(POOL_WINDOWS):
        cols = slice(g * cg, (g + 1) * cg)
        x = ext_scr[:, cols]
        acc, dist = x, 1
        while dist < win:
            acc = acc + pltpu.roll(acc, dist, 0)
            dist *= 2
        wsum = acc[POOL_HIST:, :]
        cnt = jnp.minimum(pos + 1, win).astype(F32)
        zg = wsum * (1.0 / cnt) - x[POOL_HIST:, :]
        y = _dot(zg.astype(BF16), w_ref[g]) * s_ref[:, cols]
        y_ref[:, cols] = y.astype(y_ref.dtype)

    tail = ext_scr[tc:tc + POOL_HIST, :]
    ext_scr[0:POOL_HIST, :] = tail

    @pl.when(c == pl.num_programs(1) - 1)
    def _():
        tail_ref[...] = tail


def pool(z, colblk, buf16, w, layer, scale, *, start_pos, tc):
    bsz, seq, _ = z.shape
    _, n_g, cg, _ = w.shape
    db = n_g * cg
    tc = min(tc, seq)
    kern = functools.partial(_pool_kernel, tc=tc, start_pos=start_pos, cg=cg)
    return pl.pallas_call(
        kern,
        grid=(bsz, seq // tc),
        in_specs=[
            pl.BlockSpec((None, tc, db), lambda b, c: (b, c, colblk)),
            pl.BlockSpec((None, POOL_HIST, db), lambda b, c: (b, 0, 0)),
            pl.BlockSpec((None, n_g, cg, cg), lambda b, c: (layer, 0, 0, 0)),
            pl.BlockSpec((1, db), lambda b, c: (0, 0)),
        ],
        out_specs=[
            pl.BlockSpec((None, tc, db), lambda b, c: (b, c, 0)),
            pl.BlockSpec((None, POOL_HIST, db), lambda b, c: (b, 0, 0)),
        ],
        out_shape=[
            jax.ShapeDtypeStruct((bsz, seq, db), BF16),
            jax.ShapeDtypeStruct((bsz, POOL_HIST, db), F32),
        ],
        scratch_shapes=[pltpu.VMEM((POOL_HIST + tc, db), F32)],
        compiler_params=_params(("arbitrary", "arbitrary"), 40),
        name="pool",
    )(z, buf16, w, scale.reshape(1, db))


def _head_rms(x, g):
    ms = jnp.mean(x * x, axis=-1, keepdims=True)
    return (x * lax.rsqrt(ms + RMS_EPS)) * g


def _combine(os_, lses):
    m = jnp.maximum(jnp.maximum(lses[0], lses[1]), lses[2])
    ws = [jnp.exp(l - m) for l in lses]
    tot = ws[0] + ws[1] + ws[2]
    return (ws[0] * os_[0] + ws[1] * os_[1] + ws[2] * os_[2]) / tot


def _attn_prompt_kernel(q_ref, k_ref, v_ref, qn_ref, kn_ref, att_ref, ko_ref, vo_ref,
                        qs_scr, qf_scr, kf_scr, vf_scr, qd_scr, kd_scr, vd_scr, s_scr, p_scr, m_scr, o_scr, l_scr,
                        *, seq, scale):
    blk = ATT_BLOCK
    rows = 256
    n_all = seq // blk

    def prep(r, c):
        sl = pl.ds(pl.multiple_of(r * rows, rows), rows)
        qs_scr[sl, :] = _head_rms(q_ref[sl, :].astype(F32), qn_ref[...]) * scale
        ko_ref[sl, :] = _head_rms(k_ref[sl, :].astype(F32), kn_ref[...])
        vo_ref[sl, :] = v_ref[sl, :].astype(F32)
        return c

    lax.fori_loop(0, seq // rows, prep, 0, unroll=2)
    kd_scr[0:blk, :] = jnp.zeros((blk, LANES), BF16)
    vd_scr[0:blk, :] = jnp.zeros((blk, 2 * LANES), BF16)
    vd_scr[blk:, LANES:] = jnp.ones((seq, LANES), BF16)

    qi = lax.broadcasted_iota(jnp.int32, (blk, blk), 0)
    kj = lax.broadcasted_iota(jnp.int32, (blk, blk), 1)
    cur_ok = kj <= qi
    prev_ok = kj >= qi
    band_ok = jnp.concatenate([prev_ok, cur_ok], axis=1)
    in_cur = lax.broadcasted_iota(jnp.int32, (blk, 2 * blk), 1) >= blk

    for g, (window, dil) in enumerate(BRANCHES):
        n_blk = seq // (dil * blk)
        col0 = 0 if n_blk > 1 else LANES

        def place(idx, dil=dil, n_blk=n_blk):
            res = idx // n_blk
            n = idx - res * n_blk
            start = res + n * (dil * blk)
            nat = pl.ds(start, blk, stride=dil) if dil > 1 else pl.ds(pl.multiple_of(start, blk), blk)
            cur = pl.ds(pl.multiple_of(idx * blk, blk), blk)
            kcur = pl.ds(pl.multiple_of((idx + 1) * blk, blk), blk)
            kwin = pl.ds(pl.multiple_of(idx * blk, blk), 2 * blk)
            return nat, cur, kcur, kwin, n

        keep_f32 = dil == BRANCHES[1][1]
        two_level = g == 2 and dil == BRANCHES[1][1] ** 2

        def gather(idx, c, place=place, dil=dil, n_blk=n_blk, keep_f32=keep_f32, two_level=two_level):
            nat, cur, kcur, _, n = place(idx)
            if two_level:
                mid = BRANCHES[1][1]
                res = idx // n_blk
                start = (res % mid) * (seq // mid) + res // mid + n * (mid * blk)
                src = pl.ds(start, blk, stride=mid)
                q, k, v = qf_scr[src, :], kf_scr[src, :], vf_scr[src, :]
            else:
                q, k, v = qs_scr[nat, :], ko_ref[nat, :], vo_ref[nat, :]
            if keep_f32:
                qf_scr[cur, :], kf_scr[cur, :], vf_scr[cur, :] = q, k, v
            qd_scr[cur, :] = q.astype(BF16)
            kd_scr[kcur, :] = k.astype(BF16)
            vd_scr[kcur, 0:LANES] = v.astype(BF16)
            return c

        lax.fori_loop(0, n_all, gather, 0, unroll=4)

        def scores(idx, c, place=place, n_blk=n_blk):
            _, cur, kcur, kwin, n = place(idx)
            q = qd_scr[cur, :]
            if n_blk > 1:
                ok = jnp.logical_and(band_ok, jnp.logical_or(in_cur, n > 0))
                s_scr[idx] = jnp.where(ok, _dot_nt(q, kd_scr[kwin, :]), NEG_INF)
            else:
                s_scr[idx, :, LANES:] = jnp.where(cur_ok, _dot_nt(q, kd_scr[kcur, :]), NEG_INF)
            return c

        lax.fori_loop(0, n_all, scores, 0, unroll=8)

        def softmax(idx, c, col0=col0):
            s = s_scr[idx, :, col0:]
            m = jnp.max(s, axis=-1, keepdims=True)
            p_scr[idx, :, col0:] = jnp.exp(s - m).astype(BF16)
            m_scr[idx] = jnp.broadcast_to(m, (blk, LANES))
            return c

        lax.fori_loop(0, n_all, softmax, 0, unroll=4)

        def values(idx, c, g=g, place=place, n_blk=n_blk):
            nat, _, kcur, kwin, _ = place(idx)
            if n_blk > 1:
                ov = _dot(p_scr[idx], vd_scr[kwin, :])
            else:
                ov = _dot(p_scr[idx, :, LANES:], vd_scr[kcur, :])
            l = ov[:, LANES:]
            o_scr[g, nat, :] = ov[:, 0:LANES] / l
            l_scr[g, nat, :] = m_scr[idx] + jnp.log(l)
            return c

        lax.fori_loop(0, n_all, values, 0, unroll=8)

    def comb(r, c):
        sl = pl.ds(pl.multiple_of(r * rows, rows), rows)
        out = _combine([o_scr[g, sl, :] for g in range(3)], [l_scr[g, sl, :] for g in range(3)])
        att_ref[sl, :] = out.astype(att_ref.dtype)
        return c

    lax.fori_loop(0, seq // rows, comb, 0)


def attn_prompt(z, qn, kn, *, n_heads):
    bsz, seq, _ = z.shape
    assert seq % (BRANCHES[-1][1] * ATT_BLOCK) == 0
    hd = LANES
    kern = functools.partial(_attn_prompt_kernel, seq=seq, scale=hd ** -0.5)
    blk = lambda off: pl.BlockSpec((None, seq, hd), lambda b, h: (b, 0, off + h))
    return pl.pallas_call(
        kern,
        grid=(bsz, n_heads),
        in_specs=[blk(0), blk(n_heads), blk(2 * n_heads),
                  pl.BlockSpec((1, hd), lambda b, h: (0, 0)), pl.BlockSpec((1, hd), lambda b, h: (0, 0))],
        out_specs=[blk(0), blk(0), blk(0)],
        out_shape=[
            jax.ShapeDtypeStruct((bsz, seq, n_heads * hd), BF16),
            jax.ShapeDtypeStruct((bsz, seq, n_heads * hd), F32),
            jax.ShapeDtypeStruct((bsz, seq, n_heads * hd), F32),
        ],
        scratch_shapes=[
            pltpu.VMEM((seq, hd), F32),
            pltpu.VMEM((seq, hd), F32), pltpu.VMEM((seq, hd), F32), pltpu.VMEM((seq, hd), F32),
            pltpu.VMEM((seq, hd), BF16), pltpu.VMEM((seq + ATT_BLOCK, hd), BF16),
            pltpu.VMEM((seq + ATT_BLOCK, 2 * hd), BF16),
            pltpu.VMEM((seq // ATT_BLOCK, ATT_BLOCK, 2 * ATT_BLOCK), F32),
            pltpu.VMEM((seq // ATT_BLOCK, ATT_BLOCK, 2 * ATT_BLOCK), BF16),
            pltpu.VMEM((seq // ATT_BLOCK, ATT_BLOCK, hd), F32),
            pltpu.VMEM((3, seq, hd), F32),
            pltpu.VMEM((3, seq, hd), F32),
        ],
        compiler_params=_params(("arbitrary", "arbitrary"), 40),
        name="attn_prompt",
    )(z, z, z, qn.reshape(1, hd), kn.reshape(1, hd))


def _attn_sample_kernel(q_ref, k_ref, v_ref, ck_ref, cv_ref, qn_ref, kn_ref, att_ref, ko_ref, vo_ref,
                        q_scr, kn_scr, vn_scr, *, s_new, n_buf, scale):
    pad = q_scr.shape[0]
    q_scr[...] = jnp.zeros_like(q_scr)
    kn_scr[...] = jnp.zeros_like(kn_scr)
    vn_scr[...] = jnp.zeros_like(vn_scr)
    k_new = _head_rms(k_ref[...], kn_ref[...])
    v_new = v_ref[...]
    ko_ref[...] = k_new
    vo_ref[...] = v_new
    q_scr[0:s_new, :] = _head_rms(q_ref[...], qn_ref[...])
    kn_scr[0:s_new, :] = k_new
    vn_scr[0:s_new, :] = v_new

    q = q_scr[...].astype(BF16)
    ck = ck_ref[...].astype(BF16)
    cv = cv_ref[...].astype(BF16)
    s_c = _dot_nt(q, ck) * scale
    s_n = _dot_nt(q, kn_scr[...].astype(BF16)) * scale
    qi_c = lax.broadcasted_iota(jnp.int32, (pad, n_buf), 0)
    kj_c = lax.broadcasted_iota(jnp.int32, (pad, n_buf), 1)
    dist_c = n_buf + qi_c - kj_c
    qi_n = lax.broadcasted_iota(jnp.int32, (pad, pad), 0)
    kj_n = lax.broadcasted_iota(jnp.int32, (pad, pad), 1)
    dist_n = qi_n - kj_n
    new_ok = jnp.logical_and(dist_n >= 0, kj_n < s_new)
    outs, lses = [], []
    for window, dil in BRANCHES:
        ok_c = jnp.logical_and((dist_c & (dil - 1)) == 0, dist_c <= window)
        ok_n = jnp.logical_and(new_ok, (dist_n & (dil - 1)) == 0)
        m_c = jnp.where(ok_c, s_c, NEG_INF)
        m_n = jnp.where(ok_n, s_n, NEG_INF)
        m = jnp.maximum(jnp.max(m_c, axis=-1, keepdims=True), jnp.max(m_n, axis=-1, keepdims=True))
        p_c, p_n = jnp.exp(m_c - m), jnp.exp(m_n - m)
        l = jnp.sum(p_c, axis=-1, keepdims=True) + jnp.sum(p_n, axis=-1, keepdims=True)
        o = _dot(p_c.astype(BF16), cv) + _dot(p_n.astype(BF16), vn_scr[...].astype(BF16))
        outs.append(o / l)
        lses.append(jnp.broadcast_to(m + jnp.log(l), (pad, LANES)))
    att_ref[...] = _combine(outs, lses)[0:s_new, :].astype(att_ref.dtype)


def attn_sample(z, cache_k, cache_v, row0, qn, kn, *, n_heads):
    bsz, s_new, _ = z.shape
    n_buf = cache_k.shape[1]
    assert n_buf >= BRANCHES[-1][0]
    hd = LANES
    pad = 16
    kern = functools.partial(_attn_sample_kernel, s_new=s_new, n_buf=n_buf, scale=hd ** -0.5)
    blk = lambda off: pl.BlockSpec((None, s_new, hd), lambda b, h: (b, 0, off + h))
    cblk = pl.BlockSpec((None, n_buf, hd), lambda b, h: (row0 + b, 0, h))
    vec = pl.BlockSpec((1, hd), lambda b, h: (0, 0))
    return pl.pallas_call(
        kern,
        grid=(bsz, n_heads),
        in_specs=[blk(0), blk(n_heads), blk(2 * n_heads), cblk, cblk, vec, vec],
        out_specs=[blk(0), blk(0), blk(0)],
        out_shape=[
            jax.ShapeDtypeStruct((bsz, s_new, n_heads * hd), BF16),
            jax.ShapeDtypeStruct((bsz, s_new, n_heads * hd), F32),
            jax.ShapeDtypeStruct((bsz, s_new, n_heads * hd), F32),
        ],
        scratch_shapes=[pltpu.VMEM((pad, hd), F32), pltpu.VMEM((pad, hd), F32), pltpu.VMEM((pad, hd), F32)],
        compiler_params=_params(("arbitrary", "arbitrary"), 40),
        name="attn_sample",
    )(z, z, z, cache_k, cache_v, qn.reshape(1, hd), kn.reshape(1, hd))


def _sgu_kernel(gu_ref, gv_ref, lg_ref, lb_ref, w_ref, bt_ref, o_ref, vn_ref, vb_scr, *, rows, n_g, cd):
    t = w_ref.shape[1]
    gv = _gelu(gv_ref[...].astype(F32))
    mu = jnp.mean(gv, axis=-1, keepdims=True)
    xc = gv - mu
    var = jnp.mean(xc * xc, axis=-1, keepdims=True)
    vn = (xc * lax.rsqrt(var + LN_EPS)) * lg_ref[...] + lb_ref[...]
    vn_ref[...] = vn
    if rows < t:
        vb_scr[...] = jnp.zeros_like(vb_scr)
    vb_scr[0:rows, :] = vn.astype(BF16)
    ri = lax.broadcasted_iota(jnp.int32, (t, t), 0)
    ci = lax.broadcasted_iota(jnp.int32, (t, t), 1)
    for g in range(n_g):
        cols = slice(g * cd, (g + 1) * cd)
        wg = jnp.where(ri >= ci, w_ref[g], 0.0).astype(BF16)
        mixed = _dot(wg, vb_scr[:, cols])[0:rows, :] + bt_ref[:, g:g + 1]
        o_ref[:, cols] = (_gelu(gu_ref[:, cols].astype(F32)) * mixed).astype(o_ref.dtype)


def sgu(z, colblk_u, ln_g, ln_b, w_s, b_s):
    bsz, seq, _ = z.shape
    n_g = w_s.shape[0]
    dd = ln_g.shape[0]
    cd = dd // n_g
    t = min(seq, CHUNK)
    tp = max(t, LANES)
    w = jnp.pad(w_s[:, :t, :t], ((0, 0), (0, tp - t), (0, tp - t)))
    bt = jnp.transpose(b_s[:, :t])
    kern = functools.partial(_sgu_kernel, rows=t, n_g=n_g, cd=cd)
    return pl.pallas_call(
        kern,
        grid=(bsz, seq // t),
        in_specs=[
            pl.BlockSpec((None, t, dd), lambda b, c: (b, c, colblk_u)),
            pl.BlockSpec((None, t, dd), lambda b, c: (b, c, colblk_u + 1)),
            pl.BlockSpec((1, dd), lambda b, c: (0, 0)),
            pl.BlockSpec((1, dd), lambda b, c: (0, 0)),
            pl.BlockSpec((n_g, tp, tp), lambda b, c: (0, 0, 0)),
            pl.BlockSpec((t, n_g), lambda b, c: (0, 0)),
        ],
        out_specs=[
            pl.BlockSpec((None, t, dd), lambda b, c: (b, c, 0)),
            pl.BlockSpec((None, t, dd), lambda b, c: (b, c, 0)),
        ],
        out_shape=[
            jax.ShapeDtypeStruct((bsz, seq, dd), BF16),
            jax.ShapeDtypeStruct((bsz, seq, dd), F32),
        ],
        scratch_shapes=[pltpu.VMEM((tp, dd), BF16)],
        compiler_params=_params(("arbitrary", "arbitrary"), 40),
        name="sgu",
    )(z, z, ln_g.reshape(1, dd), ln_b.reshape(1, dd), w, bt)


def _in_proj(x, norm_g, w_in, i, tiles):
    if tiles["cast"]:
        return norm_matmul_cast(x, norm_g, w_in, i, tn=MIX_TILE)
    return norm_matmul(x, norm_g, w_in, tm=tiles["tm"], out_dtype=tiles["z_dtype"]), None


def _res_proj(x, a, b, w_out, i, tiles):
    if tiles["cast"]:
        return out_proj_cast(x, a, b, w_out, i, tn=MIX_TILE)
    return out_proj(x, a, b, w_out, tm=tiles["tm_out"]), None


def _even_layer(x, bsz, seq, h0_re, h0_im, pool_buf, start_pos, norm_g, w_in, w_out, i, s5p, pool_w, pool_scale,
                w_glu, b_glu, d_skip, tiles):
    t, d = x.shape
    pw_re, pw_im, b_blk, c_blk = s5p
    n_slab = b_blk.shape[0]
    d_a = n_slab * LANES
    z, w_in_b = _in_proj(x, norm_g, w_in, i, tiles)
    z = z.reshape(bsz, seq, -1)
    h0 = jnp.concatenate([h0_re.reshape(bsz, n_slab, 1, SLAB_STATE), h0_im.reshape(bsz, n_slab, 1, SLAB_STATE)], axis=-1)
    y_pre, h_last = s5_scan(z, 0, b_blk, c_blk, pw_re, pw_im, d_skip, h0, split_in=tiles["s5_split"])
    ya = glu(y_pre.reshape(t, d_a), w_glu, b_glu, i, tm=tiles["tm_glu"], tn=MIX_TILE)
    buf16 = jnp.pad(pool_buf, ((0, 0), (POOL_HIST - pool_buf.shape[1], 0), (0, 0)))
    yb, tail = pool(z, 1, buf16, pool_w, i, pool_scale, start_pos=start_pos, tc=256)
    x, w_out_b = _res_proj(x, ya, yb.reshape(t, -1), w_out, i, tiles)
    g_a = n_slab * SLAB_GROUPS
    h_re = h_last[..., :SLAB_STATE].reshape(bsz, g_a, S5_P)
    h_im = h_last[..., SLAB_STATE:].reshape(bsz, g_a, S5_P)
    return x, h_re, h_im, tail[:, POOL_HIST - pool_buf.shape[1]:], (w_in_b, w_out_b)


def _odd_layer(x, bsz, seq, k_buf, v_buf, norm_g, w_in, w_out, i, qn, kn, ln_g, ln_b, w_s, b_s, n_heads, tiles):
    t, d = x.shape
    d_c = n_heads * LANES
    z, w_in_b = _in_proj(x, norm_g, w_in, i, tiles)
    z = z.reshape(bsz, seq, -1)
    if k_buf is None:
        att, k_new, v_new = attn_prompt(z, qn, kn, n_heads=n_heads)
    else:
        att, k_new, v_new = attn_sample(z, k_buf, v_buf, i * bsz, qn, kn, n_heads=n_heads)
    dd = ln_g.shape[0]
    sg, vn = sgu(z, (3 * d_c) // dd, ln_g, ln_b, w_s, b_s)
    x, w_out_b = _res_proj(x, att.reshape(t, d_c), sg.reshape(t, dd), w_out, i, tiles)
    hd = LANES
    return x, k_new.reshape(bsz, seq, n_heads, hd), v_new.reshape(bsz, seq, n_heads, hd), vn, (w_in_b, w_out_b)


def kernel(x_prompt, x_sample, state_s5_re, state_s5_im, state_pool, cache_k, cache_v, norm_mix, norm_ffn, ev_w_in, ev_w_out, s5_lambda_re, s5_lambda_im, s5_log_dt, s5_b_re, s5_b_im, s5_c_re, s5_c_im, s5_d, s5_w_glu, s5_b_glu, pool_w, pool_scale, od_w_in, od_w_out, q_norm, k_norm, sgu_ln_g, sgu_ln_b, sgu_w, sgu_b, ffn_w1, ffn_w3, ffn_w2):
    bp, lp, d = x_prompt.shape
    bs, ls, _ = x_sample.shape
    depth = norm_mix.shape[0]
    n_heads = cache_k.shape[3]
    xp = x_prompt.reshape(bp * lp, d)
    xs = x_sample.reshape(bs * ls, d)
    tiles_p = dict(cast=False, tm=1024, tm_glu=512, tm_out=1024, s5_split=False, z_dtype=BF16)
    tiles_s = dict(cast=True, tm_glu=bs * ls, s5_split=True)
    g_a, p_a = s5_lambda_re.shape[1:]
    w_glu_b, pool_w_b = s5_w_glu.astype(BF16), pool_w.astype(BF16)
    d_c = n_heads * LANES
    cache_k2 = cache_k.reshape(-1, cache_k.shape[2], d_c)
    cache_v2 = cache_v.reshape(-1, cache_v.shape[2], d_c)

    s5r_p, s5i_p, pool_p, k_p, v_p = [], [], [], [], []
    s5r_s, s5i_s, pool_s, k_s, v_s, sgu_s = [], [], [], [], [], []
    for l in range(depth):
        i = l // 2
        if l % 2 == 0:
            pw_re, pw_im, bb_re, bb_im = s5_prep(s5_lambda_re[i], s5_lambda_im[i], s5_log_dt[i], s5_b_re[i], s5_b_im[i])
            b_blk, c_blk = s5_block_matrices(bb_re, bb_im, s5_c_re[i], s5_c_im[i])
            s5p = (pw_re, pw_im, b_blk, c_blk)
            rest = (i, s5p, pool_w_b, pool_scale[i], w_glu_b, s5_b_glu[i], s5_d[i])
            xs, hr, hi, buf, (w_in_b, w_out_b) = _even_layer(
                xs, bs, ls, state_s5_re[i], state_s5_im[i], state_pool[i], PAST_LEN, norm_mix[l], ev_w_in, ev_w_out,
                *rest, tiles_s)
            s5r_s.append(hr); s5i_s.append(hi); pool_s.append(buf)
            zero_h = jnp.zeros((bp, g_a, p_a), F32)
            zero_buf = jnp.zeros((bp, state_pool.shape[2], state_pool.shape[3]), F32)
            xp, hr, hi, buf, _ = _even_layer(xp, bp, lp, zero_h, zero_h, zero_buf, 0, norm_mix[l], w_in_b, w_out_b,
                                             *rest, tiles_p)
            s5r_p.append(hr); s5i_p.append(hi); pool_p.append(buf)
        else:
            rest = (i, q_norm[i], k_norm[i], sgu_ln_g[i], sgu_ln_b[i], sgu_w[i], sgu_b[i], n_heads)
            xs, nk, nv, vrows, (w_in_b, w_out_b) = _odd_layer(xs, bs, ls, cache_k2, cache_v2, norm_mix[l], od_w_in,
                                                              od_w_out, *rest, tiles_s)
            k_s.append(nk); v_s.append(nv); sgu_s.append(vrows)
            xp, nk, nv, _, _ = _odd_layer(xp, bp, lp, None, None, norm_mix[l], w_in_b, w_out_b, *rest, tiles_p)
            k_p.append(nk); v_p.append(nv)
        xs, w1_b, w3_b, w2_b = ffn_cast(xs, norm_ffn[l], ffn_w1, ffn_w3, ffn_w2, l, tf=FFN_TILE)
        xp = ffn(xp, norm_ffn[l], w1_b, w3_b, w2_b, tm=512)
    return (xp.reshape(bp, lp, d), xs.reshape(bs, ls, d),
            jnp.stack(s5r_p), jnp.stack(s5i_p), jnp.stack(pool_p), jnp.stack(k_p), jnp.stack(v_p),
            jnp.stack(s5r_s), jnp.stack(s5i_s), jnp.stack(pool_s), jnp.stack(k_s), jnp.stack(v_s),
            jnp.stack(sgu_s))
```

```python
import functools
import math

import jax
import jax.numpy as jnp
from jax import lax
from jax.experimental import pallas as pl
from jax.experimental.pallas import tpu as pltpu

F32 = jnp.float32
BF16 = jnp.bfloat16

RMS_EPS = 1e-6
LN_EPS = 1e-5
NEG_INF = -1e30

LANES = 128
SUBLANES = 8
ATT_BLOCK = 128
CHUNK = 128
POOL_WINDOWS = (2, 4, 8, 16)
POOL_HIST = 16
BRANCHES = ((128, 1), (512, 4), (2048, 16))
S5_GRP = 16
S5_P = 64
SLAB_GROUPS = LANES // S5_GRP
SLAB_STATE = SLAB_GROUPS * S5_P
PAST_LEN = 8192
FFN_TILE = 256
MIX_TILE = 512


def _params(sem, vmem_mib):
    return pltpu.CompilerParams(dimension_semantics=sem, vmem_limit_bytes=vmem_mib << 20)


def _gelu(x):
    return 0.5 * x * (1.0 + lax.erf(x * (1.0 / math.sqrt(2.0))))


def _sigmoid(x):
    return 1.0 / (1.0 + jnp.exp(-x))


def _split_bf16(a):
    hi = a.astype(BF16)
    lo = (a - hi.astype(F32)).astype(BF16)
    return hi, lo


def _dot(a, b):
    return jnp.dot(a, b, preferred_element_type=F32)


def _dot_nt(a, b):
    return lax.dot_general(a, b, (((1,), (1,)), ((), ())), preferred_element_type=F32)


def _rms_rows_to(x_ref, g_ref, h_ref, rows):
    step = 16 if rows % 16 == 0 else rows

    def body(r, c):
        sl = pl.ds(pl.multiple_of(r * step, step), step)
        x = x_ref[sl, :]
        ms = jnp.mean(x * x, axis=-1, keepdims=True)
        h_ref[sl, :] = ((x * lax.rsqrt(ms + RMS_EPS)) * g_ref[...]).astype(h_ref.dtype)
        return c

    lax.fori_loop(0, rows // step, body, 0, unroll=min(4, rows // step))


def _norm_matmul_kernel(x_ref, g_ref, w_ref, o_ref, h_ref):
    @pl.when(pl.program_id(1) == 0)
    def _():
        _rms_rows_to(x_ref, g_ref, h_ref, x_ref.shape[0])

    o_ref[...] = _dot(h_ref[...], w_ref[...]).astype(o_ref.dtype)


def norm_matmul(x, g, w, *, tm, out_dtype=F32):
    t, d = x.shape
    n_t, _, tn = w.shape
    return pl.pallas_call(
        _norm_matmul_kernel,
        grid=(t // tm, n_t),
        in_specs=[
            pl.BlockSpec((tm, d), lambda i, j: (i, 0)),
            pl.BlockSpec((1, d), lambda i, j: (0, 0)),
            pl.BlockSpec((None, d, tn), lambda i, j: (j, 0, 0)),
        ],
        out_specs=pl.BlockSpec((tm, tn), lambda i, j: (i, j)),
        out_shape=jax.ShapeDtypeStruct((t, n_t * tn), out_dtype),
        scratch_shapes=[pltpu.VMEM((tm, d), BF16)],
        compiler_params=_params(("arbitrary", "arbitrary"), 58),
        name="norm_matmul",
    )(x, g.reshape(1, d), w)


def _norm_matmul_cast_kernel(x_ref, g_ref, w_ref, o_ref, wb_ref, h_ref):
    @pl.when(pl.program_id(0) == 0)
    def _():
        _rms_rows_to(x_ref, g_ref, h_ref, x_ref.shape[0])

    wb_ref[...] = w_ref[...].astype(BF16)
    o_ref[...] = _dot(h_ref[...], wb_ref[...]).astype(o_ref.dtype)


def norm_matmul_cast(x, g, w, layer, *, tn):
    t, d = x.shape
    n = w.shape[2]
    return pl.pallas_call(
        _norm_matmul_cast_kernel,
        grid=(n // tn,),
        in_specs=[
            pl.BlockSpec((t, d), lambda j: (0, 0)),
            pl.BlockSpec((1, d), lambda j: (0, 0)),
            pl.BlockSpec((None, d, tn), lambda j: (layer, 0, j)),
        ],
        out_specs=[
            pl.BlockSpec((t, tn), lambda j: (0, j)),
            pl.BlockSpec((None, d, tn), lambda j: (j, 0, 0)),
        ],
        out_shape=[
            jax.ShapeDtypeStruct((t, n), F32),
            jax.ShapeDtypeStruct((n // tn, d, tn), BF16),
        ],
        scratch_shapes=[pltpu.VMEM((t, d), BF16)],
        compiler_params=_params(("arbitrary",), 48),
        name="norm_matmul_cast",
    )(x, g.reshape(1, d), w)


FFN_OUT_CHUNK = 512


def _ffn_step(first, x_ref, g_ref, w1_ref, w3_ref, w2_ref, o_ref, h_ref):
    @pl.when(first)
    def _():
        _rms_rows_to(x_ref, g_ref, h_ref, x_ref.shape[0])
        o_ref[...] = x_ref[...]

    h = h_ref[...]
    a = _dot(h, w1_ref[...])
    b = _dot(h, w3_ref[...])
    u = ((a * _sigmoid(a)) * b).astype(BF16)
    for c in range(0, o_ref.shape[1], FFN_OUT_CHUNK):
        o_ref[:, c:c + FFN_OUT_CHUNK] += _dot(u, w2_ref[:, c:c + FFN_OUT_CHUNK])


def _ffn_kernel(x_ref, g_ref, w1_ref, w3_ref, w2_ref, o_ref, h_ref):
    _ffn_step(pl.program_id(1) == 0, x_ref, g_ref, w1_ref, w3_ref, w2_ref, o_ref, h_ref)


def ffn(x, g, w1, w3, w2, *, tm):
    t, d = x.shape
    n_f, _, tf = w1.shape
    return pl.pallas_call(
        _ffn_kernel,
        grid=(t // tm, n_f),
        in_specs=[
            pl.BlockSpec((tm, d), lambda i, j: (i, 0)),
            pl.BlockSpec((1, d), lambda i, j: (0, 0)),
            pl.BlockSpec((None, d, tf), lambda i, j: (j, 0, 0)),
            pl.BlockSpec((None, d, tf), lambda i, j: (j, 0, 0)),
            pl.BlockSpec((tf, d), lambda i, j: (j, 0)),
        ],
        out_specs=pl.BlockSpec((tm, d), lambda i, j: (i, 0)),
        out_shape=jax.ShapeDtypeStruct((t, d), F32),
        scratch_shapes=[pltpu.VMEM((tm, d), BF16)],
        compiler_params=_params(("arbitrary", "arbitrary"), 56),
        name="ffn",
    )(x, g.reshape(1, d), w1, w3, w2)


def _ffn_cast_kernel(x_ref, g_ref, w1_ref, w3_ref, w2_ref, o_ref, w1b_ref, w3b_ref, w2b_ref, h_ref):
    w1b_ref[...] = w1_ref[...].astype(BF16)
    w3b_ref[...] = w3_ref[...].astype(BF16)
    w2b_ref[...] = w2_ref[...].astype(BF16)
    _ffn_step(pl.program_id(0) == 0, x_ref, g_ref, w1b_ref, w3b_ref, w2b_ref, o_ref, h_ref)


def ffn_cast(x, g, w1, w3, w2, layer, *, tf):
    t, d = x.shape
    f = w1.shape[2]
    return pl.pallas_call(
        _ffn_cast_kernel,
        grid=(f // tf,),
        in_specs=[
            pl.BlockSpec((t, d), lambda j: (0, 0)),
            pl.BlockSpec((1, d), lambda j: (0, 0)),
            pl.BlockSpec((None, d, tf), lambda j: (layer, 0, j)),
            pl.BlockSpec((None, d, tf), lambda j: (layer, 0, j)),
            pl.BlockSpec((None, tf, d), lambda j: (layer, j, 0)),
        ],
        out_specs=[
            pl.BlockSpec((t, d), lambda j: (0, 0)),
            pl.BlockSpec((None, d, tf), lambda j: (j, 0, 0)),
            pl.BlockSpec((None, d, tf), lambda j: (j, 0, 0)),
            pl.BlockSpec((tf, d), lambda j: (j, 0)),
        ],
        out_shape=[
            jax.ShapeDtypeStruct((t, d), F32),
            jax.ShapeDtypeStruct((f // tf, d, tf), BF16),
            jax.ShapeDtypeStruct((f // tf, d, tf), BF16),
            jax.ShapeDtypeStruct((f, d), BF16),
        ],
        scratch_shapes=[pltpu.VMEM((t, d), BF16)],
        compiler_params=_params(("arbitrary",), 48),
        name="ffn_cast",
    )(x, g.reshape(1, d), w1, w3, w2)


def _out_proj_kernel(x_ref, a_ref, b_ref, wa_ref, wb_ref, o_ref):
    o_ref[...] = x_ref[...] + _dot(a_ref[...], wa_ref[...]) + _dot(b_ref[...], wb_ref[...])


def out_proj(x, a, b, w_pair, *, tm):
    t, d = x.shape
    wa, wb = w_pair
    n_t, k, tn = wa.shape
    assert a.shape[1] == k and b.shape[1] == k and wb.shape == wa.shape
    return pl.pallas_call(
        _out_proj_kernel,
        grid=(t // tm, n_t),
        in_specs=[
            pl.BlockSpec((tm, tn), lambda i, j: (i, j)),
            pl.BlockSpec((tm, k), lambda i, j: (i, 0)),
            pl.BlockSpec((tm, k), lambda i, j: (i, 0)),
            pl.BlockSpec((None, k, tn), lambda i, j: (j, 0, 0)),
            pl.BlockSpec((None, k, tn), lambda i, j: (j, 0, 0)),
        ],
        out_specs=pl.BlockSpec((tm, tn), lambda i, j: (i, j)),
        out_shape=jax.ShapeDtypeStruct((t, d), F32),
        compiler_params=_params(("arbitrary", "arbitrary"), 48),
        name="out_proj",
    )(x, a, b, wa, wb)


def _out_proj_cast_kernel(x_ref, a_ref, b_ref, wa_ref, wb_ref, o_ref, wab_ref, wbb_ref):
    wab_ref[...] = wa_ref[...].astype(BF16)
    wbb_ref[...] = wb_ref[...].astype(BF16)
    o_ref[...] = x_ref[...] + _dot(a_ref[...], wab_ref[...]) + _dot(b_ref[...], wbb_ref[...])


def out_proj_cast(x, a, b, w, layer, *, tn):
    t, d = x.shape
    k = a.shape[1]
    assert b.shape[1] == k and w.shape[1] == 2 * k
    out, wa_b, wb_b = pl.pallas_call(
        _out_proj_cast_kernel,
        grid=(d // tn,),
        in_specs=[
            pl.BlockSpec((t, tn), lambda j: (0, j)),
            pl.BlockSpec((t, k), lambda j: (0, 0)),
            pl.BlockSpec((t, k), lambda j: (0, 0)),
            pl.BlockSpec((None, k, tn), lambda j: (layer, 0, j)),
            pl.BlockSpec((None, k, tn), lambda j: (layer, 1, j)),
        ],
        out_specs=[
            pl.BlockSpec((t, tn), lambda j: (0, j)),
            pl.BlockSpec((None, k, tn), lambda j: (j, 0, 0)),
            pl.BlockSpec((None, k, tn), lambda j: (j, 0, 0)),
        ],
        out_shape=[
            jax.ShapeDtypeStruct((t, d), F32),
            jax.ShapeDtypeStruct((d // tn, k, tn), BF16),
            jax.ShapeDtypeStruct((d // tn, k, tn), BF16),
        ],
        compiler_params=_params(("arbitrary",), 48),
        name="out_proj_cast",
    )(x, a, b, w, w)
    return out, (wa_b, wb_b)


def _s5_prep_kernel(lr_ref, li_ref, ldt_ref, lrx_ref, lix_ref, ldtx_ref, br_ref, bi_ref,
                    pwr_ref, pwi_ref, bbr_ref, bbi_ref):
    dt = jnp.exp(ldt_ref[...])
    mag = jnp.exp(lr_ref[...] * dt)
    ang = li_ref[...] * dt
    p_r, p_i = mag * jnp.cos(ang), mag * jnp.sin(ang)
    c_r, c_i = p_r, p_i
    pwr_ref[0], pwi_ref[0] = c_r, c_i
    for j in range(1, SUBLANES):
        c_r, c_i = c_r * p_r - c_i * p_i, c_r * p_i + c_i * p_r
        pwr_ref[j], pwi_ref[j] = c_r, c_i
    lr, li = lrx_ref[...], lix_ref[...]
    dtx = jnp.exp(ldtx_ref[...])
    magx = jnp.exp(lr * dtx)
    angx = li * dtx
    nr, ni = magx * jnp.cos(angx) - 1.0, magx * jnp.sin(angx)
    den = lr * lr + li * li
    qr = (nr * lr + ni * li) / den
    qi = (ni * lr - nr * li) / den
    br, bi = br_ref[...], bi_ref[...]
    bbr_ref[...] = qr * br - qi * bi
    bbi_ref[...] = qr * bi + qi * br


def s5_prep(lam_re, lam_im, log_dt, b_re, b_im):
    g, p = lam_re.shape
    h = b_re.shape[2]
    n_slab = g // SLAB_GROUPS
    slab = lambda a: a.reshape(n_slab, SLAB_GROUPS * p)
    rep = lambda a: jnp.repeat(a, h, axis=1)
    ldt_gp = jnp.broadcast_to(log_dt[:, None], (g, p))
    outs = pl.pallas_call(
        _s5_prep_kernel,
        out_shape=[
            jax.ShapeDtypeStruct((SUBLANES, n_slab, SLAB_GROUPS * p), F32),
            jax.ShapeDtypeStruct((SUBLANES, n_slab, SLAB_GROUPS * p), F32),
            jax.ShapeDtypeStruct((g, p * h), F32),
            jax.ShapeDtypeStruct((g, p * h), F32),
        ],
        name="s5_prep",
    )(slab(lam_re), slab(lam_im), slab(ldt_gp), rep(lam_re), rep(lam_im), rep(ldt_gp),
      b_re.reshape(g, p * h), b_im.reshape(g, p * h))
    pw_re, pw_im, bb_re, bb_im = outs
    pw_re = jnp.transpose(pw_re, (1, 0, 2))
    pw_im = jnp.transpose(pw_im, (1, 0, 2))
    return pw_re, pw_im, bb_re.reshape(g, p, h), bb_im.reshape(g, p, h)


def s5_block_matrices(bb_re, bb_im, c_re, c_im):
    g, p, h = bb_re.shape
    n_slab = g // SLAB_GROUPS
    eye = jnp.eye(SLAB_GROUPS, dtype=F32)

    def in_map(bb):
        t = bb.reshape(n_slab, SLAB_GROUPS, p, h)
        return jnp.einsum("kgph,gj->kghjp", t, eye).reshape(n_slab, SLAB_GROUPS * h, SLAB_GROUPS * p)

    def out_map(c):
        t = c.reshape(n_slab, SLAB_GROUPS, h, p)
        return jnp.einsum("kghp,gj->kgpjh", t, eye).reshape(n_slab, SLAB_GROUPS * p, SLAB_GROUPS * h)

    b_blk = jnp.concatenate([in_map(bb_re), in_map(bb_im)], axis=2)
    c_blk = jnp.concatenate([out_map(c_re), -out_map(c_im)], axis=1)
    return b_blk, c_blk


def _s5_scan_kernel(u_ref, bblk_ref, cblk_ref, pwr_ref, pwi_ref, d_ref, h0_ref,
                    y_ref, hl_ref, h_scr, bh_scr, bl_scr, ch_scr, *, seq, row_chunk, split_in):
    ns = SLAB_STATE

    @pl.when(pl.program_id(1) == 0)
    def _():
        bh, bl = _split_bf16(bblk_ref[...])
        bh_scr[...], bl_scr[...] = bh, bl
        ch_scr[...] = cblk_ref[...].astype(BF16)

    n_chunks = seq // row_chunk

    def proj_in(r, c):
        sl = pl.ds(pl.multiple_of(r * row_chunk, row_chunk), row_chunk)
        if split_in:
            uh, ul = _split_bf16(u_ref[sl, :].astype(F32))
            h_scr[sl, :] = _dot(uh, bh_scr[...]) + _dot(ul, bh_scr[...]) + _dot(uh, bl_scr[...])
        else:
            h_scr[sl, :] = _dot(u_ref[sl, :].astype(BF16), bh_scr[...])
        return c

    lax.fori_loop(0, n_chunks, proj_in, 0)

    rowid = lax.broadcasted_iota(jnp.int32, (SUBLANES, LANES), 0)
    for c in range(ns // LANES):
        re_l = slice(c * LANES, (c + 1) * LANES)
        im_l = slice(ns + c * LANES, ns + (c + 1) * LANES)
        p_r, p_i = pwr_ref[:, re_l], pwi_ref[:, re_l]
        steps = []
        for dist in (1, 2, 4):
            a_r = jnp.where(rowid >= dist, jnp.broadcast_to(p_r[dist - 1:dist], (SUBLANES, LANES)), 0.0)
            a_i = jnp.where(rowid >= dist, jnp.broadcast_to(p_i[dist - 1:dist], (SUBLANES, LANES)), 0.0)
            steps.append((dist, a_r, a_i))
        c_r = jnp.broadcast_to(h0_ref[:, re_l], (SUBLANES, LANES))
        c_i = jnp.broadcast_to(h0_ref[:, im_l], (SUBLANES, LANES))

        last = slice(SUBLANES - 1, SUBLANES)
        full = (SUBLANES, LANES)
        p8_r, p8_i = jnp.broadcast_to(p_r[last], full), jnp.broadcast_to(p_i[last], full)
        n_groups = seq // SUBLANES
        per_it = min(4, n_groups)

        def scan_rows(it, carry, re_l=re_l, im_l=im_l, p_r=p_r, p_i=p_i, p8_r=p8_r, p8_i=p8_i, steps=steps):
            base = pl.multiple_of(it * (per_it * SUBLANES), per_it * SUBLANES)
            sls = [pl.ds(base + j * SUBLANES, SUBLANES) for j in range(per_it)]
            loc = []
            for sl in sls:
                r, i = h_scr[sl, re_l], h_scr[sl, im_l]
                for dist, a_r, a_i in steps:
                    s_r, s_i = pltpu.roll(r, dist, 0), pltpu.roll(i, dist, 0)
                    r, i = r + (s_r * a_r - s_i * a_i), i + (s_r * a_i + s_i * a_r)
                loc.append((r, i))
            c_r, c_i = carry
            outs = []
            for r, i in loc:
                outs.append((r + (c_r * p_r - c_i * p_i), i + (c_r * p_i + c_i * p_r)))
                e_r, e_i = jnp.broadcast_to(r[last], full), jnp.broadcast_to(i[last], full)
                c_r, c_i = e_r + (c_r * p8_r - c_i * p8_i), e_i + (c_r * p8_i + c_i * p8_r)
            for sl, (r, i) in zip(sls, outs):
                h_scr[sl, re_l], h_scr[sl, im_l] = r, i
            return c_r, c_i

        c_r, c_i = lax.fori_loop(0, n_groups // per_it, scan_rows, (c_r, c_i))
        hl_ref[:, re_l] = c_r[0:1]
        hl_ref[:, im_l] = c_i[0:1]

    def proj_out(r, c):
        sl = pl.ds(pl.multiple_of(r * row_chunk, row_chunk), row_chunk)
        y = _dot(h_scr[sl, :].astype(BF16), ch_scr[...]) + d_ref[...] * u_ref[sl, :].astype(F32)
        y_ref[sl, :] = _gelu(y)
        return c

    lax.fori_loop(0, n_chunks, proj_out, 0)


def s5_scan(z, col0, b_blk, c_blk, pw_re, pw_im, d_skip, h0, *, split_in):
    bsz, seq, _ = z.shape
    n_slab = b_blk.shape[0]
    ns2 = 2 * SLAB_STATE
    row_chunk = min(seq, 1024)
    kern = functools.partial(_s5_scan_kernel, seq=seq, row_chunk=row_chunk, split_in=split_in)
    return pl.pallas_call(
        kern,
        grid=(n_slab, bsz),
        in_specs=[
            pl.BlockSpec((None, seq, LANES), lambda k, b: (b, 0, col0 + k)),
            pl.BlockSpec((None, LANES, ns2), lambda k, b: (k, 0, 0)),
            pl.BlockSpec((None, ns2, LANES), lambda k, b: (k, 0, 0)),
            pl.BlockSpec((None, SUBLANES, SLAB_STATE), lambda k, b: (k, 0, 0)),
            pl.BlockSpec((None, SUBLANES, SLAB_STATE), lambda k, b: (k, 0, 0)),
            pl.BlockSpec((1, LANES), lambda k, b: (0, k)),
            pl.BlockSpec((None, None, 1, ns2), lambda k, b: (b, k, 0, 0)),
        ],
        out_specs=[
            pl.BlockSpec((None, seq, LANES), lambda k, b: (b, 0, k)),
            pl.BlockSpec((None, None, 1, ns2), lambda k, b: (b, k, 0, 0)),
        ],
        out_shape=[
            jax.ShapeDtypeStruct((bsz, seq, n_slab * LANES), F32),
            jax.ShapeDtypeStruct((bsz, n_slab, 1, ns2), F32),
        ],
        scratch_shapes=[
            pltpu.VMEM((seq, ns2), F32),
            pltpu.VMEM((LANES, ns2), BF16), pltpu.VMEM((LANES, ns2), BF16),
            pltpu.VMEM((ns2, LANES), BF16),
        ],
        compiler_params=_params(("arbitrary", "arbitrary"), 40),
        name="s5_scan",
    )(z, b_blk, c_blk, pw_re, pw_im, d_skip.reshape(1, -1), h0)


def _s5_seg_kernel(u_ref, bblk_ref, cblk_ref, pwr_ref, pwi_ref, d_ref, h0_ref,
                   y_ref, hl_ref, up_scr, h_scr, yp_scr, w_scr, bh_scr, ch_scr, *, seq, row_chunk):
    ns = SLAB_STATE
    nseg = SUBLANES
    seg = seq // nseg
    n_lb = ns // LANES
    full = (SUBLANES, LANES)
    lanes = [(slice(c * LANES, (c + 1) * LANES), slice(ns + c * LANES, ns + (c + 1) * LANES)) for c in range(n_lb)]
    last = slice(SUBLANES - 1, SUBLANES)

    @pl.when(pl.program_id(1) == 0)
    def _():
        bh_scr[...] = bblk_ref[...].astype(BF16)
        ch_scr[...] = cblk_ref[...].astype(BF16)
        for re_l, im_l in lanes:
            p_r, p_i = pwr_ref[:, re_l], pwi_ref[:, re_l]
            p8_r, p8_i = jnp.broadcast_to(p_r[last], full), jnp.broadcast_to(p_i[last], full)
            w_scr[0:SUBLANES, re_l], w_scr[0:SUBLANES, im_l] = p_r, p_i

            def grow(gi, carry, re_l=re_l, im_l=im_l, p8_r=p8_r, p8_i=p8_i):
                w_r, w_i = carry
                w_r, w_i = w_r * p8_r - w_i * p8_i, w_r * p8_i + w_i * p8_r
                sl = pl.ds(pl.multiple_of(gi * SUBLANES, SUBLANES), SUBLANES)
                w_scr[sl, re_l], w_scr[sl, im_l] = w_r, w_i
                return w_r, w_i

            lax.fori_loop(1, seg // SUBLANES, grow, (p_r, p_i))

    for s in range(nseg):
        up_scr[pl.ds(s, seg, stride=nseg), :] = u_ref[s * seg:(s + 1) * seg, :].astype(F32)

    def proj_in(r, c):
        sl = pl.ds(pl.multiple_of(r * row_chunk, row_chunk), row_chunk)
        h_scr[sl, :] = _dot(up_scr[sl, :].astype(BF16), bh_scr[...])
        return c

    lax.fori_loop(0, seq // row_chunk, proj_in, 0)

    lam = [(jnp.broadcast_to(pwr_ref[0:1, re_l], full), jnp.broadcast_to(pwi_ref[0:1, re_l], full))
           for re_l, _ in lanes]
    per_it = 2

    def scan_t(it, carry):
        base = pl.multiple_of(it * (per_it * SUBLANES), per_it * SUBLANES)
        sls = [pl.ds(base + j * SUBLANES, SUBLANES) for j in range(per_it)]
        bu = [[(h_scr[sl, re_l], h_scr[sl, im_l]) for re_l, im_l in lanes] for sl in sls]
        hs, outs = list(carry), []
        for j in range(per_it):
            hs = [(bu[j][c][0] + (hs[c][0] * lam[c][0] - hs[c][1] * lam[c][1]),
                   bu[j][c][1] + (hs[c][0] * lam[c][1] + hs[c][1] * lam[c][0])) for c in range(n_lb)]
            outs.append(hs)
        for sl, row in zip(sls, outs):
            for (re_l, im_l), (h_r, h_i) in zip(lanes, row):
                h_scr[sl, re_l], h_scr[sl, im_l] = h_r, h_i
        return tuple(hs)

    zero = jnp.zeros(full, F32)
    ends = lax.fori_loop(0, seg // per_it, scan_t, tuple((zero, zero) for _ in range(n_lb)))

    rowid = lax.broadcasted_iota(jnp.int32, full, 0)
    enter = []
    for (re_l, im_l), (e_r, e_i) in zip(lanes, ends):
        ws_r, ws_i = w_scr[seg - 1:seg, re_l], w_scr[seg - 1:seg, im_l]
        c_r, c_i = h0_ref[:, re_l], h0_ref[:, im_l]
        cv_r, cv_i = jnp.broadcast_to(c_r, full), jnp.broadcast_to(c_i, full)
        for s in range(1, nseg + 1):
            c_r, c_i = (e_r[s - 1:s] + (c_r * ws_r - c_i * ws_i), e_i[s - 1:s] + (c_r * ws_i + c_i * ws_r))
            if s < nseg:
                cv_r = jnp.where(rowid == s, jnp.broadcast_to(c_r, full), cv_r)
                cv_i = jnp.where(rowid == s, jnp.broadcast_to(c_i, full), cv_i)
        hl_ref[:, re_l], hl_ref[:, im_l] = c_r, c_i
        enter.append((cv_r, cv_i))

    def fix_t(gi, c):
        wsl = pl.ds(pl.multiple_of(gi * SUBLANES, SUBLANES), SUBLANES)
        wv = [(w_scr[wsl, re_l], w_scr[wsl, im_l]) for re_l, im_l in lanes]
        base = pl.multiple_of(gi * (SUBLANES * SUBLANES), SUBLANES * SUBLANES)
        for j in range(SUBLANES):
            sl = pl.ds(base + j * SUBLANES, SUBLANES)
            for (re_l, im_l), (cv_r, cv_i), (wv_r, wv_i) in zip(lanes, enter, wv):
                w_r = jnp.broadcast_to(wv_r[j:j + 1], full)
                w_i = jnp.broadcast_to(wv_i[j:j + 1], full)
                h_r = h_scr[sl, re_l] + (w_r * cv_r - w_i * cv_i)
                h_i = h_scr[sl, im_l] + (w_r * cv_i + w_i * cv_r)
                h_scr[sl, re_l], h_scr[sl, im_l] = h_r, h_i
        return c

    lax.fori_loop(0, seg // SUBLANES, fix_t, 0)

    def proj_out(r, c):
        sl = pl.ds(pl.multiple_of(r * row_chunk, row_chunk), row_chunk)
        y = _dot(h_scr[sl, :].astype(BF16), ch_scr[...]) + d_ref[...] * up_scr[sl, :]
        yp_scr[sl, :] = _gelu(y)
        return c

    lax.fori_loop(0, seq // row_chunk, proj_out, 0)

    for s in range(nseg):
        y_ref[s * seg:(s + 1) * seg, :] = yp_scr[pl.ds(s, seg, stride=nseg), :]


def s5_scan_long(z, col0, b_blk, c_blk, pw_re, pw_im, d_skip, h0):
    bsz, seq, _ = z.shape
    n_slab = b_blk.shape[0]
    ns2 = 2 * SLAB_STATE
    row_chunk = min(seq, 1024)
    assert seq % row_chunk == 0 and seq % (2 * SUBLANES * SUBLANES) == 0
    kern = functools.partial(_s5_seg_kernel, seq=seq, row_chunk=row_chunk)
    return pl.pallas_call(
        kern,
        grid=(n_slab, bsz),
        in_specs=[
            pl.BlockSpec((None, seq, LANES), lambda k, b: (b, 0, col0 + k)),
            pl.BlockSpec((None, LANES, ns2), lambda k, b: (k, 0, 0)),
            pl.BlockSpec((None, ns2, LANES), lambda k, b: (k, 0, 0)),
            pl.BlockSpec((None, SUBLANES, SLAB_STATE), lambda k, b: (k, 0, 0)),
            pl.BlockSpec((None, SUBLANES, SLAB_STATE), lambda k, b: (k, 0, 0)),
            pl.BlockSpec((1, LANES), lambda k, b: (0, k)),
            pl.BlockSpec((None, None, 1, ns2), lambda k, b: (b, k, 0, 0)),
        ],
        out_specs=[
            pl.BlockSpec((None, seq, LANES), lambda k, b: (b, 0, k)),
            pl.BlockSpec((None, None, 1, ns2), lambda k, b: (b, k, 0, 0)),
        ],
        out_shape=[
            jax.ShapeDtypeStruct((bsz, seq, n_slab * LANES), F32),
            jax.ShapeDtypeStruct((bsz, n_slab, 1, ns2), F32),
        ],
        scratch_shapes=[
            pltpu.VMEM((seq, LANES), F32),
            pltpu.VMEM((seq, ns2), F32),
            pltpu.VMEM((seq, LANES), F32),
            pltpu.VMEM((seq // SUBLANES, ns2), F32),
            pltpu.VMEM((LANES, ns2), BF16),
            pltpu.VMEM((ns2, LANES), BF16),
        ],
        compiler_params=_params(("arbitrary", "arbitrary"), 40),
        name="s5_scan_long",
    )(z, b_blk, c_blk, pw_re, pw_im, d_skip.reshape(1, -1), h0)


def _glu_kernel(yk_ref, yj_ref, w_ref, b_ref, o_ref, yb_scr):
    @pl.when(pl.program_id(1) == 0)
    def _():
        yb_scr[...] = yk_ref[...].astype(BF16)

    gate = _dot(yb_scr[...], w_ref[...]) + b_ref[...]
    o_ref[...] = (yj_ref[...] * _sigmoid(gate)).astype(o_ref.dtype)


def glu(y, w, b, layer, *, tm, tn):
    t, d = y.shape
    tm = min(tm, t)
    return pl.pallas_call(
        _glu_kernel,
        grid=(t // tm, d // tn),
        in_specs=[
            pl.BlockSpec((tm, d), lambda i, j: (i, 0)),
            pl.BlockSpec((tm, tn), lambda i, j: (i, j)),
            pl.BlockSpec((None, d, tn), lambda i, j: (layer, 0, j)),
            pl.BlockSpec((1, tn), lambda i, j: (0, j)),
        ],
        out_specs=pl.BlockSpec((tm, tn), lambda i, j: (i, j)),
        out_shape=jax.ShapeDtypeStruct((t, d), BF16),
        scratch_shapes=[pltpu.VMEM((tm, d), BF16)],
        compiler_params=_params(("arbitrary", "arbitrary"), 40),
        name="glu",
    )(y, y, w, b.reshape(1, d))


def _pool_kernel(u_ref, buf_ref, w_ref, s_ref, y_ref, tail_ref, ext_scr, *, tc, start_pos, cg):
    c = pl.program_id(1)

    @pl.when(c == 0)
    def _():
        ext_scr[0:POOL_HIST, :] = buf_ref[...]

    ext_scr[POOL_HIST:POOL_HIST + tc, :] = u_ref[...].astype(F32)
    pos = start_pos + c * tc + lax.broadcasted_iota(jnp.int32, (tc, 1), 0)
    for g, win in enumerate(POOL_WINDOWS):
        cols = slice(g * cg, (g + 1) * cg)
        x = ext_scr[:, cols]
        acc, dist = x, 1
        while dist < win:
            acc = acc + pltpu.roll(acc, dist, 0)
            dist *= 2
        wsum = acc[POOL_HIST:, :]
        cnt = jnp.minimum(pos + 1, win).astype(F32)
        zg = wsum * (1.0 / cnt) - x[POOL_HIST:, :]
        y = _dot(zg.astype(BF16), w_ref[g]) * s_ref[:, cols]
        y_ref[:, cols] = y.astype(y_ref.dtype)

    tail = ext_scr[tc:tc + POOL_HIST, :]
    ext_scr[0:POOL_HIST, :] = tail

    @pl.when(c == pl.num_programs(1) - 1)
    def _():
        tail_ref[...] = tail


def pool(z, colblk, buf16, w, layer, scale, *, start_pos, tc):
    bsz, seq, _ = z.shape
    _, n_g, cg, _ = w.shape
    db = n_g * cg
    tc = min(tc, seq)
    kern = functools.partial(_pool_kernel, tc=tc, start_pos=start_pos, cg=cg)
    return pl.pallas_call(
        kern,
        grid=(bsz, seq // tc),
        in_specs=[
            pl.BlockSpec((None, tc, db), lambda b, c: (b, c, colblk)),
            pl.BlockSpec((None, POOL_HIST, db), lambda b, c: (b, 0, 0)),
            pl.BlockSpec((None, n_g, cg, cg), lambda b, c: (layer, 0, 0, 0)),
            pl.BlockSpec((1, db), lambda b, c: (0, 0)),
        ],
        out_specs=[
            pl.BlockSpec((None, tc, db), lambda b, c: (b, c, 0)),
            pl.BlockSpec((None, POOL_HIST, db), lambda b, c: (b, 0, 0)),
        ],
        out_shape=[
            jax.ShapeDtypeStruct((bsz, seq, db), BF16),
            jax.ShapeDtypeStruct((bsz, POOL_HIST, db), F32),
        ],
        scratch_shapes=[pltpu.VMEM((POOL_HIST + tc, db), F32)],
        compiler_params=_params(("arbitrary", "arbitrary"), 40),
        name="pool",
    )(z, buf16, w, scale.reshape(1, db))


def _head_rms(x, g):
    ms = jnp.mean(x * x, axis=-1, keepdims=True)
    return (x * lax.rsqrt(ms + RMS_EPS)) * g


def _combine(os_, lses):
    m = jnp.maximum(jnp.maximum(lses[0], lses[1]), lses[2])
    ws = [jnp.exp(l - m) for l in lses]
    tot = ws[0] + ws[1] + ws[2]
    return (ws[0] * os_[0] + ws[1] * os_[1] + ws[2] * os_[2]) / tot


def _attn_prompt_kernel(q_ref, k_ref, v_ref, qn_ref, kn_ref, att_ref, ko_ref, vo_ref,
                        qs_scr, qf_scr, kf_scr, vf_scr, qd_scr, kd_scr, vd_scr, s_scr, p_scr, m_scr, o_scr, l_scr,
                        *, seq, scale):
    blk = ATT_BLOCK
    rows = 256
    n_all = seq // blk

    def prep(r, c):
        sl = pl.ds(pl.multiple_of(r * rows, rows), rows)
        qs_scr[sl, :] = _head_rms(q_ref[sl, :].astype(F32), qn_ref[...]) * scale
        ko_ref[sl, :] = _head_rms(k_ref[sl, :].astype(F32), kn_ref[...])
        vo_ref[sl, :] = v_ref[sl, :].astype(F32)
        return c

    lax.fori_loop(0, seq // rows, prep, 0, unroll=2)
    kd_scr[0:blk, :] = jnp.zeros((blk, LANES), BF16)
    vd_scr[0:blk, :] = jnp.zeros((blk, 2 * LANES), BF16)
    vd_scr[blk:, LANES:] = jnp.ones((seq, LANES), BF16)

    qi = lax.broadcasted_iota(jnp.int32, (blk, blk), 0)
    kj = lax.broadcasted_iota(jnp.int32, (blk, blk), 1)
    cur_ok = kj <= qi
    prev_ok = kj >= qi
    band_ok = jnp.concatenate([prev_ok, cur_ok], axis=1)
    in_cur = lax.broadcasted_iota(jnp.int32, (blk, 2 * blk), 1) >= blk

    for g, (window, dil) in enumerate(BRANCHES):
        n_blk = seq // (dil * blk)
        col0 = 0 if n_blk > 1 else LANES

        def place(idx, dil=dil, n_blk=n_blk):
            res = idx // n_blk
            n = idx - res * n_blk
            start = res + n * (dil * blk)
            nat = pl.ds(start, blk, stride=dil) if dil > 1 else pl.ds(pl.multiple_of(start, blk), blk)
            cur = pl.ds(pl.multiple_of(idx * blk, blk), blk)
            kcur = pl.ds(pl.multiple_of((idx + 1) * blk, blk), blk)
            kwin = pl.ds(pl.multiple_of(idx * blk, blk), 2 * blk)
            return nat, cur, kcur, kwin, n

        keep_f32 = dil == BRANCHES[1][1]
        two_level = g == 2 and dil == BRANCHES[1][1] ** 2

        def gather(idx, c, place=place, dil=dil, n_blk=n_blk, keep_f32=keep_f32, two_level=two_level):
            nat, cur, kcur, _, n = place(idx)
            if two_level:
                mid = BRANCHES[1][1]
                res = idx // n_blk
                start = (res % mid) * (seq // mid) + res // mid + n * (mid * blk)
                src = pl.ds(start, blk, stride=mid)
                q, k, v = qf_scr[src, :], kf_scr[src, :], vf_scr[src, :]
            else:
                q, k, v = qs_scr[nat, :], ko_ref[nat, :], vo_ref[nat, :]
            if keep_f32:
                qf_scr[cur, :], kf_scr[cur, :], vf_scr[cur, :] = q, k, v
            qd_scr[cur, :] = q.astype(BF16)
            kd_scr[kcur, :] = k.astype(BF16)
            vd_scr[kcur, 0:LANES] = v.astype(BF16)
            return c

        lax.fori_loop(0, n_all, gather, 0, unroll=4)

        def scores(idx, c, place=place, n_blk=n_blk):
            _, cur, kcur, kwin, n = place(idx)
            q = qd_scr[cur, :]
            if n_blk > 1:
                ok = jnp.logical_and(band_ok, jnp.logical_or(in_cur, n > 0))
                s_scr[idx] = jnp.where(ok, _dot_nt(q, kd_scr[kwin, :]), NEG_INF)
            else:
                s_scr[idx, :, LANES:] = jnp.where(cur_ok, _dot_nt(q, kd_scr[kcur, :]), NEG_INF)
            return c

        lax.fori_loop(0, n_all, scores, 0, unroll=8)

        def softmax(idx, c, col0=col0):
            s = s_scr[idx, :, col0:]
            m = jnp.max(s, axis=-1, keepdims=True)
            p_scr[idx, :, col0:] = jnp.exp(s - m).astype(BF16)
            m_scr[idx] = jnp.broadcast_to(m, (blk, LANES))
            return c

        lax.fori_loop(0, n_all, softmax, 0, unroll=4)

        def values(idx, c, g=g, place=place, n_blk=n_blk):
            nat, _, kcur, kwin, _ = place(idx)
            if n_blk > 1:
                ov = _dot(p_scr[idx], vd_scr[kwin, :])
            else:
                ov = _dot(p_scr[idx, :, LANES:], vd_scr[kcur, :])
            l = ov[:, LANES:]
            o_scr[g, nat, :] = ov[:, 0:LANES] / l
            l_scr[g, nat, :] = m_scr[idx] + jnp.log(l)
            return c

        lax.fori_loop(0, n_all, values, 0, unroll=8)

    def comb(r, c):
        sl = pl.ds(pl.multiple_of(r * rows, rows), rows)
        out = _combine([o_scr[g, sl, :] for g in range(3)], [l_scr[g, sl, :] for g in range(3)])
        att_ref[sl, :] = out.astype(att_ref.dtype)
        return c

    lax.fori_loop(0, seq // rows, comb, 0)


def attn_prompt(z, qn, kn, *, n_heads):
    bsz, seq, _ = z.shape
    assert seq % (BRANCHES[-1][1] * ATT_BLOCK) == 0
    hd = LANES
    kern = functools.partial(_attn_prompt_kernel, seq=seq, scale=hd ** -0.5)
    blk = lambda off: pl.BlockSpec((None, seq, hd), lambda b, h: (b, 0, off + h))
    return pl.pallas_call(
        kern,
        grid=(bsz, n_heads),
        in_specs=[blk(0), blk(n_heads), blk(2 * n_heads),
                  pl.BlockSpec((1, hd), lambda b, h: (0, 0)), pl.BlockSpec((1, hd), lambda b, h: (0, 0))],
        out_specs=[blk(0), blk(0), blk(0)],
        out_shape=[
            jax.ShapeDtypeStruct((bsz, seq, n_heads * hd), BF16),
            jax.ShapeDtypeStruct((bsz, seq, n_heads * hd), F32),
            jax.ShapeDtypeStruct((bsz, seq, n_heads * hd), F32),
        ],
        scratch_shapes=[
            pltpu.VMEM((seq, hd), F32),
            pltpu.VMEM((seq, hd), F32), pltpu.VMEM((seq, hd), F32), pltpu.VMEM((seq, hd), F32),
            pltpu.VMEM((seq, hd), BF16), pltpu.VMEM((seq + ATT_BLOCK, hd), BF16),
            pltpu.VMEM((seq + ATT_BLOCK, 2 * hd), BF16),
            pltpu.VMEM((seq // ATT_BLOCK, ATT_BLOCK, 2 * ATT_BLOCK), F32),
            pltpu.VMEM((seq // ATT_BLOCK, ATT_BLOCK, 2 * ATT_BLOCK), BF16),
            pltpu.VMEM((seq // ATT_BLOCK, ATT_BLOCK, hd), F32),
            pltpu.VMEM((3, seq, hd), F32),
            pltpu.VMEM((3, seq, hd), F32),
        ],
        compiler_params=_params(("arbitrary", "arbitrary"), 40),
        name="attn_prompt",
    )(z, z, z, qn.reshape(1, hd), kn.reshape(1, hd))


def _attn_sample_kernel(q_ref, k_ref, v_ref, ck_ref, cv_ref, qn_ref, kn_ref, att_ref, ko_ref, vo_ref,
                        q_scr, kn_scr, vn_scr, *, s_new, n_buf, scale):
    pad = q_scr.shape[0]
    q_scr[...] = jnp.zeros_like(q_scr)
    kn_scr[...] = jnp.zeros_like(kn_scr)
    vn_scr[...] = jnp.zeros_like(vn_scr)
    k_new = _head_rms(k_ref[...], kn_ref[...])
    v_new = v_ref[...]
    ko_ref[...] = k_new
    vo_ref[...] = v_new
    q_scr[0:s_new, :] = _head_rms(q_ref[...], qn_ref[...])
    kn_scr[0:s_new, :] = k_new
    vn_scr[0:s_new, :] = v_new

    q = q_scr[...].astype(BF16)
    ck = ck_ref[...].astype(BF16)
    cv = cv_ref[...].astype(BF16)
    s_c = _dot_nt(q, ck) * scale
    s_n = _dot_nt(q, kn_scr[...].astype(BF16)) * scale
    qi_c = lax.broadcasted_iota(jnp.int32, (pad, n_buf), 0)
    kj_c = lax.broadcasted_iota(jnp.int32, (pad, n_buf), 1)
    dist_c = n_buf + qi_c - kj_c
    qi_n = lax.broadcasted_iota(jnp.int32, (pad, pad), 0)
    kj_n = lax.broadcasted_iota(jnp.int32, (pad, pad), 1)
    dist_n = qi_n - kj_n
    new_ok = jnp.logical_and(dist_n >= 0, kj_n < s_new)
    outs, lses = [], []
    for window, dil in BRANCHES:
        ok_c = jnp.logical_and((dist_c & (dil - 1)) == 0, dist_c <= window)
        ok_n = jnp.logical_and(new_ok, (dist_n & (dil - 1)) == 0)
        m_c = jnp.where(ok_c, s_c, NEG_INF)
        m_n = jnp.where(ok_n, s_n, NEG_INF)
        m = jnp.maximum(jnp.max(m_c, axis=-1, keepdims=True), jnp.max(m_n, axis=-1, keepdims=True))
        p_c, p_n = jnp.exp(m_c - m), jnp.exp(m_n - m)
        l = jnp.sum(p_c, axis=-1, keepdims=True) + jnp.sum(p_n, axis=-1, keepdims=True)
        o = _dot(p_c.astype(BF16), cv) + _dot(p_n.astype(BF16), vn_scr[...].astype(BF16))
        outs.append(o / l)
        lses.append(jnp.broadcast_to(m + jnp.log(l), (pad, LANES)))
    att_ref[...] = _combine(outs, lses)[0:s_new, :].astype(att_ref.dtype)


def attn_sample(z, cache_k, cache_v, row0, qn, kn, *, n_heads):
    bsz, s_new, _ = z.shape
    n_buf = cache_k.shape[1]
    assert n_buf >= BRANCHES[-1][0]
    hd = LANES
    pad = 16
    kern = functools.partial(_attn_sample_kernel, s_new=s_new, n_buf=n_buf, scale=hd ** -0.5)
    blk = lambda off: pl.BlockSpec((None, s_new, hd), lambda b, h: (b, 0, off + h))
    cblk = pl.BlockSpec((None, n_buf, hd), lambda b, h: (row0 + b, 0, h))
    vec = pl.BlockSpec((1, hd), lambda b, h: (0, 0))
    return pl.pallas_call(
        kern,
        grid=(bsz, n_heads),
        in_specs=[blk(0), blk(n_heads), blk(2 * n_heads), cblk, cblk, vec, vec],
        out_specs=[blk(0), blk(0), blk(0)],
        out_shape=[
            jax.ShapeDtypeStruct((bsz, s_new, n_heads * hd), BF16),
            jax.ShapeDtypeStruct((bsz, s_new, n_heads * hd), F32),
            jax.ShapeDtypeStruct((bsz, s_new, n_heads * hd), F32),
        ],
        scratch_shapes=[pltpu.VMEM((pad, hd), F32), pltpu.VMEM((pad, hd), F32), pltpu.VMEM((pad, hd), F32)],
        compiler_params=_params(("arbitrary", "arbitrary"), 40),
        name="attn_sample",
    )(z, z, z, cache_k, cache_v, qn.reshape(1, hd), kn.reshape(1, hd))


def _sgu_kernel(gu_ref, gv_ref, lg_ref, lb_ref, w_ref, bt_ref, o_ref, vn_ref, vb_scr, *, rows, n_g, cd):
    t = w_ref.shape[1]
    gv = _gelu(gv_ref[...].astype(F32))
    mu = jnp.mean(gv, axis=-1, keepdims=True)
    xc = gv - mu
    var = jnp.mean(xc * xc, axis=-1, keepdims=True)
    vn = (xc * lax.rsqrt(var + LN_EPS)) * lg_ref[...] + lb_ref[...]
    vn_ref[...] = vn
    if rows < t:
        vb_scr[...] = jnp.zeros_like(vb_scr)
    vb_scr[0:rows, :] = vn.astype(BF16)
    ri = lax.broadcasted_iota(jnp.int32, (t, t), 0)
    ci = lax.broadcasted_iota(jnp.int32, (t, t), 1)
    for g in range(n_g):
        cols = slice(g * cd, (g + 1) * cd)
        wg = jnp.where(ri >= ci, w_ref[g], 0.0).astype(BF16)
        mixed = _dot(wg, vb_scr[:, cols])[0:rows, :] + bt_ref[:, g:g + 1]
        o_ref[:, cols] = (_gelu(gu_ref[:, cols].astype(F32)) * mixed).astype(o_ref.dtype)


def sgu(z, colblk_u, ln_g, ln_b, w_s, b_s):
    bsz, seq, _ = z.shape
    n_g = w_s.shape[0]
    dd = ln_g.shape[0]
    cd = dd // n_g
    t = min(seq, CHUNK)
    tp = max(t, LANES)
    w = jnp.pad(w_s[:, :t, :t], ((0, 0), (0, tp - t), (0, tp - t)))
    bt = jnp.transpose(b_s[:, :t])
    kern = functools.partial(_sgu_kernel, rows=t, n_g=n_g, cd=cd)
    return pl.pallas_call(
        kern,
        grid=(bsz, seq // t),
        in_specs=[
            pl.BlockSpec((None, t, dd), lambda b, c: (b, c, colblk_u)),
            pl.BlockSpec((None, t, dd), lambda b, c: (b, c, colblk_u + 1)),
            pl.BlockSpec((1, dd), lambda b, c: (0, 0)),
            pl.BlockSpec((1, dd), lambda b, c: (0, 0)),
            pl.BlockSpec((n_g, tp, tp), lambda b, c: (0, 0, 0)),
            pl.BlockSpec((t, n_g), lambda b, c: (0, 0)),
        ],
        out_specs=[
            pl.BlockSpec((None, t, dd), lambda b, c: (b, c, 0)),
            pl.BlockSpec((None, t, dd), lambda b, c: (b, c, 0)),
        ],
        out_shape=[
            jax.ShapeDtypeStruct((bsz, seq, dd), BF16),
            jax.ShapeDtypeStruct((bsz, seq, dd), F32),
        ],
        scratch_shapes=[pltpu.VMEM((tp, dd), BF16)],
        compiler_params=_params(("arbitrary", "arbitrary"), 40),
        name="sgu",
    )(z, z, ln_g.reshape(1, dd), ln_b.reshape(1, dd), w, bt)


def _in_proj(x, norm_g, w_in, i, tiles):
    if tiles["cast"]:
        return norm_matmul_cast(x, norm_g, w_in, i, tn=MIX_TILE)
    return norm_matmul(x, norm_g, w_in, tm=tiles["tm"], out_dtype=tiles["z_dtype"]), None


def _res_proj(x, a, b, w_out, i, tiles):
    if tiles["cast"]:
        return out_proj_cast(x, a, b, w_out, i, tn=MIX_TILE)
    return out_proj(x, a, b, w_out, tm=tiles["tm_out"]), None


def _even_layer(x, bsz, seq, h0_re, h0_im, pool_buf, start_pos, norm_g, w_in, w_out, i, s5p, pool_w, pool_scale,
                w_glu, b_glu, d_skip, tiles):
    t, d = x.shape
    pw_re, pw_im, b_blk, c_blk = s5p
    n_slab = b_blk.shape[0]
    d_a = n_slab * LANES
    z, w_in_b = _in_proj(x, norm_g, w_in, i, tiles)
    z = z.reshape(bsz, seq, -1)
    h0 = jnp.concatenate([h0_re.reshape(bsz, n_slab, 1, SLAB_STATE), h0_im.reshape(bsz, n_slab, 1, SLAB_STATE)], axis=-1)
    if tiles["s5_long"]:
        y_pre, h_last = s5_scan_long(z, 0, b_blk, c_blk, pw_re, pw_im, d_skip, h0)
    else:
        y_pre, h_last = s5_scan(z, 0, b_blk, c_blk, pw_re, pw_im, d_skip, h0, split_in=True)
    ya = glu(y_pre.reshape(t, d_a), w_glu, b_glu, i, tm=tiles["tm_glu"], tn=MIX_TILE)
    buf16 = jnp.pad(pool_buf, ((0, 0), (POOL_HIST - pool_buf.shape[1], 0), (0, 0)))
    yb, tail = pool(z, 1, buf16, pool_w, i, pool_scale, start_pos=start_pos, tc=256)
    x, w_out_b = _res_proj(x, ya, yb.reshape(t, -1), w_out, i, tiles)
    g_a = n_slab * SLAB_GROUPS
    h_re = h_last[..., :SLAB_STATE].reshape(bsz, g_a, S5_P)
    h_im = h_last[..., SLAB_STATE:].reshape(bsz, g_a, S5_P)
    return x, h_re, h_im, tail[:, POOL_HIST - pool_buf.shape[1]:], (w_in_b, w_out_b)


def _odd_layer(x, bsz, seq, k_buf, v_buf, norm_g, w_in, w_out, i, qn, kn, ln_g, ln_b, w_s, b_s, n_heads, tiles):
    t, d = x.shape
    d_c = n_heads * LANES
    z, w_in_b = _in_proj(x, norm_g, w_in, i, tiles)
    z = z.reshape(bsz, seq, -1)
    if k_buf is None:
        att, k_new, v_new = attn_prompt(z, qn, kn, n_heads=n_heads)
    else:
        att, k_new, v_new = attn_sample(z, k_buf, v_buf, i * bsz, qn, kn, n_heads=n_heads)
    dd = ln_g.shape[0]
    sg, vn = sgu(z, (3 * d_c) // dd, ln_g, ln_b, w_s, b_s)
    x, w_out_b = _res_proj(x, att.reshape(t, d_c), sg.reshape(t, dd), w_out, i, tiles)
    hd = LANES
    return x, k_new.reshape(bsz, seq, n_heads, hd), v_new.reshape(bsz, seq, n_heads, hd), vn, (w_in_b, w_out_b)


def kernel(x_prompt, x_sample, state_s5_re, state_s5_im, state_pool, cache_k, cache_v, norm_mix, norm_ffn, ev_w_in, ev_w_out, s5_lambda_re, s5_lambda_im, s5_log_dt, s5_b_re, s5_b_im, s5_c_re, s5_c_im, s5_d, s5_w_glu, s5_b_glu, pool_w, pool_scale, od_w_in, od_w_out, q_norm, k_norm, sgu_ln_g, sgu_ln_b, sgu_w, sgu_b, ffn_w1, ffn_w3, ffn_w2):
    bp, lp, d = x_prompt.shape
    bs, ls, _ = x_sample.shape
    depth = norm_mix.shape[0]
    n_heads = cache_k.shape[3]
    xp = x_prompt.reshape(bp * lp, d)
    xs = x_sample.reshape(bs * ls, d)
    tiles_p = dict(cast=False, tm=1024, tm_glu=512, tm_out=1024, s5_long=True, z_dtype=BF16)
    tiles_s = dict(cast=True, tm_glu=bs * ls, s5_long=False)
    g_a, p_a = s5_lambda_re.shape[1:]
    w_glu_b, pool_w_b = s5_w_glu.astype(BF16), pool_w.astype(BF16)
    d_c = n_heads * LANES
    cache_k2 = cache_k.reshape(-1, cache_k.shape[2], d_c)
    cache_v2 = cache_v.reshape(-1, cache_v.shape[2], d_c)

    s5r_p, s5i_p, pool_p, k_p, v_p = [], [], [], [], []
    s5r_s, s5i_s, pool_s, k_s, v_s, sgu_s = [], [], [], [], [], []
    for l in range(depth):
        i = l // 2
        if l % 2 == 0:
            pw_re, pw_im, bb_re, bb_im = s5_prep(s5_lambda_re[i], s5_lambda_im[i], s5_log_dt[i], s5_b_re[i], s5_b_im[i])
            b_blk, c_blk = s5_block_matrices(bb_re, bb_im, s5_c_re[i], s5_c_im[i])
            s5p = (pw_re, pw_im, b_blk, c_blk)
            rest = (i, s5p, pool_w_b, pool_scale[i], w_glu_b, s5_b_glu[i], s5_d[i])
            xs, hr, hi, buf, (w_in_b, w_out_b) = _even_layer(
                xs, bs, ls, state_s5_re[i], state_s5_im[i], state_pool[i], PAST_LEN, norm_mix[l], ev_w_in, ev_w_out,
                *rest, tiles_s)
            s5r_s.append(hr); s5i_s.append(hi); pool_s.append(buf)
            zero_h = jnp.zeros((bp, g_a, p_a), F32)
            zero_buf = jnp.zeros((bp, state_pool.shape[2], state_pool.shape[3]), F32)
            xp, hr, hi, buf, _ = _even_layer(xp, bp, lp, zero_h, zero_h, zero_buf, 0, norm_mix[l], w_in_b, w_out_b,
                                             *rest, tiles_p)
            s5r_p.append(hr); s5i_p.append(hi); pool_p.append(buf)
        else:
            rest = (i, q_norm[i], k_norm[i], sgu_ln_g[i], sgu_ln_b[i], sgu_w[i], sgu_b[i], n_heads)
            xs, nk, nv, vrows, (w_in_b, w_out_b) = _odd_layer(xs, bs, ls, cache_k2, cache_v2, norm_mix[l], od_w_in,
                                                              od_w_out, *rest, tiles_s)
            k_s.append(nk); v_s.append(nv); sgu_s.append(vrows)
            xp, nk, nv, _, _ = _odd_layer(xp, bp, lp, None, None, norm_mix[l], w_in_b, w_out_b, *rest, tiles_p)
            k_p.append(nk); v_p.append(nv)
        xs, w1_b, w3_b, w2_b = ffn_cast(xs, norm_ffn[l], ffn_w1, ffn_w3, ffn_w2, l, tf=FFN_TILE)
        xp = ffn(xp, norm_ffn[l], w1_b, w3_b, w2_b, tm=512)
    return (xp.reshape(bp, lp, d), xs.reshape(bs, ls, d),
            jnp.stack(s5r_p), jnp.stack(s5i_p), jnp.stack(pool_p), jnp.stack(k_p), jnp.stack(v_p),
            jnp.stack(s5r_s), jnp.stack(s5i_s), jnp.stack(pool_s), jnp.stack(k_s), jnp.stack(v_s),
            jnp.stack(sgu_s))
```

```python
import functools
import math

import jax
import jax.numpy as jnp
from jax import lax
from jax.experimental import pallas as pl
from jax.experimental.pallas import tpu as pltpu

F32 = jnp.float32
BF16 = jnp.bfloat16

RMS_EPS = 1e-6
LN_EPS = 1e-5
NEG_INF = -1e30

LANES = 128
SUBLANES = 8
ATT_BLOCK = 128
CHUNK = 128
POOL_WINDOWS = (2, 4, 8, 16)
POOL_HIST = 16
BRANCHES = ((128, 1), (512, 4), (2048, 16))
S5_GRP = 16
S5_P = 64
SLAB_GROUPS = LANES // S5_GRP
SLAB_STATE = SLAB_GROUPS * S5_P
PAST_LEN = 8192
FFN_TILE = 256
MIX_TILE = 512


def _params(sem, vmem_mib):
    return pltpu.CompilerParams(dimension_semantics=sem, vmem_limit_bytes=vmem_mib << 20)


def _gelu(x):
    return 0.5 * x * (1.0 + lax.erf(x * (1.0 / math.sqrt(2.0))))


def _sigmoid(x):
    return 1.0 / (1.0 + jnp.exp(-x))


def _split_bf16(a):
    hi = a.astype(BF16)
    lo = (a - hi.astype(F32)).astype(BF16)
    return hi, lo


def _dot(a, b):
    return jnp.dot(a, b, preferred_element_type=F32)


def _dot_nt(a, b):
    return lax.dot_general(a, b, (((1,), (1,)), ((), ())), preferred_element_type=F32)


def _rms_rows_to(x_ref, g_ref, h_ref, rows):
    step = 16 if rows % 16 == 0 else rows

    def body(r, c):
        sl = pl.ds(pl.multiple_of(r * step, step), step)
        x = x_ref[sl, :]
        ms = jnp.mean(x * x, axis=-1, keepdims=True)
        h_ref[sl, :] = ((x * lax.rsqrt(ms + RMS_EPS)) * g_ref[...]).astype(h_ref.dtype)
        return c

    lax.fori_loop(0, rows // step, body, 0, unroll=min(4, rows // step))


def _norm_matmul_kernel(x_ref, g_ref, w_ref, o_ref, h_ref):
    @pl.when(pl.program_id(1) == 0)
    def _():
        _rms_rows_to(x_ref, g_ref, h_ref, x_ref.shape[0])

    o_ref[...] = _dot(h_ref[...], w_ref[...]).astype(o_ref.dtype)


def norm_matmul(x, g, w, *, tm, out_dtype=F32):
    t, d = x.shape
    n_t, _, tn = w.shape
    return pl.pallas_call(
        _norm_matmul_kernel,
        grid=(t // tm, n_t),
        in_specs=[
            pl.BlockSpec((tm, d), lambda i, j: (i, 0)),
            pl.BlockSpec((1, d), lambda i, j: (0, 0)),
            pl.BlockSpec((None, d, tn), lambda i, j: (j, 0, 0)),
        ],
        out_specs=pl.BlockSpec((tm, tn), lambda i, j: (i, j)),
        out_shape=jax.ShapeDtypeStruct((t, n_t * tn), out_dtype),
        scratch_shapes=[pltpu.VMEM((tm, d), BF16)],
        compiler_params=_params(("arbitrary", "arbitrary"), 58),
        name="norm_matmul",
    )(x, g.reshape(1, d), w)


def _norm_matmul_cast_kernel(x_ref, g_ref, w_ref, o_ref, wb_ref, h_ref):
    @pl.when(pl.program_id(0) == 0)
    def _():
        _rms_rows_to(x_ref, g_ref, h_ref, x_ref.shape[0])

    wb_ref[...] = w_ref[...].astype(BF16)
    o_ref[...] = _dot(h_ref[...], wb_ref[...]).astype(o_ref.dtype)


def norm_matmul_cast(x, g, w, layer, *, tn):
    t, d = x.shape
    n = w.shape[2]
    return pl.pallas_call(
        _norm_matmul_cast_kernel,
        grid=(n // tn,),
        in_specs=[
            pl.BlockSpec((t, d), lambda j: (0, 0)),
            pl.BlockSpec((1, d), lambda j: (0, 0)),
            pl.BlockSpec((None, d, tn), lambda j: (layer, 0, j)),
        ],
        out_specs=[
            pl.BlockSpec((t, tn), lambda j: (0, j)),
            pl.BlockSpec((None, d, tn), lambda j: (j, 0, 0)),
        ],
        out_shape=[
            jax.ShapeDtypeStruct((t, n), F32),
            jax.ShapeDtypeStruct((n // tn, d, tn), BF16),
        ],
        scratch_shapes=[pltpu.VMEM((t, d), BF16)],
        compiler_params=_params(("arbitrary",), 48),
        name="norm_matmul_cast",
    )(x, g.reshape(1, d), w)


FFN_OUT_CHUNK = 512


def _ffn_step(first, x_ref, g_ref, w1_ref, w3_ref, w2_ref, o_ref, h_ref):
    @pl.when(first)
    def _():
        _rms_rows_to(x_ref, g_ref, h_ref, x_ref.shape[0])
        o_ref[...] = x_ref[...]

    h = h_ref[...]
    a = _dot(h, w1_ref[...])
    b = _dot(h, w3_ref[...])
    u = ((a * _sigmoid(a)) * b).astype(BF16)
    for c in range(0, o_ref.shape[1], FFN_OUT_CHUNK):
        o_ref[:, c:c + FFN_OUT_CHUNK] += _dot(u, w2_ref[:, c:c + FFN_OUT_CHUNK])


def _ffn_kernel(x_ref, g_ref, w1_ref, w3_ref, w2_ref, o_ref, h_ref):
    _ffn_step(pl.program_id(1) == 0, x_ref, g_ref, w1_ref, w3_ref, w2_ref, o_ref, h_ref)


def ffn(x, g, w1, w3, w2, *, tm):
    t, d = x.shape
    n_f, _, tf = w1.shape
    return pl.pallas_call(
        _ffn_kernel,
        grid=(t // tm, n_f),
        in_specs=[
            pl.BlockSpec((tm, d), lambda i, j: (i, 0)),
            pl.BlockSpec((1, d), lambda i, j: (0, 0)),
            pl.BlockSpec((None, d, tf), lambda i, j: (j, 0, 0)),
            pl.BlockSpec((None, d, tf), lambda i, j: (j, 0, 0)),
            pl.BlockSpec((tf, d), lambda i, j: (j, 0)),
        ],
        out_specs=pl.BlockSpec((tm, d), lambda i, j: (i, 0)),
        out_shape=jax.ShapeDtypeStruct((t, d), F32),
        scratch_shapes=[pltpu.VMEM((tm, d), BF16)],
        compiler_params=_params(("arbitrary", "arbitrary"), 56),
        name="ffn",
    )(x, g.reshape(1, d), w1, w3, w2)


def _ffn_cast_kernel(x_ref, g_ref, w1_ref, w3_ref, w2_ref, o_ref, w1b_ref, w3b_ref, w2b_ref, h_ref):
    w1b_ref[...] = w1_ref[...].astype(BF16)
    w3b_ref[...] = w3_ref[...].astype(BF16)
    w2b_ref[...] = w2_ref[...].astype(BF16)
    _ffn_step(pl.program_id(0) == 0, x_ref, g_ref, w1b_ref, w3b_ref, w2b_ref, o_ref, h_ref)


def ffn_cast(x, g, w1, w3, w2, layer, *, tf):
    t, d = x.shape
    f = w1.shape[2]
    return pl.pallas_call(
        _ffn_cast_kernel,
        grid=(f // tf,),
        in_specs=[
            pl.BlockSpec((t, d), lambda j: (0, 0)),
            pl.BlockSpec((1, d), lambda j: (0, 0)),
            pl.BlockSpec((None, d, tf), lambda j: (layer, 0, j)),
            pl.BlockSpec((None, d, tf), lambda j: (layer, 0, j)),
            pl.BlockSpec((None, tf, d), lambda j: (layer, j, 0)),
        ],
        out_specs=[
            pl.BlockSpec((t, d), lambda j: (0, 0)),
            pl.BlockSpec((None, d, tf), lambda j: (j, 0, 0)),
            pl.BlockSpec((None, d, tf), lambda j: (j, 0, 0)),
            pl.BlockSpec((tf, d), lambda j: (j, 0)),
        ],
        out_shape=[
            jax.ShapeDtypeStruct((t, d), F32),
            jax.ShapeDtypeStruct((f // tf, d, tf), BF16),
            jax.ShapeDtypeStruct((f // tf, d, tf), BF16),
            jax.ShapeDtypeStruct((f, d), BF16),
        ],
        scratch_shapes=[pltpu.VMEM((t, d), BF16)],
        compiler_params=_params(("arbitrary",), 48),
        name="ffn_cast",
    )(x, g.reshape(1, d), w1, w3, w2)


def _out_proj_kernel(x_ref, a_ref, b_ref, wa_ref, wb_ref, o_ref):
    o_ref[...] = x_ref[...] + _dot(a_ref[...], wa_ref[...]) + _dot(b_ref[...], wb_ref[...])


def out_proj(x, a, b, w_pair, *, tm):
    t, d = x.shape
    wa, wb = w_pair
    n_t, k, tn = wa.shape
    assert a.shape[1] == k and b.shape[1] == k and wb.shape == wa.shape
    return pl.pallas_call(
        _out_proj_kernel,
        grid=(t // tm, n_t),
        in_specs=[
            pl.BlockSpec((tm, tn), lambda i, j: (i, j)),
            pl.BlockSpec((tm, k), lambda i, j: (i, 0)),
            pl.BlockSpec((tm, k), lambda i, j: (i, 0)),
            pl.BlockSpec((None, k, tn), lambda i, j: (j, 0, 0)),
            pl.BlockSpec((None, k, tn), lambda i, j: (j, 0, 0)),
        ],
        out_specs=pl.BlockSpec((tm, tn), lambda i, j: (i, j)),
        out_shape=jax.ShapeDtypeStruct((t, d), F32),
        compiler_params=_params(("arbitrary", "arbitrary"), 48),
        name="out_proj",
    )(x, a, b, wa, wb)


def _out_proj_cast_kernel(x_ref, a_ref, b_ref, wa_ref, wb_ref, o_ref, wab_ref, wbb_ref):
    wab_ref[...] = wa_ref[...].astype(BF16)
    wbb_ref[...] = wb_ref[...].astype(BF16)
    o_ref[...] = x_ref[...] + _dot(a_ref[...], wab_ref[...]) + _dot(b_ref[...], wbb_ref[...])


def out_proj_cast(x, a, b, w, layer, *, tn):
    t, d = x.shape
    k = a.shape[1]
    assert b.shape[1] == k and w.shape[1] == 2 * k
    out, wa_b, wb_b = pl.pallas_call(
        _out_proj_cast_kernel,
        grid=(d // tn,),
        in_specs=[
            pl.BlockSpec((t, tn), lambda j: (0, j)),
            pl.BlockSpec((t, k), lambda j: (0, 0)),
            pl.BlockSpec((t, k), lambda j: (0, 0)),
            pl.BlockSpec((None, k, tn), lambda j: (layer, 0, j)),
            pl.BlockSpec((None, k, tn), lambda j: (layer, 1, j)),
        ],
        out_specs=[
            pl.BlockSpec((t, tn), lambda j: (0, j)),
            pl.BlockSpec((None, k, tn), lambda j: (j, 0, 0)),
            pl.BlockSpec((None, k, tn), lambda j: (j, 0, 0)),
        ],
        out_shape=[
            jax.ShapeDtypeStruct((t, d), F32),
            jax.ShapeDtypeStruct((d // tn, k, tn), BF16),
            jax.ShapeDtypeStruct((d // tn, k, tn), BF16),
        ],
        compiler_params=_params(("arbitrary",), 48),
        name="out_proj_cast",
    )(x, a, b, w, w)
    return out, (wa_b, wb_b)


def _s5_prep_kernel(lr_ref, li_ref, ldt_ref, lrx_ref, lix_ref, ldtx_ref, br_ref, bi_ref,
                    pwr_ref, pwi_ref, bbr_ref, bbi_ref):
    dt = jnp.exp(ldt_ref[...])
    mag = jnp.exp(lr_ref[...] * dt)
    ang = li_ref[...] * dt
    p_r, p_i = mag * jnp.cos(ang), mag * jnp.sin(ang)
    c_r, c_i = p_r, p_i
    pwr_ref[0], pwi_ref[0] = c_r, c_i
    for j in range(1, SUBLANES):
        c_r, c_i = c_r * p_r - c_i * p_i, c_r * p_i + c_i * p_r
        pwr_ref[j], pwi_ref[j] = c_r, c_i
    lr, li = lrx_ref[...], lix_ref[...]
    dtx = jnp.exp(ldtx_ref[...])
    magx = jnp.exp(lr * dtx)
    angx = li * dtx
    nr, ni = magx * jnp.cos(angx) - 1.0, magx * jnp.sin(angx)
    den = lr * lr + li * li
    qr = (nr * lr + ni * li) / den
    qi = (ni * lr - nr * li) / den
    br, bi = br_ref[...], bi_ref[...]
    bbr_ref[...] = qr * br - qi * bi
    bbi_ref[...] = qr * bi + qi * br


def s5_prep(lam_re, lam_im, log_dt, b_re, b_im):
    g, p = lam_re.shape
    h = b_re.shape[2]
    n_slab = g // SLAB_GROUPS
    slab = lambda a: a.reshape(n_slab, SLAB_GROUPS * p)
    rep = lambda a: jnp.repeat(a, h, axis=1)
    ldt_gp = jnp.broadcast_to(log_dt[:, None], (g, p))
    outs = pl.pallas_call(
        _s5_prep_kernel,
        out_shape=[
            jax.ShapeDtypeStruct((SUBLANES, n_slab, SLAB_GROUPS * p), F32),
            jax.ShapeDtypeStruct((SUBLANES, n_slab, SLAB_GROUPS * p), F32),
            jax.ShapeDtypeStruct((g, p * h), F32),
            jax.ShapeDtypeStruct((g, p * h), F32),
        ],
        name="s5_prep",
    )(slab(lam_re), slab(lam_im), slab(ldt_gp), rep(lam_re), rep(lam_im), rep(ldt_gp),
      b_re.reshape(g, p * h), b_im.reshape(g, p * h))
    pw_re, pw_im, bb_re, bb_im = outs
    pw_re = jnp.transpose(pw_re, (1, 0, 2))
    pw_im = jnp.transpose(pw_im, (1, 0, 2))
    return pw_re, pw_im, bb_re.reshape(g, p, h), bb_im.reshape(g, p, h)


def s5_block_matrices(bb_re, bb_im, c_re, c_im):
    g, p, h = bb_re.shape
    n_slab = g // SLAB_GROUPS
    eye = jnp.eye(SLAB_GROUPS, dtype=F32)

    def in_map(bb):
        t = bb.reshape(n_slab, SLAB_GROUPS, p, h)
        return jnp.einsum("kgph,gj->kghjp", t, eye).reshape(n_slab, SLAB_GROUPS * h, SLAB_GROUPS * p)

    def out_map(c):
        t = c.reshape(n_slab, SLAB_GROUPS, h, p)
        return jnp.einsum("kghp,gj->kgpjh", t, eye).reshape(n_slab, SLAB_GROUPS * p, SLAB_GROUPS * h)

    b_blk = jnp.concatenate([in_map(bb_re), in_map(bb_im)], axis=2)
    c_blk = jnp.concatenate([out_map(c_re), -out_map(c_im)], axis=1)
    return b_blk, c_blk


def _s5_scan_kernel(u_ref, bblk_ref, cblk_ref, pwr_ref, pwi_ref, d_ref, h0_ref,
                    y_ref, hl_ref, h_scr, bh_scr, bl_scr, ch_scr, *, seq, row_chunk, split_in):
    ns = SLAB_STATE

    @pl.when(pl.program_id(1) == 0)
    def _():
        bh, bl = _split_bf16(bblk_ref[...])
        bh_scr[...], bl_scr[...] = bh, bl
        ch_scr[...] = cblk_ref[...].astype(BF16)

    n_chunks = seq // row_chunk

    def proj_in(r, c):
        sl = pl.ds(pl.multiple_of(r * row_chunk, row_chunk), row_chunk)
        if split_in:
            uh, ul = _split_bf16(u_ref[sl, :].astype(F32))
            h_scr[sl, :] = _dot(uh, bh_scr[...]) + _dot(ul, bh_scr[...]) + _dot(uh, bl_scr[...])
        else:
            h_scr[sl, :] = _dot(u_ref[sl, :].astype(BF16), bh_scr[...])
        return c

    lax.fori_loop(0, n_chunks, proj_in, 0)

    rowid = lax.broadcasted_iota(jnp.int32, (SUBLANES, LANES), 0)
    for c in range(ns // LANES):
        re_l = slice(c * LANES, (c + 1) * LANES)
        im_l = slice(ns + c * LANES, ns + (c + 1) * LANES)
        p_r, p_i = pwr_ref[:, re_l], pwi_ref[:, re_l]
        steps = []
        for dist in (1, 2, 4):
            a_r = jnp.where(rowid >= dist, jnp.broadcast_to(p_r[dist - 1:dist], (SUBLANES, LANES)), 0.0)
            a_i = jnp.where(rowid >= dist, jnp.broadcast_to(p_i[dist - 1:dist], (SUBLANES, LANES)), 0.0)
            steps.append((dist, a_r, a_i))
        c_r = jnp.broadcast_to(h0_ref[:, re_l], (SUBLANES, LANES))
        c_i = jnp.broadcast_to(h0_ref[:, im_l], (SUBLANES, LANES))

        last = slice(SUBLANES - 1, SUBLANES)
        full = (SUBLANES, LANES)
        p8_r, p8_i = jnp.broadcast_to(p_r[last], full), jnp.broadcast_to(p_i[last], full)
        n_groups = seq // SUBLANES
        per_it = min(4, n_groups)

        def scan_rows(it, carry, re_l=re_l, im_l=im_l, p_r=p_r, p_i=p_i, p8_r=p8_r, p8_i=p8_i, steps=steps):
            base = pl.multiple_of(it * (per_it * SUBLANES), per_it * SUBLANES)
            sls = [pl.ds(base + j * SUBLANES, SUBLANES) for j in range(per_it)]
            loc = []
            for sl in sls:
                r, i = h_scr[sl, re_l], h_scr[sl, im_l]
                for dist, a_r, a_i in steps:
                    s_r, s_i = pltpu.roll(r, dist, 0), pltpu.roll(i, dist, 0)
                    r, i = r + (s_r * a_r - s_i * a_i), i + (s_r * a_i + s_i * a_r)
                loc.append((r, i))
            c_r, c_i = carry
            outs = []
            for r, i in loc:
                outs.append((r + (c_r * p_r - c_i * p_i), i + (c_r * p_i + c_i * p_r)))
                e_r, e_i = jnp.broadcast_to(r[last], full), jnp.broadcast_to(i[last], full)
                c_r, c_i = e_r + (c_r * p8_r - c_i * p8_i), e_i + (c_r * p8_i + c_i * p8_r)
            for sl, (r, i) in zip(sls, outs):
                h_scr[sl, re_l], h_scr[sl, im_l] = r, i
            return c_r, c_i

        c_r, c_i = lax.fori_loop(0, n_groups // per_it, scan_rows, (c_r, c_i))
        hl_ref[:, re_l] = c_r[0:1]
        hl_ref[:, im_l] = c_i[0:1]

    def proj_out(r, c):
        sl = pl.ds(pl.multiple_of(r * row_chunk, row_chunk), row_chunk)
        y = _dot(h_scr[sl, :].astype(BF16), ch_scr[...]) + d_ref[...] * u_ref[sl, :].astype(F32)
        y_ref[sl, :] = _gelu(y)
        return c

    lax.fori_loop(0, n_chunks, proj_out, 0)


def s5_scan(z, col0, b_blk, c_blk, pw_re, pw_im, d_skip, h0, *, split_in):
    bsz, seq, _ = z.shape
    n_slab = b_blk.shape[0]
    ns2 = 2 * SLAB_STATE
    row_chunk = min(seq, 1024)
    kern = functools.partial(_s5_scan_kernel, seq=seq, row_chunk=row_chunk, split_in=split_in)
    return pl.pallas_call(
        kern,
        grid=(n_slab, bsz),
        in_specs=[
            pl.BlockSpec((None, seq, LANES), lambda k, b: (b, 0, col0 + k)),
            pl.BlockSpec((None, LANES, ns2), lambda k, b: (k, 0, 0)),
            pl.BlockSpec((None, ns2, LANES), lambda k, b: (k, 0, 0)),
            pl.BlockSpec((None, SUBLANES, SLAB_STATE), lambda k, b: (k, 0, 0)),
            pl.BlockSpec((None, SUBLANES, SLAB_STATE), lambda k, b: (k, 0, 0)),
            pl.BlockSpec((1, LANES), lambda k, b: (0, k)),
            pl.BlockSpec((None, None, 1, ns2), lambda k, b: (b, k, 0, 0)),
        ],
        out_specs=[
            pl.BlockSpec((None, seq, LANES), lambda k, b: (b, 0, k)),
            pl.BlockSpec((None, None, 1, ns2), lambda k, b: (b, k, 0, 0)),
        ],
        out_shape=[
            jax.ShapeDtypeStruct((bsz, seq, n_slab * LANES), F32),
            jax.ShapeDtypeStruct((bsz, n_slab, 1, ns2), F32),
        ],
        scratch_shapes=[
            pltpu.VMEM((seq, ns2), F32),
            pltpu.VMEM((LANES, ns2), BF16), pltpu.VMEM((LANES, ns2), BF16),
            pltpu.VMEM((ns2, LANES), BF16),
        ],
        compiler_params=_params(("arbitrary", "arbitrary"), 40),
        name="s5_scan",
    )(z, b_blk, c_blk, pw_re, pw_im, d_skip.reshape(1, -1), h0)


def _s5_seg_kernel(u_ref, bblk_ref, cblk_ref, pwr_ref, pwi_ref, d_ref, h0_ref,
                   y_ref, hl_ref, up_scr, h_scr, yp_scr, w_scr, bh_scr, ch_scr, *, seq, row_chunk):
    ns = SLAB_STATE
    nseg = SUBLANES
    seg = seq // nseg
    n_lb = ns // LANES
    full = (SUBLANES, LANES)
    lanes = [(slice(c * LANES, (c + 1) * LANES), slice(ns + c * LANES, ns + (c + 1) * LANES)) for c in range(n_lb)]
    last = slice(SUBLANES - 1, SUBLANES)

    @pl.when(pl.program_id(1) == 0)
    def _():
        bh_scr[...] = bblk_ref[...].astype(BF16)
        ch_scr[...] = cblk_ref[...].astype(BF16)
        for re_l, im_l in lanes:
            p_r, p_i = pwr_ref[:, re_l], pwi_ref[:, re_l]
            p8_r, p8_i = jnp.broadcast_to(p_r[last], full), jnp.broadcast_to(p_i[last], full)
            w_scr[0:SUBLANES, re_l], w_scr[0:SUBLANES, im_l] = p_r, p_i

            def grow(gi, carry, re_l=re_l, im_l=im_l, p8_r=p8_r, p8_i=p8_i):
                w_r, w_i = carry
                w_r, w_i = w_r * p8_r - w_i * p8_i, w_r * p8_i + w_i * p8_r
                sl = pl.ds(pl.multiple_of(gi * SUBLANES, SUBLANES), SUBLANES)
                w_scr[sl, re_l], w_scr[sl, im_l] = w_r, w_i
                return w_r, w_i

            lax.fori_loop(1, seg // SUBLANES, grow, (p_r, p_i))

    for s in range(nseg):
        up_scr[pl.ds(s, seg, stride=nseg), :] = u_ref[s * seg:(s + 1) * seg, :].astype(F32)

    def proj_in(r, c):
        sl = pl.ds(pl.multiple_of(r * row_chunk, row_chunk), row_chunk)
        h_scr[sl, :] = _dot(up_scr[sl, :].astype(BF16), bh_scr[...])
        return c

    lax.fori_loop(0, seq // row_chunk, proj_in, 0)

    lam = [(jnp.broadcast_to(pwr_ref[0:1, re_l], full), jnp.broadcast_to(pwi_ref[0:1, re_l], full))
           for re_l, _ in lanes]
    per_it = 2

    def scan_t(it, carry):
        base = pl.multiple_of(it * (per_it * SUBLANES), per_it * SUBLANES)
        sls = [pl.ds(base + j * SUBLANES, SUBLANES) for j in range(per_it)]
        bu = [[(h_scr[sl, re_l], h_scr[sl, im_l]) for re_l, im_l in lanes] for sl in sls]
        hs, outs = list(carry), []
        for j in range(per_it):
            hs = [(bu[j][c][0] + (hs[c][0] * lam[c][0] - hs[c][1] * lam[c][1]),
                   bu[j][c][1] + (hs[c][0] * lam[c][1] + hs[c][1] * lam[c][0])) for c in range(n_lb)]
            outs.append(hs)
        for sl, row in zip(sls, outs):
            for (re_l, im_l), (h_r, h_i) in zip(lanes, row):
                h_scr[sl, re_l], h_scr[sl, im_l] = h_r, h_i
        return tuple(hs)

    zero = jnp.zeros(full, F32)
    ends = lax.fori_loop(0, seg // per_it, scan_t, tuple((zero, zero) for _ in range(n_lb)))

    rowid = lax.broadcasted_iota(jnp.int32, full, 0)
    enter = []
    for (re_l, im_l), (e_r, e_i) in zip(lanes, ends):
        ws_r, ws_i = w_scr[seg - 1:seg, re_l], w_scr[seg - 1:seg, im_l]
        c_r, c_i = h0_ref[:, re_l], h0_ref[:, im_l]
        cv_r, cv_i = jnp.broadcast_to(c_r, full), jnp.broadcast_to(c_i, full)
        for s in range(1, nseg + 1):
            c_r, c_i = (e_r[s - 1:s] + (c_r * ws_r - c_i * ws_i), e_i[s - 1:s] + (c_r * ws_i + c_i * ws_r))
            if s < nseg:
                cv_r = jnp.where(rowid == s, jnp.broadcast_to(c_r, full), cv_r)
                cv_i = jnp.where(rowid == s, jnp.broadcast_to(c_i, full), cv_i)
        hl_ref[:, re_l], hl_ref[:, im_l] = c_r, c_i
        enter.append((cv_r, cv_i))

    def fix_t(gi, c):
        wsl = pl.ds(pl.multiple_of(gi * SUBLANES, SUBLANES), SUBLANES)
        wv = [(w_scr[wsl, re_l], w_scr[wsl, im_l]) for re_l, im_l in lanes]
        base = pl.multiple_of(gi * (SUBLANES * SUBLANES), SUBLANES * SUBLANES)
        for j in range(SUBLANES):
            sl = pl.ds(base + j * SUBLANES, SUBLANES)
            for (re_l, im_l), (cv_r, cv_i), (wv_r, wv_i) in zip(lanes, enter, wv):
                w_r = jnp.broadcast_to(wv_r[j:j + 1], full)
                w_i = jnp.broadcast_to(wv_i[j:j + 1], full)
                h_r = h_scr[sl, re_l] + (w_r * cv_r - w_i * cv_i)
                h_i = h_scr[sl, im_l] + (w_r * cv_i + w_i * cv_r)
                h_scr[sl, re_l], h_scr[sl, im_l] = h_r, h_i
        return c

    lax.fori_loop(0, seg // SUBLANES, fix_t, 0)

    def proj_out(r, c):
        sl = pl.ds(pl.multiple_of(r * row_chunk, row_chunk), row_chunk)
        y = _dot(h_scr[sl, :].astype(BF16), ch_scr[...]) + d_ref[...] * up_scr[sl, :]
        yp_scr[sl, :] = _gelu(y)
        return c

    lax.fori_loop(0, seq // row_chunk, proj_out, 0)

    for s in range(nseg):
        y_ref[s * seg:(s + 1) * seg, :] = yp_scr[pl.ds(s, seg, stride=nseg), :]


def s5_scan_long(z, col0, b_blk, c_blk, pw_re, pw_im, d_skip, h0):
    bsz, seq, _ = z.shape
    n_slab = b_blk.shape[0]
    ns2 = 2 * SLAB_STATE
    row_chunk = min(seq, 1024)
    assert seq % row_chunk == 0 and seq % (2 * SUBLANES * SUBLANES) == 0
    kern = functools.partial(_s5_seg_kernel, seq=seq, row_chunk=row_chunk)
    return pl.pallas_call(
        kern,
        grid=(n_slab, bsz),
        in_specs=[
            pl.BlockSpec((None, seq, LANES), lambda k, b: (b, 0, col0 + k)),
            pl.BlockSpec((None, LANES, ns2), lambda k, b: (k, 0, 0)),
            pl.BlockSpec((None, ns2, LANES), lambda k, b: (k, 0, 0)),
            pl.BlockSpec((None, SUBLANES, SLAB_STATE), lambda k, b: (k, 0, 0)),
            pl.BlockSpec((None, SUBLANES, SLAB_STATE), lambda k, b: (k, 0, 0)),
            pl.BlockSpec((1, LANES), lambda k, b: (0, k)),
            pl.BlockSpec((None, None, 1, ns2), lambda k, b: (b, k, 0, 0)),
        ],
        out_specs=[
            pl.BlockSpec((None, seq, LANES), lambda k, b: (b, 0, k)),
            pl.BlockSpec((None, None, 1, ns2), lambda k, b: (b, k, 0, 0)),
        ],
        out_shape=[
            jax.ShapeDtypeStruct((bsz, seq, n_slab * LANES), F32),
            jax.ShapeDtypeStruct((bsz, n_slab, 1, ns2), F32),
        ],
        scratch_shapes=[
            pltpu.VMEM((seq, LANES), F32),
            pltpu.VMEM((seq, ns2), F32),
            pltpu.VMEM((seq, LANES), F32),
            pltpu.VMEM((seq // SUBLANES, ns2), F32),
            pltpu.VMEM((LANES, ns2), BF16),
            pltpu.VMEM((ns2, LANES), BF16),
        ],
        compiler_params=_params(("arbitrary", "arbitrary"), 40),
        name="s5_scan_long",
    )(z, b_blk, c_blk, pw_re, pw_im, d_skip.reshape(1, -1), h0)


def _glu_kernel(yk_ref, yj_ref, w_ref, b_ref, o_ref, yb_scr):
    @pl.when(pl.program_id(1) == 0)
    def _():
        yb_scr[...] = yk_ref[...].astype(BF16)

    gate = _dot(yb_scr[...], w_ref[...]) + b_ref[...]
    o_ref[...] = (yj_ref[...] * _sigmoid(gate)).astype(o_ref.dtype)


def glu(y, w, b, layer, *, tm, tn):
    t, d = y.shape
    tm = min(tm, t)
    return pl.pallas_call(
        _glu_kernel,
        grid=(t // tm, d // tn),
        in_specs=[
            pl.BlockSpec((tm, d), lambda i, j: (i, 0)),
            pl.BlockSpec((tm, tn), lambda i, j: (i, j)),
            pl.BlockSpec((None, d, tn), lambda i, j: (layer, 0, j)),
            pl.BlockSpec((1, tn), lambda i, j: (0, j)),
        ],
        out_specs=pl.BlockSpec((tm, tn), lambda i, j: (i, j)),
        out_shape=jax.ShapeDtypeStruct((t, d), BF16),
        scratch_shapes=[pltpu.VMEM((tm, d), BF16)],
        compiler_params=_params(("arbitrary", "arbitrary"), 40),
        name="glu",
    )(y, y, w, b.reshape(1, d))


def _pool_kernel(u_ref, buf_ref, w_ref, s_ref, y_ref, tail_ref, ext_scr, *, tc, start_pos, cg):
    c = pl.program_id(1)

    @pl.when(c == 0)
    def _():
        ext_scr[0:POOL_HIST, :] = buf_ref[...]

    ext_scr[POOL_HIST:POOL_HIST + tc, :] = u_ref[...].astype(F32)
    pos = start_pos + c * tc + lax.broadcasted_iota(jnp.int32, (tc, 1), 0)
    for g, win in enumerate(POOL_WINDOWS):
        cols = slice(g * cg, (g + 1) * cg)
        x = ext_scr[:, cols]
        acc, dist = x, 1
        while dist < win:
            acc = acc + pltpu.roll(acc, dist, 0)
            dist *= 2
        wsum = acc[POOL_HIST:, :]
        cnt = jnp.minimum(pos + 1, win).astype(F32)
        zg = wsum * (1.0 / cnt) - x[POOL_HIST:, :]
        y = _dot(zg.astype(BF16), w_ref[g]) * s_ref[:, cols]
        y_ref[:, cols] = y.astype(y_ref.dtype)

    tail = ext_scr[tc:tc + POOL_HIST, :]
    ext_scr[0:POOL_HIST, :] = tail

    @pl.when(c == pl.num_programs(1) - 1)
    def _():
        tail_ref[...] = tail


def pool(z, colblk, buf16, w, layer, scale, *, start_pos, tc):
    bsz, seq, _ = z.shape
    _, n_g, cg, _ = w.shape
    db = n_g * cg
    tc = min(tc, seq)
    kern = functools.partial(_pool_kernel, tc=tc, start_pos=start_pos, cg=cg)
    return pl.pallas_call(
        kern,
        grid=(bsz, seq // tc),
        in_specs=[
            pl.BlockSpec((None, tc, db), lambda b, c: (b, c, colblk)),
            pl.BlockSpec((None, POOL_HIST, db), lambda b, c: (b, 0, 0)),
            pl.BlockSpec((None, n_g, cg, cg), lambda b, c: (layer, 0, 0, 0)),
            pl.BlockSpec((1, db), lambda b, c: (0, 0)),
        ],
        out_specs=[
            pl.BlockSpec((None, tc, db), lambda b, c: (b, c, 0)),
            pl.BlockSpec((None, POOL_HIST, db), lambda b, c: (b, 0, 0)),
        ],
        out_shape=[
            jax.ShapeDtypeStruct((bsz, seq, db), BF16),
            jax.ShapeDtypeStruct((bsz, POOL_HIST, db), F32),
        ],
        scratch_shapes=[pltpu.VMEM((POOL_HIST + tc, db), F32)],
        compiler_params=_params(("arbitrary", "arbitrary"), 40),
        name="pool",
    )(z, buf16, w, scale.reshape(1, db))


def _head_rms(x, g):
    ms = jnp.mean(x * x, axis=-1, keepdims=True)
    return (x * lax.rsqrt(ms + RMS_EPS)) * g


def _combine(os_, lses):
    m = jnp.maximum(jnp.maximum(lses[0], lses[1]), lses[2])
    ws = [jnp.exp(l - m) for l in lses]
    tot = ws[0] + ws[1] + ws[2]
    return (ws[0] * os_[0] + ws[1] * os_[1] + ws[2] * os_[2]) / tot


def _attn_prompt_kernel(q_ref, k_ref, v_ref, qn_ref, kn_ref, att_ref, ko_ref, vo_ref,
                        qs_scr, qf_scr, kf_scr, vf_scr, qd_scr, kd_scr, vd_scr, s_scr, p_scr, m_scr, o_scr, l_scr,
                        *, seq, scale):
    blk = ATT_BLOCK
    rows = 256
    n_all = seq // blk

    def prep(r, c):
        sl = pl.ds(pl.multiple_of(r * rows, rows), rows)
        qs_scr[sl, :] = _head_rms(q_ref[sl, :].astype(F32), qn_ref[...]) * scale
        ko_ref[sl, :] = _head_rms(k_ref[sl, :].astype(F32), kn_ref[...])
        vo_ref[sl, :] = v_ref[sl, :].astype(F32)
        return c

    lax.fori_loop(0, seq // rows, prep, 0, unroll=2)
    kd_scr[0:blk, :] = jnp.zeros((blk, LANES), BF16)
    vd_scr[0:blk, :] = jnp.zeros((blk, 2 * LANES), BF16)
    vd_scr[blk:, LANES:] = jnp.ones((seq, LANES), BF16)

    qi = lax.broadcasted_iota(jnp.int32, (blk, blk), 0)
    kj = lax.broadcasted_iota(jnp.int32, (blk, blk), 1)
    cur_ok = kj <= qi
    prev_ok = kj >= qi
    band_ok = jnp.concatenate([prev_ok, cur_ok], axis=1)
    in_cur = lax.broadcasted_iota(jnp.int32, (blk, 2 * blk), 1) >= blk

    for g, (window, dil) in enumerate(BRANCHES):
        n_blk = seq // (dil * blk)
        col0 = 0 if n_blk > 1 else LANES

        def place(idx, dil=dil, n_blk=n_blk):
            res = idx // n_blk
            n = idx - res * n_blk
            start = res + n * (dil * blk)
            nat = pl.ds(start, blk, stride=dil) if dil > 1 else pl.ds(pl.multiple_of(start, blk), blk)
            cur = pl.ds(pl.multiple_of(idx * blk, blk), blk)
            kcur = pl.ds(pl.multiple_of((idx + 1) * blk, blk), blk)
            kwin = pl.ds(pl.multiple_of(idx * blk, blk), 2 * blk)
            return nat, cur, kcur, kwin, n

        keep_f32 = dil == BRANCHES[1][1]
        two_level = g == 2 and dil == BRANCHES[1][1] ** 2

        def gather(idx, c, place=place, dil=dil, n_blk=n_blk, keep_f32=keep_f32, two_level=two_level):
            nat, cur, kcur, _, n = place(idx)
            if two_level:
                mid = BRANCHES[1][1]
                res = idx // n_blk
                start = (res % mid) * (seq // mid) + res // mid + n * (mid * blk)
                src = pl.ds(start, blk, stride=mid)
                q, k, v = qf_scr[src, :], kf_scr[src, :], vf_scr[src, :]
            else:
                q, k, v = qs_scr[nat, :], ko_ref[nat, :], vo_ref[nat, :]
            if keep_f32:
                qf_scr[cur, :], kf_scr[cur, :], vf_scr[cur, :] = q, k, v
            qd_scr[cur, :] = q.astype(BF16)
            kd_scr[kcur, :] = k.astype(BF16)
            vd_scr[kcur, 0:LANES] = v.astype(BF16)
            return c

        lax.fori_loop(0, n_all, gather, 0, unroll=4)

        def scores(idx, c, place=place, n_blk=n_blk):
            _, cur, kcur, kwin, n = place(idx)
            q = qd_scr[cur, :]
            if n_blk > 1:
                ok = jnp.logical_and(band_ok, jnp.logical_or(in_cur, n > 0))
                s_scr[idx] = jnp.where(ok, _dot_nt(q, kd_scr[kwin, :]), NEG_INF)
            else:
                s_scr[idx, :, LANES:] = jnp.where(cur_ok, _dot_nt(q, kd_scr[kcur, :]), NEG_INF)
            return c

        lax.fori_loop(0, n_all, scores, 0, unroll=8)

        def softmax(idx, c, col0=col0):
            s = s_scr[idx, :, col0:]
            m = jnp.max(s, axis=-1, keepdims=True)
            p_scr[idx, :, col0:] = jnp.exp(s - m).astype(BF16)
            m_scr[idx] = jnp.broadcast_to(m, (blk, LANES))
            return c

        lax.fori_loop(0, n_all, softmax, 0, unroll=4)

        def values(idx, c, g=g, place=place, n_blk=n_blk):
            nat, _, kcur, kwin, _ = place(idx)
            if n_blk > 1:
                ov = _dot(p_scr[idx], vd_scr[kwin, :])
            else:
                ov = _dot(p_scr[idx, :, LANES:], vd_scr[kcur, :])
            l = ov[:, LANES:]
            o_scr[g, nat, :] = ov[:, 0:LANES] / l
            l_scr[g, nat, :] = m_scr[idx] + jnp.log(l)
            return c

        lax.fori_loop(0, n_all, values, 0, unroll=8)

    def comb(r, c):
        sl = pl.ds(pl.multiple_of(r * rows, rows), rows)
        out = _combine([o_scr[g, sl, :] for g in range(3)], [l_scr[g, sl, :] for g in range(3)])
        att_ref[sl, :] = out.astype(att_ref.dtype)
        return c

    lax.fori_loop(0, seq // rows, comb, 0)


def attn_prompt(z, qn, kn, *, n_heads):
    bsz, seq, _ = z.shape
    assert seq % (BRANCHES[-1][1] * ATT_BLOCK) == 0
    hd = LANES
    kern = functools.partial(_attn_prompt_kernel, seq=seq, scale=hd ** -0.5)
    blk = lambda off: pl.BlockSpec((None, seq, hd), lambda b, h: (b, 0, off + h))
    return pl.pallas_call(
        kern,
        grid=(bsz, n_heads),
        in_specs=[blk(0), blk(n_heads), blk(2 * n_heads),
                  pl.BlockSpec((1, hd), lambda b, h: (0, 0)), pl.BlockSpec((1, hd), lambda b, h: (0, 0))],
        out_specs=[blk(0), blk(0), blk(0)],
        out_shape=[
            jax.ShapeDtypeStruct((bsz, seq, n_heads * hd), BF16),
            jax.ShapeDtypeStruct((bsz, seq, n_heads * hd), F32),
            jax.ShapeDtypeStruct((bsz, seq, n_heads * hd), F32),
        ],
        scratch_shapes=[
            pltpu.VMEM((seq, hd), F32),
            pltpu.VMEM((seq, hd), F32), pltpu.VMEM((seq, hd), F32), pltpu.VMEM((seq, hd), F32),
            pltpu.VMEM((seq, hd), BF16), pltpu.VMEM((seq + ATT_BLOCK, hd), BF16),
            pltpu.VMEM((seq + ATT_BLOCK, 2 * hd), BF16),
            pltpu.VMEM((seq // ATT_BLOCK, ATT_BLOCK, 2 * ATT_BLOCK), F32),
            pltpu.VMEM((seq // ATT_BLOCK, ATT_BLOCK, 2 * ATT_BLOCK), BF16),
            pltpu.VMEM((seq // ATT_BLOCK, ATT_BLOCK, hd), F32),
            pltpu.VMEM((3, seq, hd), F32),
            pltpu.VMEM((3, seq, hd), F32),
        ],
        compiler_params=_params(("arbitrary", "arbitrary"), 40),
        name="attn_prompt",
    )(z, z, z, qn.reshape(1, hd), kn.reshape(1, hd))


SAMPLE_PAD = 16


def _attn_sample_kernel(q_ref, k_ref, v_ref, ck_ref, cv_ref, qn_ref, kn_ref, att_ref, ko_ref, vo_ref,
                        q_scr, kn_scr, vn_scr, s_scr, v_scr, o_scr, *, s_new, n_buf, n_heads, pc, scale):
    c = pl.program_id(1)
    n_ch = n_buf // pc
    pad = SAMPLE_PAD

    @pl.when(c == 0)
    def _():
        q_scr[...] = jnp.zeros_like(q_scr)
        kn_scr[...] = jnp.zeros_like(kn_scr)
        vn_scr[...] = jnp.zeros_like(vn_scr)
        for h in range(n_heads):
            lanes = slice(h * LANES, (h + 1) * LANES)
            k_new = _head_rms(k_ref[:, lanes].astype(F32), kn_ref[...])
            v_new = v_ref[:, lanes].astype(F32)
            ko_ref[:, lanes] = k_new
            vo_ref[:, lanes] = v_new
            q_scr[h, 0:s_new, :] = _head_rms(q_ref[:, lanes].astype(F32), qn_ref[...]) * scale
            kn_scr[h, 0:s_new, :] = k_new
            vn_scr[h, 0:s_new, :] = v_new

    def per_head(h, carry):
        rows = pl.ds(h, pc, stride=n_heads)
        s_scr[h, c] = _dot_nt(q_scr[h].astype(BF16), ck_ref[rows, :].astype(BF16))
        v_scr[h, pl.ds(pl.multiple_of(c * pc, pc), pc), :] = cv_ref[rows, :].astype(BF16)
        return carry

    lax.fori_loop(0, n_heads, per_head, 0, unroll=4)

    @pl.when(c == n_ch - 1)
    def _():
        qi = lax.broadcasted_iota(jnp.int32, (pad, pc), 0)
        kj = lax.broadcasted_iota(jnp.int32, (pad, pc), 1)
        qi_n = lax.broadcasted_iota(jnp.int32, (pad, pad), 0)
        kj_n = lax.broadcasted_iota(jnp.int32, (pad, pad), 1)
        dist_n = qi_n - kj_n
        new_ok = jnp.logical_and(dist_n >= 0, kj_n < s_new)

        def finish(h, carry):
            q = q_scr[h].astype(BF16)
            s_n = _dot_nt(q, kn_scr[h].astype(BF16))
            s_c = [s_scr[h, cc] for cc in range(n_ch)]
            ps, pns, ls, ms = [], [], [], []
            for window, dil in BRANCHES:
                msk = []
                for cc in range(n_ch):
                    dist = n_buf + qi - (cc * pc + kj)
                    ok = jnp.logical_and((dist & (dil - 1)) == 0, dist <= window)
                    msk.append(jnp.where(ok, s_c[cc], NEG_INF))
                m_n = jnp.where(jnp.logical_and(new_ok, (dist_n & (dil - 1)) == 0), s_n, NEG_INF)
                m = jnp.max(m_n, axis=-1, keepdims=True)
                for cc in range(n_ch):
                    m = jnp.maximum(m, jnp.max(msk[cc], axis=-1, keepdims=True))
                p_n = jnp.exp(m_n - m)
                l = jnp.sum(p_n, axis=-1, keepdims=True)
                pb = []
                for cc in range(n_ch):
                    p = jnp.exp(msk[cc] - m)
                    l = l + jnp.sum(p, axis=-1, keepdims=True)
                    pb.append(p.astype(BF16))
                ps.append(pb)
                pns.append(p_n.astype(BF16))
                ls.append(l)
                ms.append(m)
            ov = _dot(jnp.concatenate(pns, axis=0), vn_scr[h].astype(BF16))
            for cc in range(n_ch):
                ov = ov + _dot(jnp.concatenate([ps[g][cc] for g in range(3)], axis=0),
                               v_scr[h, cc * pc:(cc + 1) * pc, :])
            outs = [ov[g * pad:(g + 1) * pad] / ls[g] for g in range(3)]
            lses = [jnp.broadcast_to(ms[g] + jnp.log(ls[g]), (pad, LANES)) for g in range(3)]
            o_scr[h] = _combine(outs, lses)
            return carry

        lax.fori_loop(0, n_heads, finish, 0, unroll=2)
        for h in range(n_heads):
            att_ref[:, h * LANES:(h + 1) * LANES] = o_scr[h, 0:s_new, :].astype(att_ref.dtype)


def attn_sample(z, cache_k, cache_v, row0, qn, kn, *, n_heads):
    bsz, s_new, _ = z.shape
    hd = LANES
    d_c = n_heads * hd
    n_buf = cache_k.shape[1] // n_heads
    assert n_buf >= BRANCHES[-1][0] and s_new <= SAMPLE_PAD
    pc = min(n_buf, 512)
    kern = functools.partial(_attn_sample_kernel, s_new=s_new, n_buf=n_buf, n_heads=n_heads, pc=pc,
                             scale=hd ** -0.5)
    blk = lambda off: pl.BlockSpec((None, s_new, d_c), lambda b, c: (b, 0, off))
    cblk = pl.BlockSpec((None, pc * n_heads, hd), lambda b, c: (row0 + b, c, 0))
    vec = pl.BlockSpec((1, hd), lambda b, c: (0, 0))
    return pl.pallas_call(
        kern,
        grid=(bsz, n_buf // pc),
        in_specs=[blk(0), blk(1), blk(2), cblk, cblk, vec, vec],
        out_specs=[blk(0), blk(0), blk(0)],
        out_shape=[
            jax.ShapeDtypeStruct((bsz, s_new, d_c), BF16),
            jax.ShapeDtypeStruct((bsz, s_new, d_c), F32),
            jax.ShapeDtypeStruct((bsz, s_new, d_c), F32),
        ],
        scratch_shapes=[
            pltpu.VMEM((n_heads, SAMPLE_PAD, hd), F32),
            pltpu.VMEM((n_heads, SAMPLE_PAD, hd), F32),
            pltpu.VMEM((n_heads, SAMPLE_PAD, hd), F32),
            pltpu.VMEM((n_heads, n_buf // pc, SAMPLE_PAD, pc), F32),
            pltpu.VMEM((n_heads, n_buf, hd), BF16),
            pltpu.VMEM((n_heads, SAMPLE_PAD, hd), F32),
        ],
        compiler_params=_params(("arbitrary", "arbitrary"), 40),
        name="attn_sample",
    )(z, z, z, cache_k, cache_v, qn.reshape(1, hd), kn.reshape(1, hd))


def _sgu_kernel(gu_ref, gv_ref, lg_ref, lb_ref, w_ref, bt_ref, o_ref, vn_ref, vb_scr, *, rows, n_g, cd):
    t = w_ref.shape[1]
    gv = _gelu(gv_ref[...].astype(F32))
    mu = jnp.mean(gv, axis=-1, keepdims=True)
    xc = gv - mu
    var = jnp.mean(xc * xc, axis=-1, keepdims=True)
    vn = (xc * lax.rsqrt(var + LN_EPS)) * lg_ref[...] + lb_ref[...]
    vn_ref[...] = vn
    if rows < t:
        vb_scr[...] = jnp.zeros_like(vb_scr)
    vb_scr[0:rows, :] = vn.astype(BF16)
    ri = lax.broadcasted_iota(jnp.int32, (t, t), 0)
    ci = lax.broadcasted_iota(jnp.int32, (t, t), 1)
    for g in range(n_g):
        cols = slice(g * cd, (g + 1) * cd)
        wg = jnp.where(ri >= ci, w_ref[g], 0.0).astype(BF16)
        mixed = _dot(wg, vb_scr[:, cols])[0:rows, :] + bt_ref[:, g:g + 1]
        o_ref[:, cols] = (_gelu(gu_ref[:, cols].astype(F32)) * mixed).astype(o_ref.dtype)


def sgu(z, colblk_u, ln_g, ln_b, w_s, b_s):
    bsz, seq, _ = z.shape
    n_g = w_s.shape[0]
    dd = ln_g.shape[0]
    cd = dd // n_g
    t = min(seq, CHUNK)
    tp = max(t, LANES)
    w = jnp.pad(w_s[:, :t, :t], ((0, 0), (0, tp - t), (0, tp - t)))
    bt = jnp.transpose(b_s[:, :t])
    kern = functools.partial(_sgu_kernel, rows=t, n_g=n_g, cd=cd)
    return pl.pallas_call(
        kern,
        grid=(bsz, seq // t),
        in_specs=[
            pl.BlockSpec((None, t, dd), lambda b, c: (b, c, colblk_u)),
            pl.BlockSpec((None, t, dd), lambda b, c: (b, c, colblk_u + 1)),
            pl.BlockSpec((1, dd), lambda b, c: (0, 0)),
            pl.BlockSpec((1, dd), lambda b, c: (0, 0)),
            pl.BlockSpec((n_g, tp, tp), lambda b, c: (0, 0, 0)),
            pl.BlockSpec((t, n_g), lambda b, c: (0, 0)),
        ],
        out_specs=[
            pl.BlockSpec((None, t, dd), lambda b, c: (b, c, 0)),
            pl.BlockSpec((None, t, dd), lambda b, c: (b, c, 0)),
        ],
        out_shape=[
            jax.ShapeDtypeStruct((bsz, seq, dd), BF16),
            jax.ShapeDtypeStruct((bsz, seq, dd), F32),
        ],
        scratch_shapes=[pltpu.VMEM((tp, dd), BF16)],
        compiler_params=_params(("arbitrary", "arbitrary"), 40),
        name="sgu",
    )(z, z, ln_g.reshape(1, dd), ln_b.reshape(1, dd), w, bt)


def _in_proj(x, norm_g, w_in, i, tiles):
    if tiles["cast"]:
        return norm_matmul_cast(x, norm_g, w_in, i, tn=MIX_TILE)
    return norm_matmul(x, norm_g, w_in, tm=tiles["tm"], out_dtype=tiles["z_dtype"]), None


def _res_proj(x, a, b, w_out, i, tiles):
    if tiles["cast"]:
        return out_proj_cast(x, a, b, w_out, i, tn=MIX_TILE)
    return out_proj(x, a, b, w_out, tm=tiles["tm_out"]), None


def _even_layer(x, bsz, seq, h0_re, h0_im, pool_buf, start_pos, norm_g, w_in, w_out, i, s5p, pool_w, pool_scale,
                w_glu, b_glu, d_skip, tiles):
    t, d = x.shape
    pw_re, pw_im, b_blk, c_blk = s5p
    n_slab = b_blk.shape[0]
    d_a = n_slab * LANES
    z, w_in_b = _in_proj(x, norm_g, w_in, i, tiles)
    z = z.reshape(bsz, seq, -1)
    h0 = jnp.concatenate([h0_re.reshape(bsz, n_slab, 1, SLAB_STATE), h0_im.reshape(bsz, n_slab, 1, SLAB_STATE)], axis=-1)
    if tiles["s5_long"]:
        y_pre, h_last = s5_scan_long(z, 0, b_blk, c_blk, pw_re, pw_im, d_skip, h0)
    else:
        y_pre, h_last = s5_scan(z, 0, b_blk, c_blk, pw_re, pw_im, d_skip, h0, split_in=True)
    ya = glu(y_pre.reshape(t, d_a), w_glu, b_glu, i, tm=tiles["tm_glu"], tn=MIX_TILE)
    buf16 = jnp.pad(pool_buf, ((0, 0), (POOL_HIST - pool_buf.shape[1], 0), (0, 0)))
    yb, tail = pool(z, 1, buf16, pool_w, i, pool_scale, start_pos=start_pos, tc=256)
    x, w_out_b = _res_proj(x, ya, yb.reshape(t, -1), w_out, i, tiles)
    g_a = n_slab * SLAB_GROUPS
    h_re = h_last[..., :SLAB_STATE].reshape(bsz, g_a, S5_P)
    h_im = h_last[..., SLAB_STATE:].reshape(bsz, g_a, S5_P)
    return x, h_re, h_im, tail[:, POOL_HIST - pool_buf.shape[1]:], (w_in_b, w_out_b)


def _odd_layer(x, bsz, seq, k_buf, v_buf, norm_g, w_in, w_out, i, qn, kn, ln_g, ln_b, w_s, b_s, n_heads, tiles):
    t, d = x.shape
    d_c = n_heads * LANES
    z, w_in_b = _in_proj(x, norm_g, w_in, i, tiles)
    z = z.reshape(bsz, seq, -1)
    if k_buf is None:
        att, k_new, v_new = attn_prompt(z, qn, kn, n_heads=n_heads)
    else:
        att, k_new, v_new = attn_sample(z, k_buf, v_buf, i * bsz, qn, kn, n_heads=n_heads)
    dd = ln_g.shape[0]
    sg, vn = sgu(z, (3 * d_c) // dd, ln_g, ln_b, w_s, b_s)
    x, w_out_b = _res_proj(x, att.reshape(t, d_c), sg.reshape(t, dd), w_out, i, tiles)
    hd = LANES
    return x, k_new.reshape(bsz, seq, n_heads, hd), v_new.reshape(bsz, seq, n_heads, hd), vn, (w_in_b, w_out_b)


def kernel(x_prompt, x_sample, state_s5_re, state_s5_im, state_pool, cache_k, cache_v, norm_mix, norm_ffn, ev_w_in, ev_w_out, s5_lambda_re, s5_lambda_im, s5_log_dt, s5_b_re, s5_b_im, s5_c_re, s5_c_im, s5_d, s5_w_glu, s5_b_glu, pool_w, pool_scale, od_w_in, od_w_out, q_norm, k_norm, sgu_ln_g, sgu_ln_b, sgu_w, sgu_b, ffn_w1, ffn_w3, ffn_w2):
    bp, lp, d = x_prompt.shape
    bs, ls, _ = x_sample.shape
    depth = norm_mix.shape[0]
    n_heads = cache_k.shape[3]
    xp = x_prompt.reshape(bp * lp, d)
    xs = x_sample.reshape(bs * ls, d)
    tiles_p = dict(cast=False, tm=1024, tm_glu=512, tm_out=1024, s5_long=True, z_dtype=BF16)
    tiles_s = dict(cast=True, tm_glu=bs * ls, s5_long=False)
    g_a, p_a = s5_lambda_re.shape[1:]
    w_glu_b, pool_w_b = s5_w_glu.astype(BF16), pool_w.astype(BF16)
    d_c = n_heads * LANES
    cache_k2 = cache_k.reshape(-1, cache_k.shape[2] * n_heads, LANES)
    cache_v2 = cache_v.reshape(-1, cache_v.shape[2] * n_heads, LANES)

    s5r_p, s5i_p, pool_p, k_p, v_p = [], [], [], [], []
    s5r_s, s5i_s, pool_s, k_s, v_s, sgu_s = [], [], [], [], [], []
    for l in range(depth):
        i = l // 2
        if l % 2 == 0:
            pw_re, pw_im, bb_re, bb_im = s5_prep(s5_lambda_re[i], s5_lambda_im[i], s5_log_dt[i], s5_b_re[i], s5_b_im[i])
            b_blk, c_blk = s5_block_matrices(bb_re, bb_im, s5_c_re[i], s5_c_im[i])
            s5p = (pw_re, pw_im, b_blk, c_blk)
            rest = (i, s5p, pool_w_b, pool_scale[i], w_glu_b, s5_b_glu[i], s5_d[i])
            xs, hr, hi, buf, (w_in_b, w_out_b) = _even_layer(
                xs, bs, ls, state_s5_re[i], state_s5_im[i], state_pool[i], PAST_LEN, norm_mix[l], ev_w_in, ev_w_out,
                *rest, tiles_s)
            s5r_s.append(hr); s5i_s.append(hi); pool_s.append(buf)
            zero_h = jnp.zeros((bp, g_a, p_a), F32)
            zero_buf = jnp.zeros((bp, state_pool.shape[2], state_pool.shape[3]), F32)
            xp, hr, hi, buf, _ = _even_layer(xp, bp, lp, zero_h, zero_h, zero_buf, 0, norm_mix[l], w_in_b, w_out_b,
                                             *rest, tiles_p)
            s5r_p.append(hr); s5i_p.append(hi); pool_p.append(buf)
        else:
            rest = (i, q_norm[i], k_norm[i], sgu_ln_g[i], sgu_ln_b[i], sgu_w[i], sgu_b[i], n_heads)
            xs, nk, nv, vrows, (w_in_b, w_out_b) = _odd_layer(xs, bs, ls, cache_k2, cache_v2, norm_mix[l], od_w_in,
                                                              od_w_out, *rest, tiles_s)
            k_s.append(nk); v_s.append(nv); sgu_s.append(vrows)
            xp, nk, nv, _, _ = _odd_layer(xp, bp, lp, None, None, norm_mix[l], w_in_b, w_out_b, *rest, tiles_p)
            k_p.append(nk); v_p.append(nv)
        xs, w1_b, w3_b, w2_b = ffn_cast(xs, norm_ffn[l], ffn_w1, ffn_w3, ffn_w2, l, tf=FFN_TILE)
        xp = ffn(xp, norm_ffn[l], w1_b, w3_b, w2_b, tm=512)
    return (xp.reshape(bp, lp, d), xs.reshape(bs, ls, d),
            jnp.stack(s5r_p), jnp.stack(s5i_p), jnp.stack(pool_p), jnp.stack(k_p), jnp.stack(v_p),
            jnp.stack(s5r_s), jnp.stack(s5i_s), jnp.stack(pool_s), jnp.stack(k_s), jnp.stack(v_s),
            jnp.stack(sgu_s))
```

```python
import functools
import math

import jax
import jax.numpy as jnp
from jax import lax
from jax.experimental import pallas as pl
from jax.experimental.pallas import tpu as pltpu

F32 = jnp.float32
BF16 = jnp.bfloat16

RMS_EPS = 1e-6
LN_EPS = 1e-5
NEG_INF = -1e30

LANES = 128
SUBLANES = 8
ATT_BLOCK = 128
CHUNK = 128
POOL_WINDOWS = (2, 4, 8, 16)
POOL_HIST = 16
BRANCHES = ((128, 1), (512, 4), (2048, 16))
S5_GRP = 16
S5_P = 64
SLAB_GROUPS = LANES // S5_GRP
SLAB_STATE = SLAB_GROUPS * S5_P
PAST_LEN = 8192
FFN_TILE = 256
FFN_GROUP = 2
MIX_TILE = 512


def _params(sem, vmem_mib):
    return pltpu.CompilerParams(dimension_semantics=sem, vmem_limit_bytes=vmem_mib << 20)


def _gelu(x):
    return 0.5 * x * (1.0 + lax.erf(x * (1.0 / math.sqrt(2.0))))


def _sigmoid(x):
    return 1.0 / (1.0 + jnp.exp(-x))


def _split_bf16(a):
    hi = a.astype(BF16)
    lo = (a - hi.astype(F32)).astype(BF16)
    return hi, lo


def _dot(a, b):
    return jnp.dot(a, b, preferred_element_type=F32)


def _dot_nt(a, b):
    return lax.dot_general(a, b, (((1,), (1,)), ((), ())), preferred_element_type=F32)


def _rms_rows_to(x_ref, g_ref, h_ref, rows):
    step = 16 if rows % 16 == 0 else rows

    def body(r, c):
        sl = pl.ds(pl.multiple_of(r * step, step), step)
        x = x_ref[sl, :]
        ms = jnp.mean(x * x, axis=-1, keepdims=True)
        h_ref[sl, :] = ((x * lax.rsqrt(ms + RMS_EPS)) * g_ref[...]).astype(h_ref.dtype)
        return c

    lax.fori_loop(0, rows // step, body, 0, unroll=min(4, rows // step))


def _norm_matmul_kernel(x_ref, g_ref, w_ref, o_ref, h_ref):
    @pl.when(pl.program_id(1) == 0)
    def _():
        _rms_rows_to(x_ref, g_ref, h_ref, x_ref.shape[0])

    o_ref[...] = _dot(h_ref[...], w_ref[...]).astype(o_ref.dtype)


def norm_matmul(x, g, w, *, tm, out_dtype=F32):
    t, d = x.shape
    n_t, _, tn = w.shape
    return pl.pallas_call(
        _norm_matmul_kernel,
        grid=(t // tm, n_t),
        in_specs=[
            pl.BlockSpec((tm, d), lambda i, j: (i, 0)),
            pl.BlockSpec((1, d), lambda i, j: (0, 0)),
            pl.BlockSpec((None, d, tn), lambda i, j: (j, 0, 0)),
        ],
        out_specs=pl.BlockSpec((tm, tn), lambda i, j: (i, j)),
        out_shape=jax.ShapeDtypeStruct((t, n_t * tn), out_dtype),
        scratch_shapes=[pltpu.VMEM((tm, d), BF16)],
        compiler_params=_params(("arbitrary", "arbitrary"), 58),
        name="norm_matmul",
    )(x, g.reshape(1, d), w)


def _norm_matmul_cast_kernel(x_ref, g_ref, w_ref, o_ref, wb_ref, h_ref):
    @pl.when(pl.program_id(0) == 0)
    def _():
        _rms_rows_to(x_ref, g_ref, h_ref, x_ref.shape[0])

    wb_ref[...] = w_ref[...].astype(BF16)
    o_ref[...] = _dot(h_ref[...], wb_ref[...]).astype(o_ref.dtype)


def norm_matmul_cast(x, g, w, layer, *, tn):
    t, d = x.shape
    n = w.shape[2]
    return pl.pallas_call(
        _norm_matmul_cast_kernel,
        grid=(n // tn,),
        in_specs=[
            pl.BlockSpec((t, d), lambda j: (0, 0)),
            pl.BlockSpec((1, d), lambda j: (0, 0)),
            pl.BlockSpec((None, d, tn), lambda j: (layer, 0, j)),
        ],
        out_specs=[
            pl.BlockSpec((t, tn), lambda j: (0, j)),
            pl.BlockSpec((None, d, tn), lambda j: (j, 0, 0)),
        ],
        out_shape=[
            jax.ShapeDtypeStruct((t, n), F32),
            jax.ShapeDtypeStruct((n // tn, d, tn), BF16),
        ],
        scratch_shapes=[pltpu.VMEM((t, d), BF16)],
        compiler_params=_params(("arbitrary",), 48),
        name="norm_matmul_cast",
    )(x, g.reshape(1, d), w)


FFN_OUT_CHUNK = 512


def _ffn_step(first, x_ref, g_ref, w1_ref, w3_ref, w2_ref, o_ref, h_ref):
    @pl.when(first)
    def _():
        _rms_rows_to(x_ref, g_ref, h_ref, x_ref.shape[0])
        o_ref[...] = x_ref[...]

    h = h_ref[...]
    a = _dot(h, w1_ref[...])
    b = _dot(h, w3_ref[...])
    u = ((a * _sigmoid(a)) * b).astype(BF16)
    for c in range(0, o_ref.shape[1], FFN_OUT_CHUNK):
        o_ref[:, c:c + FFN_OUT_CHUNK] += _dot(u, w2_ref[:, c:c + FFN_OUT_CHUNK])


def _ffn_kernel(x_ref, g_ref, w1_ref, w3_ref, w2_ref, o_ref, h_ref):
    _ffn_step(pl.program_id(1) == 0, x_ref, g_ref, w1_ref, w3_ref, w2_ref, o_ref, h_ref)


def ffn(x, g, w1, w3, w2, *, tm):
    t, d = x.shape
    n_f, _, tf = w1.shape
    return pl.pallas_call(
        _ffn_kernel,
        grid=(t // tm, n_f),
        in_specs=[
            pl.BlockSpec((tm, d), lambda i, j: (i, 0), pipeline_mode=pl.Buffered(1)),
            pl.BlockSpec((1, d), lambda i, j: (0, 0)),
            pl.BlockSpec((None, d, tf), lambda i, j: (j, 0, 0)),
            pl.BlockSpec((None, d, tf), lambda i, j: (j, 0, 0)),
            pl.BlockSpec((tf, d), lambda i, j: (j, 0)),
        ],
        out_specs=pl.BlockSpec((tm, d), lambda i, j: (i, 0)),
        out_shape=jax.ShapeDtypeStruct((t, d), F32),
        scratch_shapes=[pltpu.VMEM((tm, d), BF16)],
        compiler_params=_params(("arbitrary", "arbitrary"), 60),
        name="ffn",
    )(x, g.reshape(1, d), w1, w3, w2)


def _ffn_cast_kernel(x_ref, g_ref, w1_ref, w3_ref, w2_ref, o_ref, w1b_ref, w3b_ref, w2b_ref, h_ref, *, n_live):
    live = pl.program_id(0) < n_live
    w1b_ref[...] = jnp.where(live, w1_ref[...], 0.0).astype(BF16)
    w3b_ref[...] = jnp.where(live, w3_ref[...], 0.0).astype(BF16)
    w2b_ref[...] = jnp.where(live, w2_ref[...], 0.0).astype(BF16)
    _ffn_step(pl.program_id(0) == 0, x_ref, g_ref, w1b_ref, w3b_ref, w2b_ref, o_ref, h_ref)


def ffn_cast(x, g, w1, w3, w2, layer, *, tf, group):
    t, d = x.shape
    f = w1.shape[2]
    n_live = f // tf
    n_big = -(-n_live // group)
    steps = n_big * group
    src = lambda j: jnp.minimum(j, n_live - 1)
    return pl.pallas_call(
        functools.partial(_ffn_cast_kernel, n_live=n_live),
        grid=(steps,),
        in_specs=[
            pl.BlockSpec((t, d), lambda j: (0, 0)),
            pl.BlockSpec((1, d), lambda j: (0, 0)),
            pl.BlockSpec((None, d, tf), lambda j: (layer, 0, src(j))),
            pl.BlockSpec((None, d, tf), lambda j: (layer, 0, src(j))),
            pl.BlockSpec((None, tf, d), lambda j: (layer, src(j), 0)),
        ],
        out_specs=[
            pl.BlockSpec((t, d), lambda j: (0, 0)),
            pl.BlockSpec((None, d, tf), lambda j: (j // group, 0, j % group)),
            pl.BlockSpec((None, d, tf), lambda j: (j // group, 0, j % group)),
            pl.BlockSpec((tf, d), lambda j: (j, 0)),
        ],
        out_shape=[
            jax.ShapeDtypeStruct((t, d), F32),
            jax.ShapeDtypeStruct((n_big, d, group * tf), BF16),
            jax.ShapeDtypeStruct((n_big, d, group * tf), BF16),
            jax.ShapeDtypeStruct((steps * tf, d), BF16),
        ],
        scratch_shapes=[pltpu.VMEM((t, d), BF16)],
        compiler_params=_params(("arbitrary",), 48),
        name="ffn_cast",
    )(x, g.reshape(1, d), w1, w3, w2)


def _out_proj_kernel(x_ref, a_ref, b_ref, wa_ref, wb_ref, o_ref):
    o_ref[...] = x_ref[...] + _dot(a_ref[...], wa_ref[...]) + _dot(b_ref[...], wb_ref[...])


def out_proj(x, a, b, w_pair, *, tm):
    t, d = x.shape
    wa, wb = w_pair
    n_t, k, tn = wa.shape
    assert a.shape[1] == k and b.shape[1] == k and wb.shape == wa.shape
    return pl.pallas_call(
        _out_proj_kernel,
        grid=(t // tm, n_t),
        in_specs=[
            pl.BlockSpec((tm, tn), lambda i, j: (i, j)),
            pl.BlockSpec((tm, k), lambda i, j: (i, 0)),
            pl.BlockSpec((tm, k), lambda i, j: (i, 0)),
            pl.BlockSpec((None, k, tn), lambda i, j: (j, 0, 0)),
            pl.BlockSpec((None, k, tn), lambda i, j: (j, 0, 0)),
        ],
        out_specs=pl.BlockSpec((tm, tn), lambda i, j: (i, j)),
        out_shape=jax.ShapeDtypeStruct((t, d), F32),
        compiler_params=_params(("arbitrary", "arbitrary"), 48),
        name="out_proj",
    )(x, a, b, wa, wb)


def _out_proj_cast_kernel(x_ref, a_ref, b_ref, wa_ref, wb_ref, o_ref, wab_ref, wbb_ref):
    wab_ref[...] = wa_ref[...].astype(BF16)
    wbb_ref[...] = wb_ref[...].astype(BF16)
    o_ref[...] = x_ref[...] + _dot(a_ref[...], wab_ref[...]) + _dot(b_ref[...], wbb_ref[...])


def out_proj_cast(x, a, b, w, layer, *, tn):
    t, d = x.shape
    k = a.shape[1]
    assert b.shape[1] == k and w.shape[1] == 2 * k
    out, wa_b, wb_b = pl.pallas_call(
        _out_proj_cast_kernel,
        grid=(d // tn,),
        in_specs=[
            pl.BlockSpec((t, tn), lambda j: (0, j)),
            pl.BlockSpec((t, k), lambda j: (0, 0)),
            pl.BlockSpec((t, k), lambda j: (0, 0)),
            pl.BlockSpec((None, k, tn), lambda j: (layer, 0, j)),
            pl.BlockSpec((None, k, tn), lambda j: (layer, 1, j)),
        ],
        out_specs=[
            pl.BlockSpec((t, tn), lambda j: (0, j)),
            pl.BlockSpec((None, k, tn), lambda j: (j, 0, 0)),
            pl.BlockSpec((None, k, tn), lambda j: (j, 0, 0)),
        ],
        out_shape=[
            jax.ShapeDtypeStruct((t, d), F32),
            jax.ShapeDtypeStruct((d // tn, k, tn), BF16),
            jax.ShapeDtypeStruct((d // tn, k, tn), BF16),
        ],
        compiler_params=_params(("arbitrary",), 48),
        name="out_proj_cast",
    )(x, a, b, w, w)
    return out, (wa_b, wb_b)


def _s5_prep_kernel(lr_ref, li_ref, ldt_ref, lrx_ref, lix_ref, ldtx_ref, br_ref, bi_ref,
                    pwr_ref, pwi_ref, bbr_ref, bbi_ref):
    dt = jnp.exp(ldt_ref[...])
    mag = jnp.exp(lr_ref[...] * dt)
    ang = li_ref[...] * dt
    p_r, p_i = mag * jnp.cos(ang), mag * jnp.sin(ang)
    c_r, c_i = p_r, p_i
    pwr_ref[0], pwi_ref[0] = c_r, c_i
    for j in range(1, SUBLANES):
        c_r, c_i = c_r * p_r - c_i * p_i, c_r * p_i + c_i * p_r
        pwr_ref[j], pwi_ref[j] = c_r, c_i
    lr, li = lrx_ref[...], lix_ref[...]
    dtx = jnp.exp(ldtx_ref[...])
    magx = jnp.exp(lr * dtx)
    angx = li * dtx
    nr, ni = magx * jnp.cos(angx) - 1.0, magx * jnp.sin(angx)
    den = lr * lr + li * li
    qr = (nr * lr + ni * li) / den
    qi = (ni * lr - nr * li) / den
    br, bi = br_ref[...], bi_ref[...]
    bbr_ref[...] = qr * br - qi * bi
    bbi_ref[...] = qr * bi + qi * br


def s5_prep(lam_re, lam_im, log_dt, b_re, b_im):
    g, p = lam_re.shape
    h = b_re.shape[2]
    n_slab = g // SLAB_GROUPS
    slab = lambda a: a.reshape(n_slab, SLAB_GROUPS * p)
    rep = lambda a: jnp.repeat(a, h, axis=1)
    ldt_gp = jnp.broadcast_to(log_dt[:, None], (g, p))
    outs = pl.pallas_call(
        _s5_prep_kernel,
        out_shape=[
            jax.ShapeDtypeStruct((SUBLANES, n_slab, SLAB_GROUPS * p), F32),
            jax.ShapeDtypeStruct((SUBLANES, n_slab, SLAB_GROUPS * p), F32),
            jax.ShapeDtypeStruct((g, p * h), F32),
            jax.ShapeDtypeStruct((g, p * h), F32),
        ],
        name="s5_prep",
    )(slab(lam_re), slab(lam_im), slab(ldt_gp), rep(lam_re), rep(lam_im), rep(ldt_gp),
      b_re.reshape(g, p * h), b_im.reshape(g, p * h))
    pw_re, pw_im, bb_re, bb_im = outs
    pw_re = jnp.transpose(pw_re, (1, 0, 2))
    pw_im = jnp.transpose(pw_im, (1, 0, 2))
    return pw_re, pw_im, bb_re.reshape(g, p, h), bb_im.reshape(g, p, h)


def s5_block_matrices(bb_re, bb_im, c_re, c_im):
    g, p, h = bb_re.shape
    n_slab = g // SLAB_GROUPS
    eye = jnp.eye(SLAB_GROUPS, dtype=F32)

    def in_map(bb):
        t = bb.reshape(n_slab, SLAB_GROUPS, p, h)
        return jnp.einsum("kgph,gj->kghjp", t, eye).reshape(n_slab, SLAB_GROUPS * h, SLAB_GROUPS * p)

    def out_map(c):
        t = c.reshape(n_slab, SLAB_GROUPS, h, p)
        return jnp.einsum("kghp,gj->kgpjh", t, eye).reshape(n_slab, SLAB_GROUPS * p, SLAB_GROUPS * h)

    b_blk = jnp.concatenate([in_map(bb_re), in_map(bb_im)], axis=2)
    c_blk = jnp.concatenate([out_map(c_re), -out_map(c_im)], axis=1)
    return b_blk, c_blk


def _s5_scan_kernel(u_ref, bblk_ref, cblk_ref, pwr_ref, pwi_ref, d_ref, h0_ref,
                    y_ref, hl_ref, h_scr, bh_scr, bl_scr, ch_scr, *, seq, row_chunk, split_in):
    ns = SLAB_STATE

    @pl.when(pl.program_id(1) == 0)
    def _():
        bh, bl = _split_bf16(bblk_ref[...])
        bh_scr[...], bl_scr[...] = bh, bl
        ch_scr[...] = cblk_ref[...].astype(BF16)

    n_chunks = seq // row_chunk

    def proj_in(r, c):
        sl = pl.ds(pl.multiple_of(r * row_chunk, row_chunk), row_chunk)
        if split_in:
            uh, ul = _split_bf16(u_ref[sl, :].astype(F32))
            h_scr[sl, :] = _dot(uh, bh_scr[...]) + _dot(ul, bh_scr[...]) + _dot(uh, bl_scr[...])
        else:
            h_scr[sl, :] = _dot(u_ref[sl, :].astype(BF16), bh_scr[...])
        return c

    lax.fori_loop(0, n_chunks, proj_in, 0)

    rowid = lax.broadcasted_iota(jnp.int32, (SUBLANES, LANES), 0)
    for c in range(ns // LANES):
        re_l = slice(c * LANES, (c + 1) * LANES)
        im_l = slice(ns + c * LANES, ns + (c + 1) * LANES)
        p_r, p_i = pwr_ref[:, re_l], pwi_ref[:, re_l]
        steps = []
        for dist in (1, 2, 4):
            a_r = jnp.where(rowid >= dist, jnp.broadcast_to(p_r[dist - 1:dist], (SUBLANES, LANES)), 0.0)
            a_i = jnp.where(rowid >= dist, jnp.broadcast_to(p_i[dist - 1:dist], (SUBLANES, LANES)), 0.0)
            steps.append((dist, a_r, a_i))
        c_r = jnp.broadcast_to(h0_ref[:, re_l], (SUBLANES, LANES))
        c_i = jnp.broadcast_to(h0_ref[:, im_l], (SUBLANES, LANES))

        last = slice(SUBLANES - 1, SUBLANES)
        full = (SUBLANES, LANES)
        p8_r, p8_i = jnp.broadcast_to(p_r[last], full), jnp.broadcast_to(p_i[last], full)
        n_groups = seq // SUBLANES
        per_it = min(4, n_groups)

        def scan_rows(it, carry, re_l=re_l, im_l=im_l, p_r=p_r, p_i=p_i, p8_r=p8_r, p8_i=p8_i, steps=steps):
            base = pl.multiple_of(it * (per_it * SUBLANES), per_it * SUBLANES)
            sls = [pl.ds(base + j * SUBLANES, SUBLANES) for j in range(per_it)]
            loc = []
            for sl in sls:
                r, i = h_scr[sl, re_l], h_scr[sl, im_l]
                for dist, a_r, a_i in steps:
                    s_r, s_i = pltpu.roll(r, dist, 0), pltpu.roll(i, dist, 0)
                    r, i = r + (s_r * a_r - s_i * a_i), i + (s_r * a_i + s_i * a_r)
                loc.append((r, i))
            c_r, c_i = carry
            outs = []
            for r, i in loc:
                outs.append((r + (c_r * p_r - c_i * p_i), i + (c_r * p_i + c_i * p_r)))
                e_r, e_i = jnp.broadcast_to(r[last], full), jnp.broadcast_to(i[last], full)
                c_r, c_i = e_r + (c_r * p8_r - c_i * p8_i), e_i + (c_r * p8_i + c_i * p8_r)
            for sl, (r, i) in zip(sls, outs):
                h_scr[sl, re_l], h_scr[sl, im_l] = r, i
            return c_r, c_i

        c_r, c_i = lax.fori_loop(0, n_groups // per_it, scan_rows, (c_r, c_i))
        hl_ref[:, re_l] = c_r[0:1]
        hl_ref[:, im_l] = c_i[0:1]

    def proj_out(r, c):
        sl = pl.ds(pl.multiple_of(r * row_chunk, row_chunk), row_chunk)
        y = _dot(h_scr[sl, :].astype(BF16), ch_scr[...]) + d_ref[...] * u_ref[sl, :].astype(F32)
        y_ref[sl, :] = _gelu(y)
        return c

    lax.fori_loop(0, n_chunks, proj_out, 0)


def s5_scan(z, col0, b_blk, c_blk, pw_re, pw_im, d_skip, h0, *, split_in):
    bsz, seq, _ = z.shape
    n_slab = b_blk.shape[0]
    ns2 = 2 * SLAB_STATE
    row_chunk = min(seq, 1024)
    kern = functools.partial(_s5_scan_kernel, seq=seq, row_chunk=row_chunk, split_in=split_in)
    return pl.pallas_call(
        kern,
        grid=(n_slab, bsz),
        in_specs=[
            pl.BlockSpec((None, seq, LANES), lambda k, b: (b, 0, col0 + k)),
            pl.BlockSpec((None, LANES, ns2), lambda k, b: (k, 0, 0)),
            pl.BlockSpec((None, ns2, LANES), lambda k, b: (k, 0, 0)),
            pl.BlockSpec((None, SUBLANES, SLAB_STATE), lambda k, b: (k, 0, 0)),
            pl.BlockSpec((None, SUBLANES, SLAB_STATE), lambda k, b: (k, 0, 0)),
            pl.BlockSpec((1, LANES), lambda k, b: (0, k)),
            pl.BlockSpec((None, None, 1, ns2), lambda k, b: (b, k, 0, 0)),
        ],
        out_specs=[
            pl.BlockSpec((None, seq, LANES), lambda k, b: (b, 0, k)),
            pl.BlockSpec((None, None, 1, ns2), lambda k, b: (b, k, 0, 0)),
        ],
        out_shape=[
            jax.ShapeDtypeStruct((bsz, seq, n_slab * LANES), F32),
            jax.ShapeDtypeStruct((bsz, n_slab, 1, ns2), F32),
        ],
        scratch_shapes=[
            pltpu.VMEM((seq, ns2), F32),
            pltpu.VMEM((LANES, ns2), BF16), pltpu.VMEM((LANES, ns2), BF16),
            pltpu.VMEM((ns2, LANES), BF16),
        ],
        compiler_params=_params(("arbitrary", "arbitrary"), 40),
        name="s5_scan",
    )(z, b_blk, c_blk, pw_re, pw_im, d_skip.reshape(1, -1), h0)


def _s5_seg_kernel(u_ref, bblk_ref, cblk_ref, pwr_ref, pwi_ref, d_ref, h0_ref,
                   y_ref, hl_ref, up_scr, h_scr, yp_scr, w_scr, bh_scr, ch_scr, *, seq, row_chunk):
    ns = SLAB_STATE
    nseg = SUBLANES
    seg = seq // nseg
    n_lb = ns // LANES
    full = (SUBLANES, LANES)
    lanes = [(slice(c * LANES, (c + 1) * LANES), slice(ns + c * LANES, ns + (c + 1) * LANES)) for c in range(n_lb)]
    last = slice(SUBLANES - 1, SUBLANES)

    @pl.when(pl.program_id(1) == 0)
    def _():
        bh_scr[...] = bblk_ref[...].astype(BF16)
        ch_scr[...] = cblk_ref[...].astype(BF16)
        for re_l, im_l in lanes:
            p_r, p_i = pwr_ref[:, re_l], pwi_ref[:, re_l]
            p8_r, p8_i = jnp.broadcast_to(p_r[last], full), jnp.broadcast_to(p_i[last], full)
            w_scr[0:SUBLANES, re_l], w_scr[0:SUBLANES, im_l] = p_r, p_i

            def grow(gi, carry, re_l=re_l, im_l=im_l, p8_r=p8_r, p8_i=p8_i):
                w_r, w_i = carry
                w_r, w_i = w_r * p8_r - w_i * p8_i, w_r * p8_i + w_i * p8_r
                sl = pl.ds(pl.multiple_of(gi * SUBLANES, SUBLANES), SUBLANES)
                w_scr[sl, re_l], w_scr[sl, im_l] = w_r, w_i
                return w_r, w_i

            lax.fori_loop(1, seg // SUBLANES, grow, (p_r, p_i))

    for s in range(nseg):
        up_scr[pl.ds(s, seg, stride=nseg), :] = u_ref[s * seg:(s + 1) * seg, :].astype(F32)

    def proj_in(r, c):
        sl = pl.ds(pl.multiple_of(r * row_chunk, row_chunk), row_chunk)
        h_scr[sl, :] = _dot(up_scr[sl, :].astype(BF16), bh_scr[...])
        return c

    lax.fori_loop(0, seq // row_chunk, proj_in, 0)

    lam = [(jnp.broadcast_to(pwr_ref[0:1, re_l], full), jnp.broadcast_to(pwi_ref[0:1, re_l], full))
           for re_l, _ in lanes]
    per_it = 2

    def scan_t(it, carry):
        base = pl.multiple_of(it * (per_it * SUBLANES), per_it * SUBLANES)
        sls = [pl.ds(base + j * SUBLANES, SUBLANES) for j in range(per_it)]
        bu = [[(h_scr[sl, re_l], h_scr[sl, im_l]) for re_l, im_l in lanes] for sl in sls]
        hs, outs = list(carry), []
        for j in range(per_it):
            hs = [(bu[j][c][0] + (hs[c][0] * lam[c][0] - hs[c][1] * lam[c][1]),
                   bu[j][c][1] + (hs[c][0] * lam[c][1] + hs[c][1] * lam[c][0])) for c in range(n_lb)]
            outs.append(hs)
        for sl, row in zip(sls, outs):
            for (re_l, im_l), (h_r, h_i) in zip(lanes, row):
                h_scr[sl, re_l], h_scr[sl, im_l] = h_r, h_i
        return tuple(hs)

    zero = jnp.zeros(full, F32)
    ends = lax.fori_loop(0, seg // per_it, scan_t, tuple((zero, zero) for _ in range(n_lb)))

    rowid = lax.broadcasted_iota(jnp.int32, full, 0)
    enter = []
    for (re_l, im_l), (e_r, e_i) in zip(lanes, ends):
        ws_r, ws_i = w_scr[seg - 1:seg, re_l], w_scr[seg - 1:seg, im_l]
        c_r, c_i = h0_ref[:, re_l], h0_ref[:, im_l]
        cv_r, cv_i = jnp.broadcast_to(c_r, full), jnp.broadcast_to(c_i, full)
        for s in range(1, nseg + 1):
            c_r, c_i = (e_r[s - 1:s] + (c_r * ws_r - c_i * ws_i), e_i[s - 1:s] + (c_r * ws_i + c_i * ws_r))
            if s < nseg:
                cv_r = jnp.where(rowid == s, jnp.broadcast_to(c_r, full), cv_r)
                cv_i = jnp.where(rowid == s, jnp.broadcast_to(c_i, full), cv_i)
        hl_ref[:, re_l], hl_ref[:, im_l] = c_r, c_i
        enter.append((cv_r, cv_i))

    def fix_t(gi, c):
        wsl = pl.ds(pl.multiple_of(gi * SUBLANES, SUBLANES), SUBLANES)
        wv = [(w_scr[wsl, re_l], w_scr[wsl, im_l]) for re_l, im_l in lanes]
        base = pl.multiple_of(gi * (SUBLANES * SUBLANES), SUBLANES * SUBLANES)
        for j in range(SUBLANES):
            sl = pl.ds(base + j * SUBLANES, SUBLANES)
            for (re_l, im_l), (cv_r, cv_i), (wv_r, wv_i) in zip(lanes, enter, wv):
                w_r = jnp.broadcast_to(wv_r[j:j + 1], full)
                w_i = jnp.broadcast_to(wv_i[j:j + 1], full)
                h_r = h_scr[sl, re_l] + (w_r * cv_r - w_i * cv_i)
                h_i = h_scr[sl, im_l] + (w_r * cv_i + w_i * cv_r)
                h_scr[sl, re_l], h_scr[sl, im_l] = h_r, h_i
        return c

    lax.fori_loop(0, seg // SUBLANES, fix_t, 0)

    def proj_out(r, c):
        sl = pl.ds(pl.multiple_of(r * row_chunk, row_chunk), row_chunk)
        y = _dot(h_scr[sl, :].astype(BF16), ch_scr[...]) + d_ref[...] * up_scr[sl, :]
        yp_scr[sl, :] = _gelu(y)
        return c

    lax.fori_loop(0, seq // row_chunk, proj_out, 0)

    for s in range(nseg):
        y_ref[s * seg:(s + 1) * seg, :] = yp_scr[pl.ds(s, seg, stride=nseg), :]


def s5_scan_long(z, col0, b_blk, c_blk, pw_re, pw_im, d_skip, h0):
    bsz, seq, _ = z.shape
    n_slab = b_blk.shape[0]
    ns2 = 2 * SLAB_STATE
    row_chunk = min(seq, 1024)
    assert seq % row_chunk == 0 and seq % (2 * SUBLANES * SUBLANES) == 0
    kern = functools.partial(_s5_seg_kernel, seq=seq, row_chunk=row_chunk)
    return pl.pallas_call(
        kern,
        grid=(n_slab, bsz),
        in_specs=[
            pl.BlockSpec((None, seq, LANES), lambda k, b: (b, 0, col0 + k)),
            pl.BlockSpec((None, LANES, ns2), lambda k, b: (k, 0, 0)),
            pl.BlockSpec((None, ns2, LANES), lambda k, b: (k, 0, 0)),
            pl.BlockSpec((None, SUBLANES, SLAB_STATE), lambda k, b: (k, 0, 0)),
            pl.BlockSpec((None, SUBLANES, SLAB_STATE), lambda k, b: (k, 0, 0)),
            pl.BlockSpec((1, LANES), lambda k, b: (0, k)),
            pl.BlockSpec((None, None, 1, ns2), lambda k, b: (b, k, 0, 0)),
        ],
        out_specs=[
            pl.BlockSpec((None, seq, LANES), lambda k, b: (b, 0, k)),
            pl.BlockSpec((None, None, 1, ns2), lambda k, b: (b, k, 0, 0)),
        ],
        out_shape=[
            jax.ShapeDtypeStruct((bsz, seq, n_slab * LANES), F32),
            jax.ShapeDtypeStruct((bsz, n_slab, 1, ns2), F32),
        ],
        scratch_shapes=[
            pltpu.VMEM((seq, LANES), F32),
            pltpu.VMEM((seq, ns2), F32),
            pltpu.VMEM((seq, LANES), F32),
            pltpu.VMEM((seq // SUBLANES, ns2), F32),
            pltpu.VMEM((LANES, ns2), BF16),
            pltpu.VMEM((ns2, LANES), BF16),
        ],
        compiler_params=_params(("arbitrary", "arbitrary"), 40),
        name="s5_scan_long",
    )(z, b_blk, c_blk, pw_re, pw_im, d_skip.reshape(1, -1), h0)


def _glu_kernel(yk_ref, yj_ref, w_ref, b_ref, o_ref, yb_scr):
    @pl.when(pl.program_id(1) == 0)
    def _():
        yb_scr[...] = yk_ref[...].astype(BF16)

    gate = _dot(yb_scr[...], w_ref[...]) + b_ref[...]
    o_ref[...] = (yj_ref[...] * _sigmoid(gate)).astype(o_ref.dtype)


def glu(y, w, b, layer, *, tm, tn):
    t, d = y.shape
    tm = min(tm, t)
    return pl.pallas_call(
        _glu_kernel,
        grid=(t // tm, d // tn),
        in_specs=[
            pl.BlockSpec((tm, d), lambda i, j: (i, 0)),
            pl.BlockSpec((tm, tn), lambda i, j: (i, j)),
            pl.BlockSpec((None, d, tn), lambda i, j: (layer, 0, j)),
            pl.BlockSpec((1, tn), lambda i, j: (0, j)),
        ],
        out_specs=pl.BlockSpec((tm, tn), lambda i, j: (i, j)),
        out_shape=jax.ShapeDtypeStruct((t, d), BF16),
        scratch_shapes=[pltpu.VMEM((tm, d), BF16)],
        compiler_params=_params(("arbitrary", "arbitrary"), 40),
        name="glu",
    )(y, y, w, b.reshape(1, d))


def _pool_kernel(u_ref, buf_ref, w_ref, s_ref, y_ref, tail_ref, ext_scr, *, tc, start_pos, cg):
    c = pl.program_id(1)

    @pl.when(c == 0)
    def _():
        ext_scr[0:POOL_HIST, :] = buf_ref[...]

    ext_scr[POOL_HIST:POOL_HIST + tc, :] = u_ref[...].astype(F32)
    pos = start_pos + c * tc + lax.broadcasted_iota(jnp.int32, (tc, 1), 0)
    for g, win in enumerate(POOL_WINDOWS):
        cols = slice(g * cg, (g + 1) * cg)
        x = ext_scr[:, cols]
        acc, dist = x, 1
        while dist < win:
            acc = acc + pltpu.roll(acc, dist, 0)
            dist *= 2
        wsum = acc[POOL_HIST:, :]
        cnt = jnp.minimum(pos + 1, win).astype(F32)
        zg = wsum * (1.0 / cnt) - x[POOL_HIST:, :]
        y = _dot(zg.astype(BF16), w_ref[g]) * s_ref[:, cols]
        y_ref[:, cols] = y.astype(y_ref.dtype)

    tail = ext_scr[tc:tc + POOL_HIST, :]
    ext_scr[0:POOL_HIST, :] = tail

    @pl.when(c == pl.num_programs(1) - 1)
    def _():
        tail_ref[...] = tail


def pool(z, colblk, buf16, w, layer, scale, *, start_pos, tc):
    bsz, seq, _ = z.shape
    _, n_g, cg, _ = w.shape
    db = n_g * cg
    tc = min(tc, seq)
    kern = functools.partial(_pool_kernel, tc=tc, start_pos=start_pos, cg=cg)
    return pl.pallas_call(
        kern,
        grid=(bsz, seq // tc),
        in_specs=[
            pl.BlockSpec((None, tc, db), lambda b, c: (b, c, colblk)),
            pl.BlockSpec((None, POOL_HIST, db), lambda b, c: (b, 0, 0)),
            pl.BlockSpec((None, n_g, cg, cg), lambda b, c: (layer, 0, 0, 0)),
            pl.BlockSpec((1, db), lambda b, c: (0, 0)),
        ],
        out_specs=[
            pl.BlockSpec((None, tc, db), lambda b, c: (b, c, 0)),
            pl.BlockSpec((None, POOL_HIST, db), lambda b, c: (b, 0, 0)),
        ],
        out_shape=[
            jax.ShapeDtypeStruct((bsz, seq, db), BF16),
            jax.ShapeDtypeStruct((bsz, POOL_HIST, db), F32),
        ],
        scratch_shapes=[pltpu.VMEM((POOL_HIST + tc, db), F32)],
        compiler_params=_params(("arbitrary", "arbitrary"), 40),
        name="pool",
    )(z, buf16, w, scale.reshape(1, db))


def _head_rms(x, g):
    ms = jnp.mean(x * x, axis=-1, keepdims=True)
    return (x * lax.rsqrt(ms + RMS_EPS)) * g


def _combine(os_, lses):
    m = jnp.maximum(jnp.maximum(lses[0], lses[1]), lses[2])
    ws = [jnp.exp(l - m) for l in lses]
    tot = ws[0] + ws[1] + ws[2]
    return (ws[0] * os_[0] + ws[1] * os_[1] + ws[2] * os_[2]) / tot


def _attn_prompt_kernel(q_ref, k_ref, v_ref, qn_ref, kn_ref, att_ref, ko_ref, vo_ref,
                        qs_scr, qf_scr, kf_scr, vf_scr, qd_scr, kd_scr, vd_scr, s_scr, p_scr, m_scr, o_scr, l_scr,
                        *, seq, scale):
    blk = ATT_BLOCK
    rows = 256
    n_all = seq // blk

    def prep(r, c):
        sl = pl.ds(pl.multiple_of(r * rows, rows), rows)
        qs_scr[sl, :] = _head_rms(q_ref[sl, :].astype(F32), qn_ref[...]) * scale
        ko_ref[sl, :] = _head_rms(k_ref[sl, :].astype(F32), kn_ref[...])
        vo_ref[sl, :] = v_ref[sl, :].astype(F32)
        return c

    lax.fori_loop(0, seq // rows, prep, 0, unroll=2)
    kd_scr[0:blk, :] = jnp.zeros((blk, LANES), BF16)
    vd_scr[0:blk, :] = jnp.zeros((blk, 2 * LANES), BF16)
    vd_scr[blk:, LANES:] = jnp.ones((seq, LANES), BF16)

    qi = lax.broadcasted_iota(jnp.int32, (blk, blk), 0)
    kj = lax.broadcasted_iota(jnp.int32, (blk, blk), 1)
    cur_ok = kj <= qi
    prev_ok = kj >= qi
    band_ok = jnp.concatenate([prev_ok, cur_ok], axis=1)
    in_cur = lax.broadcasted_iota(jnp.int32, (blk, 2 * blk), 1) >= blk

    for g, (window, dil) in enumerate(BRANCHES):
        n_blk = seq // (dil * blk)
        col0 = 0 if n_blk > 1 else LANES

        def place(idx, dil=dil, n_blk=n_blk):
            res = idx // n_blk
            n = idx - res * n_blk
            start = res + n * (dil * blk)
            nat = pl.ds(start, blk, stride=dil) if dil > 1 else pl.ds(pl.multiple_of(start, blk), blk)
            cur = pl.ds(pl.multiple_of(idx * blk, blk), blk)
            kcur = pl.ds(pl.multiple_of((idx + 1) * blk, blk), blk)
            kwin = pl.ds(pl.multiple_of(idx * blk, blk), 2 * blk)
            return nat, cur, kcur, kwin, n

        keep_f32 = dil == BRANCHES[1][1]
        two_level = g == 2 and dil == BRANCHES[1][1] ** 2

        def gather(idx, c, place=place, dil=dil, n_blk=n_blk, keep_f32=keep_f32, two_level=two_level):
            nat, cur, kcur, _, n = place(idx)
            if two_level:
                mid = BRANCHES[1][1]
                res = idx // n_blk
                start = (res % mid) * (seq // mid) + res // mid + n * (mid * blk)
                src = pl.ds(start, blk, stride=mid)
                q, k, v = qf_scr[src, :], kf_scr[src, :], vf_scr[src, :]
            else:
                q, k, v = qs_scr[nat, :], ko_ref[nat, :], vo_ref[nat, :]
            if keep_f32:
                qf_scr[cur, :], kf_scr[cur, :], vf_scr[cur, :] = q, k, v
            qd_scr[cur, :] = q.astype(BF16)
            kd_scr[kcur, :] = k.astype(BF16)
            vd_scr[kcur, 0:LANES] = v.astype(BF16)
            return c

        lax.fori_loop(0, n_all, gather, 0, unroll=4)

        def scores(idx, c, place=place, n_blk=n_blk):
            _, cur, kcur, kwin, n = place(idx)
            q = qd_scr[cur, :]
            if n_blk > 1:
                ok = jnp.logical_and(band_ok, jnp.logical_or(in_cur, n > 0))
                s_scr[idx] = jnp.where(ok, _dot_nt(q, kd_scr[kwin, :]), NEG_INF)
            else:
                s_scr[idx, :, LANES:] = jnp.where(cur_ok, _dot_nt(q, kd_scr[kcur, :]), NEG_INF)
            return c

        lax.fori_loop(0, n_all, scores, 0, unroll=8)

        def softmax(idx, c, col0=col0):
            s = s_scr[idx, :, col0:]
            m = jnp.max(s, axis=-1, keepdims=True)
            p_scr[idx, :, col0:] = jnp.exp(s - m).astype(BF16)
            m_scr[idx] = jnp.broadcast_to(m, (blk, LANES))
            return c

        lax.fori_loop(0, n_all, softmax, 0, unroll=4)

        def values(idx, c, g=g, place=place, n_blk=n_blk):
            nat, _, kcur, kwin, _ = place(idx)
            if n_blk > 1:
                ov = _dot(p_scr[idx], vd_scr[kwin, :])
            else:
                ov = _dot(p_scr[idx, :, LANES:], vd_scr[kcur, :])
            l = ov[:, LANES:]
            o_scr[g, nat, :] = ov[:, 0:LANES] / l
            l_scr[g, nat, :] = m_scr[idx] + jnp.log(l)
            return c

        lax.fori_loop(0, n_all, values, 0, unroll=8)

    def comb(r, c):
        sl = pl.ds(pl.multiple_of(r * rows, rows), rows)
        out = _combine([o_scr[g, sl, :] for g in range(3)], [l_scr[g, sl, :] for g in range(3)])
        att_ref[sl, :] = out.astype(att_ref.dtype)
        return c

    lax.fori_loop(0, seq // rows, comb, 0)


def attn_prompt(z, qn, kn, *, n_heads):
    bsz, seq, _ = z.shape
    assert seq % (BRANCHES[-1][1] * ATT_BLOCK) == 0
    hd = LANES
    kern = functools.partial(_attn_prompt_kernel, seq=seq, scale=hd ** -0.5)
    blk = lambda off: pl.BlockSpec((None, seq, hd), lambda b, h: (b, 0, off + h))
    return pl.pallas_call(
        kern,
        grid=(bsz, n_heads),
        in_specs=[blk(0), blk(n_heads), blk(2 * n_heads),
                  pl.BlockSpec((1, hd), lambda b, h: (0, 0)), pl.BlockSpec((1, hd), lambda b, h: (0, 0))],
        out_specs=[blk(0), blk(0), blk(0)],
        out_shape=[
            jax.ShapeDtypeStruct((bsz, seq, n_heads * hd), BF16),
            jax.ShapeDtypeStruct((bsz, seq, n_heads * hd), F32),
            jax.ShapeDtypeStruct((bsz, seq, n_heads * hd), F32),
        ],
        scratch_shapes=[
            pltpu.VMEM((seq, hd), F32),
            pltpu.VMEM((seq, hd), F32), pltpu.VMEM((seq, hd), F32), pltpu.VMEM((seq, hd), F32),
            pltpu.VMEM((seq, hd), BF16), pltpu.VMEM((seq + ATT_BLOCK, hd), BF16),
            pltpu.VMEM((seq + ATT_BLOCK, 2 * hd), BF16),
            pltpu.VMEM((seq // ATT_BLOCK, ATT_BLOCK, 2 * ATT_BLOCK), F32),
            pltpu.VMEM((seq // ATT_BLOCK, ATT_BLOCK, 2 * ATT_BLOCK), BF16),
            pltpu.VMEM((seq // ATT_BLOCK, ATT_BLOCK, hd), F32),
            pltpu.VMEM((3, seq, hd), F32),
            pltpu.VMEM((3, seq, hd), F32),
        ],
        compiler_params=_params(("arbitrary", "arbitrary"), 40),
        name="attn_prompt",
    )(z, z, z, qn.reshape(1, hd), kn.reshape(1, hd))


SAMPLE_PAD = 16


def _attn_sample_kernel(q_ref, k_ref, v_ref, ck_ref, cv_ref, qn_ref, kn_ref, att_ref, ko_ref, vo_ref,
                        q_scr, kn_scr, vn_scr, s_scr, v_scr, o_scr, *, s_new, n_buf, n_heads, pc, scale):
    c = pl.program_id(1)
    n_ch = n_buf // pc
    pad = SAMPLE_PAD

    @pl.when(c == 0)
    def _():
        q_scr[...] = jnp.zeros_like(q_scr)
        kn_scr[...] = jnp.zeros_like(kn_scr)
        vn_scr[...] = jnp.zeros_like(vn_scr)
        for h in range(n_heads):
            lanes = slice(h * LANES, (h + 1) * LANES)
            k_new = _head_rms(k_ref[:, lanes].astype(F32), kn_ref[...])
            v_new = v_ref[:, lanes].astype(F32)
            ko_ref[:, lanes] = k_new
            vo_ref[:, lanes] = v_new
            q_scr[h, 0:s_new, :] = _head_rms(q_ref[:, lanes].astype(F32), qn_ref[...]) * scale
            kn_scr[h, 0:s_new, :] = k_new
            vn_scr[h, 0:s_new, :] = v_new

    def per_head(h, carry):
        rows = pl.ds(h, pc, stride=n_heads)
        s_scr[h, c] = _dot_nt(q_scr[h].astype(BF16), ck_ref[rows, :].astype(BF16))
        v_scr[h, pl.ds(pl.multiple_of(c * pc, pc), pc), :] = cv_ref[rows, :].astype(BF16)
        return carry

    lax.fori_loop(0, n_heads, per_head, 0, unroll=4)

    @pl.when(c == n_ch - 1)
    def _():
        qi = lax.broadcasted_iota(jnp.int32, (pad, pc), 0)
        kj = lax.broadcasted_iota(jnp.int32, (pad, pc), 1)
        qi_n = lax.broadcasted_iota(jnp.int32, (pad, pad), 0)
        kj_n = lax.broadcasted_iota(jnp.int32, (pad, pad), 1)
        dist_n = qi_n - kj_n
        new_ok = jnp.logical_and(dist_n >= 0, kj_n < s_new)

        def finish(h, carry):
            q = q_scr[h].astype(BF16)
            s_n = _dot_nt(q, kn_scr[h].astype(BF16))
            s_c = [s_scr[h, cc] for cc in range(n_ch)]
            ps, pns, ls, ms = [], [], [], []
            for window, dil in BRANCHES:
                msk = []
                for cc in range(n_ch):
                    dist = n_buf + qi - (cc * pc + kj)
                    ok = jnp.logical_and((dist & (dil - 1)) == 0, dist <= window)
                    msk.append(jnp.where(ok, s_c[cc], NEG_INF))
                m_n = jnp.where(jnp.logical_and(new_ok, (dist_n & (dil - 1)) == 0), s_n, NEG_INF)
                m = jnp.max(m_n, axis=-1, keepdims=True)
                for cc in range(n_ch):
                    m = jnp.maximum(m, jnp.max(msk[cc], axis=-1, keepdims=True))
                p_n = jnp.exp(m_n - m)
                l = jnp.sum(p_n, axis=-1, keepdims=True)
                pb = []
                for cc in range(n_ch):
                    p = jnp.exp(msk[cc] - m)
                    l = l + jnp.sum(p, axis=-1, keepdims=True)
                    pb.append(p.astype(BF16))
                ps.append(pb)
                pns.append(p_n.astype(BF16))
                ls.append(l)
                ms.append(m)
            ov = _dot(jnp.concatenate(pns, axis=0), vn_scr[h].astype(BF16))
            for cc in range(n_ch):
                ov = ov + _dot(jnp.concatenate([ps[g][cc] for g in range(3)], axis=0),
                               v_scr[h, cc * pc:(cc + 1) * pc, :])
            outs = [ov[g * pad:(g + 1) * pad] / ls[g] for g in range(3)]
            lses = [jnp.broadcast_to(ms[g] + jnp.log(ls[g]), (pad, LANES)) for g in range(3)]
            o_scr[h] = _combine(outs, lses)
            return carry

        lax.fori_loop(0, n_heads, finish, 0, unroll=2)
        for h in range(n_heads):
            att_ref[:, h * LANES:(h + 1) * LANES] = o_scr[h, 0:s_new, :].astype(att_ref.dtype)


def attn_sample(z, cache_k, cache_v, row0, qn, kn, *, n_heads):
    bsz, s_new, _ = z.shape
    hd = LANES
    d_c = n_heads * hd
    n_buf = cache_k.shape[1] // n_heads
    assert n_buf >= BRANCHES[-1][0] and s_new <= SAMPLE_PAD
    pc = min(n_buf, 512)
    kern = functools.partial(_attn_sample_kernel, s_new=s_new, n_buf=n_buf, n_heads=n_heads, pc=pc,
                             scale=hd ** -0.5)
    blk = lambda off: pl.BlockSpec((None, s_new, d_c), lambda b, c: (b, 0, off))
    cblk = pl.BlockSpec((None, pc * n_heads, hd), lambda b, c: (row0 + b, c, 0))
    vec = pl.BlockSpec((1, hd), lambda b, c: (0, 0))
    return pl.pallas_call(
        kern,
        grid=(bsz, n_buf // pc),
        in_specs=[blk(0), blk(1), blk(2), cblk, cblk, vec, vec],
        out_specs=[blk(0), blk(0), blk(0)],
        out_shape=[
            jax.ShapeDtypeStruct((bsz, s_new, d_c), BF16),
            jax.ShapeDtypeStruct((bsz, s_new, d_c), F32),
            jax.ShapeDtypeStruct((bsz, s_new, d_c), F32),
        ],
        scratch_shapes=[
            pltpu.VMEM((n_heads, SAMPLE_PAD, hd), F32),
            pltpu.VMEM((n_heads, SAMPLE_PAD, hd), F32),
            pltpu.VMEM((n_heads, SAMPLE_PAD, hd), F32),
            pltpu.VMEM((n_heads, n_buf // pc, SAMPLE_PAD, pc), F32),
            pltpu.VMEM((n_heads, n_buf, hd), BF16),
            pltpu.VMEM((n_heads, SAMPLE_PAD, hd), F32),
        ],
        compiler_params=_params(("arbitrary", "arbitrary"), 40),
        name="attn_sample",
    )(z, z, z, cache_k, cache_v, qn.reshape(1, hd), kn.reshape(1, hd))


def _sgu_kernel(gu_ref, gv_ref, lg_ref, lb_ref, w_ref, bt_ref, o_ref, vn_ref, vb_scr, *, rows, n_g, cd):
    t = w_ref.shape[1]
    gv = _gelu(gv_ref[...].astype(F32))
    mu = jnp.mean(gv, axis=-1, keepdims=True)
    xc = gv - mu
    var = jnp.mean(xc * xc, axis=-1, keepdims=True)
    vn = (xc * lax.rsqrt(var + LN_EPS)) * lg_ref[...] + lb_ref[...]
    vn_ref[...] = vn
    if rows < t:
        vb_scr[...] = jnp.zeros_like(vb_scr)
    vb_scr[0:rows, :] = vn.astype(BF16)
    ri = lax.broadcasted_iota(jnp.int32, (t, t), 0)
    ci = lax.broadcasted_iota(jnp.int32, (t, t), 1)
    for g in range(n_g):
        cols = slice(g * cd, (g + 1) * cd)
        wg = jnp.where(ri >= ci, w_ref[g], 0.0).astype(BF16)
        mixed = _dot(wg, vb_scr[:, cols])[0:rows, :] + bt_ref[:, g:g + 1]
        o_ref[:, cols] = (_gelu(gu_ref[:, cols].astype(F32)) * mixed).astype(o_ref.dtype)


def sgu(z, colblk_u, ln_g, ln_b, w_s, b_s):
    bsz, seq, _ = z.shape
    n_g = w_s.shape[0]
    dd = ln_g.shape[0]
    cd = dd // n_g
    t = min(seq, CHUNK)
    tp = max(t, LANES)
    w = jnp.pad(w_s[:, :t, :t], ((0, 0), (0, tp - t), (0, tp - t)))
    bt = jnp.transpose(b_s[:, :t])
    kern = functools.partial(_sgu_kernel, rows=t, n_g=n_g, cd=cd)
    return pl.pallas_call(
        kern,
        grid=(bsz, seq // t),
        in_specs=[
            pl.BlockSpec((None, t, dd), lambda b, c: (b, c, colblk_u)),
            pl.BlockSpec((None, t, dd), lambda b, c: (b, c, colblk_u + 1)),
            pl.BlockSpec((1, dd), lambda b, c: (0, 0)),
            pl.BlockSpec((1, dd), lambda b, c: (0, 0)),
            pl.BlockSpec((n_g, tp, tp), lambda b, c: (0, 0, 0)),
            pl.BlockSpec((t, n_g), lambda b, c: (0, 0)),
        ],
        out_specs=[
            pl.BlockSpec((None, t, dd), lambda b, c: (b, c, 0)),
            pl.BlockSpec((None, t, dd), lambda b, c: (b, c, 0)),
        ],
        out_shape=[
            jax.ShapeDtypeStruct((bsz, seq, dd), BF16),
            jax.ShapeDtypeStruct((bsz, seq, dd), F32),
        ],
        scratch_shapes=[pltpu.VMEM((tp, dd), BF16)],
        compiler_params=_params(("arbitrary", "arbitrary"), 40),
        name="sgu",
    )(z, z, ln_g.reshape(1, dd), ln_b.reshape(1, dd), w, bt)


def _in_proj(x, norm_g, w_in, i, tiles):
    if tiles["cast"]:
        return norm_matmul_cast(x, norm_g, w_in, i, tn=MIX_TILE)
    return norm_matmul(x, norm_g, w_in, tm=tiles["tm"], out_dtype=tiles["z_dtype"]), None


def _res_proj(x, a, b, w_out, i, tiles):
    if tiles["cast"]:
        return out_proj_cast(x, a, b, w_out, i, tn=MIX_TILE)
    return out_proj(x, a, b, w_out, tm=tiles["tm_out"]), None


def _even_layer(x, bsz, seq, h0_re, h0_im, pool_buf, start_pos, norm_g, w_in, w_out, i, s5p, pool_w, pool_scale,
                w_glu, b_glu, d_skip, tiles):
    t, d = x.shape
    pw_re, pw_im, b_blk, c_blk = s5p
    n_slab = b_blk.shape[0]
    d_a = n_slab * LANES
    z, w_in_b = _in_proj(x, norm_g, w_in, i, tiles)
    z = z.reshape(bsz, seq, -1)
    h0 = jnp.concatenate([h0_re.reshape(bsz, n_slab, 1, SLAB_STATE), h0_im.reshape(bsz, n_slab, 1, SLAB_STATE)], axis=-1)
    if tiles["s5_long"]:
        y_pre, h_last = s5_scan_long(z, 0, b_blk, c_blk, pw_re, pw_im, d_skip, h0)
    else:
        y_pre, h_last = s5_scan(z, 0, b_blk, c_blk, pw_re, pw_im, d_skip, h0, split_in=True)
    ya = glu(y_pre.reshape(t, d_a), w_glu, b_glu, i, tm=tiles["tm_glu"], tn=MIX_TILE)
    buf16 = jnp.pad(pool_buf, ((0, 0), (POOL_HIST - pool_buf.shape[1], 0), (0, 0)))
    yb, tail = pool(z, 1, buf16, pool_w, i, pool_scale, start_pos=start_pos, tc=256)
    x, w_out_b = _res_proj(x, ya, yb.reshape(t, -1), w_out, i, tiles)
    g_a = n_slab * SLAB_GROUPS
    h_re = h_last[..., :SLAB_STATE].reshape(bsz, g_a, S5_P)
    h_im = h_last[..., SLAB_STATE:].reshape(bsz, g_a, S5_P)
    return x, h_re, h_im, tail[:, POOL_HIST - pool_buf.shape[1]:], (w_in_b, w_out_b)


def _odd_layer(x, bsz, seq, k_buf, v_buf, norm_g, w_in, w_out, i, qn, kn, ln_g, ln_b, w_s, b_s, n_heads, tiles):
    t, d = x.shape
    d_c = n_heads * LANES
    z, w_in_b = _in_proj(x, norm_g, w_in, i, tiles)
    z = z.reshape(bsz, seq, -1)
    if k_buf is None:
        att, k_new, v_new = attn_prompt(z, qn, kn, n_heads=n_heads)
    else:
        att, k_new, v_new = attn_sample(z, k_buf, v_buf, i * bsz, qn, kn, n_heads=n_heads)
    dd = ln_g.shape[0]
    sg, vn = sgu(z, (3 * d_c) // dd, ln_g, ln_b, w_s, b_s)
    x, w_out_b = _res_proj(x, att.reshape(t, d_c), sg.reshape(t, dd), w_out, i, tiles)
    hd = LANES
    return x, k_new.reshape(bsz, seq, n_heads, hd), v_new.reshape(bsz, seq, n_heads, hd), vn, (w_in_b, w_out_b)


def kernel(x_prompt, x_sample, state_s5_re, state_s5_im, state_pool, cache_k, cache_v, norm_mix, norm_ffn, ev_w_in, ev_w_out, s5_lambda_re, s5_lambda_im, s5_log_dt, s5_b_re, s5_b_im, s5_c_re, s5_c_im, s5_d, s5_w_glu, s5_b_glu, pool_w, pool_scale, od_w_in, od_w_out, q_norm, k_norm, sgu_ln_g, sgu_ln_b, sgu_w, sgu_b, ffn_w1, ffn_w3, ffn_w2):
    bp, lp, d = x_prompt.shape
    bs, ls, _ = x_sample.shape
    depth = norm_mix.shape[0]
    n_heads = cache_k.shape[3]
    xp = x_prompt.reshape(bp * lp, d)
    xs = x_sample.reshape(bs * ls, d)
    tiles_p = dict(cast=False, tm=1024, tm_glu=512, tm_out=1024, s5_long=True, z_dtype=BF16)
    tiles_s = dict(cast=True, tm_glu=bs * ls, s5_long=False)
    g_a, p_a = s5_lambda_re.shape[1:]
    w_glu_b, pool_w_b = s5_w_glu.astype(BF16), pool_w.astype(BF16)
    d_c = n_heads * LANES
    cache_k2 = cache_k.reshape(-1, cache_k.shape[2] * n_heads, LANES)
    cache_v2 = cache_v.reshape(-1, cache_v.shape[2] * n_heads, LANES)

    s5r_p, s5i_p, pool_p, k_p, v_p = [], [], [], [], []
    s5r_s, s5i_s, pool_s, k_s, v_s, sgu_s = [], [], [], [], [], []
    for l in range(depth):
        i = l // 2
        if l % 2 == 0:
            pw_re, pw_im, bb_re, bb_im = s5_prep(s5_lambda_re[i], s5_lambda_im[i], s5_log_dt[i], s5_b_re[i], s5_b_im[i])
            b_blk, c_blk = s5_block_matrices(bb_re, bb_im, s5_c_re[i], s5_c_im[i])
            s5p = (pw_re, pw_im, b_blk, c_blk)
            rest = (i, s5p, pool_w_b, pool_scale[i], w_glu_b, s5_b_glu[i], s5_d[i])
            xs, hr, hi, buf, (w_in_b, w_out_b) = _even_layer(
                xs, bs, ls, state_s5_re[i], state_s5_im[i], state_pool[i], PAST_LEN, norm_mix[l], ev_w_in, ev_w_out,
                *rest, tiles_s)
            s5r_s.append(hr); s5i_s.append(hi); pool_s.append(buf)
            zero_h = jnp.zeros((bp, g_a, p_a), F32)
            zero_buf = jnp.zeros((bp, state_pool.shape[2], state_pool.shape[3]), F32)
            xp, hr, hi, buf, _ = _even_layer(xp, bp, lp, zero_h, zero_h, zero_buf, 0, norm_mix[l], w_in_b, w_out_b,
                                             *rest, tiles_p)
            s5r_p.append(hr); s5i_p.append(hi); pool_p.append(buf)
        else:
            rest = (i, q_norm[i], k_norm[i], sgu_ln_g[i], sgu_ln_b[i], sgu_w[i], sgu_b[i], n_heads)
            xs, nk, nv, vrows, (w_in_b, w_out_b) = _odd_layer(xs, bs, ls, cache_k2, cache_v2, norm_mix[l], od_w_in,
                                                              od_w_out, *rest, tiles_s)
            k_s.append(nk); v_s.append(nv); sgu_s.append(vrows)
            xp, nk, nv, _, _ = _odd_layer(xp, bp, lp, None, None, norm_mix[l], w_in_b, w_out_b, *rest, tiles_p)
            k_p.append(nk); v_p.append(nv)
        xs, w1_b, w3_b, w2_b = ffn_cast(xs, norm_ffn[l], ffn_w1, ffn_w3, ffn_w2, l, tf=FFN_TILE, group=FFN_GROUP)
        xp = ffn(xp, norm_ffn[l], w1_b, w3_b, w2_b, tm=512)
    return (xp.reshape(bp, lp, d), xs.reshape(bs, ls, d),
            jnp.stack(s5r_p), jnp.stack(s5i_p), jnp.stack(pool_p), jnp.stack(k_p), jnp.stack(v_p),
            jnp.stack(s5r_s), jnp.stack(s5i_s), jnp.stack(pool_s), jnp.stack(k_s), jnp.stack(v_s),
            jnp.stack(sgu_s))
```

```python
import functools
import math

import jax
import jax.numpy as jnp
from jax import lax
from jax.experimental import pallas as pl
from jax.experimental.pallas import tpu as pltpu

F32 = jnp.float32
BF16 = jnp.bfloat16

RMS_EPS = 1e-6
LN_EPS = 1e-5
NEG_INF = -1e30

LANES = 128
SUBLANES = 8
ATT_BLOCK = 128
CHUNK = 128
POOL_WINDOWS = (2, 4, 8, 16)
POOL_HIST = 16
BRANCHES = ((128, 1), (512, 4), (2048, 16))
S5_GRP = 16
S5_P = 64
SLAB_GROUPS = LANES // S5_GRP
SLAB_STATE = SLAB_GROUPS * S5_P
PAST_LEN = 8192
FFN_TILE = 256
MIX_TILE = 512


def _params(sem, vmem_mib):
    return pltpu.CompilerParams(dimension_semantics=sem, vmem_limit_bytes=vmem_mib << 20)


def _gelu(x):
    return 0.5 * x * (1.0 + lax.erf(x * (1.0 / math.sqrt(2.0))))


def _sigmoid(x):
    return 1.0 / (1.0 + jnp.exp(-x))


def _split_bf16(a):
    hi = a.astype(BF16)
    lo = (a - hi.astype(F32)).astype(BF16)
    return hi, lo


def _dot(a, b):
    return jnp.dot(a, b, preferred_element_type=F32)


def _dot_nt(a, b):
    return lax.dot_general(a, b, (((1,), (1,)), ((), ())), preferred_element_type=F32)


def _rms_rows_to(x_ref, g_ref, h_ref, rows):
    step = 16 if rows % 16 == 0 else rows

    def body(r, c):
        sl = pl.ds(pl.multiple_of(r * step, step), step)
        x = x_ref[sl, :]
        ms = jnp.mean(x * x, axis=-1, keepdims=True)
        h_ref[sl, :] = ((x * lax.rsqrt(ms + RMS_EPS)) * g_ref[...]).astype(h_ref.dtype)
        return c

    lax.fori_loop(0, rows // step, body, 0, unroll=min(4, rows // step))


def _norm_matmul_kernel(x_ref, g_ref, w_ref, o_ref, h_ref):
    @pl.when(pl.program_id(1) == 0)
    def _():
        _rms_rows_to(x_ref, g_ref, h_ref, x_ref.shape[0])

    o_ref[...] = _dot(h_ref[...], w_ref[...]).astype(o_ref.dtype)


def norm_matmul(x, g, w, *, tm, out_dtype=F32):
    t, d = x.shape
    n_t, _, tn = w.shape
    return pl.pallas_call(
        _norm_matmul_kernel,
        grid=(t // tm, n_t),
        in_specs=[
            pl.BlockSpec((tm, d), lambda i, j: (i, 0)),
            pl.BlockSpec((1, d), lambda i, j: (0, 0)),
            pl.BlockSpec((None, d, tn), lambda i, j: (j, 0, 0)),
        ],
        out_specs=pl.BlockSpec((tm, tn), lambda i, j: (i, j)),
        out_shape=jax.ShapeDtypeStruct((t, n_t * tn), out_dtype),
        scratch_shapes=[pltpu.VMEM((tm, d), BF16)],
        compiler_params=_params(("arbitrary", "arbitrary"), 58),
        name="norm_matmul",
    )(x, g.reshape(1, d), w)


def _norm_matmul_cast_kernel(x_ref, g_ref, w_ref, o_ref, wb_ref, h_ref):
    @pl.when(pl.program_id(0) == 0)
    def _():
        _rms_rows_to(x_ref, g_ref, h_ref, x_ref.shape[0])

    wb_ref[...] = w_ref[...].astype(BF16)
    o_ref[...] = _dot(h_ref[...], wb_ref[...]).astype(o_ref.dtype)


def norm_matmul_cast(x, g, w, layer, *, tn):
    t, d = x.shape
    n = w.shape[2]
    return pl.pallas_call(
        _norm_matmul_cast_kernel,
        grid=(n // tn,),
        in_specs=[
            pl.BlockSpec((t, d), lambda j: (0, 0)),
            pl.BlockSpec((1, d), lambda j: (0, 0)),
            pl.BlockSpec((None, d, tn), lambda j: (layer, 0, j)),
        ],
        out_specs=[
            pl.BlockSpec((t, tn), lambda j: (0, j)),
            pl.BlockSpec((None, d, tn), lambda j: (j, 0, 0)),
        ],
        out_shape=[
            jax.ShapeDtypeStruct((t, n), F32),
            jax.ShapeDtypeStruct((n // tn, d, tn), BF16),
        ],
        scratch_shapes=[pltpu.VMEM((t, d), BF16)],
        compiler_params=_params(("arbitrary",), 48),
        name="norm_matmul_cast",
    )(x, g.reshape(1, d), w)


FFN_OUT_CHUNK = 512


def _ffn_step(first, x_ref, g_ref, w1_ref, w3_ref, w2_ref, o_ref, h_ref):
    @pl.when(first)
    def _():
        _rms_rows_to(x_ref, g_ref, h_ref, x_ref.shape[0])
        o_ref[...] = x_ref[...]

    h = h_ref[...]
    a = _dot(h, w1_ref[...])
    b = _dot(h, w3_ref[...])
    u = ((a * _sigmoid(a)) * b).astype(BF16)
    for c in range(0, o_ref.shape[1], FFN_OUT_CHUNK):
        o_ref[:, c:c + FFN_OUT_CHUNK] += _dot(u, w2_ref[:, c:c + FFN_OUT_CHUNK])


def _ffn_kernel(x_ref, g_ref, w1_ref, w3_ref, w2_ref, o_ref, h_ref):
    _ffn_step(pl.program_id(1) == 0, x_ref, g_ref, w1_ref, w3_ref, w2_ref, o_ref, h_ref)


def _ffn_next_kernel(x_ref, g_ref, w1_ref, w3_ref, w2_ref, n1_ref, n3_ref, n2_ref,
                     o_ref, c1_ref, c3_ref, c2_ref, h_ref):
    c1_ref[...] = n1_ref[...].astype(BF16)
    c3_ref[...] = n3_ref[...].astype(BF16)
    c2_ref[...] = n2_ref[...].astype(BF16)
    _ffn_step(pl.program_id(1) == 0, x_ref, g_ref, w1_ref, w3_ref, w2_ref, o_ref, h_ref)


def ffn(x, g, w1, w3, w2, *, tm, nxt=None):
    t, d = x.shape
    n_f, _, tf = w1.shape
    n_m = t // tm
    specs = [
        pl.BlockSpec((tm, d), lambda i, j: (i, 0)),
        pl.BlockSpec((1, d), lambda i, j: (0, 0)),
        pl.BlockSpec((None, d, tf), lambda i, j: (j, 0, 0)),
        pl.BlockSpec((None, d, tf), lambda i, j: (j, 0, 0)),
        pl.BlockSpec((tf, d), lambda i, j: (j, 0)),
    ]
    out_spec = pl.BlockSpec((tm, d), lambda i, j: (i, 0))
    out_shape = jax.ShapeDtypeStruct((t, d), F32)
    common = dict(grid=(n_m, n_f), scratch_shapes=[pltpu.VMEM((tm, d), BF16)],
                  compiler_params=_params(("arbitrary", "arbitrary"), 56))
    if nxt is None:
        return pl.pallas_call(_ffn_kernel, in_specs=specs, out_specs=out_spec, out_shape=out_shape, name="ffn",
                              **common)(x, g.reshape(1, d), w1, w3, w2)
    n1, n3, n2, layer = nxt
    piece = d // n_m
    assert d % n_m == 0 and piece % LANES == 0
    col = pl.BlockSpec((None, piece, tf), lambda i, j: (layer, i, j))
    row = pl.BlockSpec((None, tf, piece), lambda i, j: (layer, j, i))
    col_out = pl.BlockSpec((None, piece, tf), lambda i, j: (j, i, 0))
    row_out = pl.BlockSpec((tf, piece), lambda i, j: (j, i))
    out, c1, c3, c2 = pl.pallas_call(
        _ffn_next_kernel,
        in_specs=specs + [col, col, row],
        out_specs=[out_spec, col_out, col_out, row_out],
        out_shape=[out_shape, jax.ShapeDtypeStruct(w1.shape, BF16), jax.ShapeDtypeStruct(w3.shape, BF16),
                   jax.ShapeDtypeStruct(w2.shape, BF16)],
        name="ffn_next", **common,
    )(x, g.reshape(1, d), w1, w3, w2, n1, n3, n2)
    return out, (c1, c3, c2)


def _ffn_cast_kernel(x_ref, g_ref, w1_ref, w3_ref, w2_ref, o_ref, w1b_ref, w3b_ref, w2b_ref, h_ref):
    w1b_ref[...] = w1_ref[...].astype(BF16)
    w3b_ref[...] = w3_ref[...].astype(BF16)
    w2b_ref[...] = w2_ref[...].astype(BF16)
    _ffn_step(pl.program_id(0) == 0, x_ref, g_ref, w1b_ref, w3b_ref, w2b_ref, o_ref, h_ref)


def ffn_cast(x, g, w1, w3, w2, layer, *, tf):
    t, d = x.shape
    f = w1.shape[2]
    return pl.pallas_call(
        _ffn_cast_kernel,
        grid=(f // tf,),
        in_specs=[
            pl.BlockSpec((t, d), lambda j: (0, 0)),
            pl.BlockSpec((1, d), lambda j: (0, 0)),
            pl.BlockSpec((None, d, tf), lambda j: (layer, 0, j)),
            pl.BlockSpec((None, d, tf), lambda j: (layer, 0, j)),
            pl.BlockSpec((None, tf, d), lambda j: (layer, j, 0)),
        ],
        out_specs=[
            pl.BlockSpec((t, d), lambda j: (0, 0)),
            pl.BlockSpec((None, d, tf), lambda j: (j, 0, 0)),
            pl.BlockSpec((None, d, tf), lambda j: (j, 0, 0)),
            pl.BlockSpec((tf, d), lambda j: (j, 0)),
        ],
        out_shape=[
            jax.ShapeDtypeStruct((t, d), F32),
            jax.ShapeDtypeStruct((f // tf, d, tf), BF16),
            jax.ShapeDtypeStruct((f // tf, d, tf), BF16),
            jax.ShapeDtypeStruct((f, d), BF16),
        ],
        scratch_shapes=[pltpu.VMEM((t, d), BF16)],
        compiler_params=_params(("arbitrary",), 48),
        name="ffn_cast",
    )(x, g.reshape(1, d), w1, w3, w2)


def _out_proj_kernel(x_ref, a_ref, b_ref, wa_ref, wb_ref, o_ref):
    o_ref[...] = x_ref[...] + _dot(a_ref[...], wa_ref[...]) + _dot(b_ref[...], wb_ref[...])


def out_proj(x, a, b, w_pair, *, tm):
    t, d = x.shape
    wa, wb = w_pair
    n_t, k, tn = wa.shape
    assert a.shape[1] == k and b.shape[1] == k and wb.shape == wa.shape
    return pl.pallas_call(
        _out_proj_kernel,
        grid=(t // tm, n_t),
        in_specs=[
            pl.BlockSpec((tm, tn), lambda i, j: (i, j)),
            pl.BlockSpec((tm, k), lambda i, j: (i, 0)),
            pl.BlockSpec((tm, k), lambda i, j: (i, 0)),
            pl.BlockSpec((None, k, tn), lambda i, j: (j, 0, 0)),
            pl.BlockSpec((None, k, tn), lambda i, j: (j, 0, 0)),
        ],
        out_specs=pl.BlockSpec((tm, tn), lambda i, j: (i, j)),
        out_shape=jax.ShapeDtypeStruct((t, d), F32),
        compiler_params=_params(("arbitrary", "arbitrary"), 48),
        name="out_proj",
    )(x, a, b, wa, wb)


def _out_proj_cast_kernel(x_ref, a_ref, b_ref, wa_ref, wb_ref, o_ref, wab_ref, wbb_ref):
    wab_ref[...] = wa_ref[...].astype(BF16)
    wbb_ref[...] = wb_ref[...].astype(BF16)
    o_ref[...] = x_ref[...] + _dot(a_ref[...], wab_ref[...]) + _dot(b_ref[...], wbb_ref[...])


def out_proj_cast(x, a, b, w, layer, *, tn):
    t, d = x.shape
    k = a.shape[1]
    assert b.shape[1] == k and w.shape[1] == 2 * k
    out, wa_b, wb_b = pl.pallas_call(
        _out_proj_cast_kernel,
        grid=(d // tn,),
        in_specs=[
            pl.BlockSpec((t, tn), lambda j: (0, j)),
            pl.BlockSpec((t, k), lambda j: (0, 0)),
            pl.BlockSpec((t, k), lambda j: (0, 0)),
            pl.BlockSpec((None, k, tn), lambda j: (layer, 0, j)),
            pl.BlockSpec((None, k, tn), lambda j: (layer, 1, j)),
        ],
        out_specs=[
            pl.BlockSpec((t, tn), lambda j: (0, j)),
            pl.BlockSpec((None, k, tn), lambda j: (j, 0, 0)),
            pl.BlockSpec((None, k, tn), lambda j: (j, 0, 0)),
        ],
        out_shape=[
            jax.ShapeDtypeStruct((t, d), F32),
            jax.ShapeDtypeStruct((d // tn, k, tn), BF16),
            jax.ShapeDtypeStruct((d // tn, k, tn), BF16),
        ],
        compiler_params=_params(("arbitrary",), 48),
        name="out_proj_cast",
    )(x, a, b, w, w)
    return out, (wa_b, wb_b)


def _s5_prep_kernel(lr_ref, li_ref, ldt_ref, lrx_ref, lix_ref, ldtx_ref, br_ref, bi_ref,
                    pwr_ref, pwi_ref, bbr_ref, bbi_ref):
    dt = jnp.exp(ldt_ref[...])
    mag = jnp.exp(lr_ref[...] * dt)
    ang = li_ref[...] * dt
    p_r, p_i = mag * jnp.cos(ang), mag * jnp.sin(ang)
    c_r, c_i = p_r, p_i
    pwr_ref[0], pwi_ref[0] = c_r, c_i
    for j in range(1, SUBLANES):
        c_r, c_i = c_r * p_r - c_i * p_i, c_r * p_i + c_i * p_r
        pwr_ref[j], pwi_ref[j] = c_r, c_i
    lr, li = lrx_ref[...], lix_ref[...]
    dtx = jnp.exp(ldtx_ref[...])
    magx = jnp.exp(lr * dtx)
    angx = li * dtx
    nr, ni = magx * jnp.cos(angx) - 1.0, magx * jnp.sin(angx)
    den = lr * lr + li * li
    qr = (nr * lr + ni * li) / den
    qi = (ni * lr - nr * li) / den
    br, bi = br_ref[...], bi_ref[...]
    bbr_ref[...] = qr * br - qi * bi
    bbi_ref[...] = qr * bi + qi * br


def s5_prep(lam_re, lam_im, log_dt, b_re, b_im):
    g, p = lam_re.shape
    h = b_re.shape[2]
    n_slab = g // SLAB_GROUPS
    slab = lambda a: a.reshape(n_slab, SLAB_GROUPS * p)
    rep = lambda a: jnp.repeat(a, h, axis=1)
    ldt_gp = jnp.broadcast_to(log_dt[:, None], (g, p))
    outs = pl.pallas_call(
        _s5_prep_kernel,
        out_shape=[
            jax.ShapeDtypeStruct((SUBLANES, n_slab, SLAB_GROUPS * p), F32),
            jax.ShapeDtypeStruct((SUBLANES, n_slab, SLAB_GROUPS * p), F32),
            jax.ShapeDtypeStruct((g, p * h), F32),
            jax.ShapeDtypeStruct((g, p * h), F32),
        ],
        name="s5_prep",
    )(slab(lam_re), slab(lam_im), slab(ldt_gp), rep(lam_re), rep(lam_im), rep(ldt_gp),
      b_re.reshape(g, p * h), b_im.reshape(g, p * h))
    pw_re, pw_im, bb_re, bb_im = outs
    pw_re = jnp.transpose(pw_re, (1, 0, 2))
    pw_im = jnp.transpose(pw_im, (1, 0, 2))
    return pw_re, pw_im, bb_re.reshape(g, p, h), bb_im.reshape(g, p, h)


def s5_block_matrices(bb_re, bb_im, c_re, c_im):
    g, p, h = bb_re.shape
    n_slab = g // SLAB_GROUPS
    eye = jnp.eye(SLAB_GROUPS, dtype=F32)

    def in_map(bb):
        t = bb.reshape(n_slab, SLAB_GROUPS, p, h)
        return jnp.einsum("kgph,gj->kghjp", t, eye).reshape(n_slab, SLAB_GROUPS * h, SLAB_GROUPS * p)

    def out_map(c):
        t = c.reshape(n_slab, SLAB_GROUPS, h, p)
        return jnp.einsum("kghp,gj->kgpjh", t, eye).reshape(n_slab, SLAB_GROUPS * p, SLAB_GROUPS * h)

    b_blk = jnp.concatenate([in_map(bb_re), in_map(bb_im)], axis=2)
    c_blk = jnp.concatenate([out_map(c_re), -out_map(c_im)], axis=1)
    return b_blk, c_blk


def _s5_scan_kernel(u_ref, bblk_ref, cblk_ref, pwr_ref, pwi_ref, d_ref, h0_ref,
                    y_ref, hl_ref, h_scr, bh_scr, bl_scr, ch_scr, *, seq, row_chunk, split_in):
    ns = SLAB_STATE

    @pl.when(pl.program_id(1) == 0)
    def _():
        bh, bl = _split_bf16(bblk_ref[...])
        bh_scr[...], bl_scr[...] = bh, bl
        ch_scr[...] = cblk_ref[...].astype(BF16)

    n_chunks = seq // row_chunk

    def proj_in(r, c):
        sl = pl.ds(pl.multiple_of(r * row_chunk, row_chunk), row_chunk)
        if split_in:
            uh, ul = _split_bf16(u_ref[sl, :].astype(F32))
            h_scr[sl, :] = _dot(uh, bh_scr[...]) + _dot(ul, bh_scr[...]) + _dot(uh, bl_scr[...])
        else:
            h_scr[sl, :] = _dot(u_ref[sl, :].astype(BF16), bh_scr[...])
        return c

    lax.fori_loop(0, n_chunks, proj_in, 0)

    rowid = lax.broadcasted_iota(jnp.int32, (SUBLANES, LANES), 0)
    for c in range(ns // LANES):
        re_l = slice(c * LANES, (c + 1) * LANES)
        im_l = slice(ns + c * LANES, ns + (c + 1) * LANES)
        p_r, p_i = pwr_ref[:, re_l], pwi_ref[:, re_l]
        steps = []
        for dist in (1, 2, 4):
            a_r = jnp.where(rowid >= dist, jnp.broadcast_to(p_r[dist - 1:dist], (SUBLANES, LANES)), 0.0)
            a_i = jnp.where(rowid >= dist, jnp.broadcast_to(p_i[dist - 1:dist], (SUBLANES, LANES)), 0.0)
            steps.append((dist, a_r, a_i))
        c_r = jnp.broadcast_to(h0_ref[:, re_l], (SUBLANES, LANES))
        c_i = jnp.broadcast_to(h0_ref[:, im_l], (SUBLANES, LANES))

        last = slice(SUBLANES - 1, SUBLANES)
        full = (SUBLANES, LANES)
        p8_r, p8_i = jnp.broadcast_to(p_r[last], full), jnp.broadcast_to(p_i[last], full)
        n_groups = seq // SUBLANES
        per_it = min(4, n_groups)

        def scan_rows(it, carry, re_l=re_l, im_l=im_l, p_r=p_r, p_i=p_i, p8_r=p8_r, p8_i=p8_i, steps=steps):
            base = pl.multiple_of(it * (per_it * SUBLANES), per_it * SUBLANES)
            sls = [pl.ds(base + j * SUBLANES, SUBLANES) for j in range(per_it)]
            loc = []
            for sl in sls:
                r, i = h_scr[sl, re_l], h_scr[sl, im_l]
                for dist, a_r, a_i in steps:
                    s_r, s_i = pltpu.roll(r, dist, 0), pltpu.roll(i, dist, 0)
                    r, i = r + (s_r * a_r - s_i * a_i), i + (s_r * a_i + s_i * a_r)
                loc.append((r, i))
            c_r, c_i = carry
            outs = []
            for r, i in loc:
                outs.append((r + (c_r * p_r - c_i * p_i), i + (c_r * p_i + c_i * p_r)))
                e_r, e_i = jnp.broadcast_to(r[last], full), jnp.broadcast_to(i[last], full)
                c_r, c_i = e_r + (c_r * p8_r - c_i * p8_i), e_i + (c_r * p8_i + c_i * p8_r)
            for sl, (r, i) in zip(sls, outs):
                h_scr[sl, re_l], h_scr[sl, im_l] = r, i
            return c_r, c_i

        c_r, c_i = lax.fori_loop(0, n_groups // per_it, scan_rows, (c_r, c_i))
        hl_ref[:, re_l] = c_r[0:1]
        hl_ref[:, im_l] = c_i[0:1]

    def proj_out(r, c):
        sl = pl.ds(pl.multiple_of(r * row_chunk, row_chunk), row_chunk)
        y = _dot(h_scr[sl, :].astype(BF16), ch_scr[...]) + d_ref[...] * u_ref[sl, :].astype(F32)
        y_ref[sl, :] = _gelu(y)
        return c

    lax.fori_loop(0, n_chunks, proj_out, 0)


def s5_scan(z, col0, b_blk, c_blk, pw_re, pw_im, d_skip, h0, *, split_in):
    bsz, seq, _ = z.shape
    n_slab = b_blk.shape[0]
    ns2 = 2 * SLAB_STATE
    row_chunk = min(seq, 1024)
    kern = functools.partial(_s5_scan_kernel, seq=seq, row_chunk=row_chunk, split_in=split_in)
    return pl.pallas_call(
        kern,
        grid=(n_slab, bsz),
        in_specs=[
            pl.BlockSpec((None, seq, LANES), lambda k, b: (b, 0, col0 + k)),
            pl.BlockSpec((None, LANES, ns2), lambda k, b: (k, 0, 0)),
            pl.BlockSpec((None, ns2, LANES), lambda k, b: (k, 0, 0)),
            pl.BlockSpec((None, SUBLANES, SLAB_STATE), lambda k, b: (k, 0, 0)),
            pl.BlockSpec((None, SUBLANES, SLAB_STATE), lambda k, b: (k, 0, 0)),
            pl.BlockSpec((1, LANES), lambda k, b: (0, k)),
            pl.BlockSpec((None, None, 1, ns2), lambda k, b: (b, k, 0, 0)),
        ],
        out_specs=[
            pl.BlockSpec((None, seq, LANES), lambda k, b: (b, 0, k)),
            pl.BlockSpec((None, None, 1, ns2), lambda k, b: (b, k, 0, 0)),
        ],
        out_shape=[
            jax.ShapeDtypeStruct((bsz, seq, n_slab * LANES), F32),
            jax.ShapeDtypeStruct((bsz, n_slab, 1, ns2), F32),
        ],
        scratch_shapes=[
            pltpu.VMEM((seq, ns2), F32),
            pltpu.VMEM((LANES, ns2), BF16), pltpu.VMEM((LANES, ns2), BF16),
            pltpu.VMEM((ns2, LANES), BF16),
        ],
        compiler_params=_params(("arbitrary", "arbitrary"), 40),
        name="s5_scan",
    )(z, b_blk, c_blk, pw_re, pw_im, d_skip.reshape(1, -1), h0)


def _s5_seg_kernel(u_ref, bblk_ref, cblk_ref, pwr_ref, pwi_ref, d_ref, h0_ref,
                   y_ref, hl_ref, up_scr, h_scr, yp_scr, w_scr, bh_scr, ch_scr, *, seq, row_chunk):
    ns = SLAB_STATE
    nseg = SUBLANES
    seg = seq // nseg
    n_lb = ns // LANES
    full = (SUBLANES, LANES)
    lanes = [(slice(c * LANES, (c + 1) * LANES), slice(ns + c * LANES, ns + (c + 1) * LANES)) for c in range(n_lb)]
    last = slice(SUBLANES - 1, SUBLANES)

    @pl.when(pl.program_id(1) == 0)
    def _():
        bh_scr[...] = bblk_ref[...].astype(BF16)
        ch_scr[...] = cblk_ref[...].astype(BF16)
        for re_l, im_l in lanes:
            p_r, p_i = pwr_ref[:, re_l], pwi_ref[:, re_l]
            p8_r, p8_i = jnp.broadcast_to(p_r[last], full), jnp.broadcast_to(p_i[last], full)
            w_scr[0:SUBLANES, re_l], w_scr[0:SUBLANES, im_l] = p_r, p_i

            def grow(gi, carry, re_l=re_l, im_l=im_l, p8_r=p8_r, p8_i=p8_i):
                w_r, w_i = carry
                w_r, w_i = w_r * p8_r - w_i * p8_i, w_r * p8_i + w_i * p8_r
                sl = pl.ds(pl.multiple_of(gi * SUBLANES, SUBLANES), SUBLANES)
                w_scr[sl, re_l], w_scr[sl, im_l] = w_r, w_i
                return w_r, w_i

            lax.fori_loop(1, seg // SUBLANES, grow, (p_r, p_i))

    for s in range(nseg):
        up_scr[pl.ds(s, seg, stride=nseg), :] = u_ref[s * seg:(s + 1) * seg, :].astype(F32)

    def proj_in(r, c):
        sl = pl.ds(pl.multiple_of(r * row_chunk, row_chunk), row_chunk)
        h_scr[sl, :] = _dot(up_scr[sl, :].astype(BF16), bh_scr[...])
        return c

    lax.fori_loop(0, seq // row_chunk, proj_in, 0)

    lam = [(jnp.broadcast_to(pwr_ref[0:1, re_l], full), jnp.broadcast_to(pwi_ref[0:1, re_l], full))
           for re_l, _ in lanes]
    per_it = 2

    def scan_t(it, carry):
        base = pl.multiple_of(it * (per_it * SUBLANES), per_it * SUBLANES)
        sls = [pl.ds(base + j * SUBLANES, SUBLANES) for j in range(per_it)]
        bu = [[(h_scr[sl, re_l], h_scr[sl, im_l]) for re_l, im_l in lanes] for sl in sls]
        hs, outs = list(carry), []
        for j in range(per_it):
            hs = [(bu[j][c][0] + (hs[c][0] * lam[c][0] - hs[c][1] * lam[c][1]),
                   bu[j][c][1] + (hs[c][0] * lam[c][1] + hs[c][1] * lam[c][0])) for c in range(n_lb)]
            outs.append(hs)
        for sl, row in zip(sls, outs):
            for (re_l, im_l), (h_r, h_i) in zip(lanes, row):
                h_scr[sl, re_l], h_scr[sl, im_l] = h_r, h_i
        return tuple(hs)

    zero = jnp.zeros(full, F32)
    ends = lax.fori_loop(0, seg // per_it, scan_t, tuple((zero, zero) for _ in range(n_lb)))

    rowid = lax.broadcasted_iota(jnp.int32, full, 0)
    enter = []
    for (re_l, im_l), (e_r, e_i) in zip(lanes, ends):
        ws_r, ws_i = w_scr[seg - 1:seg, re_l], w_scr[seg - 1:seg, im_l]
        c_r, c_i = h0_ref[:, re_l], h0_ref[:, im_l]
        cv_r, cv_i = jnp.broadcast_to(c_r, full), jnp.broadcast_to(c_i, full)
        for s in range(1, nseg + 1):
            c_r, c_i = (e_r[s - 1:s] + (c_r * ws_r - c_i * ws_i), e_i[s - 1:s] + (c_r * ws_i + c_i * ws_r))
            if s < nseg:
                cv_r = jnp.where(rowid == s, jnp.broadcast_to(c_r, full), cv_r)
                cv_i = jnp.where(rowid == s, jnp.broadcast_to(c_i, full), cv_i)
        hl_ref[:, re_l], hl_ref[:, im_l] = c_r, c_i
        enter.append((cv_r, cv_i))

    def fix_t(gi, c):
        wsl = pl.ds(pl.multiple_of(gi * SUBLANES, SUBLANES), SUBLANES)
        wv = [(w_scr[wsl, re_l], w_scr[wsl, im_l]) for re_l, im_l in lanes]
        base = pl.multiple_of(gi * (SUBLANES * SUBLANES), SUBLANES * SUBLANES)
        for j in range(SUBLANES):
            sl = pl.ds(base + j * SUBLANES, SUBLANES)
            for (re_l, im_l), (cv_r, cv_i), (wv_r, wv_i) in zip(lanes, enter, wv):
                w_r = jnp.broadcast_to(wv_r[j:j + 1], full)
                w_i = jnp.broadcast_to(wv_i[j:j + 1], full)
                h_r = h_scr[sl, re_l] + (w_r * cv_r - w_i * cv_i)
                h_i = h_scr[sl, im_l] + (w_r * cv_i + w_i * cv_r)
                h_scr[sl, re_l], h_scr[sl, im_l] = h_r, h_i
        return c

    lax.fori_loop(0, seg // SUBLANES, fix_t, 0)

    def proj_out(r, c):
        sl = pl.ds(pl.multiple_of(r * row_chunk, row_chunk), row_chunk)
        y = _dot(h_scr[sl, :].astype(BF16), ch_scr[...]) + d_ref[...] * up_scr[sl, :]
        yp_scr[sl, :] = _gelu(y)
        return c

    lax.fori_loop(0, seq // row_chunk, proj_out, 0)

    for s in range(nseg):
        y_ref[s * seg:(s + 1) * seg, :] = yp_scr[pl.ds(s, seg, stride=nseg), :]


def s5_scan_long(z, col0, b_blk, c_blk, pw_re, pw_im, d_skip, h0):
    bsz, seq, _ = z.shape
    n_slab = b_blk.shape[0]
    ns2 = 2 * SLAB_STATE
    row_chunk = min(seq, 1024)
    assert seq % row_chunk == 0 and seq % (2 * SUBLANES * SUBLANES) == 0
    kern = functools.partial(_s5_seg_kernel, seq=seq, row_chunk=row_chunk)
    return pl.pallas_call(
        kern,
        grid=(n_slab, bsz),
        in_specs=[
            pl.BlockSpec((None, seq, LANES), lambda k, b: (b, 0, col0 + k)),
            pl.BlockSpec((None, LANES, ns2), lambda k, b: (k, 0, 0)),
            pl.BlockSpec((None, ns2, LANES), lambda k, b: (k, 0, 0)),
            pl.BlockSpec((None, SUBLANES, SLAB_STATE), lambda k, b: (k, 0, 0)),
            pl.BlockSpec((None, SUBLANES, SLAB_STATE), lambda k, b: (k, 0, 0)),
            pl.BlockSpec((1, LANES), lambda k, b: (0, k)),
            pl.BlockSpec((None, None, 1, ns2), lambda k, b: (b, k, 0, 0)),
        ],
        out_specs=[
            pl.BlockSpec((None, seq, LANES), lambda k, b: (b, 0, k)),
            pl.BlockSpec((None, None, 1, ns2), lambda k, b: (b, k, 0, 0)),
        ],
        out_shape=[
            jax.ShapeDtypeStruct((bsz, seq, n_slab * LANES), F32),
            jax.ShapeDtypeStruct((bsz, n_slab, 1, ns2), F32),
        ],
        scratch_shapes=[
            pltpu.VMEM((seq, LANES), F32),
            pltpu.VMEM((seq, ns2), F32),
            pltpu.VMEM((seq, LANES), F32),
            pltpu.VMEM((seq // SUBLANES, ns2), F32),
            pltpu.VMEM((LANES, ns2), BF16),
            pltpu.VMEM((ns2, LANES), BF16),
        ],
        compiler_params=_params(("arbitrary", "arbitrary"), 40),
        name="s5_scan_long",
    )(z, b_blk, c_blk, pw_re, pw_im, d_skip.reshape(1, -1), h0)


def _glu_kernel(yk_ref, yj_ref, w_ref, b_ref, o_ref, yb_scr):
    @pl.when(pl.program_id(1) == 0)
    def _():
        yb_scr[...] = yk_ref[...].astype(BF16)

    gate = _dot(yb_scr[...], w_ref[...]) + b_ref[...]
    o_ref[...] = (yj_ref[...] * _sigmoid(gate)).astype(o_ref.dtype)


def glu(y, w, b, layer, *, tm, tn):
    t, d = y.shape
    tm = min(tm, t)
    return pl.pallas_call(
        _glu_kernel,
        grid=(t // tm, d // tn),
        in_specs=[
            pl.BlockSpec((tm, d), lambda i, j: (i, 0)),
            pl.BlockSpec((tm, tn), lambda i, j: (i, j)),
            pl.BlockSpec((None, d, tn), lambda i, j: (layer, 0, j)),
            pl.BlockSpec((1, tn), lambda i, j: (0, j)),
        ],
        out_specs=pl.BlockSpec((tm, tn), lambda i, j: (i, j)),
        out_shape=jax.ShapeDtypeStruct((t, d), BF16),
        scratch_shapes=[pltpu.VMEM((tm, d), BF16)],
        compiler_params=_params(("arbitrary", "arbitrary"), 40),
        name="glu",
    )(y, y, w, b.reshape(1, d))


def _pool_kernel(u_ref, buf_ref, w_ref, s_ref, y_ref, tail_ref, ext_scr, *, tc, start_pos, cg):
    c = pl.program_id(1)

    @pl.when(c == 0)
    def _():
        ext_scr[0:POOL_HIST, :] = buf_ref[...]

    ext_scr[POOL_HIST:POOL_HIST + tc, :] = u_ref[...].astype(F32)
    pos = start_pos + c * tc + lax.broadcasted_iota(jnp.int32, (tc, 1), 0)
    for g, win in enumerate(POOL_WINDOWS):
        cols = slice(g * cg, (g + 1) * cg)
        x = ext_scr[:, cols]
        acc, dist = x, 1
        while dist < win:
            acc = acc + pltpu.roll(acc, dist, 0)
            dist *= 2
        wsum = acc[POOL_HIST:, :]
        cnt = jnp.minimum(pos + 1, win).astype(F32)
        zg = wsum * (1.0 / cnt) - x[POOL_HIST:, :]
        y = _dot(zg.astype(BF16), w_ref[g]) * s_ref[:, cols]
        y_ref[:, cols] = y.astype(y_ref.dtype)

    tail = ext_scr[tc:tc + POOL_HIST, :]
    ext_scr[0:POOL_HIST, :] = tail

    @pl.when(c == pl.num_programs(1) - 1)
    def _():
        tail_ref[...] = tail


def pool(z, colblk, buf16, w, layer, scale, *, start_pos, tc):
    bsz, seq, _ = z.shape
    _, n_g, cg, _ = w.shape
    db = n_g * cg
    tc = min(tc, seq)
    kern = functools.partial(_pool_kernel, tc=tc, start_pos=start_pos, cg=cg)
    return pl.pallas_call(
        kern,
        grid=(bsz, seq // tc),
        in_specs=[
            pl.BlockSpec((None, tc, db), lambda b, c: (b, c, colblk)),
            pl.BlockSpec((None, POOL_HIST, db), lambda b, c: (b, 0, 0)),
            pl.BlockSpec((None, n_g, cg, cg), lambda b, c: (layer, 0, 0, 0)),
            pl.BlockSpec((1, db), lambda b, c: (0, 0)),
        ],
        out_specs=[
            pl.BlockSpec((None, tc, db), lambda b, c: (b, c, 0)),
            pl.BlockSpec((None, POOL_HIST, db), lambda b, c: (b, 0, 0)),
        ],
        out_shape=[
            jax.ShapeDtypeStruct((bsz, seq, db), BF16),
            jax.ShapeDtypeStruct((bsz, POOL_HIST, db), F32),
        ],
        scratch_shapes=[pltpu.VMEM((POOL_HIST + tc, db), F32)],
        compiler_params=_params(("arbitrary", "arbitrary"), 40),
        name="pool",
    )(z, buf16, w, scale.reshape(1, db))


def _head_rms(x, g):
    ms = jnp.mean(x * x, axis=-1, keepdims=True)
    return (x * lax.rsqrt(ms + RMS_EPS)) * g


def _combine(os_, lses):
    m = jnp.maximum(jnp.maximum(lses[0], lses[1]), lses[2])
    ws = [jnp.exp(l - m) for l in lses]
    tot = ws[0] + ws[1] + ws[2]
    return (ws[0] * os_[0] + ws[1] * os_[1] + ws[2] * os_[2]) / tot


def _attn_prompt_kernel(q_ref, k_ref, v_ref, qn_ref, kn_ref, att_ref, ko_ref, vo_ref,
                        qs_scr, qf_scr, kf_scr, vf_scr, qd_scr, kd_scr, vd_scr, s_scr, p_scr, m_scr, o_scr, l_scr,
                        *, seq, scale):
    blk = ATT_BLOCK
    rows = 256
    n_all = seq // blk

    def prep(r, c):
        sl = pl.ds(pl.multiple_of(r * rows, rows), rows)
        qs_scr[sl, :] = _head_rms(q_ref[sl, :].astype(F32), qn_ref[...]) * scale
        ko_ref[sl, :] = _head_rms(k_ref[sl, :].astype(F32), kn_ref[...])
        vo_ref[sl, :] = v_ref[sl, :].astype(F32)
        return c

    lax.fori_loop(0, seq // rows, prep, 0, unroll=2)
    kd_scr[0:blk, :] = jnp.zeros((blk, LANES), BF16)
    vd_scr[0:blk, :] = jnp.zeros((blk, 2 * LANES), BF16)
    vd_scr[blk:, LANES:] = jnp.ones((seq, LANES), BF16)

    qi = lax.broadcasted_iota(jnp.int32, (blk, blk), 0)
    kj = lax.broadcasted_iota(jnp.int32, (blk, blk), 1)
    cur_ok = kj <= qi
    prev_ok = kj >= qi
    band_ok = jnp.concatenate([prev_ok, cur_ok], axis=1)
    in_cur = lax.broadcasted_iota(jnp.int32, (blk, 2 * blk), 1) >= blk

    for g, (window, dil) in enumerate(BRANCHES):
        n_blk = seq // (dil * blk)
        col0 = 0 if n_blk > 1 else LANES

        def place(idx, dil=dil, n_blk=n_blk):
            res = idx // n_blk
            n = idx - res * n_blk
            start = res + n * (dil * blk)
            nat = pl.ds(start, blk, stride=dil) if dil > 1 else pl.ds(pl.multiple_of(start, blk), blk)
            cur = pl.ds(pl.multiple_of(idx * blk, blk), blk)
            kcur = pl.ds(pl.multiple_of((idx + 1) * blk, blk), blk)
            kwin = pl.ds(pl.multiple_of(idx * blk, blk), 2 * blk)
            return nat, cur, kcur, kwin, n

        keep_f32 = dil == BRANCHES[1][1]
        two_level = g == 2 and dil == BRANCHES[1][1] ** 2

        def gather(idx, c, place=place, dil=dil, n_blk=n_blk, keep_f32=keep_f32, two_level=two_level):
            nat, cur, kcur, _, n = place(idx)
            if two_level:
                mid = BRANCHES[1][1]
                res = idx // n_blk
                start = (res % mid) * (seq // mid) + res // mid + n * (mid * blk)
                src = pl.ds(start, blk, stride=mid)
                q, k, v = qf_scr[src, :], kf_scr[src, :], vf_scr[src, :]
            else:
                q, k, v = qs_scr[nat, :], ko_ref[nat, :], vo_ref[nat, :]
            if keep_f32:
                qf_scr[cur, :], kf_scr[cur, :], vf_scr[cur, :] = q, k, v
            qd_scr[cur, :] = q.astype(BF16)
            kd_scr[kcur, :] = k.astype(BF16)
            vd_scr[kcur, 0:LANES] = v.astype(BF16)
            return c

        lax.fori_loop(0, n_all, gather, 0, unroll=4)

        def scores(idx, c, place=place, n_blk=n_blk):
            _, cur, kcur, kwin, n = place(idx)
            q = qd_scr[cur, :]
            if n_blk > 1:
                ok = jnp.logical_and(band_ok, jnp.logical_or(in_cur, n > 0))
                s_scr[idx] = jnp.where(ok, _dot_nt(q, kd_scr[kwin, :]), NEG_INF)
            else:
                s_scr[idx, :, LANES:] = jnp.where(cur_ok, _dot_nt(q, kd_scr[kcur, :]), NEG_INF)
            return c

        lax.fori_loop(0, n_all, scores, 0, unroll=8)

        def softmax(idx, c, col0=col0):
            s = s_scr[idx, :, col0:]
            m = jnp.max(s, axis=-1, keepdims=True)
            p_scr[idx, :, col0:] = jnp.exp(s - m).astype(BF16)
            m_scr[idx] = jnp.broadcast_to(m, (blk, LANES))
            return c

        lax.fori_loop(0, n_all, softmax, 0, unroll=4)

        def values(idx, c, g=g, place=place, n_blk=n_blk):
            nat, _, kcur, kwin, _ = place(idx)
            if n_blk > 1:
                ov = _dot(p_scr[idx], vd_scr[kwin, :])
            else:
                ov = _dot(p_scr[idx, :, LANES:], vd_scr[kcur, :])
            l = ov[:, LANES:]
            o_scr[g, nat, :] = ov[:, 0:LANES] / l
            l_scr[g, nat, :] = m_scr[idx] + jnp.log(l)
            return c

        lax.fori_loop(0, n_all, values, 0, unroll=8)

    def comb(r, c):
        sl = pl.ds(pl.multiple_of(r * rows, rows), rows)
        out = _combine([o_scr[g, sl, :] for g in range(3)], [l_scr[g, sl, :] for g in range(3)])
        att_ref[sl, :] = out.astype(att_ref.dtype)
        return c

    lax.fori_loop(0, seq // rows, comb, 0)


def attn_prompt(z, qn, kn, *, n_heads):
    bsz, seq, _ = z.shape
    assert seq % (BRANCHES[-1][1] * ATT_BLOCK) == 0
    hd = LANES
    kern = functools.partial(_attn_prompt_kernel, seq=seq, scale=hd ** -0.5)
    blk = lambda off: pl.BlockSpec((None, seq, hd), lambda b, h: (b, 0, off + h))
    return pl.pallas_call(
        kern,
        grid=(bsz, n_heads),
        in_specs=[blk(0), blk(n_heads), blk(2 * n_heads),
                  pl.BlockSpec((1, hd), lambda b, h: (0, 0)), pl.BlockSpec((1, hd), lambda b, h: (0, 0))],
        out_specs=[blk(0), blk(0), blk(0)],
        out_shape=[
            jax.ShapeDtypeStruct((bsz, seq, n_heads * hd), BF16),
            jax.ShapeDtypeStruct((bsz, seq, n_heads * hd), F32),
            jax.ShapeDtypeStruct((bsz, seq, n_heads * hd), F32),
        ],
        scratch_shapes=[
            pltpu.VMEM((seq, hd), F32),
            pltpu.VMEM((seq, hd), F32), pltpu.VMEM((seq, hd), F32), pltpu.VMEM((seq, hd), F32),
            pltpu.VMEM((seq, hd), BF16), pltpu.VMEM((seq + ATT_BLOCK, hd), BF16),
            pltpu.VMEM((seq + ATT_BLOCK, 2 * hd), BF16),
            pltpu.VMEM((seq // ATT_BLOCK, ATT_BLOCK, 2 * ATT_BLOCK), F32),
            pltpu.VMEM((seq // ATT_BLOCK, ATT_BLOCK, 2 * ATT_BLOCK), BF16),
            pltpu.VMEM((seq // ATT_BLOCK, ATT_BLOCK, hd), F32),
            pltpu.VMEM((3, seq, hd), F32),
            pltpu.VMEM((3, seq, hd), F32),
        ],
        compiler_params=_params(("arbitrary", "arbitrary"), 40),
        name="attn_prompt",
    )(z, z, z, qn.reshape(1, hd), kn.reshape(1, hd))


SAMPLE_PAD = 16


def _attn_sample_kernel(q_ref, k_ref, v_ref, ck_ref, cv_ref, qn_ref, kn_ref, att_ref, ko_ref, vo_ref,
                        q_scr, kn_scr, vn_scr, s_scr, v_scr, o_scr, *, s_new, n_buf, n_heads, pc, scale):
    c = pl.program_id(1)
    n_ch = n_buf // pc
    pad = SAMPLE_PAD

    @pl.when(c == 0)
    def _():
        q_scr[...] = jnp.zeros_like(q_scr)
        kn_scr[...] = jnp.zeros_like(kn_scr)
        vn_scr[...] = jnp.zeros_like(vn_scr)
        for h in range(n_heads):
            lanes = slice(h * LANES, (h + 1) * LANES)
            k_new = _head_rms(k_ref[:, lanes].astype(F32), kn_ref[...])
            v_new = v_ref[:, lanes].astype(F32)
            ko_ref[:, lanes] = k_new
            vo_ref[:, lanes] = v_new
            q_scr[h, 0:s_new, :] = _head_rms(q_ref[:, lanes].astype(F32), qn_ref[...]) * scale
            kn_scr[h, 0:s_new, :] = k_new
            vn_scr[h, 0:s_new, :] = v_new

    def per_head(h, carry):
        rows = pl.ds(h, pc, stride=n_heads)
        s_scr[h, c] = _dot_nt(q_scr[h].astype(BF16), ck_ref[rows, :].astype(BF16))
        v_scr[h, pl.ds(pl.multiple_of(c * pc, pc), pc), :] = cv_ref[rows, :].astype(BF16)
        return carry

    lax.fori_loop(0, n_heads, per_head, 0, unroll=4)

    @pl.when(c == n_ch - 1)
    def _():
        qi = lax.broadcasted_iota(jnp.int32, (pad, pc), 0)
        kj = lax.broadcasted_iota(jnp.int32, (pad, pc), 1)
        qi_n = lax.broadcasted_iota(jnp.int32, (pad, pad), 0)
        kj_n = lax.broadcasted_iota(jnp.int32, (pad, pad), 1)
        dist_n = qi_n - kj_n
        new_ok = jnp.logical_and(dist_n >= 0, kj_n < s_new)

        def finish(h, carry):
            q = q_scr[h].astype(BF16)
            s_n = _dot_nt(q, kn_scr[h].astype(BF16))
            s_c = [s_scr[h, cc] for cc in range(n_ch)]
            ps, pns, ls, ms = [], [], [], []
            for window, dil in BRANCHES:
                msk = []
                for cc in range(n_ch):
                    dist = n_buf + qi - (cc * pc + kj)
                    ok = jnp.logical_and((dist & (dil - 1)) == 0, dist <= window)
                    msk.append(jnp.where(ok, s_c[cc], NEG_INF))
                m_n = jnp.where(jnp.logical_and(new_ok, (dist_n & (dil - 1)) == 0), s_n, NEG_INF)
                m = jnp.max(m_n, axis=-1, keepdims=True)
                for cc in range(n_ch):
                    m = jnp.maximum(m, jnp.max(msk[cc], axis=-1, keepdims=True))
                p_n = jnp.exp(m_n - m)
                l = jnp.sum(p_n, axis=-1, keepdims=True)
                pb = []
                for cc in range(n_ch):
                    p = jnp.exp(msk[cc] - m)
                    l = l + jnp.sum(p, axis=-1, keepdims=True)
                    pb.append(p.astype(BF16))
                ps.append(pb)
                pns.append(p_n.astype(BF16))
                ls.append(l)
                ms.append(m)
            ov = _dot(jnp.concatenate(pns, axis=0), vn_scr[h].astype(BF16))
            for cc in range(n_ch):
                ov = ov + _dot(jnp.concatenate([ps[g][cc] for g in range(3)], axis=0),
                               v_scr[h, cc * pc:(cc + 1) * pc, :])
            outs = [ov[g * pad:(g + 1) * pad] / ls[g] for g in range(3)]
            lses = [jnp.broadcast_to(ms[g] + jnp.log(ls[g]), (pad, LANES)) for g in range(3)]
            o_scr[h] = _combine(outs, lses)
            return carry

        lax.fori_loop(0, n_heads, finish, 0, unroll=2)
        for h in range(n_heads):
            att_ref[:, h * LANES:(h + 1) * LANES] = o_scr[h, 0:s_new, :].astype(att_ref.dtype)


def attn_sample(z, cache_k, cache_v, row0, qn, kn, *, n_heads):
    bsz, s_new, _ = z.shape
    hd = LANES
    d_c = n_heads * hd
    n_buf = cache_k.shape[1] // n_heads
    assert n_buf >= BRANCHES[-1][0] and s_new <= SAMPLE_PAD
    pc = min(n_buf, 512)
    kern = functools.partial(_attn_sample_kernel, s_new=s_new, n_buf=n_buf, n_heads=n_heads, pc=pc,
                             scale=hd ** -0.5)
    blk = lambda off: pl.BlockSpec((None, s_new, d_c), lambda b, c: (b, 0, off))
    cblk = pl.BlockSpec((None, pc * n_heads, hd), lambda b, c: (row0 + b, c, 0))
    vec = pl.BlockSpec((1, hd), lambda b, c: (0, 0))
    return pl.pallas_call(
        kern,
        grid=(bsz, n_buf // pc),
        in_specs=[blk(0), blk(1), blk(2), cblk, cblk, vec, vec],
        out_specs=[blk(0), blk(0), blk(0)],
        out_shape=[
            jax.ShapeDtypeStruct((bsz, s_new, d_c), BF16),
            jax.ShapeDtypeStruct((bsz, s_new, d_c), F32),
            jax.ShapeDtypeStruct((bsz, s_new, d_c), F32),
        ],
        scratch_shapes=[
            pltpu.VMEM((n_heads, SAMPLE_PAD, hd), F32),
            pltpu.VMEM((n_heads, SAMPLE_PAD, hd), F32),
            pltpu.VMEM((n_heads, SAMPLE_PAD, hd), F32),
            pltpu.VMEM((n_heads, n_buf // pc, SAMPLE_PAD, pc), F32),
            pltpu.VMEM((n_heads, n_buf, hd), BF16),
            pltpu.VMEM((n_heads, SAMPLE_PAD, hd), F32),
        ],
        compiler_params=_params(("arbitrary", "arbitrary"), 40),
        name="attn_sample",
    )(z, z, z, cache_k, cache_v, qn.reshape(1, hd), kn.reshape(1, hd))


def _sgu_kernel(gu_ref, gv_ref, lg_ref, lb_ref, w_ref, bt_ref, o_ref, vn_ref, vb_scr, *, rows, n_g, cd):
    t = w_ref.shape[1]
    gv = _gelu(gv_ref[...].astype(F32))
    mu = jnp.mean(gv, axis=-1, keepdims=True)
    xc = gv - mu
    var = jnp.mean(xc * xc, axis=-1, keepdims=True)
    vn = (xc * lax.rsqrt(var + LN_EPS)) * lg_ref[...] + lb_ref[...]
    vn_ref[...] = vn
    if rows < t:
        vb_scr[...] = jnp.zeros_like(vb_scr)
    vb_scr[0:rows, :] = vn.astype(BF16)
    ri = lax.broadcasted_iota(jnp.int32, (t, t), 0)
    ci = lax.broadcasted_iota(jnp.int32, (t, t), 1)
    for g in range(n_g):
        cols = slice(g * cd, (g + 1) * cd)
        wg = jnp.where(ri >= ci, w_ref[g], 0.0).astype(BF16)
        mixed = _dot(wg, vb_scr[:, cols])[0:rows, :] + bt_ref[:, g:g + 1]
        o_ref[:, cols] = (_gelu(gu_ref[:, cols].astype(F32)) * mixed).astype(o_ref.dtype)


def sgu(z, colblk_u, ln_g, ln_b, w_s, b_s):
    bsz, seq, _ = z.shape
    n_g = w_s.shape[0]
    dd = ln_g.shape[0]
    cd = dd // n_g
    t = min(seq, CHUNK)
    tp = max(t, LANES)
    w = jnp.pad(w_s[:, :t, :t], ((0, 0), (0, tp - t), (0, tp - t)))
    bt = jnp.transpose(b_s[:, :t])
    kern = functools.partial(_sgu_kernel, rows=t, n_g=n_g, cd=cd)
    return pl.pallas_call(
        kern,
        grid=(bsz, seq // t),
        in_specs=[
            pl.BlockSpec((None, t, dd), lambda b, c: (b, c, colblk_u)),
            pl.BlockSpec((None, t, dd), lambda b, c: (b, c, colblk_u + 1)),
            pl.BlockSpec((1, dd), lambda b, c: (0, 0)),
            pl.BlockSpec((1, dd), lambda b, c: (0, 0)),
            pl.BlockSpec((n_g, tp, tp), lambda b, c: (0, 0, 0)),
            pl.BlockSpec((t, n_g), lambda b, c: (0, 0)),
        ],
        out_specs=[
            pl.BlockSpec((None, t, dd), lambda b, c: (b, c, 0)),
            pl.BlockSpec((None, t, dd), lambda b, c: (b, c, 0)),
        ],
        out_shape=[
            jax.ShapeDtypeStruct((bsz, seq, dd), BF16),
            jax.ShapeDtypeStruct((bsz, seq, dd), F32),
        ],
        scratch_shapes=[pltpu.VMEM((tp, dd), BF16)],
        compiler_params=_params(("arbitrary", "arbitrary"), 40),
        name="sgu",
    )(z, z, ln_g.reshape(1, dd), ln_b.reshape(1, dd), w, bt)


def _in_proj(x, norm_g, w_in, i, tiles):
    if tiles["cast"]:
        return norm_matmul_cast(x, norm_g, w_in, i, tn=MIX_TILE)
    return norm_matmul(x, norm_g, w_in, tm=tiles["tm"], out_dtype=tiles["z_dtype"]), None


def _res_proj(x, a, b, w_out, i, tiles):
    if tiles["cast"]:
        return out_proj_cast(x, a, b, w_out, i, tn=MIX_TILE)
    return out_proj(x, a, b, w_out, tm=tiles["tm_out"]), None


def _even_layer(x, bsz, seq, h0_re, h0_im, pool_buf, start_pos, norm_g, w_in, w_out, i, s5p, pool_w, pool_scale,
                w_glu, b_glu, d_skip, tiles):
    t, d = x.shape
    pw_re, pw_im, b_blk, c_blk = s5p
    n_slab = b_blk.shape[0]
    d_a = n_slab * LANES
    z, w_in_b = _in_proj(x, norm_g, w_in, i, tiles)
    z = z.reshape(bsz, seq, -1)
    h0 = jnp.concatenate([h0_re.reshape(bsz, n_slab, 1, SLAB_STATE), h0_im.reshape(bsz, n_slab, 1, SLAB_STATE)], axis=-1)
    if tiles["s5_long"]:
        y_pre, h_last = s5_scan_long(z, 0, b_blk, c_blk, pw_re, pw_im, d_skip, h0)
    else:
        y_pre, h_last = s5_scan(z, 0, b_blk, c_blk, pw_re, pw_im, d_skip, h0, split_in=True)
    ya = glu(y_pre.reshape(t, d_a), w_glu, b_glu, i, tm=tiles["tm_glu"], tn=MIX_TILE)
    buf16 = jnp.pad(pool_buf, ((0, 0), (POOL_HIST - pool_buf.shape[1], 0), (0, 0)))
    yb, tail = pool(z, 1, buf16, pool_w, i, pool_scale, start_pos=start_pos, tc=256)
    x, w_out_b = _res_proj(x, ya, yb.reshape(t, -1), w_out, i, tiles)
    g_a = n_slab * SLAB_GROUPS
    h_re = h_last[..., :SLAB_STATE].reshape(bsz, g_a, S5_P)
    h_im = h_last[..., SLAB_STATE:].reshape(bsz, g_a, S5_P)
    return x, h_re, h_im, tail[:, POOL_HIST - pool_buf.shape[1]:], (w_in_b, w_out_b)


def _odd_layer(x, bsz, seq, k_buf, v_buf, norm_g, w_in, w_out, i, qn, kn, ln_g, ln_b, w_s, b_s, n_heads, tiles):
    t, d = x.shape
    d_c = n_heads * LANES
    z, w_in_b = _in_proj(x, norm_g, w_in, i, tiles)
    z = z.reshape(bsz, seq, -1)
    if k_buf is None:
        att, k_new, v_new = attn_prompt(z, qn, kn, n_heads=n_heads)
    else:
        att, k_new, v_new = attn_sample(z, k_buf, v_buf, i * bsz, qn, kn, n_heads=n_heads)
    dd = ln_g.shape[0]
    sg, vn = sgu(z, (3 * d_c) // dd, ln_g, ln_b, w_s, b_s)
    x, w_out_b = _res_proj(x, att.reshape(t, d_c), sg.reshape(t, dd), w_out, i, tiles)
    hd = LANES
    return x, k_new.reshape(bsz, seq, n_heads, hd), v_new.reshape(bsz, seq, n_heads, hd), vn, (w_in_b, w_out_b)


def kernel(x_prompt, x_sample, state_s5_re, state_s5_im, state_pool, cache_k, cache_v, norm_mix, norm_ffn, ev_w_in, ev_w_out, s5_lambda_re, s5_lambda_im, s5_log_dt, s5_b_re, s5_b_im, s5_c_re, s5_c_im, s5_d, s5_w_glu, s5_b_glu, pool_w, pool_scale, od_w_in, od_w_out, q_norm, k_norm, sgu_ln_g, sgu_ln_b, sgu_w, sgu_b, ffn_w1, ffn_w3, ffn_w2):
    bp, lp, d = x_prompt.shape
    bs, ls, _ = x_sample.shape
    depth = norm_mix.shape[0]
    n_heads = cache_k.shape[3]
    xp = x_prompt.reshape(bp * lp, d)
    xs = x_sample.reshape(bs * ls, d)
    tiles_p = dict(cast=False, tm=1024, tm_glu=512, tm_out=1024, s5_long=True, z_dtype=BF16)
    tiles_s = dict(cast=True, tm_glu=bs * ls, s5_long=False)
    g_a, p_a = s5_lambda_re.shape[1:]
    w_glu_b, pool_w_b = s5_w_glu.astype(BF16), pool_w.astype(BF16)
    d_c = n_heads * LANES
    cache_k2 = cache_k.reshape(-1, cache_k.shape[2] * n_heads, LANES)
    cache_v2 = cache_v.reshape(-1, cache_v.shape[2] * n_heads, LANES)

    s5r_p, s5i_p, pool_p, k_p, v_p = [], [], [], [], []
    s5r_s, s5i_s, pool_s, k_s, v_s, sgu_s = [], [], [], [], [], []
    for l in range(depth):
        i = l // 2
        if l % 2 == 0:
            pw_re, pw_im, bb_re, bb_im = s5_prep(s5_lambda_re[i], s5_lambda_im[i], s5_log_dt[i], s5_b_re[i], s5_b_im[i])
            b_blk, c_blk = s5_block_matrices(bb_re, bb_im, s5_c_re[i], s5_c_im[i])
            s5p = (pw_re, pw_im, b_blk, c_blk)
            rest = (i, s5p, pool_w_b, pool_scale[i], w_glu_b, s5_b_glu[i], s5_d[i])
            xs, hr, hi, buf, (w_in_b, w_out_b) = _even_layer(
                xs, bs, ls, state_s5_re[i], state_s5_im[i], state_pool[i], PAST_LEN, norm_mix[l], ev_w_in, ev_w_out,
                *rest, tiles_s)
            s5r_s.append(hr); s5i_s.append(hi); pool_s.append(buf)
            zero_h = jnp.zeros((bp, g_a, p_a), F32)
            zero_buf = jnp.zeros((bp, state_pool.shape[2], state_pool.shape[3]), F32)
            xp, hr, hi, buf, _ = _even_layer(xp, bp, lp, zero_h, zero_h, zero_buf, 0, norm_mix[l], w_in_b, w_out_b,
                                             *rest, tiles_p)
            s5r_p.append(hr); s5i_p.append(hi); pool_p.append(buf)
        else:
            rest = (i, q_norm[i], k_norm[i], sgu_ln_g[i], sgu_ln_b[i], sgu_w[i], sgu_b[i], n_heads)
            xs, nk, nv, vrows, (w_in_b, w_out_b) = _odd_layer(xs, bs, ls, cache_k2, cache_v2, norm_mix[l], od_w_in,
                                                              od_w_out, *rest, tiles_s)
            k_s.append(nk); v_s.append(nv); sgu_s.append(vrows)
            xp, nk, nv, _, _ = _odd_layer(xp, bp, lp, None, None, norm_mix[l], w_in_b, w_out_b, *rest, tiles_p)
            k_p.append(nk); v_p.append(nv)
        if l == 0:
            xs, *ffn_wb = ffn_cast(xs, norm_ffn[l], ffn_w1, ffn_w3, ffn_w2, l, tf=FFN_TILE)
        else:
            xs = ffn(xs, norm_ffn[l], *ffn_wb, tm=bs * ls)
        if l + 1 < depth:
            xp, ffn_wb = ffn(xp, norm_ffn[l], *ffn_wb, tm=512, nxt=(ffn_w1, ffn_w3, ffn_w2, l + 1))
        else:
            xp = ffn(xp, norm_ffn[l], *ffn_wb, tm=512)
    return (xp.reshape(bp, lp, d), xs.reshape(bs, ls, d),
            jnp.stack(s5r_p), jnp.stack(s5i_p), jnp.stack(pool_p), jnp.stack(k_p), jnp.stack(v_p),
            jnp.stack(s5r_s), jnp.stack(s5i_s), jnp.stack(pool_s), jnp.stack(k_s), jnp.stack(v_s),
            jnp.stack(sgu_s))
```

```python
import functools
import math

import jax
import jax.numpy as jnp
from jax import lax
from jax.experimental import pallas as pl
from jax.experimental.pallas import tpu as pltpu

F32 = jnp.float32
BF16 = jnp.bfloat16

RMS_EPS = 1e-6
LN_EPS = 1e-5
NEG_INF = -1e30

LANES = 128
SUBLANES = 8
ATT_BLOCK = 128
CHUNK = 128
POOL_WINDOWS = (2, 4, 8, 16)
POOL_HIST = 16
BRANCHES = ((128, 1), (512, 4), (2048, 16))
S5_GRP = 16
S5_P = 64
SLAB_GROUPS = LANES // S5_GRP
SLAB_STATE = SLAB_GROUPS * S5_P
PAST_LEN = 8192
FFN_TILE = 256
MIX_TILE = 512


def _params(sem, vmem_mib):
    return pltpu.CompilerParams(dimension_semantics=sem, vmem_limit_bytes=vmem_mib << 20)


def _gelu(x):
    return 0.5 * x * (1.0 + lax.erf(x * (1.0 / math.sqrt(2.0))))


def _sigmoid(x):
    return 1.0 / (1.0 + jnp.exp(-x))


def _split_bf16(a):
    hi = a.astype(BF16)
    lo = (a - hi.astype(F32)).astype(BF16)
    return hi, lo


def _dot(a, b):
    return jnp.dot(a, b, preferred_element_type=F32)


def _dot_nt(a, b):
    return lax.dot_general(a, b, (((1,), (1,)), ((), ())), preferred_element_type=F32)


def _rms_rows_to(x_ref, g_ref, h_ref, rows):
    step = 16 if rows % 16 == 0 else rows

    def body(r, c):
        sl = pl.ds(pl.multiple_of(r * step, step), step)
        x = x_ref[sl, :]
        ms = jnp.mean(x * x, axis=-1, keepdims=True)
        h_ref[sl, :] = ((x * lax.rsqrt(ms + RMS_EPS)) * g_ref[...]).astype(h_ref.dtype)
        return c

    lax.fori_loop(0, rows // step, body, 0, unroll=min(4, rows // step))


def _norm_matmul_kernel(x_ref, g_ref, w_ref, o_ref, h_ref):
    @pl.when(pl.program_id(1) == 0)
    def _():
        _rms_rows_to(x_ref, g_ref, h_ref, x_ref.shape[0])

    o_ref[...] = _dot(h_ref[...], w_ref[...]).astype(o_ref.dtype)


def norm_matmul(x, g, w, *, tm, out_dtype=F32):
    t, d = x.shape
    n_t, _, tn = w.shape
    return pl.pallas_call(
        _norm_matmul_kernel,
        grid=(t // tm, n_t),
        in_specs=[
            pl.BlockSpec((tm, d), lambda i, j: (i, 0)),
            pl.BlockSpec((1, d), lambda i, j: (0, 0)),
            pl.BlockSpec((None, d, tn), lambda i, j: (j, 0, 0)),
        ],
        out_specs=pl.BlockSpec((tm, tn), lambda i, j: (i, j)),
        out_shape=jax.ShapeDtypeStruct((t, n_t * tn), out_dtype),
        scratch_shapes=[pltpu.VMEM((tm, d), BF16)],
        compiler_params=_params(("arbitrary", "arbitrary"), 58),
        name="norm_matmul",
    )(x, g.reshape(1, d), w)


def _norm_matmul_cast_kernel(x_ref, g_ref, w_ref, o_ref, wb_ref, h_ref):
    @pl.when(pl.program_id(0) == 0)
    def _():
        _rms_rows_to(x_ref, g_ref, h_ref, x_ref.shape[0])

    wb_ref[...] = w_ref[...].astype(BF16)
    o_ref[...] = _dot(h_ref[...], wb_ref[...]).astype(o_ref.dtype)


def norm_matmul_cast(x, g, w, layer, *, tn):
    t, d = x.shape
    n = w.shape[2]
    return pl.pallas_call(
        _norm_matmul_cast_kernel,
        grid=(n // tn,),
        in_specs=[
            pl.BlockSpec((t, d), lambda j: (0, 0)),
            pl.BlockSpec((1, d), lambda j: (0, 0)),
            pl.BlockSpec((None, d, tn), lambda j: (layer, 0, j)),
        ],
        out_specs=[
            pl.BlockSpec((t, tn), lambda j: (0, j)),
            pl.BlockSpec((None, d, tn), lambda j: (j, 0, 0)),
        ],
        out_shape=[
            jax.ShapeDtypeStruct((t, n), F32),
            jax.ShapeDtypeStruct((n // tn, d, tn), BF16),
        ],
        scratch_shapes=[pltpu.VMEM((t, d), BF16)],
        compiler_params=_params(("arbitrary",), 48),
        name="norm_matmul_cast",
    )(x, g.reshape(1, d), w)


FFN_OUT_CHUNK = 512


def _ffn_step(first, x_ref, g_ref, w1_ref, w3_ref, w2_ref, o_ref, h_ref):
    @pl.when(first)
    def _():
        _rms_rows_to(x_ref, g_ref, h_ref, x_ref.shape[0])
        o_ref[...] = x_ref[...]

    h = h_ref[...]
    a = _dot(h, w1_ref[...])
    b = _dot(h, w3_ref[...])
    u = ((a * _sigmoid(a)) * b).astype(BF16)
    for c in range(0, o_ref.shape[1], FFN_OUT_CHUNK):
        o_ref[:, c:c + FFN_OUT_CHUNK] += _dot(u, w2_ref[:, c:c + FFN_OUT_CHUNK])


def _ffn_kernel(x_ref, g_ref, w1_ref, w3_ref, w2_ref, o_ref, h_ref):
    _ffn_step(pl.program_id(1) == 0, x_ref, g_ref, w1_ref, w3_ref, w2_ref, o_ref, h_ref)


def _ffn_next_kernel(x_ref, g_ref, w1_ref, w3_ref, w2_ref, n1_ref, n3_ref, n2_ref,
                     o_ref, c1_ref, c3_ref, c2_ref, h_ref):
    c1_ref[...] = n1_ref[...].astype(BF16)
    c3_ref[...] = n3_ref[...].astype(BF16)
    c2_ref[...] = n2_ref[...].astype(BF16)
    _ffn_step(pl.program_id(1) == 0, x_ref, g_ref, w1_ref, w3_ref, w2_ref, o_ref, h_ref)


def ffn(x, g, w1, w3, w2, *, tm, nxt=None):
    t, d = x.shape
    n_f, _, tf = w1.shape
    n_m = t // tm
    specs = [
        pl.BlockSpec((tm, d), lambda i, j: (i, 0)),
        pl.BlockSpec((1, d), lambda i, j: (0, 0)),
        pl.BlockSpec((None, d, tf), lambda i, j: (j, 0, 0)),
        pl.BlockSpec((None, d, tf), lambda i, j: (j, 0, 0)),
        pl.BlockSpec((tf, d), lambda i, j: (j, 0)),
    ]
    out_spec = pl.BlockSpec((tm, d), lambda i, j: (i, 0))
    out_shape = jax.ShapeDtypeStruct((t, d), F32)
    common = dict(grid=(n_m, n_f), scratch_shapes=[pltpu.VMEM((tm, d), BF16)],
                  compiler_params=_params(("arbitrary", "arbitrary"), 56))
    if nxt is None:
        return pl.pallas_call(_ffn_kernel, in_specs=specs, out_specs=out_spec, out_shape=out_shape, name="ffn",
                              **common)(x, g.reshape(1, d), w1, w3, w2)
    n1, n3, n2, layer = nxt
    piece = d // n_m
    assert d % n_m == 0 and piece % LANES == 0
    col = pl.BlockSpec((None, piece, tf), lambda i, j: (layer, i, j))
    row = pl.BlockSpec((None, tf, piece), lambda i, j: (layer, j, i))
    col_out = pl.BlockSpec((None, piece, tf), lambda i, j: (j, i, 0))
    row_out = pl.BlockSpec((tf, piece), lambda i, j: (j, i))
    out, c1, c3, c2 = pl.pallas_call(
        _ffn_next_kernel,
        in_specs=specs + [col, col, row],
        out_specs=[out_spec, col_out, col_out, row_out],
        out_shape=[out_shape, jax.ShapeDtypeStruct(w1.shape, BF16), jax.ShapeDtypeStruct(w3.shape, BF16),
                   jax.ShapeDtypeStruct(w2.shape, BF16)],
        name="ffn_next", **common,
    )(x, g.reshape(1, d), w1, w3, w2, n1, n3, n2)
    return out, (c1, c3, c2)


def _ffn_cast_kernel(x_ref, g_ref, w1_ref, w3_ref, w2_ref, o_ref, w1b_ref, w3b_ref, w2b_ref, h_ref):
    w1b_ref[...] = w1_ref[...].astype(BF16)
    w3b_ref[...] = w3_ref[...].astype(BF16)
    w2b_ref[...] = w2_ref[...].astype(BF16)
    _ffn_step(pl.program_id(0) == 0, x_ref, g_ref, w1b_ref, w3b_ref, w2b_ref, o_ref, h_ref)


def ffn_cast(x, g, w1, w3, w2, layer, *, tf):
    t, d = x.shape
    f = w1.shape[2]
    return pl.pallas_call(
        _ffn_cast_kernel,
        grid=(f // tf,),
        in_specs=[
            pl.BlockSpec((t, d), lambda j: (0, 0)),
            pl.BlockSpec((1, d), lambda j: (0, 0)),
            pl.BlockSpec((None, d, tf), lambda j: (layer, 0, j)),
            pl.BlockSpec((None, d, tf), lambda j: (layer, 0, j)),
            pl.BlockSpec((None, tf, d), lambda j: (layer, j, 0)),
        ],
        out_specs=[
            pl.BlockSpec((t, d), lambda j: (0, 0)),
            pl.BlockSpec((None, d, tf), lambda j: (j, 0, 0)),
            pl.BlockSpec((None, d, tf), lambda j: (j, 0, 0)),
            pl.BlockSpec((tf, d), lambda j: (j, 0)),
        ],
        out_shape=[
            jax.ShapeDtypeStruct((t, d), F32),
            jax.ShapeDtypeStruct((f // tf, d, tf), BF16),
            jax.ShapeDtypeStruct((f // tf, d, tf), BF16),
            jax.ShapeDtypeStruct((f, d), BF16),
        ],
        scratch_shapes=[pltpu.VMEM((t, d), BF16)],
        compiler_params=_params(("arbitrary",), 48),
        name="ffn_cast",
    )(x, g.reshape(1, d), w1, w3, w2)


def _out_proj_kernel(x_ref, a_ref, b_ref, wa_ref, wb_ref, o_ref):
    o_ref[...] = x_ref[...] + _dot(a_ref[...], wa_ref[...]) + _dot(b_ref[...], wb_ref[...])


def out_proj(x, a, b, w_pair, *, tm):
    t, d = x.shape
    wa, wb = w_pair
    n_t, k, tn = wa.shape
    assert a.shape[1] == k and b.shape[1] == k and wb.shape == wa.shape
    return pl.pallas_call(
        _out_proj_kernel,
        grid=(t // tm, n_t),
        in_specs=[
            pl.BlockSpec((tm, tn), lambda i, j: (i, j)),
            pl.BlockSpec((tm, k), lambda i, j: (i, 0)),
            pl.BlockSpec((tm, k), lambda i, j: (i, 0)),
            pl.BlockSpec((None, k, tn), lambda i, j: (j, 0, 0)),
            pl.BlockSpec((None, k, tn), lambda i, j: (j, 0, 0)),
        ],
        out_specs=pl.BlockSpec((tm, tn), lambda i, j: (i, j)),
        out_shape=jax.ShapeDtypeStruct((t, d), F32),
        compiler_params=_params(("arbitrary", "arbitrary"), 48),
        name="out_proj",
    )(x, a, b, wa, wb)


def _out_proj_cast_kernel(x_ref, a_ref, b_ref, wa_ref, wb_ref, o_ref, wab_ref, wbb_ref):
    wab_ref[...] = wa_ref[...].astype(BF16)
    wbb_ref[...] = wb_ref[...].astype(BF16)
    o_ref[...] = x_ref[...] + _dot(a_ref[...], wab_ref[...]) + _dot(b_ref[...], wbb_ref[...])


def out_proj_cast(x, a, b, w, layer, *, tn):
    t, d = x.shape
    k = a.shape[1]
    assert b.shape[1] == k and w.shape[1] == 2 * k
    out, wa_b, wb_b = pl.pallas_call(
        _out_proj_cast_kernel,
        grid=(d // tn,),
        in_specs=[
            pl.BlockSpec((t, tn), lambda j: (0, j)),
            pl.BlockSpec((t, k), lambda j: (0, 0)),
            pl.BlockSpec((t, k), lambda j: (0, 0)),
            pl.BlockSpec((None, k, tn), lambda j: (layer, 0, j)),
            pl.BlockSpec((None, k, tn), lambda j: (layer, 1, j)),
        ],
        out_specs=[
            pl.BlockSpec((t, tn), lambda j: (0, j)),
            pl.BlockSpec((None, k, tn), lambda j: (j, 0, 0)),
            pl.BlockSpec((None, k, tn), lambda j: (j, 0, 0)),
        ],
        out_shape=[
            jax.ShapeDtypeStruct((t, d), F32),
            jax.ShapeDtypeStruct((d // tn, k, tn), BF16),
            jax.ShapeDtypeStruct((d // tn, k, tn), BF16),
        ],
        compiler_params=_params(("arbitrary",), 48),
        name="out_proj_cast",
    )(x, a, b, w, w)
    return out, (wa_b, wb_b)


def _s5_prep_kernel(lr_ref, li_ref, ldt_ref, lrx_ref, lix_ref, ldtx_ref, br_ref, bi_ref,
                    pwr_ref, pwi_ref, bbr_ref, bbi_ref):
    dt = jnp.exp(ldt_ref[...])
    mag = jnp.exp(lr_ref[...] * dt)
    ang = li_ref[...] * dt
    p_r, p_i = mag * jnp.cos(ang), mag * jnp.sin(ang)
    c_r, c_i = p_r, p_i
    pwr_ref[0], pwi_ref[0] = c_r, c_i
    for j in range(1, SUBLANES):
        c_r, c_i = c_r * p_r - c_i * p_i, c_r * p_i + c_i * p_r
        pwr_ref[j], pwi_ref[j] = c_r, c_i
    lr, li = lrx_ref[...], lix_ref[...]
    dtx = jnp.exp(ldtx_ref[...])
    magx = jnp.exp(lr * dtx)
    angx = li * dtx
    nr, ni = magx * jnp.cos(angx) - 1.0, magx * jnp.sin(angx)
    den = lr * lr + li * li
    qr = (nr * lr + ni * li) / den
    qi = (ni * lr - nr * li) / den
    br, bi = br_ref[...], bi_ref[...]
    bbr_ref[...] = qr * br - qi * bi
    bbi_ref[...] = qr * bi + qi * br


def s5_prep(lam_re, lam_im, log_dt, b_re, b_im):
    g, p = lam_re.shape
    h = b_re.shape[2]
    n_slab = g // SLAB_GROUPS
    slab = lambda a: a.reshape(n_slab, SLAB_GROUPS * p)
    rep = lambda a: jnp.repeat(a, h, axis=1)
    ldt_gp = jnp.broadcast_to(log_dt[:, None], (g, p))
    outs = pl.pallas_call(
        _s5_prep_kernel,
        out_shape=[
            jax.ShapeDtypeStruct((SUBLANES, n_slab, SLAB_GROUPS * p), F32),
            jax.ShapeDtypeStruct((SUBLANES, n_slab, SLAB_GROUPS * p), F32),
            jax.ShapeDtypeStruct((g, p * h), F32),
            jax.ShapeDtypeStruct((g, p * h), F32),
        ],
        name="s5_prep",
    )(slab(lam_re), slab(lam_im), slab(ldt_gp), rep(lam_re), rep(lam_im), rep(ldt_gp),
      b_re.reshape(g, p * h), b_im.reshape(g, p * h))
    pw_re, pw_im, bb_re, bb_im = outs
    pw_re = jnp.transpose(pw_re, (1, 0, 2))
    pw_im = jnp.transpose(pw_im, (1, 0, 2))
    return pw_re, pw_im, bb_re.reshape(g, p, h), bb_im.reshape(g, p, h)


def s5_block_matrices(bb_re, bb_im, c_re, c_im):
    g, p, h = bb_re.shape
    n_slab = g // SLAB_GROUPS
    eye = jnp.eye(SLAB_GROUPS, dtype=F32)

    def in_map(bb):
        t = bb.reshape(n_slab, SLAB_GROUPS, p, h)
        return jnp.einsum("kgph,gj->kghjp", t, eye).reshape(n_slab, SLAB_GROUPS * h, SLAB_GROUPS * p)

    def out_map(c):
        t = c.reshape(n_slab, SLAB_GROUPS, h, p)
        return jnp.einsum("kghp,gj->kgpjh", t, eye).reshape(n_slab, SLAB_GROUPS * p, SLAB_GROUPS * h)

    b_blk = jnp.concatenate([in_map(bb_re), in_map(bb_im)], axis=2)
    c_blk = jnp.concatenate([out_map(c_re), -out_map(c_im)], axis=1)
    return b_blk, c_blk


def _s5_scan_kernel(u_ref, bblk_ref, cblk_ref, pwr_ref, pwi_ref, d_ref, h0_ref,
                    y_ref, hl_ref, h_scr, bh_scr, bl_scr, ch_scr, *, seq, row_chunk, split_in):
    ns = SLAB_STATE

    @pl.when(pl.program_id(1) == 0)
    def _():
        bh, bl = _split_bf16(bblk_ref[...])
        bh_scr[...], bl_scr[...] = bh, bl
        ch_scr[...] = cblk_ref[...].astype(BF16)

    n_chunks = seq // row_chunk

    def proj_in(r, c):
        sl = pl.ds(pl.multiple_of(r * row_chunk, row_chunk), row_chunk)
        if split_in:
            uh, ul = _split_bf16(u_ref[sl, :].astype(F32))
            h_scr[sl, :] = _dot(uh, bh_scr[...]) + _dot(ul, bh_scr[...]) + _dot(uh, bl_scr[...])
        else:
            h_scr[sl, :] = _dot(u_ref[sl, :].astype(BF16), bh_scr[...])
        return c

    lax.fori_loop(0, n_chunks, proj_in, 0)

    rowid = lax.broadcasted_iota(jnp.int32, (SUBLANES, LANES), 0)
    for c in range(ns // LANES):
        re_l = slice(c * LANES, (c + 1) * LANES)
        im_l = slice(ns + c * LANES, ns + (c + 1) * LANES)
        p_r, p_i = pwr_ref[:, re_l], pwi_ref[:, re_l]
        steps = []
        for dist in (1, 2, 4):
            a_r = jnp.where(rowid >= dist, jnp.broadcast_to(p_r[dist - 1:dist], (SUBLANES, LANES)), 0.0)
            a_i = jnp.where(rowid >= dist, jnp.broadcast_to(p_i[dist - 1:dist], (SUBLANES, LANES)), 0.0)
            steps.append((dist, a_r, a_i))
        c_r = jnp.broadcast_to(h0_ref[:, re_l], (SUBLANES, LANES))
        c_i = jnp.broadcast_to(h0_ref[:, im_l], (SUBLANES, LANES))

        last = slice(SUBLANES - 1, SUBLANES)
        full = (SUBLANES, LANES)
        p8_r, p8_i = jnp.broadcast_to(p_r[last], full), jnp.broadcast_to(p_i[last], full)
        n_groups = seq // SUBLANES
        per_it = min(4, n_groups)

        def scan_rows(it, carry, re_l=re_l, im_l=im_l, p_r=p_r, p_i=p_i, p8_r=p8_r, p8_i=p8_i, steps=steps):
            base = pl.multiple_of(it * (per_it * SUBLANES), per_it * SUBLANES)
            sls = [pl.ds(base + j * SUBLANES, SUBLANES) for j in range(per_it)]
            loc = []
            for sl in sls:
                r, i = h_scr[sl, re_l], h_scr[sl, im_l]
                for dist, a_r, a_i in steps:
                    s_r, s_i = pltpu.roll(r, dist, 0), pltpu.roll(i, dist, 0)
                    r, i = r + (s_r * a_r - s_i * a_i), i + (s_r * a_i + s_i * a_r)
                loc.append((r, i))
            c_r, c_i = carry
            outs = []
            for r, i in loc:
                outs.append((r + (c_r * p_r - c_i * p_i), i + (c_r * p_i + c_i * p_r)))
                e_r, e_i = jnp.broadcast_to(r[last], full), jnp.broadcast_to(i[last], full)
                c_r, c_i = e_r + (c_r * p8_r - c_i * p8_i), e_i + (c_r * p8_i + c_i * p8_r)
            for sl, (r, i) in zip(sls, outs):
                h_scr[sl, re_l], h_scr[sl, im_l] = r, i
            return c_r, c_i

        c_r, c_i = lax.fori_loop(0, n_groups // per_it, scan_rows, (c_r, c_i))
        hl_ref[:, re_l] = c_r[0:1]
        hl_ref[:, im_l] = c_i[0:1]

    def proj_out(r, c):
        sl = pl.ds(pl.multiple_of(r * row_chunk, row_chunk), row_chunk)
        y = _dot(h_scr[sl, :].astype(BF16), ch_scr[...]) + d_ref[...] * u_ref[sl, :].astype(F32)
        y_ref[sl, :] = _gelu(y)
        return c

    lax.fori_loop(0, n_chunks, proj_out, 0)


def s5_scan(z, col0, b_blk, c_blk, pw_re, pw_im, d_skip, h0, *, split_in):
    bsz, seq, _ = z.shape
    n_slab = b_blk.shape[0]
    ns2 = 2 * SLAB_STATE
    row_chunk = min(seq, 1024)
    kern = functools.partial(_s5_scan_kernel, seq=seq, row_chunk=row_chunk, split_in=split_in)
    return pl.pallas_call(
        kern,
        grid=(n_slab, bsz),
        in_specs=[
            pl.BlockSpec((None, seq, LANES), lambda k, b: (b, 0, col0 + k)),
            pl.BlockSpec((None, LANES, ns2), lambda k, b: (k, 0, 0)),
            pl.BlockSpec((None, ns2, LANES), lambda k, b: (k, 0, 0)),
            pl.BlockSpec((None, SUBLANES, SLAB_STATE), lambda k, b: (k, 0, 0)),
            pl.BlockSpec((None, SUBLANES, SLAB_STATE), lambda k, b: (k, 0, 0)),
            pl.BlockSpec((1, LANES), lambda k, b: (0, k)),
            pl.BlockSpec((None, None, 1, ns2), lambda k, b: (b, k, 0, 0)),
        ],
        out_specs=[
            pl.BlockSpec((None, seq, LANES), lambda k, b: (b, 0, k)),
            pl.BlockSpec((None, None, 1, ns2), lambda k, b: (b, k, 0, 0)),
        ],
        out_shape=[
            jax.ShapeDtypeStruct((bsz, seq, n_slab * LANES), F32),
            jax.ShapeDtypeStruct((bsz, n_slab, 1, ns2), F32),
        ],
        scratch_shapes=[
            pltpu.VMEM((seq, ns2), F32),
            pltpu.VMEM((LANES, ns2), BF16), pltpu.VMEM((LANES, ns2), BF16),
            pltpu.VMEM((ns2, LANES), BF16),
        ],
        compiler_params=_params(("arbitrary", "arbitrary"), 40),
        name="s5_scan",
    )(z, b_blk, c_blk, pw_re, pw_im, d_skip.reshape(1, -1), h0)


def _s5_seg_kernel(u_ref, bblk_ref, cblk_ref, pwr_ref, pwi_ref, d_ref, h0_ref,
                   y_ref, hl_ref, up_scr, h_scr, yp_scr, w_scr, bh_scr, ch_scr, *, seq, row_chunk):
    ns = SLAB_STATE
    nseg = SUBLANES
    seg = seq // nseg
    n_lb = ns // LANES
    full = (SUBLANES, LANES)
    lanes = [(slice(c * LANES, (c + 1) * LANES), slice(ns + c * LANES, ns + (c + 1) * LANES)) for c in range(n_lb)]
    last = slice(SUBLANES - 1, SUBLANES)

    @pl.when(pl.program_id(1) == 0)
    def _():
        bh_scr[...] = bblk_ref[...].astype(BF16)
        ch_scr[...] = cblk_ref[...].astype(BF16)
        for re_l, im_l in lanes:
            p_r, p_i = pwr_ref[:, re_l], pwi_ref[:, re_l]
            p8_r, p8_i = jnp.broadcast_to(p_r[last], full), jnp.broadcast_to(p_i[last], full)
            w_scr[0:SUBLANES, re_l], w_scr[0:SUBLANES, im_l] = p_r, p_i

            def grow(gi, carry, re_l=re_l, im_l=im_l, p8_r=p8_r, p8_i=p8_i):
                w_r, w_i = carry
                w_r, w_i = w_r * p8_r - w_i * p8_i, w_r * p8_i + w_i * p8_r
                sl = pl.ds(pl.multiple_of(gi * SUBLANES, SUBLANES), SUBLANES)
                w_scr[sl, re_l], w_scr[sl, im_l] = w_r, w_i
                return w_r, w_i

            lax.fori_loop(1, seg // SUBLANES, grow, (p_r, p_i))

    for s in range(nseg):
        up_scr[pl.ds(s, seg, stride=nseg), :] = u_ref[s * seg:(s + 1) * seg, :].astype(F32)

    def proj_in(r, c):
        sl = pl.ds(pl.multiple_of(r * row_chunk, row_chunk), row_chunk)
        h_scr[sl, :] = _dot(up_scr[sl, :].astype(BF16), bh_scr[...])
        return c

    lax.fori_loop(0, seq // row_chunk, proj_in, 0, unroll=True)

    lam = [(jnp.broadcast_to(pwr_ref[0:1, re_l], full), jnp.broadcast_to(pwi_ref[0:1, re_l], full))
           for re_l, _ in lanes]
    per_it = 2

    def scan_t(it, carry):
        base = pl.multiple_of(it * (per_it * SUBLANES), per_it * SUBLANES)
        sls = [pl.ds(base + j * SUBLANES, SUBLANES) for j in range(per_it)]
        bu = [[(h_scr[sl, re_l], h_scr[sl, im_l]) for re_l, im_l in lanes] for sl in sls]
        hs, outs = list(carry), []
        for j in range(per_it):
            hs = [(bu[j][c][0] + (hs[c][0] * lam[c][0] - hs[c][1] * lam[c][1]),
                   bu[j][c][1] + (hs[c][0] * lam[c][1] + hs[c][1] * lam[c][0])) for c in range(n_lb)]
            outs.append(hs)
        for sl, row in zip(sls, outs):
            for (re_l, im_l), (h_r, h_i) in zip(lanes, row):
                h_scr[sl, re_l], h_scr[sl, im_l] = h_r, h_i
        return tuple(hs)

    zero = jnp.zeros(full, F32)
    ends = lax.fori_loop(0, seg // per_it, scan_t, tuple((zero, zero) for _ in range(n_lb)))

    rowid = lax.broadcasted_iota(jnp.int32, full, 0)
    enter = []
    for (re_l, im_l), (e_r, e_i) in zip(lanes, ends):
        ws_r, ws_i = w_scr[seg - 1:seg, re_l], w_scr[seg - 1:seg, im_l]
        c_r, c_i = h0_ref[:, re_l], h0_ref[:, im_l]
        cv_r, cv_i = jnp.broadcast_to(c_r, full), jnp.broadcast_to(c_i, full)
        for s in range(1, nseg + 1):
            c_r, c_i = (e_r[s - 1:s] + (c_r * ws_r - c_i * ws_i), e_i[s - 1:s] + (c_r * ws_i + c_i * ws_r))
            if s < nseg:
                cv_r = jnp.where(rowid == s, jnp.broadcast_to(c_r, full), cv_r)
                cv_i = jnp.where(rowid == s, jnp.broadcast_to(c_i, full), cv_i)
        hl_ref[:, re_l], hl_ref[:, im_l] = c_r, c_i
        enter.append((cv_r, cv_i))

    def fix_t(gi, c):
        wsl = pl.ds(pl.multiple_of(gi * SUBLANES, SUBLANES), SUBLANES)
        wv = [(w_scr[wsl, re_l], w_scr[wsl, im_l]) for re_l, im_l in lanes]
        base = pl.multiple_of(gi * (SUBLANES * SUBLANES), SUBLANES * SUBLANES)
        for j in range(SUBLANES):
            sl = pl.ds(base + j * SUBLANES, SUBLANES)
            for (re_l, im_l), (cv_r, cv_i), (wv_r, wv_i) in zip(lanes, enter, wv):
                w_r = jnp.broadcast_to(wv_r[j:j + 1], full)
                w_i = jnp.broadcast_to(wv_i[j:j + 1], full)
                h_r = h_scr[sl, re_l] + (w_r * cv_r - w_i * cv_i)
                h_i = h_scr[sl, im_l] + (w_r * cv_i + w_i * cv_r)
                h_scr[sl, re_l], h_scr[sl, im_l] = h_r, h_i
        return c

    lax.fori_loop(0, seg // SUBLANES, fix_t, 0)

    def proj_out(r, c):
        sl = pl.ds(pl.multiple_of(r * row_chunk, row_chunk), row_chunk)
        y = _dot(h_scr[sl, :].astype(BF16), ch_scr[...]) + d_ref[...] * up_scr[sl, :]
        yp_scr[sl, :] = _gelu(y)
        return c

    lax.fori_loop(0, seq // row_chunk, proj_out, 0, unroll=True)

    for s in range(nseg):
        y_ref[s * seg:(s + 1) * seg, :] = yp_scr[pl.ds(s, seg, stride=nseg), :]


def s5_scan_long(z, col0, b_blk, c_blk, pw_re, pw_im, d_skip, h0):
    bsz, seq, _ = z.shape
    n_slab = b_blk.shape[0]
    ns2 = 2 * SLAB_STATE
    row_chunk = min(seq, 1024)
    assert seq % row_chunk == 0 and seq % (2 * SUBLANES * SUBLANES) == 0
    kern = functools.partial(_s5_seg_kernel, seq=seq, row_chunk=row_chunk)
    return pl.pallas_call(
        kern,
        grid=(n_slab, bsz),
        in_specs=[
            pl.BlockSpec((None, seq, LANES), lambda k, b: (b, 0, col0 + k)),
            pl.BlockSpec((None, LANES, ns2), lambda k, b: (k, 0, 0)),
            pl.BlockSpec((None, ns2, LANES), lambda k, b: (k, 0, 0)),
            pl.BlockSpec((None, SUBLANES, SLAB_STATE), lambda k, b: (k, 0, 0)),
            pl.BlockSpec((None, SUBLANES, SLAB_STATE), lambda k, b: (k, 0, 0)),
            pl.BlockSpec((1, LANES), lambda k, b: (0, k)),
            pl.BlockSpec((None, None, 1, ns2), lambda k, b: (b, k, 0, 0)),
        ],
        out_specs=[
            pl.BlockSpec((None, seq, LANES), lambda k, b: (b, 0, k)),
            pl.BlockSpec((None, None, 1, ns2), lambda k, b: (b, k, 0, 0)),
        ],
        out_shape=[
            jax.ShapeDtypeStruct((bsz, seq, n_slab * LANES), F32),
            jax.ShapeDtypeStruct((bsz, n_slab, 1, ns2), F32),
        ],
        scratch_shapes=[
            pltpu.VMEM((seq, LANES), F32),
            pltpu.VMEM((seq, ns2), F32),
            pltpu.VMEM((seq, LANES), F32),
            pltpu.VMEM((seq // SUBLANES, ns2), F32),
            pltpu.VMEM((LANES, ns2), BF16),
            pltpu.VMEM((ns2, LANES), BF16),
        ],
        compiler_params=_params(("arbitrary", "arbitrary"), 40),
        name="s5_scan_long",
    )(z, b_blk, c_blk, pw_re, pw_im, d_skip.reshape(1, -1), h0)


def _glu_kernel(yk_ref, yj_ref, w_ref, b_ref, o_ref, yb_scr):
    @pl.when(pl.program_id(1) == 0)
    def _():
        yb_scr[...] = yk_ref[...].astype(BF16)

    gate = _dot(yb_scr[...], w_ref[...]) + b_ref[...]
    o_ref[...] = (yj_ref[...] * _sigmoid(gate)).astype(o_ref.dtype)


def glu(y, w, b, layer, *, tm, tn):
    t, d = y.shape
    tm = min(tm, t)
    return pl.pallas_call(
        _glu_kernel,
        grid=(t // tm, d // tn),
        in_specs=[
            pl.BlockSpec((tm, d), lambda i, j: (i, 0)),
            pl.BlockSpec((tm, tn), lambda i, j: (i, j)),
            pl.BlockSpec((None, d, tn), lambda i, j: (layer, 0, j)),
            pl.BlockSpec((1, tn), lambda i, j: (0, j)),
        ],
        out_specs=pl.BlockSpec((tm, tn), lambda i, j: (i, j)),
        out_shape=jax.ShapeDtypeStruct((t, d), BF16),
        scratch_shapes=[pltpu.VMEM((tm, d), BF16)],
        compiler_params=_params(("arbitrary", "arbitrary"), 40),
        name="glu",
    )(y, y, w, b.reshape(1, d))


def _pool_kernel(u_ref, buf_ref, w_ref, s_ref, y_ref, tail_ref, ext_scr, *, tc, start_pos, cg):
    c = pl.program_id(1)

    @pl.when(c == 0)
    def _():
        ext_scr[0:POOL_HIST, :] = buf_ref[...]

    ext_scr[POOL_HIST:POOL_HIST + tc, :] = u_ref[...].astype(F32)
    pos = start_pos + c * tc + lax.broadcasted_iota(jnp.int32, (tc, 1), 0)
    for g, win in enumerate(POOL_WINDOWS):
        cols = slice(g * cg, (g + 1) * cg)
        x = ext_scr[:, cols]
        acc, dist = x, 1
        while dist < win:
            acc = acc + pltpu.roll(acc, dist, 0)
            dist *= 2
        wsum = acc[POOL_HIST:, :]
        cnt = jnp.minimum(pos + 1, win).astype(F32)
        zg = wsum * (1.0 / cnt) - x[POOL_HIST:, :]
        y = _dot(zg.astype(BF16), w_ref[g]) * s_ref[:, cols]
        y_ref[:, cols] = y.astype(y_ref.dtype)

    tail = ext_scr[tc:tc + POOL_HIST, :]
    ext_scr[0:POOL_HIST, :] = tail

    @pl.when(c == pl.num_programs(1) - 1)
    def _():
        tail_ref[...] = tail


def pool(z, colblk, buf16, w, layer, scale, *, start_pos, tc):
    bsz, seq, _ = z.shape
    _, n_g, cg, _ = w.shape
    db = n_g * cg
    tc = min(tc, seq)
    kern = functools.partial(_pool_kernel, tc=tc, start_pos=start_pos, cg=cg)
    return pl.pallas_call(
        kern,
        grid=(bsz, seq // tc),
        in_specs=[
            pl.BlockSpec((None, tc, db), lambda b, c: (b, c, colblk)),
            pl.BlockSpec((None, POOL_HIST, db), lambda b, c: (b, 0, 0)),
            pl.BlockSpec((None, n_g, cg, cg), lambda b, c: (layer, 0, 0, 0)),
            pl.BlockSpec((1, db), lambda b, c: (0, 0)),
        ],
        out_specs=[
            pl.BlockSpec((None, tc, db), lambda b, c: (b, c, 0)),
            pl.BlockSpec((None, POOL_HIST, db), lambda b, c: (b, 0, 0)),
        ],
        out_shape=[
            jax.ShapeDtypeStruct((bsz, seq, db), BF16),
            jax.ShapeDtypeStruct((bsz, POOL_HIST, db), F32),
        ],
        scratch_shapes=[pltpu.VMEM((POOL_HIST + tc, db), F32)],
        compiler_params=_params(("arbitrary", "arbitrary"), 40),
        name="pool",
    )(z, buf16, w, scale.reshape(1, db))


def _head_rms(x, g):
    ms = jnp.mean(x * x, axis=-1, keepdims=True)
    return (x * lax.rsqrt(ms + RMS_EPS)) * g


def _combine(os_, lses):
    m = jnp.maximum(jnp.maximum(lses[0], lses[1]), lses[2])
    ws = [jnp.exp(l - m) for l in lses]
    tot = ws[0] + ws[1] + ws[2]
    return (ws[0] * os_[0] + ws[1] * os_[1] + ws[2] * os_[2]) / tot


def _attn_prompt_kernel(q_ref, k_ref, v_ref, qn_ref, kn_ref, att_ref, ko_ref, vo_ref,
                        qs_scr, qf_scr, kf_scr, vf_scr, qd_scr, kd_scr, vd_scr, s_scr, p_scr, m_scr, o_scr, l_scr,
                        *, seq, scale):
    blk = ATT_BLOCK
    rows = 256
    n_all = seq // blk

    def prep(r, c):
        sl = pl.ds(pl.multiple_of(r * rows, rows), rows)
        qs_scr[sl, :] = _head_rms(q_ref[sl, :].astype(F32), qn_ref[...]) * scale
        ko_ref[sl, :] = _head_rms(k_ref[sl, :].astype(F32), kn_ref[...])
        vo_ref[sl, :] = v_ref[sl, :].astype(F32)
        return c

    lax.fori_loop(0, seq // rows, prep, 0, unroll=True)
    kd_scr[0:blk, :] = jnp.zeros((blk, LANES), BF16)
    vd_scr[0:blk, :] = jnp.zeros((blk, 2 * LANES), BF16)
    vd_scr[blk:, LANES:] = jnp.ones((seq, LANES), BF16)

    qi = lax.broadcasted_iota(jnp.int32, (blk, blk), 0)
    kj = lax.broadcasted_iota(jnp.int32, (blk, blk), 1)
    cur_ok = kj <= qi
    prev_ok = kj >= qi
    band_ok = jnp.concatenate([prev_ok, cur_ok], axis=1)
    in_cur = lax.broadcasted_iota(jnp.int32, (blk, 2 * blk), 1) >= blk

    for g, (window, dil) in enumerate(BRANCHES):
        n_blk = seq // (dil * blk)
        col0 = 0 if n_blk > 1 else LANES

        def place(idx, dil=dil, n_blk=n_blk):
            res = idx // n_blk
            n = idx - res * n_blk
            start = res + n * (dil * blk)
            nat = pl.ds(start, blk, stride=dil) if dil > 1 else pl.ds(pl.multiple_of(start, blk), blk)
            cur = pl.ds(pl.multiple_of(idx * blk, blk), blk)
            kcur = pl.ds(pl.multiple_of((idx + 1) * blk, blk), blk)
            kwin = pl.ds(pl.multiple_of(idx * blk, blk), 2 * blk)
            return nat, cur, kcur, kwin, n

        keep_f32 = dil == BRANCHES[1][1]
        two_level = g == 2 and dil == BRANCHES[1][1] ** 2

        def gather(idx, c, place=place, dil=dil, n_blk=n_blk, keep_f32=keep_f32, two_level=two_level):
            nat, cur, kcur, _, n = place(idx)
            if two_level:
                mid = BRANCHES[1][1]
                res = idx // n_blk
                start = (res % mid) * (seq // mid) + res // mid + n * (mid * blk)
                src = pl.ds(start, blk, stride=mid)
                q, k, v = qf_scr[src, :], kf_scr[src, :], vf_scr[src, :]
            else:
                q, k, v = qs_scr[nat, :], ko_ref[nat, :], vo_ref[nat, :]
            if keep_f32:
                qf_scr[cur, :], kf_scr[cur, :], vf_scr[cur, :] = q, k, v
            qd_scr[cur, :] = q.astype(BF16)
            kd_scr[kcur, :] = k.astype(BF16)
            vd_scr[kcur, 0:LANES] = v.astype(BF16)
            return c

        lax.fori_loop(0, n_all, gather, 0, unroll=True)

        def scores(idx, c, place=place, n_blk=n_blk):
            _, cur, kcur, kwin, n = place(idx)
            q = qd_scr[cur, :]
            if n_blk > 1:
                ok = jnp.logical_and(band_ok, jnp.logical_or(in_cur, n > 0))
                s_scr[idx] = jnp.where(ok, _dot_nt(q, kd_scr[kwin, :]), NEG_INF)
            else:
                s_scr[idx, :, LANES:] = jnp.where(cur_ok, _dot_nt(q, kd_scr[kcur, :]), NEG_INF)
            return c

        lax.fori_loop(0, n_all, scores, 0, unroll=True)

        def softmax(idx, c, col0=col0):
            s = s_scr[idx, :, col0:]
            m = jnp.max(s, axis=-1, keepdims=True)
            p_scr[idx, :, col0:] = jnp.exp(s - m).astype(BF16)
            m_scr[idx] = jnp.broadcast_to(m, (blk, LANES))
            return c

        lax.fori_loop(0, n_all, softmax, 0, unroll=True)

        def values(idx, c, g=g, place=place, n_blk=n_blk):
            nat, _, kcur, kwin, _ = place(idx)
            if n_blk > 1:
                ov = _dot(p_scr[idx], vd_scr[kwin, :])
            else:
                ov = _dot(p_scr[idx, :, LANES:], vd_scr[kcur, :])
            l = ov[:, LANES:]
            o_scr[g, nat, :] = ov[:, 0:LANES] / l
            l_scr[g, nat, :] = m_scr[idx] + jnp.log(l)
            return c

        lax.fori_loop(0, n_all, values, 0, unroll=True)

    def comb(r, c):
        sl = pl.ds(pl.multiple_of(r * rows, rows), rows)
        out = _combine([o_scr[g, sl, :] for g in range(3)], [l_scr[g, sl, :] for g in range(3)])
        att_ref[sl, :] = out.astype(att_ref.dtype)
        return c

    lax.fori_loop(0, seq // rows, comb, 0, unroll=True)


def attn_prompt(z, qn, kn, *, n_heads):
    bsz, seq, _ = z.shape
    assert seq % (BRANCHES[-1][1] * ATT_BLOCK) == 0
    hd = LANES
    kern = functools.partial(_attn_prompt_kernel, seq=seq, scale=hd ** -0.5)
    blk = lambda off: pl.BlockSpec((None, seq, hd), lambda b, h: (b, 0, off + h))
    return pl.pallas_call(
        kern,
        grid=(bsz, n_heads),
        in_specs=[blk(0), blk(n_heads), blk(2 * n_heads),
                  pl.BlockSpec((1, hd), lambda b, h: (0, 0)), pl.BlockSpec((1, hd), lambda b, h: (0, 0))],
        out_specs=[blk(0), blk(0), blk(0)],
        out_shape=[
            jax.ShapeDtypeStruct((bsz, seq, n_heads * hd), BF16),
            jax.ShapeDtypeStruct((bsz, seq, n_heads * hd), F32),
            jax.ShapeDtypeStruct((bsz, seq, n_heads * hd), F32),
        ],
        scratch_shapes=[
            pltpu.VMEM((seq, hd), F32),
            pltpu.VMEM((seq, hd), F32), pltpu.VMEM((seq, hd), F32), pltpu.VMEM((seq, hd), F32),
            pltpu.VMEM((seq, hd), BF16), pltpu.VMEM((seq + ATT_BLOCK, hd), BF16),
            pltpu.VMEM((seq + ATT_BLOCK, 2 * hd), BF16),
            pltpu.VMEM((seq // ATT_BLOCK, ATT_BLOCK, 2 * ATT_BLOCK), F32),
            pltpu.VMEM((seq // ATT_BLOCK, ATT_BLOCK, 2 * ATT_BLOCK), BF16),
            pltpu.VMEM((seq // ATT_BLOCK, ATT_BLOCK, hd), F32),
            pltpu.VMEM((3, seq, hd), F32),
            pltpu.VMEM((3, seq, hd), F32),
        ],
        compiler_params=_params(("arbitrary", "arbitrary"), 40),
        name="attn_prompt",
    )(z, z, z, qn.reshape(1, hd), kn.reshape(1, hd))


SAMPLE_PAD = 16


def _attn_sample_kernel(q_ref, k_ref, v_ref, ck_ref, cv_ref, qn_ref, kn_ref, att_ref, ko_ref, vo_ref,
                        q_scr, kn_scr, vn_scr, s_scr, v_scr, o_scr, *, s_new, n_buf, n_heads, pc, scale):
    c = pl.program_id(1)
    n_ch = n_buf // pc
    pad = SAMPLE_PAD

    @pl.when(c == 0)
    def _():
        q_scr[...] = jnp.zeros_like(q_scr)
        kn_scr[...] = jnp.zeros_like(kn_scr)
        vn_scr[...] = jnp.zeros_like(vn_scr)
        for h in range(n_heads):
            lanes = slice(h * LANES, (h + 1) * LANES)
            k_new = _head_rms(k_ref[:, lanes].astype(F32), kn_ref[...])
            v_new = v_ref[:, lanes].astype(F32)
            ko_ref[:, lanes] = k_new
            vo_ref[:, lanes] = v_new
            q_scr[h, 0:s_new, :] = _head_rms(q_ref[:, lanes].astype(F32), qn_ref[...]) * scale
            kn_scr[h, 0:s_new, :] = k_new
            vn_scr[h, 0:s_new, :] = v_new

    def per_head(h, carry):
        rows = pl.ds(h, pc, stride=n_heads)
        s_scr[h, c] = _dot_nt(q_scr[h].astype(BF16), ck_ref[rows, :].astype(BF16))
        v_scr[h, pl.ds(pl.multiple_of(c * pc, pc), pc), :] = cv_ref[rows, :].astype(BF16)
        return carry

    lax.fori_loop(0, n_heads, per_head, 0, unroll=True)

    @pl.when(c == n_ch - 1)
    def _():
        qi = lax.broadcasted_iota(jnp.int32, (pad, pc), 0)
        kj = lax.broadcasted_iota(jnp.int32, (pad, pc), 1)
        qi_n = lax.broadcasted_iota(jnp.int32, (pad, pad), 0)
        kj_n = lax.broadcasted_iota(jnp.int32, (pad, pad), 1)
        dist_n = qi_n - kj_n
        new_ok = jnp.logical_and(dist_n >= 0, kj_n < s_new)

        def finish(h, carry):
            q = q_scr[h].astype(BF16)
            s_n = _dot_nt(q, kn_scr[h].astype(BF16))
            s_c = [s_scr[h, cc] for cc in range(n_ch)]
            ps, pns, ls, ms = [], [], [], []
            for window, dil in BRANCHES:
                msk = []
                for cc in range(n_ch):
                    dist = n_buf + qi - (cc * pc + kj)
                    ok = jnp.logical_and((dist & (dil - 1)) == 0, dist <= window)
                    msk.append(jnp.where(ok, s_c[cc], NEG_INF))
                m_n = jnp.where(jnp.logical_and(new_ok, (dist_n & (dil - 1)) == 0), s_n, NEG_INF)
                m = jnp.max(m_n, axis=-1, keepdims=True)
                for cc in range(n_ch):
                    m = jnp.maximum(m, jnp.max(msk[cc], axis=-1, keepdims=True))
                p_n = jnp.exp(m_n - m)
                l = jnp.sum(p_n, axis=-1, keepdims=True)
                pb = []
                for cc in range(n_ch):
                    p = jnp.exp(msk[cc] - m)
                    l = l + jnp.sum(p, axis=-1, keepdims=True)
                    pb.append(p.astype(BF16))
                ps.append(pb)
                pns.append(p_n.astype(BF16))
                ls.append(l)
                ms.append(m)
            ov = _dot(jnp.concatenate(pns, axis=0), vn_scr[h].astype(BF16))
            for cc in range(n_ch):
                ov = ov + _dot(jnp.concatenate([ps[g][cc] for g in range(3)], axis=0),
                               v_scr[h, cc * pc:(cc + 1) * pc, :])
            outs = [ov[g * pad:(g + 1) * pad] / ls[g] for g in range(3)]
            lses = [jnp.broadcast_to(ms[g] + jnp.log(ls[g]), (pad, LANES)) for g in range(3)]
            o_scr[h] = _combine(outs, lses)
            return carry

        lax.fori_loop(0, n_heads, finish, 0, unroll=4)
        for h in range(n_heads):
            att_ref[:, h * LANES:(h + 1) * LANES] = o_scr[h, 0:s_new, :].astype(att_ref.dtype)


def attn_sample(z, cache_k, cache_v, row0, qn, kn, *, n_heads):
    bsz, s_new, _ = z.shape
    hd = LANES
    d_c = n_heads * hd
    n_buf = cache_k.shape[1] // n_heads
    assert n_buf >= BRANCHES[-1][0] and s_new <= SAMPLE_PAD
    pc = min(n_buf, 512)
    kern = functools.partial(_attn_sample_kernel, s_new=s_new, n_buf=n_buf, n_heads=n_heads, pc=pc,
                             scale=hd ** -0.5)
    blk = lambda off: pl.BlockSpec((None, s_new, d_c), lambda b, c: (b, 0, off))
    cblk = pl.BlockSpec((None, pc * n_heads, hd), lambda b, c: (row0 + b, c, 0))
    vec = pl.BlockSpec((1, hd), lambda b, c: (0, 0))
    return pl.pallas_call(
        kern,
        grid=(bsz, n_buf // pc),
        in_specs=[blk(0), blk(1), blk(2), cblk, cblk, vec, vec],
        out_specs=[blk(0), blk(0), blk(0)],
        out_shape=[
            jax.ShapeDtypeStruct((bsz, s_new, d_c), BF16),
            jax.ShapeDtypeStruct((bsz, s_new, d_c), F32),
            jax.ShapeDtypeStruct((bsz, s_new, d_c), F32),
        ],
        scratch_shapes=[
            pltpu.VMEM((n_heads, SAMPLE_PAD, hd), F32),
            pltpu.VMEM((n_heads, SAMPLE_PAD, hd), F32),
            pltpu.VMEM((n_heads, SAMPLE_PAD, hd), F32),
            pltpu.VMEM((n_heads, n_buf // pc, SAMPLE_PAD, pc), F32),
            pltpu.VMEM((n_heads, n_buf, hd), BF16),
            pltpu.VMEM((n_heads, SAMPLE_PAD, hd), F32),
        ],
        compiler_params=_params(("arbitrary", "arbitrary"), 40),
        name="attn_sample",
    )(z, z, z, cache_k, cache_v, qn.reshape(1, hd), kn.reshape(1, hd))


def _sgu_kernel(gu_ref, gv_ref, lg_ref, lb_ref, w_ref, bt_ref, o_ref, vn_ref, vb_scr, *, rows, n_g, cd):
    t = w_ref.shape[1]
    gv = _gelu(gv_ref[...].astype(F32))
    mu = jnp.mean(gv, axis=-1, keepdims=True)
    xc = gv - mu
    var = jnp.mean(xc * xc, axis=-1, keepdims=True)
    vn = (xc * lax.rsqrt(var + LN_EPS)) * lg_ref[...] + lb_ref[...]
    vn_ref[...] = vn
    if rows < t:
        vb_scr[...] = jnp.zeros_like(vb_scr)
    vb_scr[0:rows, :] = vn.astype(BF16)
    ri = lax.broadcasted_iota(jnp.int32, (t, t), 0)
    ci = lax.broadcasted_iota(jnp.int32, (t, t), 1)
    for g in range(n_g):
        cols = slice(g * cd, (g + 1) * cd)
        wg = jnp.where(ri >= ci, w_ref[g], 0.0).astype(BF16)
        mixed = _dot(wg, vb_scr[:, cols])[0:rows, :] + bt_ref[:, g:g + 1]
        o_ref[:, cols] = (_gelu(gu_ref[:, cols].astype(F32)) * mixed).astype(o_ref.dtype)


def sgu(z, colblk_u, ln_g, ln_b, w_s, b_s):
    bsz, seq, _ = z.shape
    n_g = w_s.shape[0]
    dd = ln_g.shape[0]
    cd = dd // n_g
    t = min(seq, CHUNK)
    tp = max(t, LANES)
    w = jnp.pad(w_s[:, :t, :t], ((0, 0), (0, tp - t), (0, tp - t)))
    bt = jnp.transpose(b_s[:, :t])
    kern = functools.partial(_sgu_kernel, rows=t, n_g=n_g, cd=cd)
    return pl.pallas_call(
        kern,
        grid=(bsz, seq // t),
        in_specs=[
            pl.BlockSpec((None, t, dd), lambda b, c: (b, c, colblk_u)),
            pl.BlockSpec((None, t, dd), lambda b, c: (b, c, colblk_u + 1)),
            pl.BlockSpec((1, dd), lambda b, c: (0, 0)),
            pl.BlockSpec((1, dd), lambda b, c: (0, 0)),
            pl.BlockSpec((n_g, tp, tp), lambda b, c: (0, 0, 0)),
            pl.BlockSpec((t, n_g), lambda b, c: (0, 0)),
        ],
        out_specs=[
            pl.BlockSpec((None, t, dd), lambda b, c: (b, c, 0)),
            pl.BlockSpec((None, t, dd), lambda b, c: (b, c, 0)),
        ],
        out_shape=[
            jax.ShapeDtypeStruct((bsz, seq, dd), BF16),
            jax.ShapeDtypeStruct((bsz, seq, dd), F32),
        ],
        scratch_shapes=[pltpu.VMEM((tp, dd), BF16)],
        compiler_params=_params(("arbitrary", "arbitrary"), 40),
        name="sgu",
    )(z, z, ln_g.reshape(1, dd), ln_b.reshape(1, dd), w, bt)


def _in_proj(x, norm_g, w_in, i, tiles):
    if tiles["cast"]:
        return norm_matmul_cast(x, norm_g, w_in, i, tn=MIX_TILE)
    return norm_matmul(x, norm_g, w_in, tm=tiles["tm"], out_dtype=tiles["z_dtype"]), None


def _res_proj(x, a, b, w_out, i, tiles):
    if tiles["cast"]:
        return out_proj_cast(x, a, b, w_out, i, tn=MIX_TILE)
    return out_proj(x, a, b, w_out, tm=tiles["tm_out"]), None


def _even_layer(x, bsz, seq, h0_re, h0_im, pool_buf, start_pos, norm_g, w_in, w_out, i, s5p, pool_w, pool_scale,
                w_glu, b_glu, d_skip, tiles):
    t, d = x.shape
    pw_re, pw_im, b_blk, c_blk = s5p
    n_slab = b_blk.shape[0]
    d_a = n_slab * LANES
    z, w_in_b = _in_proj(x, norm_g, w_in, i, tiles)
    z = z.reshape(bsz, seq, -1)
    h0 = jnp.concatenate([h0_re.reshape(bsz, n_slab, 1, SLAB_STATE), h0_im.reshape(bsz, n_slab, 1, SLAB_STATE)], axis=-1)
    if tiles["s5_long"]:
        y_pre, h_last = s5_scan_long(z, 0, b_blk, c_blk, pw_re, pw_im, d_skip, h0)
    else:
        y_pre, h_last = s5_scan(z, 0, b_blk, c_blk, pw_re, pw_im, d_skip, h0, split_in=True)
    ya = glu(y_pre.reshape(t, d_a), w_glu, b_glu, i, tm=tiles["tm_glu"], tn=MIX_TILE)
    buf16 = jnp.pad(pool_buf, ((0, 0), (POOL_HIST - pool_buf.shape[1], 0), (0, 0)))
    yb, tail = pool(z, 1, buf16, pool_w, i, pool_scale, start_pos=start_pos, tc=256)
    x, w_out_b = _res_proj(x, ya, yb.reshape(t, -1), w_out, i, tiles)
    g_a = n_slab * SLAB_GROUPS
    h_re = h_last[..., :SLAB_STATE].reshape(bsz, g_a, S5_P)
    h_im = h_last[..., SLAB_STATE:].reshape(bsz, g_a, S5_P)
    return x, h_re, h_im, tail[:, POOL_HIST - pool_buf.shape[1]:], (w_in_b, w_out_b)


def _odd_layer(x, bsz, seq, k_buf, v_buf, norm_g, w_in, w_out, i, qn, kn, ln_g, ln_b, w_s, b_s, n_heads, tiles):
    t, d = x.shape
    d_c = n_heads * LANES
    z, w_in_b = _in_proj(x, norm_g, w_in, i, tiles)
    z = z.reshape(bsz, seq, -1)
    if k_buf is None:
        att, k_new, v_new = attn_prompt(z, qn, kn, n_heads=n_heads)
    else:
        att, k_new, v_new = attn_sample(z, k_buf, v_buf, i * bsz, qn, kn, n_heads=n_heads)
    dd = ln_g.shape[0]
    sg, vn = sgu(z, (3 * d_c) // dd, ln_g, ln_b, w_s, b_s)
    x, w_out_b = _res_proj(x, att.reshape(t, d_c), sg.reshape(t, dd), w_out, i, tiles)
    hd = LANES
    return x, k_new.reshape(bsz, seq, n_heads, hd), v_new.reshape(bsz, seq, n_heads, hd), vn, (w_in_b, w_out_b)


def kernel(x_prompt, x_sample, state_s5_re, state_s5_im, state_pool, cache_k, cache_v, norm_mix, norm_ffn, ev_w_in, ev_w_out, s5_lambda_re, s5_lambda_im, s5_log_dt, s5_b_re, s5_b_im, s5_c_re, s5_c_im, s5_d, s5_w_glu, s5_b_glu, pool_w, pool_scale, od_w_in, od_w_out, q_norm, k_norm, sgu_ln_g, sgu_ln_b, sgu_w, sgu_b, ffn_w1, ffn_w3, ffn_w2):
    bp, lp, d = x_prompt.shape
    bs, ls, _ = x_sample.shape
    depth = norm_mix.shape[0]
    n_heads = cache_k.shape[3]
    xp = x_prompt.reshape(bp * lp, d)
    xs = x_sample.reshape(bs * ls, d)
    tiles_p = dict(cast=False, tm=1024, tm_glu=512, tm_out=1024, s5_long=True, z_dtype=BF16)
    tiles_s = dict(cast=True, tm_glu=bs * ls, s5_long=False)
    g_a, p_a = s5_lambda_re.shape[1:]
    w_glu_b, pool_w_b = s5_w_glu.astype(BF16), pool_w.astype(BF16)
    d_c = n_heads * LANES
    cache_k2 = cache_k.reshape(-1, cache_k.shape[2] * n_heads, LANES)
    cache_v2 = cache_v.reshape(-1, cache_v.shape[2] * n_heads, LANES)

    s5r_p, s5i_p, pool_p, k_p, v_p = [], [], [], [], []
    s5r_s, s5i_s, pool_s, k_s, v_s, sgu_s = [], [], [], [], [], []
    for l in range(depth):
        i = l // 2
        if l % 2 == 0:
            pw_re, pw_im, bb_re, bb_im = s5_prep(s5_lambda_re[i], s5_lambda_im[i], s5_log_dt[i], s5_b_re[i], s5_b_im[i])
            b_blk, c_blk = s5_block_matrices(bb_re, bb_im, s5_c_re[i], s5_c_im[i])
            s5p = (pw_re, pw_im, b_blk, c_blk)
            rest = (i, s5p, pool_w_b, pool_scale[i], w_glu_b, s5_b_glu[i], s5_d[i])
            xs, hr, hi, buf, (w_in_b, w_out_b) = _even_layer(
                xs, bs, ls, state_s5_re[i], state_s5_im[i], state_pool[i], PAST_LEN, norm_mix[l], ev_w_in, ev_w_out,
                *rest, tiles_s)
            s5r_s.append(hr); s5i_s.append(hi); pool_s.append(buf)
            zero_h = jnp.zeros((bp, g_a, p_a), F32)
            zero_buf = jnp.zeros((bp, state_pool.shape[2], state_pool.shape[3]), F32)
            xp, hr, hi, buf, _ = _even_layer(xp, bp, lp, zero_h, zero_h, zero_buf, 0, norm_mix[l], w_in_b, w_out_b,
                                             *rest, tiles_p)
            s5r_p.append(hr); s5i_p.append(hi); pool_p.append(buf)
        else:
            rest = (i, q_norm[i], k_norm[i], sgu_ln_g[i], sgu_ln_b[i], sgu_w[i], sgu_b[i], n_heads)
            xs, nk, nv, vrows, (w_in_b, w_out_b) = _odd_layer(xs, bs, ls, cache_k2, cache_v2, norm_mix[l], od_w_in,
                                                              od_w_out, *rest, tiles_s)
            k_s.append(nk); v_s.append(nv); sgu_s.append(vrows)
            xp, nk, nv, _, _ = _odd_layer(xp, bp, lp, None, None, norm_mix[l], w_in_b, w_out_b, *rest, tiles_p)
            k_p.append(nk); v_p.append(nv)
        if l == 0:
            xs, *ffn_wb = ffn_cast(xs, norm_ffn[l], ffn_w1, ffn_w3, ffn_w2, l, tf=FFN_TILE)
        else:
            xs = ffn(xs, norm_ffn[l], *ffn_wb, tm=bs * ls)
        if l + 1 < depth:
            xp, ffn_wb = ffn(xp, norm_ffn[l], *ffn_wb, tm=512, nxt=(ffn_w1, ffn_w3, ffn_w2, l + 1))
        else:
            xp = ffn(xp, norm_ffn[l], *ffn_wb, tm=512)
    return (xp.reshape(bp, lp, d), xs.reshape(bs, ls, d),
            jnp.stack(s5r_p), jnp.stack(s5i_p), jnp.stack(pool_p), jnp.stack(k_p), jnp.stack(v_p),
            jnp.stack(s5r_s), jnp.stack(s5i_s), jnp.stack(pool_s), jnp.stack(k_s), jnp.stack(v_s),
            jnp.stack(sgu_s))
```

```python
import functools
import math

import jax
import jax.numpy as jnp
from jax import lax
from jax.experimental import pallas as pl
from jax.experimental.pallas import tpu as pltpu

F32 = jnp.float32
BF16 = jnp.bfloat16

RMS_EPS = 1e-6
LN_EPS = 1e-5
NEG_INF = -1e30

LANES = 128
SUBLANES = 8
ATT_BLOCK = 128
CHUNK = 128
POOL_WINDOWS = (2, 4, 8, 16)
POOL_HIST = 16
BRANCHES = ((128, 1), (512, 4), (2048, 16))
S5_GRP = 16
S5_P = 64
SLAB_GROUPS = LANES // S5_GRP
SLAB_STATE = SLAB_GROUPS * S5_P
PAST_LEN = 8192
FFN_TILE = 256
MIX_TILE = 512


def _params(sem, vmem_mib):
    return pltpu.CompilerParams(dimension_semantics=sem, vmem_limit_bytes=vmem_mib << 20)


def _gelu(x):
    return 0.5 * x * (1.0 + lax.erf(x * (1.0 / math.sqrt(2.0))))


def _sigmoid(x):
    return 1.0 / (1.0 + jnp.exp(-x))


def _split_bf16(a):
    hi = a.astype(BF16)
    lo = (a - hi.astype(F32)).astype(BF16)
    return hi, lo


def _dot(a, b):
    return jnp.dot(a, b, preferred_element_type=F32)


def _dot_nt(a, b):
    return lax.dot_general(a, b, (((1,), (1,)), ((), ())), preferred_element_type=F32)


def _rms_rows_to(x_ref, g_ref, h_ref, rows):
    step = 16 if rows % 16 == 0 else rows

    def body(r, c):
        sl = pl.ds(pl.multiple_of(r * step, step), step)
        x = x_ref[sl, :]
        ms = jnp.mean(x * x, axis=-1, keepdims=True)
        h_ref[sl, :] = ((x * lax.rsqrt(ms + RMS_EPS)) * g_ref[...]).astype(h_ref.dtype)
        return c

    lax.fori_loop(0, rows // step, body, 0, unroll=min(4, rows // step))


def _norm_matmul_kernel(x_ref, g_ref, w_ref, o_ref, h_ref):
    @pl.when(pl.program_id(1) == 0)
    def _():
        _rms_rows_to(x_ref, g_ref, h_ref, x_ref.shape[0])

    o_ref[...] = _dot(h_ref[...], w_ref[...]).astype(o_ref.dtype)


def norm_matmul(x, g, w, *, tm, out_dtype=F32):
    t, d = x.shape
    n_t, _, tn = w.shape
    return pl.pallas_call(
        _norm_matmul_kernel,
        grid=(t // tm, n_t),
        in_specs=[
            pl.BlockSpec((tm, d), lambda i, j: (i, 0)),
            pl.BlockSpec((1, d), lambda i, j: (0, 0)),
            pl.BlockSpec((None, d, tn), lambda i, j: (j, 0, 0)),
        ],
        out_specs=pl.BlockSpec((tm, tn), lambda i, j: (i, j)),
        out_shape=jax.ShapeDtypeStruct((t, n_t * tn), out_dtype),
        scratch_shapes=[pltpu.VMEM((tm, d), BF16)],
        compiler_params=_params(("arbitrary", "arbitrary"), 58),
        name="norm_matmul",
    )(x, g.reshape(1, d), w)


def _norm_matmul_cast_kernel(x_ref, g_ref, w_ref, o_ref, wb_ref, h_ref):
    @pl.when(pl.program_id(0) == 0)
    def _():
        _rms_rows_to(x_ref, g_ref, h_ref, x_ref.shape[0])

    wb_ref[...] = w_ref[...].astype(BF16)
    o_ref[...] = _dot(h_ref[...], wb_ref[...]).astype(o_ref.dtype)


def norm_matmul_cast(x, g, w, layer, *, tn):
    t, d = x.shape
    n = w.shape[2]
    return pl.pallas_call(
        _norm_matmul_cast_kernel,
        grid=(n // tn,),
        in_specs=[
            pl.BlockSpec((t, d), lambda j: (0, 0)),
            pl.BlockSpec((1, d), lambda j: (0, 0)),
            pl.BlockSpec((None, d, tn), lambda j: (layer, 0, j)),
        ],
        out_specs=[
            pl.BlockSpec((t, tn), lambda j: (0, j)),
            pl.BlockSpec((None, d, tn), lambda j: (j, 0, 0)),
        ],
        out_shape=[
            jax.ShapeDtypeStruct((t, n), F32),
            jax.ShapeDtypeStruct((n // tn, d, tn), BF16),
        ],
        scratch_shapes=[pltpu.VMEM((t, d), BF16)],
        compiler_params=_params(("arbitrary",), 48),
        name="norm_matmul_cast",
    )(x, g.reshape(1, d), w)


FFN_OUT_CHUNK = 512


def _ffn_step(first, x_ref, g_ref, w1_ref, w3_ref, w2_ref, o_ref, h_ref):
    @pl.when(first)
    def _():
        _rms_rows_to(x_ref, g_ref, h_ref, x_ref.shape[0])
        o_ref[...] = x_ref[...]

    h = h_ref[...]
    a = _dot(h, w1_ref[...])
    b = _dot(h, w3_ref[...])
    u = ((a * _sigmoid(a)) * b).astype(BF16)
    for c in range(0, o_ref.shape[1], FFN_OUT_CHUNK):
        o_ref[:, c:c + FFN_OUT_CHUNK] += _dot(u, w2_ref[:, c:c + FFN_OUT_CHUNK])


def _ffn_kernel(x_ref, g_ref, w1_ref, w3_ref, w2_ref, o_ref, h_ref):
    _ffn_step(pl.program_id(1) == 0, x_ref, g_ref, w1_ref, w3_ref, w2_ref, o_ref, h_ref)


def _ffn_next_kernel(x_ref, g_ref, w1_ref, w3_ref, w2_ref, n1_ref, n3_ref, n2_ref,
                     o_ref, c1_ref, c3_ref, c2_ref, h_ref):
    c1_ref[...] = n1_ref[...].astype(BF16)
    c3_ref[...] = n3_ref[...].astype(BF16)
    c2_ref[...] = n2_ref[...].astype(BF16)
    _ffn_step(pl.program_id(1) == 0, x_ref, g_ref, w1_ref, w3_ref, w2_ref, o_ref, h_ref)


def ffn(x, g, w1, w3, w2, *, tm, nxt=None):
    t, d = x.shape
    n_f, _, tf = w1.shape
    n_m = t // tm
    specs = [
        pl.BlockSpec((tm, d), lambda i, j: (i, 0)),
        pl.BlockSpec((1, d), lambda i, j: (0, 0)),
        pl.BlockSpec((None, d, tf), lambda i, j: (j, 0, 0)),
        pl.BlockSpec((None, d, tf), lambda i, j: (j, 0, 0)),
        pl.BlockSpec((tf, d), lambda i, j: (j, 0)),
    ]
    out_spec = pl.BlockSpec((tm, d), lambda i, j: (i, 0))
    out_shape = jax.ShapeDtypeStruct((t, d), F32)
    common = dict(grid=(n_m, n_f), scratch_shapes=[pltpu.VMEM((tm, d), BF16)],
                  compiler_params=_params(("arbitrary", "arbitrary"), 56))
    if nxt is None:
        return pl.pallas_call(_ffn_kernel, in_specs=specs, out_specs=out_spec, out_shape=out_shape, name="ffn",
                              **common)(x, g.reshape(1, d), w1, w3, w2)
    n1, n3, n2, layer = nxt
    piece = d // n_m
    assert d % n_m == 0 and piece % LANES == 0
    col = pl.BlockSpec((None, piece, tf), lambda i, j: (layer, i, j))
    row = pl.BlockSpec((None, tf, piece), lambda i, j: (layer, j, i))
    col_out = pl.BlockSpec((None, piece, tf), lambda i, j: (j, i, 0))
    row_out = pl.BlockSpec((tf, piece), lambda i, j: (j, i))
    out, c1, c3, c2 = pl.pallas_call(
        _ffn_next_kernel,
        in_specs=specs + [col, col, row],
        out_specs=[out_spec, col_out, col_out, row_out],
        out_shape=[out_shape, jax.ShapeDtypeStruct(w1.shape, BF16), jax.ShapeDtypeStruct(w3.shape, BF16),
                   jax.ShapeDtypeStruct(w2.shape, BF16)],
        name="ffn_next", **common,
    )(x, g.reshape(1, d), w1, w3, w2, n1, n3, n2)
    return out, (c1, c3, c2)


def _ffn_cast_kernel(x_ref, g_ref, w1_ref, w3_ref, w2_ref, o_ref, w1b_ref, w3b_ref, w2b_ref, h_ref):
    w1b_ref[...] = w1_ref[...].astype(BF16)
    w3b_ref[...] = w3_ref[...].astype(BF16)
    w2b_ref[...] = w2_ref[...].astype(BF16)
    _ffn_step(pl.program_id(0) == 0, x_ref, g_ref, w1b_ref, w3b_ref, w2b_ref, o_ref, h_ref)


def ffn_cast(x, g, w1, w3, w2, layer, *, tf):
    t, d = x.shape
    f = w1.shape[2]
    return pl.pallas_call(
        _ffn_cast_kernel,
        grid=(f // tf,),
        in_specs=[
            pl.BlockSpec((t, d), lambda j: (0, 0)),
            pl.BlockSpec((1, d), lambda j: (0, 0)),
            pl.BlockSpec((None, d, tf), lambda j: (layer, 0, j)),
            pl.BlockSpec((None, d, tf), lambda j: (layer, 0, j)),
            pl.BlockSpec((None, tf, d), lambda j: (layer, j, 0)),
        ],
        out_specs=[
            pl.BlockSpec((t, d), lambda j: (0, 0)),
            pl.BlockSpec((None, d, tf), lambda j: (j, 0, 0)),
            pl.BlockSpec((None, d, tf), lambda j: (j, 0, 0)),
            pl.BlockSpec((tf, d), lambda j: (j, 0)),
        ],
        out_shape=[
            jax.ShapeDtypeStruct((t, d), F32),
            jax.ShapeDtypeStruct((f // tf, d, tf), BF16),
            jax.ShapeDtypeStruct((f // tf, d, tf), BF16),
            jax.ShapeDtypeStruct((f, d), BF16),
        ],
        scratch_shapes=[pltpu.VMEM((t, d), BF16)],
        compiler_params=_params(("arbitrary",), 48),
        name="ffn_cast",
    )(x, g.reshape(1, d), w1, w3, w2)


def _out_proj_kernel(x_ref, a_ref, b_ref, wa_ref, wb_ref, o_ref):
    o_ref[...] = x_ref[...] + _dot(a_ref[...], wa_ref[...]) + _dot(b_ref[...], wb_ref[...])


def out_proj(x, a, b, w_pair, *, tm):
    t, d = x.shape
    wa, wb = w_pair
    n_t, k, tn = wa.shape
    assert a.shape[1] == k and b.shape[1] == k and wb.shape == wa.shape
    return pl.pallas_call(
        _out_proj_kernel,
        grid=(t // tm, n_t),
        in_specs=[
            pl.BlockSpec((tm, tn), lambda i, j: (i, j)),
            pl.BlockSpec((tm, k), lambda i, j: (i, 0)),
            pl.BlockSpec((tm, k), lambda i, j: (i, 0)),
            pl.BlockSpec((None, k, tn), lambda i, j: (j, 0, 0)),
            pl.BlockSpec((None, k, tn), lambda i, j: (j, 0, 0)),
        ],
        out_specs=pl.BlockSpec((tm, tn), lambda i, j: (i, j)),
        out_shape=jax.ShapeDtypeStruct((t, d), F32),
        compiler_params=_params(("arbitrary", "arbitrary"), 48),
        name="out_proj",
    )(x, a, b, wa, wb)


def _out_proj_cast_kernel(x_ref, a_ref, b_ref, wa_ref, wb_ref, o_ref, wab_ref, wbb_ref):
    wab_ref[...] = wa_ref[...].astype(BF16)
    wbb_ref[...] = wb_ref[...].astype(BF16)
    o_ref[...] = x_ref[...] + _dot(a_ref[...], wab_ref[...]) + _dot(b_ref[...], wbb_ref[...])


def out_proj_cast(x, a, b, w, layer, *, tn):
    t, d = x.shape
    k = a.shape[1]
    assert b.shape[1] == k and w.shape[1] == 2 * k
    out, wa_b, wb_b = pl.pallas_call(
        _out_proj_cast_kernel,
        grid=(d // tn,),
        in_specs=[
            pl.BlockSpec((t, tn), lambda j: (0, j)),
            pl.BlockSpec((t, k), lambda j: (0, 0)),
            pl.BlockSpec((t, k), lambda j: (0, 0)),
            pl.BlockSpec((None, k, tn), lambda j: (layer, 0, j)),
            pl.BlockSpec((None, k, tn), lambda j: (layer, 1, j)),
        ],
        out_specs=[
            pl.BlockSpec((t, tn), lambda j: (0, j)),
            pl.BlockSpec((None, k, tn), lambda j: (j, 0, 0)),
            pl.BlockSpec((None, k, tn), lambda j: (j, 0, 0)),
        ],
        out_shape=[
            jax.ShapeDtypeStruct((t, d), F32),
            jax.ShapeDtypeStruct((d // tn, k, tn), BF16),
            jax.ShapeDtypeStruct((d // tn, k, tn), BF16),
        ],
        compiler_params=_params(("arbitrary",), 48),
        name="out_proj_cast",
    )(x, a, b, w, w)
    return out, (wa_b, wb_b)


def _s5_prep_kernel(lr_ref, li_ref, ldt_ref, lrx_ref, lix_ref, ldtx_ref, br_ref, bi_ref,
                    pwr_ref, pwi_ref, bbr_ref, bbi_ref):
    dt = jnp.exp(ldt_ref[...])
    mag = jnp.exp(lr_ref[...] * dt)
    ang = li_ref[...] * dt
    p_r, p_i = mag * jnp.cos(ang), mag * jnp.sin(ang)
    c_r, c_i = p_r, p_i
    pwr_ref[0], pwi_ref[0] = c_r, c_i
    for j in range(1, SUBLANES):
        c_r, c_i = c_r * p_r - c_i * p_i, c_r * p_i + c_i * p_r
        pwr_ref[j], pwi_ref[j] = c_r, c_i
    lr, li = lrx_ref[...], lix_ref[...]
    dtx = jnp.exp(ldtx_ref[...])
    magx = jnp.exp(lr * dtx)
    angx = li * dtx
    nr, ni = magx * jnp.cos(angx) - 1.0, magx * jnp.sin(angx)
    den = lr * lr + li * li
    qr = (nr * lr + ni * li) / den
    qi = (ni * lr - nr * li) / den
    br, bi = br_ref[...], bi_ref[...]
    bbr_ref[...] = qr * br - qi * bi
    bbi_ref[...] = qr * bi + qi * br


def s5_prep(lam_re, lam_im, log_dt, b_re, b_im):
    g, p = lam_re.shape
    h = b_re.shape[2]
    n_slab = g // SLAB_GROUPS
    slab = lambda a: a.reshape(n_slab, SLAB_GROUPS * p)
    rep = lambda a: jnp.repeat(a, h, axis=1)
    ldt_gp = jnp.broadcast_to(log_dt[:, None], (g, p))
    outs = pl.pallas_call(
        _s5_prep_kernel,
        out_shape=[
            jax.ShapeDtypeStruct((SUBLANES, n_slab, SLAB_GROUPS * p), F32),
            jax.ShapeDtypeStruct((SUBLANES, n_slab, SLAB_GROUPS * p), F32),
            jax.ShapeDtypeStruct((g, p * h), F32),
            jax.ShapeDtypeStruct((g, p * h), F32),
        ],
        name="s5_prep",
    )(slab(lam_re), slab(lam_im), slab(ldt_gp), rep(lam_re), rep(lam_im), rep(ldt_gp),
      b_re.reshape(g, p * h), b_im.reshape(g, p * h))
    pw_re, pw_im, bb_re, bb_im = outs
    pw_re = jnp.transpose(pw_re, (1, 0, 2))
    pw_im = jnp.transpose(pw_im, (1, 0, 2))
    return pw_re, pw_im, bb_re.reshape(g, p, h), bb_im.reshape(g, p, h)


def s5_block_matrices(bb_re, bb_im, c_re, c_im):
    g, p, h = bb_re.shape
    n_slab = g // SLAB_GROUPS
    eye = jnp.eye(SLAB_GROUPS, dtype=F32)

    def in_map(bb):
        t = bb.reshape(n_slab, SLAB_GROUPS, p, h)
        return jnp.einsum("kgph,gj->kghjp", t, eye).reshape(n_slab, SLAB_GROUPS * h, SLAB_GROUPS * p)

    def out_map(c):
        t = c.reshape(n_slab, SLAB_GROUPS, h, p)
        return jnp.einsum("kghp,gj->kgpjh", t, eye).reshape(n_slab, SLAB_GROUPS * p, SLAB_GROUPS * h)

    b_blk = jnp.concatenate([in_map(bb_re), in_map(bb_im)], axis=2)
    c_blk = jnp.concatenate([out_map(c_re), -out_map(c_im)], axis=1)
    return b_blk, c_blk


def _s5_scan_kernel(u_ref, bblk_ref, cblk_ref, pwr_ref, pwi_ref, d_ref, h0_ref,
                    y_ref, hl_ref, h_scr, bh_scr, bl_scr, ch_scr, *, seq, row_chunk, split_in):
    ns = SLAB_STATE

    @pl.when(pl.program_id(1) == 0)
    def _():
        bh, bl = _split_bf16(bblk_ref[...])
        bh_scr[...], bl_scr[...] = bh, bl
        ch_scr[...] = cblk_ref[...].astype(BF16)

    n_chunks = seq // row_chunk

    def proj_in(r, c):
        sl = pl.ds(pl.multiple_of(r * row_chunk, row_chunk), row_chunk)
        if split_in:
            uh, ul = _split_bf16(u_ref[sl, :].astype(F32))
            h_scr[sl, :] = _dot(uh, bh_scr[...]) + _dot(ul, bh_scr[...]) + _dot(uh, bl_scr[...])
        else:
            h_scr[sl, :] = _dot(u_ref[sl, :].astype(BF16), bh_scr[...])
        return c

    lax.fori_loop(0, n_chunks, proj_in, 0)

    rowid = lax.broadcasted_iota(jnp.int32, (SUBLANES, LANES), 0)
    for c in range(ns // LANES):
        re_l = slice(c * LANES, (c + 1) * LANES)
        im_l = slice(ns + c * LANES, ns + (c + 1) * LANES)
        p_r, p_i = pwr_ref[:, re_l], pwi_ref[:, re_l]
        steps = []
        for dist in (1, 2, 4):
            a_r = jnp.where(rowid >= dist, jnp.broadcast_to(p_r[dist - 1:dist], (SUBLANES, LANES)), 0.0)
            a_i = jnp.where(rowid >= dist, jnp.broadcast_to(p_i[dist - 1:dist], (SUBLANES, LANES)), 0.0)
            steps.append((dist, a_r, a_i))
        c_r = jnp.broadcast_to(h0_ref[:, re_l], (SUBLANES, LANES))
        c_i = jnp.broadcast_to(h0_ref[:, im_l], (SUBLANES, LANES))

        last = slice(SUBLANES - 1, SUBLANES)
        full = (SUBLANES, LANES)
        p8_r, p8_i = jnp.broadcast_to(p_r[last], full), jnp.broadcast_to(p_i[last], full)
        n_groups = seq // SUBLANES
        per_it = min(4, n_groups)

        def scan_rows(it, carry, re_l=re_l, im_l=im_l, p_r=p_r, p_i=p_i, p8_r=p8_r, p8_i=p8_i, steps=steps):
            base = pl.multiple_of(it * (per_it * SUBLANES), per_it * SUBLANES)
            sls = [pl.ds(base + j * SUBLANES, SUBLANES) for j in range(per_it)]
            loc = []
            for sl in sls:
                r, i = h_scr[sl, re_l], h_scr[sl, im_l]
                for dist, a_r, a_i in steps:
                    s_r, s_i = pltpu.roll(r, dist, 0), pltpu.roll(i, dist, 0)
                    r, i = r + (s_r * a_r - s_i * a_i), i + (s_r * a_i + s_i * a_r)
                loc.append((r, i))
            c_r, c_i = carry
            outs = []
            for r, i in loc:
                outs.append((r + (c_r * p_r - c_i * p_i), i + (c_r * p_i + c_i * p_r)))
                e_r, e_i = jnp.broadcast_to(r[last], full), jnp.broadcast_to(i[last], full)
                c_r, c_i = e_r + (c_r * p8_r - c_i * p8_i), e_i + (c_r * p8_i + c_i * p8_r)
            for sl, (r, i) in zip(sls, outs):
                h_scr[sl, re_l], h_scr[sl, im_l] = r, i
            return c_r, c_i

        c_r, c_i = lax.fori_loop(0, n_groups // per_it, scan_rows, (c_r, c_i))
        hl_ref[:, re_l] = c_r[0:1]
        hl_ref[:, im_l] = c_i[0:1]

    def proj_out(r, c):
        sl = pl.ds(pl.multiple_of(r * row_chunk, row_chunk), row_chunk)
        y = _dot(h_scr[sl, :].astype(BF16), ch_scr[...]) + d_ref[...] * u_ref[sl, :].astype(F32)
        y_ref[sl, :] = _gelu(y)
        return c

    lax.fori_loop(0, n_chunks, proj_out, 0)


def s5_scan(z, col0, b_blk, c_blk, pw_re, pw_im, d_skip, h0, *, split_in):
    bsz, seq, _ = z.shape
    n_slab = b_blk.shape[0]
    ns2 = 2 * SLAB_STATE
    row_chunk = min(seq, 1024)
    kern = functools.partial(_s5_scan_kernel, seq=seq, row_chunk=row_chunk, split_in=split_in)
    return pl.pallas_call(
        kern,
        grid=(n_slab, bsz),
        in_specs=[
            pl.BlockSpec((None, seq, LANES), lambda k, b: (b, 0, col0 + k)),
            pl.BlockSpec((None, LANES, ns2), lambda k, b: (k, 0, 0)),
            pl.BlockSpec((None, ns2, LANES), lambda k, b: (k, 0, 0)),
            pl.BlockSpec((None, SUBLANES, SLAB_STATE), lambda k, b: (k, 0, 0)),
            pl.BlockSpec((None, SUBLANES, SLAB_STATE), lambda k, b: (k, 0, 0)),
            pl.BlockSpec((1, LANES), lambda k, b: (0, k)),
            pl.BlockSpec((None, None, 1, ns2), lambda k, b: (b, k, 0, 0)),
        ],
        out_specs=[
            pl.BlockSpec((None, seq, LANES), lambda k, b: (b, 0, k)),
            pl.BlockSpec((None, None, 1, ns2), lambda k, b: (b, k, 0, 0)),
        ],
        out_shape=[
            jax.ShapeDtypeStruct((bsz, seq, n_slab * LANES), F32),
            jax.ShapeDtypeStruct((bsz, n_slab, 1, ns2), F32),
        ],
        scratch_shapes=[
            pltpu.VMEM((seq, ns2), F32),
            pltpu.VMEM((LANES, ns2), BF16), pltpu.VMEM((LANES, ns2), BF16),
            pltpu.VMEM((ns2, LANES), BF16),
        ],
        compiler_params=_params(("arbitrary", "arbitrary"), 40),
        name="s5_scan",
    )(z, b_blk, c_blk, pw_re, pw_im, d_skip.reshape(1, -1), h0)


def _s5_seg_kernel(u_ref, bblk_ref, cblk_ref, pwr_ref, pwi_ref, d_ref, h0_ref,
                   y_ref, hl_ref, up_scr, h_scr, yp_scr, w_scr, bh_scr, ch_scr, *, seq, row_chunk):
    ns = SLAB_STATE
    nseg = SUBLANES
    seg = seq // nseg
    n_lb = ns // LANES
    full = (SUBLANES, LANES)
    lanes = [(slice(c * LANES, (c + 1) * LANES), slice(ns + c * LANES, ns + (c + 1) * LANES)) for c in range(n_lb)]
    last = slice(SUBLANES - 1, SUBLANES)

    @pl.when(pl.program_id(1) == 0)
    def _():
        bh_scr[...] = bblk_ref[...].astype(BF16)
        ch_scr[...] = cblk_ref[...].astype(BF16)
        for re_l, im_l in lanes:
            p_r, p_i = pwr_ref[:, re_l], pwi_ref[:, re_l]
            p8_r, p8_i = jnp.broadcast_to(p_r[last], full), jnp.broadcast_to(p_i[last], full)
            w_scr[0:SUBLANES, re_l], w_scr[0:SUBLANES, im_l] = p_r, p_i

            def grow(gi, carry, re_l=re_l, im_l=im_l, p8_r=p8_r, p8_i=p8_i):
                w_r, w_i = carry
                w_r, w_i = w_r * p8_r - w_i * p8_i, w_r * p8_i + w_i * p8_r
                sl = pl.ds(pl.multiple_of(gi * SUBLANES, SUBLANES), SUBLANES)
                w_scr[sl, re_l], w_scr[sl, im_l] = w_r, w_i
                return w_r, w_i

            lax.fori_loop(1, seg // SUBLANES, grow, (p_r, p_i))

    for s in range(nseg):
        up_scr[pl.ds(s, seg, stride=nseg), :] = u_ref[s * seg:(s + 1) * seg, :].astype(F32)

    def proj_in(r, c):
        sl = pl.ds(pl.multiple_of(r * row_chunk, row_chunk), row_chunk)
        h_scr[sl, :] = _dot(up_scr[sl, :].astype(BF16), bh_scr[...])
        return c

    lax.fori_loop(0, seq // row_chunk, proj_in, 0, unroll=True)

    lam = [(jnp.broadcast_to(pwr_ref[0:1, re_l], full), jnp.broadcast_to(pwi_ref[0:1, re_l], full))
           for re_l, _ in lanes]
    per_it = 2

    def scan_t(it, carry):
        base = pl.multiple_of(it * (per_it * SUBLANES), per_it * SUBLANES)
        sls = [pl.ds(base + j * SUBLANES, SUBLANES) for j in range(per_it)]
        bu = [[(h_scr[sl, re_l], h_scr[sl, im_l]) for re_l, im_l in lanes] for sl in sls]
        hs, outs = list(carry), []
        for j in range(per_it):
            hs = [(bu[j][c][0] + (hs[c][0] * lam[c][0] - hs[c][1] * lam[c][1]),
                   bu[j][c][1] + (hs[c][0] * lam[c][1] + hs[c][1] * lam[c][0])) for c in range(n_lb)]
            outs.append(hs)
        for sl, row in zip(sls, outs):
            for (re_l, im_l), (h_r, h_i) in zip(lanes, row):
                h_scr[sl, re_l], h_scr[sl, im_l] = h_r, h_i
        return tuple(hs)

    zero = jnp.zeros(full, F32)
    ends = lax.fori_loop(0, seg // per_it, scan_t, tuple((zero, zero) for _ in range(n_lb)), unroll=True)

    rowid = lax.broadcasted_iota(jnp.int32, full, 0)
    enter = []
    for (re_l, im_l), (e_r, e_i) in zip(lanes, ends):
        ws_r, ws_i = w_scr[seg - 1:seg, re_l], w_scr[seg - 1:seg, im_l]
        c_r, c_i = h0_ref[:, re_l], h0_ref[:, im_l]
        cv_r, cv_i = jnp.broadcast_to(c_r, full), jnp.broadcast_to(c_i, full)
        for s in range(1, nseg + 1):
            c_r, c_i = (e_r[s - 1:s] + (c_r * ws_r - c_i * ws_i), e_i[s - 1:s] + (c_r * ws_i + c_i * ws_r))
            if s < nseg:
                cv_r = jnp.where(rowid == s, jnp.broadcast_to(c_r, full), cv_r)
                cv_i = jnp.where(rowid == s, jnp.broadcast_to(c_i, full), cv_i)
        hl_ref[:, re_l], hl_ref[:, im_l] = c_r, c_i
        enter.append((cv_r, cv_i))

    def fix_t(gi, c):
        wsl = pl.ds(pl.multiple_of(gi * SUBLANES, SUBLANES), SUBLANES)
        wv = [(w_scr[wsl, re_l], w_scr[wsl, im_l]) for re_l, im_l in lanes]
        base = pl.multiple_of(gi * (SUBLANES * SUBLANES), SUBLANES * SUBLANES)
        for j in range(SUBLANES):
            sl = pl.ds(base + j * SUBLANES, SUBLANES)
            for (re_l, im_l), (cv_r, cv_i), (wv_r, wv_i) in zip(lanes, enter, wv):
                w_r = jnp.broadcast_to(wv_r[j:j + 1], full)
                w_i = jnp.broadcast_to(wv_i[j:j + 1], full)
                h_r = h_scr[sl, re_l] + (w_r * cv_r - w_i * cv_i)
                h_i = h_scr[sl, im_l] + (w_r * cv_i + w_i * cv_r)
                h_scr[sl, re_l], h_scr[sl, im_l] = h_r, h_i
        return c

    lax.fori_loop(0, seg // SUBLANES, fix_t, 0, unroll=True)

    def proj_out(r, c):
        sl = pl.ds(pl.multiple_of(r * row_chunk, row_chunk), row_chunk)
        y = _dot(h_scr[sl, :].astype(BF16), ch_scr[...]) + d_ref[...] * up_scr[sl, :]
        yp_scr[sl, :] = _gelu(y)
        return c

    lax.fori_loop(0, seq // row_chunk, proj_out, 0, unroll=True)

    for s in range(nseg):
        y_ref[s * seg:(s + 1) * seg, :] = yp_scr[pl.ds(s, seg, stride=nseg), :]


def s5_scan_long(z, col0, b_blk, c_blk, pw_re, pw_im, d_skip, h0):
    bsz, seq, _ = z.shape
    n_slab = b_blk.shape[0]
    ns2 = 2 * SLAB_STATE
    row_chunk = min(seq, 1024)
    assert seq % row_chunk == 0 and seq % (2 * SUBLANES * SUBLANES) == 0
    kern = functools.partial(_s5_seg_kernel, seq=seq, row_chunk=row_chunk)
    return pl.pallas_call(
        kern,
        grid=(n_slab, bsz),
        in_specs=[
            pl.BlockSpec((None, seq, LANES), lambda k, b: (b, 0, col0 + k)),
            pl.BlockSpec((None, LANES, ns2), lambda k, b: (k, 0, 0)),
            pl.BlockSpec((None, ns2, LANES), lambda k, b: (k, 0, 0)),
            pl.BlockSpec((None, SUBLANES, SLAB_STATE), lambda k, b: (k, 0, 0)),
            pl.BlockSpec((None, SUBLANES, SLAB_STATE), lambda k, b: (k, 0, 0)),
            pl.BlockSpec((1, LANES), lambda k, b: (0, k)),
            pl.BlockSpec((None, None, 1, ns2), lambda k, b: (b, k, 0, 0)),
        ],
        out_specs=[
            pl.BlockSpec((None, seq, LANES), lambda k, b: (b, 0, k)),
            pl.BlockSpec((None, None, 1, ns2), lambda k, b: (b, k, 0, 0)),
        ],
        out_shape=[
            jax.ShapeDtypeStruct((bsz, seq, n_slab * LANES), F32),
            jax.ShapeDtypeStruct((bsz, n_slab, 1, ns2), F32),
        ],
        scratch_shapes=[
            pltpu.VMEM((seq, LANES), F32),
            pltpu.VMEM((seq, ns2), F32),
            pltpu.VMEM((seq, LANES), F32),
            pltpu.VMEM((seq // SUBLANES, ns2), F32),
            pltpu.VMEM((LANES, ns2), BF16),
            pltpu.VMEM((ns2, LANES), BF16),
        ],
        compiler_params=_params(("arbitrary", "arbitrary"), 40),
        name="s5_scan_long",
    )(z, b_blk, c_blk, pw_re, pw_im, d_skip.reshape(1, -1), h0)


def _glu_kernel(yk_ref, yj_ref, w_ref, b_ref, o_ref, yb_scr):
    @pl.when(pl.program_id(1) == 0)
    def _():
        yb_scr[...] = yk_ref[...].astype(BF16)

    gate = _dot(yb_scr[...], w_ref[...]) + b_ref[...]
    o_ref[...] = (yj_ref[...] * _sigmoid(gate)).astype(o_ref.dtype)


def glu(y, w, b, layer, *, tm, tn):
    t, d = y.shape
    tm = min(tm, t)
    return pl.pallas_call(
        _glu_kernel,
        grid=(t // tm, d // tn),
        in_specs=[
            pl.BlockSpec((tm, d), lambda i, j: (i, 0)),
            pl.BlockSpec((tm, tn), lambda i, j: (i, j)),
            pl.BlockSpec((None, d, tn), lambda i, j: (layer, 0, j)),
            pl.BlockSpec((1, tn), lambda i, j: (0, j)),
        ],
        out_specs=pl.BlockSpec((tm, tn), lambda i, j: (i, j)),
        out_shape=jax.ShapeDtypeStruct((t, d), BF16),
        scratch_shapes=[pltpu.VMEM((tm, d), BF16)],
        compiler_params=_params(("arbitrary", "arbitrary"), 40),
        name="glu",
    )(y, y, w, b.reshape(1, d))


def _pool_kernel(u_ref, buf_ref, w_ref, s_ref, y_ref, tail_ref, ext_scr, *, tc, start_pos, cg):
    c = pl.program_id(1)

    @pl.when(c == 0)
    def _():
        ext_scr[0:POOL_HIST, :] = buf_ref[...]

    ext_scr[POOL_HIST:POOL_HIST + tc, :] = u_ref[...].astype(F32)
    pos = start_pos + c * tc + lax.broadcasted_iota(jnp.int32, (tc, 1), 0)
    for g, win in enumerate(POOL_WINDOWS):
        cols = slice(g * cg, (g + 1) * cg)
        x = ext_scr[:, cols]
        acc, dist = x, 1
        while dist < win:
            acc = acc + pltpu.roll(acc, dist, 0)
            dist *= 2
        wsum = acc[POOL_HIST:, :]
        cnt = jnp.minimum(pos + 1, win).astype(F32)
        zg = wsum * (1.0 / cnt) - x[POOL_HIST:, :]
        y = _dot(zg.astype(BF16), w_ref[g]) * s_ref[:, cols]
        y_ref[:, cols] = y.astype(y_ref.dtype)

    tail = ext_scr[tc:tc + POOL_HIST, :]
    ext_scr[0:POOL_HIST, :] = tail

    @pl.when(c == pl.num_programs(1) - 1)
    def _():
        tail_ref[...] = tail


def pool(z, colblk, buf16, w, layer, scale, *, start_pos, tc):
    bsz, seq, _ = z.shape
    _, n_g, cg, _ = w.shape
    db = n_g * cg
    tc = min(tc, seq)
    kern = functools.partial(_pool_kernel, tc=tc, start_pos=start_pos, cg=cg)
    return pl.pallas_call(
        kern,
        grid=(bsz, seq // tc),
        in_specs=[
            pl.BlockSpec((None, tc, db), lambda b, c: (b, c, colblk)),
            pl.BlockSpec((None, POOL_HIST, db), lambda b, c: (b, 0, 0)),
            pl.BlockSpec((None, n_g, cg, cg), lambda b, c: (layer, 0, 0, 0)),
            pl.BlockSpec((1, db), lambda b, c: (0, 0)),
        ],
        out_specs=[
            pl.BlockSpec((None, tc, db), lambda b, c: (b, c, 0)),
            pl.BlockSpec((None, POOL_HIST, db), lambda b, c: (b, 0, 0)),
        ],
        out_shape=[
            jax.ShapeDtypeStruct((bsz, seq, db), BF16),
            jax.ShapeDtypeStruct((bsz, POOL_HIST, db), F32),
        ],
        scratch_shapes=[pltpu.VMEM((POOL_HIST + tc, db), F32)],
        compiler_params=_params(("arbitrary", "arbitrary"), 40),
        name="pool",
    )(z, buf16, w, scale.reshape(1, db))


def _head_rms(x, g):
    ms = jnp.mean(x * x, axis=-1, keepdims=True)
    return (x * lax.rsqrt(ms + RMS_EPS)) * g


def _combine(os_, lses):
    m = jnp.maximum(jnp.maximum(lses[0], lses[1]), lses[2])
    ws = [jnp.exp(l - m) for l in lses]
    tot = ws[0] + ws[1] + ws[2]
    return (ws[0] * os_[0] + ws[1] * os_[1] + ws[2] * os_[2]) / tot


def _attn_prompt_kernel(q_ref, k_ref, v_ref, qn_ref, kn_ref, att_ref, ko_ref, vo_ref,
                        qs_scr, qf_scr, kf_scr, vf_scr, qd_scr, kd_scr, vd_scr, s_scr, p_scr, m_scr, o_scr, l_scr,
                        *, seq, scale):
    blk = ATT_BLOCK
    rows = 256
    n_all = seq // blk

    def prep(r, c):
        sl = pl.ds(pl.multiple_of(r * rows, rows), rows)
        qs_scr[sl, :] = _head_rms(q_ref[sl, :].astype(F32), qn_ref[...]) * scale
        ko_ref[sl, :] = _head_rms(k_ref[sl, :].astype(F32), kn_ref[...])
        vo_ref[sl, :] = v_ref[sl, :].astype(F32)
        return c

    lax.fori_loop(0, seq // rows, prep, 0, unroll=True)
    kd_scr[0:blk, :] = jnp.zeros((blk, LANES), BF16)
    vd_scr[0:blk, :] = jnp.zeros((blk, 2 * LANES), BF16)
    vd_scr[blk:, LANES:] = jnp.ones((seq, LANES), BF16)

    qi = lax.broadcasted_iota(jnp.int32, (blk, blk), 0)
    kj = lax.broadcasted_iota(jnp.int32, (blk, blk), 1)
    cur_ok = kj <= qi
    prev_ok = kj >= qi
    band_ok = jnp.concatenate([prev_ok, cur_ok], axis=1)
    in_cur = lax.broadcasted_iota(jnp.int32, (blk, 2 * blk), 1) >= blk

    for g, (window, dil) in enumerate(BRANCHES):
        n_blk = seq // (dil * blk)
        col0 = 0 if n_blk > 1 else LANES

        def place(idx, dil=dil, n_blk=n_blk):
            res = idx // n_blk
            n = idx - res * n_blk
            start = res + n * (dil * blk)
            nat = pl.ds(start, blk, stride=dil) if dil > 1 else pl.ds(pl.multiple_of(start, blk), blk)
            cur = pl.ds(pl.multiple_of(idx * blk, blk), blk)
            kcur = pl.ds(pl.multiple_of((idx + 1) * blk, blk), blk)
            kwin = pl.ds(pl.multiple_of(idx * blk, blk), 2 * blk)
            return nat, cur, kcur, kwin, n

        keep_f32 = dil == BRANCHES[1][1]
        two_level = g == 2 and dil == BRANCHES[1][1] ** 2

        def gather(idx, c, place=place, dil=dil, n_blk=n_blk, keep_f32=keep_f32, two_level=two_level):
            nat, cur, kcur, _, n = place(idx)
            if two_level:
                mid = BRANCHES[1][1]
                res = idx // n_blk
                start = (res % mid) * (seq // mid) + res // mid + n * (mid * blk)
                src = pl.ds(start, blk, stride=mid)
                q, k, v = qf_scr[src, :], kf_scr[src, :], vf_scr[src, :]
            else:
                q, k, v = qs_scr[nat, :], ko_ref[nat, :], vo_ref[nat, :]
            if keep_f32:
                qf_scr[cur, :], kf_scr[cur, :], vf_scr[cur, :] = q, k, v
            qd_scr[cur, :] = q.astype(BF16)
            kd_scr[kcur, :] = k.astype(BF16)
            vd_scr[kcur, 0:LANES] = v.astype(BF16)
            return c

        lax.fori_loop(0, n_all, gather, 0, unroll=True)

        def scores(idx, c, place=place, n_blk=n_blk):
            _, cur, kcur, kwin, n = place(idx)
            q = qd_scr[cur, :]
            if n_blk > 1:
                ok = jnp.logical_and(band_ok, jnp.logical_or(in_cur, n > 0))
                s_scr[idx] = jnp.where(ok, _dot_nt(q, kd_scr[kwin, :]), NEG_INF)
            else:
                s_scr[idx, :, LANES:] = jnp.where(cur_ok, _dot_nt(q, kd_scr[kcur, :]), NEG_INF)
            return c

        lax.fori_loop(0, n_all, scores, 0, unroll=True)

        def softmax(idx, c, col0=col0):
            s = s_scr[idx, :, col0:]
            m = jnp.max(s, axis=-1, keepdims=True)
            p_scr[idx, :, col0:] = jnp.exp(s - m).astype(BF16)
            m_scr[idx] = jnp.broadcast_to(m, (blk, LANES))
            return c

        lax.fori_loop(0, n_all, softmax, 0, unroll=True)

        def values(idx, c, g=g, place=place, n_blk=n_blk):
            nat, _, kcur, kwin, _ = place(idx)
            if n_blk > 1:
                ov = _dot(p_scr[idx], vd_scr[kwin, :])
            else:
                ov = _dot(p_scr[idx, :, LANES:], vd_scr[kcur, :])
            l = ov[:, LANES:]
            o_scr[g, nat, :] = ov[:, 0:LANES] / l
            l_scr[g, nat, :] = m_scr[idx] + jnp.log(l)
            return c

        lax.fori_loop(0, n_all, values, 0, unroll=True)

    def comb(r, c):
        sl = pl.ds(pl.multiple_of(r * rows, rows), rows)
        out = _combine([o_scr[g, sl, :] for g in range(3)], [l_scr[g, sl, :] for g in range(3)])
        att_ref[sl, :] = out.astype(att_ref.dtype)
        return c

    lax.fori_loop(0, seq // rows, comb, 0, unroll=True)


def attn_prompt(z, qn, kn, *, n_heads):
    bsz, seq, _ = z.shape
    assert seq % (BRANCHES[-1][1] * ATT_BLOCK) == 0
    hd = LANES
    kern = functools.partial(_attn_prompt_kernel, seq=seq, scale=hd ** -0.5)
    blk = lambda off: pl.BlockSpec((None, seq, hd), lambda b, h: (b, 0, off + h))
    return pl.pallas_call(
        kern,
        grid=(bsz, n_heads),
        in_specs=[blk(0), blk(n_heads), blk(2 * n_heads),
                  pl.BlockSpec((1, hd), lambda b, h: (0, 0)), pl.BlockSpec((1, hd), lambda b, h: (0, 0))],
        out_specs=[blk(0), blk(0), blk(0)],
        out_shape=[
            jax.ShapeDtypeStruct((bsz, seq, n_heads * hd), BF16),
            jax.ShapeDtypeStruct((bsz, seq, n_heads * hd), F32),
            jax.ShapeDtypeStruct((bsz, seq, n_heads * hd), F32),
        ],
        scratch_shapes=[
            pltpu.VMEM((seq, hd), F32),
            pltpu.VMEM((seq, hd), F32), pltpu.VMEM((seq, hd), F32), pltpu.VMEM((seq, hd), F32),
            pltpu.VMEM((seq, hd), BF16), pltpu.VMEM((seq + ATT_BLOCK, hd), BF16),
            pltpu.VMEM((seq + ATT_BLOCK, 2 * hd), BF16),
            pltpu.VMEM((seq // ATT_BLOCK, ATT_BLOCK, 2 * ATT_BLOCK), F32),
            pltpu.VMEM((seq // ATT_BLOCK, ATT_BLOCK, 2 * ATT_BLOCK), BF16),
            pltpu.VMEM((seq // ATT_BLOCK, ATT_BLOCK, hd), F32),
            pltpu.VMEM((3, seq, hd), F32),
            pltpu.VMEM((3, seq, hd), F32),
        ],
        compiler_params=_params(("arbitrary", "arbitrary"), 40),
        name="attn_prompt",
    )(z, z, z, qn.reshape(1, hd), kn.reshape(1, hd))


SAMPLE_PAD = 16


def _attn_sample_kernel(q_ref, k_ref, v_ref, ck_ref, cv_ref, qn_ref, kn_ref, att_ref, ko_ref, vo_ref,
                        q_scr, kn_scr, vn_scr, s_scr, v_scr, o_scr, k4_scr, v4_scr,
                        *, s_new, n_buf, n_heads, pc, scale):
    c = pl.program_id(1)
    n_ch = n_buf // pc
    pad = SAMPLE_PAD

    @pl.when(c == 0)
    def _():
        q_scr[...] = jnp.zeros_like(q_scr)
        kn_scr[...] = jnp.zeros_like(kn_scr)
        vn_scr[...] = jnp.zeros_like(vn_scr)
        for h in range(n_heads):
            lanes = slice(h * LANES, (h + 1) * LANES)
            k_new = _head_rms(k_ref[:, lanes].astype(F32), kn_ref[...])
            v_new = v_ref[:, lanes].astype(F32)
            ko_ref[:, lanes] = k_new
            vo_ref[:, lanes] = v_new
            q_scr[h, 0:s_new, :] = _head_rms(q_ref[:, lanes].astype(F32), qn_ref[...]) * scale
            kn_scr[h, 0:s_new, :] = k_new
            vn_scr[h, 0:s_new, :] = v_new

    mid = math.isqrt(n_heads)
    two_level = mid > 1 and mid * mid == n_heads and mid % 8 != 0
    if two_level:
        part = pc * n_heads // mid
        for a in range(mid):
            k4_scr[a] = ck_ref[pl.ds(a, part, stride=mid), :]
            v4_scr[a] = cv_ref[pl.ds(a, part, stride=mid), :]
    dst = pl.ds(pl.multiple_of(c * pc, pc), pc)
    for h in range(n_heads):
        if two_level:
            rows = pl.ds(h // mid, pc, stride=mid)
            k_h, v_h = k4_scr[h % mid, rows, :], v4_scr[h % mid, rows, :]
        else:
            rows = pl.ds(h, pc, stride=n_heads)
            k_h, v_h = ck_ref[rows, :], cv_ref[rows, :]
        s_scr[h, c] = _dot_nt(q_scr[h].astype(BF16), k_h.astype(BF16))
        v_scr[h, dst, :] = v_h.astype(BF16)

    @pl.when(c == n_ch - 1)
    def _():
        qi = lax.broadcasted_iota(jnp.int32, (pad, pc), 0)
        kj = lax.broadcasted_iota(jnp.int32, (pad, pc), 1)
        qi_n = lax.broadcasted_iota(jnp.int32, (pad, pad), 0)
        kj_n = lax.broadcasted_iota(jnp.int32, (pad, pad), 1)
        dist_n = qi_n - kj_n
        new_ok = jnp.logical_and(dist_n >= 0, kj_n < s_new)

        def finish(h, carry):
            q = q_scr[h].astype(BF16)
            s_n = _dot_nt(q, kn_scr[h].astype(BF16))
            s_c = [s_scr[h, cc] for cc in range(n_ch)]
            ps, pns, ls, ms = [], [], [], []
            for window, dil in BRANCHES:
                msk = []
                for cc in range(n_ch):
                    dist = n_buf + qi - (cc * pc + kj)
                    ok = jnp.logical_and((dist & (dil - 1)) == 0, dist <= window)
                    msk.append(jnp.where(ok, s_c[cc], NEG_INF))
                m_n = jnp.where(jnp.logical_and(new_ok, (dist_n & (dil - 1)) == 0), s_n, NEG_INF)
                m = jnp.max(m_n, axis=-1, keepdims=True)
                for cc in range(n_ch):
                    m = jnp.maximum(m, jnp.max(msk[cc], axis=-1, keepdims=True))
                p_n = jnp.exp(m_n - m)
                l = jnp.sum(p_n, axis=-1, keepdims=True)
                pb = []
                for cc in range(n_ch):
                    p = jnp.exp(msk[cc] - m)
                    l = l + jnp.sum(p, axis=-1, keepdims=True)
                    pb.append(p.astype(BF16))
                ps.append(pb)
                pns.append(p_n.astype(BF16))
                ls.append(l)
                ms.append(m)
            ov = _dot(jnp.concatenate(pns, axis=0), vn_scr[h].astype(BF16))
            for cc in range(n_ch):
                ov = ov + _dot(jnp.concatenate([ps[g][cc] for g in range(3)], axis=0),
                               v_scr[h, cc * pc:(cc + 1) * pc, :])
            outs = [ov[g * pad:(g + 1) * pad] / ls[g] for g in range(3)]
            lses = [jnp.broadcast_to(ms[g] + jnp.log(ls[g]), (pad, LANES)) for g in range(3)]
            o_scr[h] = _combine(outs, lses)
            return carry

        lax.fori_loop(0, n_heads, finish, 0, unroll=4)
        for h in range(n_heads):
            att_ref[:, h * LANES:(h + 1) * LANES] = o_scr[h, 0:s_new, :].astype(att_ref.dtype)


def attn_sample(z, cache_k, cache_v, row0, qn, kn, *, n_heads):
    bsz, s_new, _ = z.shape
    hd = LANES
    d_c = n_heads * hd
    n_buf = cache_k.shape[1] // n_heads
    assert n_buf >= BRANCHES[-1][0] and s_new <= SAMPLE_PAD
    pc = min(n_buf, 512)
    mid = math.isqrt(n_heads)
    kern = functools.partial(_attn_sample_kernel, s_new=s_new, n_buf=n_buf, n_heads=n_heads, pc=pc,
                             scale=hd ** -0.5)
    blk = lambda off: pl.BlockSpec((None, s_new, d_c), lambda b, c: (b, 0, off))
    cblk = pl.BlockSpec((None, pc * n_heads, hd), lambda b, c: (row0 + b, c, 0))
    vec = pl.BlockSpec((1, hd), lambda b, c: (0, 0))
    return pl.pallas_call(
        kern,
        grid=(bsz, n_buf // pc),
        in_specs=[blk(0), blk(1), blk(2), cblk, cblk, vec, vec],
        out_specs=[blk(0), blk(0), blk(0)],
        out_shape=[
            jax.ShapeDtypeStruct((bsz, s_new, d_c), BF16),
            jax.ShapeDtypeStruct((bsz, s_new, d_c), F32),
            jax.ShapeDtypeStruct((bsz, s_new, d_c), F32),
        ],
        scratch_shapes=[
            pltpu.VMEM((n_heads, SAMPLE_PAD, hd), F32),
            pltpu.VMEM((n_heads, SAMPLE_PAD, hd), F32),
            pltpu.VMEM((n_heads, SAMPLE_PAD, hd), F32),
            pltpu.VMEM((n_heads, n_buf // pc, SAMPLE_PAD, pc), F32),
            pltpu.VMEM((n_heads, n_buf, hd), BF16),
            pltpu.VMEM((n_heads, SAMPLE_PAD, hd), F32),
            pltpu.VMEM((mid, pc * n_heads // mid, hd), F32),
            pltpu.VMEM((mid, pc * n_heads // mid, hd), F32),
        ],
        compiler_params=_params(("arbitrary", "arbitrary"), 48),
        name="attn_sample",
    )(z, z, z, cache_k, cache_v, qn.reshape(1, hd), kn.reshape(1, hd))


def _sgu_kernel(gu_ref, gv_ref, lg_ref, lb_ref, w_ref, bt_ref, o_ref, vn_ref, vb_scr, *, rows, n_g, cd):
    t = w_ref.shape[1]
    gv = _gelu(gv_ref[...].astype(F32))
    mu = jnp.mean(gv, axis=-1, keepdims=True)
    xc = gv - mu
    var = jnp.mean(xc * xc, axis=-1, keepdims=True)
    vn = (xc * lax.rsqrt(var + LN_EPS)) * lg_ref[...] + lb_ref[...]
    vn_ref[...] = vn
    if rows < t:
        vb_scr[...] = jnp.zeros_like(vb_scr)
    vb_scr[0:rows, :] = vn.astype(BF16)
    ri = lax.broadcasted_iota(jnp.int32, (t, t), 0)
    ci = lax.broadcasted_iota(jnp.int32, (t, t), 1)
    for g in range(n_g):
        cols = slice(g * cd, (g + 1) * cd)
        wg = jnp.where(ri >= ci, w_ref[g], 0.0).astype(BF16)
        mixed = _dot(wg, vb_scr[:, cols])[0:rows, :] + bt_ref[:, g:g + 1]
        o_ref[:, cols] = (_gelu(gu_ref[:, cols].astype(F32)) * mixed).astype(o_ref.dtype)


def sgu(z, colblk_u, ln_g, ln_b, w_s, b_s):
    bsz, seq, _ = z.shape
    n_g = w_s.shape[0]
    dd = ln_g.shape[0]
    cd = dd // n_g
    t = min(seq, CHUNK)
    tp = max(t, LANES)
    w = jnp.pad(w_s[:, :t, :t], ((0, 0), (0, tp - t), (0, tp - t)))
    bt = jnp.transpose(b_s[:, :t])
    kern = functools.partial(_sgu_kernel, rows=t, n_g=n_g, cd=cd)
    return pl.pallas_call(
        kern,
        grid=(bsz, seq // t),
        in_specs=[
            pl.BlockSpec((None, t, dd), lambda b, c: (b, c, colblk_u)),
            pl.BlockSpec((None, t, dd), lambda b, c: (b, c, colblk_u + 1)),
            pl.BlockSpec((1, dd), lambda b, c: (0, 0)),
            pl.BlockSpec((1, dd), lambda b, c: (0, 0)),
            pl.BlockSpec((n_g, tp, tp), lambda b, c: (0, 0, 0)),
            pl.BlockSpec((t, n_g), lambda b, c: (0, 0)),
        ],
        out_specs=[
            pl.BlockSpec((None, t, dd), lambda b, c: (b, c, 0)),
            pl.BlockSpec((None, t, dd), lambda b, c: (b, c, 0)),
        ],
        out_shape=[
            jax.ShapeDtypeStruct((bsz, seq, dd), BF16),
            jax.ShapeDtypeStruct((bsz, seq, dd), F32),
        ],
        scratch_shapes=[pltpu.VMEM((tp, dd), BF16)],
        compiler_params=_params(("arbitrary", "arbitrary"), 40),
        name="sgu",
    )(z, z, ln_g.reshape(1, dd), ln_b.reshape(1, dd), w, bt)


def _in_proj(x, norm_g, w_in, i, tiles):
    if tiles["cast"]:
        return norm_matmul_cast(x, norm_g, w_in, i, tn=MIX_TILE)
    return norm_matmul(x, norm_g, w_in, tm=tiles["tm"], out_dtype=tiles["z_dtype"]), None


def _res_proj(x, a, b, w_out, i, tiles):
    if tiles["cast"]:
        return out_proj_cast(x, a, b, w_out, i, tn=MIX_TILE)
    return out_proj(x, a, b, w_out, tm=tiles["tm_out"]), None


def _even_layer(x, bsz, seq, h0_re, h0_im, pool_buf, start_pos, norm_g, w_in, w_out, i, s5p, pool_w, pool_scale,
                w_glu, b_glu, d_skip, tiles):
    t, d = x.shape
    pw_re, pw_im, b_blk, c_blk = s5p
    n_slab = b_blk.shape[0]
    d_a = n_slab * LANES
    z, w_in_b = _in_proj(x, norm_g, w_in, i, tiles)
    z = z.reshape(bsz, seq, -1)
    h0 = jnp.concatenate([h0_re.reshape(bsz, n_slab, 1, SLAB_STATE), h0_im.reshape(bsz, n_slab, 1, SLAB_STATE)], axis=-1)
    if tiles["s5_long"]:
        y_pre, h_last = s5_scan_long(z, 0, b_blk, c_blk, pw_re, pw_im, d_skip, h0)
    else:
        y_pre, h_last = s5_scan(z, 0, b_blk, c_blk, pw_re, pw_im, d_skip, h0, split_in=True)
    ya = glu(y_pre.reshape(t, d_a), w_glu, b_glu, i, tm=tiles["tm_glu"], tn=MIX_TILE)
    buf16 = jnp.pad(pool_buf, ((0, 0), (POOL_HIST - pool_buf.shape[1], 0), (0, 0)))
    yb, tail = pool(z, 1, buf16, pool_w, i, pool_scale, start_pos=start_pos, tc=256)
    x, w_out_b = _res_proj(x, ya, yb.reshape(t, -1), w_out, i, tiles)
    g_a = n_slab * SLAB_GROUPS
    h_re = h_last[..., :SLAB_STATE].reshape(bsz, g_a, S5_P)
    h_im = h_last[..., SLAB_STATE:].reshape(bsz, g_a, S5_P)
    return x, h_re, h_im, tail[:, POOL_HIST - pool_buf.shape[1]:], (w_in_b, w_out_b)


def _odd_layer(x, bsz, seq, k_buf, v_buf, norm_g, w_in, w_out, i, qn, kn, ln_g, ln_b, w_s, b_s, n_heads, tiles):
    t, d = x.shape
    d_c = n_heads * LANES
    z, w_in_b = _in_proj(x, norm_g, w_in, i, tiles)
    z = z.reshape(bsz, seq, -1)
    if k_buf is None:
        att, k_new, v_new = attn_prompt(z, qn, kn, n_heads=n_heads)
    else:
        att, k_new, v_new = attn_sample(z, k_buf, v_buf, i * bsz, qn, kn, n_heads=n_heads)
    dd = ln_g.shape[0]
    sg, vn = sgu(z, (3 * d_c) // dd, ln_g, ln_b, w_s, b_s)
    x, w_out_b = _res_proj(x, att.reshape(t, d_c), sg.reshape(t, dd), w_out, i, tiles)
    hd = LANES
    return x, k_new.reshape(bsz, seq, n_heads, hd), v_new.reshape(bsz, seq, n_heads, hd), vn, (w_in_b, w_out_b)


def kernel(x_prompt, x_sample, state_s5_re, state_s5_im, state_pool, cache_k, cache_v, norm_mix, norm_ffn, ev_w_in, ev_w_out, s5_lambda_re, s5_lambda_im, s5_log_dt, s5_b_re, s5_b_im, s5_c_re, s5_c_im, s5_d, s5_w_glu, s5_b_glu, pool_w, pool_scale, od_w_in, od_w_out, q_norm, k_norm, sgu_ln_g, sgu_ln_b, sgu_w, sgu_b, ffn_w1, ffn_w3, ffn_w2):
    bp, lp, d = x_prompt.shape
    bs, ls, _ = x_sample.shape
    depth = norm_mix.shape[0]
    n_heads = cache_k.shape[3]
    xp = x_prompt.reshape(bp * lp, d)
    xs = x_sample.reshape(bs * ls, d)
    tiles_p = dict(cast=False, tm=1024, tm_glu=512, tm_out=1024, s5_long=True, z_dtype=BF16)
    tiles_s = dict(cast=True, tm_glu=bs * ls, s5_long=False)
    g_a, p_a = s5_lambda_re.shape[1:]
    w_glu_b, pool_w_b = s5_w_glu.astype(BF16), pool_w.astype(BF16)
    d_c = n_heads * LANES
    cache_k2 = cache_k.reshape(-1, cache_k.shape[2] * n_heads, LANES)
    cache_v2 = cache_v.reshape(-1, cache_v.shape[2] * n_heads, LANES)

    s5r_p, s5i_p, pool_p, k_p, v_p = [], [], [], [], []
    s5r_s, s5i_s, pool_s, k_s, v_s, sgu_s = [], [], [], [], [], []
    for l in range(depth):
        i = l // 2
        if l % 2 == 0:
            pw_re, pw_im, bb_re, bb_im = s5_prep(s5_lambda_re[i], s5_lambda_im[i], s5_log_dt[i], s5_b_re[i], s5_b_im[i])
            b_blk, c_blk = s5_block_matrices(bb_re, bb_im, s5_c_re[i], s5_c_im[i])
            s5p = (pw_re, pw_im, b_blk, c_blk)
            rest = (i, s5p, pool_w_b, pool_scale[i], w_glu_b, s5_b_glu[i], s5_d[i])
            xs, hr, hi, buf, (w_in_b, w_out_b) = _even_layer(
                xs, bs, ls, state_s5_re[i], state_s5_im[i], state_pool[i], PAST_LEN, norm_mix[l], ev_w_in, ev_w_out,
                *rest, tiles_s)
            s5r_s.append(hr); s5i_s.append(hi); pool_s.append(buf)
            zero_h = jnp.zeros((bp, g_a, p_a), F32)
            zero_buf = jnp.zeros((bp, state_pool.shape[2], state_pool.shape[3]), F32)
            xp, hr, hi, buf, _ = _even_layer(xp, bp, lp, zero_h, zero_h, zero_buf, 0, norm_mix[l], w_in_b, w_out_b,
                                             *rest, tiles_p)
            s5r_p.append(hr); s5i_p.append(hi); pool_p.append(buf)
        else:
            rest = (i, q_norm[i], k_norm[i], sgu_ln_g[i], sgu_ln_b[i], sgu_w[i], sgu_b[i], n_heads)
            xs, nk, nv, vrows, (w_in_b, w_out_b) = _odd_layer(xs, bs, ls, cache_k2, cache_v2, norm_mix[l], od_w_in,
                                                              od_w_out, *rest, tiles_s)
            k_s.append(nk); v_s.append(nv); sgu_s.append(vrows)
            xp, nk, nv, _, _ = _odd_layer(xp, bp, lp, None, None, norm_mix[l], w_in_b, w_out_b, *rest, tiles_p)
            k_p.append(nk); v_p.append(nv)
        if l == 0:
            xs, *ffn_wb = ffn_cast(xs, norm_ffn[l], ffn_w1, ffn_w3, ffn_w2, l, tf=FFN_TILE)
        else:
            xs = ffn(xs, norm_ffn[l], *ffn_wb, tm=bs * ls)
        if l + 1 < depth:
            xp, ffn_wb = ffn(xp, norm_ffn[l], *ffn_wb, tm=512, nxt=(ffn_w1, ffn_w3, ffn_w2, l + 1))
        else:
            xp = ffn(xp, norm_ffn[l], *ffn_wb, tm=512)
    return (xp.reshape(bp, lp, d), xs.reshape(bs, ls, d),
            jnp.stack(s5r_p), jnp.stack(s5i_p), jnp.stack(pool_p), jnp.stack(k_p), jnp.stack(v_p),
            jnp.stack(s5r_s), jnp.stack(s5i_s), jnp.stack(pool_s), jnp.stack(k_s), jnp.stack(v_s),
            jnp.stack(sgu_s))
```

```python
import functools
import math

import jax
import jax.numpy as jnp
from jax import lax
from jax.experimental import pallas as pl
from jax.experimental.pallas import tpu as pltpu

F32 = jnp.float32
BF16 = jnp.bfloat16

RMS_EPS = 1e-6
LN_EPS = 1e-5
NEG_INF = -1e30

LANES = 128
SUBLANES = 8
ATT_BLOCK = 128
CHUNK = 128
POOL_WINDOWS = (2, 4, 8, 16)
POOL_HIST = 16
BRANCHES = ((128, 1), (512, 4), (2048, 16))
S5_GRP = 16
S5_P = 64
SLAB_GROUPS = LANES // S5_GRP
SLAB_STATE = SLAB_GROUPS * S5_P
PAST_LEN = 8192
FFN_TILE = 256
MIX_TILE = 512


def _params(sem, vmem_mib):
    return pltpu.CompilerParams(dimension_semantics=sem, vmem_limit_bytes=vmem_mib << 20)


def _gelu(x):
    return 0.5 * x * (1.0 + lax.erf(x * (1.0 / math.sqrt(2.0))))


def _sigmoid(x):
    return 1.0 / (1.0 + jnp.exp(-x))


def _split_bf16(a):
    hi = a.astype(BF16)
    lo = (a - hi.astype(F32)).astype(BF16)
    return hi, lo


def _dot(a, b):
    return jnp.dot(a, b, preferred_element_type=F32)


def _dot_nt(a, b):
    return lax.dot_general(a, b, (((1,), (1,)), ((), ())), preferred_element_type=F32)


def _rms_rows_to(x_ref, g_ref, h_ref, rows):
    step = 16 if rows % 16 == 0 else rows

    def body(r, c):
        sl = pl.ds(pl.multiple_of(r * step, step), step)
        x = x_ref[sl, :]
        ms = jnp.mean(x * x, axis=-1, keepdims=True)
        h_ref[sl, :] = ((x * lax.rsqrt(ms + RMS_EPS)) * g_ref[...]).astype(h_ref.dtype)
        return c

    lax.fori_loop(0, rows // step, body, 0, unroll=min(4, rows // step))


def _norm_matmul_kernel(x_ref, g_ref, w_ref, o_ref, h_ref):
    @pl.when(pl.program_id(1) == 0)
    def _():
        _rms_rows_to(x_ref, g_ref, h_ref, x_ref.shape[0])

    o_ref[...] = _dot(h_ref[...], w_ref[...]).astype(o_ref.dtype)


def norm_matmul(x, g, w, *, tm, out_dtype=F32):
    t, d = x.shape
    n_t, _, tn = w.shape
    return pl.pallas_call(
        _norm_matmul_kernel,
        grid=(t // tm, n_t),
        in_specs=[
            pl.BlockSpec((tm, d), lambda i, j: (i, 0)),
            pl.BlockSpec((1, d), lambda i, j: (0, 0)),
            pl.BlockSpec((None, d, tn), lambda i, j: (j, 0, 0)),
        ],
        out_specs=pl.BlockSpec((tm, tn), lambda i, j: (i, j)),
        out_shape=jax.ShapeDtypeStruct((t, n_t * tn), out_dtype),
        scratch_shapes=[pltpu.VMEM((tm, d), BF16)],
        compiler_params=_params(("arbitrary", "arbitrary"), 58),
        name="norm_matmul",
    )(x, g.reshape(1, d), w)


def _norm_matmul_cast_kernel(x_ref, g_ref, w_ref, o_ref, wb_ref, h_ref):
    @pl.when(pl.program_id(0) == 0)
    def _():
        _rms_rows_to(x_ref, g_ref, h_ref, x_ref.shape[0])

    wb_ref[...] = w_ref[...].astype(BF16)
    o_ref[...] = _dot(h_ref[...], wb_ref[...]).astype(o_ref.dtype)


def norm_matmul_cast(x, g, w, layer, *, tn):
    t, d = x.shape
    n = w.shape[2]
    return pl.pallas_call(
        _norm_matmul_cast_kernel,
        grid=(n // tn,),
        in_specs=[
            pl.BlockSpec((t, d), lambda j: (0, 0)),
            pl.BlockSpec((1, d), lambda j: (0, 0)),
            pl.BlockSpec((None, d, tn), lambda j: (layer, 0, j)),
        ],
        out_specs=[
            pl.BlockSpec((t, tn), lambda j: (0, j)),
            pl.BlockSpec((None, d, tn), lambda j: (j, 0, 0)),
        ],
        out_shape=[
            jax.ShapeDtypeStruct((t, n), F32),
            jax.ShapeDtypeStruct((n // tn, d, tn), BF16),
        ],
        scratch_shapes=[pltpu.VMEM((t, d), BF16)],
        compiler_params=_params(("arbitrary",), 48),
        name="norm_matmul_cast",
    )(x, g.reshape(1, d), w)


FFN_OUT_CHUNK = 512


def _ffn_step(first, x_ref, g_ref, w1_ref, w3_ref, w2_ref, o_ref, h_ref):
    @pl.when(first)
    def _():
        _rms_rows_to(x_ref, g_ref, h_ref, x_ref.shape[0])
        o_ref[...] = x_ref[...]

    h = h_ref[...]
    a = _dot(h, w1_ref[...])
    b = _dot(h, w3_ref[...])
    u = ((a * _sigmoid(a)) * b).astype(BF16)
    for c in range(0, o_ref.shape[1], FFN_OUT_CHUNK):
        o_ref[:, c:c + FFN_OUT_CHUNK] += _dot(u, w2_ref[:, c:c + FFN_OUT_CHUNK])


def _ffn_kernel(x_ref, g_ref, w1_ref, w3_ref, w2_ref, o_ref, h_ref):
    _ffn_step(pl.program_id(1) == 0, x_ref, g_ref, w1_ref, w3_ref, w2_ref, o_ref, h_ref)


def _ffn_next_kernel(x_ref, g_ref, w1_ref, w3_ref, w2_ref, n1_ref, n3_ref, n2_ref,
                     o_ref, c1_ref, c3_ref, c2_ref, h_ref, *, half):
    i = pl.program_id(0)

    @pl.when(i < half)
    def _():
        c1_ref[...] = n1_ref[...].astype(BF16)
        c3_ref[...] = n3_ref[...].astype(BF16)

    @pl.when(i >= half)
    def _():
        c2_ref[...] = n2_ref[...].astype(BF16)

    _ffn_step(pl.program_id(1) == 0, x_ref, g_ref, w1_ref, w3_ref, w2_ref, o_ref, h_ref)


def ffn(x, g, w1, w3, w2, *, tm, nxt=None):
    t, d = x.shape
    n_f, _, tf = w1.shape
    n_m = t // tm
    specs = [
        pl.BlockSpec((tm, d), lambda i, j: (i, 0)),
        pl.BlockSpec((1, d), lambda i, j: (0, 0)),
        pl.BlockSpec((None, d, tf), lambda i, j: (j, 0, 0)),
        pl.BlockSpec((None, d, tf), lambda i, j: (j, 0, 0)),
        pl.BlockSpec((tf, d), lambda i, j: (j, 0)),
    ]
    out_spec = pl.BlockSpec((tm, d), lambda i, j: (i, 0))
    out_shape = jax.ShapeDtypeStruct((t, d), F32)
    common = dict(grid=(n_m, n_f), scratch_shapes=[pltpu.VMEM((tm, d), BF16)],
                  compiler_params=_params(("arbitrary", "arbitrary"), 56))
    if nxt is None:
        return pl.pallas_call(_ffn_kernel, in_specs=specs, out_specs=out_spec, out_shape=out_shape, name="ffn",
                              **common)(x, g.reshape(1, d), w1, w3, w2)
    n1, n3, n2, layer = nxt
    half = n_m // 2
    piece = d // half
    assert n_m % 2 == 0 and d % half == 0 and piece % LANES == 0
    def col_idx(i, j):
        on = i < half
        return jnp.where(on, i, half - 1), jnp.where(on, j, n_f - 1)

    def row_idx(i, j):
        on = i >= half
        return jnp.where(on, j, 0), jnp.where(on, i - half, 0)

    col = pl.BlockSpec((None, piece, tf), lambda i, j: (layer, *col_idx(i, j)))
    row = pl.BlockSpec((None, tf, piece), lambda i, j: (layer, *row_idx(i, j)))
    col_out = pl.BlockSpec((None, piece, tf), lambda i, j: (col_idx(i, j)[1], col_idx(i, j)[0], 0))
    row_out = pl.BlockSpec((tf, piece), lambda i, j: row_idx(i, j))
    out, c1, c3, c2 = pl.pallas_call(
        functools.partial(_ffn_next_kernel, half=half),
        in_specs=specs + [col, col, row],
        out_specs=[out_spec, col_out, col_out, row_out],
        out_shape=[out_shape, jax.ShapeDtypeStruct(w1.shape, BF16), jax.ShapeDtypeStruct(w3.shape, BF16),
                   jax.ShapeDtypeStruct(w2.shape, BF16)],
        name="ffn_next", **common,
    )(x, g.reshape(1, d), w1, w3, w2, n1, n3, n2)
    return out, (c1, c3, c2)


def _ffn_cast_kernel(x_ref, g_ref, w1_ref, w3_ref, w2_ref, o_ref, w1b_ref, w3b_ref, w2b_ref, h_ref):
    w1b_ref[...] = w1_ref[...].astype(BF16)
    w3b_ref[...] = w3_ref[...].astype(BF16)
    w2b_ref[...] = w2_ref[...].astype(BF16)
    _ffn_step(pl.program_id(0) == 0, x_ref, g_ref, w1b_ref, w3b_ref, w2b_ref, o_ref, h_ref)


def ffn_cast(x, g, w1, w3, w2, layer, *, tf):
    t, d = x.shape
    f = w1.shape[2]
    return pl.pallas_call(
        _ffn_cast_kernel,
        grid=(f // tf,),
        in_specs=[
            pl.BlockSpec((t, d), lambda j: (0, 0)),
            pl.BlockSpec((1, d), lambda j: (0, 0)),
            pl.BlockSpec((None, d, tf), lambda j: (layer, 0, j)),
            pl.BlockSpec((None, d, tf), lambda j: (layer, 0, j)),
            pl.BlockSpec((None, tf, d), lambda j: (layer, j, 0)),
        ],
        out_specs=[
            pl.BlockSpec((t, d), lambda j: (0, 0)),
            pl.BlockSpec((None, d, tf), lambda j: (j, 0, 0)),
            pl.BlockSpec((None, d, tf), lambda j: (j, 0, 0)),
            pl.BlockSpec((tf, d), lambda j: (j, 0)),
        ],
        out_shape=[
            jax.ShapeDtypeStruct((t, d), F32),
            jax.ShapeDtypeStruct((f // tf, d, tf), BF16),
            jax.ShapeDtypeStruct((f // tf, d, tf), BF16),
            jax.ShapeDtypeStruct((f, d), BF16),
        ],
        scratch_shapes=[pltpu.VMEM((t, d), BF16)],
        compiler_params=_params(("arbitrary",), 48),
        name="ffn_cast",
    )(x, g.reshape(1, d), w1, w3, w2)


def _out_proj_kernel(x_ref, a_ref, b_ref, wa_ref, wb_ref, o_ref):
    o_ref[...] = x_ref[...] + _dot(a_ref[...], wa_ref[...]) + _dot(b_ref[...], wb_ref[...])


def out_proj(x, a, b, w_pair, *, tm):
    t, d = x.shape
    wa, wb = w_pair
    n_t, k, tn = wa.shape
    assert a.shape[1] == k and b.shape[1] == k and wb.shape == wa.shape
    return pl.pallas_call(
        _out_proj_kernel,
        grid=(t // tm, n_t),
        in_specs=[
            pl.BlockSpec((tm, tn), lambda i, j: (i, j)),
            pl.BlockSpec((tm, k), lambda i, j: (i, 0)),
            pl.BlockSpec((tm, k), lambda i, j: (i, 0)),
            pl.BlockSpec((None, k, tn), lambda i, j: (j, 0, 0)),
            pl.BlockSpec((None, k, tn), lambda i, j: (j, 0, 0)),
        ],
        out_specs=pl.BlockSpec((tm, tn), lambda i, j: (i, j)),
        out_shape=jax.ShapeDtypeStruct((t, d), F32),
        compiler_params=_params(("arbitrary", "arbitrary"), 48),
        name="out_proj",
    )(x, a, b, wa, wb)


def _out_proj_cast_kernel(x_ref, a_ref, b_ref, wa_ref, wb_ref, o_ref, wab_ref, wbb_ref):
    wab_ref[...] = wa_ref[...].astype(BF16)
    wbb_ref[...] = wb_ref[...].astype(BF16)
    o_ref[...] = x_ref[...] + _dot(a_ref[...], wab_ref[...]) + _dot(b_ref[...], wbb_ref[...])


def out_proj_cast(x, a, b, w, layer, *, tn):
    t, d = x.shape
    k = a.shape[1]
    assert b.shape[1] == k and w.shape[1] == 2 * k
    out, wa_b, wb_b = pl.pallas_call(
        _out_proj_cast_kernel,
        grid=(d // tn,),
        in_specs=[
            pl.BlockSpec((t, tn), lambda j: (0, j)),
            pl.BlockSpec((t, k), lambda j: (0, 0)),
            pl.BlockSpec((t, k), lambda j: (0, 0)),
            pl.BlockSpec((None, k, tn), lambda j: (layer, 0, j)),
            pl.BlockSpec((None, k, tn), lambda j: (layer, 1, j)),
        ],
        out_specs=[
            pl.BlockSpec((t, tn), lambda j: (0, j)),
            pl.BlockSpec((None, k, tn), lambda j: (j, 0, 0)),
            pl.BlockSpec((None, k, tn), lambda j: (j, 0, 0)),
        ],
        out_shape=[
            jax.ShapeDtypeStruct((t, d), F32),
            jax.ShapeDtypeStruct((d // tn, k, tn), BF16),
            jax.ShapeDtypeStruct((d // tn, k, tn), BF16),
        ],
        compiler_params=_params(("arbitrary",), 48),
        name="out_proj_cast",
    )(x, a, b, w, w)
    return out, (wa_b, wb_b)


def _s5_prep_kernel(lr_ref, li_ref, ldt_ref, lrx_ref, lix_ref, ldtx_ref, br_ref, bi_ref,
                    pwr_ref, pwi_ref, bbr_ref, bbi_ref):
    dt = jnp.exp(ldt_ref[...])
    mag = jnp.exp(lr_ref[...] * dt)
    ang = li_ref[...] * dt
    p_r, p_i = mag * jnp.cos(ang), mag * jnp.sin(ang)
    c_r, c_i = p_r, p_i
    pwr_ref[0], pwi_ref[0] = c_r, c_i
    for j in range(1, SUBLANES):
        c_r, c_i = c_r * p_r - c_i * p_i, c_r * p_i + c_i * p_r
        pwr_ref[j], pwi_ref[j] = c_r, c_i
    lr, li = lrx_ref[...], lix_ref[...]
    dtx = jnp.exp(ldtx_ref[...])
    magx = jnp.exp(lr * dtx)
    angx = li * dtx
    nr, ni = magx * jnp.cos(angx) - 1.0, magx * jnp.sin(angx)
    den = lr * lr + li * li
    qr = (nr * lr + ni * li) / den
    qi = (ni * lr - nr * li) / den
    br, bi = br_ref[...], bi_ref[...]
    bbr_ref[...] = qr * br - qi * bi
    bbi_ref[...] = qr * bi + qi * br


def s5_prep(lam_re, lam_im, log_dt, b_re, b_im):
    g, p = lam_re.shape
    h = b_re.shape[2]
    n_slab = g // SLAB_GROUPS
    slab = lambda a: a.reshape(n_slab, SLAB_GROUPS * p)
    rep = lambda a: jnp.repeat(a, h, axis=1)
    ldt_gp = jnp.broadcast_to(log_dt[:, None], (g, p))
    outs = pl.pallas_call(
        _s5_prep_kernel,
        out_shape=[
            jax.ShapeDtypeStruct((SUBLANES, n_slab, SLAB_GROUPS * p), F32),
            jax.ShapeDtypeStruct((SUBLANES, n_slab, SLAB_GROUPS * p), F32),
            jax.ShapeDtypeStruct((g, p * h), F32),
            jax.ShapeDtypeStruct((g, p * h), F32),
        ],
        name="s5_prep",
    )(slab(lam_re), slab(lam_im), slab(ldt_gp), rep(lam_re), rep(lam_im), rep(ldt_gp),
      b_re.reshape(g, p * h), b_im.reshape(g, p * h))
    pw_re, pw_im, bb_re, bb_im = outs
    pw_re = jnp.transpose(pw_re, (1, 0, 2))
    pw_im = jnp.transpose(pw_im, (1, 0, 2))
    return pw_re, pw_im, bb_re.reshape(g, p, h), bb_im.reshape(g, p, h)


def s5_block_matrices(bb_re, bb_im, c_re, c_im):
    g, p, h = bb_re.shape
    n_slab = g // SLAB_GROUPS
    eye = jnp.eye(SLAB_GROUPS, dtype=F32)

    def in_map(bb):
        t = bb.reshape(n_slab, SLAB_GROUPS, p, h)
        return jnp.einsum("kgph,gj->kghjp", t, eye).reshape(n_slab, SLAB_GROUPS * h, SLAB_GROUPS * p)

    def out_map(c):
        t = c.reshape(n_slab, SLAB_GROUPS, h, p)
        return jnp.einsum("kghp,gj->kgpjh", t, eye).reshape(n_slab, SLAB_GROUPS * p, SLAB_GROUPS * h)

    b_blk = jnp.concatenate([in_map(bb_re), in_map(bb_im)], axis=2)
    c_blk = jnp.concatenate([out_map(c_re), -out_map(c_im)], axis=1)
    return b_blk, c_blk


def _s5_scan_kernel(u_ref, bblk_ref, cblk_ref, pwr_ref, pwi_ref, d_ref, h0_ref,
                    y_ref, hl_ref, h_scr, bh_scr, bl_scr, ch_scr, *, seq, row_chunk, split_in):
    ns = SLAB_STATE

    @pl.when(pl.program_id(1) == 0)
    def _():
        bh, bl = _split_bf16(bblk_ref[...])
        bh_scr[...], bl_scr[...] = bh, bl
        ch_scr[...] = cblk_ref[...].astype(BF16)

    n_chunks = seq // row_chunk

    def proj_in(r, c):
        sl = pl.ds(pl.multiple_of(r * row_chunk, row_chunk), row_chunk)
        if split_in:
            uh, ul = _split_bf16(u_ref[sl, :].astype(F32))
            h_scr[sl, :] = _dot(uh, bh_scr[...]) + _dot(ul, bh_scr[...]) + _dot(uh, bl_scr[...])
        else:
            h_scr[sl, :] = _dot(u_ref[sl, :].astype(BF16), bh_scr[...])
        return c

    lax.fori_loop(0, n_chunks, proj_in, 0)

    rowid = lax.broadcasted_iota(jnp.int32, (SUBLANES, LANES), 0)
    for c in range(ns // LANES):
        re_l = slice(c * LANES, (c + 1) * LANES)
        im_l = slice(ns + c * LANES, ns + (c + 1) * LANES)
        p_r, p_i = pwr_ref[:, re_l], pwi_ref[:, re_l]
        steps = []
        for dist in (1, 2, 4):
            a_r = jnp.where(rowid >= dist, jnp.broadcast_to(p_r[dist - 1:dist], (SUBLANES, LANES)), 0.0)
            a_i = jnp.where(rowid >= dist, jnp.broadcast_to(p_i[dist - 1:dist], (SUBLANES, LANES)), 0.0)
            steps.append((dist, a_r, a_i))
        c_r = jnp.broadcast_to(h0_ref[:, re_l], (SUBLANES, LANES))
        c_i = jnp.broadcast_to(h0_ref[:, im_l], (SUBLANES, LANES))

        last = slice(SUBLANES - 1, SUBLANES)
        full = (SUBLANES, LANES)
        p8_r, p8_i = jnp.broadcast_to(p_r[last], full), jnp.broadcast_to(p_i[last], full)
        n_groups = seq // SUBLANES
        per_it = min(4, n_groups)

        def scan_rows(it, carry, re_l=re_l, im_l=im_l, p_r=p_r, p_i=p_i, p8_r=p8_r, p8_i=p8_i, steps=steps):
            base = pl.multiple_of(it * (per_it * SUBLANES), per_it * SUBLANES)
            sls = [pl.ds(base + j * SUBLANES, SUBLANES) for j in range(per_it)]
            loc = []
            for sl in sls:
                r, i = h_scr[sl, re_l], h_scr[sl, im_l]
                for dist, a_r, a_i in steps:
                    s_r, s_i = pltpu.roll(r, dist, 0), pltpu.roll(i, dist, 0)
                    r, i = r + (s_r * a_r - s_i * a_i), i + (s_r * a_i + s_i * a_r)
                loc.append((r, i))
            c_r, c_i = carry
            outs = []
            for r, i in loc:
                outs.append((r + (c_r * p_r - c_i * p_i), i + (c_r * p_i + c_i * p_r)))
                e_r, e_i = jnp.broadcast_to(r[last], full), jnp.broadcast_to(i[last], full)
                c_r, c_i = e_r + (c_r * p8_r - c_i * p8_i), e_i + (c_r * p8_i + c_i * p8_r)
            for sl, (r, i) in zip(sls, outs):
                h_scr[sl, re_l], h_scr[sl, im_l] = r, i
            return c_r, c_i

        c_r, c_i = lax.fori_loop(0, n_groups // per_it, scan_rows, (c_r, c_i))
        hl_ref[:, re_l] = c_r[0:1]
        hl_ref[:, im_l] = c_i[0:1]

    def proj_out(r, c):
        sl = pl.ds(pl.multiple_of(r * row_chunk, row_chunk), row_chunk)
        y = _dot(h_scr[sl, :].astype(BF16), ch_scr[...]) + d_ref[...] * u_ref[sl, :].astype(F32)
        y_ref[sl, :] = _gelu(y)
        return c

    lax.fori_loop(0, n_chunks, proj_out, 0)


def s5_scan(z, col0, b_blk, c_blk, pw_re, pw_im, d_skip, h0, *, split_in):
    bsz, seq, _ = z.shape
    n_slab = b_blk.shape[0]
    ns2 = 2 * SLAB_STATE
    row_chunk = min(seq, 1024)
    kern = functools.partial(_s5_scan_kernel, seq=seq, row_chunk=row_chunk, split_in=split_in)
    return pl.pallas_call(
        kern,
        grid=(n_slab, bsz),
        in_specs=[
            pl.BlockSpec((None, seq, LANES), lambda k, b: (b, 0, col0 + k)),
            pl.BlockSpec((None, LANES, ns2), lambda k, b: (k, 0, 0)),
            pl.BlockSpec((None, ns2, LANES), lambda k, b: (k, 0, 0)),
            pl.BlockSpec((None, SUBLANES, SLAB_STATE), lambda k, b: (k, 0, 0)),
            pl.BlockSpec((None, SUBLANES, SLAB_STATE), lambda k, b: (k, 0, 0)),
            pl.BlockSpec((1, LANES), lambda k, b: (0, k)),
            pl.BlockSpec((None, None, 1, ns2), lambda k, b: (b, k, 0, 0)),
        ],
        out_specs=[
            pl.BlockSpec((None, seq, LANES), lambda k, b: (b, 0, k)),
            pl.BlockSpec((None, None, 1, ns2), lambda k, b: (b, k, 0, 0)),
        ],
        out_shape=[
            jax.ShapeDtypeStruct((bsz, seq, n_slab * LANES), F32),
            jax.ShapeDtypeStruct((bsz, n_slab, 1, ns2), F32),
        ],
        scratch_shapes=[
            pltpu.VMEM((seq, ns2), F32),
            pltpu.VMEM((LANES, ns2), BF16), pltpu.VMEM((LANES, ns2), BF16),
            pltpu.VMEM((ns2, LANES), BF16),
        ],
        compiler_params=_params(("arbitrary", "arbitrary"), 40),
        name="s5_scan",
    )(z, b_blk, c_blk, pw_re, pw_im, d_skip.reshape(1, -1), h0)


def _s5_seg_kernel(u_ref, bblk_ref, cblk_ref, pwr_ref, pwi_ref, d_ref, h0_ref,
                   y_ref, hl_ref, up_scr, h_scr, yp_scr, w_scr, bh_scr, ch_scr, *, seq, row_chunk):
    ns = SLAB_STATE
    nseg = SUBLANES
    seg = seq // nseg
    n_lb = ns // LANES
    full = (SUBLANES, LANES)
    lanes = [(slice(c * LANES, (c + 1) * LANES), slice(ns + c * LANES, ns + (c + 1) * LANES)) for c in range(n_lb)]
    last = slice(SUBLANES - 1, SUBLANES)

    @pl.when(pl.program_id(1) == 0)
    def _():
        bh_scr[...] = bblk_ref[...].astype(BF16)
        ch_scr[...] = cblk_ref[...].astype(BF16)
        for re_l, im_l in lanes:
            p_r, p_i = pwr_ref[:, re_l], pwi_ref[:, re_l]
            p8_r, p8_i = jnp.broadcast_to(p_r[last], full), jnp.broadcast_to(p_i[last], full)
            w_scr[0:SUBLANES, re_l], w_scr[0:SUBLANES, im_l] = p_r, p_i

            def grow(gi, carry, re_l=re_l, im_l=im_l, p8_r=p8_r, p8_i=p8_i):
                w_r, w_i = carry
                w_r, w_i = w_r * p8_r - w_i * p8_i, w_r * p8_i + w_i * p8_r
                sl = pl.ds(pl.multiple_of(gi * SUBLANES, SUBLANES), SUBLANES)
                w_scr[sl, re_l], w_scr[sl, im_l] = w_r, w_i
                return w_r, w_i

            lax.fori_loop(1, seg // SUBLANES, grow, (p_r, p_i))

    for s in range(nseg):
        up_scr[pl.ds(s, seg, stride=nseg), :] = u_ref[s * seg:(s + 1) * seg, :].astype(F32)

    def proj_in(r, c):
        sl = pl.ds(pl.multiple_of(r * row_chunk, row_chunk), row_chunk)
        h_scr[sl, :] = _dot(up_scr[sl, :].astype(BF16), bh_scr[...])
        return c

    lax.fori_loop(0, seq // row_chunk, proj_in, 0, unroll=True)

    lam = [(jnp.broadcast_to(pwr_ref[0:1, re_l], full), jnp.broadcast_to(pwi_ref[0:1, re_l], full))
           for re_l, _ in lanes]
    per_it = 2

    def scan_t(it, carry):
        base = pl.multiple_of(it * (per_it * SUBLANES), per_it * SUBLANES)
        sls = [pl.ds(base + j * SUBLANES, SUBLANES) for j in range(per_it)]
        bu = [[(h_scr[sl, re_l], h_scr[sl, im_l]) for re_l, im_l in lanes] for sl in sls]
        hs, outs = list(carry), []
        for j in range(per_it):
            hs = [(bu[j][c][0] + (hs[c][0] * lam[c][0] - hs[c][1] * lam[c][1]),
                   bu[j][c][1] + (hs[c][0] * lam[c][1] + hs[c][1] * lam[c][0])) for c in range(n_lb)]
            outs.append(hs)
        for sl, row in zip(sls, outs):
            for (re_l, im_l), (h_r, h_i) in zip(lanes, row):
                h_scr[sl, re_l], h_scr[sl, im_l] = h_r, h_i
        return tuple(hs)

    zero = jnp.zeros(full, F32)
    ends = lax.fori_loop(0, seg // per_it, scan_t, tuple((zero, zero) for _ in range(n_lb)), unroll=True)

    rowid = lax.broadcasted_iota(jnp.int32, full, 0)
    enter = []
    for (re_l, im_l), (e_r, e_i) in zip(lanes, ends):
        ws_r, ws_i = w_scr[seg - 1:seg, re_l], w_scr[seg - 1:seg, im_l]
        c_r, c_i = h0_ref[:, re_l], h0_ref[:, im_l]
        cv_r, cv_i = jnp.broadcast_to(c_r, full), jnp.broadcast_to(c_i, full)
        for s in range(1, nseg + 1):
            c_r, c_i = (e_r[s - 1:s] + (c_r * ws_r - c_i * ws_i), e_i[s - 1:s] + (c_r * ws_i + c_i * ws_r))
            if s < nseg:
                cv_r = jnp.where(rowid == s, jnp.broadcast_to(c_r, full), cv_r)
                cv_i = jnp.where(rowid == s, jnp.broadcast_to(c_i, full), cv_i)
        hl_ref[:, re_l], hl_ref[:, im_l] = c_r, c_i
        enter.append((cv_r, cv_i))

    def fix_t(gi, c):
        wsl = pl.ds(pl.multiple_of(gi * SUBLANES, SUBLANES), SUBLANES)
        wv = [(w_scr[wsl, re_l], w_scr[wsl, im_l]) for re_l, im_l in lanes]
        base = pl.multiple_of(gi * (SUBLANES * SUBLANES), SUBLANES * SUBLANES)
        for j in range(SUBLANES):
            sl = pl.ds(base + j * SUBLANES, SUBLANES)
            for (re_l, im_l), (cv_r, cv_i), (wv_r, wv_i) in zip(lanes, enter, wv):
                w_r = jnp.broadcast_to(wv_r[j:j + 1], full)
                w_i = jnp.broadcast_to(wv_i[j:j + 1], full)
                h_r = h_scr[sl, re_l] + (w_r * cv_r - w_i * cv_i)
                h_i = h_scr[sl, im_l] + (w_r * cv_i + w_i * cv_r)
                h_scr[sl, re_l], h_scr[sl, im_l] = h_r, h_i
        return c

    lax.fori_loop(0, seg // SUBLANES, fix_t, 0, unroll=True)

    def proj_out(r, c):
        sl = pl.ds(pl.multiple_of(r * row_chunk, row_chunk), row_chunk)
        y = _dot(h_scr[sl, :].astype(BF16), ch_scr[...]) + d_ref[...] * up_scr[sl, :]
        yp_scr[sl, :] = _gelu(y)
        return c

    lax.fori_loop(0, seq // row_chunk, proj_out, 0, unroll=True)

    for s in range(nseg):
        y_ref[s * seg:(s + 1) * seg, :] = yp_scr[pl.ds(s, seg, stride=nseg), :]


def s5_scan_long(z, col0, b_blk, c_blk, pw_re, pw_im, d_skip, h0):
    bsz, seq, _ = z.shape
    n_slab = b_blk.shape[0]
    ns2 = 2 * SLAB_STATE
    row_chunk = min(seq, 1024)
    assert seq % row_chunk == 0 and seq % (2 * SUBLANES * SUBLANES) == 0
    kern = functools.partial(_s5_seg_kernel, seq=seq, row_chunk=row_chunk)
    return pl.pallas_call(
        kern,
        grid=(n_slab, bsz),
        in_specs=[
            pl.BlockSpec((None, seq, LANES), lambda k, b: (b, 0, col0 + k)),
            pl.BlockSpec((None, LANES, ns2), lambda k, b: (k, 0, 0)),
            pl.BlockSpec((None, ns2, LANES), lambda k, b: (k, 0, 0)),
            pl.BlockSpec((None, SUBLANES, SLAB_STATE), lambda k, b: (k, 0, 0)),
            pl.BlockSpec((None, SUBLANES, SLAB_STATE), lambda k, b: (k, 0, 0)),
            pl.BlockSpec((1, LANES), lambda k, b: (0, k)),
            pl.BlockSpec((None, None, 1, ns2), lambda k, b: (b, k, 0, 0)),
        ],
        out_specs=[
            pl.BlockSpec((None, seq, LANES), lambda k, b: (b, 0, k)),
            pl.BlockSpec((None, None, 1, ns2), lambda k, b: (b, k, 0, 0)),
        ],
        out_shape=[
            jax.ShapeDtypeStruct((bsz, seq, n_slab * LANES), F32),
            jax.ShapeDtypeStruct((bsz, n_slab, 1, ns2), F32),
        ],
        scratch_shapes=[
            pltpu.VMEM((seq, LANES), F32),
            pltpu.VMEM((seq, ns2), F32),
            pltpu.VMEM((seq, LANES), F32),
            pltpu.VMEM((seq // SUBLANES, ns2), F32),
            pltpu.VMEM((LANES, ns2), BF16),
            pltpu.VMEM((ns2, LANES), BF16),
        ],
        compiler_params=_params(("arbitrary", "arbitrary"), 40),
        name="s5_scan_long",
    )(z, b_blk, c_blk, pw_re, pw_im, d_skip.reshape(1, -1), h0)


def _glu_kernel(yk_ref, yj_ref, w_ref, b_ref, o_ref, yb_scr):
    @pl.when(pl.program_id(1) == 0)
    def _():
        yb_scr[...] = yk_ref[...].astype(BF16)

    gate = _dot(yb_scr[...], w_ref[...]) + b_ref[...]
    o_ref[...] = (yj_ref[...] * _sigmoid(gate)).astype(o_ref.dtype)


def glu(y, w, b, layer, *, tm, tn):
    t, d = y.shape
    tm = min(tm, t)
    return pl.pallas_call(
        _glu_kernel,
        grid=(t // tm, d // tn),
        in_specs=[
            pl.BlockSpec((tm, d), lambda i, j: (i, 0)),
            pl.BlockSpec((tm, tn), lambda i, j: (i, j)),
            pl.BlockSpec((None, d, tn), lambda i, j: (layer, 0, j)),
            pl.BlockSpec((1, tn), lambda i, j: (0, j)),
        ],
        out_specs=pl.BlockSpec((tm, tn), lambda i, j: (i, j)),
        out_shape=jax.ShapeDtypeStruct((t, d), BF16),
        scratch_shapes=[pltpu.VMEM((tm, d), BF16)],
        compiler_params=_params(("arbitrary", "arbitrary"), 40),
        name="glu",
    )(y, y, w, b.reshape(1, d))


def _pool_kernel(u_ref, buf_ref, w_ref, s_ref, y_ref, tail_ref, ext_scr, *, tc, start_pos, cg):
    c = pl.program_id(1)

    @pl.when(c == 0)
    def _():
        ext_scr[0:POOL_HIST, :] = buf_ref[...]

    ext_scr[POOL_HIST:POOL_HIST + tc, :] = u_ref[...].astype(F32)
    pos = start_pos + c * tc + lax.broadcasted_iota(jnp.int32, (tc, 1), 0)
    for g, win in enumerate(POOL_WINDOWS):
        cols = slice(g * cg, (g + 1) * cg)
        x = ext_scr[:, cols]
        acc, dist = x, 1
        while dist < win:
            acc = acc + pltpu.roll(acc, dist, 0)
            dist *= 2
        wsum = acc[POOL_HIST:, :]
        cnt = jnp.minimum(pos + 1, win).astype(F32)
        zg = wsum * (1.0 / cnt) - x[POOL_HIST:, :]
        y = _dot(zg.astype(BF16), w_ref[g]) * s_ref[:, cols]
        y_ref[:, cols] = y.astype(y_ref.dtype)

    tail = ext_scr[tc:tc + POOL_HIST, :]
    ext_scr[0:POOL_HIST, :] = tail

    @pl.when(c == pl.num_programs(1) - 1)
    def _():
        tail_ref[...] = tail


def pool(z, colblk, buf16, w, layer, scale, *, start_pos, tc):
    bsz, seq, _ = z.shape
    _, n_g, cg, _ = w.shape
    db = n_g * cg
    tc = min(tc, seq)
    kern = functools.partial(_pool_kernel, tc=tc, start_pos=start_pos, cg=cg)
    return pl.pallas_call(
        kern,
        grid=(bsz, seq // tc),
        in_specs=[
            pl.BlockSpec((None, tc, db), lambda b, c: (b, c, colblk)),
            pl.BlockSpec((None, POOL_HIST, db), lambda b, c: (b, 0, 0)),
            pl.BlockSpec((None, n_g, cg, cg), lambda b, c: (layer, 0, 0, 0)),
            pl.BlockSpec((1, db), lambda b, c: (0, 0)),
        ],
        out_specs=[
            pl.BlockSpec((None, tc, db), lambda b, c: (b, c, 0)),
            pl.BlockSpec((None, POOL_HIST, db), lambda b, c: (b, 0, 0)),
        ],
        out_shape=[
            jax.ShapeDtypeStruct((bsz, seq, db), BF16),
            jax.ShapeDtypeStruct((bsz, POOL_HIST, db), F32),
        ],
        scratch_shapes=[pltpu.VMEM((POOL_HIST + tc, db), F32)],
        compiler_params=_params(("arbitrary", "arbitrary"), 40),
        name="pool",
    )(z, buf16, w, scale.reshape(1, db))


def _head_rms(x, g):
    ms = jnp.mean(x * x, axis=-1, keepdims=True)
    return (x * lax.rsqrt(ms + RMS_EPS)) * g


def _combine(os_, lses):
    m = jnp.maximum(jnp.maximum(lses[0], lses[1]), lses[2])
    ws = [jnp.exp(l - m) for l in lses]
    tot = ws[0] + ws[1] + ws[2]
    return (ws[0] * os_[0] + ws[1] * os_[1] + ws[2] * os_[2]) / tot


def _attn_prompt_kernel(q_ref, k_ref, v_ref, qn_ref, kn_ref, att_ref, ko_ref, vo_ref,
                        qs_scr, qf_scr, kf_scr, vf_scr, qd_scr, kd_scr, vd_scr, s_scr, p_scr, m_scr, o_scr, l_scr,
                        *, seq, scale):
    blk = ATT_BLOCK
    rows = 256
    n_all = seq // blk

    def prep(r, c):
        sl = pl.ds(pl.multiple_of(r * rows, rows), rows)
        qs_scr[sl, :] = _head_rms(q_ref[sl, :].astype(F32), qn_ref[...]) * scale
        ko_ref[sl, :] = _head_rms(k_ref[sl, :].astype(F32), kn_ref[...])
        vo_ref[sl, :] = v_ref[sl, :].astype(F32)
        return c

    lax.fori_loop(0, seq // rows, prep, 0, unroll=True)
    kd_scr[0:blk, :] = jnp.zeros((blk, LANES), BF16)
    vd_scr[0:blk, :] = jnp.zeros((blk, 2 * LANES), BF16)
    vd_scr[blk:, LANES:] = jnp.ones((seq, LANES), BF16)

    qi = lax.broadcasted_iota(jnp.int32, (blk, blk), 0)
    kj = lax.broadcasted_iota(jnp.int32, (blk, blk), 1)
    cur_ok = kj <= qi
    prev_ok = kj >= qi
    band_ok = jnp.concatenate([prev_ok, cur_ok], axis=1)
    in_cur = lax.broadcasted_iota(jnp.int32, (blk, 2 * blk), 1) >= blk

    for g, (window, dil) in enumerate(BRANCHES):
        n_blk = seq // (dil * blk)
        col0 = 0 if n_blk > 1 else LANES

        def place(idx, dil=dil, n_blk=n_blk):
            res = idx // n_blk
            n = idx - res * n_blk
            start = res + n * (dil * blk)
            nat = pl.ds(start, blk, stride=dil) if dil > 1 else pl.ds(pl.multiple_of(start, blk), blk)
            cur = pl.ds(pl.multiple_of(idx * blk, blk), blk)
            kcur = pl.ds(pl.multiple_of((idx + 1) * blk, blk), blk)
            kwin = pl.ds(pl.multiple_of(idx * blk, blk), 2 * blk)
            return nat, cur, kcur, kwin, n

        keep_f32 = dil == BRANCHES[1][1]
        two_level = g == 2 and dil == BRANCHES[1][1] ** 2

        def gather(idx, c, place=place, dil=dil, n_blk=n_blk, keep_f32=keep_f32, two_level=two_level):
            nat, cur, kcur, _, n = place(idx)
            if two_level:
                mid = BRANCHES[1][1]
                res = idx // n_blk
                start = (res % mid) * (seq // mid) + res // mid + n * (mid * blk)
                src = pl.ds(start, blk, stride=mid)
                q, k, v = qf_scr[src, :], kf_scr[src, :], vf_scr[src, :]
            else:
                q, k, v = qs_scr[nat, :], ko_ref[nat, :], vo_ref[nat, :]
            if keep_f32:
                qf_scr[cur, :], kf_scr[cur, :], vf_scr[cur, :] = q, k, v
            qd_scr[cur, :] = q.astype(BF16)
            kd_scr[kcur, :] = k.astype(BF16)
            vd_scr[kcur, 0:LANES] = v.astype(BF16)
            return c

        lax.fori_loop(0, n_all, gather, 0, unroll=True)

        def scores(idx, c, place=place, n_blk=n_blk):
            _, cur, kcur, kwin, n = place(idx)
            q = qd_scr[cur, :]
            if n_blk > 1:
                ok = jnp.logical_and(band_ok, jnp.logical_or(in_cur, n > 0))
                s_scr[idx] = jnp.where(ok, _dot_nt(q, kd_scr[kwin, :]), NEG_INF)
            else:
                s_scr[idx, :, LANES:] = jnp.where(cur_ok, _dot_nt(q, kd_scr[kcur, :]), NEG_INF)
            return c

        lax.fori_loop(0, n_all, scores, 0, unroll=True)

        def softmax(idx, c, col0=col0):
            s = s_scr[idx, :, col0:]
            m = jnp.max(s, axis=-1, keepdims=True)
            p_scr[idx, :, col0:] = jnp.exp(s - m).astype(BF16)
            m_scr[idx] = jnp.broadcast_to(m, (blk, LANES))
            return c

        lax.fori_loop(0, n_all, softmax, 0, unroll=True)

        def values(idx, c, g=g, place=place, n_blk=n_blk):
            nat, _, kcur, kwin, _ = place(idx)
            if n_blk > 1:
                ov = _dot(p_scr[idx], vd_scr[kwin, :])
            else:
                ov = _dot(p_scr[idx, :, LANES:], vd_scr[kcur, :])
            l = ov[:, LANES:]
            o_scr[g, nat, :] = ov[:, 0:LANES] / l
            l_scr[g, nat, :] = m_scr[idx] + jnp.log(l)
            return c

        lax.fori_loop(0, n_all, values, 0, unroll=True)

    def comb(r, c):
        sl = pl.ds(pl.multiple_of(r * rows, rows), rows)
        out = _combine([o_scr[g, sl, :] for g in range(3)], [l_scr[g, sl, :] for g in range(3)])
        att_ref[sl, :] = out.astype(att_ref.dtype)
        return c

    lax.fori_loop(0, seq // rows, comb, 0, unroll=True)


def attn_prompt(z, qn, kn, *, n_heads):
    bsz, seq, _ = z.shape
    assert seq % (BRANCHES[-1][1] * ATT_BLOCK) == 0
    hd = LANES
    kern = functools.partial(_attn_prompt_kernel, seq=seq, scale=hd ** -0.5)
    blk = lambda off: pl.BlockSpec((None, seq, hd), lambda b, h: (b, 0, off + h))
    return pl.pallas_call(
        kern,
        grid=(bsz, n_heads),
        in_specs=[blk(0), blk(n_heads), blk(2 * n_heads),
                  pl.BlockSpec((1, hd), lambda b, h: (0, 0)), pl.BlockSpec((1, hd), lambda b, h: (0, 0))],
        out_specs=[blk(0), blk(0), blk(0)],
        out_shape=[
            jax.ShapeDtypeStruct((bsz, seq, n_heads * hd), BF16),
            jax.ShapeDtypeStruct((bsz, seq, n_heads * hd), F32),
            jax.ShapeDtypeStruct((bsz, seq, n_heads * hd), F32),
        ],
        scratch_shapes=[
            pltpu.VMEM((seq, hd), F32),
            pltpu.VMEM((seq, hd), F32), pltpu.VMEM((seq, hd), F32), pltpu.VMEM((seq, hd), F32),
            pltpu.VMEM((seq, hd), BF16), pltpu.VMEM((seq + ATT_BLOCK, hd), BF16),
            pltpu.VMEM((seq + ATT_BLOCK, 2 * hd), BF16),
            pltpu.VMEM((seq // ATT_BLOCK, ATT_BLOCK, 2 * ATT_BLOCK), F32),
            pltpu.VMEM((seq // ATT_BLOCK, ATT_BLOCK, 2 * ATT_BLOCK), BF16),
            pltpu.VMEM((seq // ATT_BLOCK, ATT_BLOCK, hd), F32),
            pltpu.VMEM((3, seq, hd), F32),
            pltpu.VMEM((3, seq, hd), F32),
        ],
        compiler_params=_params(("arbitrary", "arbitrary"), 40),
        name="attn_prompt",
    )(z, z, z, qn.reshape(1, hd), kn.reshape(1, hd))


SAMPLE_PAD = 16


def _attn_sample_kernel(q_ref, k_ref, v_ref, ck_ref, cv_ref, qn_ref, kn_ref, att_ref, ko_ref, vo_ref,
                        q_scr, kn_scr, vn_scr, s_scr, v_scr, o_scr, k4_scr, v4_scr,
                        *, s_new, n_buf, n_heads, pc, scale):
    c = pl.program_id(1)
    n_ch = n_buf // pc
    pad = SAMPLE_PAD

    @pl.when(c == 0)
    def _():
        q_scr[...] = jnp.zeros_like(q_scr)
        kn_scr[...] = jnp.zeros_like(kn_scr)
        vn_scr[...] = jnp.zeros_like(vn_scr)
        for h in range(n_heads):
            lanes = slice(h * LANES, (h + 1) * LANES)
            k_new = _head_rms(k_ref[:, lanes].astype(F32), kn_ref[...])
            v_new = v_ref[:, lanes].astype(F32)
            ko_ref[:, lanes] = k_new
            vo_ref[:, lanes] = v_new
            q_scr[h, 0:s_new, :] = _head_rms(q_ref[:, lanes].astype(F32), qn_ref[...]) * scale
            kn_scr[h, 0:s_new, :] = k_new
            vn_scr[h, 0:s_new, :] = v_new

    mid = math.isqrt(n_heads)
    two_level = mid > 1 and mid * mid == n_heads and mid % 8 != 0
    if two_level:
        part = pc * n_heads // mid
        for a in range(mid):
            k4_scr[a] = ck_ref[pl.ds(a, part, stride=mid), :]
            v4_scr[a] = cv_ref[pl.ds(a, part, stride=mid), :]
    dst = pl.ds(pl.multiple_of(c * pc, pc), pc)
    for h in range(n_heads):
        if two_level:
            rows = pl.ds(h // mid, pc, stride=mid)
            k_h, v_h = k4_scr[h % mid, rows, :], v4_scr[h % mid, rows, :]
        else:
            rows = pl.ds(h, pc, stride=n_heads)
            k_h, v_h = ck_ref[rows, :], cv_ref[rows, :]
        s_scr[h, c] = _dot_nt(q_scr[h].astype(BF16), k_h.astype(BF16))
        v_scr[h, dst, :] = v_h.astype(BF16)

    @pl.when(c == n_ch - 1)
    def _():
        qi = lax.broadcasted_iota(jnp.int32, (pad, pc), 0)
        kj = lax.broadcasted_iota(jnp.int32, (pad, pc), 1)
        qi_n = lax.broadcasted_iota(jnp.int32, (pad, pad), 0)
        kj_n = lax.broadcasted_iota(jnp.int32, (pad, pad), 1)
        dist_n = qi_n - kj_n
        new_ok = jnp.logical_and(dist_n >= 0, kj_n < s_new)

        def finish(h, carry):
            q = q_scr[h].astype(BF16)
            s_n = _dot_nt(q, kn_scr[h].astype(BF16))
            s_c = [s_scr[h, cc] for cc in range(n_ch)]
            ps, pns, ls, ms = [], [], [], []
            for window, dil in BRANCHES:
                msk = []
                for cc in range(n_ch):
                    dist = n_buf + qi - (cc * pc + kj)
                    ok = jnp.logical_and((dist & (dil - 1)) == 0, dist <= window)
                    msk.append(jnp.where(ok, s_c[cc], NEG_INF))
                m_n = jnp.where(jnp.logical_and(new_ok, (dist_n & (dil - 1)) == 0), s_n, NEG_INF)
                m = jnp.max(m_n, axis=-1, keepdims=True)
                for cc in range(n_ch):
                    m = jnp.maximum(m, jnp.max(msk[cc], axis=-1, keepdims=True))
                p_n = jnp.exp(m_n - m)
                l = jnp.sum(p_n, axis=-1, keepdims=True)
                pb = []
                for cc in range(n_ch):
                    p = jnp.exp(msk[cc] - m)
                    l = l + jnp.sum(p, axis=-1, keepdims=True)
                    pb.append(p.astype(BF16))
                ps.append(pb)
                pns.append(p_n.astype(BF16))
                ls.append(l)
                ms.append(m)
            ov = _dot(jnp.concatenate(pns, axis=0), vn_scr[h].astype(BF16))
            for cc in range(n_ch):
                ov = ov + _dot(jnp.concatenate([ps[g][cc] for g in range(3)], axis=0),
                               v_scr[h, cc * pc:(cc + 1) * pc, :])
            outs = [ov[g * pad:(g + 1) * pad] / ls[g] for g in range(3)]
            lses = [jnp.broadcast_to(ms[g] + jnp.log(ls[g]), (pad, LANES)) for g in range(3)]
            o_scr[h] = _combine(outs, lses)
            return carry

        lax.fori_loop(0, n_heads, finish, 0, unroll=4)
        for h in range(n_heads):
            att_ref[:, h * LANES:(h + 1) * LANES] = o_scr[h, 0:s_new, :].astype(att_ref.dtype)


def attn_sample(z, cache_k, cache_v, row0, qn, kn, *, n_heads):
    bsz, s_new, _ = z.shape
    hd = LANES
    d_c = n_heads * hd
    n_buf = cache_k.shape[1] // n_heads
    assert n_buf >= BRANCHES[-1][0] and s_new <= SAMPLE_PAD
    pc = min(n_buf, 512)
    mid = math.isqrt(n_heads)
    kern = functools.partial(_attn_sample_kernel, s_new=s_new, n_buf=n_buf, n_heads=n_heads, pc=pc,
                             scale=hd ** -0.5)
    blk = lambda off: pl.BlockSpec((None, s_new, d_c), lambda b, c: (b, 0, off))
    cblk = pl.BlockSpec((None, pc * n_heads, hd), lambda b, c: (row0 + b, c, 0))
    vec = pl.BlockSpec((1, hd), lambda b, c: (0, 0))
    return pl.pallas_call(
        kern,
        grid=(bsz, n_buf // pc),
        in_specs=[blk(0), blk(1), blk(2), cblk, cblk, vec, vec],
        out_specs=[blk(0), blk(0), blk(0)],
        out_shape=[
            jax.ShapeDtypeStruct((bsz, s_new, d_c), BF16),
            jax.ShapeDtypeStruct((bsz, s_new, d_c), F32),
            jax.ShapeDtypeStruct((bsz, s_new, d_c), F32),
        ],
        scratch_shapes=[
            pltpu.VMEM((n_heads, SAMPLE_PAD, hd), F32),
            pltpu.VMEM((n_heads, SAMPLE_PAD, hd), F32),
            pltpu.VMEM((n_heads, SAMPLE_PAD, hd), F32),
            pltpu.VMEM((n_heads, n_buf // pc, SAMPLE_PAD, pc), F32),
            pltpu.VMEM((n_heads, n_buf, hd), BF16),
            pltpu.VMEM((n_heads, SAMPLE_PAD, hd), F32),
            pltpu.VMEM((mid, pc * n_heads // mid, hd), F32),
            pltpu.VMEM((mid, pc * n_heads // mid, hd), F32),
        ],
        compiler_params=_params(("arbitrary", "arbitrary"), 48),
        name="attn_sample",
    )(z, z, z, cache_k, cache_v, qn.reshape(1, hd), kn.reshape(1, hd))


def _sgu_kernel(gu_ref, gv_ref, lg_ref, lb_ref, w_ref, bt_ref, o_ref, vn_ref, vb_scr, *, rows, n_g, cd):
    t = w_ref.shape[1]
    gv = _gelu(gv_ref[...].astype(F32))
    mu = jnp.mean(gv, axis=-1, keepdims=True)
    xc = gv - mu
    var = jnp.mean(xc * xc, axis=-1, keepdims=True)
    vn = (xc * lax.rsqrt(var + LN_EPS)) * lg_ref[...] + lb_ref[...]
    vn_ref[...] = vn
    if rows < t:
        vb_scr[...] = jnp.zeros_like(vb_scr)
    vb_scr[0:rows, :] = vn.astype(BF16)
    ri = lax.broadcasted_iota(jnp.int32, (t, t), 0)
    ci = lax.broadcasted_iota(jnp.int32, (t, t), 1)
    for g in range(n_g):
        cols = slice(g * cd, (g + 1) * cd)
        wg = jnp.where(ri >= ci, w_ref[g], 0.0).astype(BF16)
        mixed = _dot(wg, vb_scr[:, cols])[0:rows, :] + bt_ref[:, g:g + 1]
        o_ref[:, cols] = (_gelu(gu_ref[:, cols].astype(F32)) * mixed).astype(o_ref.dtype)


def sgu(z, colblk_u, ln_g, ln_b, w_s, b_s):
    bsz, seq, _ = z.shape
    n_g = w_s.shape[0]
    dd = ln_g.shape[0]
    cd = dd // n_g
    t = min(seq, CHUNK)
    tp = max(t, LANES)
    w = jnp.pad(w_s[:, :t, :t], ((0, 0), (0, tp - t), (0, tp - t)))
    bt = jnp.transpose(b_s[:, :t])
    kern = functools.partial(_sgu_kernel, rows=t, n_g=n_g, cd=cd)
    return pl.pallas_call(
        kern,
        grid=(bsz, seq // t),
        in_specs=[
            pl.BlockSpec((None, t, dd), lambda b, c: (b, c, colblk_u)),
            pl.BlockSpec((None, t, dd), lambda b, c: (b, c, colblk_u + 1)),
            pl.BlockSpec((1, dd), lambda b, c: (0, 0)),
            pl.BlockSpec((1, dd), lambda b, c: (0, 0)),
            pl.BlockSpec((n_g, tp, tp), lambda b, c: (0, 0, 0)),
            pl.BlockSpec((t, n_g), lambda b, c: (0, 0)),
        ],
        out_specs=[
            pl.BlockSpec((None, t, dd), lambda b, c: (b, c, 0)),
            pl.BlockSpec((None, t, dd), lambda b, c: (b, c, 0)),
        ],
        out_shape=[
            jax.ShapeDtypeStruct((bsz, seq, dd), BF16),
            jax.ShapeDtypeStruct((bsz, seq, dd), F32),
        ],
        scratch_shapes=[pltpu.VMEM((tp, dd), BF16)],
        compiler_params=_params(("arbitrary", "arbitrary"), 40),
        name="sgu",
    )(z, z, ln_g.reshape(1, dd), ln_b.reshape(1, dd), w, bt)


def _in_proj(x, norm_g, w_in, i, tiles):
    if tiles["cast"]:
        return norm_matmul_cast(x, norm_g, w_in, i, tn=MIX_TILE)
    return norm_matmul(x, norm_g, w_in, tm=tiles["tm"], out_dtype=tiles["z_dtype"]), None


def _res_proj(x, a, b, w_out, i, tiles):
    if tiles["cast"]:
        return out_proj_cast(x, a, b, w_out, i, tn=MIX_TILE)
    return out_proj(x, a, b, w_out, tm=tiles["tm_out"]), None


def _even_layer(x, bsz, seq, h0_re, h0_im, pool_buf, start_pos, norm_g, w_in, w_out, i, s5p, pool_w, pool_scale,
                w_glu, b_glu, d_skip, tiles):
    t, d = x.shape
    pw_re, pw_im, b_blk, c_blk = s5p
    n_slab = b_blk.shape[0]
    d_a = n_slab * LANES
    z, w_in_b = _in_proj(x, norm_g, w_in, i, tiles)
    z = z.reshape(bsz, seq, -1)
    h0 = jnp.concatenate([h0_re.reshape(bsz, n_slab, 1, SLAB_STATE), h0_im.reshape(bsz, n_slab, 1, SLAB_STATE)], axis=-1)
    if tiles["s5_long"]:
        y_pre, h_last = s5_scan_long(z, 0, b_blk, c_blk, pw_re, pw_im, d_skip, h0)
    else:
        y_pre, h_last = s5_scan(z, 0, b_blk, c_blk, pw_re, pw_im, d_skip, h0, split_in=True)
    ya = glu(y_pre.reshape(t, d_a), w_glu, b_glu, i, tm=tiles["tm_glu"], tn=MIX_TILE)
    buf16 = jnp.pad(pool_buf, ((0, 0), (POOL_HIST - pool_buf.shape[1], 0), (0, 0)))
    yb, tail = pool(z, 1, buf16, pool_w, i, pool_scale, start_pos=start_pos, tc=256)
    x, w_out_b = _res_proj(x, ya, yb.reshape(t, -1), w_out, i, tiles)
    g_a = n_slab * SLAB_GROUPS
    h_re = h_last[..., :SLAB_STATE].reshape(bsz, g_a, S5_P)
    h_im = h_last[..., SLAB_STATE:].reshape(bsz, g_a, S5_P)
    return x, h_re, h_im, tail[:, POOL_HIST - pool_buf.shape[1]:], (w_in_b, w_out_b)


def _odd_layer(x, bsz, seq, k_buf, v_buf, norm_g, w_in, w_out, i, qn, kn, ln_g, ln_b, w_s, b_s, n_heads, tiles):
    t, d = x.shape
    d_c = n_heads * LANES
    z, w_in_b = _in_proj(x, norm_g, w_in, i, tiles)
    z = z.reshape(bsz, seq, -1)
    if k_buf is None:
        att, k_new, v_new = attn_prompt(z, qn, kn, n_heads=n_heads)
    else:
        att, k_new, v_new = attn_sample(z, k_buf, v_buf, i * bsz, qn, kn, n_heads=n_heads)
    dd = ln_g.shape[0]
    sg, vn = sgu(z, (3 * d_c) // dd, ln_g, ln_b, w_s, b_s)
    x, w_out_b = _res_proj(x, att.reshape(t, d_c), sg.reshape(t, dd), w_out, i, tiles)
    hd = LANES
    return x, k_new.reshape(bsz, seq, n_heads, hd), v_new.reshape(bsz, seq, n_heads, hd), vn, (w_in_b, w_out_b)


def kernel(x_prompt, x_sample, state_s5_re, state_s5_im, state_pool, cache_k, cache_v, norm_mix, norm_ffn, ev_w_in, ev_w_out, s5_lambda_re, s5_lambda_im, s5_log_dt, s5_b_re, s5_b_im, s5_c_re, s5_c_im, s5_d, s5_w_glu, s5_b_glu, pool_w, pool_scale, od_w_in, od_w_out, q_norm, k_norm, sgu_ln_g, sgu_ln_b, sgu_w, sgu_b, ffn_w1, ffn_w3, ffn_w2):
    bp, lp, d = x_prompt.shape
    bs, ls, _ = x_sample.shape
    depth = norm_mix.shape[0]
    n_heads = cache_k.shape[3]
    xp = x_prompt.reshape(bp * lp, d)
    xs = x_sample.reshape(bs * ls, d)
    tiles_p = dict(cast=False, tm=1024, tm_glu=512, tm_out=1024, s5_long=True, z_dtype=BF16)
    tiles_s = dict(cast=True, tm_glu=bs * ls, s5_long=False)
    g_a, p_a = s5_lambda_re.shape[1:]
    w_glu_b, pool_w_b = s5_w_glu.astype(BF16), pool_w.astype(BF16)
    d_c = n_heads * LANES
    cache_k2 = cache_k.reshape(-1, cache_k.shape[2] * n_heads, LANES)
    cache_v2 = cache_v.reshape(-1, cache_v.shape[2] * n_heads, LANES)

    s5r_p, s5i_p, pool_p, k_p, v_p = [], [], [], [], []
    s5r_s, s5i_s, pool_s, k_s, v_s, sgu_s = [], [], [], [], [], []
    for l in range(depth):
        i = l // 2
        if l % 2 == 0:
            pw_re, pw_im, bb_re, bb_im = s5_prep(s5_lambda_re[i], s5_lambda_im[i], s5_log_dt[i], s5_b_re[i], s5_b_im[i])
            b_blk, c_blk = s5_block_matrices(bb_re, bb_im, s5_c_re[i], s5_c_im[i])
            s5p = (pw_re, pw_im, b_blk, c_blk)
            rest = (i, s5p, pool_w_b, pool_scale[i], w_glu_b, s5_b_glu[i], s5_d[i])
            xs, hr, hi, buf, (w_in_b, w_out_b) = _even_layer(
                xs, bs, ls, state_s5_re[i], state_s5_im[i], state_pool[i], PAST_LEN, norm_mix[l], ev_w_in, ev_w_out,
                *rest, tiles_s)
            s5r_s.append(hr); s5i_s.append(hi); pool_s.append(buf)
            zero_h = jnp.zeros((bp, g_a, p_a), F32)
            zero_buf = jnp.zeros((bp, state_pool.shape[2], state_pool.shape[3]), F32)
            xp, hr, hi, buf, _ = _even_layer(xp, bp, lp, zero_h, zero_h, zero_buf, 0, norm_mix[l], w_in_b, w_out_b,
                                             *rest, tiles_p)
            s5r_p.append(hr); s5i_p.append(hi); pool_p.append(buf)
        else:
            rest = (i, q_norm[i], k_norm[i], sgu_ln_g[i], sgu_ln_b[i], sgu_w[i], sgu_b[i], n_heads)
            xs, nk, nv, vrows, (w_in_b, w_out_b) = _odd_layer(xs, bs, ls, cache_k2, cache_v2, norm_mix[l], od_w_in,
                                                              od_w_out, *rest, tiles_s)
            k_s.append(nk); v_s.append(nv); sgu_s.append(vrows)
            xp, nk, nv, _, _ = _odd_layer(xp, bp, lp, None, None, norm_mix[l], w_in_b, w_out_b, *rest, tiles_p)
            k_p.append(nk); v_p.append(nv)
        if l == 0:
            xs, *ffn_wb = ffn_cast(xs, norm_ffn[l], ffn_w1, ffn_w3, ffn_w2, l, tf=FFN_TILE)
        else:
            xs = ffn(xs, norm_ffn[l], *ffn_wb, tm=bs * ls)
        if l + 1 < depth:
            xp, ffn_wb = ffn(xp, norm_ffn[l], *ffn_wb, tm=512, nxt=(ffn_w1, ffn_w3, ffn_w2, l + 1))
        else:
            xp = ffn(xp, norm_ffn[l], *ffn_wb, tm=512)
    return (xp.reshape(bp, lp, d), xs.reshape(bs, ls, d),
            jnp.stack(s5r_p), jnp.stack(s5i_p), jnp.stack(pool_p), jnp.stack(k_p), jnp.stack(v_p),
            jnp.stack(s5r_s), jnp.stack(s5i_s), jnp.stack(pool_s), jnp.stack(k_s), jnp.stack(v_s),
            jnp.stack(sgu_s))
```

```python
import functools
import math

import jax
import jax.numpy as jnp
from jax import lax
from jax.experimental import pallas as pl
from jax.experimental.pallas import tpu as pltpu

F32 = jnp.float32
BF16 = jnp.bfloat16

RMS_EPS = 1e-6
LN_EPS = 1e-5
NEG_INF = -1e30

LANES = 128
SUBLANES = 8
ATT_BLOCK = 128
CHUNK = 128
POOL_WINDOWS = (2, 4, 8, 16)
POOL_HIST = 16
BRANCHES = ((128, 1), (512, 4), (2048, 16))
S5_GRP = 16
S5_P = 64
SLAB_GROUPS = LANES // S5_GRP
SLAB_STATE = SLAB_GROUPS * S5_P
PAST_LEN = 8192
FFN_TILE = 256
MIX_TILE = 512


def _params(sem, vmem_mib):
    return pltpu.CompilerParams(dimension_semantics=sem, vmem_limit_bytes=vmem_mib << 20)


def _gelu(x):
    return 0.5 * x * (1.0 + lax.erf(x * (1.0 / math.sqrt(2.0))))


def _sigmoid(x):
    return 1.0 / (1.0 + jnp.exp(-x))


def _split_bf16(a):
    hi = a.astype(BF16)
    lo = (a - hi.astype(F32)).astype(BF16)
    return hi, lo


def _dot(a, b):
    return jnp.dot(a, b, preferred_element_type=F32)


def _dot_nt(a, b):
    return lax.dot_general(a, b, (((1,), (1,)), ((), ())), preferred_element_type=F32)


def _rms_rows_to(x_ref, g_ref, h_ref, rows):
    step = 16 if rows % 16 == 0 else rows

    def body(r, c):
        sl = pl.ds(pl.multiple_of(r * step, step), step)
        x = x_ref[sl, :]
        ms = jnp.mean(x * x, axis=-1, keepdims=True)
        h_ref[sl, :] = ((x * lax.rsqrt(ms + RMS_EPS)) * g_ref[...]).astype(h_ref.dtype)
        return c

    lax.fori_loop(0, rows // step, body, 0, unroll=True)


def _norm_matmul_kernel(x_ref, g_ref, w_ref, o_ref, h_ref):
    first = pl.program_id(1) == 0

    def tile():
        o_ref[...] = _dot(h_ref[...], w_ref[...]).astype(o_ref.dtype)

    @pl.when(first)
    def _():
        _rms_rows_to(x_ref, g_ref, h_ref, x_ref.shape[0])
        tile()

    @pl.when(jnp.logical_not(first))
    def _():
        tile()


def norm_matmul(x, g, w, *, tm, out_dtype=F32):
    t, d = x.shape
    n_t, _, tn = w.shape
    return pl.pallas_call(
        _norm_matmul_kernel,
        grid=(t // tm, n_t),
        in_specs=[
            pl.BlockSpec((tm, d), lambda i, j: (i, 0)),
            pl.BlockSpec((1, d), lambda i, j: (0, 0)),
            pl.BlockSpec((None, d, tn), lambda i, j: (j, 0, 0)),
        ],
        out_specs=pl.BlockSpec((tm, tn), lambda i, j: (i, j)),
        out_shape=jax.ShapeDtypeStruct((t, n_t * tn), out_dtype),
        scratch_shapes=[pltpu.VMEM((tm, d), BF16)],
        compiler_params=_params(("arbitrary", "arbitrary"), 58),
        name="norm_matmul",
    )(x, g.reshape(1, d), w)


def _norm_matmul_cast_kernel(x_ref, g_ref, w_ref, o_ref, wb_ref, h_ref):
    @pl.when(pl.program_id(0) == 0)
    def _():
        _rms_rows_to(x_ref, g_ref, h_ref, x_ref.shape[0])

    wb_ref[...] = w_ref[...].astype(BF16)
    o_ref[...] = _dot(h_ref[...], wb_ref[...]).astype(o_ref.dtype)


def norm_matmul_cast(x, g, w, layer, *, tn):
    t, d = x.shape
    n = w.shape[2]
    return pl.pallas_call(
        _norm_matmul_cast_kernel,
        grid=(n // tn,),
        in_specs=[
            pl.BlockSpec((t, d), lambda j: (0, 0)),
            pl.BlockSpec((1, d), lambda j: (0, 0)),
            pl.BlockSpec((None, d, tn), lambda j: (layer, 0, j)),
        ],
        out_specs=[
            pl.BlockSpec((t, tn), lambda j: (0, j)),
            pl.BlockSpec((None, d, tn), lambda j: (j, 0, 0)),
        ],
        out_shape=[
            jax.ShapeDtypeStruct((t, n), F32),
            jax.ShapeDtypeStruct((n // tn, d, tn), BF16),
        ],
        scratch_shapes=[pltpu.VMEM((t, d), BF16)],
        compiler_params=_params(("arbitrary",), 48),
        name="norm_matmul_cast",
    )(x, g.reshape(1, d), w)


FFN_OUT_CHUNK = 512


def _ffn_step(first, x_ref, g_ref, w1_ref, w3_ref, w2_ref, o_ref, h_ref):
    def tile(acc_ref):
        h = h_ref[...]
        a = _dot(h, w1_ref[...])
        b = _dot(h, w3_ref[...])
        u = ((a * _sigmoid(a)) * b).astype(BF16)
        for c in range(0, o_ref.shape[1], FFN_OUT_CHUNK):
            cols = slice(c, c + FFN_OUT_CHUNK)
            o_ref[:, cols] = acc_ref[:, cols] + _dot(u, w2_ref[:, cols])

    @pl.when(first)
    def _():
        _rms_rows_to(x_ref, g_ref, h_ref, x_ref.shape[0])
        tile(x_ref)

    @pl.when(jnp.logical_not(first))
    def _():
        tile(o_ref)


def _ffn_kernel(x_ref, g_ref, w1_ref, w3_ref, w2_ref, o_ref, h_ref):
    _ffn_step(pl.program_id(1) == 0, x_ref, g_ref, w1_ref, w3_ref, w2_ref, o_ref, h_ref)


def _ffn_next_kernel(x_ref, g_ref, w1_ref, w3_ref, w2_ref, n1_ref, n3_ref, n2_ref,
                     o_ref, c1_ref, c3_ref, c2_ref, h_ref):
    c1_ref[...] = n1_ref[...].astype(BF16)
    c3_ref[...] = n3_ref[...].astype(BF16)
    c2_ref[...] = n2_ref[...].astype(BF16)
    _ffn_step(pl.program_id(1) == 0, x_ref, g_ref, w1_ref, w3_ref, w2_ref, o_ref, h_ref)


def ffn(x, g, w1, w3, w2, *, tm, nxt=None):
    t, d = x.shape
    n_f, _, tf = w1.shape
    n_m = t // tm
    specs = [
        pl.BlockSpec((tm, d), lambda i, j: (i, 0)),
        pl.BlockSpec((1, d), lambda i, j: (0, 0)),
        pl.BlockSpec((None, d, tf), lambda i, j: (j, 0, 0)),
        pl.BlockSpec((None, d, tf), lambda i, j: (j, 0, 0)),
        pl.BlockSpec((tf, d), lambda i, j: (j, 0)),
    ]
    out_spec = pl.BlockSpec((tm, d), lambda i, j: (i, 0))
    out_shape = jax.ShapeDtypeStruct((t, d), F32)
    common = dict(grid=(n_m, n_f), scratch_shapes=[pltpu.VMEM((tm, d), BF16)],
                  compiler_params=_params(("arbitrary", "arbitrary"), 56))
    if nxt is None:
        return pl.pallas_call(_ffn_kernel, in_specs=specs, out_specs=out_spec, out_shape=out_shape, name="ffn",
                              **common)(x, g.reshape(1, d), w1, w3, w2)
    n1, n3, n2, layer = nxt
    piece = d // n_m
    assert d % n_m == 0 and piece % LANES == 0
    col = pl.BlockSpec((None, piece, tf), lambda i, j: (layer, i, j))
    row = pl.BlockSpec((None, tf, piece), lambda i, j: (layer, j, i))
    col_out = pl.BlockSpec((None, piece, tf), lambda i, j: (j, i, 0))
    row_out = pl.BlockSpec((tf, piece), lambda i, j: (j, i))
    out, c1, c3, c2 = pl.pallas_call(
        _ffn_next_kernel,
        in_specs=specs + [col, col, row],
        out_specs=[out_spec, col_out, col_out, row_out],
        out_shape=[out_shape, jax.ShapeDtypeStruct(w1.shape, BF16), jax.ShapeDtypeStruct(w3.shape, BF16),
                   jax.ShapeDtypeStruct(w2.shape, BF16)],
        name="ffn_next", **common,
    )(x, g.reshape(1, d), w1, w3, w2, n1, n3, n2)
    return out, (c1, c3, c2)


def _ffn_cast_kernel(x_ref, g_ref, w1_ref, w3_ref, w2_ref, o_ref, w1b_ref, w3b_ref, w2b_ref, h_ref):
    w1b_ref[...] = w1_ref[...].astype(BF16)
    w3b_ref[...] = w3_ref[...].astype(BF16)
    w2b_ref[...] = w2_ref[...].astype(BF16)
    _ffn_step(pl.program_id(0) == 0, x_ref, g_ref, w1b_ref, w3b_ref, w2b_ref, o_ref, h_ref)


def ffn_cast(x, g, w1, w3, w2, layer, *, tf):
    t, d = x.shape
    f = w1.shape[2]
    return pl.pallas_call(
        _ffn_cast_kernel,
        grid=(f // tf,),
        in_specs=[
            pl.BlockSpec((t, d), lambda j: (0, 0)),
            pl.BlockSpec((1, d), lambda j: (0, 0)),
            pl.BlockSpec((None, d, tf), lambda j: (layer, 0, j)),
            pl.BlockSpec((None, d, tf), lambda j: (layer, 0, j)),
            pl.BlockSpec((None, tf, d), lambda j: (layer, j, 0)),
        ],
        out_specs=[
            pl.BlockSpec((t, d), lambda j: (0, 0)),
            pl.BlockSpec((None, d, tf), lambda j: (j, 0, 0)),
            pl.BlockSpec((None, d, tf), lambda j: (j, 0, 0)),
            pl.BlockSpec((tf, d), lambda j: (j, 0)),
        ],
        out_shape=[
            jax.ShapeDtypeStruct((t, d), F32),
            jax.ShapeDtypeStruct((f // tf, d, tf), BF16),
            jax.ShapeDtypeStruct((f // tf, d, tf), BF16),
            jax.ShapeDtypeStruct((f, d), BF16),
        ],
        scratch_shapes=[pltpu.VMEM((t, d), BF16)],
        compiler_params=_params(("arbitrary",), 48),
        name="ffn_cast",
    )(x, g.reshape(1, d), w1, w3, w2)


def _out_proj_kernel(x_ref, a_ref, b_ref, wa_ref, wb_ref, o_ref):
    o_ref[...] = x_ref[...] + _dot(a_ref[...], wa_ref[...]) + _dot(b_ref[...], wb_ref[...])


def out_proj(x, a, b, w_pair, *, tm):
    t, d = x.shape
    wa, wb = w_pair
    n_t, k, tn = wa.shape
    assert a.shape[1] == k and b.shape[1] == k and wb.shape == wa.shape
    return pl.pallas_call(
        _out_proj_kernel,
        grid=(t // tm, n_t),
        in_specs=[
            pl.BlockSpec((tm, tn), lambda i, j: (i, j)),
            pl.BlockSpec((tm, k), lambda i, j: (i, 0)),
            pl.BlockSpec((tm, k), lambda i, j: (i, 0)),
            pl.BlockSpec((None, k, tn), lambda i, j: (j, 0, 0)),
            pl.BlockSpec((None, k, tn), lambda i, j: (j, 0, 0)),
        ],
        out_specs=pl.BlockSpec((tm, tn), lambda i, j: (i, j)),
        out_shape=jax.ShapeDtypeStruct((t, d), F32),
        compiler_params=_params(("arbitrary", "arbitrary"), 48),
        name="out_proj",
    )(x, a, b, wa, wb)


def _out_proj_cast_kernel(x_ref, a_ref, b_ref, wa_ref, wb_ref, o_ref, wab_ref, wbb_ref):
    wab_ref[...] = wa_ref[...].astype(BF16)
    wbb_ref[...] = wb_ref[...].astype(BF16)
    o_ref[...] = x_ref[...] + _dot(a_ref[...], wab_ref[...]) + _dot(b_ref[...], wbb_ref[...])


def out_proj_cast(x, a, b, w, layer, *, tn):
    t, d = x.shape
    k = a.shape[1]
    assert b.shape[1] == k and w.shape[1] == 2 * k
    out, wa_b, wb_b = pl.pallas_call(
        _out_proj_cast_kernel,
        grid=(d // tn,),
        in_specs=[
            pl.BlockSpec((t, tn), lambda j: (0, j)),
            pl.BlockSpec((t, k), lambda j: (0, 0)),
            pl.BlockSpec((t, k), lambda j: (0, 0)),
            pl.BlockSpec((None, k, tn), lambda j: (layer, 0, j)),
            pl.BlockSpec((None, k, tn), lambda j: (layer, 1, j)),
        ],
        out_specs=[
            pl.BlockSpec((t, tn), lambda j: (0, j)),
            pl.BlockSpec((None, k, tn), lambda j: (j, 0, 0)),
            pl.BlockSpec((None, k, tn), lambda j: (j, 0, 0)),
        ],
        out_shape=[
            jax.ShapeDtypeStruct((t, d), F32),
            jax.ShapeDtypeStruct((d // tn, k, tn), BF16),
            jax.ShapeDtypeStruct((d // tn, k, tn), BF16),
        ],
        compiler_params=_params(("arbitrary",), 48),
        name="out_proj_cast",
    )(x, a, b, w, w)
    return out, (wa_b, wb_b)


def _s5_prep_kernel(lr_ref, li_ref, ldt_ref, lrx_ref, lix_ref, ldtx_ref, br_ref, bi_ref,
                    pwr_ref, pwi_ref, bbr_ref, bbi_ref):
    dt = jnp.exp(ldt_ref[...])
    mag = jnp.exp(lr_ref[...] * dt)
    ang = li_ref[...] * dt
    p_r, p_i = mag * jnp.cos(ang), mag * jnp.sin(ang)
    c_r, c_i = p_r, p_i
    pwr_ref[0], pwi_ref[0] = c_r, c_i
    for j in range(1, SUBLANES):
        c_r, c_i = c_r * p_r - c_i * p_i, c_r * p_i + c_i * p_r
        pwr_ref[j], pwi_ref[j] = c_r, c_i
    lr, li = lrx_ref[...], lix_ref[...]
    dtx = jnp.exp(ldtx_ref[...])
    magx = jnp.exp(lr * dtx)
    angx = li * dtx
    nr, ni = magx * jnp.cos(angx) - 1.0, magx * jnp.sin(angx)
    den = lr * lr + li * li
    qr = (nr * lr + ni * li) / den
    qi = (ni * lr - nr * li) / den
    br, bi = br_ref[...], bi_ref[...]
    bbr_ref[...] = qr * br - qi * bi
    bbi_ref[...] = qr * bi + qi * br


def s5_prep(lam_re, lam_im, log_dt, b_re, b_im):
    g, p = lam_re.shape
    h = b_re.shape[2]
    n_slab = g // SLAB_GROUPS
    slab = lambda a: a.reshape(n_slab, SLAB_GROUPS * p)
    rep = lambda a: jnp.repeat(a, h, axis=1)
    ldt_gp = jnp.broadcast_to(log_dt[:, None], (g, p))
    outs = pl.pallas_call(
        _s5_prep_kernel,
        out_shape=[
            jax.ShapeDtypeStruct((SUBLANES, n_slab, SLAB_GROUPS * p), F32),
            jax.ShapeDtypeStruct((SUBLANES, n_slab, SLAB_GROUPS * p), F32),
            jax.ShapeDtypeStruct((g, p * h), F32),
            jax.ShapeDtypeStruct((g, p * h), F32),
        ],
        name="s5_prep",
    )(slab(lam_re), slab(lam_im), slab(ldt_gp), rep(lam_re), rep(lam_im), rep(ldt_gp),
      b_re.reshape(g, p * h), b_im.reshape(g, p * h))
    pw_re, pw_im, bb_re, bb_im = outs
    pw_re = jnp.transpose(pw_re, (1, 0, 2))
    pw_im = jnp.transpose(pw_im, (1, 0, 2))
    return pw_re, pw_im, bb_re.reshape(g, p, h), bb_im.reshape(g, p, h)


def s5_block_matrices(bb_re, bb_im, c_re, c_im):
    g, p, h = bb_re.shape
    n_slab = g // SLAB_GROUPS
    eye = jnp.eye(SLAB_GROUPS, dtype=F32)

    def in_map(bb):
        t = bb.reshape(n_slab, SLAB_GROUPS, p, h)
        return jnp.einsum("kgph,gj->kghjp", t, eye).reshape(n_slab, SLAB_GROUPS * h, SLAB_GROUPS * p)

    def out_map(c):
        t = c.reshape(n_slab, SLAB_GROUPS, h, p)
        return jnp.einsum("kghp,gj->kgpjh", t, eye).reshape(n_slab, SLAB_GROUPS * p, SLAB_GROUPS * h)

    b_blk = jnp.concatenate([in_map(bb_re), in_map(bb_im)], axis=2)
    c_blk = jnp.concatenate([out_map(c_re), -out_map(c_im)], axis=1)
    return b_blk, c_blk


def _s5_scan_kernel(u_ref, bblk_ref, cblk_ref, pwr_ref, pwi_ref, d_ref, h0_ref,
                    y_ref, hl_ref, h_scr, bh_scr, bl_scr, ch_scr, *, seq, row_chunk, split_in):
    ns = SLAB_STATE

    @pl.when(pl.program_id(1) == 0)
    def _():
        bh, bl = _split_bf16(bblk_ref[...])
        bh_scr[...], bl_scr[...] = bh, bl
        ch_scr[...] = cblk_ref[...].astype(BF16)

    n_chunks = seq // row_chunk

    def proj_in(r, c):
        sl = pl.ds(pl.multiple_of(r * row_chunk, row_chunk), row_chunk)
        if split_in:
            uh, ul = _split_bf16(u_ref[sl, :].astype(F32))
            h_scr[sl, :] = _dot(uh, bh_scr[...]) + _dot(ul, bh_scr[...]) + _dot(uh, bl_scr[...])
        else:
            h_scr[sl, :] = _dot(u_ref[sl, :].astype(BF16), bh_scr[...])
        return c

    lax.fori_loop(0, n_chunks, proj_in, 0)

    rowid = lax.broadcasted_iota(jnp.int32, (SUBLANES, LANES), 0)
    for c in range(ns // LANES):
        re_l = slice(c * LANES, (c + 1) * LANES)
        im_l = slice(ns + c * LANES, ns + (c + 1) * LANES)
        p_r, p_i = pwr_ref[:, re_l], pwi_ref[:, re_l]
        steps = []
        for dist in (1, 2, 4):
            a_r = jnp.where(rowid >= dist, jnp.broadcast_to(p_r[dist - 1:dist], (SUBLANES, LANES)), 0.0)
            a_i = jnp.where(rowid >= dist, jnp.broadcast_to(p_i[dist - 1:dist], (SUBLANES, LANES)), 0.0)
            steps.append((dist, a_r, a_i))
        c_r = jnp.broadcast_to(h0_ref[:, re_l], (SUBLANES, LANES))
        c_i = jnp.broadcast_to(h0_ref[:, im_l], (SUBLANES, LANES))

        last = slice(SUBLANES - 1, SUBLANES)
        full = (SUBLANES, LANES)
        p8_r, p8_i = jnp.broadcast_to(p_r[last], full), jnp.broadcast_to(p_i[last], full)
        n_groups = seq // SUBLANES
        per_it = min(4, n_groups)

        def scan_rows(it, carry, re_l=re_l, im_l=im_l, p_r=p_r, p_i=p_i, p8_r=p8_r, p8_i=p8_i, steps=steps):
            base = pl.multiple_of(it * (per_it * SUBLANES), per_it * SUBLANES)
            sls = [pl.ds(base + j * SUBLANES, SUBLANES) for j in range(per_it)]
            loc = []
            for sl in sls:
                r, i = h_scr[sl, re_l], h_scr[sl, im_l]
                for dist, a_r, a_i in steps:
                    s_r, s_i = pltpu.roll(r, dist, 0), pltpu.roll(i, dist, 0)
                    r, i = r + (s_r * a_r - s_i * a_i), i + (s_r * a_i + s_i * a_r)
                loc.append((r, i))
            c_r, c_i = carry
            outs = []
            for r, i in loc:
                outs.append((r + (c_r * p_r - c_i * p_i), i + (c_r * p_i + c_i * p_r)))
                e_r, e_i = jnp.broadcast_to(r[last], full), jnp.broadcast_to(i[last], full)
                c_r, c_i = e_r + (c_r * p8_r - c_i * p8_i), e_i + (c_r * p8_i + c_i * p8_r)
            for sl, (r, i) in zip(sls, outs):
                h_scr[sl, re_l], h_scr[sl, im_l] = r, i
            return c_r, c_i

        c_r, c_i = lax.fori_loop(0, n_groups // per_it, scan_rows, (c_r, c_i))
        hl_ref[:, re_l] = c_r[0:1]
        hl_ref[:, im_l] = c_i[0:1]

    def proj_out(r, c):
        sl = pl.ds(pl.multiple_of(r * row_chunk, row_chunk), row_chunk)
        y = _dot(h_scr[sl, :].astype(BF16), ch_scr[...]) + d_ref[...] * u_ref[sl, :].astype(F32)
        y_ref[sl, :] = _gelu(y)
        return c

    lax.fori_loop(0, n_chunks, proj_out, 0)


def s5_scan(z, col0, b_blk, c_blk, pw_re, pw_im, d_skip, h0, *, split_in):
    bsz, seq, _ = z.shape
    n_slab = b_blk.shape[0]
    ns2 = 2 * SLAB_STATE
    row_chunk = min(seq, 1024)
    kern = functools.partial(_s5_scan_kernel, seq=seq, row_chunk=row_chunk, split_in=split_in)
    return pl.pallas_call(
        kern,
        grid=(n_slab, bsz),
        in_specs=[
            pl.BlockSpec((None, seq, LANES), lambda k, b: (b, 0, col0 + k)),
            pl.BlockSpec((None, LANES, ns2), lambda k, b: (k, 0, 0)),
            pl.BlockSpec((None, ns2, LANES), lambda k, b: (k, 0, 0)),
            pl.BlockSpec((None, SUBLANES, SLAB_STATE), lambda k, b: (k, 0, 0)),
            pl.BlockSpec((None, SUBLANES, SLAB_STATE), lambda k, b: (k, 0, 0)),
            pl.BlockSpec((1, LANES), lambda k, b: (0, k)),
            pl.BlockSpec((None, None, 1, ns2), lambda k, b: (b, k, 0, 0)),
        ],
        out_specs=[
            pl.BlockSpec((None, seq, LANES), lambda k, b: (b, 0, k)),
            pl.BlockSpec((None, None, 1, ns2), lambda k, b: (b, k, 0, 0)),
        ],
        out_shape=[
            jax.ShapeDtypeStruct((bsz, seq, n_slab * LANES), F32),
            jax.ShapeDtypeStruct((bsz, n_slab, 1, ns2), F32),
        ],
        scratch_shapes=[
            pltpu.VMEM((seq, ns2), F32),
            pltpu.VMEM((LANES, ns2), BF16), pltpu.VMEM((LANES, ns2), BF16),
            pltpu.VMEM((ns2, LANES), BF16),
        ],
        compiler_params=_params(("arbitrary", "arbitrary"), 40),
        name="s5_scan",
    )(z, b_blk, c_blk, pw_re, pw_im, d_skip.reshape(1, -1), h0)


def _s5_seg_kernel(u_ref, bblk_ref, cblk_ref, pwr_ref, pwi_ref, d_ref, h0_ref,
                   y_ref, hl_ref, up_scr, h_scr, yp_scr, w_scr, bh_scr, ch_scr, *, seq, row_chunk):
    ns = SLAB_STATE
    nseg = SUBLANES
    seg = seq // nseg
    n_lb = ns // LANES
    full = (SUBLANES, LANES)
    lanes = [(slice(c * LANES, (c + 1) * LANES), slice(ns + c * LANES, ns + (c + 1) * LANES)) for c in range(n_lb)]
    last = slice(SUBLANES - 1, SUBLANES)

    @pl.when(pl.program_id(1) == 0)
    def _():
        bh_scr[...] = bblk_ref[...].astype(BF16)
        ch_scr[...] = cblk_ref[...].astype(BF16)
        for re_l, im_l in lanes:
            p_r, p_i = pwr_ref[:, re_l], pwi_ref[:, re_l]
            p8_r, p8_i = jnp.broadcast_to(p_r[last], full), jnp.broadcast_to(p_i[last], full)
            w_scr[0:SUBLANES, re_l], w_scr[0:SUBLANES, im_l] = p_r, p_i

            def grow(gi, carry, re_l=re_l, im_l=im_l, p8_r=p8_r, p8_i=p8_i):
                w_r, w_i = carry
                w_r, w_i = w_r * p8_r - w_i * p8_i, w_r * p8_i + w_i * p8_r
                sl = pl.ds(pl.multiple_of(gi * SUBLANES, SUBLANES), SUBLANES)
                w_scr[sl, re_l], w_scr[sl, im_l] = w_r, w_i
                return w_r, w_i

            lax.fori_loop(1, seg // SUBLANES, grow, (p_r, p_i))

    for s in range(nseg):
        up_scr[pl.ds(s, seg, stride=nseg), :] = u_ref[s * seg:(s + 1) * seg, :].astype(F32)

    def proj_in(r, c):
        sl = pl.ds(pl.multiple_of(r * row_chunk, row_chunk), row_chunk)
        h_scr[sl, :] = _dot(up_scr[sl, :].astype(BF16), bh_scr[...])
        return c

    lax.fori_loop(0, seq // row_chunk, proj_in, 0, unroll=True)

    lam = [(jnp.broadcast_to(pwr_ref[0:1, re_l], full), jnp.broadcast_to(pwi_ref[0:1, re_l], full))
           for re_l, _ in lanes]
    per_it = 2

    def scan_t(it, carry):
        base = pl.multiple_of(it * (per_it * SUBLANES), per_it * SUBLANES)
        sls = [pl.ds(base + j * SUBLANES, SUBLANES) for j in range(per_it)]
        bu = [[(h_scr[sl, re_l], h_scr[sl, im_l]) for re_l, im_l in lanes] for sl in sls]
        hs, outs = list(carry), []
        for j in range(per_it):
            hs = [(bu[j][c][0] + (hs[c][0] * lam[c][0] - hs[c][1] * lam[c][1]),
                   bu[j][c][1] + (hs[c][0] * lam[c][1] + hs[c][1] * lam[c][0])) for c in range(n_lb)]
            outs.append(hs)
        for sl, row in zip(sls, outs):
            for (re_l, im_l), (h_r, h_i) in zip(lanes, row):
                h_scr[sl, re_l], h_scr[sl, im_l] = h_r, h_i
        return tuple(hs)

    zero = jnp.zeros(full, F32)
    ends = lax.fori_loop(0, seg // per_it, scan_t, tuple((zero, zero) for _ in range(n_lb)), unroll=True)

    rowid = lax.broadcasted_iota(jnp.int32, full, 0)
    enter = []
    for (re_l, im_l), (e_r, e_i) in zip(lanes, ends):
        ws_r, ws_i = w_scr[seg - 1:seg, re_l], w_scr[seg - 1:seg, im_l]
        c_r, c_i = h0_ref[:, re_l], h0_ref[:, im_l]
        cv_r, cv_i = jnp.broadcast_to(c_r, full), jnp.broadcast_to(c_i, full)
        for s in range(1, nseg + 1):
            c_r, c_i = (e_r[s - 1:s] + (c_r * ws_r - c_i * ws_i), e_i[s - 1:s] + (c_r * ws_i + c_i * ws_r))
            if s < nseg:
                cv_r = jnp.where(rowid == s, jnp.broadcast_to(c_r, full), cv_r)
                cv_i = jnp.where(rowid == s, jnp.broadcast_to(c_i, full), cv_i)
        hl_ref[:, re_l], hl_ref[:, im_l] = c_r, c_i
        enter.append((cv_r, cv_i))

    def fix_t(gi, c):
        wsl = pl.ds(pl.multiple_of(gi * SUBLANES, SUBLANES), SUBLANES)
        wv = [(w_scr[wsl, re_l], w_scr[wsl, im_l]) for re_l, im_l in lanes]
        base = pl.multiple_of(gi * (SUBLANES * SUBLANES), SUBLANES * SUBLANES)
        for j in range(SUBLANES):
            sl = pl.ds(base + j * SUBLANES, SUBLANES)
            for (re_l, im_l), (cv_r, cv_i), (wv_r, wv_i) in zip(lanes, enter, wv):
                w_r = jnp.broadcast_to(wv_r[j:j + 1], full)
                w_i = jnp.broadcast_to(wv_i[j:j + 1], full)
                h_r = h_scr[sl, re_l] + (w_r * cv_r - w_i * cv_i)
                h_i = h_scr[sl, im_l] + (w_r * cv_i + w_i * cv_r)
                h_scr[sl, re_l], h_scr[sl, im_l] = h_r, h_i
        return c

    lax.fori_loop(0, seg // SUBLANES, fix_t, 0, unroll=True)

    def proj_out(r, c):
        sl = pl.ds(pl.multiple_of(r * row_chunk, row_chunk), row_chunk)
        y = _dot(h_scr[sl, :].astype(BF16), ch_scr[...]) + d_ref[...] * up_scr[sl, :]
        yp_scr[sl, :] = _gelu(y)
        return c

    lax.fori_loop(0, seq // row_chunk, proj_out, 0, unroll=True)

    for s in range(nseg):
        y_ref[s * seg:(s + 1) * seg, :] = yp_scr[pl.ds(s, seg, stride=nseg), :]


def s5_scan_long(z, col0, b_blk, c_blk, pw_re, pw_im, d_skip, h0):
    bsz, seq, _ = z.shape
    n_slab = b_blk.shape[0]
    ns2 = 2 * SLAB_STATE
    row_chunk = min(seq, 1024)
    assert seq % row_chunk == 0 and seq % (2 * SUBLANES * SUBLANES) == 0
    kern = functools.partial(_s5_seg_kernel, seq=seq, row_chunk=row_chunk)
    return pl.pallas_call(
        kern,
        grid=(n_slab, bsz),
        in_specs=[
            pl.BlockSpec((None, seq, LANES), lambda k, b: (b, 0, col0 + k)),
            pl.BlockSpec((None, LANES, ns2), lambda k, b: (k, 0, 0)),
            pl.BlockSpec((None, ns2, LANES), lambda k, b: (k, 0, 0)),
            pl.BlockSpec((None, SUBLANES, SLAB_STATE), lambda k, b: (k, 0, 0)),
            pl.BlockSpec((None, SUBLANES, SLAB_STATE), lambda k, b: (k, 0, 0)),
            pl.BlockSpec((1, LANES), lambda k, b: (0, k)),
            pl.BlockSpec((None, None, 1, ns2), lambda k, b: (b, k, 0, 0)),
        ],
        out_specs=[
            pl.BlockSpec((None, seq, LANES), lambda k, b: (b, 0, k)),
            pl.BlockSpec((None, None, 1, ns2), lambda k, b: (b, k, 0, 0)),
        ],
        out_shape=[
            jax.ShapeDtypeStruct((bsz, seq, n_slab * LANES), F32),
            jax.ShapeDtypeStruct((bsz, n_slab, 1, ns2), F32),
        ],
        scratch_shapes=[
            pltpu.VMEM((seq, LANES), F32),
            pltpu.VMEM((seq, ns2), F32),
            pltpu.VMEM((seq, LANES), F32),
            pltpu.VMEM((seq // SUBLANES, ns2), F32),
            pltpu.VMEM((LANES, ns2), BF16),
            pltpu.VMEM((ns2, LANES), BF16),
        ],
        compiler_params=_params(("arbitrary", "arbitrary"), 40),
        name="s5_scan_long",
    )(z, b_blk, c_blk, pw_re, pw_im, d_skip.reshape(1, -1), h0)


def _glu_kernel(yk_ref, yj_ref, w_ref, b_ref, o_ref, yb_scr):
    @pl.when(pl.program_id(1) == 0)
    def _():
        yb_scr[...] = yk_ref[...].astype(BF16)

    gate = _dot(yb_scr[...], w_ref[...]) + b_ref[...]
    o_ref[...] = (yj_ref[...] * _sigmoid(gate)).astype(o_ref.dtype)


def glu(y, w, b, layer, *, tm, tn):
    t, d = y.shape
    tm = min(tm, t)
    return pl.pallas_call(
        _glu_kernel,
        grid=(t // tm, d // tn),
        in_specs=[
            pl.BlockSpec((tm, d), lambda i, j: (i, 0)),
            pl.BlockSpec((tm, tn), lambda i, j: (i, j)),
            pl.BlockSpec((None, d, tn), lambda i, j: (layer, 0, j)),
            pl.BlockSpec((1, tn), lambda i, j: (0, j)),
        ],
        out_specs=pl.BlockSpec((tm, tn), lambda i, j: (i, j)),
        out_shape=jax.ShapeDtypeStruct((t, d), BF16),
        scratch_shapes=[pltpu.VMEM((tm, d), BF16)],
        compiler_params=_params(("arbitrary", "arbitrary"), 40),
        name="glu",
    )(y, y, w, b.reshape(1, d))


def _pool_kernel(u_ref, buf_ref, w_ref, s_ref, y_ref, tail_ref, ext_scr, *, tc, start_pos, cg):
    c = pl.program_id(1)

    @pl.when(c == 0)
    def _():
        ext_scr[0:POOL_HIST, :] = buf_ref[...]

    ext_scr[POOL_HIST:POOL_HIST + tc, :] = u_ref[...].astype(F32)
    pos = start_pos + c * tc + lax.broadcasted_iota(jnp.int32, (tc, 1), 0)
    for g, win in enumerate(POOL_WINDOWS):
        cols = slice(g * cg, (g + 1) * cg)
        x = ext_scr[:, cols]
        acc, dist = x, 1
        while dist < win:
            acc = acc + pltpu.roll(acc, dist, 0)
            dist *= 2
        wsum = acc[POOL_HIST:, :]
        cnt = jnp.minimum(pos + 1, win).astype(F32)
        zg = wsum * (1.0 / cnt) - x[POOL_HIST:, :]
        y = _dot(zg.astype(BF16), w_ref[g]) * s_ref[:, cols]
        y_ref[:, cols] = y.astype(y_ref.dtype)

    tail = ext_scr[tc:tc + POOL_HIST, :]
    ext_scr[0:POOL_HIST, :] = tail

    @pl.when(c == pl.num_programs(1) - 1)
    def _():
        tail_ref[...] = tail


def pool(z, colblk, buf16, w, layer, scale, *, start_pos, tc):
    bsz, seq, _ = z.shape
    _, n_g, cg, _ = w.shape
    db = n_g * cg
    tc = min(tc, seq)
    kern = functools.partial(_pool_kernel, tc=tc, start_pos=start_pos, cg=cg)
    return pl.pallas_call(
        kern,
        grid=(bsz, seq // tc),
        in_specs=[
            pl.BlockSpec((None, tc, db), lambda b, c: (b, c, colblk)),
            pl.BlockSpec((None, POOL_HIST, db), lambda b, c: (b, 0, 0)),
            pl.BlockSpec((None, n_g, cg, cg), lambda b, c: (layer, 0, 0, 0)),
            pl.BlockSpec((1, db), lambda b, c: (0, 0)),
        ],
        out_specs=[
            pl.BlockSpec((None, tc, db), lambda b, c: (b, c, 0)),
            pl.BlockSpec((None, POOL_HIST, db), lambda b, c: (b, 0, 0)),
        ],
        out_shape=[
            jax.ShapeDtypeStruct((bsz, seq, db), BF16),
            jax.ShapeDtypeStruct((bsz, POOL_HIST, db), F32),
        ],
        scratch_shapes=[pltpu.VMEM((POOL_HIST + tc, db), F32)],
        compiler_params=_params(("arbitrary", "arbitrary"), 40),
        name="pool",
    )(z, buf16, w, scale.reshape(1, db))


def _head_rms(x, g):
    ms = jnp.mean(x * x, axis=-1, keepdims=True)
    return (x * lax.rsqrt(ms + RMS_EPS)) * g


def _combine(os_, lses):
    m = jnp.maximum(jnp.maximum(lses[0], lses[1]), lses[2])
    ws = [jnp.exp(l - m) for l in lses]
    tot = ws[0] + ws[1] + ws[2]
    return (ws[0] * os_[0] + ws[1] * os_[1] + ws[2] * os_[2]) / tot


def _attn_prompt_kernel(q_ref, k_ref, v_ref, qn_ref, kn_ref, att_ref, ko_ref, vo_ref,
                        qs_scr, qf_scr, kf_scr, vf_scr, qd_scr, kd_scr, vd_scr, s_scr, p_scr, m_scr, o_scr, l_scr,
                        *, seq, scale):
    blk = ATT_BLOCK
    rows = 256
    n_all = seq // blk

    def prep(r, c):
        sl = pl.ds(pl.multiple_of(r * rows, rows), rows)
        qs_scr[sl, :] = _head_rms(q_ref[sl, :].astype(F32), qn_ref[...]) * scale
        ko_ref[sl, :] = _head_rms(k_ref[sl, :].astype(F32), kn_ref[...])
        vo_ref[sl, :] = v_ref[sl, :].astype(F32)
        return c

    lax.fori_loop(0, seq // rows, prep, 0, unroll=True)
    kd_scr[0:blk, :] = jnp.zeros((blk, LANES), BF16)
    vd_scr[0:blk, :] = jnp.zeros((blk, 2 * LANES), BF16)
    vd_scr[blk:, LANES:] = jnp.ones((seq, LANES), BF16)

    qi = lax.broadcasted_iota(jnp.int32, (blk, blk), 0)
    kj = lax.broadcasted_iota(jnp.int32, (blk, blk), 1)
    cur_ok = kj <= qi
    prev_ok = kj >= qi
    band_ok = jnp.concatenate([prev_ok, cur_ok], axis=1)
    in_cur = lax.broadcasted_iota(jnp.int32, (blk, 2 * blk), 1) >= blk

    for g, (window, dil) in enumerate(BRANCHES):
        n_blk = seq // (dil * blk)
        col0 = 0 if n_blk > 1 else LANES

        def place(idx, dil=dil, n_blk=n_blk):
            res = idx // n_blk
            n = idx - res * n_blk
            start = res + n * (dil * blk)
            nat = pl.ds(start, blk, stride=dil) if dil > 1 else pl.ds(pl.multiple_of(start, blk), blk)
            cur = pl.ds(pl.multiple_of(idx * blk, blk), blk)
            kcur = pl.ds(pl.multiple_of((idx + 1) * blk, blk), blk)
            kwin = pl.ds(pl.multiple_of(idx * blk, blk), 2 * blk)
            return nat, cur, kcur, kwin, n

        keep_f32 = dil == BRANCHES[1][1]
        two_level = g == 2 and dil == BRANCHES[1][1] ** 2

        def gather(idx, c, place=place, dil=dil, n_blk=n_blk, keep_f32=keep_f32, two_level=two_level):
            nat, cur, kcur, _, n = place(idx)
            if two_level:
                mid = BRANCHES[1][1]
                res = idx // n_blk
                start = (res % mid) * (seq // mid) + res // mid + n * (mid * blk)
                src = pl.ds(start, blk, stride=mid)
                q, k, v = qf_scr[src, :], kf_scr[src, :], vf_scr[src, :]
            else:
                q, k, v = qs_scr[nat, :], ko_ref[nat, :], vo_ref[nat, :]
            if keep_f32:
                qf_scr[cur, :], kf_scr[cur, :], vf_scr[cur, :] = q, k, v
            qd_scr[cur, :] = q.astype(BF16)
            kd_scr[kcur, :] = k.astype(BF16)
            vd_scr[kcur, 0:LANES] = v.astype(BF16)
            return c

        lax.fori_loop(0, n_all, gather, 0, unroll=True)

        def scores(idx, c, place=place, n_blk=n_blk):
            _, cur, kcur, kwin, n = place(idx)
            q = qd_scr[cur, :]
            if n_blk > 1:
                ok = jnp.logical_and(band_ok, jnp.logical_or(in_cur, n > 0))
                s_scr[idx] = jnp.where(ok, _dot_nt(q, kd_scr[kwin, :]), NEG_INF)
            else:
                s_scr[idx, :, LANES:] = jnp.where(cur_ok, _dot_nt(q, kd_scr[kcur, :]), NEG_INF)
            return c

        lax.fori_loop(0, n_all, scores, 0, unroll=True)

        def softmax(idx, c, col0=col0):
            s = s_scr[idx, :, col0:]
            m = jnp.max(s, axis=-1, keepdims=True)
            p_scr[idx, :, col0:] = jnp.exp(s - m).astype(BF16)
            m_scr[idx] = jnp.broadcast_to(m, (blk, LANES))
            return c

        lax.fori_loop(0, n_all, softmax, 0, unroll=True)

        def values(idx, c, g=g, place=place, n_blk=n_blk):
            nat, _, kcur, kwin, _ = place(idx)
            if n_blk > 1:
                ov = _dot(p_scr[idx], vd_scr[kwin, :])
            else:
                ov = _dot(p_scr[idx, :, LANES:], vd_scr[kcur, :])
            l = ov[:, LANES:]
            o_scr[g, nat, :] = ov[:, 0:LANES] / l
            l_scr[g, nat, :] = m_scr[idx] + jnp.log(l)
            return c

        lax.fori_loop(0, n_all, values, 0, unroll=True)

    def comb(r, c):
        sl = pl.ds(pl.multiple_of(r * rows, rows), rows)
        out = _combine([o_scr[g, sl, :] for g in range(3)], [l_scr[g, sl, :] for g in range(3)])
        att_ref[sl, :] = out.astype(att_ref.dtype)
        return c

    lax.fori_loop(0, seq // rows, comb, 0, unroll=True)


def attn_prompt(z, qn, kn, *, n_heads):
    bsz, seq, _ = z.shape
    assert seq % (BRANCHES[-1][1] * ATT_BLOCK) == 0
    hd = LANES
    kern = functools.partial(_attn_prompt_kernel, seq=seq, scale=hd ** -0.5)
    blk = lambda off: pl.BlockSpec((None, seq, hd), lambda b, h: (b, 0, off + h))
    return pl.pallas_call(
        kern,
        grid=(bsz, n_heads),
        in_specs=[blk(0), blk(n_heads), blk(2 * n_heads),
                  pl.BlockSpec((1, hd), lambda b, h: (0, 0)), pl.BlockSpec((1, hd), lambda b, h: (0, 0))],
        out_specs=[blk(0), blk(0), blk(0)],
        out_shape=[
            jax.ShapeDtypeStruct((bsz, seq, n_heads * hd), BF16),
            jax.ShapeDtypeStruct((bsz, seq, n_heads * hd), F32),
            jax.ShapeDtypeStruct((bsz, seq, n_heads * hd), F32),
        ],
        scratch_shapes=[
            pltpu.VMEM((seq, hd), F32),
            pltpu.VMEM((seq, hd), F32), pltpu.VMEM((seq, hd), F32), pltpu.VMEM((seq, hd), F32),
            pltpu.VMEM((seq, hd), BF16), pltpu.VMEM((seq + ATT_BLOCK, hd), BF16),
            pltpu.VMEM((seq + ATT_BLOCK, 2 * hd), BF16),
            pltpu.VMEM((seq // ATT_BLOCK, ATT_BLOCK, 2 * ATT_BLOCK), F32),
            pltpu.VMEM((seq // ATT_BLOCK, ATT_BLOCK, 2 * ATT_BLOCK), BF16),
            pltpu.VMEM((seq // ATT_BLOCK, ATT_BLOCK, hd), F32),
            pltpu.VMEM((3, seq, hd), F32),
            pltpu.VMEM((3, seq, hd), F32),
        ],
        compiler_params=_params(("arbitrary", "arbitrary"), 40),
        name="attn_prompt",
    )(z, z, z, qn.reshape(1, hd), kn.reshape(1, hd))


SAMPLE_PAD = 16


def _attn_sample_kernel(q_ref, k_ref, v_ref, ck_ref, cv_ref, qn_ref, kn_ref, att_ref, ko_ref, vo_ref,
                        q_scr, kn_scr, vn_scr, s_scr, v_scr, o_scr, k4_scr, v4_scr,
                        *, s_new, n_buf, n_heads, pc, scale):
    c = pl.program_id(1)
    n_ch = n_buf // pc
    pad = SAMPLE_PAD

    @pl.when(c == 0)
    def _():
        q_scr[...] = jnp.zeros_like(q_scr)
        kn_scr[...] = jnp.zeros_like(kn_scr)
        vn_scr[...] = jnp.zeros_like(vn_scr)
        for h in range(n_heads):
            lanes = slice(h * LANES, (h + 1) * LANES)
            k_new = _head_rms(k_ref[:, lanes].astype(F32), kn_ref[...])
            v_new = v_ref[:, lanes].astype(F32)
            ko_ref[:, lanes] = k_new
            vo_ref[:, lanes] = v_new
            q_scr[h, 0:s_new, :] = _head_rms(q_ref[:, lanes].astype(F32), qn_ref[...]) * scale
            kn_scr[h, 0:s_new, :] = k_new
            vn_scr[h, 0:s_new, :] = v_new

    mid = math.isqrt(n_heads)
    two_level = mid > 1 and mid * mid == n_heads and mid % 8 != 0
    if two_level:
        part = pc * n_heads // mid
        for a in range(mid):
            k4_scr[a] = ck_ref[pl.ds(a, part, stride=mid), :]
            v4_scr[a] = cv_ref[pl.ds(a, part, stride=mid), :]
    dst = pl.ds(pl.multiple_of(c * pc, pc), pc)
    for h in range(n_heads):
        if two_level:
            rows = pl.ds(h // mid, pc, stride=mid)
            k_h, v_h = k4_scr[h % mid, rows, :], v4_scr[h % mid, rows, :]
        else:
            rows = pl.ds(h, pc, stride=n_heads)
            k_h, v_h = ck_ref[rows, :], cv_ref[rows, :]
        s_scr[h, c] = _dot_nt(q_scr[h].astype(BF16), k_h.astype(BF16))
        v_scr[h, dst, :] = v_h.astype(BF16)

    @pl.when(c == n_ch - 1)
    def _():
        qi = lax.broadcasted_iota(jnp.int32, (pad, pc), 0)
        kj = lax.broadcasted_iota(jnp.int32, (pad, pc), 1)
        qi_n = lax.broadcasted_iota(jnp.int32, (pad, pad), 0)
        kj_n = lax.broadcasted_iota(jnp.int32, (pad, pad), 1)
        dist_n = qi_n - kj_n
        new_ok = jnp.logical_and(dist_n >= 0, kj_n < s_new)

        def finish(h, carry):
            q = q_scr[h].astype(BF16)
            s_n = _dot_nt(q, kn_scr[h].astype(BF16))
            s_c = [s_scr[h, cc] for cc in range(n_ch)]
            ps, pns, ls, ms = [], [], [], []
            for window, dil in BRANCHES:
                msk = []
                for cc in range(n_ch):
                    dist = n_buf + qi - (cc * pc + kj)
                    ok = jnp.logical_and((dist & (dil - 1)) == 0, dist <= window)
                    msk.append(jnp.where(ok, s_c[cc], NEG_INF))
                m_n = jnp.where(jnp.logical_and(new_ok, (dist_n & (dil - 1)) == 0), s_n, NEG_INF)
                m = jnp.max(m_n, axis=-1, keepdims=True)
                for cc in range(n_ch):
                    m = jnp.maximum(m, jnp.max(msk[cc], axis=-1, keepdims=True))
                p_n = jnp.exp(m_n - m)
                l = jnp.sum(p_n, axis=-1, keepdims=True)
                pb = []
                for cc in range(n_ch):
                    p = jnp.exp(msk[cc] - m)
                    l = l + jnp.sum(p, axis=-1, keepdims=True)
                    pb.append(p.astype(BF16))
                ps.append(pb)
                pns.append(p_n.astype(BF16))
                ls.append(l)
                ms.append(m)
            ov = _dot(jnp.concatenate(pns, axis=0), vn_scr[h].astype(BF16))
            for cc in range(n_ch):
                ov = ov + _dot(jnp.concatenate([ps[g][cc] for g in range(3)], axis=0),
                               v_scr[h, cc * pc:(cc + 1) * pc, :])
            outs = [ov[g * pad:(g + 1) * pad] / ls[g] for g in range(3)]
            lses = [jnp.broadcast_to(ms[g] + jnp.log(ls[g]), (pad, LANES)) for g in range(3)]
            o_scr[h] = _combine(outs, lses)
            return carry

        lax.fori_loop(0, n_heads, finish, 0, unroll=4)
        for h in range(n_heads):
            att_ref[:, h * LANES:(h + 1) * LANES] = o_scr[h, 0:s_new, :].astype(att_ref.dtype)


def attn_sample(z, cache_k, cache_v, row0, qn, kn, *, n_heads):
    bsz, s_new, _ = z.shape
    hd = LANES
    d_c = n_heads * hd
    n_buf = cache_k.shape[1] // n_heads
    assert n_buf >= BRANCHES[-1][0] and s_new <= SAMPLE_PAD
    pc = min(n_buf, 512)
    mid = math.isqrt(n_heads)
    kern = functools.partial(_attn_sample_kernel, s_new=s_new, n_buf=n_buf, n_heads=n_heads, pc=pc,
                             scale=hd ** -0.5)
    blk = lambda off: pl.BlockSpec((None, s_new, d_c), lambda b, c: (b, 0, off))
    cblk = pl.BlockSpec((None, pc * n_heads, hd), lambda b, c: (row0 + b, c, 0))
    vec = pl.BlockSpec((1, hd), lambda b, c: (0, 0))
    return pl.pallas_call(
        kern,
        grid=(bsz, n_buf // pc),
        in_specs=[blk(0), blk(1), blk(2), cblk, cblk, vec, vec],
        out_specs=[blk(0), blk(0), blk(0)],
        out_shape=[
            jax.ShapeDtypeStruct((bsz, s_new, d_c), BF16),
            jax.ShapeDtypeStruct((bsz, s_new, d_c), F32),
            jax.ShapeDtypeStruct((bsz, s_new, d_c), F32),
        ],
        scratch_shapes=[
            pltpu.VMEM((n_heads, SAMPLE_PAD, hd), F32),
            pltpu.VMEM((n_heads, SAMPLE_PAD, hd), F32),
            pltpu.VMEM((n_heads, SAMPLE_PAD, hd), F32),
            pltpu.VMEM((n_heads, n_buf // pc, SAMPLE_PAD, pc), F32),
            pltpu.VMEM((n_heads, n_buf, hd), BF16),
            pltpu.VMEM((n_heads, SAMPLE_PAD, hd), F32),
            pltpu.VMEM((mid, pc * n_heads // mid, hd), F32),
            pltpu.VMEM((mid, pc * n_heads // mid, hd), F32),
        ],
        compiler_params=_params(("arbitrary", "arbitrary"), 48),
        name="attn_sample",
    )(z, z, z, cache_k, cache_v, qn.reshape(1, hd), kn.reshape(1, hd))


def _sgu_kernel(gu_ref, gv_ref, lg_ref, lb_ref, w_ref, bt_ref, o_ref, vn_ref, vb_scr, *, rows, n_g, cd):
    t = w_ref.shape[1]
    gv = _gelu(gv_ref[...].astype(F32))
    mu = jnp.mean(gv, axis=-1, keepdims=True)
    xc = gv - mu
    var = jnp.mean(xc * xc, axis=-1, keepdims=True)
    vn = (xc * lax.rsqrt(var + LN_EPS)) * lg_ref[...] + lb_ref[...]
    vn_ref[...] = vn
    if rows < t:
        vb_scr[...] = jnp.zeros_like(vb_scr)
    vb_scr[0:rows, :] = vn.astype(BF16)
    ri = lax.broadcasted_iota(jnp.int32, (t, t), 0)
    ci = lax.broadcasted_iota(jnp.int32, (t, t), 1)
    for g in range(n_g):
        cols = slice(g * cd, (g + 1) * cd)
        wg = jnp.where(ri >= ci, w_ref[g], 0.0).astype(BF16)
        mixed = _dot(wg, vb_scr[:, cols])[0:rows, :] + bt_ref[:, g:g + 1]
        o_ref[:, cols] = (_gelu(gu_ref[:, cols].astype(F32)) * mixed).astype(o_ref.dtype)


def sgu(z, colblk_u, ln_g, ln_b, w_s, b_s):
    bsz, seq, _ = z.shape
    n_g = w_s.shape[0]
    dd = ln_g.shape[0]
    cd = dd // n_g
    t = min(seq, CHUNK)
    tp = max(t, LANES)
    w = jnp.pad(w_s[:, :t, :t], ((0, 0), (0, tp - t), (0, tp - t)))
    bt = jnp.transpose(b_s[:, :t])
    kern = functools.partial(_sgu_kernel, rows=t, n_g=n_g, cd=cd)
    return pl.pallas_call(
        kern,
        grid=(bsz, seq // t),
        in_specs=[
            pl.BlockSpec((None, t, dd), lambda b, c: (b, c, colblk_u)),
            pl.BlockSpec((None, t, dd), lambda b, c: (b, c, colblk_u + 1)),
            pl.BlockSpec((1, dd), lambda b, c: (0, 0)),
            pl.BlockSpec((1, dd), lambda b, c: (0, 0)),
            pl.BlockSpec((n_g, tp, tp), lambda b, c: (0, 0, 0)),
            pl.BlockSpec((t, n_g), lambda b, c: (0, 0)),
        ],
        out_specs=[
            pl.BlockSpec((None, t, dd), lambda b, c: (b, c, 0)),
            pl.BlockSpec((None, t, dd), lambda b, c: (b, c, 0)),
        ],
        out_shape=[
            jax.ShapeDtypeStruct((bsz, seq, dd), BF16),
            jax.ShapeDtypeStruct((bsz, seq, dd), F32),
        ],
        scratch_shapes=[pltpu.VMEM((tp, dd), BF16)],
        compiler_params=_params(("arbitrary", "arbitrary"), 40),
        name="sgu",
    )(z, z, ln_g.reshape(1, dd), ln_b.reshape(1, dd), w, bt)


def _in_proj(x, norm_g, w_in, i, tiles):
    if tiles["cast"]:
        return norm_matmul_cast(x, norm_g, w_in, i, tn=MIX_TILE)
    return norm_matmul(x, norm_g, w_in, tm=tiles["tm"], out_dtype=tiles["z_dtype"]), None


def _res_proj(x, a, b, w_out, i, tiles):
    if tiles["cast"]:
        return out_proj_cast(x, a, b, w_out, i, tn=MIX_TILE)
    return out_proj(x, a, b, w_out, tm=tiles["tm_out"]), None


def _even_layer(x, bsz, seq, h0_re, h0_im, pool_buf, start_pos, norm_g, w_in, w_out, i, s5p, pool_w, pool_scale,
                w_glu, b_glu, d_skip, tiles):
    t, d = x.shape
    pw_re, pw_im, b_blk, c_blk = s5p
    n_slab = b_blk.shape[0]
    d_a = n_slab * LANES
    z, w_in_b = _in_proj(x, norm_g, w_in, i, tiles)
    z = z.reshape(bsz, seq, -1)
    h0 = jnp.concatenate([h0_re.reshape(bsz, n_slab, 1, SLAB_STATE), h0_im.reshape(bsz, n_slab, 1, SLAB_STATE)], axis=-1)
    if tiles["s5_long"]:
        y_pre, h_last = s5_scan_long(z, 0, b_blk, c_blk, pw_re, pw_im, d_skip, h0)
    else:
        y_pre, h_last = s5_scan(z, 0, b_blk, c_blk, pw_re, pw_im, d_skip, h0, split_in=True)
    ya = glu(y_pre.reshape(t, d_a), w_glu, b_glu, i, tm=tiles["tm_glu"], tn=MIX_TILE)
    buf16 = jnp.pad(pool_buf, ((0, 0), (POOL_HIST - pool_buf.shape[1], 0), (0, 0)))
    yb, tail = pool(z, 1, buf16, pool_w, i, pool_scale, start_pos=start_pos, tc=256)
    x, w_out_b = _res_proj(x, ya, yb.reshape(t, -1), w_out, i, tiles)
    g_a = n_slab * SLAB_GROUPS
    h_re = h_last[..., :SLAB_STATE].reshape(bsz, g_a, S5_P)
    h_im = h_last[..., SLAB_STATE:].reshape(bsz, g_a, S5_P)
    return x, h_re, h_im, tail[:, POOL_HIST - pool_buf.shape[1]:], (w_in_b, w_out_b)


def _odd_layer(x, bsz, seq, k_buf, v_buf, norm_g, w_in, w_out, i, qn, kn, ln_g, ln_b, w_s, b_s, n_heads, tiles):
    t, d = x.shape
    d_c = n_heads * LANES
    z, w_in_b = _in_proj(x, norm_g, w_in, i, tiles)
    z = z.reshape(bsz, seq, -1)
    if k_buf is None:
        att, k_new, v_new = attn_prompt(z, qn, kn, n_heads=n_heads)
    else:
        att, k_new, v_new = attn_sample(z, k_buf, v_buf, i * bsz, qn, kn, n_heads=n_heads)
    dd = ln_g.shape[0]
    sg, vn = sgu(z, (3 * d_c) // dd, ln_g, ln_b, w_s, b_s)
    x, w_out_b = _res_proj(x, att.reshape(t, d_c), sg.reshape(t, dd), w_out, i, tiles)
    hd = LANES
    return x, k_new.reshape(bsz, seq, n_heads, hd), v_new.reshape(bsz, seq, n_heads, hd), vn, (w_in_b, w_out_b)


def kernel(x_prompt, x_sample, state_s5_re, state_s5_im, state_pool, cache_k, cache_v, norm_mix, norm_ffn, ev_w_in, ev_w_out, s5_lambda_re, s5_lambda_im, s5_log_dt, s5_b_re, s5_b_im, s5_c_re, s5_c_im, s5_d, s5_w_glu, s5_b_glu, pool_w, pool_scale, od_w_in, od_w_out, q_norm, k_norm, sgu_ln_g, sgu_ln_b, sgu_w, sgu_b, ffn_w1, ffn_w3, ffn_w2):
    bp, lp, d = x_prompt.shape
    bs, ls, _ = x_sample.shape
    depth = norm_mix.shape[0]
    n_heads = cache_k.shape[3]
    xp = x_prompt.reshape(bp * lp, d)
    xs = x_sample.reshape(bs * ls, d)
    tiles_p = dict(cast=False, tm=1024, tm_glu=512, tm_out=1024, s5_long=True, z_dtype=BF16)
    tiles_s = dict(cast=True, tm_glu=bs * ls, s5_long=False)
    g_a, p_a = s5_lambda_re.shape[1:]
    w_glu_b, pool_w_b = s5_w_glu.astype(BF16), pool_w.astype(BF16)
    d_c = n_heads * LANES
    cache_k2 = cache_k.reshape(-1, cache_k.shape[2] * n_heads, LANES)
    cache_v2 = cache_v.reshape(-1, cache_v.shape[2] * n_heads, LANES)

    s5r_p, s5i_p, pool_p, k_p, v_p = [], [], [], [], []
    s5r_s, s5i_s, pool_s, k_s, v_s, sgu_s = [], [], [], [], [], []
    for l in range(depth):
        i = l // 2
        if l % 2 == 0:
            pw_re, pw_im, bb_re, bb_im = s5_prep(s5_lambda_re[i], s5_lambda_im[i], s5_log_dt[i], s5_b_re[i], s5_b_im[i])
            b_blk, c_blk = s5_block_matrices(bb_re, bb_im, s5_c_re[i], s5_c_im[i])
            s5p = (pw_re, pw_im, b_blk, c_blk)
            rest = (i, s5p, pool_w_b, pool_scale[i], w_glu_b, s5_b_glu[i], s5_d[i])
            xs, hr, hi, buf, (w_in_b, w_out_b) = _even_layer(
                xs, bs, ls, state_s5_re[i], state_s5_im[i], state_pool[i], PAST_LEN, norm_mix[l], ev_w_in, ev_w_out,
                *rest, tiles_s)
            s5r_s.append(hr); s5i_s.append(hi); pool_s.append(buf)
            zero_h = jnp.zeros((bp, g_a, p_a), F32)
            zero_buf = jnp.zeros((bp, state_pool.shape[2], state_pool.shape[3]), F32)
            xp, hr, hi, buf, _ = _even_layer(xp, bp, lp, zero_h, zero_h, zero_buf, 0, norm_mix[l], w_in_b, w_out_b,
                                             *rest, tiles_p)
            s5r_p.append(hr); s5i_p.append(hi); pool_p.append(buf)
        else:
            rest = (i, q_norm[i], k_norm[i], sgu_ln_g[i], sgu_ln_b[i], sgu_w[i], sgu_b[i], n_heads)
            xs, nk, nv, vrows, (w_in_b, w_out_b) = _odd_layer(xs, bs, ls, cache_k2, cache_v2, norm_mix[l], od_w_in,
                                                              od_w_out, *rest, tiles_s)
            k_s.append(nk); v_s.append(nv); sgu_s.append(vrows)
            xp, nk, nv, _, _ = _odd_layer(xp, bp, lp, None, None, norm_mix[l], w_in_b, w_out_b, *rest, tiles_p)
            k_p.append(nk); v_p.append(nv)
        if l == 0:
            xs, *ffn_wb = ffn_cast(xs, norm_ffn[l], ffn_w1, ffn_w3, ffn_w2, l, tf=FFN_TILE)
        else:
            xs = ffn(xs, norm_ffn[l], *ffn_wb, tm=bs * ls)
        if l + 1 < depth:
            xp, ffn_wb = ffn(xp, norm_ffn[l], *ffn_wb, tm=512, nxt=(ffn_w1, ffn_w3, ffn_w2, l + 1))
        else:
            xp = ffn(xp, norm_ffn[l], *ffn_wb, tm=512)
    return (xp.reshape(bp, lp, d), xs.reshape(bs, ls, d),
            jnp.stack(s5r_p), jnp.stack(s5i_p), jnp.stack(pool_p), jnp.stack(k_p), jnp.stack(v_p),
            jnp.stack(s5r_s), jnp.stack(s5i_s), jnp.stack(pool_s), jnp.stack(k_s), jnp.stack(v_s),
            jnp.stack(sgu_s))
```

```python
import functools
import math

import jax
import jax.numpy as jnp
from jax import lax
from jax.experimental import pallas as pl
from jax.experimental.pallas import tpu as pltpu

F32 = jnp.float32
BF16 = jnp.bfloat16

RMS_EPS = 1e-6
LN_EPS = 1e-5
NEG_INF = -1e30

LANES = 128
SUBLANES = 8
ATT_BLOCK = 128
CHUNK = 128
POOL_WINDOWS = (2, 4, 8, 16)
POOL_HIST = 16
BRANCHES = ((128, 1), (512, 4), (2048, 16))
S5_GRP = 16
S5_P = 64
SLAB_GROUPS = LANES // S5_GRP
SLAB_STATE = SLAB_GROUPS * S5_P
PAST_LEN = 8192
FFN_TILE = 256
MIX_TILE = 512


def _params(sem, vmem_mib):
    return pltpu.CompilerParams(dimension_semantics=sem, vmem_limit_bytes=vmem_mib << 20)


def _gelu(x):
    return 0.5 * x * (1.0 + lax.erf(x * (1.0 / math.sqrt(2.0))))


def _sigmoid(x):
    return 1.0 / (1.0 + jnp.exp(-x))


def _split_bf16(a):
    hi = a.astype(BF16)
    lo = (a - hi.astype(F32)).astype(BF16)
    return hi, lo


def _dot(a, b):
    return jnp.dot(a, b, preferred_element_type=F32)


def _dot_nt(a, b):
    return lax.dot_general(a, b, (((1,), (1,)), ((), ())), preferred_element_type=F32)


def _rms_rows_to(x_ref, g_ref, h_ref, rows):
    step = 16 if rows % 16 == 0 else rows

    def body(r, c):
        sl = pl.ds(pl.multiple_of(r * step, step), step)
        x = x_ref[sl, :]
        ms = jnp.mean(x * x, axis=-1, keepdims=True)
        h_ref[sl, :] = ((x * lax.rsqrt(ms + RMS_EPS)) * g_ref[...]).astype(h_ref.dtype)
        return c

    lax.fori_loop(0, rows // step, body, 0, unroll=True)


def _norm_matmul_kernel(x_ref, g_ref, w_ref, o_ref, h_ref):
    first = pl.program_id(1) == 0

    def tile():
        o_ref[...] = _dot(h_ref[...], w_ref[...]).astype(o_ref.dtype)

    @pl.when(first)
    def _():
        _rms_rows_to(x_ref, g_ref, h_ref, x_ref.shape[0])
        tile()

    @pl.when(jnp.logical_not(first))
    def _():
        tile()


def norm_matmul(x, g, w, *, tm, out_dtype=F32):
    t, d = x.shape
    n_t, _, tn = w.shape
    return pl.pallas_call(
        _norm_matmul_kernel,
        grid=(t // tm, n_t),
        in_specs=[
            pl.BlockSpec((tm, d), lambda i, j: (i, 0)),
            pl.BlockSpec((1, d), lambda i, j: (0, 0)),
            pl.BlockSpec((None, d, tn), lambda i, j: (j, 0, 0)),
        ],
        out_specs=pl.BlockSpec((tm, tn), lambda i, j: (i, j)),
        out_shape=jax.ShapeDtypeStruct((t, n_t * tn), out_dtype),
        scratch_shapes=[pltpu.VMEM((tm, d), BF16)],
        compiler_params=_params(("arbitrary", "arbitrary"), 58),
        name="norm_matmul",
    )(x, g.reshape(1, d), w)


def _norm_matmul_cast_kernel(x_ref, g_ref, w_ref, o_ref, wb_ref, h_ref):
    @pl.when(pl.program_id(0) == 0)
    def _():
        _rms_rows_to(x_ref, g_ref, h_ref, x_ref.shape[0])

    wb_ref[...] = w_ref[...].astype(BF16)
    o_ref[...] = _dot(h_ref[...], wb_ref[...]).astype(o_ref.dtype)


def norm_matmul_cast(x, g, w, layer, *, tn):
    t, d = x.shape
    n = w.shape[2]
    return pl.pallas_call(
        _norm_matmul_cast_kernel,
        grid=(n // tn,),
        in_specs=[
            pl.BlockSpec((t, d), lambda j: (0, 0)),
            pl.BlockSpec((1, d), lambda j: (0, 0)),
            pl.BlockSpec((None, d, tn), lambda j: (layer, 0, j)),
        ],
        out_specs=[
            pl.BlockSpec((t, tn), lambda j: (0, j)),
            pl.BlockSpec((None, d, tn), lambda j: (j, 0, 0)),
        ],
        out_shape=[
            jax.ShapeDtypeStruct((t, n), F32),
            jax.ShapeDtypeStruct((n // tn, d, tn), BF16),
        ],
        scratch_shapes=[pltpu.VMEM((t, d), BF16)],
        compiler_params=_params(("arbitrary",), 48),
        name="norm_matmul_cast",
    )(x, g.reshape(1, d), w)


FFN_OUT_CHUNK = 512


def _ffn_step(first, x_ref, g_ref, w1_ref, w3_ref, w2_ref, o_ref, h_ref):
    def tile(acc_ref):
        h = h_ref[...]
        a = _dot(h, w1_ref[...])
        b = _dot(h, w3_ref[...])
        u = ((a * _sigmoid(a)) * b).astype(BF16)
        for c in range(0, o_ref.shape[1], FFN_OUT_CHUNK):
            cols = slice(c, c + FFN_OUT_CHUNK)
            o_ref[:, cols] = acc_ref[:, cols] + _dot(u, w2_ref[:, cols])

    @pl.when(first)
    def _():
        _rms_rows_to(x_ref, g_ref, h_ref, x_ref.shape[0])
        tile(x_ref)

    @pl.when(jnp.logical_not(first))
    def _():
        tile(o_ref)


def _ffn_kernel(x_ref, g_ref, w1_ref, w3_ref, w2_ref, o_ref, h_ref):
    _ffn_step(pl.program_id(1) == 0, x_ref, g_ref, w1_ref, w3_ref, w2_ref, o_ref, h_ref)


def _ffn_next_kernel(x_ref, g_ref, w1_ref, w3_ref, w2_ref, n1_ref, n3_ref, n2_ref,
                     o_ref, c1_ref, c3_ref, c2_ref, h_ref):
    c1_ref[...] = n1_ref[...].astype(BF16)
    c3_ref[...] = n3_ref[...].astype(BF16)
    c2_ref[...] = n2_ref[...].astype(BF16)
    _ffn_step(pl.program_id(1) == 0, x_ref, g_ref, w1_ref, w3_ref, w2_ref, o_ref, h_ref)


def ffn(x, g, w1, w3, w2, *, tm, nxt=None):
    t, d = x.shape
    n_f, _, tf = w1.shape
    n_m = t // tm
    specs = [
        pl.BlockSpec((tm, d), lambda i, j: (i, 0)),
        pl.BlockSpec((1, d), lambda i, j: (0, 0)),
        pl.BlockSpec((None, d, tf), lambda i, j: (j, 0, 0)),
        pl.BlockSpec((None, d, tf), lambda i, j: (j, 0, 0)),
        pl.BlockSpec((tf, d), lambda i, j: (j, 0)),
    ]
    out_spec = pl.BlockSpec((tm, d), lambda i, j: (i, 0))
    out_shape = jax.ShapeDtypeStruct((t, d), F32)
    common = dict(grid=(n_m, n_f), scratch_shapes=[pltpu.VMEM((tm, d), BF16)],
                  compiler_params=_params(("arbitrary", "arbitrary"), 56))
    if nxt is None:
        return pl.pallas_call(_ffn_kernel, in_specs=specs, out_specs=out_spec, out_shape=out_shape, name="ffn",
                              **common)(x, g.reshape(1, d), w1, w3, w2)
    n1, n3, n2, layer = nxt
    piece = d // n_m
    assert d % n_m == 0 and piece % LANES == 0
    col = pl.BlockSpec((None, piece, tf), lambda i, j: (layer, i, j))
    row = pl.BlockSpec((None, tf, piece), lambda i, j: (layer, j, i))
    col_out = pl.BlockSpec((None, piece, tf), lambda i, j: (j, i, 0))
    row_out = pl.BlockSpec((tf, piece), lambda i, j: (j, i))
    out, c1, c3, c2 = pl.pallas_call(
        _ffn_next_kernel,
        in_specs=specs + [col, col, row],
        out_specs=[out_spec, col_out, col_out, row_out],
        out_shape=[out_shape, jax.ShapeDtypeStruct(w1.shape, BF16), jax.ShapeDtypeStruct(w3.shape, BF16),
                   jax.ShapeDtypeStruct(w2.shape, BF16)],
        name="ffn_next", **common,
    )(x, g.reshape(1, d), w1, w3, w2, n1, n3, n2)
    return out, (c1, c3, c2)


def _ffn_cast_kernel(x_ref, g_ref, w1_ref, w3_ref, w2_ref, o_ref, w1b_ref, w3b_ref, w2b_ref, h_ref):
    w1b_ref[...] = w1_ref[...].astype(BF16)
    w3b_ref[...] = w3_ref[...].astype(BF16)
    w2b_ref[...] = w2_ref[...].astype(BF16)
    _ffn_step(pl.program_id(0) == 0, x_ref, g_ref, w1b_ref, w3b_ref, w2b_ref, o_ref, h_ref)


def ffn_cast(x, g, w1, w3, w2, layer, *, tf):
    t, d = x.shape
    f = w1.shape[2]
    return pl.pallas_call(
        _ffn_cast_kernel,
        grid=(f // tf,),
        in_specs=[
            pl.BlockSpec((t, d), lambda j: (0, 0)),
            pl.BlockSpec((1, d), lambda j: (0, 0)),
            pl.BlockSpec((None, d, tf), lambda j: (layer, 0, j)),
            pl.BlockSpec((None, d, tf), lambda j: (layer, 0, j)),
            pl.BlockSpec((None, tf, d), lambda j: (layer, j, 0)),
        ],
        out_specs=[
            pl.BlockSpec((t, d), lambda j: (0, 0)),
            pl.BlockSpec((None, d, tf), lambda j: (j, 0, 0)),
            pl.BlockSpec((None, d, tf), lambda j: (j, 0, 0)),
            pl.BlockSpec((tf, d), lambda j: (j, 0)),
        ],
        out_shape=[
            jax.ShapeDtypeStruct((t, d), F32),
            jax.ShapeDtypeStruct((f // tf, d, tf), BF16),
            jax.ShapeDtypeStruct((f // tf, d, tf), BF16),
            jax.ShapeDtypeStruct((f, d), BF16),
        ],
        scratch_shapes=[pltpu.VMEM((t, d), BF16)],
        compiler_params=_params(("arbitrary",), 48),
        name="ffn_cast",
    )(x, g.reshape(1, d), w1, w3, w2)


def _out_proj_kernel(x_ref, a_ref, b_ref, wa_ref, wb_ref, o_ref):
    o_ref[...] = x_ref[...] + _dot(a_ref[...], wa_ref[...]) + _dot(b_ref[...], wb_ref[...])


def out_proj(x, a, b, w_pair, *, tm):
    t, d = x.shape
    wa, wb = w_pair
    n_t, k, tn = wa.shape
    assert a.shape[1] == k and b.shape[1] == k and wb.shape == wa.shape
    return pl.pallas_call(
        _out_proj_kernel,
        grid=(t // tm, n_t),
        in_specs=[
            pl.BlockSpec((tm, tn), lambda i, j: (i, j)),
            pl.BlockSpec((tm, k), lambda i, j: (i, 0)),
            pl.BlockSpec((tm, k), lambda i, j: (i, 0)),
            pl.BlockSpec((None, k, tn), lambda i, j: (j, 0, 0)),
            pl.BlockSpec((None, k, tn), lambda i, j: (j, 0, 0)),
        ],
        out_specs=pl.BlockSpec((tm, tn), lambda i, j: (i, j)),
        out_shape=jax.ShapeDtypeStruct((t, d), F32),
        compiler_params=_params(("arbitrary", "arbitrary"), 48),
        name="out_proj",
    )(x, a, b, wa, wb)


def _out_proj_cast_kernel(x_ref, a_ref, b_ref, wa_ref, wb_ref, o_ref, wab_ref, wbb_ref):
    wab_ref[...] = wa_ref[...].astype(BF16)
    wbb_ref[...] = wb_ref[...].astype(BF16)
    o_ref[...] = x_ref[...] + _dot(a_ref[...], wab_ref[...]) + _dot(b_ref[...], wbb_ref[...])


def out_proj_cast(x, a, b, w, layer, *, tn):
    t, d = x.shape
    k = a.shape[1]
    assert b.shape[1] == k and w.shape[1] == 2 * k
    out, wa_b, wb_b = pl.pallas_call(
        _out_proj_cast_kernel,
        grid=(d // tn,),
        in_specs=[
            pl.BlockSpec((t, tn), lambda j: (0, j)),
            pl.BlockSpec((t, k), lambda j: (0, 0)),
            pl.BlockSpec((t, k), lambda j: (0, 0)),
            pl.BlockSpec((None, k, tn), lambda j: (layer, 0, j)),
            pl.BlockSpec((None, k, tn), lambda j: (layer, 1, j)),
        ],
        out_specs=[
            pl.BlockSpec((t, tn), lambda j: (0, j)),
            pl.BlockSpec((None, k, tn), lambda j: (j, 0, 0)),
            pl.BlockSpec((None, k, tn), lambda j: (j, 0, 0)),
        ],
        out_shape=[
            jax.ShapeDtypeStruct((t, d), F32),
            jax.ShapeDtypeStruct((d // tn, k, tn), BF16),
            jax.ShapeDtypeStruct((d // tn, k, tn), BF16),
        ],
        compiler_params=_params(("arbitrary",), 48),
        name="out_proj_cast",
    )(x, a, b, w, w)
    return out, (wa_b, wb_b)


def _s5_prep_kernel(lr_ref, li_ref, ldt_ref, lrx_ref, lix_ref, ldtx_ref, br_ref, bi_ref,
                    pwr_ref, pwi_ref, bbr_ref, bbi_ref):
    dt = jnp.exp(ldt_ref[...])
    mag = jnp.exp(lr_ref[...] * dt)
    ang = li_ref[...] * dt
    p_r, p_i = mag * jnp.cos(ang), mag * jnp.sin(ang)
    c_r, c_i = p_r, p_i
    pwr_ref[0], pwi_ref[0] = c_r, c_i
    for j in range(1, SUBLANES):
        c_r, c_i = c_r * p_r - c_i * p_i, c_r * p_i + c_i * p_r
        pwr_ref[j], pwi_ref[j] = c_r, c_i
    lr, li = lrx_ref[...], lix_ref[...]
    dtx = jnp.exp(ldtx_ref[...])
    magx = jnp.exp(lr * dtx)
    angx = li * dtx
    nr, ni = magx * jnp.cos(angx) - 1.0, magx * jnp.sin(angx)
    den = lr * lr + li * li
    qr = (nr * lr + ni * li) / den
    qi = (ni * lr - nr * li) / den
    br, bi = br_ref[...], bi_ref[...]
    bbr_ref[...] = qr * br - qi * bi
    bbi_ref[...] = qr * bi + qi * br


def s5_prep(lam_re, lam_im, log_dt, b_re, b_im):
    g, p = lam_re.shape
    h = b_re.shape[2]
    n_slab = g // SLAB_GROUPS
    slab = lambda a: a.reshape(n_slab, SLAB_GROUPS * p)
    rep = lambda a: jnp.repeat(a, h, axis=1)
    ldt_gp = jnp.broadcast_to(log_dt[:, None], (g, p))
    outs = pl.pallas_call(
        _s5_prep_kernel,
        out_shape=[
            jax.ShapeDtypeStruct((SUBLANES, n_slab, SLAB_GROUPS * p), F32),
            jax.ShapeDtypeStruct((SUBLANES, n_slab, SLAB_GROUPS * p), F32),
            jax.ShapeDtypeStruct((g, p * h), F32),
            jax.ShapeDtypeStruct((g, p * h), F32),
        ],
        name="s5_prep",
    )(slab(lam_re), slab(lam_im), slab(ldt_gp), rep(lam_re), rep(lam_im), rep(ldt_gp),
      b_re.reshape(g, p * h), b_im.reshape(g, p * h))
    pw_re, pw_im, bb_re, bb_im = outs
    pw_re = jnp.transpose(pw_re, (1, 0, 2))
    pw_im = jnp.transpose(pw_im, (1, 0, 2))
    return pw_re, pw_im, bb_re.reshape(g, p, h), bb_im.reshape(g, p, h)


def s5_block_matrices(bb_re, bb_im, c_re, c_im):
    g, p, h = bb_re.shape
    n_slab = g // SLAB_GROUPS
    eye = jnp.eye(SLAB_GROUPS, dtype=F32)

    def in_map(bb):
        t = bb.reshape(n_slab, SLAB_GROUPS, p, h)
        return jnp.einsum("kgph,gj->kghjp", t, eye).reshape(n_slab, SLAB_GROUPS * h, SLAB_GROUPS * p)

    def out_map(c):
        t = c.reshape(n_slab, SLAB_GROUPS, h, p)
        return jnp.einsum("kghp,gj->kgpjh", t, eye).reshape(n_slab, SLAB_GROUPS * p, SLAB_GROUPS * h)

    b_blk = jnp.concatenate([in_map(bb_re), in_map(bb_im)], axis=2)
    c_blk = jnp.concatenate([out_map(c_re), -out_map(c_im)], axis=1)
    return b_blk, c_blk


def _s5_scan_kernel(u_ref, bblk_ref, cblk_ref, pwr_ref, pwi_ref, d_ref, h0_ref,
                    y_ref, hl_ref, h_scr, bh_scr, bl_scr, ch_scr, *, seq, row_chunk, split_in):
    ns = SLAB_STATE

    @pl.when(pl.program_id(1) == 0)
    def _():
        bh, bl = _split_bf16(bblk_ref[...])
        bh_scr[...], bl_scr[...] = bh, bl
        ch_scr[...] = cblk_ref[...].astype(BF16)

    n_chunks = seq // row_chunk

    def proj_in(r, c):
        sl = pl.ds(pl.multiple_of(r * row_chunk, row_chunk), row_chunk)
        if split_in:
            uh, ul = _split_bf16(u_ref[sl, :].astype(F32))
            h_scr[sl, :] = _dot(uh, bh_scr[...]) + _dot(ul, bh_scr[...]) + _dot(uh, bl_scr[...])
        else:
            h_scr[sl, :] = _dot(u_ref[sl, :].astype(BF16), bh_scr[...])
        return c

    lax.fori_loop(0, n_chunks, proj_in, 0)

    rowid = lax.broadcasted_iota(jnp.int32, (SUBLANES, LANES), 0)
    for c in range(ns // LANES):
        re_l = slice(c * LANES, (c + 1) * LANES)
        im_l = slice(ns + c * LANES, ns + (c + 1) * LANES)
        p_r, p_i = pwr_ref[:, re_l], pwi_ref[:, re_l]
        steps = []
        for dist in (1, 2, 4):
            a_r = jnp.where(rowid >= dist, jnp.broadcast_to(p_r[dist - 1:dist], (SUBLANES, LANES)), 0.0)
            a_i = jnp.where(rowid >= dist, jnp.broadcast_to(p_i[dist - 1:dist], (SUBLANES, LANES)), 0.0)
            steps.append((dist, a_r, a_i))
        c_r = jnp.broadcast_to(h0_ref[:, re_l], (SUBLANES, LANES))
        c_i = jnp.broadcast_to(h0_ref[:, im_l], (SUBLANES, LANES))

        last = slice(SUBLANES - 1, SUBLANES)
        full = (SUBLANES, LANES)
        p8_r, p8_i = jnp.broadcast_to(p_r[last], full), jnp.broadcast_to(p_i[last], full)
        n_groups = seq // SUBLANES
        per_it = min(4, n_groups)

        def scan_rows(it, carry, re_l=re_l, im_l=im_l, p_r=p_r, p_i=p_i, p8_r=p8_r, p8_i=p8_i, steps=steps):
            base = pl.multiple_of(it * (per_it * SUBLANES), per_it * SUBLANES)
            sls = [pl.ds(base + j * SUBLANES, SUBLANES) for j in range(per_it)]
            loc = []
            for sl in sls:
                r, i = h_scr[sl, re_l], h_scr[sl, im_l]
                for dist, a_r, a_i in steps:
                    s_r, s_i = pltpu.roll(r, dist, 0), pltpu.roll(i, dist, 0)
                    r, i = r + (s_r * a_r - s_i * a_i), i + (s_r * a_i + s_i * a_r)
                loc.append((r, i))
            c_r, c_i = carry
            outs = []
            for r, i in loc:
                outs.append((r + (c_r * p_r - c_i * p_i), i + (c_r * p_i + c_i * p_r)))
                e_r, e_i = jnp.broadcast_to(r[last], full), jnp.broadcast_to(i[last], full)
                c_r, c_i = e_r + (c_r * p8_r - c_i * p8_i), e_i + (c_r * p8_i + c_i * p8_r)
            for sl, (r, i) in zip(sls, outs):
                h_scr[sl, re_l], h_scr[sl, im_l] = r, i
            return c_r, c_i

        c_r, c_i = lax.fori_loop(0, n_groups // per_it, scan_rows, (c_r, c_i))
        hl_ref[:, re_l] = c_r[0:1]
        hl_ref[:, im_l] = c_i[0:1]

    def proj_out(r, c):
        sl = pl.ds(pl.multiple_of(r * row_chunk, row_chunk), row_chunk)
        y = _dot(h_scr[sl, :].astype(BF16), ch_scr[...]) + d_ref[...] * u_ref[sl, :].astype(F32)
        y_ref[sl, :] = _gelu(y)
        return c

    lax.fori_loop(0, n_chunks, proj_out, 0)


def s5_scan(z, col0, b_blk, c_blk, pw_re, pw_im, d_skip, h0, *, split_in):
    bsz, seq, _ = z.shape
    n_slab = b_blk.shape[0]
    ns2 = 2 * SLAB_STATE
    row_chunk = min(seq, 1024)
    kern = functools.partial(_s5_scan_kernel, seq=seq, row_chunk=row_chunk, split_in=split_in)
    return pl.pallas_call(
        kern,
        grid=(n_slab, bsz),
        in_specs=[
            pl.BlockSpec((None, seq, LANES), lambda k, b: (b, 0, col0 + k)),
            pl.BlockSpec((None, LANES, ns2), lambda k, b: (k, 0, 0)),
            pl.BlockSpec((None, ns2, LANES), lambda k, b: (k, 0, 0)),
            pl.BlockSpec((None, SUBLANES, SLAB_STATE), lambda k, b: (k, 0, 0)),
            pl.BlockSpec((None, SUBLANES, SLAB_STATE), lambda k, b: (k, 0, 0)),
            pl.BlockSpec((1, LANES), lambda k, b: (0, k)),
            pl.BlockSpec((None, None, 1, ns2), lambda k, b: (b, k, 0, 0)),
        ],
        out_specs=[
            pl.BlockSpec((None, seq, LANES), lambda k, b: (b, 0, k)),
            pl.BlockSpec((None, None, 1, ns2), lambda k, b: (b, k, 0, 0)),
        ],
        out_shape=[
            jax.ShapeDtypeStruct((bsz, seq, n_slab * LANES), F32),
            jax.ShapeDtypeStruct((bsz, n_slab, 1, ns2), F32),
        ],
        scratch_shapes=[
            pltpu.VMEM((seq, ns2), F32),
            pltpu.VMEM((LANES, ns2), BF16), pltpu.VMEM((LANES, ns2), BF16),
            pltpu.VMEM((ns2, LANES), BF16),
        ],
        compiler_params=_params(("arbitrary", "arbitrary"), 40),
        name="s5_scan",
    )(z, b_blk, c_blk, pw_re, pw_im, d_skip.reshape(1, -1), h0)


def _s5_seg_kernel(u_ref, bblk_ref, cblk_ref, pwr_ref, pwi_ref, d_ref, h0_ref,
                   y_ref, hl_ref, up_scr, h_scr, yp_scr, w_scr, bh_scr, ch_scr, *, seq, row_chunk):
    ns = SLAB_STATE
    nseg = SUBLANES
    seg = seq // nseg
    n_lb = ns // LANES
    full = (SUBLANES, LANES)
    lanes = [(slice(c * LANES, (c + 1) * LANES), slice(ns + c * LANES, ns + (c + 1) * LANES)) for c in range(n_lb)]
    last = slice(SUBLANES - 1, SUBLANES)

    @pl.when(pl.program_id(1) == 0)
    def _():
        bh_scr[...] = bblk_ref[...].astype(BF16)
        ch_scr[...] = cblk_ref[...].astype(BF16)
        for re_l, im_l in lanes:
            p_r, p_i = pwr_ref[:, re_l], pwi_ref[:, re_l]
            p8_r, p8_i = jnp.broadcast_to(p_r[last], full), jnp.broadcast_to(p_i[last], full)
            w_scr[0:SUBLANES, re_l], w_scr[0:SUBLANES, im_l] = p_r, p_i

            def grow(gi, carry, re_l=re_l, im_l=im_l, p8_r=p8_r, p8_i=p8_i):
                w_r, w_i = carry
                w_r, w_i = w_r * p8_r - w_i * p8_i, w_r * p8_i + w_i * p8_r
                sl = pl.ds(pl.multiple_of(gi * SUBLANES, SUBLANES), SUBLANES)
                w_scr[sl, re_l], w_scr[sl, im_l] = w_r, w_i
                return w_r, w_i

            lax.fori_loop(1, seg // SUBLANES, grow, (p_r, p_i))

    for s in range(nseg):
        up_scr[pl.ds(s, seg, stride=nseg), :] = u_ref[s * seg:(s + 1) * seg, :].astype(F32)

    def proj_in(r, c):
        sl = pl.ds(pl.multiple_of(r * row_chunk, row_chunk), row_chunk)
        h_scr[sl, :] = _dot(up_scr[sl, :].astype(BF16), bh_scr[...])
        return c

    lax.fori_loop(0, seq // row_chunk, proj_in, 0, unroll=True)

    lam = [(jnp.broadcast_to(pwr_ref[0:1, re_l], full), jnp.broadcast_to(pwi_ref[0:1, re_l], full))
           for re_l, _ in lanes]
    per_it = 2

    def scan_t(it, carry):
        base = pl.multiple_of(it * (per_it * SUBLANES), per_it * SUBLANES)
        sls = [pl.ds(base + j * SUBLANES, SUBLANES) for j in range(per_it)]
        bu = [[(h_scr[sl, re_l], h_scr[sl, im_l]) for re_l, im_l in lanes] for sl in sls]
        hs, outs = list(carry), []
        for j in range(per_it):
            hs = [(bu[j][c][0] + (hs[c][0] * lam[c][0] - hs[c][1] * lam[c][1]),
                   bu[j][c][1] + (hs[c][0] * lam[c][1] + hs[c][1] * lam[c][0])) for c in range(n_lb)]
            outs.append(hs)
        for sl, row in zip(sls, outs):
            for (re_l, im_l), (h_r, h_i) in zip(lanes, row):
                h_scr[sl, re_l], h_scr[sl, im_l] = h_r, h_i
        return tuple(hs)

    zero = jnp.zeros(full, F32)
    ends = lax.fori_loop(0, seg // per_it, scan_t, tuple((zero, zero) for _ in range(n_lb)), unroll=True)

    rowid = lax.broadcasted_iota(jnp.int32, full, 0)
    enter = []
    for (re_l, im_l), (e_r, e_i) in zip(lanes, ends):
        ws_r, ws_i = w_scr[seg - 1:seg, re_l], w_scr[seg - 1:seg, im_l]
        c_r, c_i = h0_ref[:, re_l], h0_ref[:, im_l]
        cv_r, cv_i = jnp.broadcast_to(c_r, full), jnp.broadcast_to(c_i, full)
        for s in range(1, nseg + 1):
            c_r, c_i = (e_r[s - 1:s] + (c_r * ws_r - c_i * ws_i), e_i[s - 1:s] + (c_r * ws_i + c_i * ws_r))
            if s < nseg:
                cv_r = jnp.where(rowid == s, jnp.broadcast_to(c_r, full), cv_r)
                cv_i = jnp.where(rowid == s, jnp.broadcast_to(c_i, full), cv_i)
        hl_ref[:, re_l], hl_ref[:, im_l] = c_r, c_i
        enter.append((cv_r, cv_i))

    def fix_t(gi, c):
        wsl = pl.ds(pl.multiple_of(gi * SUBLANES, SUBLANES), SUBLANES)
        wv = [(w_scr[wsl, re_l], w_scr[wsl, im_l]) for re_l, im_l in lanes]
        base = pl.multiple_of(gi * (SUBLANES * SUBLANES), SUBLANES * SUBLANES)
        for j in range(SUBLANES):
            sl = pl.ds(base + j * SUBLANES, SUBLANES)
            for (re_l, im_l), (cv_r, cv_i), (wv_r, wv_i) in zip(lanes, enter, wv):
                w_r = jnp.broadcast_to(wv_r[j:j + 1], full)
                w_i = jnp.broadcast_to(wv_i[j:j + 1], full)
                h_r = h_scr[sl, re_l] + (w_r * cv_r - w_i * cv_i)
                h_i = h_scr[sl, im_l] + (w_r * cv_i + w_i * cv_r)
                h_scr[sl, re_l], h_scr[sl, im_l] = h_r, h_i
        return c

    lax.fori_loop(0, seg // SUBLANES, fix_t, 0, unroll=True)

    def proj_out(r, c):
        sl = pl.ds(pl.multiple_of(r * row_chunk, row_chunk), row_chunk)
        y = _dot(h_scr[sl, :].astype(BF16), ch_scr[...]) + d_ref[...] * up_scr[sl, :]
        yp_scr[sl, :] = _gelu(y)
        return c

    lax.fori_loop(0, seq // row_chunk, proj_out, 0, unroll=True)

    for s in range(nseg):
        y_ref[s * seg:(s + 1) * seg, :] = yp_scr[pl.ds(s, seg, stride=nseg), :]


def s5_scan_long(z, col0, b_blk, c_blk, pw_re, pw_im, d_skip, h0):
    bsz, seq, _ = z.shape
    n_slab = b_blk.shape[0]
    ns2 = 2 * SLAB_STATE
    row_chunk = min(seq, 1024)
    assert seq % row_chunk == 0 and seq % (2 * SUBLANES * SUBLANES) == 0
    kern = functools.partial(_s5_seg_kernel, seq=seq, row_chunk=row_chunk)
    return pl.pallas_call(
        kern,
        grid=(n_slab, bsz),
        in_specs=[
            pl.BlockSpec((None, seq, LANES), lambda k, b: (b, 0, col0 + k)),
            pl.BlockSpec((None, LANES, ns2), lambda k, b: (k, 0, 0)),
            pl.BlockSpec((None, ns2, LANES), lambda k, b: (k, 0, 0)),
            pl.BlockSpec((None, SUBLANES, SLAB_STATE), lambda k, b: (k, 0, 0)),
            pl.BlockSpec((None, SUBLANES, SLAB_STATE), lambda k, b: (k, 0, 0)),
            pl.BlockSpec((1, LANES), lambda k, b: (0, k)),
            pl.BlockSpec((None, None, 1, ns2), lambda k, b: (b, k, 0, 0)),
        ],
        out_specs=[
            pl.BlockSpec((None, seq, LANES), lambda k, b: (b, 0, k)),
            pl.BlockSpec((None, None, 1, ns2), lambda k, b: (b, k, 0, 0)),
        ],
        out_shape=[
            jax.ShapeDtypeStruct((bsz, seq, n_slab * LANES), F32),
            jax.ShapeDtypeStruct((bsz, n_slab, 1, ns2), F32),
        ],
        scratch_shapes=[
            pltpu.VMEM((seq, LANES), F32),
            pltpu.VMEM((seq, ns2), F32),
            pltpu.VMEM((seq, LANES), F32),
            pltpu.VMEM((seq // SUBLANES, ns2), F32),
            pltpu.VMEM((LANES, ns2), BF16),
            pltpu.VMEM((ns2, LANES), BF16),
        ],
        compiler_params=_params(("arbitrary", "arbitrary"), 40),
        name="s5_scan_long",
    )(z, b_blk, c_blk, pw_re, pw_im, d_skip.reshape(1, -1), h0)


def _glu_kernel(yk_ref, yj_ref, w_ref, b_ref, o_ref, yb_scr):
    first = pl.program_id(1) == 0

    def tile():
        gate = _dot(yb_scr[...], w_ref[...]) + b_ref[...]
        o_ref[...] = (yj_ref[...] * _sigmoid(gate)).astype(o_ref.dtype)

    @pl.when(first)
    def _():
        yb_scr[...] = yk_ref[...].astype(BF16)
        tile()

    @pl.when(jnp.logical_not(first))
    def _():
        tile()


def glu(y, w, b, layer, *, tm, tn):
    t, d = y.shape
    tm = min(tm, t)
    return pl.pallas_call(
        _glu_kernel,
        grid=(t // tm, d // tn),
        in_specs=[
            pl.BlockSpec((tm, d), lambda i, j: (i, 0)),
            pl.BlockSpec((tm, tn), lambda i, j: (i, j)),
            pl.BlockSpec((None, d, tn), lambda i, j: (layer, 0, j)),
            pl.BlockSpec((1, tn), lambda i, j: (0, j)),
        ],
        out_specs=pl.BlockSpec((tm, tn), lambda i, j: (i, j)),
        out_shape=jax.ShapeDtypeStruct((t, d), BF16),
        scratch_shapes=[pltpu.VMEM((tm, d), BF16)],
        compiler_params=_params(("arbitrary", "arbitrary"), 40),
        name="glu",
    )(y, y, w, b.reshape(1, d))


def _pool_kernel(u_ref, buf_ref, w_ref, s_ref, y_ref, tail_ref, ext_scr, *, tc, start_pos, cg):
    c = pl.program_id(1)

    @pl.when(c == 0)
    def _():
        ext_scr[0:POOL_HIST, :] = buf_ref[...]

    ext_scr[POOL_HIST:POOL_HIST + tc, :] = u_ref[...].astype(F32)
    pos = start_pos + c * tc + lax.broadcasted_iota(jnp.int32, (tc, 1), 0)
    for g, win in enumerate(POOL_WINDOWS):
        cols = slice(g * cg, (g + 1) * cg)
        x = ext_scr[:, cols]
        acc, dist = x, 1
        while dist < win:
            acc = acc + pltpu.roll(acc, dist, 0)
            dist *= 2
        wsum = acc[POOL_HIST:, :]
        cnt = jnp.minimum(pos + 1, win).astype(F32)
        zg = wsum * (1.0 / cnt) - x[POOL_HIST:, :]
        y = _dot(zg.astype(BF16), w_ref[g]) * s_ref[:, cols]
        y_ref[:, cols] = y.astype(y_ref.dtype)

    tail = ext_scr[tc:tc + POOL_HIST, :]
    ext_scr[0:POOL_HIST, :] = tail

    @pl.when(c == pl.num_programs(1) - 1)
    def _():
        tail_ref[...] = tail


def pool(z, colblk, buf16, w, layer, scale, *, start_pos, tc):
    bsz, seq, _ = z.shape
    _, n_g, cg, _ = w.shape
    db = n_g * cg
    tc = min(tc, seq)
    kern = functools.partial(_pool_kernel, tc=tc, start_pos=start_pos, cg=cg)
    return pl.pallas_call(
        kern,
        grid=(bsz, seq // tc),
        in_specs=[
            pl.BlockSpec((None, tc, db), lambda b, c: (b, c, colblk)),
            pl.BlockSpec((None, POOL_HIST, db), lambda b, c: (b, 0, 0)),
            pl.BlockSpec((None, n_g, cg, cg), lambda b, c: (layer, 0, 0, 0)),
            pl.BlockSpec((1, db), lambda b, c: (0, 0)),
        ],
        out_specs=[
            pl.BlockSpec((None, tc, db), lambda b, c: (b, c, 0)),
            pl.BlockSpec((None, POOL_HIST, db), lambda b, c: (b, 0, 0)),
        ],
        out_shape=[
            jax.ShapeDtypeStruct((bsz, seq, db), BF16),
            jax.ShapeDtypeStruct((bsz, POOL_HIST, db), F32),
        ],
        scratch_shapes=[pltpu.VMEM((POOL_HIST + tc, db), F32)],
        compiler_params=_params(("arbitrary", "arbitrary"), 40),
        name="pool",
    )(z, buf16, w, scale.reshape(1, db))


def _head_rms(x, g):
    ms = jnp.mean(x * x, axis=-1, keepdims=True)
    return (x * lax.rsqrt(ms + RMS_EPS)) * g


def _combine(os_, lses):
    m = jnp.maximum(jnp.maximum(lses[0], lses[1]), lses[2])
    ws = [jnp.exp(l - m) for l in lses]
    tot = ws[0] + ws[1] + ws[2]
    return (ws[0] * os_[0] + ws[1] * os_[1] + ws[2] * os_[2]) / tot


def _attn_prompt_kernel(q_ref, k_ref, v_ref, qn_ref, kn_ref, att_ref, ko_ref, vo_ref,
                        qs_scr, qf_scr, kf_scr, vf_scr, qd_scr, kd_scr, vd_scr, s_scr, p_scr, m_scr, o_scr, l_scr,
                        *, seq, scale):
    blk = ATT_BLOCK
    rows = 256
    n_all = seq // blk

    def prep(r, c):
        sl = pl.ds(pl.multiple_of(r * rows, rows), rows)
        qs_scr[sl, :] = _head_rms(q_ref[sl, :].astype(F32), qn_ref[...]) * scale
        ko_ref[sl, :] = _head_rms(k_ref[sl, :].astype(F32), kn_ref[...])
        vo_ref[sl, :] = v_ref[sl, :].astype(F32)
        return c

    lax.fori_loop(0, seq // rows, prep, 0, unroll=True)
    kd_scr[0:blk, :] = jnp.zeros((blk, LANES), BF16)
    vd_scr[0:blk, :] = jnp.zeros((blk, 2 * LANES), BF16)
    vd_scr[blk:, LANES:] = jnp.ones((seq, LANES), BF16)

    qi = lax.broadcasted_iota(jnp.int32, (blk, blk), 0)
    kj = lax.broadcasted_iota(jnp.int32, (blk, blk), 1)
    cur_ok = kj <= qi
    prev_ok = kj >= qi
    band_ok = jnp.concatenate([prev_ok, cur_ok], axis=1)
    in_cur = lax.broadcasted_iota(jnp.int32, (blk, 2 * blk), 1) >= blk

    for g, (window, dil) in enumerate(BRANCHES):
        n_blk = seq // (dil * blk)
        col0 = 0 if n_blk > 1 else LANES

        def place(idx, dil=dil, n_blk=n_blk):
            res = idx // n_blk
            n = idx - res * n_blk
            start = res + n * (dil * blk)
            nat = pl.ds(start, blk, stride=dil) if dil > 1 else pl.ds(pl.multiple_of(start, blk), blk)
            cur = pl.ds(pl.multiple_of(idx * blk, blk), blk)
            kcur = pl.ds(pl.multiple_of((idx + 1) * blk, blk), blk)
            kwin = pl.ds(pl.multiple_of(idx * blk, blk), 2 * blk)
            return nat, cur, kcur, kwin, n

        keep_f32 = dil == BRANCHES[1][1]
        two_level = g == 2 and dil == BRANCHES[1][1] ** 2

        def gather(idx, c, place=place, dil=dil, n_blk=n_blk, keep_f32=keep_f32, two_level=two_level):
            nat, cur, kcur, _, n = place(idx)
            if two_level:
                mid = BRANCHES[1][1]
                res = idx // n_blk
                start = (res % mid) * (seq // mid) + res // mid + n * (mid * blk)
                src = pl.ds(start, blk, stride=mid)
                q, k, v = qf_scr[src, :], kf_scr[src, :], vf_scr[src, :]
            else:
                q, k, v = qs_scr[nat, :], ko_ref[nat, :], vo_ref[nat, :]
            if keep_f32:
                qf_scr[cur, :], kf_scr[cur, :], vf_scr[cur, :] = q, k, v
            qd_scr[cur, :] = q.astype(BF16)
            kd_scr[kcur, :] = k.astype(BF16)
            vd_scr[kcur, 0:LANES] = v.astype(BF16)
            return c

        lax.fori_loop(0, n_all, gather, 0, unroll=True)

        def scores(idx, c, place=place, n_blk=n_blk):
            _, cur, kcur, kwin, n = place(idx)
            q = qd_scr[cur, :]
            if n_blk > 1:
                ok = jnp.logical_and(band_ok, jnp.logical_or(in_cur, n > 0))
                s_scr[idx] = jnp.where(ok, _dot_nt(q, kd_scr[kwin, :]), NEG_INF)
            else:
                s_scr[idx, :, LANES:] = jnp.where(cur_ok, _dot_nt(q, kd_scr[kcur, :]), NEG_INF)
            return c

        lax.fori_loop(0, n_all, scores, 0, unroll=True)

        def softmax(idx, c, col0=col0):
            s = s_scr[idx, :, col0:]
            m = jnp.max(s, axis=-1, keepdims=True)
            p_scr[idx, :, col0:] = jnp.exp(s - m).astype(BF16)
            m_scr[idx] = jnp.broadcast_to(m, (blk, LANES))
            return c

        lax.fori_loop(0, n_all, softmax, 0, unroll=True)

        def values(idx, c, g=g, place=place, n_blk=n_blk):
            nat, _, kcur, kwin, _ = place(idx)
            if n_blk > 1:
                ov = _dot(p_scr[idx], vd_scr[kwin, :])
            else:
                ov = _dot(p_scr[idx, :, LANES:], vd_scr[kcur, :])
            l = ov[:, LANES:]
            o_scr[g, nat, :] = ov[:, 0:LANES] / l
            l_scr[g, nat, :] = m_scr[idx] + jnp.log(l)
            return c

        lax.fori_loop(0, n_all, values, 0, unroll=True)

    def comb(r, c):
        sl = pl.ds(pl.multiple_of(r * rows, rows), rows)
        out = _combine([o_scr[g, sl, :] for g in range(3)], [l_scr[g, sl, :] for g in range(3)])
        att_ref[sl, :] = out.astype(att_ref.dtype)
        return c

    lax.fori_loop(0, seq // rows, comb, 0, unroll=True)


def attn_prompt(z, qn, kn, *, n_heads):
    bsz, seq, _ = z.shape
    assert seq % (BRANCHES[-1][1] * ATT_BLOCK) == 0
    hd = LANES
    kern = functools.partial(_attn_prompt_kernel, seq=seq, scale=hd ** -0.5)
    blk = lambda off: pl.BlockSpec((None, seq, hd), lambda b, h: (b, 0, off + h))
    return pl.pallas_call(
        kern,
        grid=(bsz, n_heads),
        in_specs=[blk(0), blk(n_heads), blk(2 * n_heads),
                  pl.BlockSpec((1, hd), lambda b, h: (0, 0)), pl.BlockSpec((1, hd), lambda b, h: (0, 0))],
        out_specs=[blk(0), blk(0), blk(0)],
        out_shape=[
            jax.ShapeDtypeStruct((bsz, seq, n_heads * hd), BF16),
            jax.ShapeDtypeStruct((bsz, seq, n_heads * hd), F32),
            jax.ShapeDtypeStruct((bsz, seq, n_heads * hd), F32),
        ],
        scratch_shapes=[
            pltpu.VMEM((seq, hd), F32),
            pltpu.VMEM((seq, hd), F32), pltpu.VMEM((seq, hd), F32), pltpu.VMEM((seq, hd), F32),
            pltpu.VMEM((seq, hd), BF16), pltpu.VMEM((seq + ATT_BLOCK, hd), BF16),
            pltpu.VMEM((seq + ATT_BLOCK, 2 * hd), BF16),
            pltpu.VMEM((seq // ATT_BLOCK, ATT_BLOCK, 2 * ATT_BLOCK), F32),
            pltpu.VMEM((seq // ATT_BLOCK, ATT_BLOCK, 2 * ATT_BLOCK), BF16),
            pltpu.VMEM((seq // ATT_BLOCK, ATT_BLOCK, hd), F32),
            pltpu.VMEM((3, seq, hd), F32),
            pltpu.VMEM((3, seq, hd), F32),
        ],
        compiler_params=_params(("arbitrary", "arbitrary"), 40),
        name="attn_prompt",
    )(z, z, z, qn.reshape(1, hd), kn.reshape(1, hd))


SAMPLE_PAD = 16


def _attn_sample_kernel(q_ref, k_ref, v_ref, ck_ref, cv_ref, qn_ref, kn_ref, att_ref, ko_ref, vo_ref,
                        q_scr, kn_scr, vn_scr, s_scr, v_scr, o_scr, k4_scr, v4_scr,
                        *, s_new, n_buf, n_heads, pc, scale):
    c = pl.program_id(1)
    n_ch = n_buf // pc
    pad = SAMPLE_PAD

    @pl.when(c == 0)
    def _():
        q_scr[...] = jnp.zeros_like(q_scr)
        kn_scr[...] = jnp.zeros_like(kn_scr)
        vn_scr[...] = jnp.zeros_like(vn_scr)
        for h in range(n_heads):
            lanes = slice(h * LANES, (h + 1) * LANES)
            k_new = _head_rms(k_ref[:, lanes].astype(F32), kn_ref[...])
            v_new = v_ref[:, lanes].astype(F32)
            ko_ref[:, lanes] = k_new
            vo_ref[:, lanes] = v_new
            q_scr[h, 0:s_new, :] = _head_rms(q_ref[:, lanes].astype(F32), qn_ref[...]) * scale
            kn_scr[h, 0:s_new, :] = k_new
            vn_scr[h, 0:s_new, :] = v_new

    mid = math.isqrt(n_heads)
    two_level = mid > 1 and mid * mid == n_heads and mid % 8 != 0
    if two_level:
        part = pc * n_heads // mid
        for a in range(mid):
            k4_scr[a] = ck_ref[pl.ds(a, part, stride=mid), :]
            v4_scr[a] = cv_ref[pl.ds(a, part, stride=mid), :]
    dst = pl.ds(pl.multiple_of(c * pc, pc), pc)
    for h in range(n_heads):
        if two_level:
            rows = pl.ds(h // mid, pc, stride=mid)
            k_h, v_h = k4_scr[h % mid, rows, :], v4_scr[h % mid, rows, :]
        else:
            rows = pl.ds(h, pc, stride=n_heads)
            k_h, v_h = ck_ref[rows, :], cv_ref[rows, :]
        s_scr[h, c] = _dot_nt(q_scr[h].astype(BF16), k_h.astype(BF16))
        v_scr[h, dst, :] = v_h.astype(BF16)

    @pl.when(c == n_ch - 1)
    def _():
        qi = lax.broadcasted_iota(jnp.int32, (pad, pc), 0)
        kj = lax.broadcasted_iota(jnp.int32, (pad, pc), 1)
        qi_n = lax.broadcasted_iota(jnp.int32, (pad, pad), 0)
        kj_n = lax.broadcasted_iota(jnp.int32, (pad, pad), 1)
        dist_n = qi_n - kj_n
        new_ok = jnp.logical_and(dist_n >= 0, kj_n < s_new)

        def finish(h, carry):
            q = q_scr[h].astype(BF16)
            s_n = _dot_nt(q, kn_scr[h].astype(BF16))
            s_c = [s_scr[h, cc] for cc in range(n_ch)]
            ps, pns, ls, ms = [], [], [], []
            for window, dil in BRANCHES:
                msk = []
                for cc in range(n_ch):
                    dist = n_buf + qi - (cc * pc + kj)
                    ok = jnp.logical_and((dist & (dil - 1)) == 0, dist <= window)
                    msk.append(jnp.where(ok, s_c[cc], NEG_INF))
                m_n = jnp.where(jnp.logical_and(new_ok, (dist_n & (dil - 1)) == 0), s_n, NEG_INF)
                m = jnp.max(m_n, axis=-1, keepdims=True)
                for cc in range(n_ch):
                    m = jnp.maximum(m, jnp.max(msk[cc], axis=-1, keepdims=True))
                p_n = jnp.exp(m_n - m)
                l = jnp.sum(p_n, axis=-1, keepdims=True)
                pb = []
                for cc in range(n_ch):
                    p = jnp.exp(msk[cc] - m)
                    l = l + jnp.sum(p, axis=-1, keepdims=True)
                    pb.append(p.astype(BF16))
                ps.append(pb)
                pns.append(p_n.astype(BF16))
                ls.append(l)
                ms.append(m)
            ov = _dot(jnp.concatenate(pns, axis=0), vn_scr[h].astype(BF16))
            for cc in range(n_ch):
                ov = ov + _dot(jnp.concatenate([ps[g][cc] for g in range(3)], axis=0),
                               v_scr[h, cc * pc:(cc + 1) * pc, :])
            outs = [ov[g * pad:(g + 1) * pad] / ls[g] for g in range(3)]
            lses = [jnp.broadcast_to(ms[g] + jnp.log(ls[g]), (pad, LANES)) for g in range(3)]
            o_scr[h] = _combine(outs, lses)
            return carry

        lax.fori_loop(0, n_heads, finish, 0, unroll=4)
        for h in range(n_heads):
            att_ref[:, h * LANES:(h + 1) * LANES] = o_scr[h, 0:s_new, :].astype(att_ref.dtype)


def attn_sample(z, cache_k, cache_v, row0, qn, kn, *, n_heads):
    bsz, s_new, _ = z.shape
    hd = LANES
    d_c = n_heads * hd
    n_buf = cache_k.shape[1] // n_heads
    assert n_buf >= BRANCHES[-1][0] and s_new <= SAMPLE_PAD
    pc = min(n_buf, 512)
    mid = math.isqrt(n_heads)
    kern = functools.partial(_attn_sample_kernel, s_new=s_new, n_buf=n_buf, n_heads=n_heads, pc=pc,
                             scale=hd ** -0.5)
    blk = lambda off: pl.BlockSpec((None, s_new, d_c), lambda b, c: (b, 0, off))
    cblk = pl.BlockSpec((None, pc * n_heads, hd), lambda b, c: (row0 + b, c, 0))
    vec = pl.BlockSpec((1, hd), lambda b, c: (0, 0))
    return pl.pallas_call(
        kern,
        grid=(bsz, n_buf // pc),
        in_specs=[blk(0), blk(1), blk(2), cblk, cblk, vec, vec],
        out_specs=[blk(0), blk(0), blk(0)],
        out_shape=[
            jax.ShapeDtypeStruct((bsz, s_new, d_c), BF16),
            jax.ShapeDtypeStruct((bsz, s_new, d_c), F32),
            jax.ShapeDtypeStruct((bsz, s_new, d_c), F32),
        ],
        scratch_shapes=[
            pltpu.VMEM((n_heads, SAMPLE_PAD, hd), F32),
            pltpu.VMEM((n_heads, SAMPLE_PAD, hd), F32),
            pltpu.VMEM((n_heads, SAMPLE_PAD, hd), F32),
            pltpu.VMEM((n_heads, n_buf // pc, SAMPLE_PAD, pc), F32),
            pltpu.VMEM((n_heads, n_buf, hd), BF16),
            pltpu.VMEM((n_heads, SAMPLE_PAD, hd), F32),
            pltpu.VMEM((mid, pc * n_heads // mid, hd), F32),
            pltpu.VMEM((mid, pc * n_heads // mid, hd), F32),
        ],
        compiler_params=_params(("arbitrary", "arbitrary"), 48),
        name="attn_sample",
    )(z, z, z, cache_k, cache_v, qn.reshape(1, hd), kn.reshape(1, hd))


def _sgu_kernel(gu_ref, gv_ref, lg_ref, lb_ref, w_ref, bt_ref, o_ref, vn_ref, vb_scr, *, rows, n_g, cd):
    t = w_ref.shape[1]
    gv = _gelu(gv_ref[...].astype(F32))
    mu = jnp.mean(gv, axis=-1, keepdims=True)
    xc = gv - mu
    var = jnp.mean(xc * xc, axis=-1, keepdims=True)
    vn = (xc * lax.rsqrt(var + LN_EPS)) * lg_ref[...] + lb_ref[...]
    vn_ref[...] = vn
    if rows < t:
        vb_scr[...] = jnp.zeros_like(vb_scr)
    vb_scr[0:rows, :] = vn.astype(BF16)
    ri = lax.broadcasted_iota(jnp.int32, (t, t), 0)
    ci = lax.broadcasted_iota(jnp.int32, (t, t), 1)
    for g in range(n_g):
        cols = slice(g * cd, (g + 1) * cd)
        wg = jnp.where(ri >= ci, w_ref[g], 0.0).astype(BF16)
        mixed = _dot(wg, vb_scr[:, cols])[0:rows, :] + bt_ref[:, g:g + 1]
        o_ref[:, cols] = (_gelu(gu_ref[:, cols].astype(F32)) * mixed).astype(o_ref.dtype)


def sgu(z, colblk_u, ln_g, ln_b, w_s, b_s):
    bsz, seq, _ = z.shape
    n_g = w_s.shape[0]
    dd = ln_g.shape[0]
    cd = dd // n_g
    t = min(seq, CHUNK)
    tp = max(t, LANES)
    w = jnp.pad(w_s[:, :t, :t], ((0, 0), (0, tp - t), (0, tp - t)))
    bt = jnp.transpose(b_s[:, :t])
    kern = functools.partial(_sgu_kernel, rows=t, n_g=n_g, cd=cd)
    return pl.pallas_call(
        kern,
        grid=(bsz, seq // t),
        in_specs=[
            pl.BlockSpec((None, t, dd), lambda b, c: (b, c, colblk_u)),
            pl.BlockSpec((None, t, dd), lambda b, c: (b, c, colblk_u + 1)),
            pl.BlockSpec((1, dd), lambda b, c: (0, 0)),
            pl.BlockSpec((1, dd), lambda b, c: (0, 0)),
            pl.BlockSpec((n_g, tp, tp), lambda b, c: (0, 0, 0)),
            pl.BlockSpec((t, n_g), lambda b, c: (0, 0)),
        ],
        out_specs=[
            pl.BlockSpec((None, t, dd), lambda b, c: (b, c, 0)),
            pl.BlockSpec((None, t, dd), lambda b, c: (b, c, 0)),
        ],
        out_shape=[
            jax.ShapeDtypeStruct((bsz, seq, dd), BF16),
            jax.ShapeDtypeStruct((bsz, seq, dd), F32),
        ],
        scratch_shapes=[pltpu.VMEM((tp, dd), BF16)],
        compiler_params=_params(("arbitrary", "arbitrary"), 40),
        name="sgu",
    )(z, z, ln_g.reshape(1, dd), ln_b.reshape(1, dd), w, bt)


def _in_proj(x, norm_g, w_in, i, tiles):
    if tiles["cast"]:
        return norm_matmul_cast(x, norm_g, w_in, i, tn=MIX_TILE)
    return norm_matmul(x, norm_g, w_in, tm=tiles["tm"], out_dtype=tiles["z_dtype"]), None


def _res_proj(x, a, b, w_out, i, tiles):
    if tiles["cast"]:
        return out_proj_cast(x, a, b, w_out, i, tn=MIX_TILE)
    return out_proj(x, a, b, w_out, tm=tiles["tm_out"]), None


def _even_layer(x, bsz, seq, h0_re, h0_im, pool_buf, start_pos, norm_g, w_in, w_out, i, s5p, pool_w, pool_scale,
                w_glu, b_glu, d_skip, tiles):
    t, d = x.shape
    pw_re, pw_im, b_blk, c_blk = s5p
    n_slab = b_blk.shape[0]
    d_a = n_slab * LANES
    z, w_in_b = _in_proj(x, norm_g, w_in, i, tiles)
    z = z.reshape(bsz, seq, -1)
    h0 = jnp.concatenate([h0_re.reshape(bsz, n_slab, 1, SLAB_STATE), h0_im.reshape(bsz, n_slab, 1, SLAB_STATE)], axis=-1)
    if tiles["s5_long"]:
        y_pre, h_last = s5_scan_long(z, 0, b_blk, c_blk, pw_re, pw_im, d_skip, h0)
    else:
        y_pre, h_last = s5_scan(z, 0, b_blk, c_blk, pw_re, pw_im, d_skip, h0, split_in=True)
    ya = glu(y_pre.reshape(t, d_a), w_glu, b_glu, i, tm=tiles["tm_glu"], tn=MIX_TILE)
    buf16 = jnp.pad(pool_buf, ((0, 0), (POOL_HIST - pool_buf.shape[1], 0), (0, 0)))
    yb, tail = pool(z, 1, buf16, pool_w, i, pool_scale, start_pos=start_pos, tc=256)
    x, w_out_b = _res_proj(x, ya, yb.reshape(t, -1), w_out, i, tiles)
    g_a = n_slab * SLAB_GROUPS
    h_re = h_last[..., :SLAB_STATE].reshape(bsz, g_a, S5_P)
    h_im = h_last[..., SLAB_STATE:].reshape(bsz, g_a, S5_P)
    return x, h_re, h_im, tail[:, POOL_HIST - pool_buf.shape[1]:], (w_in_b, w_out_b)


def _odd_layer(x, bsz, seq, k_buf, v_buf, norm_g, w_in, w_out, i, qn, kn, ln_g, ln_b, w_s, b_s, n_heads, tiles):
    t, d = x.shape
    d_c = n_heads * LANES
    z, w_in_b = _in_proj(x, norm_g, w_in, i, tiles)
    z = z.reshape(bsz, seq, -1)
    if k_buf is None:
        att, k_new, v_new = attn_prompt(z, qn, kn, n_heads=n_heads)
    else:
        att, k_new, v_new = attn_sample(z, k_buf, v_buf, i * bsz, qn, kn, n_heads=n_heads)
    dd = ln_g.shape[0]
    sg, vn = sgu(z, (3 * d_c) // dd, ln_g, ln_b, w_s, b_s)
    x, w_out_b = _res_proj(x, att.reshape(t, d_c), sg.reshape(t, dd), w_out, i, tiles)
    hd = LANES
    return x, k_new.reshape(bsz, seq, n_heads, hd), v_new.reshape(bsz, seq, n_heads, hd), vn, (w_in_b, w_out_b)


def kernel(x_prompt, x_sample, state_s5_re, state_s5_im, state_pool, cache_k, cache_v, norm_mix, norm_ffn, ev_w_in, ev_w_out, s5_lambda_re, s5_lambda_im, s5_log_dt, s5_b_re, s5_b_im, s5_c_re, s5_c_im, s5_d, s5_w_glu, s5_b_glu, pool_w, pool_scale, od_w_in, od_w_out, q_norm, k_norm, sgu_ln_g, sgu_ln_b, sgu_w, sgu_b, ffn_w1, ffn_w3, ffn_w2):
    bp, lp, d = x_prompt.shape
    bs, ls, _ = x_sample.shape
    depth = norm_mix.shape[0]
    n_heads = cache_k.shape[3]
    xp = x_prompt.reshape(bp * lp, d)
    xs = x_sample.reshape(bs * ls, d)
    tiles_p = dict(cast=False, tm=1024, tm_glu=1024, tm_out=1024, s5_long=True, z_dtype=BF16)
    tiles_s = dict(cast=True, tm_glu=bs * ls, s5_long=False)
    g_a, p_a = s5_lambda_re.shape[1:]
    w_glu_b, pool_w_b = s5_w_glu.astype(BF16), pool_w.astype(BF16)
    d_c = n_heads * LANES
    cache_k2 = cache_k.reshape(-1, cache_k.shape[2] * n_heads, LANES)
    cache_v2 = cache_v.reshape(-1, cache_v.shape[2] * n_heads, LANES)

    s5r_p, s5i_p, pool_p, k_p, v_p = [], [], [], [], []
    s5r_s, s5i_s, pool_s, k_s, v_s, sgu_s = [], [], [], [], [], []
    for l in range(depth):
        i = l // 2
        if l % 2 == 0:
            pw_re, pw_im, bb_re, bb_im = s5_prep(s5_lambda_re[i], s5_lambda_im[i], s5_log_dt[i], s5_b_re[i], s5_b_im[i])
            b_blk, c_blk = s5_block_matrices(bb_re, bb_im, s5_c_re[i], s5_c_im[i])
            s5p = (pw_re, pw_im, b_blk, c_blk)
            rest = (i, s5p, pool_w_b, pool_scale[i], w_glu_b, s5_b_glu[i], s5_d[i])
            xs, hr, hi, buf, (w_in_b, w_out_b) = _even_layer(
                xs, bs, ls, state_s5_re[i], state_s5_im[i], state_pool[i], PAST_LEN, norm_mix[l], ev_w_in, ev_w_out,
                *rest, tiles_s)
            s5r_s.append(hr); s5i_s.append(hi); pool_s.append(buf)
            zero_h = jnp.zeros((bp, g_a, p_a), F32)
            zero_buf = jnp.zeros((bp, state_pool.shape[2], state_pool.shape[3]), F32)
            xp, hr, hi, buf, _ = _even_layer(xp, bp, lp, zero_h, zero_h, zero_buf, 0, norm_mix[l], w_in_b, w_out_b,
                                             *rest, tiles_p)
            s5r_p.append(hr); s5i_p.append(hi); pool_p.append(buf)
        else:
            rest = (i, q_norm[i], k_norm[i], sgu_ln_g[i], sgu_ln_b[i], sgu_w[i], sgu_b[i], n_heads)
            xs, nk, nv, vrows, (w_in_b, w_out_b) = _odd_layer(xs, bs, ls, cache_k2, cache_v2, norm_mix[l], od_w_in,
                                                              od_w_out, *rest, tiles_s)
            k_s.append(nk); v_s.append(nv); sgu_s.append(vrows)
            xp, nk, nv, _, _ = _odd_layer(xp, bp, lp, None, None, norm_mix[l], w_in_b, w_out_b, *rest, tiles_p)
            k_p.append(nk); v_p.append(nv)
        if l == 0:
            xs, *ffn_wb = ffn_cast(xs, norm_ffn[l], ffn_w1, ffn_w3, ffn_w2, l, tf=FFN_TILE)
        else:
            xs = ffn(xs, norm_ffn[l], *ffn_wb, tm=bs * ls)
        if l + 1 < depth:
            xp, ffn_wb = ffn(xp, norm_ffn[l], *ffn_wb, tm=512, nxt=(ffn_w1, ffn_w3, ffn_w2, l + 1))
        else:
            xp = ffn(xp, norm_ffn[l], *ffn_wb, tm=512)
    return (xp.reshape(bp, lp, d), xs.reshape(bs, ls, d),
            jnp.stack(s5r_p), jnp.stack(s5i_p), jnp.stack(pool_p), jnp.stack(k_p), jnp.stack(v_p),
            jnp.stack(s5r_s), jnp.stack(s5i_s), jnp.stack(pool_s), jnp.stack(k_s), jnp.stack(v_s),
            jnp.stack(sgu_s))
```

```python
import functools
import math

import jax
import jax.numpy as jnp
from jax import lax
from jax.experimental import pallas as pl
from jax.experimental.pallas import tpu as pltpu

F32 = jnp.float32
BF16 = jnp.bfloat16

RMS_EPS = 1e-6
LN_EPS = 1e-5
NEG_INF = -1e30

LANES = 128
SUBLANES = 8
ATT_BLOCK = 128
CHUNK = 128
POOL_WINDOWS = (2, 4, 8, 16)
POOL_HIST = 16
BRANCHES = ((128, 1), (512, 4), (2048, 16))
S5_GRP = 16
S5_P = 64
SLAB_GROUPS = LANES // S5_GRP
SLAB_STATE = SLAB_GROUPS * S5_P
PAST_LEN = 8192
FFN_TILE = 256
MIX_TILE = 512


def _params(sem, vmem_mib):
    return pltpu.CompilerParams(dimension_semantics=sem, vmem_limit_bytes=vmem_mib << 20)


def _gelu(x):
    return 0.5 * x * (1.0 + lax.erf(x * (1.0 / math.sqrt(2.0))))


def _sigmoid(x):
    return 1.0 / (1.0 + jnp.exp(-x))


def _split_bf16(a):
    hi = a.astype(BF16)
    lo = (a - hi.astype(F32)).astype(BF16)
    return hi, lo


def _dot(a, b):
    return jnp.dot(a, b, preferred_element_type=F32)


def _dot_nt(a, b):
    return lax.dot_general(a, b, (((1,), (1,)), ((), ())), preferred_element_type=F32)


def _rms_rows_to(x_ref, g_ref, h_ref, rows):
    step = 16 if rows % 16 == 0 else rows

    def body(r, c):
        sl = pl.ds(pl.multiple_of(r * step, step), step)
        x = x_ref[sl, :]
        ms = jnp.mean(x * x, axis=-1, keepdims=True)
        h_ref[sl, :] = ((x * lax.rsqrt(ms + RMS_EPS)) * g_ref[...]).astype(h_ref.dtype)
        return c

    lax.fori_loop(0, rows // step, body, 0, unroll=True)


def _norm_matmul_kernel(x_ref, g_ref, w_ref, o_ref, h_ref):
    first = pl.program_id(1) == 0

    def tile():
        o_ref[...] = _dot(h_ref[...], w_ref[...]).astype(o_ref.dtype)

    @pl.when(first)
    def _():
        _rms_rows_to(x_ref, g_ref, h_ref, x_ref.shape[0])
        tile()

    @pl.when(jnp.logical_not(first))
    def _():
        tile()


def norm_matmul(x, g, w, *, tm, out_dtype=F32):
    t, d = x.shape
    n_t, _, tn = w.shape
    return pl.pallas_call(
        _norm_matmul_kernel,
        grid=(t // tm, n_t),
        in_specs=[
            pl.BlockSpec((tm, d), lambda i, j: (i, 0)),
            pl.BlockSpec((1, d), lambda i, j: (0, 0)),
            pl.BlockSpec((None, d, tn), lambda i, j: (j, 0, 0)),
        ],
        out_specs=pl.BlockSpec((tm, tn), lambda i, j: (i, j)),
        out_shape=jax.ShapeDtypeStruct((t, n_t * tn), out_dtype),
        scratch_shapes=[pltpu.VMEM((tm, d), BF16)],
        compiler_params=_params(("arbitrary", "arbitrary"), 58),
        name="norm_matmul",
    )(x, g.reshape(1, d), w)


def _norm_matmul_cast_kernel(x_ref, g_ref, w_ref, o_ref, wb_ref, h_ref):
    @pl.when(pl.program_id(0) == 0)
    def _():
        _rms_rows_to(x_ref, g_ref, h_ref, x_ref.shape[0])

    wb_ref[...] = w_ref[...].astype(BF16)
    o_ref[...] = _dot(h_ref[...], wb_ref[...]).astype(o_ref.dtype)


def norm_matmul_cast(x, g, w, layer, *, tn):
    t, d = x.shape
    n = w.shape[2]
    return pl.pallas_call(
        _norm_matmul_cast_kernel,
        grid=(n // tn,),
        in_specs=[
            pl.BlockSpec((t, d), lambda j: (0, 0)),
            pl.BlockSpec((1, d), lambda j: (0, 0)),
            pl.BlockSpec((None, d, tn), lambda j: (layer, 0, j)),
        ],
        out_specs=[
            pl.BlockSpec((t, tn), lambda j: (0, j)),
            pl.BlockSpec((None, d, tn), lambda j: (j, 0, 0)),
        ],
        out_shape=[
            jax.ShapeDtypeStruct((t, n), F32),
            jax.ShapeDtypeStruct((n // tn, d, tn), BF16),
        ],
        scratch_shapes=[pltpu.VMEM((t, d), BF16)],
        compiler_params=_params(("arbitrary",), 48),
        name="norm_matmul_cast",
    )(x, g.reshape(1, d), w)


FFN_OUT_CHUNK = 512


def _ffn_step(first, x_ref, g_ref, w1_ref, w3_ref, w2_ref, o_ref, h_ref, side_work=None):
    def tile(acc_ref):
        if side_work is not None:
            side_work()
        h = h_ref[...]
        a = _dot(h, w1_ref[...])
        b = _dot(h, w3_ref[...])
        u = ((a * _sigmoid(a)) * b).astype(BF16)
        for c in range(0, o_ref.shape[1], FFN_OUT_CHUNK):
            cols = slice(c, c + FFN_OUT_CHUNK)
            o_ref[:, cols] = acc_ref[:, cols] + _dot(u, w2_ref[:, cols])

    @pl.when(first)
    def _():
        _rms_rows_to(x_ref, g_ref, h_ref, x_ref.shape[0])
        tile(x_ref)

    @pl.when(jnp.logical_not(first))
    def _():
        tile(o_ref)


def _ffn_kernel(x_ref, g_ref, w1_ref, w3_ref, w2_ref, o_ref, h_ref):
    _ffn_step(pl.program_id(1) == 0, x_ref, g_ref, w1_ref, w3_ref, w2_ref, o_ref, h_ref)


def _ffn_next_kernel(x_ref, g_ref, w1_ref, w3_ref, w2_ref, n1_ref, n3_ref, n2_ref,
                     o_ref, c1_ref, c3_ref, c2_ref, h_ref):
    def cast_next():
        c1_ref[...] = n1_ref[...].astype(BF16)
        c3_ref[...] = n3_ref[...].astype(BF16)
        c2_ref[...] = n2_ref[...].astype(BF16)

    _ffn_step(pl.program_id(1) == 0, x_ref, g_ref, w1_ref, w3_ref, w2_ref, o_ref, h_ref, side_work=cast_next)


def ffn(x, g, w1, w3, w2, *, tm, nxt=None):
    t, d = x.shape
    n_f, _, tf = w1.shape
    n_m = t // tm
    specs = [
        pl.BlockSpec((tm, d), lambda i, j: (i, 0)),
        pl.BlockSpec((1, d), lambda i, j: (0, 0)),
        pl.BlockSpec((None, d, tf), lambda i, j: (j, 0, 0)),
        pl.BlockSpec((None, d, tf), lambda i, j: (j, 0, 0)),
        pl.BlockSpec((tf, d), lambda i, j: (j, 0)),
    ]
    out_spec = pl.BlockSpec((tm, d), lambda i, j: (i, 0))
    out_shape = jax.ShapeDtypeStruct((t, d), F32)
    common = dict(grid=(n_m, n_f), scratch_shapes=[pltpu.VMEM((tm, d), BF16)],
                  compiler_params=_params(("arbitrary", "arbitrary"), 56))
    if nxt is None:
        return pl.pallas_call(_ffn_kernel, in_specs=specs, out_specs=out_spec, out_shape=out_shape, name="ffn",
                              **common)(x, g.reshape(1, d), w1, w3, w2)
    n1, n3, n2, layer = nxt
    piece = d // n_m
    assert d % n_m == 0 and piece % LANES == 0
    col = pl.BlockSpec((None, piece, tf), lambda i, j: (layer, i, j))
    row = pl.BlockSpec((None, tf, piece), lambda i, j: (layer, j, i))
    col_out = pl.BlockSpec((None, piece, tf), lambda i, j: (j, i, 0))
    row_out = pl.BlockSpec((tf, piece), lambda i, j: (j, i))
    out, c1, c3, c2 = pl.pallas_call(
        _ffn_next_kernel,
        in_specs=specs + [col, col, row],
        out_specs=[out_spec, col_out, col_out, row_out],
        out_shape=[out_shape, jax.ShapeDtypeStruct(w1.shape, BF16), jax.ShapeDtypeStruct(w3.shape, BF16),
                   jax.ShapeDtypeStruct(w2.shape, BF16)],
        name="ffn_next", **common,
    )(x, g.reshape(1, d), w1, w3, w2, n1, n3, n2)
    return out, (c1, c3, c2)


def _ffn_cast_kernel(x_ref, g_ref, w1_ref, w3_ref, w2_ref, o_ref, w1b_ref, w3b_ref, w2b_ref, h_ref):
    w1b_ref[...] = w1_ref[...].astype(BF16)
    w3b_ref[...] = w3_ref[...].astype(BF16)
    w2b_ref[...] = w2_ref[...].astype(BF16)
    _ffn_step(pl.program_id(0) == 0, x_ref, g_ref, w1b_ref, w3b_ref, w2b_ref, o_ref, h_ref)


def ffn_cast(x, g, w1, w3, w2, layer, *, tf):
    t, d = x.shape
    f = w1.shape[2]
    return pl.pallas_call(
        _ffn_cast_kernel,
        grid=(f // tf,),
        in_specs=[
            pl.BlockSpec((t, d), lambda j: (0, 0)),
            pl.BlockSpec((1, d), lambda j: (0, 0)),
            pl.BlockSpec((None, d, tf), lambda j: (layer, 0, j)),
            pl.BlockSpec((None, d, tf), lambda j: (layer, 0, j)),
            pl.BlockSpec((None, tf, d), lambda j: (layer, j, 0)),
        ],
        out_specs=[
            pl.BlockSpec((t, d), lambda j: (0, 0)),
            pl.BlockSpec((None, d, tf), lambda j: (j, 0, 0)),
            pl.BlockSpec((None, d, tf), lambda j: (j, 0, 0)),
            pl.BlockSpec((tf, d), lambda j: (j, 0)),
        ],
        out_shape=[
            jax.ShapeDtypeStruct((t, d), F32),
            jax.ShapeDtypeStruct((f // tf, d, tf), BF16),
            jax.ShapeDtypeStruct((f // tf, d, tf), BF16),
            jax.ShapeDtypeStruct((f, d), BF16),
        ],
        scratch_shapes=[pltpu.VMEM((t, d), BF16)],
        compiler_params=_params(("arbitrary",), 48),
        name="ffn_cast",
    )(x, g.reshape(1, d), w1, w3, w2)


def _out_proj_kernel(x_ref, a_ref, b_ref, wa_ref, wb_ref, o_ref):
    o_ref[...] = x_ref[...] + _dot(a_ref[...], wa_ref[...]) + _dot(b_ref[...], wb_ref[...])


def out_proj(x, a, b, w_pair, *, tm):
    t, d = x.shape
    wa, wb = w_pair
    n_t, k, tn = wa.shape
    assert a.shape[1] == k and b.shape[1] == k and wb.shape == wa.shape
    return pl.pallas_call(
        _out_proj_kernel,
        grid=(t // tm, n_t),
        in_specs=[
            pl.BlockSpec((tm, tn), lambda i, j: (i, j)),
            pl.BlockSpec((tm, k), lambda i, j: (i, 0)),
            pl.BlockSpec((tm, k), lambda i, j: (i, 0)),
            pl.BlockSpec((None, k, tn), lambda i, j: (j, 0, 0)),
            pl.BlockSpec((None, k, tn), lambda i, j: (j, 0, 0)),
        ],
        out_specs=pl.BlockSpec((tm, tn), lambda i, j: (i, j)),
        out_shape=jax.ShapeDtypeStruct((t, d), F32),
        compiler_params=_params(("arbitrary", "arbitrary"), 48),
        name="out_proj",
    )(x, a, b, wa, wb)


def _out_proj_cast_kernel(x_ref, a_ref, b_ref, wa_ref, wb_ref, o_ref, wab_ref, wbb_ref):
    wab_ref[...] = wa_ref[...].astype(BF16)
    wbb_ref[...] = wb_ref[...].astype(BF16)
    o_ref[...] = x_ref[...] + _dot(a_ref[...], wab_ref[...]) + _dot(b_ref[...], wbb_ref[...])


def out_proj_cast(x, a, b, w, layer, *, tn):
    t, d = x.shape
    k = a.shape[1]
    assert b.shape[1] == k and w.shape[1] == 2 * k
    out, wa_b, wb_b = pl.pallas_call(
        _out_proj_cast_kernel,
        grid=(d // tn,),
        in_specs=[
            pl.BlockSpec((t, tn), lambda j: (0, j)),
            pl.BlockSpec((t, k), lambda j: (0, 0)),
            pl.BlockSpec((t, k), lambda j: (0, 0)),
            pl.BlockSpec((None, k, tn), lambda j: (layer, 0, j)),
            pl.BlockSpec((None, k, tn), lambda j: (layer, 1, j)),
        ],
        out_specs=[
            pl.BlockSpec((t, tn), lambda j: (0, j)),
            pl.BlockSpec((None, k, tn), lambda j: (j, 0, 0)),
            pl.BlockSpec((None, k, tn), lambda j: (j, 0, 0)),
        ],
        out_shape=[
            jax.ShapeDtypeStruct((t, d), F32),
            jax.ShapeDtypeStruct((d // tn, k, tn), BF16),
            jax.ShapeDtypeStruct((d // tn, k, tn), BF16),
        ],
        compiler_params=_params(("arbitrary",), 48),
        name="out_proj_cast",
    )(x, a, b, w, w)
    return out, (wa_b, wb_b)


def _s5_prep_kernel(lr_ref, li_ref, ldt_ref, lrx_ref, lix_ref, ldtx_ref, br_ref, bi_ref,
                    pwr_ref, pwi_ref, bbr_ref, bbi_ref):
    dt = jnp.exp(ldt_ref[...])
    mag = jnp.exp(lr_ref[...] * dt)
    ang = li_ref[...] * dt
    p_r, p_i = mag * jnp.cos(ang), mag * jnp.sin(ang)
    c_r, c_i = p_r, p_i
    pwr_ref[0], pwi_ref[0] = c_r, c_i
    for j in range(1, SUBLANES):
        c_r, c_i = c_r * p_r - c_i * p_i, c_r * p_i + c_i * p_r
        pwr_ref[j], pwi_ref[j] = c_r, c_i
    lr, li = lrx_ref[...], lix_ref[...]
    dtx = jnp.exp(ldtx_ref[...])
    magx = jnp.exp(lr * dtx)
    angx = li * dtx
    nr, ni = magx * jnp.cos(angx) - 1.0, magx * jnp.sin(angx)
    den = lr * lr + li * li
    qr = (nr * lr + ni * li) / den
    qi = (ni * lr - nr * li) / den
    br, bi = br_ref[...], bi_ref[...]
    bbr_ref[...] = qr * br - qi * bi
    bbi_ref[...] = qr * bi + qi * br


def s5_prep(lam_re, lam_im, log_dt, b_re, b_im):
    g, p = lam_re.shape
    h = b_re.shape[2]
    n_slab = g // SLAB_GROUPS
    slab = lambda a: a.reshape(n_slab, SLAB_GROUPS * p)
    rep = lambda a: jnp.repeat(a, h, axis=1)
    ldt_gp = jnp.broadcast_to(log_dt[:, None], (g, p))
    outs = pl.pallas_call(
        _s5_prep_kernel,
        out_shape=[
            jax.ShapeDtypeStruct((SUBLANES, n_slab, SLAB_GROUPS * p), F32),
            jax.ShapeDtypeStruct((SUBLANES, n_slab, SLAB_GROUPS * p), F32),
            jax.ShapeDtypeStruct((g, p * h), F32),
            jax.ShapeDtypeStruct((g, p * h), F32),
        ],
        name="s5_prep",
    )(slab(lam_re), slab(lam_im), slab(ldt_gp), rep(lam_re), rep(lam_im), rep(ldt_gp),
      b_re.reshape(g, p * h), b_im.reshape(g, p * h))
    pw_re, pw_im, bb_re, bb_im = outs
    pw_re = jnp.transpose(pw_re, (1, 0, 2))
    pw_im = jnp.transpose(pw_im, (1, 0, 2))
    return pw_re, pw_im, bb_re.reshape(g, p, h), bb_im.reshape(g, p, h)


def s5_block_matrices(bb_re, bb_im, c_re, c_im):
    g, p, h = bb_re.shape
    n_slab = g // SLAB_GROUPS
    eye = jnp.eye(SLAB_GROUPS, dtype=F32)

    def in_map(bb):
        t = bb.reshape(n_slab, SLAB_GROUPS, p, h)
        return jnp.einsum("kgph,gj->kghjp", t, eye).reshape(n_slab, SLAB_GROUPS * h, SLAB_GROUPS * p)

    def out_map(c):
        t = c.reshape(n_slab, SLAB_GROUPS, h, p)
        return jnp.einsum("kghp,gj->kgpjh", t, eye).reshape(n_slab, SLAB_GROUPS * p, SLAB_GROUPS * h)

    b_blk = jnp.concatenate([in_map(bb_re), in_map(bb_im)], axis=2)
    c_blk = jnp.concatenate([out_map(c_re), -out_map(c_im)], axis=1)
    return b_blk, c_blk


def _s5_scan_kernel(u_ref, bblk_ref, cblk_ref, pwr_ref, pwi_ref, d_ref, h0_ref,
                    y_ref, hl_ref, h_scr, bh_scr, bl_scr, ch_scr, *, seq, row_chunk, split_in):
    ns = SLAB_STATE

    @pl.when(pl.program_id(1) == 0)
    def _():
        bh, bl = _split_bf16(bblk_ref[...])
        bh_scr[...], bl_scr[...] = bh, bl
        ch_scr[...] = cblk_ref[...].astype(BF16)

    n_chunks = seq // row_chunk

    def proj_in(r, c):
        sl = pl.ds(pl.multiple_of(r * row_chunk, row_chunk), row_chunk)
        if split_in:
            uh, ul = _split_bf16(u_ref[sl, :].astype(F32))
            h_scr[sl, :] = _dot(uh, bh_scr[...]) + _dot(ul, bh_scr[...]) + _dot(uh, bl_scr[...])
        else:
            h_scr[sl, :] = _dot(u_ref[sl, :].astype(BF16), bh_scr[...])
        return c

    lax.fori_loop(0, n_chunks, proj_in, 0)

    rowid = lax.broadcasted_iota(jnp.int32, (SUBLANES, LANES), 0)
    for c in range(ns // LANES):
        re_l = slice(c * LANES, (c + 1) * LANES)
        im_l = slice(ns + c * LANES, ns + (c + 1) * LANES)
        p_r, p_i = pwr_ref[:, re_l], pwi_ref[:, re_l]
        steps = []
        for dist in (1, 2, 4):
            a_r = jnp.where(rowid >= dist, jnp.broadcast_to(p_r[dist - 1:dist], (SUBLANES, LANES)), 0.0)
            a_i = jnp.where(rowid >= dist, jnp.broadcast_to(p_i[dist - 1:dist], (SUBLANES, LANES)), 0.0)
            steps.append((dist, a_r, a_i))
        c_r = jnp.broadcast_to(h0_ref[:, re_l], (SUBLANES, LANES))
        c_i = jnp.broadcast_to(h0_ref[:, im_l], (SUBLANES, LANES))

        last = slice(SUBLANES - 1, SUBLANES)
        full = (SUBLANES, LANES)
        p8_r, p8_i = jnp.broadcast_to(p_r[last], full), jnp.broadcast_to(p_i[last], full)
        n_groups = seq // SUBLANES
        per_it = min(4, n_groups)

        def scan_rows(it, carry, re_l=re_l, im_l=im_l, p_r=p_r, p_i=p_i, p8_r=p8_r, p8_i=p8_i, steps=steps):
            base = pl.multiple_of(it * (per_it * SUBLANES), per_it * SUBLANES)
            sls = [pl.ds(base + j * SUBLANES, SUBLANES) for j in range(per_it)]
            loc = []
            for sl in sls:
                r, i = h_scr[sl, re_l], h_scr[sl, im_l]
                for dist, a_r, a_i in steps:
                    s_r, s_i = pltpu.roll(r, dist, 0), pltpu.roll(i, dist, 0)
                    r, i = r + (s_r * a_r - s_i * a_i), i + (s_r * a_i + s_i * a_r)
                loc.append((r, i))
            c_r, c_i = carry
            outs = []
            for r, i in loc:
                outs.append((r + (c_r * p_r - c_i * p_i), i + (c_r * p_i + c_i * p_r)))
                e_r, e_i = jnp.broadcast_to(r[last], full), jnp.broadcast_to(i[last], full)
                c_r, c_i = e_r + (c_r * p8_r - c_i * p8_i), e_i + (c_r * p8_i + c_i * p8_r)
            for sl, (r, i) in zip(sls, outs):
                h_scr[sl, re_l], h_scr[sl, im_l] = r, i
            return c_r, c_i

        c_r, c_i = lax.fori_loop(0, n_groups // per_it, scan_rows, (c_r, c_i))
        hl_ref[:, re_l] = c_r[0:1]
        hl_ref[:, im_l] = c_i[0:1]

    def proj_out(r, c):
        sl = pl.ds(pl.multiple_of(r * row_chunk, row_chunk), row_chunk)
        y = _dot(h_scr[sl, :].astype(BF16), ch_scr[...]) + d_ref[...] * u_ref[sl, :].astype(F32)
        y_ref[sl, :] = _gelu(y)
        return c

    lax.fori_loop(0, n_chunks, proj_out, 0)


def s5_scan(z, col0, b_blk, c_blk, pw_re, pw_im, d_skip, h0, *, split_in):
    bsz, seq, _ = z.shape
    n_slab = b_blk.shape[0]
    ns2 = 2 * SLAB_STATE
    row_chunk = min(seq, 1024)
    kern = functools.partial(_s5_scan_kernel, seq=seq, row_chunk=row_chunk, split_in=split_in)
    return pl.pallas_call(
        kern,
        grid=(n_slab, bsz),
        in_specs=[
            pl.BlockSpec((None, seq, LANES), lambda k, b: (b, 0, col0 + k)),
            pl.BlockSpec((None, LANES, ns2), lambda k, b: (k, 0, 0)),
            pl.BlockSpec((None, ns2, LANES), lambda k, b: (k, 0, 0)),
            pl.BlockSpec((None, SUBLANES, SLAB_STATE), lambda k, b: (k, 0, 0)),
            pl.BlockSpec((None, SUBLANES, SLAB_STATE), lambda k, b: (k, 0, 0)),
            pl.BlockSpec((1, LANES), lambda k, b: (0, k)),
            pl.BlockSpec((None, None, 1, ns2), lambda k, b: (b, k, 0, 0)),
        ],
        out_specs=[
            pl.BlockSpec((None, seq, LANES), lambda k, b: (b, 0, k)),
            pl.BlockSpec((None, None, 1, ns2), lambda k, b: (b, k, 0, 0)),
        ],
        out_shape=[
            jax.ShapeDtypeStruct((bsz, seq, n_slab * LANES), F32),
            jax.ShapeDtypeStruct((bsz, n_slab, 1, ns2), F32),
        ],
        scratch_shapes=[
            pltpu.VMEM((seq, ns2), F32),
            pltpu.VMEM((LANES, ns2), BF16), pltpu.VMEM((LANES, ns2), BF16),
            pltpu.VMEM((ns2, LANES), BF16),
        ],
        compiler_params=_params(("arbitrary", "arbitrary"), 40),
        name="s5_scan",
    )(z, b_blk, c_blk, pw_re, pw_im, d_skip.reshape(1, -1), h0)


def _s5_seg_kernel(u_ref, bblk_ref, cblk_ref, pwr_ref, pwi_ref, d_ref, h0_ref,
                   y_ref, hl_ref, up_scr, h_scr, yp_scr, w_scr, bh_scr, ch_scr, *, seq, row_chunk):
    ns = SLAB_STATE
    nseg = SUBLANES
    seg = seq // nseg
    n_lb = ns // LANES
    full = (SUBLANES, LANES)
    lanes = [(slice(c * LANES, (c + 1) * LANES), slice(ns + c * LANES, ns + (c + 1) * LANES)) for c in range(n_lb)]
    last = slice(SUBLANES - 1, SUBLANES)

    @pl.when(pl.program_id(1) == 0)
    def _():
        bh_scr[...] = bblk_ref[...].astype(BF16)
        ch_scr[...] = cblk_ref[...].astype(BF16)
        for re_l, im_l in lanes:
            p_r, p_i = pwr_ref[:, re_l], pwi_ref[:, re_l]
            p8_r, p8_i = jnp.broadcast_to(p_r[last], full), jnp.broadcast_to(p_i[last], full)
            w_scr[0:SUBLANES, re_l], w_scr[0:SUBLANES, im_l] = p_r, p_i

            def grow(gi, carry, re_l=re_l, im_l=im_l, p8_r=p8_r, p8_i=p8_i):
                w_r, w_i = carry
                w_r, w_i = w_r * p8_r - w_i * p8_i, w_r * p8_i + w_i * p8_r
                sl = pl.ds(pl.multiple_of(gi * SUBLANES, SUBLANES), SUBLANES)
                w_scr[sl, re_l], w_scr[sl, im_l] = w_r, w_i
                return w_r, w_i

            lax.fori_loop(1, seg // SUBLANES, grow, (p_r, p_i))

    for s in range(nseg):
        up_scr[pl.ds(s, seg, stride=nseg), :] = u_ref[s * seg:(s + 1) * seg, :].astype(F32)

    def proj_in(r, c):
        sl = pl.ds(pl.multiple_of(r * row_chunk, row_chunk), row_chunk)
        h_scr[sl, :] = _dot(up_scr[sl, :].astype(BF16), bh_scr[...])
        return c

    lax.fori_loop(0, seq // row_chunk, proj_in, 0, unroll=True)

    lam = [(jnp.broadcast_to(pwr_ref[0:1, re_l], full), jnp.broadcast_to(pwi_ref[0:1, re_l], full))
           for re_l, _ in lanes]
    per_it = 2

    def scan_t(it, carry):
        base = pl.multiple_of(it * (per_it * SUBLANES), per_it * SUBLANES)
        sls = [pl.ds(base + j * SUBLANES, SUBLANES) for j in range(per_it)]
        bu = [[(h_scr[sl, re_l], h_scr[sl, im_l]) for re_l, im_l in lanes] for sl in sls]
        hs, outs = list(carry), []
        for j in range(per_it):
            hs = [(bu[j][c][0] + (hs[c][0] * lam[c][0] - hs[c][1] * lam[c][1]),
                   bu[j][c][1] + (hs[c][0] * lam[c][1] + hs[c][1] * lam[c][0])) for c in range(n_lb)]
            outs.append(hs)
        for sl, row in zip(sls, outs):
            for (re_l, im_l), (h_r, h_i) in zip(lanes, row):
                h_scr[sl, re_l], h_scr[sl, im_l] = h_r, h_i
        return tuple(hs)

    zero = jnp.zeros(full, F32)
    ends = lax.fori_loop(0, seg // per_it, scan_t, tuple((zero, zero) for _ in range(n_lb)), unroll=True)

    rowid = lax.broadcasted_iota(jnp.int32, full, 0)
    enter = []
    for (re_l, im_l), (e_r, e_i) in zip(lanes, ends):
        ws_r, ws_i = w_scr[seg - 1:seg, re_l], w_scr[seg - 1:seg, im_l]
        c_r, c_i = h0_ref[:, re_l], h0_ref[:, im_l]
        cv_r, cv_i = jnp.broadcast_to(c_r, full), jnp.broadcast_to(c_i, full)
        for s in range(1, nseg + 1):
            c_r, c_i = (e_r[s - 1:s] + (c_r * ws_r - c_i * ws_i), e_i[s - 1:s] + (c_r * ws_i + c_i * ws_r))
            if s < nseg:
                cv_r = jnp.where(rowid == s, jnp.broadcast_to(c_r, full), cv_r)
                cv_i = jnp.where(rowid == s, jnp.broadcast_to(c_i, full), cv_i)
        hl_ref[:, re_l], hl_ref[:, im_l] = c_r, c_i
        enter.append((cv_r, cv_i))

    def fix_t(gi, c):
        wsl = pl.ds(pl.multiple_of(gi * SUBLANES, SUBLANES), SUBLANES)
        wv = [(w_scr[wsl, re_l], w_scr[wsl, im_l]) for re_l, im_l in lanes]
        base = pl.multiple_of(gi * (SUBLANES * SUBLANES), SUBLANES * SUBLANES)
        for j in range(SUBLANES):
            sl = pl.ds(base + j * SUBLANES, SUBLANES)
            for (re_l, im_l), (cv_r, cv_i), (wv_r, wv_i) in zip(lanes, enter, wv):
                w_r = jnp.broadcast_to(wv_r[j:j + 1], full)
                w_i = jnp.broadcast_to(wv_i[j:j + 1], full)
                h_r = h_scr[sl, re_l] + (w_r * cv_r - w_i * cv_i)
                h_i = h_scr[sl, im_l] + (w_r * cv_i + w_i * cv_r)
                h_scr[sl, re_l], h_scr[sl, im_l] = h_r, h_i
        return c

    lax.fori_loop(0, seg // SUBLANES, fix_t, 0, unroll=True)

    def proj_out(r, c):
        sl = pl.ds(pl.multiple_of(r * row_chunk, row_chunk), row_chunk)
        y = _dot(h_scr[sl, :].astype(BF16), ch_scr[...]) + d_ref[...] * up_scr[sl, :]
        yp_scr[sl, :] = _gelu(y)
        return c

    lax.fori_loop(0, seq // row_chunk, proj_out, 0, unroll=True)

    for s in range(nseg):
        y_ref[s * seg:(s + 1) * seg, :] = yp_scr[pl.ds(s, seg, stride=nseg), :]


def s5_scan_long(z, col0, b_blk, c_blk, pw_re, pw_im, d_skip, h0):
    bsz, seq, _ = z.shape
    n_slab = b_blk.shape[0]
    ns2 = 2 * SLAB_STATE
    row_chunk = min(seq, 1024)
    assert seq % row_chunk == 0 and seq % (2 * SUBLANES * SUBLANES) == 0
    kern = functools.partial(_s5_seg_kernel, seq=seq, row_chunk=row_chunk)
    return pl.pallas_call(
        kern,
        grid=(n_slab, bsz),
        in_specs=[
            pl.BlockSpec((None, seq, LANES), lambda k, b: (b, 0, col0 + k)),
            pl.BlockSpec((None, LANES, ns2), lambda k, b: (k, 0, 0)),
            pl.BlockSpec((None, ns2, LANES), lambda k, b: (k, 0, 0)),
            pl.BlockSpec((None, SUBLANES, SLAB_STATE), lambda k, b: (k, 0, 0)),
            pl.BlockSpec((None, SUBLANES, SLAB_STATE), lambda k, b: (k, 0, 0)),
            pl.BlockSpec((1, LANES), lambda k, b: (0, k)),
            pl.BlockSpec((None, None, 1, ns2), lambda k, b: (b, k, 0, 0)),
        ],
        out_specs=[
            pl.BlockSpec((None, seq, LANES), lambda k, b: (b, 0, k)),
            pl.BlockSpec((None, None, 1, ns2), lambda k, b: (b, k, 0, 0)),
        ],
        out_shape=[
            jax.ShapeDtypeStruct((bsz, seq, n_slab * LANES), F32),
            jax.ShapeDtypeStruct((bsz, n_slab, 1, ns2), F32),
        ],
        scratch_shapes=[
            pltpu.VMEM((seq, LANES), F32),
            pltpu.VMEM((seq, ns2), F32),
            pltpu.VMEM((seq, LANES), F32),
            pltpu.VMEM((seq // SUBLANES, ns2), F32),
            pltpu.VMEM((LANES, ns2), BF16),
            pltpu.VMEM((ns2, LANES), BF16),
        ],
        compiler_params=_params(("arbitrary", "arbitrary"), 40),
        name="s5_scan_long",
    )(z, b_blk, c_blk, pw_re, pw_im, d_skip.reshape(1, -1), h0)


def _glu_kernel(yk_ref, yj_ref, w_ref, b_ref, o_ref, yb_scr):
    first = pl.program_id(1) == 0

    def tile():
        gate = _dot(yb_scr[...], w_ref[...]) + b_ref[...]
        o_ref[...] = (yj_ref[...] * _sigmoid(gate)).astype(o_ref.dtype)

    @pl.when(first)
    def _():
        yb_scr[...] = yk_ref[...].astype(BF16)
        tile()

    @pl.when(jnp.logical_not(first))
    def _():
        tile()


def glu(y, w, b, layer, *, tm, tn):
    t, d = y.shape
    tm = min(tm, t)
    return pl.pallas_call(
        _glu_kernel,
        grid=(t // tm, d // tn),
        in_specs=[
            pl.BlockSpec((tm, d), lambda i, j: (i, 0)),
            pl.BlockSpec((tm, tn), lambda i, j: (i, j)),
            pl.BlockSpec((None, d, tn), lambda i, j: (layer, 0, j)),
            pl.BlockSpec((1, tn), lambda i, j: (0, j)),
        ],
        out_specs=pl.BlockSpec((tm, tn), lambda i, j: (i, j)),
        out_shape=jax.ShapeDtypeStruct((t, d), BF16),
        scratch_shapes=[pltpu.VMEM((tm, d), BF16)],
        compiler_params=_params(("arbitrary", "arbitrary"), 40),
        name="glu",
    )(y, y, w, b.reshape(1, d))


def _pool_kernel(u_ref, buf_ref, w_ref, s_ref, y_ref, tail_ref, ext_scr, *, tc, start_pos, cg):
    c = pl.program_id(1)

    @pl.when(c == 0)
    def _():
        ext_scr[0:POOL_HIST, :] = buf_ref[...]

    ext_scr[POOL_HIST:POOL_HIST + tc, :] = u_ref[...].astype(F32)
    pos = start_pos + c * tc + lax.broadcasted_iota(jnp.int32, (tc, 1), 0)
    for g, win in enumerate(POOL_WINDOWS):
        cols = slice(g * cg, (g + 1) * cg)
        x = ext_scr[:, cols]
        acc, dist = x, 1
        while dist < win:
            acc = acc + pltpu.roll(acc, dist, 0)
            dist *= 2
        wsum = acc[POOL_HIST:, :]
        cnt = jnp.minimum(pos + 1, win).astype(F32)
        zg = wsum * (1.0 / cnt) - x[POOL_HIST:, :]
        y = _dot(zg.astype(BF16), w_ref[g]) * s_ref[:, cols]
        y_ref[:, cols] = y.astype(y_ref.dtype)

    tail = ext_scr[tc:tc + POOL_HIST, :]
    ext_scr[0:POOL_HIST, :] = tail

    @pl.when(c == pl.num_programs(1) - 1)
    def _():
        tail_ref[...] = tail


def pool(z, colblk, buf16, w, layer, scale, *, start_pos, tc):
    bsz, seq, _ = z.shape
    _, n_g, cg, _ = w.shape
    db = n_g * cg
    tc = min(tc, seq)
    kern = functools.partial(_pool_kernel, tc=tc, start_pos=start_pos, cg=cg)
    return pl.pallas_call(
        kern,
        grid=(bsz, seq // tc),
        in_specs=[
            pl.BlockSpec((None, tc, db), lambda b, c: (b, c, colblk)),
            pl.BlockSpec((None, POOL_HIST, db), lambda b, c: (b, 0, 0)),
            pl.BlockSpec((None, n_g, cg, cg), lambda b, c: (layer, 0, 0, 0)),
            pl.BlockSpec((1, db), lambda b, c: (0, 0)),
        ],
        out_specs=[
            pl.BlockSpec((None, tc, db), lambda b, c: (b, c, 0)),
            pl.BlockSpec((None, POOL_HIST, db), lambda b, c: (b, 0, 0)),
        ],
        out_shape=[
            jax.ShapeDtypeStruct((bsz, seq, db), BF16),
            jax.ShapeDtypeStruct((bsz, POOL_HIST, db), F32),
        ],
        scratch_shapes=[pltpu.VMEM((POOL_HIST + tc, db), F32)],
        compiler_params=_params(("arbitrary", "arbitrary"), 40),
        name="pool",
    )(z, buf16, w, scale.reshape(1, db))


def _head_rms(x, g):
    ms = jnp.mean(x * x, axis=-1, keepdims=True)
    return (x * lax.rsqrt(ms + RMS_EPS)) * g


def _combine(os_, lses):
    m = jnp.maximum(jnp.maximum(lses[0], lses[1]), lses[2])
    ws = [jnp.exp(l - m) for l in lses]
    tot = ws[0] + ws[1] + ws[2]
    return (ws[0] * os_[0] + ws[1] * os_[1] + ws[2] * os_[2]) / tot


def _attn_prompt_kernel(q_ref, k_ref, v_ref, qn_ref, kn_ref, att_ref, ko_ref, vo_ref,
                        qs_scr, qf_scr, kf_scr, vf_scr, qd_scr, kd_scr, vd_scr, s_scr, p_scr, m_scr, o_scr, l_scr,
                        *, seq, scale):
    blk = ATT_BLOCK
    rows = 256
    n_all = seq // blk

    def prep(r, c):
        sl = pl.ds(pl.multiple_of(r * rows, rows), rows)
        qs_scr[sl, :] = _head_rms(q_ref[sl, :].astype(F32), qn_ref[...]) * scale
        ko_ref[sl, :] = _head_rms(k_ref[sl, :].astype(F32), kn_ref[...])
        vo_ref[sl, :] = v_ref[sl, :].astype(F32)
        return c

    lax.fori_loop(0, seq // rows, prep, 0, unroll=True)
    kd_scr[0:blk, :] = jnp.zeros((blk, LANES), BF16)
    vd_scr[0:blk, :] = jnp.zeros((blk, 2 * LANES), BF16)
    vd_scr[blk:, LANES:] = jnp.ones((seq, LANES), BF16)

    qi = lax.broadcasted_iota(jnp.int32, (blk, blk), 0)
    kj = lax.broadcasted_iota(jnp.int32, (blk, blk), 1)
    cur_ok = kj <= qi
    prev_ok = kj >= qi
    band_ok = jnp.concatenate([prev_ok, cur_ok], axis=1)
    in_cur = lax.broadcasted_iota(jnp.int32, (blk, 2 * blk), 1) >= blk

    for g, (window, dil) in enumerate(BRANCHES):
        n_blk = seq // (dil * blk)
        col0 = 0 if n_blk > 1 else LANES

        def place(idx, dil=dil, n_blk=n_blk):
            res = idx // n_blk
            n = idx - res * n_blk
            start = res + n * (dil * blk)
            nat = pl.ds(start, blk, stride=dil) if dil > 1 else pl.ds(pl.multiple_of(start, blk), blk)
            cur = pl.ds(pl.multiple_of(idx * blk, blk), blk)
            kcur = pl.ds(pl.multiple_of((idx + 1) * blk, blk), blk)
            kwin = pl.ds(pl.multiple_of(idx * blk, blk), 2 * blk)
            return nat, cur, kcur, kwin, n

        keep_f32 = dil == BRANCHES[1][1]
        two_level = g == 2 and dil == BRANCHES[1][1] ** 2

        def gather(idx, c, place=place, dil=dil, n_blk=n_blk, keep_f32=keep_f32, two_level=two_level):
            nat, cur, kcur, _, n = place(idx)
            if two_level:
                mid = BRANCHES[1][1]
                res = idx // n_blk
                start = (res % mid) * (seq // mid) + res // mid + n * (mid * blk)
                src = pl.ds(start, blk, stride=mid)
                q, k, v = qf_scr[src, :], kf_scr[src, :], vf_scr[src, :]
            else:
                q, k, v = qs_scr[nat, :], ko_ref[nat, :], vo_ref[nat, :]
            if keep_f32:
                qf_scr[cur, :], kf_scr[cur, :], vf_scr[cur, :] = q, k, v
            qd_scr[cur, :] = q.astype(BF16)
            kd_scr[kcur, :] = k.astype(BF16)
            vd_scr[kcur, 0:LANES] = v.astype(BF16)
            return c

        lax.fori_loop(0, n_all, gather, 0, unroll=True)

        def scores(idx, c, place=place, n_blk=n_blk):
            _, cur, kcur, kwin, n = place(idx)
            q = qd_scr[cur, :]
            if n_blk > 1:
                ok = jnp.logical_and(band_ok, jnp.logical_or(in_cur, n > 0))
                s_scr[idx] = jnp.where(ok, _dot_nt(q, kd_scr[kwin, :]), NEG_INF)
            else:
                s_scr[idx, :, LANES:] = jnp.where(cur_ok, _dot_nt(q, kd_scr[kcur, :]), NEG_INF)
            return c

        lax.fori_loop(0, n_all, scores, 0, unroll=True)

        def softmax(idx, c, col0=col0):
            s = s_scr[idx, :, col0:]
            m = jnp.max(s, axis=-1, keepdims=True)
            p_scr[idx, :, col0:] = jnp.exp(s - m).astype(BF16)
            m_scr[idx] = jnp.broadcast_to(m, (blk, LANES))
            return c

        lax.fori_loop(0, n_all, softmax, 0, unroll=True)

        def values(idx, c, g=g, place=place, n_blk=n_blk):
            nat, _, kcur, kwin, _ = place(idx)
            if n_blk > 1:
                ov = _dot(p_scr[idx], vd_scr[kwin, :])
            else:
                ov = _dot(p_scr[idx, :, LANES:], vd_scr[kcur, :])
            l = ov[:, LANES:]
            o_scr[g, nat, :] = ov[:, 0:LANES] / l
            l_scr[g, nat, :] = m_scr[idx] + jnp.log(l)
            return c

        lax.fori_loop(0, n_all, values, 0, unroll=True)

    def comb(r, c):
        sl = pl.ds(pl.multiple_of(r * rows, rows), rows)
        out = _combine([o_scr[g, sl, :] for g in range(3)], [l_scr[g, sl, :] for g in range(3)])
        att_ref[sl, :] = out.astype(att_ref.dtype)
        return c

    lax.fori_loop(0, seq // rows, comb, 0, unroll=True)


def attn_prompt(z, qn, kn, *, n_heads):
    bsz, seq, _ = z.shape
    assert seq % (BRANCHES[-1][1] * ATT_BLOCK) == 0
    hd = LANES
    kern = functools.partial(_attn_prompt_kernel, seq=seq, scale=hd ** -0.5)
    blk = lambda off: pl.BlockSpec((None, seq, hd), lambda b, h: (b, 0, off + h))
    return pl.pallas_call(
        kern,
        grid=(bsz, n_heads),
        in_specs=[blk(0), blk(n_heads), blk(2 * n_heads),
                  pl.BlockSpec((1, hd), lambda b, h: (0, 0)), pl.BlockSpec((1, hd), lambda b, h: (0, 0))],
        out_specs=[blk(0), blk(0), blk(0)],
        out_shape=[
            jax.ShapeDtypeStruct((bsz, seq, n_heads * hd), BF16),
            jax.ShapeDtypeStruct((bsz, seq, n_heads * hd), F32),
            jax.ShapeDtypeStruct((bsz, seq, n_heads * hd), F32),
        ],
        scratch_shapes=[
            pltpu.VMEM((seq, hd), F32),
            pltpu.VMEM((seq, hd), F32), pltpu.VMEM((seq, hd), F32), pltpu.VMEM((seq, hd), F32),
            pltpu.VMEM((seq, hd), BF16), pltpu.VMEM((seq + ATT_BLOCK, hd), BF16),
            pltpu.VMEM((seq + ATT_BLOCK, 2 * hd), BF16),
            pltpu.VMEM((seq // ATT_BLOCK, ATT_BLOCK, 2 * ATT_BLOCK), F32),
            pltpu.VMEM((seq // ATT_BLOCK, ATT_BLOCK, 2 * ATT_BLOCK), BF16),
            pltpu.VMEM((seq // ATT_BLOCK, ATT_BLOCK, hd), F32),
            pltpu.VMEM((3, seq, hd), F32),
            pltpu.VMEM((3, seq, hd), F32),
        ],
        compiler_params=_params(("arbitrary", "arbitrary"), 40),
        name="attn_prompt",
    )(z, z, z, qn.reshape(1, hd), kn.reshape(1, hd))


SAMPLE_PAD = 16


def _attn_sample_kernel(q_ref, k_ref, v_ref, ck_ref, cv_ref, qn_ref, kn_ref, att_ref, ko_ref, vo_ref,
                        q_scr, kn_scr, vn_scr, s_scr, v_scr, o_scr, k4_scr, v4_scr,
                        *, s_new, n_buf, n_heads, pc, scale):
    c = pl.program_id(1)
    n_ch = n_buf // pc
    pad = SAMPLE_PAD

    @pl.when(c == 0)
    def _():
        q_scr[...] = jnp.zeros_like(q_scr)
        kn_scr[...] = jnp.zeros_like(kn_scr)
        vn_scr[...] = jnp.zeros_like(vn_scr)
        for h in range(n_heads):
            lanes = slice(h * LANES, (h + 1) * LANES)
            k_new = _head_rms(k_ref[:, lanes].astype(F32), kn_ref[...])
            v_new = v_ref[:, lanes].astype(F32)
            ko_ref[:, lanes] = k_new
            vo_ref[:, lanes] = v_new
            q_scr[h, 0:s_new, :] = _head_rms(q_ref[:, lanes].astype(F32), qn_ref[...]) * scale
            kn_scr[h, 0:s_new, :] = k_new
            vn_scr[h, 0:s_new, :] = v_new

    mid = math.isqrt(n_heads)
    two_level = mid > 1 and mid * mid == n_heads and mid % 8 != 0
    if two_level:
        part = pc * n_heads // mid
        for a in range(mid):
            k4_scr[a] = ck_ref[pl.ds(a, part, stride=mid), :]
            v4_scr[a] = cv_ref[pl.ds(a, part, stride=mid), :]
    dst = pl.ds(pl.multiple_of(c * pc, pc), pc)
    for h in range(n_heads):
        if two_level:
            rows = pl.ds(h // mid, pc, stride=mid)
            k_h, v_h = k4_scr[h % mid, rows, :], v4_scr[h % mid, rows, :]
        else:
            rows = pl.ds(h, pc, stride=n_heads)
            k_h, v_h = ck_ref[rows, :], cv_ref[rows, :]
        s_scr[h, c] = _dot_nt(q_scr[h].astype(BF16), k_h.astype(BF16))
        v_scr[h, dst, :] = v_h.astype(BF16)

    @pl.when(c == n_ch - 1)
    def _():
        qi = lax.broadcasted_iota(jnp.int32, (pad, pc), 0)
        kj = lax.broadcasted_iota(jnp.int32, (pad, pc), 1)
        qi_n = lax.broadcasted_iota(jnp.int32, (pad, pad), 0)
        kj_n = lax.broadcasted_iota(jnp.int32, (pad, pad), 1)
        dist_n = qi_n - kj_n
        new_ok = jnp.logical_and(dist_n >= 0, kj_n < s_new)

        def finish(h, carry):
            q = q_scr[h].astype(BF16)
            s_n = _dot_nt(q, kn_scr[h].astype(BF16))
            s_c = [s_scr[h, cc] for cc in range(n_ch)]
            ps, pns, ls, ms = [], [], [], []
            for window, dil in BRANCHES:
                msk = []
                for cc in range(n_ch):
                    dist = n_buf + qi - (cc * pc + kj)
                    ok = jnp.logical_and((dist & (dil - 1)) == 0, dist <= window)
                    msk.append(jnp.where(ok, s_c[cc], NEG_INF))
                m_n = jnp.where(jnp.logical_and(new_ok, (dist_n & (dil - 1)) == 0), s_n, NEG_INF)
                m = jnp.max(m_n, axis=-1, keepdims=True)
                for cc in range(n_ch):
                    m = jnp.maximum(m, jnp.max(msk[cc], axis=-1, keepdims=True))
                p_n = jnp.exp(m_n - m)
                l = jnp.sum(p_n, axis=-1, keepdims=True)
                pb = []
                for cc in range(n_ch):
                    p = jnp.exp(msk[cc] - m)
                    l = l + jnp.sum(p, axis=-1, keepdims=True)
                    pb.append(p.astype(BF16))
                ps.append(pb)
                pns.append(p_n.astype(BF16))
                ls.append(l)
                ms.append(m)
            ov = _dot(jnp.concatenate(pns, axis=0), vn_scr[h].astype(BF16))
            for cc in range(n_ch):
                ov = ov + _dot(jnp.concatenate([ps[g][cc] for g in range(3)], axis=0),
                               v_scr[h, cc * pc:(cc + 1) * pc, :])
            outs = [ov[g * pad:(g + 1) * pad] / ls[g] for g in range(3)]
            lses = [jnp.broadcast_to(ms[g] + jnp.log(ls[g]), (pad, LANES)) for g in range(3)]
            o_scr[h] = _combine(outs, lses)
            return carry

        lax.fori_loop(0, n_heads, finish, 0, unroll=4)
        for h in range(n_heads):
            att_ref[:, h * LANES:(h + 1) * LANES] = o_scr[h, 0:s_new, :].astype(att_ref.dtype)


def attn_sample(z, cache_k, cache_v, row0, qn, kn, *, n_heads):
    bsz, s_new, _ = z.shape
    hd = LANES
    d_c = n_heads * hd
    n_buf = cache_k.shape[1] // n_heads
    assert n_buf >= BRANCHES[-1][0] and s_new <= SAMPLE_PAD
    pc = min(n_buf, 512)
    mid = math.isqrt(n_heads)
    kern = functools.partial(_attn_sample_kernel, s_new=s_new, n_buf=n_buf, n_heads=n_heads, pc=pc,
                             scale=hd ** -0.5)
    blk = lambda off: pl.BlockSpec((None, s_new, d_c), lambda b, c: (b, 0, off))
    cblk = pl.BlockSpec((None, pc * n_heads, hd), lambda b, c: (row0 + b, c, 0))
    vec = pl.BlockSpec((1, hd), lambda b, c: (0, 0))
    return pl.pallas_call(
        kern,
        grid=(bsz, n_buf // pc),
        in_specs=[blk(0), blk(1), blk(2), cblk, cblk, vec, vec],
        out_specs=[blk(0), blk(0), blk(0)],
        out_shape=[
            jax.ShapeDtypeStruct((bsz, s_new, d_c), BF16),
            jax.ShapeDtypeStruct((bsz, s_new, d_c), F32),
            jax.ShapeDtypeStruct((bsz, s_new, d_c), F32),
        ],
        scratch_shapes=[
            pltpu.VMEM((n_heads, SAMPLE_PAD, hd), F32),
            pltpu.VMEM((n_heads, SAMPLE_PAD, hd), F32),
            pltpu.VMEM((n_heads, SAMPLE_PAD, hd), F32),
            pltpu.VMEM((n_heads, n_buf // pc, SAMPLE_PAD, pc), F32),
            pltpu.VMEM((n_heads, n_buf, hd), BF16),
            pltpu.VMEM((n_heads, SAMPLE_PAD, hd), F32),
            pltpu.VMEM((mid, pc * n_heads // mid, hd), F32),
            pltpu.VMEM((mid, pc * n_heads // mid, hd), F32),
        ],
        compiler_params=_params(("arbitrary", "arbitrary"), 48),
        name="attn_sample",
    )(z, z, z, cache_k, cache_v, qn.reshape(1, hd), kn.reshape(1, hd))


def _sgu_kernel(gu_ref, gv_ref, lg_ref, lb_ref, w_ref, bt_ref, o_ref, vn_ref, vb_scr, *, rows, n_g, cd):
    t = w_ref.shape[1]
    gv = _gelu(gv_ref[...].astype(F32))
    mu = jnp.mean(gv, axis=-1, keepdims=True)
    xc = gv - mu
    var = jnp.mean(xc * xc, axis=-1, keepdims=True)
    vn = (xc * lax.rsqrt(var + LN_EPS)) * lg_ref[...] + lb_ref[...]
    vn_ref[...] = vn
    if rows < t:
        vb_scr[...] = jnp.zeros_like(vb_scr)
    vb_scr[0:rows, :] = vn.astype(BF16)
    ri = lax.broadcasted_iota(jnp.int32, (t, t), 0)
    ci = lax.broadcasted_iota(jnp.int32, (t, t), 1)
    for g in range(n_g):
        cols = slice(g * cd, (g + 1) * cd)
        wg = jnp.where(ri >= ci, w_ref[g], 0.0).astype(BF16)
        mixed = _dot(wg, vb_scr[:, cols])[0:rows, :] + bt_ref[:, g:g + 1]
        o_ref[:, cols] = (_gelu(gu_ref[:, cols].astype(F32)) * mixed).astype(o_ref.dtype)


def sgu(z, colblk_u, ln_g, ln_b, w_s, b_s):
    bsz, seq, _ = z.shape
    n_g = w_s.shape[0]
    dd = ln_g.shape[0]
    cd = dd // n_g
    t = min(seq, CHUNK)
    tp = max(t, LANES)
    w = jnp.pad(w_s[:, :t, :t], ((0, 0), (0, tp - t), (0, tp - t)))
    bt = jnp.transpose(b_s[:, :t])
    kern = functools.partial(_sgu_kernel, rows=t, n_g=n_g, cd=cd)
    return pl.pallas_call(
        kern,
        grid=(bsz, seq // t),
        in_specs=[
            pl.BlockSpec((None, t, dd), lambda b, c: (b, c, colblk_u)),
            pl.BlockSpec((None, t, dd), lambda b, c: (b, c, colblk_u + 1)),
            pl.BlockSpec((1, dd), lambda b, c: (0, 0)),
            pl.BlockSpec((1, dd), lambda b, c: (0, 0)),
            pl.BlockSpec((n_g, tp, tp), lambda b, c: (0, 0, 0)),
            pl.BlockSpec((t, n_g), lambda b, c: (0, 0)),
        ],
        out_specs=[
            pl.BlockSpec((None, t, dd), lambda b, c: (b, c, 0)),
            pl.BlockSpec((None, t, dd), lambda b, c: (b, c, 0)),
        ],
        out_shape=[
            jax.ShapeDtypeStruct((bsz, seq, dd), BF16),
            jax.ShapeDtypeStruct((bsz, seq, dd), F32),
        ],
        scratch_shapes=[pltpu.VMEM((tp, dd), BF16)],
        compiler_params=_params(("arbitrary", "arbitrary"), 40),
        name="sgu",
    )(z, z, ln_g.reshape(1, dd), ln_b.reshape(1, dd), w, bt)


def _in_proj(x, norm_g, w_in, i, tiles):
    if tiles["cast"]:
        return norm_matmul_cast(x, norm_g, w_in, i, tn=MIX_TILE)
    return norm_matmul(x, norm_g, w_in, tm=tiles["tm"], out_dtype=tiles["z_dtype"]), None


def _res_proj(x, a, b, w_out, i, tiles):
    if tiles["cast"]:
        return out_proj_cast(x, a, b, w_out, i, tn=MIX_TILE)
    return out_proj(x, a, b, w_out, tm=tiles["tm_out"]), None


def _even_layer(x, bsz, seq, h0_re, h0_im, pool_buf, start_pos, norm_g, w_in, w_out, i, s5p, pool_w, pool_scale,
                w_glu, b_glu, d_skip, tiles):
    t, d = x.shape
    pw_re, pw_im, b_blk, c_blk = s5p
    n_slab = b_blk.shape[0]
    d_a = n_slab * LANES
    z, w_in_b = _in_proj(x, norm_g, w_in, i, tiles)
    z = z.reshape(bsz, seq, -1)
    h0 = jnp.concatenate([h0_re.reshape(bsz, n_slab, 1, SLAB_STATE), h0_im.reshape(bsz, n_slab, 1, SLAB_STATE)], axis=-1)
    if tiles["s5_long"]:
        y_pre, h_last = s5_scan_long(z, 0, b_blk, c_blk, pw_re, pw_im, d_skip, h0)
    else:
        y_pre, h_last = s5_scan(z, 0, b_blk, c_blk, pw_re, pw_im, d_skip, h0, split_in=True)
    ya = glu(y_pre.reshape(t, d_a), w_glu, b_glu, i, tm=tiles["tm_glu"], tn=MIX_TILE)
    buf16 = jnp.pad(pool_buf, ((0, 0), (POOL_HIST - pool_buf.shape[1], 0), (0, 0)))
    yb, tail = pool(z, 1, buf16, pool_w, i, pool_scale, start_pos=start_pos, tc=256)
    x, w_out_b = _res_proj(x, ya, yb.reshape(t, -1), w_out, i, tiles)
    g_a = n_slab * SLAB_GROUPS
    h_re = h_last[..., :SLAB_STATE].reshape(bsz, g_a, S5_P)
    h_im = h_last[..., SLAB_STATE:].reshape(bsz, g_a, S5_P)
    return x, h_re, h_im, tail[:, POOL_HIST - pool_buf.shape[1]:], (w_in_b, w_out_b)


def _odd_layer(x, bsz, seq, k_buf, v_buf, norm_g, w_in, w_out, i, qn, kn, ln_g, ln_b, w_s, b_s, n_heads, tiles):
    t, d = x.shape
    d_c = n_heads * LANES
    z, w_in_b = _in_proj(x, norm_g, w_in, i, tiles)
    z = z.reshape(bsz, seq, -1)
    if k_buf is None:
        att, k_new, v_new = attn_prompt(z, qn, kn, n_heads=n_heads)
    else:
        att, k_new, v_new = attn_sample(z, k_buf, v_buf, i * bsz, qn, kn, n_heads=n_heads)
    dd = ln_g.shape[0]
    sg, vn = sgu(z, (3 * d_c) // dd, ln_g, ln_b, w_s, b_s)
    x, w_out_b = _res_proj(x, att.reshape(t, d_c), sg.reshape(t, dd), w_out, i, tiles)
    hd = LANES
    return x, k_new.reshape(bsz, seq, n_heads, hd), v_new.reshape(bsz, seq, n_heads, hd), vn, (w_in_b, w_out_b)


def kernel(x_prompt, x_sample, state_s5_re, state_s5_im, state_pool, cache_k, cache_v, norm_mix, norm_ffn, ev_w_in, ev_w_out, s5_lambda_re, s5_lambda_im, s5_log_dt, s5_b_re, s5_b_im, s5_c_re, s5_c_im, s5_d, s5_w_glu, s5_b_glu, pool_w, pool_scale, od_w_in, od_w_out, q_norm, k_norm, sgu_ln_g, sgu_ln_b, sgu_w, sgu_b, ffn_w1, ffn_w3, ffn_w2):
    bp, lp, d = x_prompt.shape
    bs, ls, _ = x_sample.shape
    depth = norm_mix.shape[0]
    n_heads = cache_k.shape[3]
    xp = x_prompt.reshape(bp * lp, d)
    xs = x_sample.reshape(bs * ls, d)
    tiles_p = dict(cast=False, tm=1024, tm_glu=1024, tm_out=1024, s5_long=True, z_dtype=BF16)
    tiles_s = dict(cast=True, tm_glu=bs * ls, s5_long=False)
    g_a, p_a = s5_lambda_re.shape[1:]
    w_glu_b, pool_w_b = s5_w_glu.astype(BF16), pool_w.astype(BF16)
    d_c = n_heads * LANES
    cache_k2 = cache_k.reshape(-1, cache_k.shape[2] * n_heads, LANES)
    cache_v2 = cache_v.reshape(-1, cache_v.shape[2] * n_heads, LANES)

    s5r_p, s5i_p, pool_p, k_p, v_p = [], [], [], [], []
    s5r_s, s5i_s, pool_s, k_s, v_s, sgu_s = [], [], [], [], [], []
    for l in range(depth):
        i = l // 2
        if l % 2 == 0:
            pw_re, pw_im, bb_re, bb_im = s5_prep(s5_lambda_re[i], s5_lambda_im[i], s5_log_dt[i], s5_b_re[i], s5_b_im[i])
            b_blk, c_blk = s5_block_matrices(bb_re, bb_im, s5_c_re[i], s5_c_im[i])
            s5p = (pw_re, pw_im, b_blk, c_blk)
            rest = (i, s5p, pool_w_b, pool_scale[i], w_glu_b, s5_b_glu[i], s5_d[i])
            xs, hr, hi, buf, (w_in_b, w_out_b) = _even_layer(
                xs, bs, ls, state_s5_re[i], state_s5_im[i], state_pool[i], PAST_LEN, norm_mix[l], ev_w_in, ev_w_out,
                *rest, tiles_s)
            s5r_s.append(hr); s5i_s.append(hi); pool_s.append(buf)
            zero_h = jnp.zeros((bp, g_a, p_a), F32)
            zero_buf = jnp.zeros((bp, state_pool.shape[2], state_pool.shape[3]), F32)
            xp, hr, hi, buf, _ = _even_layer(xp, bp, lp, zero_h, zero_h, zero_buf, 0, norm_mix[l], w_in_b, w_out_b,
                                             *rest, tiles_p)
            s5r_p.append(hr); s5i_p.append(hi); pool_p.append(buf)
        else:
            rest = (i, q_norm[i], k_norm[i], sgu_ln_g[i], sgu_ln_b[i], sgu_w[i], sgu_b[i], n_heads)
            xs, nk, nv, vrows, (w_in_b, w_out_b) = _odd_layer(xs, bs, ls, cache_k2, cache_v2, norm_mix[l], od_w_in,
                                                              od_w_out, *rest, tiles_s)
            k_s.append(nk); v_s.append(nv); sgu_s.append(vrows)
            xp, nk, nv, _, _ = _odd_layer(xp, bp, lp, None, None, norm_mix[l], w_in_b, w_out_b, *rest, tiles_p)
            k_p.append(nk); v_p.append(nv)
        if l == 0:
            xs, *ffn_wb = ffn_cast(xs, norm_ffn[l], ffn_w1, ffn_w3, ffn_w2, l, tf=FFN_TILE)
        else:
            xs = ffn(xs, norm_ffn[l], *ffn_wb, tm=bs * ls)
        if l + 1 < depth:
            xp, ffn_wb = ffn(xp, norm_ffn[l], *ffn_wb, tm=512, nxt=(ffn_w1, ffn_w3, ffn_w2, l + 1))
        else:
            xp = ffn(xp, norm_ffn[l], *ffn_wb, tm=512)
    return (xp.reshape(bp, lp, d), xs.reshape(bs, ls, d),
            jnp.stack(s5r_p), jnp.stack(s5i_p), jnp.stack(pool_p), jnp.stack(k_p), jnp.stack(v_p),
            jnp.stack(s5r_s), jnp.stack(s5i_s), jnp.stack(pool_s), jnp.stack(k_s), jnp.stack(v_s),
            jnp.stack(sgu_s))
```

```python
import functools
import math

import jax
import jax.numpy as jnp
from jax import lax
from jax.experimental import pallas as pl
from jax.experimental.pallas import tpu as pltpu

F32 = jnp.float32
BF16 = jnp.bfloat16

RMS_EPS = 1e-6
LN_EPS = 1e-5
NEG_INF = -1e30

LANES = 128
SUBLANES = 8
ATT_BLOCK = 128
CHUNK = 128
POOL_WINDOWS = (2, 4, 8, 16)
POOL_HIST = 16
BRANCHES = ((128, 1), (512, 4), (2048, 16))
S5_GRP = 16
S5_P = 64
SLAB_GROUPS = LANES // S5_GRP
SLAB_STATE = SLAB_GROUPS * S5_P
PAST_LEN = 8192
FFN_TILE = 256
MIX_TILE = 512


def _params(sem, vmem_mib):
    return pltpu.CompilerParams(dimension_semantics=sem, vmem_limit_bytes=vmem_mib << 20)


def _gelu(x):
    return 0.5 * x * (1.0 + lax.erf(x * (1.0 / math.sqrt(2.0))))


def _sigmoid(x):
    return 1.0 / (1.0 + jnp.exp(-x))


def _split_bf16(a):
    hi = a.astype(BF16)
    lo = (a - hi.astype(F32)).astype(BF16)
    return hi, lo


def _dot(a, b):
    return jnp.dot(a, b, preferred_element_type=F32)


def _dot_nt(a, b):
    return lax.dot_general(a, b, (((1,), (1,)), ((), ())), preferred_element_type=F32)


def _rms_rows_to(x_ref, g_ref, h_ref, rows):
    step = 16 if rows % 16 == 0 else rows

    def body(r, c):
        sl = pl.ds(pl.multiple_of(r * step, step), step)
        x = x_ref[sl, :]
        ms = jnp.mean(x * x, axis=-1, keepdims=True)
        h_ref[sl, :] = ((x * lax.rsqrt(ms + RMS_EPS)) * g_ref[...]).astype(h_ref.dtype)
        return c

    lax.fori_loop(0, rows // step, body, 0, unroll=True)


def _norm_matmul_kernel(x_ref, g_ref, w_ref, o_ref, h_ref):
    first = pl.program_id(1) == 0

    def tile():
        o_ref[...] = _dot(h_ref[...], w_ref[...]).astype(o_ref.dtype)

    @pl.when(first)
    def _():
        _rms_rows_to(x_ref, g_ref, h_ref, x_ref.shape[0])
        tile()

    @pl.when(jnp.logical_not(first))
    def _():
        tile()


def norm_matmul(x, g, w, *, tm, out_dtype=F32):
    t, d = x.shape
    n_t, _, tn = w.shape
    return pl.pallas_call(
        _norm_matmul_kernel,
        grid=(t // tm, n_t),
        in_specs=[
            pl.BlockSpec((tm, d), lambda i, j: (i, 0)),
            pl.BlockSpec((1, d), lambda i, j: (0, 0)),
            pl.BlockSpec((None, d, tn), lambda i, j: (j, 0, 0)),
        ],
        out_specs=pl.BlockSpec((tm, tn), lambda i, j: (i, j)),
        out_shape=jax.ShapeDtypeStruct((t, n_t * tn), out_dtype),
        scratch_shapes=[pltpu.VMEM((tm, d), BF16)],
        compiler_params=_params(("arbitrary", "arbitrary"), 58),
        name="norm_matmul",
    )(x, g.reshape(1, d), w)


def _norm_matmul_cast_kernel(x_ref, g_ref, w_ref, o_ref, wb_ref, h_ref):
    @pl.when(pl.program_id(0) == 0)
    def _():
        _rms_rows_to(x_ref, g_ref, h_ref, x_ref.shape[0])

    wb_ref[...] = w_ref[...].astype(BF16)
    o_ref[...] = _dot(h_ref[...], wb_ref[...]).astype(o_ref.dtype)


def norm_matmul_cast(x, g, w, layer, *, tn):
    t, d = x.shape
    n = w.shape[2]
    return pl.pallas_call(
        _norm_matmul_cast_kernel,
        grid=(n // tn,),
        in_specs=[
            pl.BlockSpec((t, d), lambda j: (0, 0)),
            pl.BlockSpec((1, d), lambda j: (0, 0)),
            pl.BlockSpec((None, d, tn), lambda j: (layer, 0, j)),
        ],
        out_specs=[
            pl.BlockSpec((t, tn), lambda j: (0, j)),
            pl.BlockSpec((None, d, tn), lambda j: (j, 0, 0)),
        ],
        out_shape=[
            jax.ShapeDtypeStruct((t, n), F32),
            jax.ShapeDtypeStruct((n // tn, d, tn), BF16),
        ],
        scratch_shapes=[pltpu.VMEM((t, d), BF16)],
        compiler_params=_params(("arbitrary",), 48),
        name="norm_matmul_cast",
    )(x, g.reshape(1, d), w)


FFN_OUT_CHUNK = 512


def _ffn_step(first, x_ref, g_ref, w1_ref, w3_ref, w2_ref, o_ref, h_ref):
    def tile(acc_ref):
        h = h_ref[...]
        a = _dot(h, w1_ref[...])
        b = _dot(h, w3_ref[...])
        u = ((a * _sigmoid(a)) * b).astype(BF16)
        for c in range(0, o_ref.shape[1], FFN_OUT_CHUNK):
            cols = slice(c, c + FFN_OUT_CHUNK)
            o_ref[:, cols] = acc_ref[:, cols] + _dot(u, w2_ref[:, cols])

    @pl.when(first)
    def _():
        _rms_rows_to(x_ref, g_ref, h_ref, x_ref.shape[0])
        tile(x_ref)

    @pl.when(jnp.logical_not(first))
    def _():
        tile(o_ref)


def _ffn_kernel(x_ref, g_ref, w1_ref, w3_ref, w2_ref, o_ref, h_ref):
    _ffn_step(pl.program_id(1) == 0, x_ref, g_ref, w1_ref, w3_ref, w2_ref, o_ref, h_ref)


def _ffn_next_kernel(x_ref, g_ref, w1_ref, w3_ref, w2_ref, n1_ref, n3_ref, n2_ref,
                     o_ref, c1_ref, c3_ref, c2_ref, h_ref):
    c1_ref[...] = n1_ref[...].astype(BF16)
    c3_ref[...] = n3_ref[...].astype(BF16)
    c2_ref[...] = n2_ref[...].astype(BF16)
    _ffn_step(pl.program_id(1) == 0, x_ref, g_ref, w1_ref, w3_ref, w2_ref, o_ref, h_ref)


def ffn(x, g, w1, w3, w2, *, tm, nxt=None):
    t, d = x.shape
    n_f, _, tf = w1.shape
    n_m = t // tm
    specs = [
        pl.BlockSpec((tm, d), lambda i, j: (i, 0)),
        pl.BlockSpec((1, d), lambda i, j: (0, 0)),
        pl.BlockSpec((None, d, tf), lambda i, j: (j, 0, 0)),
        pl.BlockSpec((None, d, tf), lambda i, j: (j, 0, 0)),
        pl.BlockSpec((tf, d), lambda i, j: (j, 0)),
    ]
    out_spec = pl.BlockSpec((tm, d), lambda i, j: (i, 0))
    out_shape = jax.ShapeDtypeStruct((t, d), F32)
    common = dict(grid=(n_m, n_f), scratch_shapes=[pltpu.VMEM((tm, d), BF16)],
                  compiler_params=_params(("arbitrary", "arbitrary"), 56))
    if nxt is None:
        return pl.pallas_call(_ffn_kernel, in_specs=specs, out_specs=out_spec, out_shape=out_shape, name="ffn",
                              **common)(x, g.reshape(1, d), w1, w3, w2)
    n1, n3, n2, layer = nxt
    piece = d // n_m
    assert d % n_m == 0 and piece % LANES == 0
    col = pl.BlockSpec((None, piece, tf), lambda i, j: (layer, i, j))
    row = pl.BlockSpec((None, tf, piece), lambda i, j: (layer, j, i))
    col_out = pl.BlockSpec((None, piece, tf), lambda i, j: (j, i, 0))
    row_out = pl.BlockSpec((tf, piece), lambda i, j: (j, i))
    out, c1, c3, c2 = pl.pallas_call(
        _ffn_next_kernel,
        in_specs=specs + [col, col, row],
        out_specs=[out_spec, col_out, col_out, row_out],
        out_shape=[out_shape, jax.ShapeDtypeStruct(w1.shape, BF16), jax.ShapeDtypeStruct(w3.shape, BF16),
                   jax.ShapeDtypeStruct(w2.shape, BF16)],
        name="ffn_next", **common,
    )(x, g.reshape(1, d), w1, w3, w2, n1, n3, n2)
    return out, (c1, c3, c2)


def _ffn_cast_kernel(x_ref, g_ref, w1_ref, w3_ref, w2_ref, o_ref, w1b_ref, w3b_ref, w2b_ref, h_ref):
    w1b_ref[...] = w1_ref[...].astype(BF16)
    w3b_ref[...] = w3_ref[...].astype(BF16)
    w2b_ref[...] = w2_ref[...].astype(BF16)
    _ffn_step(pl.program_id(0) == 0, x_ref, g_ref, w1b_ref, w3b_ref, w2b_ref, o_ref, h_ref)


def ffn_cast(x, g, w1, w3, w2, layer, *, tf):
    t, d = x.shape
    f = w1.shape[2]
    return pl.pallas_call(
        _ffn_cast_kernel,
        grid=(f // tf,),
        in_specs=[
            pl.BlockSpec((t, d), lambda j: (0, 0)),
            pl.BlockSpec((1, d), lambda j: (0, 0)),
            pl.BlockSpec((None, d, tf), lambda j: (layer, 0, j)),
            pl.BlockSpec((None, d, tf), lambda j: (layer, 0, j)),
            pl.BlockSpec((None, tf, d), lambda j: (layer, j, 0)),
        ],
        out_specs=[
            pl.BlockSpec((t, d), lambda j: (0, 0)),
            pl.BlockSpec((None, d, tf), lambda j: (j, 0, 0)),
            pl.BlockSpec((None, d, tf), lambda j: (j, 0, 0)),
            pl.BlockSpec((tf, d), lambda j: (j, 0)),
        ],
        out_shape=[
            jax.ShapeDtypeStruct((t, d), F32),
            jax.ShapeDtypeStruct((f // tf, d, tf), BF16),
            jax.ShapeDtypeStruct((f // tf, d, tf), BF16),
            jax.ShapeDtypeStruct((f, d), BF16),
        ],
        scratch_shapes=[pltpu.VMEM((t, d), BF16)],
        compiler_params=_params(("arbitrary",), 48),
        name="ffn_cast",
    )(x, g.reshape(1, d), w1, w3, w2)


def _out_proj_kernel(x_ref, a_ref, b_ref, wa_ref, wb_ref, o_ref):
    o_ref[...] = x_ref[...] + _dot(a_ref[...], wa_ref[...]) + _dot(b_ref[...], wb_ref[...])


def out_proj(x, a, b, w_pair, *, tm):
    t, d = x.shape
    wa, wb = w_pair
    n_t, k, tn = wa.shape
    assert a.shape[1] == k and b.shape[1] == k and wb.shape == wa.shape
    return pl.pallas_call(
        _out_proj_kernel,
        grid=(t // tm, n_t),
        in_specs=[
            pl.BlockSpec((tm, tn), lambda i, j: (i, j)),
            pl.BlockSpec((tm, k), lambda i, j: (i, 0)),
            pl.BlockSpec((tm, k), lambda i, j: (i, 0)),
            pl.BlockSpec((None, k, tn), lambda i, j: (j, 0, 0)),
            pl.BlockSpec((None, k, tn), lambda i, j: (j, 0, 0)),
        ],
        out_specs=pl.BlockSpec((tm, tn), lambda i, j: (i, j)),
        out_shape=jax.ShapeDtypeStruct((t, d), F32),
        compiler_params=_params(("arbitrary", "arbitrary"), 48),
        name="out_proj",
    )(x, a, b, wa, wb)


def _out_proj_cast_kernel(x_ref, a_ref, b_ref, wa_ref, wb_ref, o_ref, wab_ref, wbb_ref):
    wab_ref[...] = wa_ref[...].astype(BF16)
    wbb_ref[...] = wb_ref[...].astype(BF16)
    o_ref[...] = x_ref[...] + _dot(a_ref[...], wab_ref[...]) + _dot(b_ref[...], wbb_ref[...])


def out_proj_cast(x, a, b, w, layer, *, tn):
    t, d = x.shape
    k = a.shape[1]
    assert b.shape[1] == k and w.shape[1] == 2 * k
    out, wa_b, wb_b = pl.pallas_call(
        _out_proj_cast_kernel,
        grid=(d // tn,),
        in_specs=[
            pl.BlockSpec((t, tn), lambda j: (0, j)),
            pl.BlockSpec((t, k), lambda j: (0, 0)),
            pl.BlockSpec((t, k), lambda j: (0, 0)),
            pl.BlockSpec((None, k, tn), lambda j: (layer, 0, j)),
            pl.BlockSpec((None, k, tn), lambda j: (layer, 1, j)),
        ],
        out_specs=[
            pl.BlockSpec((t, tn), lambda j: (0, j)),
            pl.BlockSpec((None, k, tn), lambda j: (j, 0, 0)),
            pl.BlockSpec((None, k, tn), lambda j: (j, 0, 0)),
        ],
        out_shape=[
            jax.ShapeDtypeStruct((t, d), F32),
            jax.ShapeDtypeStruct((d // tn, k, tn), BF16),
            jax.ShapeDtypeStruct((d // tn, k, tn), BF16),
        ],
        compiler_params=_params(("arbitrary",), 48),
        name="out_proj_cast",
    )(x, a, b, w, w)
    return out, (wa_b, wb_b)


def _s5_prep_kernel(lr_ref, li_ref, ldt_ref, lrx_ref, lix_ref, ldtx_ref, br_ref, bi_ref,
                    pwr_ref, pwi_ref, bbr_ref, bbi_ref):
    dt = jnp.exp(ldt_ref[...])
    mag = jnp.exp(lr_ref[...] * dt)
    ang = li_ref[...] * dt
    p_r, p_i = mag * jnp.cos(ang), mag * jnp.sin(ang)
    c_r, c_i = p_r, p_i
    pwr_ref[0], pwi_ref[0] = c_r, c_i
    for j in range(1, SUBLANES):
        c_r, c_i = c_r * p_r - c_i * p_i, c_r * p_i + c_i * p_r
        pwr_ref[j], pwi_ref[j] = c_r, c_i
    lr, li = lrx_ref[...], lix_ref[...]
    dtx = jnp.exp(ldtx_ref[...])
    magx = jnp.exp(lr * dtx)
    angx = li * dtx
    nr, ni = magx * jnp.cos(angx) - 1.0, magx * jnp.sin(angx)
    den = lr * lr + li * li
    qr = (nr * lr + ni * li) / den
    qi = (ni * lr - nr * li) / den
    br, bi = br_ref[...], bi_ref[...]
    bbr_ref[...] = qr * br - qi * bi
    bbi_ref[...] = qr * bi + qi * br


def s5_prep(lam_re, lam_im, log_dt, b_re, b_im):
    g, p = lam_re.shape
    h = b_re.shape[2]
    n_slab = g // SLAB_GROUPS
    slab = lambda a: a.reshape(n_slab, SLAB_GROUPS * p)
    rep = lambda a: jnp.repeat(a, h, axis=1)
    ldt_gp = jnp.broadcast_to(log_dt[:, None], (g, p))
    outs = pl.pallas_call(
        _s5_prep_kernel,
        out_shape=[
            jax.ShapeDtypeStruct((SUBLANES, n_slab, SLAB_GROUPS * p), F32),
            jax.ShapeDtypeStruct((SUBLANES, n_slab, SLAB_GROUPS * p), F32),
            jax.ShapeDtypeStruct((g, p * h), F32),
            jax.ShapeDtypeStruct((g, p * h), F32),
        ],
        name="s5_prep",
    )(slab(lam_re), slab(lam_im), slab(ldt_gp), rep(lam_re), rep(lam_im), rep(ldt_gp),
      b_re.reshape(g, p * h), b_im.reshape(g, p * h))
    pw_re, pw_im, bb_re, bb_im = outs
    pw_re = jnp.transpose(pw_re, (1, 0, 2))
    pw_im = jnp.transpose(pw_im, (1, 0, 2))
    return pw_re, pw_im, bb_re.reshape(g, p, h), bb_im.reshape(g, p, h)


def s5_block_matrices(bb_re, bb_im, c_re, c_im):
    g, p, h = bb_re.shape
    n_slab = g // SLAB_GROUPS
    eye = jnp.eye(SLAB_GROUPS, dtype=F32)

    def in_map(bb):
        t = bb.reshape(n_slab, SLAB_GROUPS, p, h)
        return jnp.einsum("kgph,gj->kghjp", t, eye).reshape(n_slab, SLAB_GROUPS * h, SLAB_GROUPS * p)

    def out_map(c):
        t = c.reshape(n_slab, SLAB_GROUPS, h, p)
        return jnp.einsum("kghp,gj->kgpjh", t, eye).reshape(n_slab, SLAB_GROUPS * p, SLAB_GROUPS * h)

    b_blk = jnp.concatenate([in_map(bb_re), in_map(bb_im)], axis=2)
    c_blk = jnp.concatenate([out_map(c_re), -out_map(c_im)], axis=1)
    return b_blk, c_blk


def _s5_scan_kernel(u_ref, bblk_ref, cblk_ref, pwr_ref, pwi_ref, d_ref, h0_ref,
                    y_ref, hl_ref, h_scr, bh_scr, bl_scr, ch_scr, *, seq, row_chunk, split_in):
    ns = SLAB_STATE

    @pl.when(pl.program_id(1) == 0)
    def _():
        bh, bl = _split_bf16(bblk_ref[...])
        bh_scr[...], bl_scr[...] = bh, bl
        ch_scr[...] = cblk_ref[...].astype(BF16)

    n_chunks = seq // row_chunk

    def proj_in(r, c):
        sl = pl.ds(pl.multiple_of(r * row_chunk, row_chunk), row_chunk)
        if split_in:
            uh, ul = _split_bf16(u_ref[sl, :].astype(F32))
            h_scr[sl, :] = _dot(uh, bh_scr[...]) + _dot(ul, bh_scr[...]) + _dot(uh, bl_scr[...])
        else:
            h_scr[sl, :] = _dot(u_ref[sl, :].astype(BF16), bh_scr[...])
        return c

    lax.fori_loop(0, n_chunks, proj_in, 0)

    rowid = lax.broadcasted_iota(jnp.int32, (SUBLANES, LANES), 0)
    for c in range(ns // LANES):
        re_l = slice(c * LANES, (c + 1) * LANES)
        im_l = slice(ns + c * LANES, ns + (c + 1) * LANES)
        p_r, p_i = pwr_ref[:, re_l], pwi_ref[:, re_l]
        steps = []
        for dist in (1, 2, 4):
            a_r = jnp.where(rowid >= dist, jnp.broadcast_to(p_r[dist - 1:dist], (SUBLANES, LANES)), 0.0)
            a_i = jnp.where(rowid >= dist, jnp.broadcast_to(p_i[dist - 1:dist], (SUBLANES, LANES)), 0.0)
            steps.append((dist, a_r, a_i))
        c_r = jnp.broadcast_to(h0_ref[:, re_l], (SUBLANES, LANES))
        c_i = jnp.broadcast_to(h0_ref[:, im_l], (SUBLANES, LANES))

        last = slice(SUBLANES - 1, SUBLANES)
        full = (SUBLANES, LANES)
        p8_r, p8_i = jnp.broadcast_to(p_r[last], full), jnp.broadcast_to(p_i[last], full)
        n_groups = seq // SUBLANES
        per_it = min(4, n_groups)

        def scan_rows(it, carry, re_l=re_l, im_l=im_l, p_r=p_r, p_i=p_i, p8_r=p8_r, p8_i=p8_i, steps=steps):
            base = pl.multiple_of(it * (per_it * SUBLANES), per_it * SUBLANES)
            sls = [pl.ds(base + j * SUBLANES, SUBLANES) for j in range(per_it)]
            loc = []
            for sl in sls:
                r, i = h_scr[sl, re_l], h_scr[sl, im_l]
                for dist, a_r, a_i in steps:
                    s_r, s_i = pltpu.roll(r, dist, 0), pltpu.roll(i, dist, 0)
                    r, i = r + (s_r * a_r - s_i * a_i), i + (s_r * a_i + s_i * a_r)
                loc.append((r, i))
            c_r, c_i = carry
            outs = []
            for r, i in loc:
                outs.append((r + (c_r * p_r - c_i * p_i), i + (c_r * p_i + c_i * p_r)))
                e_r, e_i = jnp.broadcast_to(r[last], full), jnp.broadcast_to(i[last], full)
                c_r, c_i = e_r + (c_r * p8_r - c_i * p8_i), e_i + (c_r * p8_i + c_i * p8_r)
            for sl, (r, i) in zip(sls, outs):
                h_scr[sl, re_l], h_scr[sl, im_l] = r, i
            return c_r, c_i

        c_r, c_i = lax.fori_loop(0, n_groups // per_it, scan_rows, (c_r, c_i))
        hl_ref[:, re_l] = c_r[0:1]
        hl_ref[:, im_l] = c_i[0:1]

    def proj_out(r, c):
        sl = pl.ds(pl.multiple_of(r * row_chunk, row_chunk), row_chunk)
        y = _dot(h_scr[sl, :].astype(BF16), ch_scr[...]) + d_ref[...] * u_ref[sl, :].astype(F32)
        y_ref[sl, :] = _gelu(y)
        return c

    lax.fori_loop(0, n_chunks, proj_out, 0)


def s5_scan(z, col0, b_blk, c_blk, pw_re, pw_im, d_skip, h0, *, split_in):
    bsz, seq, _ = z.shape
    n_slab = b_blk.shape[0]
    ns2 = 2 * SLAB_STATE
    row_chunk = min(seq, 1024)
    kern = functools.partial(_s5_scan_kernel, seq=seq, row_chunk=row_chunk, split_in=split_in)
    return pl.pallas_call(
        kern,
        grid=(n_slab, bsz),
        in_specs=[
            pl.BlockSpec((None, seq, LANES), lambda k, b: (b, 0, col0 + k)),
            pl.BlockSpec((None, LANES, ns2), lambda k, b: (k, 0, 0)),
            pl.BlockSpec((None, ns2, LANES), lambda k, b: (k, 0, 0)),
            pl.BlockSpec((None, SUBLANES, SLAB_STATE), lambda k, b: (k, 0, 0)),
            pl.BlockSpec((None, SUBLANES, SLAB_STATE), lambda k, b: (k, 0, 0)),
            pl.BlockSpec((1, LANES), lambda k, b: (0, k)),
            pl.BlockSpec((None, None, 1, ns2), lambda k, b: (b, k, 0, 0)),
        ],
        out_specs=[
            pl.BlockSpec((None, seq, LANES), lambda k, b: (b, 0, k)),
            pl.BlockSpec((None, None, 1, ns2), lambda k, b: (b, k, 0, 0)),
        ],
        out_shape=[
            jax.ShapeDtypeStruct((bsz, seq, n_slab * LANES), F32),
            jax.ShapeDtypeStruct((bsz, n_slab, 1, ns2), F32),
        ],
        scratch_shapes=[
            pltpu.VMEM((seq, ns2), F32),
            pltpu.VMEM((LANES, ns2), BF16), pltpu.VMEM((LANES, ns2), BF16),
            pltpu.VMEM((ns2, LANES), BF16),
        ],
        compiler_params=_params(("arbitrary", "arbitrary"), 40),
        name="s5_scan",
    )(z, b_blk, c_blk, pw_re, pw_im, d_skip.reshape(1, -1), h0)


def _s5_seg_kernel(u_ref, bblk_ref, cblk_ref, pwr_ref, pwi_ref, d_ref, h0_ref,
                   y_ref, hl_ref, up_scr, h_scr, yp_scr, w_scr, bh_scr, ch_scr, *, seq, row_chunk):
    ns = SLAB_STATE
    nseg = SUBLANES
    seg = seq // nseg
    n_lb = ns // LANES
    full = (SUBLANES, LANES)
    lanes = [(slice(c * LANES, (c + 1) * LANES), slice(ns + c * LANES, ns + (c + 1) * LANES)) for c in range(n_lb)]
    last = slice(SUBLANES - 1, SUBLANES)

    @pl.when(pl.program_id(1) == 0)
    def _():
        bh_scr[...] = bblk_ref[...].astype(BF16)
        ch_scr[...] = cblk_ref[...].astype(BF16)
        for re_l, im_l in lanes:
            p_r, p_i = pwr_ref[:, re_l], pwi_ref[:, re_l]
            p8_r, p8_i = jnp.broadcast_to(p_r[last], full), jnp.broadcast_to(p_i[last], full)
            w_scr[0:SUBLANES, re_l], w_scr[0:SUBLANES, im_l] = p_r, p_i

            def grow(gi, carry, re_l=re_l, im_l=im_l, p8_r=p8_r, p8_i=p8_i):
                w_r, w_i = carry
                w_r, w_i = w_r * p8_r - w_i * p8_i, w_r * p8_i + w_i * p8_r
                sl = pl.ds(pl.multiple_of(gi * SUBLANES, SUBLANES), SUBLANES)
                w_scr[sl, re_l], w_scr[sl, im_l] = w_r, w_i
                return w_r, w_i

            lax.fori_loop(1, seg // SUBLANES, grow, (p_r, p_i))

    for s in range(nseg):
        up_scr[pl.ds(s, seg, stride=nseg), :] = u_ref[s * seg:(s + 1) * seg, :].astype(F32)

    def proj_in(r, c):
        sl = pl.ds(pl.multiple_of(r * row_chunk, row_chunk), row_chunk)
        h_scr[sl, :] = _dot(up_scr[sl, :].astype(BF16), bh_scr[...])
        return c

    lax.fori_loop(0, seq // row_chunk, proj_in, 0, unroll=True)

    lam = [(jnp.broadcast_to(pwr_ref[0:1, re_l], full), jnp.broadcast_to(pwi_ref[0:1, re_l], full))
           for re_l, _ in lanes]
    per_it = 2

    def scan_t(it, carry):
        base = pl.multiple_of(it * (per_it * SUBLANES), per_it * SUBLANES)
        sls = [pl.ds(base + j * SUBLANES, SUBLANES) for j in range(per_it)]
        bu = [[(h_scr[sl, re_l], h_scr[sl, im_l]) for re_l, im_l in lanes] for sl in sls]
        hs, outs = list(carry), []
        for j in range(per_it):
            hs = [(bu[j][c][0] + (hs[c][0] * lam[c][0] - hs[c][1] * lam[c][1]),
                   bu[j][c][1] + (hs[c][0] * lam[c][1] + hs[c][1] * lam[c][0])) for c in range(n_lb)]
            outs.append(hs)
        for sl, row in zip(sls, outs):
            for (re_l, im_l), (h_r, h_i) in zip(lanes, row):
                h_scr[sl, re_l], h_scr[sl, im_l] = h_r, h_i
        return tuple(hs)

    zero = jnp.zeros(full, F32)
    ends = lax.fori_loop(0, seg // per_it, scan_t, tuple((zero, zero) for _ in range(n_lb)), unroll=True)

    rowid = lax.broadcasted_iota(jnp.int32, full, 0)
    enter = []
    for (re_l, im_l), (e_r, e_i) in zip(lanes, ends):
        ws_r, ws_i = w_scr[seg - 1:seg, re_l], w_scr[seg - 1:seg, im_l]
        c_r, c_i = h0_ref[:, re_l], h0_ref[:, im_l]
        cv_r, cv_i = jnp.broadcast_to(c_r, full), jnp.broadcast_to(c_i, full)
        for s in range(1, nseg + 1):
            c_r, c_i = (e_r[s - 1:s] + (c_r * ws_r - c_i * ws_i), e_i[s - 1:s] + (c_r * ws_i + c_i * ws_r))
            if s < nseg:
                cv_r = jnp.where(rowid == s, jnp.broadcast_to(c_r, full), cv_r)
                cv_i = jnp.where(rowid == s, jnp.broadcast_to(c_i, full), cv_i)
        hl_ref[:, re_l], hl_ref[:, im_l] = c_r, c_i
        enter.append((cv_r, cv_i))

    def fix_t(gi, c):
        wsl = pl.ds(pl.multiple_of(gi * SUBLANES, SUBLANES), SUBLANES)
        wv = [(w_scr[wsl, re_l], w_scr[wsl, im_l]) for re_l, im_l in lanes]
        base = pl.multiple_of(gi * (SUBLANES * SUBLANES), SUBLANES * SUBLANES)
        for j in range(SUBLANES):
            sl = pl.ds(base + j * SUBLANES, SUBLANES)
            for (re_l, im_l), (cv_r, cv_i), (wv_r, wv_i) in zip(lanes, enter, wv):
                w_r = jnp.broadcast_to(wv_r[j:j + 1], full)
                w_i = jnp.broadcast_to(wv_i[j:j + 1], full)
                h_r = h_scr[sl, re_l] + (w_r * cv_r - w_i * cv_i)
                h_i = h_scr[sl, im_l] + (w_r * cv_i + w_i * cv_r)
                h_scr[sl, re_l], h_scr[sl, im_l] = h_r, h_i
        return c

    lax.fori_loop(0, seg // SUBLANES, fix_t, 0, unroll=True)

    def proj_out(r, c):
        sl = pl.ds(pl.multiple_of(r * row_chunk, row_chunk), row_chunk)
        y = _dot(h_scr[sl, :].astype(BF16), ch_scr[...]) + d_ref[...] * up_scr[sl, :]
        yp_scr[sl, :] = _gelu(y)
        return c

    lax.fori_loop(0, seq // row_chunk, proj_out, 0, unroll=True)

    for s in range(nseg):
        y_ref[s * seg:(s + 1) * seg, :] = yp_scr[pl.ds(s, seg, stride=nseg), :]


def s5_scan_long(z, col0, b_blk, c_blk, pw_re, pw_im, d_skip, h0):
    bsz, seq, _ = z.shape
    n_slab = b_blk.shape[0]
    ns2 = 2 * SLAB_STATE
    row_chunk = min(seq, 1024)
    assert seq % row_chunk == 0 and seq % (2 * SUBLANES * SUBLANES) == 0
    kern = functools.partial(_s5_seg_kernel, seq=seq, row_chunk=row_chunk)
    return pl.pallas_call(
        kern,
        grid=(n_slab, bsz),
        in_specs=[
            pl.BlockSpec((None, seq, LANES), lambda k, b: (b, 0, col0 + k)),
            pl.BlockSpec((None, LANES, ns2), lambda k, b: (k, 0, 0)),
            pl.BlockSpec((None, ns2, LANES), lambda k, b: (k, 0, 0)),
            pl.BlockSpec((None, SUBLANES, SLAB_STATE), lambda k, b: (k, 0, 0)),
            pl.BlockSpec((None, SUBLANES, SLAB_STATE), lambda k, b: (k, 0, 0)),
            pl.BlockSpec((1, LANES), lambda k, b: (0, k)),
            pl.BlockSpec((None, None, 1, ns2), lambda k, b: (b, k, 0, 0)),
        ],
        out_specs=[
            pl.BlockSpec((None, seq, LANES), lambda k, b: (b, 0, k)),
            pl.BlockSpec((None, None, 1, ns2), lambda k, b: (b, k, 0, 0)),
        ],
        out_shape=[
            jax.ShapeDtypeStruct((bsz, seq, n_slab * LANES), F32),
            jax.ShapeDtypeStruct((bsz, n_slab, 1, ns2), F32),
        ],
        scratch_shapes=[
            pltpu.VMEM((seq, LANES), F32),
            pltpu.VMEM((seq, ns2), F32),
            pltpu.VMEM((seq, LANES), F32),
            pltpu.VMEM((seq // SUBLANES, ns2), F32),
            pltpu.VMEM((LANES, ns2), BF16),
            pltpu.VMEM((ns2, LANES), BF16),
        ],
        compiler_params=_params(("arbitrary", "arbitrary"), 40),
        name="s5_scan_long",
    )(z, b_blk, c_blk, pw_re, pw_im, d_skip.reshape(1, -1), h0)


def _glu_kernel(yk_ref, yj_ref, w_ref, b_ref, o_ref, yb_scr):
    first = pl.program_id(1) == 0

    def tile():
        gate = _dot(yb_scr[...], w_ref[...]) + b_ref[...]
        o_ref[...] = (yj_ref[...] * _sigmoid(gate)).astype(o_ref.dtype)

    @pl.when(first)
    def _():
        yb_scr[...] = yk_ref[...].astype(BF16)
        tile()

    @pl.when(jnp.logical_not(first))
    def _():
        tile()


def glu(y, w, b, layer, *, tm, tn):
    t, d = y.shape
    tm = min(tm, t)
    return pl.pallas_call(
        _glu_kernel,
        grid=(t // tm, d // tn),
        in_specs=[
            pl.BlockSpec((tm, d), lambda i, j: (i, 0)),
            pl.BlockSpec((tm, tn), lambda i, j: (i, j)),
            pl.BlockSpec((None, d, tn), lambda i, j: (layer, 0, j)),
            pl.BlockSpec((1, tn), lambda i, j: (0, j)),
        ],
        out_specs=pl.BlockSpec((tm, tn), lambda i, j: (i, j)),
        out_shape=jax.ShapeDtypeStruct((t, d), BF16),
        scratch_shapes=[pltpu.VMEM((tm, d), BF16)],
        compiler_params=_params(("arbitrary", "arbitrary"), 40),
        name="glu",
    )(y, y, w, b.reshape(1, d))


def _pool_kernel(u_ref, buf_ref, w_ref, s_ref, y_ref, tail_ref, ext_scr, *, tc, start_pos, cg):
    c = pl.program_id(1)

    @pl.when(c == 0)
    def _():
        ext_scr[0:POOL_HIST, :] = buf_ref[...]

    ext_scr[POOL_HIST:POOL_HIST + tc, :] = u_ref[...].astype(F32)
    pos = start_pos + c * tc + lax.broadcasted_iota(jnp.int32, (tc, 1), 0)
    for g, win in enumerate(POOL_WINDOWS):
        cols = slice(g * cg, (g + 1) * cg)
        x = ext_scr[:, cols]
        acc, dist = x, 1
        while dist < win:
            acc = acc + pltpu.roll(acc, dist, 0)
            dist *= 2
        wsum = acc[POOL_HIST:, :]
        cnt = jnp.minimum(pos + 1, win).astype(F32)
        zg = wsum * (1.0 / cnt) - x[POOL_HIST:, :]
        y = _dot(zg.astype(BF16), w_ref[g]) * s_ref[:, cols]
        y_ref[:, cols] = y.astype(y_ref.dtype)

    tail = ext_scr[tc:tc + POOL_HIST, :]
    ext_scr[0:POOL_HIST, :] = tail

    @pl.when(c == pl.num_programs(1) - 1)
    def _():
        tail_ref[...] = tail


def pool(z, colblk, buf16, w, layer, scale, *, start_pos, tc):
    bsz, seq, _ = z.shape
    _, n_g, cg, _ = w.shape
    db = n_g * cg
    tc = min(tc, seq)
    kern = functools.partial(_pool_kernel, tc=tc, start_pos=start_pos, cg=cg)
    return pl.pallas_call(
        kern,
        grid=(bsz, seq // tc),
        in_specs=[
            pl.BlockSpec((None, tc, db), lambda b, c: (b, c, colblk)),
            pl.BlockSpec((None, POOL_HIST, db), lambda b, c: (b, 0, 0)),
            pl.BlockSpec((None, n_g, cg, cg), lambda b, c: (layer, 0, 0, 0)),
            pl.BlockSpec((1, db), lambda b, c: (0, 0)),
        ],
        out_specs=[
            pl.BlockSpec((None, tc, db), lambda b, c: (b, c, 0)),
            pl.BlockSpec((None, POOL_HIST, db), lambda b, c: (b, 0, 0)),
        ],
        out_shape=[
            jax.ShapeDtypeStruct((bsz, seq, db), BF16),
            jax.ShapeDtypeStruct((bsz, POOL_HIST, db), F32),
        ],
        scratch_shapes=[pltpu.VMEM((POOL_HIST + tc, db), F32)],
        compiler_params=_params(("arbitrary", "arbitrary"), 40),
        name="pool",
    )(z, buf16, w, scale.reshape(1, db))


def _head_rms(x, g):
    ms = jnp.mean(x * x, axis=-1, keepdims=True)
    return (x * lax.rsqrt(ms + RMS_EPS)) * g


def _combine(os_, lses):
    m = jnp.maximum(jnp.maximum(lses[0], lses[1]), lses[2])
    ws = [jnp.exp(l - m) for l in lses]
    tot = ws[0] + ws[1] + ws[2]
    return (ws[0] * os_[0] + ws[1] * os_[1] + ws[2] * os_[2]) / tot


def _attn_prompt_kernel(q_ref, k_ref, v_ref, qn_ref, kn_ref, att_ref, ko_ref, vo_ref,
                        qs_scr, qf_scr, kf_scr, vf_scr, qd_scr, kd_scr, vd_scr, s_scr, p_scr, m_scr, o_scr, l_scr,
                        *, seq, scale):
    blk = ATT_BLOCK
    rows = 256
    n_all = seq // blk

    def prep(r, c):
        sl = pl.ds(pl.multiple_of(r * rows, rows), rows)
        qs_scr[sl, :] = _head_rms(q_ref[sl, :].astype(F32), qn_ref[...]) * scale
        ko_ref[sl, :] = _head_rms(k_ref[sl, :].astype(F32), kn_ref[...])
        vo_ref[sl, :] = v_ref[sl, :].astype(F32)
        return c

    lax.fori_loop(0, seq // rows, prep, 0, unroll=True)
    kd_scr[0:blk, :] = jnp.zeros((blk, LANES), BF16)
    vd_scr[0:blk, :] = jnp.zeros((blk, 2 * LANES), BF16)
    vd_scr[blk:, LANES:] = jnp.ones((seq, LANES), BF16)

    qi = lax.broadcasted_iota(jnp.int32, (blk, blk), 0)
    kj = lax.broadcasted_iota(jnp.int32, (blk, blk), 1)
    cur_ok = kj <= qi
    prev_ok = kj >= qi
    band_ok = jnp.concatenate([prev_ok, cur_ok], axis=1)
    in_cur = lax.broadcasted_iota(jnp.int32, (blk, 2 * blk), 1) >= blk

    for g, (window, dil) in enumerate(BRANCHES):
        n_blk = seq // (dil * blk)
        col0 = 0 if n_blk > 1 else LANES

        def place(idx, dil=dil, n_blk=n_blk):
            res = idx // n_blk
            n = idx - res * n_blk
            start = res + n * (dil * blk)
            nat = pl.ds(start, blk, stride=dil) if dil > 1 else pl.ds(pl.multiple_of(start, blk), blk)
            cur = pl.ds(pl.multiple_of(idx * blk, blk), blk)
            kcur = pl.ds(pl.multiple_of((idx + 1) * blk, blk), blk)
            kwin = pl.ds(pl.multiple_of(idx * blk, blk), 2 * blk)
            return nat, cur, kcur, kwin, n

        keep_f32 = dil == BRANCHES[1][1]
        two_level = g == 2 and dil == BRANCHES[1][1] ** 2

        def gather(idx, c, place=place, dil=dil, n_blk=n_blk, keep_f32=keep_f32, two_level=two_level):
            nat, cur, kcur, _, n = place(idx)
            if two_level:
                mid = BRANCHES[1][1]
                res = idx // n_blk
                start = (res % mid) * (seq // mid) + res // mid + n * (mid * blk)
                src = pl.ds(start, blk, stride=mid)
                q, k, v = qf_scr[src, :], kf_scr[src, :], vf_scr[src, :]
            else:
                q, k, v = qs_scr[nat, :], ko_ref[nat, :], vo_ref[nat, :]
            if keep_f32:
                qf_scr[cur, :], kf_scr[cur, :], vf_scr[cur, :] = q, k, v
            qd_scr[cur, :] = q.astype(BF16)
            kd_scr[kcur, :] = k.astype(BF16)
            vd_scr[kcur, 0:LANES] = v.astype(BF16)
            return c

        lax.fori_loop(0, n_all, gather, 0, unroll=True)

        def scores(idx, c, place=place, n_blk=n_blk):
            _, cur, kcur, kwin, n = place(idx)
            q = qd_scr[cur, :]
            if n_blk > 1:
                ok = jnp.logical_and(band_ok, jnp.logical_or(in_cur, n > 0))
                s_scr[idx] = jnp.where(ok, _dot_nt(q, kd_scr[kwin, :]), NEG_INF)
            else:
                s_scr[idx, :, LANES:] = jnp.where(cur_ok, _dot_nt(q, kd_scr[kcur, :]), NEG_INF)
            return c

        lax.fori_loop(0, n_all, scores, 0, unroll=True)

        def softmax(idx, c, col0=col0):
            s = s_scr[idx, :, col0:]
            m = jnp.max(s, axis=-1, keepdims=True)
            p_scr[idx, :, col0:] = jnp.exp(s - m).astype(BF16)
            m_scr[idx] = jnp.broadcast_to(m, (blk, LANES))
            return c

        lax.fori_loop(0, n_all, softmax, 0, unroll=True)

        def values(idx, c, g=g, place=place, n_blk=n_blk):
            nat, _, kcur, kwin, _ = place(idx)
            if n_blk > 1:
                ov = _dot(p_scr[idx], vd_scr[kwin, :])
            else:
                ov = _dot(p_scr[idx, :, LANES:], vd_scr[kcur, :])
            l = ov[:, LANES:]
            o_scr[g, nat, :] = ov[:, 0:LANES] / l
            l_scr[g, nat, :] = m_scr[idx] + jnp.log(l)
            return c

        lax.fori_loop(0, n_all, values, 0, unroll=True)

    def comb(r, c):
        sl = pl.ds(pl.multiple_of(r * rows, rows), rows)
        out = _combine([o_scr[g, sl, :] for g in range(3)], [l_scr[g, sl, :] for g in range(3)])
        att_ref[sl, :] = out.astype(att_ref.dtype)
        return c

    lax.fori_loop(0, seq // rows, comb, 0, unroll=True)


def attn_prompt(z, qn, kn, *, n_heads):
    bsz, seq, _ = z.shape
    assert seq % (BRANCHES[-1][1] * ATT_BLOCK) == 0
    hd = LANES
    kern = functools.partial(_attn_prompt_kernel, seq=seq, scale=hd ** -0.5)
    blk = lambda off: pl.BlockSpec((None, seq, hd), lambda b, h: (b, 0, off + h))
    return pl.pallas_call(
        kern,
        grid=(bsz, n_heads),
        in_specs=[blk(0), blk(n_heads), blk(2 * n_heads),
                  pl.BlockSpec((1, hd), lambda b, h: (0, 0)), pl.BlockSpec((1, hd), lambda b, h: (0, 0))],
        out_specs=[blk(0), blk(0), blk(0)],
        out_shape=[
            jax.ShapeDtypeStruct((bsz, seq, n_heads * hd), BF16),
            jax.ShapeDtypeStruct((bsz, seq, n_heads * hd), F32),
            jax.ShapeDtypeStruct((bsz, seq, n_heads * hd), F32),
        ],
        scratch_shapes=[
            pltpu.VMEM((seq, hd), F32),
            pltpu.VMEM((seq, hd), F32), pltpu.VMEM((seq, hd), F32), pltpu.VMEM((seq, hd), F32),
            pltpu.VMEM((seq, hd), BF16), pltpu.VMEM((seq + ATT_BLOCK, hd), BF16),
            pltpu.VMEM((seq + ATT_BLOCK, 2 * hd), BF16),
            pltpu.VMEM((seq // ATT_BLOCK, ATT_BLOCK, 2 * ATT_BLOCK), F32),
            pltpu.VMEM((seq // ATT_BLOCK, ATT_BLOCK, 2 * ATT_BLOCK), BF16),
            pltpu.VMEM((seq // ATT_BLOCK, ATT_BLOCK, hd), F32),
            pltpu.VMEM((3, seq, hd), F32),
            pltpu.VMEM((3, seq, hd), F32),
        ],
        compiler_params=_params(("arbitrary", "arbitrary"), 40),
        name="attn_prompt",
    )(z, z, z, qn.reshape(1, hd), kn.reshape(1, hd))


SAMPLE_PAD = 16


def _attn_sample_kernel(q_ref, k_ref, v_ref, ck_ref, cv_ref, qn_ref, kn_ref, att_ref, ko_ref, vo_ref,
                        q_scr, kn_scr, vn_scr, s_scr, v_scr, o_scr, k4_scr, v4_scr,
                        *, s_new, n_buf, n_heads, pc, scale):
    c = pl.program_id(1)
    n_ch = n_buf // pc
    pad = SAMPLE_PAD

    @pl.when(c == 0)
    def _():
        q_scr[...] = jnp.zeros_like(q_scr)
        kn_scr[...] = jnp.zeros_like(kn_scr)
        vn_scr[...] = jnp.zeros_like(vn_scr)
        for h in range(n_heads):
            lanes = slice(h * LANES, (h + 1) * LANES)
            k_new = _head_rms(k_ref[:, lanes].astype(F32), kn_ref[...])
            v_new = v_ref[:, lanes].astype(F32)
            ko_ref[:, lanes] = k_new
            vo_ref[:, lanes] = v_new
            q_scr[h, 0:s_new, :] = _head_rms(q_ref[:, lanes].astype(F32), qn_ref[...]) * scale
            kn_scr[h, 0:s_new, :] = k_new
            vn_scr[h, 0:s_new, :] = v_new

    mid = math.isqrt(n_heads)
    two_level = mid > 1 and mid * mid == n_heads and mid % 8 != 0
    if two_level:
        part = pc * n_heads // mid
        for a in range(mid):
            k4_scr[a] = ck_ref[pl.ds(a, part, stride=mid), :]
            v4_scr[a] = cv_ref[pl.ds(a, part, stride=mid), :]
    dst = pl.ds(pl.multiple_of(c * pc, pc), pc)
    for h in range(n_heads):
        if two_level:
            rows = pl.ds(h // mid, pc, stride=mid)
            k_h, v_h = k4_scr[h % mid, rows, :], v4_scr[h % mid, rows, :]
        else:
            rows = pl.ds(h, pc, stride=n_heads)
            k_h, v_h = ck_ref[rows, :], cv_ref[rows, :]
        s_scr[h, c] = _dot_nt(q_scr[h].astype(BF16), k_h.astype(BF16))
        v_scr[h, dst, :] = v_h.astype(BF16)

    @pl.when(c == n_ch - 1)
    def _():
        qi = lax.broadcasted_iota(jnp.int32, (pad, pc), 0)
        kj = lax.broadcasted_iota(jnp.int32, (pad, pc), 1)
        qi_n = lax.broadcasted_iota(jnp.int32, (pad, pad), 0)
        kj_n = lax.broadcasted_iota(jnp.int32, (pad, pad), 1)
        dist_n = qi_n - kj_n
        new_ok = jnp.logical_and(dist_n >= 0, kj_n < s_new)

        def finish(h, carry):
            q = q_scr[h].astype(BF16)
            s_n = _dot_nt(q, kn_scr[h].astype(BF16))
            s_c = [s_scr[h, cc] for cc in range(n_ch)]
            ps, pns, ls, ms = [], [], [], []
            for window, dil in BRANCHES:
                msk = []
                for cc in range(n_ch):
                    dist = n_buf + qi - (cc * pc + kj)
                    ok = jnp.logical_and((dist & (dil - 1)) == 0, dist <= window)
                    msk.append(jnp.where(ok, s_c[cc], NEG_INF))
                m_n = jnp.where(jnp.logical_and(new_ok, (dist_n & (dil - 1)) == 0), s_n, NEG_INF)
                m = jnp.max(m_n, axis=-1, keepdims=True)
                for cc in range(n_ch):
                    m = jnp.maximum(m, jnp.max(msk[cc], axis=-1, keepdims=True))
                p_n = jnp.exp(m_n - m)
                l = jnp.sum(p_n, axis=-1, keepdims=True)
                pb = []
                for cc in range(n_ch):
                    p = jnp.exp(msk[cc] - m)
                    l = l + jnp.sum(p, axis=-1, keepdims=True)
                    pb.append(p.astype(BF16))
                ps.append(pb)
                pns.append(p_n.astype(BF16))
                ls.append(l)
                ms.append(m)
            ov = _dot(jnp.concatenate(pns, axis=0), vn_scr[h].astype(BF16))
            for cc in range(n_ch):
                ov = ov + _dot(jnp.concatenate([ps[g][cc] for g in range(3)], axis=0),
                               v_scr[h, cc * pc:(cc + 1) * pc, :])
            outs = [ov[g * pad:(g + 1) * pad] / ls[g] for g in range(3)]
            lses = [jnp.broadcast_to(ms[g] + jnp.log(ls[g]), (pad, LANES)) for g in range(3)]
            o_scr[h] = _combine(outs, lses)
            return carry

        lax.fori_loop(0, n_heads, finish, 0, unroll=4)
        for h in range(n_heads):
            att_ref[:, h * LANES:(h + 1) * LANES] = o_scr[h, 0:s_new, :].astype(att_ref.dtype)


def attn_sample(z, cache_k, cache_v, row0, qn, kn, *, n_heads):
    bsz, s_new, _ = z.shape
    hd = LANES
    d_c = n_heads * hd
    n_buf = cache_k.shape[1] // n_heads
    assert n_buf >= BRANCHES[-1][0] and s_new <= SAMPLE_PAD
    pc = min(n_buf, 512)
    mid = math.isqrt(n_heads)
    kern = functools.partial(_attn_sample_kernel, s_new=s_new, n_buf=n_buf, n_heads=n_heads, pc=pc,
                             scale=hd ** -0.5)
    blk = lambda off: pl.BlockSpec((None, s_new, d_c), lambda b, c: (b, 0, off))
    cblk = pl.BlockSpec((None, pc * n_heads, hd), lambda b, c: (row0 + b, c, 0))
    vec = pl.BlockSpec((1, hd), lambda b, c: (0, 0))
    return pl.pallas_call(
        kern,
        grid=(bsz, n_buf // pc),
        in_specs=[blk(0), blk(1), blk(2), cblk, cblk, vec, vec],
        out_specs=[blk(0), blk(0), blk(0)],
        out_shape=[
            jax.ShapeDtypeStruct((bsz, s_new, d_c), BF16),
            jax.ShapeDtypeStruct((bsz, s_new, d_c), F32),
            jax.ShapeDtypeStruct((bsz, s_new, d_c), F32),
        ],
        scratch_shapes=[
            pltpu.VMEM((n_heads, SAMPLE_PAD, hd), F32),
            pltpu.VMEM((n_heads, SAMPLE_PAD, hd), F32),
            pltpu.VMEM((n_heads, SAMPLE_PAD, hd), F32),
            pltpu.VMEM((n_heads, n_buf // pc, SAMPLE_PAD, pc), F32),
            pltpu.VMEM((n_heads, n_buf, hd), BF16),
            pltpu.VMEM((n_heads, SAMPLE_PAD, hd), F32),
            pltpu.VMEM((mid, pc * n_heads // mid, hd), F32),
            pltpu.VMEM((mid, pc * n_heads // mid, hd), F32),
        ],
        compiler_params=_params(("arbitrary", "arbitrary"), 48),
        name="attn_sample",
    )(z, z, z, cache_k, cache_v, qn.reshape(1, hd), kn.reshape(1, hd))


def _sgu_kernel(gu_ref, gv_ref, lg_ref, lb_ref, w_ref, bt_ref, o_ref, *rest, rows, n_g, cd):
    vn_ref, vb_scr = rest if len(rest) == 2 else (None, rest[0])
    t = w_ref.shape[1]
    gv = _gelu(gv_ref[...].astype(F32))
    mu = jnp.mean(gv, axis=-1, keepdims=True)
    xc = gv - mu
    var = jnp.mean(xc * xc, axis=-1, keepdims=True)
    vn = (xc * lax.rsqrt(var + LN_EPS)) * lg_ref[...] + lb_ref[...]
    if vn_ref is not None:
        vn_ref[...] = vn
    if rows < t:
        vb_scr[...] = jnp.zeros_like(vb_scr)
    vb_scr[0:rows, :] = vn.astype(BF16)
    ri = lax.broadcasted_iota(jnp.int32, (t, t), 0)
    ci = lax.broadcasted_iota(jnp.int32, (t, t), 1)
    for g in range(n_g):
        cols = slice(g * cd, (g + 1) * cd)
        wg = jnp.where(ri >= ci, w_ref[g], 0.0).astype(BF16)
        mixed = _dot(wg, vb_scr[:, cols])[0:rows, :] + bt_ref[:, g:g + 1]
        o_ref[:, cols] = (_gelu(gu_ref[:, cols].astype(F32)) * mixed).astype(o_ref.dtype)


def sgu(z, colblk_u, ln_g, ln_b, w_s, b_s, *, want_vn):
    n_out = 2 if want_vn else 1
    bsz, seq, _ = z.shape
    n_g = w_s.shape[0]
    dd = ln_g.shape[0]
    cd = dd // n_g
    t = min(seq, CHUNK)
    tp = max(t, LANES)
    w = jnp.pad(w_s[:, :t, :t], ((0, 0), (0, tp - t), (0, tp - t)))
    bt = jnp.transpose(b_s[:, :t])
    kern = functools.partial(_sgu_kernel, rows=t, n_g=n_g, cd=cd)
    outs = pl.pallas_call(
        kern,
        grid=(bsz, seq // t),
        in_specs=[
            pl.BlockSpec((None, t, dd), lambda b, c: (b, c, colblk_u)),
            pl.BlockSpec((None, t, dd), lambda b, c: (b, c, colblk_u + 1)),
            pl.BlockSpec((1, dd), lambda b, c: (0, 0)),
            pl.BlockSpec((1, dd), lambda b, c: (0, 0)),
            pl.BlockSpec((n_g, tp, tp), lambda b, c: (0, 0, 0)),
            pl.BlockSpec((t, n_g), lambda b, c: (0, 0)),
        ],
        out_specs=[pl.BlockSpec((None, t, dd), lambda b, c: (b, c, 0))] * n_out,
        out_shape=[jax.ShapeDtypeStruct((bsz, seq, dd), BF16), jax.ShapeDtypeStruct((bsz, seq, dd), F32)][:n_out],
        scratch_shapes=[pltpu.VMEM((tp, dd), BF16)],
        compiler_params=_params(("arbitrary", "arbitrary"), 40),
        name="sgu",
    )(z, z, ln_g.reshape(1, dd), ln_b.reshape(1, dd), w, bt)
    return (outs[0], outs[1]) if want_vn else (outs[0], None)


def _in_proj(x, norm_g, w_in, i, tiles):
    if tiles["cast"]:
        return norm_matmul_cast(x, norm_g, w_in, i, tn=MIX_TILE)
    return norm_matmul(x, norm_g, w_in, tm=tiles["tm"], out_dtype=tiles["z_dtype"]), None


def _res_proj(x, a, b, w_out, i, tiles):
    if tiles["cast"]:
        return out_proj_cast(x, a, b, w_out, i, tn=MIX_TILE)
    return out_proj(x, a, b, w_out, tm=tiles["tm_out"]), None


def _even_layer(x, bsz, seq, h0_re, h0_im, pool_buf, start_pos, norm_g, w_in, w_out, i, s5p, pool_w, pool_scale,
                w_glu, b_glu, d_skip, tiles):
    t, d = x.shape
    pw_re, pw_im, b_blk, c_blk = s5p
    n_slab = b_blk.shape[0]
    d_a = n_slab * LANES
    z, w_in_b = _in_proj(x, norm_g, w_in, i, tiles)
    z = z.reshape(bsz, seq, -1)
    h0 = jnp.concatenate([h0_re.reshape(bsz, n_slab, 1, SLAB_STATE), h0_im.reshape(bsz, n_slab, 1, SLAB_STATE)], axis=-1)
    if tiles["s5_long"]:
        y_pre, h_last = s5_scan_long(z, 0, b_blk, c_blk, pw_re, pw_im, d_skip, h0)
    else:
        y_pre, h_last = s5_scan(z, 0, b_blk, c_blk, pw_re, pw_im, d_skip, h0, split_in=True)
    ya = glu(y_pre.reshape(t, d_a), w_glu, b_glu, i, tm=tiles["tm_glu"], tn=MIX_TILE)
    buf16 = jnp.pad(pool_buf, ((0, 0), (POOL_HIST - pool_buf.shape[1], 0), (0, 0)))
    yb, tail = pool(z, 1, buf16, pool_w, i, pool_scale, start_pos=start_pos, tc=256)
    x, w_out_b = _res_proj(x, ya, yb.reshape(t, -1), w_out, i, tiles)
    g_a = n_slab * SLAB_GROUPS
    h_re = h_last[..., :SLAB_STATE].reshape(bsz, g_a, S5_P)
    h_im = h_last[..., SLAB_STATE:].reshape(bsz, g_a, S5_P)
    return x, h_re, h_im, tail[:, POOL_HIST - pool_buf.shape[1]:], (w_in_b, w_out_b)


def _odd_layer(x, bsz, seq, k_buf, v_buf, norm_g, w_in, w_out, i, qn, kn, ln_g, ln_b, w_s, b_s, n_heads, tiles):
    t, d = x.shape
    d_c = n_heads * LANES
    z, w_in_b = _in_proj(x, norm_g, w_in, i, tiles)
    z = z.reshape(bsz, seq, -1)
    if k_buf is None:
        att, k_new, v_new = attn_prompt(z, qn, kn, n_heads=n_heads)
    else:
        att, k_new, v_new = attn_sample(z, k_buf, v_buf, i * bsz, qn, kn, n_heads=n_heads)
    dd = ln_g.shape[0]
    sg, vn = sgu(z, (3 * d_c) // dd, ln_g, ln_b, w_s, b_s, want_vn=k_buf is not None)
    x, w_out_b = _res_proj(x, att.reshape(t, d_c), sg.reshape(t, dd), w_out, i, tiles)
    hd = LANES
    return x, k_new.reshape(bsz, seq, n_heads, hd), v_new.reshape(bsz, seq, n_heads, hd), vn, (w_in_b, w_out_b)


def kernel(x_prompt, x_sample, state_s5_re, state_s5_im, state_pool, cache_k, cache_v, norm_mix, norm_ffn, ev_w_in, ev_w_out, s5_lambda_re, s5_lambda_im, s5_log_dt, s5_b_re, s5_b_im, s5_c_re, s5_c_im, s5_d, s5_w_glu, s5_b_glu, pool_w, pool_scale, od_w_in, od_w_out, q_norm, k_norm, sgu_ln_g, sgu_ln_b, sgu_w, sgu_b, ffn_w1, ffn_w3, ffn_w2):
    bp, lp, d = x_prompt.shape
    bs, ls, _ = x_sample.shape
    depth = norm_mix.shape[0]
    n_heads = cache_k.shape[3]
    xp = x_prompt.reshape(bp * lp, d)
    xs = x_sample.reshape(bs * ls, d)
    tiles_p = dict(cast=False, tm=1024, tm_glu=1024, tm_out=1024, s5_long=True, z_dtype=BF16)
    tiles_s = dict(cast=True, tm_glu=bs * ls, s5_long=False)
    g_a, p_a = s5_lambda_re.shape[1:]
    w_glu_b, pool_w_b = s5_w_glu.astype(BF16), pool_w.astype(BF16)
    d_c = n_heads * LANES
    cache_k2 = cache_k.reshape(-1, cache_k.shape[2] * n_heads, LANES)
    cache_v2 = cache_v.reshape(-1, cache_v.shape[2] * n_heads, LANES)

    s5r_p, s5i_p, pool_p, k_p, v_p = [], [], [], [], []
    s5r_s, s5i_s, pool_s, k_s, v_s, sgu_s = [], [], [], [], [], []
    for l in range(depth):
        i = l // 2
        if l % 2 == 0:
            pw_re, pw_im, bb_re, bb_im = s5_prep(s5_lambda_re[i], s5_lambda_im[i], s5_log_dt[i], s5_b_re[i], s5_b_im[i])
            b_blk, c_blk = s5_block_matrices(bb_re, bb_im, s5_c_re[i], s5_c_im[i])
            s5p = (pw_re, pw_im, b_blk, c_blk)
            rest = (i, s5p, pool_w_b, pool_scale[i], w_glu_b, s5_b_glu[i], s5_d[i])
            xs, hr, hi, buf, (w_in_b, w_out_b) = _even_layer(
                xs, bs, ls, state_s5_re[i], state_s5_im[i], state_pool[i], PAST_LEN, norm_mix[l], ev_w_in, ev_w_out,
                *rest, tiles_s)
            s5r_s.append(hr); s5i_s.append(hi); pool_s.append(buf)
            zero_h = jnp.zeros((bp, g_a, p_a), F32)
            zero_buf = jnp.zeros((bp, state_pool.shape[2], state_pool.shape[3]), F32)
            xp, hr, hi, buf, _ = _even_layer(xp, bp, lp, zero_h, zero_h, zero_buf, 0, norm_mix[l], w_in_b, w_out_b,
                                             *rest, tiles_p)
            s5r_p.append(hr); s5i_p.append(hi); pool_p.append(buf)
        else:
            rest = (i, q_norm[i], k_norm[i], sgu_ln_g[i], sgu_ln_b[i], sgu_w[i], sgu_b[i], n_heads)
            xs, nk, nv, vrows, (w_in_b, w_out_b) = _odd_layer(xs, bs, ls, cache_k2, cache_v2, norm_mix[l], od_w_in,
                                                              od_w_out, *rest, tiles_s)
            k_s.append(nk); v_s.append(nv); sgu_s.append(vrows)
            xp, nk, nv, _, _ = _odd_layer(xp, bp, lp, None, None, norm_mix[l], w_in_b, w_out_b, *rest, tiles_p)
            k_p.append(nk); v_p.append(nv)
        if l == 0:
            xs, *ffn_wb = ffn_cast(xs, norm_ffn[l], ffn_w1, ffn_w3, ffn_w2, l, tf=FFN_TILE)
        else:
            xs = ffn(xs, norm_ffn[l], *ffn_wb, tm=bs * ls)
        if l + 1 < depth:
            xp, ffn_wb = ffn(xp, norm_ffn[l], *ffn_wb, tm=512, nxt=(ffn_w1, ffn_w3, ffn_w2, l + 1))
        else:
            xp = ffn(xp, norm_ffn[l], *ffn_wb, tm=512)
    return (xp.reshape(bp, lp, d), xs.reshape(bs, ls, d),
            jnp.stack(s5r_p), jnp.stack(s5i_p), jnp.stack(pool_p), jnp.stack(k_p), jnp.stack(v_p),
            jnp.stack(s5r_s), jnp.stack(s5i_s), jnp.stack(pool_s), jnp.stack(k_s), jnp.stack(v_s),
            jnp.stack(sgu_s))
```
